```python
import jax, jax.numpy as jnp
from jax import lax
import numpy as np

D_MODEL = 1024
BATCH = 32
SEQ = 2048
DEPTH = 2

CONV_CH = 512
CONV_WIDTH = 31
ATTN_HEADS = 8
HEAD_DIM = 64
ATTN_W = ATTN_HEADS * HEAD_DIM
Q_BLOCK = 128
D_FF = 2816
FFN_CONV_WIDTH = 3
LN_EPS = 1e-5
FORGET_BIAS_INIT = 3.0
ALPHA = (2.0 * DEPTH) ** 0.25
BETA = (8.0 * DEPTH) ** -0.25

GLU_OFF = 0
Q_OFF = GLU_OFF + 2 * CONV_CH
K_OFF = Q_OFF + ATTN_W
V_OFF = K_OFF + ATTN_W
F_OFF = V_OFF + ATTN_W
G_OFF = F_OFF + ATTN_HEADS
N_IN = G_OFF + 2 * D_MODEL

kernel_name = 'hybrid_conformer_fox_convffn_deepnorm_adaln'


def layer_norm(x, g=None, b=None):
    xf = x.astype(jnp.float32)
    mu = jnp.mean(xf, axis=-1, keepdims=True)
    xc = xf - mu
    var = jnp.mean(xc * xc, axis=-1, keepdims=True)
    y = (xc * lax.rsqrt(var + LN_EPS)).astype(x.dtype)
    if g is not None:
        y = y * g + b
    return y


def causal_dwconv(x, w, b):
    k_w, ch = w.shape
    y = lax.conv_general_dilated(
        x, w[:, None, :], window_strides=(1,), padding=[(k_w - 1, 0)],
        dimension_numbers=('NWC', 'WIO', 'NWC'), feature_group_count=ch)
    return y + b


def forgetting_attention(q, k, v, log_f):
    seq = q.shape[1]
    scale = HEAD_DIM ** -0.5
    qh = jnp.transpose(q, (0, 2, 1, 3)) * scale
    kh = jnp.transpose(k, (0, 2, 1, 3))
    vh = jnp.transpose(v, (0, 2, 1, 3))
    cum = jnp.transpose(jnp.cumsum(log_f, axis=1), (0, 2, 1))
    neg = jnp.finfo(jnp.float32).min
    outs = []
    for i in range(seq // Q_BLOCK):
        q0 = i * Q_BLOCK
        q1 = q0 + Q_BLOCK
        qb = qh[:, :, q0:q1]
        kb = kh[:, :, :q1]
        vb = vh[:, :, :q1]
        logits = jnp.einsum('bhqd,bhkd->bhqk', qb, kb).astype(jnp.float32)
        logits = logits + cum[:, :, q0:q1, None] - cum[:, :, None, :q1]
        mask = (q0 + jnp.arange(Q_BLOCK))[:, None] >= jnp.arange(q1)[None, :]
        logits = jnp.where(mask[None, None], logits, neg)
        p = jax.nn.softmax(logits, axis=-1).astype(v.dtype)
        outs.append(jnp.einsum('bhqk,bhkd->bhqd', p, vb))
    o = jnp.concatenate(outs, axis=2)
    bsz = q.shape[0]
    return jnp.transpose(o, (0, 2, 1, 3)).reshape(bsz, seq, ATTN_W)


def hybrid_layer(x, c_act, w_ada, b_ada, w_in, b_in, conv_a_w, conv_a_b,
                 ln_conv_g, ln_conv_b, w_conv_proj, w_attn_proj, w_mix_out, b_mix_out,
                 ln1_g, ln1_b, w_ffn_up, ffn_conv_w, ffn_conv_b, w_ffn_down, ln2_g, ln2_b):
    bsz, seq, _ = x.shape
    mod = c_act @ w_ada + b_ada
    shift1, scale1, gate1, shift2, scale2, gate2 = jnp.split(mod[:, None, :], 6, axis=-1)

    u = layer_norm(x) * (1.0 + scale1) + shift1
    z = u @ w_in + b_in

    a = z[..., GLU_OFF:GLU_OFF + CONV_CH] * jax.nn.sigmoid(z[..., GLU_OFF + CONV_CH:Q_OFF])
    a = causal_dwconv(a, conv_a_w, conv_a_b)
    a = jax.nn.silu(layer_norm(a, ln_conv_g, ln_conv_b))
    y_a = a @ w_conv_proj

    q = z[..., Q_OFF:K_OFF].reshape(bsz, seq, ATTN_HEADS, HEAD_DIM)
    k = z[..., K_OFF:V_OFF].reshape(bsz, seq, ATTN_HEADS, HEAD_DIM)
    v = z[..., V_OFF:F_OFF].reshape(bsz, seq, ATTN_HEADS, HEAD_DIM)
    log_f = jax.nn.log_sigmoid(z[..., F_OFF:G_OFF].astype(jnp.float32))
    y_b = forgetting_attention(q, k, v, log_f) @ w_attn_proj

    g_a = jax.nn.sigmoid(z[..., G_OFF:G_OFF + D_MODEL])
    g_b = jax.nn.sigmoid(z[..., G_OFF + D_MODEL:N_IN])
    mix = (g_a * y_a + g_b * y_b) @ w_mix_out + b_mix_out
    x = layer_norm(ALPHA * x + (1.0 + gate1) * mix, ln1_g, ln1_b)

    u = layer_norm(x) * (1.0 + scale2) + shift2
    h = causal_dwconv(u @ w_ffn_up, ffn_conv_w, ffn_conv_b)
    f = jax.nn.gelu(h[..., :D_FF], approximate=False) * h[..., D_FF:]
    ffn = f @ w_ffn_down
    x = layer_norm(ALPHA * x + (1.0 + gate2) * ffn, ln2_g, ln2_b)
    return x


def _fwd_setup_inputs(seed: int = 0) -> dict:
    key = jax.random.key(seed)
    ks = jax.random.split(key, 24)

    def nrm(k, shape, scale):
        return jax.random.normal(k, shape, jnp.float32) * scale

    L, D = DEPTH, D_MODEL
    w_in = nrm(ks[4], (L, D, N_IN), D ** -0.5)
    w_in = w_in.at[:, :, V_OFF:F_OFF].multiply(BETA)
    b_in = nrm(ks[5], (L, N_IN), 0.02).at[:, F_OFF:G_OFF].add(FORGET_BIAS_INIT)
    return {
        'x': nrm(ks[0], (BATCH, SEQ, D), 1.0),
        'c': nrm(ks[1], (BATCH, D), 1.0),
        'w_ada': nrm(ks[2], (L, D, 6 * D), 0.5 * D ** -0.5),
        'b_ada': nrm(ks[3], (L, 6 * D), 0.02),
        'w_in': w_in,
        'b_in': b_in,
        'conv_a_w': nrm(ks[6], (L, CONV_WIDTH, CONV_CH), CONV_WIDTH ** -0.5),
        'conv_a_b': nrm(ks[7], (L, CONV_CH), 0.02),
        'ln_conv_g': 1.0 + nrm(ks[8], (L, CONV_CH), 0.02),
        'ln_conv_b': nrm(ks[9], (L, CONV_CH), 0.02),
        'w_conv_proj': nrm(ks[10], (L, CONV_CH, D), BETA * CONV_CH ** -0.5),
        'w_attn_proj': nrm(ks[11], (L, ATTN_W, D), BETA * ATTN_W ** -0.5),
        'w_mix_out': nrm(ks[12], (L, D, D), BETA * D ** -0.5),
        'b_mix_out': nrm(ks[13], (L, D), 0.02),
        'ln1_g': 1.0 + nrm(ks[14], (L, D), 0.02),
        'ln1_b': nrm(ks[15], (L, D), 0.02),
        'w_ffn_up': nrm(ks[16], (L, D, 2 * D_FF), BETA * D ** -0.5),
        'ffn_conv_w': nrm(ks[17], (L, FFN_CONV_WIDTH, 2 * D_FF), FFN_CONV_WIDTH ** -0.5),
        'ffn_conv_b': nrm(ks[18], (L, 2 * D_FF), 0.02),
        'w_ffn_down': nrm(ks[19], (L, D_FF, D), BETA * D_FF ** -0.5),
        'ln2_g': 1.0 + nrm(ks[20], (L, D), 0.02),
        'ln2_b': nrm(ks[21], (L, D), 0.02),
    }


def _fwd_reference(x, c, w_ada, b_ada, w_in, b_in, conv_a_w, conv_a_b, ln_conv_g, ln_conv_b,
              w_conv_proj, w_attn_proj, w_mix_out, b_mix_out, ln1_g, ln1_b,
              w_ffn_up, ffn_conv_w, ffn_conv_b, w_ffn_down, ln2_g, ln2_b):
    c_act = jax.nn.silu(c)
    for l in range(DEPTH):
        x = hybrid_layer(
            x, c_act, w_ada[l], b_ada[l], w_in[l], b_in[l], conv_a_w[l], conv_a_b[l],
            ln_conv_g[l], ln_conv_b[l], w_conv_proj[l], w_attn_proj[l], w_mix_out[l],
            b_mix_out[l], ln1_g[l], ln1_b[l], w_ffn_up[l], ffn_conv_w[l], ffn_conv_b[l],
            w_ffn_down[l], ln2_g[l], ln2_b[l])
    return x


import jax as _jax
import jax.numpy as _jnp

TWIN_FORMAT = 'train_step'
FWD_PARAMS = ['x', 'c', 'w_ada', 'b_ada', 'w_in', 'b_in', 'conv_a_w', 'conv_a_b', 'ln_conv_g', 'ln_conv_b', 'w_conv_proj', 'w_attn_proj', 'w_mix_out', 'b_mix_out', 'ln1_g', 'ln1_b', 'w_ffn_up', 'ffn_conv_w', 'ffn_conv_b', 'w_ffn_down', 'ln2_g', 'ln2_b']
TWIN_WEIGHTS = ['w_ada', 'b_ada', 'w_in', 'b_in', 'conv_a_w', 'conv_a_b', 'ln_conv_g', 'ln_conv_b', 'w_conv_proj', 'w_attn_proj', 'w_mix_out', 'b_mix_out', 'ln1_g', 'ln1_b', 'w_ffn_up', 'ffn_conv_w', 'ffn_conv_b', 'w_ffn_down', 'ln2_g', 'ln2_b']
TWIN_DIFF_INPUT = 'x'
TWIN_INPUTS = ['x', 'c', 'w_ada', 'b_ada', 'w_in', 'b_in', 'conv_a_w', 'conv_a_b', 'ln_conv_g', 'ln_conv_b', 'w_conv_proj', 'w_attn_proj', 'w_mix_out', 'b_mix_out', 'ln1_g', 'ln1_b', 'w_ffn_up', 'ffn_conv_w', 'ffn_conv_b', 'w_ffn_down', 'ln2_g', 'ln2_b', 'loss_target', 'm_w_ada', 'm_b_ada', 'm_w_in', 'm_b_in', 'm_conv_a_w', 'm_conv_a_b', 'm_ln_conv_g', 'm_ln_conv_b', 'm_w_conv_proj', 'm_w_attn_proj', 'm_w_mix_out', 'm_b_mix_out', 'm_ln1_g', 'm_ln1_b', 'm_w_ffn_up', 'm_ffn_conv_w', 'm_ffn_conv_b', 'm_w_ffn_down', 'm_ln2_g', 'm_ln2_b', 'v_w_ada', 'v_b_ada', 'v_w_in', 'v_b_in', 'v_conv_a_w', 'v_conv_a_b', 'v_ln_conv_g', 'v_ln_conv_b', 'v_w_conv_proj', 'v_w_attn_proj', 'v_w_mix_out', 'v_b_mix_out', 'v_ln1_g', 'v_ln1_b', 'v_w_ffn_up', 'v_ffn_conv_w', 'v_ffn_conv_b', 'v_w_ffn_down', 'v_ln2_g', 'v_ln2_b']
TWIN_OUTPUTS = ['loss', 'grad_x', 'grad_w_ada', 'grad_b_ada', 'grad_w_in', 'grad_b_in', 'grad_conv_a_w', 'grad_conv_a_b', 'grad_ln_conv_g', 'grad_ln_conv_b', 'grad_w_conv_proj', 'grad_w_attn_proj', 'grad_w_mix_out', 'grad_b_mix_out', 'grad_ln1_g', 'grad_ln1_b', 'grad_w_ffn_up', 'grad_ffn_conv_w', 'grad_ffn_conv_b', 'grad_w_ffn_down', 'grad_ln2_g', 'grad_ln2_b', 'delta_w_ada', 'delta_b_ada', 'delta_w_in', 'delta_b_in', 'delta_conv_a_w', 'delta_conv_a_b', 'delta_ln_conv_g', 'delta_ln_conv_b', 'delta_w_conv_proj', 'delta_w_attn_proj', 'delta_w_mix_out', 'delta_b_mix_out', 'delta_ln1_g', 'delta_ln1_b', 'delta_w_ffn_up', 'delta_ffn_conv_w', 'delta_ffn_conv_b', 'delta_w_ffn_down', 'delta_ln2_g', 'delta_ln2_b', 'new_m_w_ada', 'new_m_b_ada', 'new_m_w_in', 'new_m_b_in', 'new_m_conv_a_w', 'new_m_conv_a_b', 'new_m_ln_conv_g', 'new_m_ln_conv_b', 'new_m_w_conv_proj', 'new_m_w_attn_proj', 'new_m_w_mix_out', 'new_m_b_mix_out', 'new_m_ln1_g', 'new_m_ln1_b', 'new_m_w_ffn_up', 'new_m_ffn_conv_w', 'new_m_ffn_conv_b', 'new_m_w_ffn_down', 'new_m_ln2_g', 'new_m_ln2_b', 'new_v_w_ada', 'new_v_b_ada', 'new_v_w_in', 'new_v_b_in', 'new_v_conv_a_w', 'new_v_conv_a_b', 'new_v_ln_conv_g', 'new_v_ln_conv_b', 'new_v_w_conv_proj', 'new_v_w_attn_proj', 'new_v_w_mix_out', 'new_v_b_mix_out', 'new_v_ln1_g', 'new_v_ln1_b', 'new_v_w_ffn_up', 'new_v_ffn_conv_w', 'new_v_ffn_conv_b', 'new_v_w_ffn_down', 'new_v_ln2_g', 'new_v_ln2_b']
TWIN_LEAF_KINDS = {'loss': 'loss', 'grad_x': 'grad_x', 'grad_w_ada': 'grad_w', 'grad_b_ada': 'grad_w', 'grad_w_in': 'grad_w', 'grad_b_in': 'grad_w', 'grad_conv_a_w': 'grad_w', 'grad_conv_a_b': 'grad_w', 'grad_ln_conv_g': 'grad_w', 'grad_ln_conv_b': 'grad_w', 'grad_w_conv_proj': 'grad_w', 'grad_w_attn_proj': 'grad_w', 'grad_w_mix_out': 'grad_w', 'grad_b_mix_out': 'grad_w', 'grad_ln1_g': 'grad_w', 'grad_ln1_b': 'grad_w', 'grad_w_ffn_up': 'grad_w', 'grad_ffn_conv_w': 'grad_w', 'grad_ffn_conv_b': 'grad_w', 'grad_w_ffn_down': 'grad_w', 'grad_ln2_g': 'grad_w', 'grad_ln2_b': 'grad_w', 'delta_w_ada': 'delta_w', 'delta_b_ada': 'delta_w', 'delta_w_in': 'delta_w', 'delta_b_in': 'delta_w', 'delta_conv_a_w': 'delta_w', 'delta_conv_a_b': 'delta_w', 'delta_ln_conv_g': 'delta_w', 'delta_ln_conv_b': 'delta_w', 'delta_w_conv_proj': 'delta_w', 'delta_w_attn_proj': 'delta_w', 'delta_w_mix_out': 'delta_w', 'delta_b_mix_out': 'delta_w', 'delta_ln1_g': 'delta_w', 'delta_ln1_b': 'delta_w', 'delta_w_ffn_up': 'delta_w', 'delta_ffn_conv_w': 'delta_w', 'delta_ffn_conv_b': 'delta_w', 'delta_w_ffn_down': 'delta_w', 'delta_ln2_g': 'delta_w', 'delta_ln2_b': 'delta_w', 'new_m_w_ada': 'new_m', 'new_m_b_ada': 'new_m', 'new_m_w_in': 'new_m', 'new_m_b_in': 'new_m', 'new_m_conv_a_w': 'new_m', 'new_m_conv_a_b': 'new_m', 'new_m_ln_conv_g': 'new_m', 'new_m_ln_conv_b': 'new_m', 'new_m_w_conv_proj': 'new_m', 'new_m_w_attn_proj': 'new_m', 'new_m_w_mix_out': 'new_m', 'new_m_b_mix_out': 'new_m', 'new_m_ln1_g': 'new_m', 'new_m_ln1_b': 'new_m', 'new_m_w_ffn_up': 'new_m', 'new_m_ffn_conv_w': 'new_m', 'new_m_ffn_conv_b': 'new_m', 'new_m_w_ffn_down': 'new_m', 'new_m_ln2_g': 'new_m', 'new_m_ln2_b': 'new_m', 'new_v_w_ada': 'new_v', 'new_v_b_ada': 'new_v', 'new_v_w_in': 'new_v', 'new_v_b_in': 'new_v', 'new_v_conv_a_w': 'new_v', 'new_v_conv_a_b': 'new_v', 'new_v_ln_conv_g': 'new_v', 'new_v_ln_conv_b': 'new_v', 'new_v_w_conv_proj': 'new_v', 'new_v_w_attn_proj': 'new_v', 'new_v_w_mix_out': 'new_v', 'new_v_b_mix_out': 'new_v', 'new_v_ln1_g': 'new_v', 'new_v_ln1_b': 'new_v', 'new_v_w_ffn_up': 'new_v', 'new_v_ffn_conv_w': 'new_v', 'new_v_ffn_conv_b': 'new_v', 'new_v_w_ffn_down': 'new_v', 'new_v_ln2_g': 'new_v', 'new_v_ln2_b': 'new_v'}


def _forward(args):
    return _fwd_reference(*[args[k] for k in FWD_PARAMS])


def _output_shape():
    out = _jax.eval_shape(lambda: _forward(_fwd_setup_inputs(0)))
    return out.shape, out.dtype

N_MICROBATCH = 1
ADAM_LR = 0.001
ADAM_B1 = 0.9
ADAM_B2 = 0.999
ADAM_EPS = 1e-08
ADAM_WD = 0.01
ADAM_STEP = 10
PER_EXAMPLE_BATCH_AXIS = {'x': 0, 'c': 0, 'loss_target': 0}
SHARED_INPUTS = []
_WEIGHT_DTYPES = {'w_ada': _jnp.float32, 'b_ada': _jnp.float32, 'w_in': _jnp.float32, 'b_in': _jnp.float32, 'conv_a_w': _jnp.float32, 'conv_a_b': _jnp.float32, 'ln_conv_g': _jnp.float32, 'ln_conv_b': _jnp.float32, 'w_conv_proj': _jnp.float32, 'w_attn_proj': _jnp.float32, 'w_mix_out': _jnp.float32, 'b_mix_out': _jnp.float32, 'ln1_g': _jnp.float32, 'ln1_b': _jnp.float32, 'w_ffn_up': _jnp.float32, 'ffn_conv_w': _jnp.float32, 'ffn_conv_b': _jnp.float32, 'w_ffn_down': _jnp.float32, 'ln2_g': _jnp.float32, 'ln2_b': _jnp.float32}
MOMENT_SCALE = {'w_ada': 1.332449e-02, 'b_ada': 2.476309e-02, 'w_in': 1.129357e-02, 'b_in': 3.219661e-02, 'conv_a_w': 2.403481e-02, 'conv_a_b': 6.668568e-02, 'ln_conv_g': 3.809548e-02, 'ln_conv_b': 4.831499e-02, 'w_conv_proj': 3.751304e-02, 'w_attn_proj': 1.403548e-02, 'w_mix_out': 3.936684e-02, 'b_mix_out': 5.384683e-01, 'ln1_g': 1.966247e+00, 'ln1_b': 7.634213e-01, 'w_ffn_up': 2.075994e-02, 'ffn_conv_w': 1.044131e-02, 'ffn_conv_b': 2.114487e-02, 'w_ffn_down': 3.395021e-02, 'ln2_g': 4.534456e+01, 'ln2_b': 1.981982e+00}


def _to_microbatches(a, axis):
    t = _jnp.moveaxis(a, axis, 0)
    t = t.reshape((N_MICROBATCH, t.shape[0] // N_MICROBATCH) + t.shape[1:])
    return _jnp.moveaxis(t, 1, axis + 1)


def setup_inputs(seed: int = 0) -> dict:
    inp = _fwd_setup_inputs(seed)
    key = _jax.random.fold_in(_jax.random.key(seed), 7919)
    shape, _ = _output_shape()
    out = dict(inp)
    out["loss_target"] = _jax.random.normal(_jax.random.fold_in(key, 0), shape, _jnp.float32)
    for i, name in enumerate(TWIN_WEIGHTS):
        w = inp[name].astype(_jnp.float32)
        if MOMENT_SCALE is None:
            s = _jnp.sqrt(_jnp.mean(_jnp.square(w)) + 1e-30)
        else:
            s = MOMENT_SCALE[name]
        km, kv = _jax.random.split(_jax.random.fold_in(key, i + 1))
        out[name] = w
        out["m_" + name] = s * _jax.random.normal(km, w.shape, _jnp.float32)
        out["v_" + name] = (s * s) * _jax.random.uniform(kv, w.shape, _jnp.float32, 0.5, 1.5)
    if N_MICROBATCH > 1:
        for name, axis in PER_EXAMPLE_BATCH_AXIS.items():
            out[name] = _to_microbatches(out[name], axis)
    return {'x': out['x'], 'c': out['c'], 'w_ada': out['w_ada'], 'b_ada': out['b_ada'], 'w_in': out['w_in'], 'b_in': out['b_in'], 'conv_a_w': out['conv_a_w'], 'conv_a_b': out['conv_a_b'], 'ln_conv_g': out['ln_conv_g'], 'ln_conv_b': out['ln_conv_b'], 'w_conv_proj': out['w_conv_proj'], 'w_attn_proj': out['w_attn_proj'], 'w_mix_out': out['w_mix_out'], 'b_mix_out': out['b_mix_out'], 'ln1_g': out['ln1_g'], 'ln1_b': out['ln1_b'], 'w_ffn_up': out['w_ffn_up'], 'ffn_conv_w': out['ffn_conv_w'], 'ffn_conv_b': out['ffn_conv_b'], 'w_ffn_down': out['w_ffn_down'], 'ln2_g': out['ln2_g'], 'ln2_b': out['ln2_b'], 'loss_target': out['loss_target'], 'm_w_ada': out['m_w_ada'], 'm_b_ada': out['m_b_ada'], 'm_w_in': out['m_w_in'], 'm_b_in': out['m_b_in'], 'm_conv_a_w': out['m_conv_a_w'], 'm_conv_a_b': out['m_conv_a_b'], 'm_ln_conv_g': out['m_ln_conv_g'], 'm_ln_conv_b': out['m_ln_conv_b'], 'm_w_conv_proj': out['m_w_conv_proj'], 'm_w_attn_proj': out['m_w_attn_proj'], 'm_w_mix_out': out['m_w_mix_out'], 'm_b_mix_out': out['m_b_mix_out'], 'm_ln1_g': out['m_ln1_g'], 'm_ln1_b': out['m_ln1_b'], 'm_w_ffn_up': out['m_w_ffn_up'], 'm_ffn_conv_w': out['m_ffn_conv_w'], 'm_ffn_conv_b': out['m_ffn_conv_b'], 'm_w_ffn_down': out['m_w_ffn_down'], 'm_ln2_g': out['m_ln2_g'], 'm_ln2_b': out['m_ln2_b'], 'v_w_ada': out['v_w_ada'], 'v_b_ada': out['v_b_ada'], 'v_w_in': out['v_w_in'], 'v_b_in': out['v_b_in'], 'v_conv_a_w': out['v_conv_a_w'], 'v_conv_a_b': out['v_conv_a_b'], 'v_ln_conv_g': out['v_ln_conv_g'], 'v_ln_conv_b': out['v_ln_conv_b'], 'v_w_conv_proj': out['v_w_conv_proj'], 'v_w_attn_proj': out['v_w_attn_proj'], 'v_w_mix_out': out['v_w_mix_out'], 'v_b_mix_out': out['v_b_mix_out'], 'v_ln1_g': out['v_ln1_g'], 'v_ln1_b': out['v_ln1_b'], 'v_w_ffn_up': out['v_w_ffn_up'], 'v_ffn_conv_w': out['v_ffn_conv_w'], 'v_ffn_conv_b': out['v_ffn_conv_b'], 'v_w_ffn_down': out['v_w_ffn_down'], 'v_ln2_g': out['v_ln2_g'], 'v_ln2_b': out['v_ln2_b']}


def _loss(weights, diff, rest, loss_target):
    with _jax.named_scope("forward"):
        args = {**rest, TWIN_DIFF_INPUT: diff, **{k: w.astype(_WEIGHT_DTYPES[k]) for k, w in weights.items()}}
        y = _forward(args)
    with _jax.named_scope("loss_head"):
        err = _jnp.square(y.astype(_jnp.float32) - loss_target)
        return 0.5 * _jnp.sum(_jnp.mean(err, axis=-1)) if err.ndim else 0.5 * err


def _adamw(w, g, m, v):
    m = ADAM_B1 * m + (1.0 - ADAM_B1) * g
    v = ADAM_B2 * v + (1.0 - ADAM_B2) * _jnp.square(g)
    m_hat = m / (1.0 - ADAM_B1 ** ADAM_STEP)
    v_hat = v / (1.0 - ADAM_B2 ** ADAM_STEP)
    delta = -ADAM_LR * (m_hat / (_jnp.sqrt(v_hat) + ADAM_EPS) + ADAM_WD * w)
    return delta, m, v


def reference(x, c, w_ada, b_ada, w_in, b_in, conv_a_w, conv_a_b, ln_conv_g, ln_conv_b, w_conv_proj, w_attn_proj, w_mix_out, b_mix_out, ln1_g, ln1_b, w_ffn_up, ffn_conv_w, ffn_conv_b, w_ffn_down, ln2_g, ln2_b, loss_target, m_w_ada, m_b_ada, m_w_in, m_b_in, m_conv_a_w, m_conv_a_b, m_ln_conv_g, m_ln_conv_b, m_w_conv_proj, m_w_attn_proj, m_w_mix_out, m_b_mix_out, m_ln1_g, m_ln1_b, m_w_ffn_up, m_ffn_conv_w, m_ffn_conv_b, m_w_ffn_down, m_ln2_g, m_ln2_b, v_w_ada, v_b_ada, v_w_in, v_b_in, v_conv_a_w, v_conv_a_b, v_ln_conv_g, v_ln_conv_b, v_w_conv_proj, v_w_attn_proj, v_w_mix_out, v_b_mix_out, v_ln1_g, v_ln1_b, v_w_ffn_up, v_ffn_conv_w, v_ffn_conv_b, v_w_ffn_down, v_ln2_g, v_ln2_b):
    given = dict(x=x, c=c, w_ada=w_ada, b_ada=b_ada, w_in=w_in, b_in=b_in, conv_a_w=conv_a_w, conv_a_b=conv_a_b, ln_conv_g=ln_conv_g, ln_conv_b=ln_conv_b, w_conv_proj=w_conv_proj, w_attn_proj=w_attn_proj, w_mix_out=w_mix_out, b_mix_out=b_mix_out, ln1_g=ln1_g, ln1_b=ln1_b, w_ffn_up=w_ffn_up, ffn_conv_w=ffn_conv_w, ffn_conv_b=ffn_conv_b, w_ffn_down=w_ffn_down, ln2_g=ln2_g, ln2_b=ln2_b, loss_target=loss_target, m_w_ada=m_w_ada, m_b_ada=m_b_ada, m_w_in=m_w_in, m_b_in=m_b_in, m_conv_a_w=m_conv_a_w, m_conv_a_b=m_conv_a_b, m_ln_conv_g=m_ln_conv_g, m_ln_conv_b=m_ln_conv_b, m_w_conv_proj=m_w_conv_proj, m_w_attn_proj=m_w_attn_proj, m_w_mix_out=m_w_mix_out, m_b_mix_out=m_b_mix_out, m_ln1_g=m_ln1_g, m_ln1_b=m_ln1_b, m_w_ffn_up=m_w_ffn_up, m_ffn_conv_w=m_ffn_conv_w, m_ffn_conv_b=m_ffn_conv_b, m_w_ffn_down=m_w_ffn_down, m_ln2_g=m_ln2_g, m_ln2_b=m_ln2_b, v_w_ada=v_w_ada, v_b_ada=v_b_ada, v_w_in=v_w_in, v_b_in=v_b_in, v_conv_a_w=v_conv_a_w, v_conv_a_b=v_conv_a_b, v_ln_conv_g=v_ln_conv_g, v_ln_conv_b=v_ln_conv_b, v_w_conv_proj=v_w_conv_proj, v_w_attn_proj=v_w_attn_proj, v_w_mix_out=v_w_mix_out, v_b_mix_out=v_b_mix_out, v_ln1_g=v_ln1_g, v_ln1_b=v_ln1_b, v_w_ffn_up=v_w_ffn_up, v_ffn_conv_w=v_ffn_conv_w, v_ffn_conv_b=v_ffn_conv_b, v_w_ffn_down=v_w_ffn_down, v_ln2_g=v_ln2_g, v_ln2_b=v_ln2_b)
    weights = {n: given[n] for n in TWIN_WEIGHTS}
    shared = {n: given[n] for n in SHARED_INPUTS}
    per_example = {n: given[n] for n in ['x', 'c']}
    grad_fn = _jax.value_and_grad(_loss, argnums=(0, 1))

    def one_microbatch(ex, loss_target):
        ex = dict(ex)
        diff = ex.pop(TWIN_DIFF_INPUT)
        return grad_fn(weights, diff, {**shared, **ex}, loss_target)

    if N_MICROBATCH == 1:
        loss, (grad_w, grad_x) = one_microbatch(per_example, given["loss_target"])
    else:
        def body(carry, xs):
            loss_sum, grad_sum = carry
            l_k, (gw_k, gx_k) = one_microbatch(xs[0], xs[1])
            with _jax.named_scope("update"):
                return (loss_sum + l_k, _jax.tree.map(_jnp.add, grad_sum, gw_k)), gx_k

        init = (_jnp.zeros((), _jnp.float32), _jax.tree.map(_jnp.zeros_like, weights))
        (loss, grad_w), grad_x = _jax.lax.scan(body, init, (per_example, given["loss_target"]))
    with _jax.named_scope("update"):
        delta_w, new_m, new_v = {}, {}, {}
        for n in TWIN_WEIGHTS:
            delta_w[n], new_m[n], new_v[n] = _adamw(weights[n], grad_w[n], given["m_" + n], given["v_" + n])
    return (loss, grad_x, *[grad_w[n] for n in TWIN_WEIGHTS], *[delta_w[n] for n in TWIN_WEIGHTS],
            *[new_m[n] for n in TWIN_WEIGHTS], *[new_v[n] for n in TWIN_WEIGHTS])
```

```python
import functools
import math
from typing import NamedTuple

import jax
import jax.numpy as jnp
from jax import lax
from jax.experimental import pallas as pl
from jax.experimental.pallas import tpu as pltpu

F32, BF16 = jnp.float32, jnp.bfloat16
LN_EPS = 1e-5
ADAM_LR, ADAM_B1, ADAM_B2, ADAM_EPS, ADAM_WD, ADAM_STEP = 0.001, 0.9, 0.999, 1e-08, 0.01, 10
N_DEV = 8
LANES = 128
VMEM_LIMIT = 56 * 1024 * 1024
NEG = -1e30
NT = (((1,), (1,)), ((), ()))
TN = (((0,), (0,)), ((), ()))


class Cfg(NamedTuple):
    L: int
    Bl: int
    S: int
    D: int
    C: int
    KW: int
    H: int
    Dh: int
    F: int
    KF: int

    @property
    def T(self): return self.Bl * self.S
    @property
    def AW(self): return self.H * self.Dh
    @property
    def NM(self): return 2 * self.C + 3 * self.AW + 2 * self.D
    @property
    def q_off(self): return 2 * self.C
    @property
    def g_off(self): return 2 * self.C + 3 * self.AW
    @property
    def alpha(self): return (2.0 * self.L) ** 0.25


def _pcall(body, **kw):
    return pl.pallas_call(body, **kw)


def _params(*sem):
    return pltpu.CompilerParams(dimension_semantics=sem, vmem_limit_bytes=VMEM_LIMIT)


def _pick(n, prefs):
    for p in prefs:
        if n % p == 0:
            return p
    return n


def _sigmoid(x):
    return 1.0 / (1.0 + jnp.exp(-x))


def _ln_stats(x):
    mu = jnp.mean(x, axis=-1, keepdims=True)
    xc = x - mu
    var = jnp.mean(xc * xc, axis=-1, keepdims=True)
    rstd = lax.rsqrt(var + LN_EPS)
    return xc * rstd, rstd


def _ln_bwd(dxh, xh, rstd):
    return rstd * (dxh - jnp.mean(dxh, axis=-1, keepdims=True) - xh * jnp.mean(dxh * xh, axis=-1, keepdims=True))


def _matmul(a, b, *, mode, name, bias=None, add=None, out_dtype=F32, tm=None, tn=None, tk=None):
    if mode == "tn":
        K, M = a.shape
    else:
        M, K = a.shape
    N = b.shape[0] if mode == "nt" else b.shape[1]
    lane_tiles = (1536, 1408, 1024, 768, 512, 256, 128)
    tm = tm or _pick(M, lane_tiles if mode == "tn" else (1024, 512, 256, 128, 64, 32, 16, 8))
    tn = tn or _pick(N, lane_tiles)
    tk = tk or _pick(K, (512, 256, 128) if mode == "tn" else lane_tiles)
    nk = K // tk
    dn = {"nn": (((1,), (0,)), ((), ())), "nt": NT, "tn": TN}[mode]
    has_bias, has_add = bias is not None, add is not None

    def body(*refs):
        a_ref, b_ref = refs[0], refs[1]
        pos = 2
        bias_ref = refs[pos] if has_bias else None
        pos += has_bias
        add_ref = refs[pos] if has_add else None
        pos += has_add
        o_ref = refs[pos]
        part = lax.dot_general(a_ref[...], b_ref[...], dn, preferred_element_type=F32)

        def finish(acc):
            if has_bias:
                acc = acc + bias_ref[...]
            if has_add:
                acc = acc + add_ref[...]
            o_ref[...] = acc.astype(out_dtype)

        if nk == 1:
            finish(part)
        else:
            acc_ref = refs[pos + 1]
            k = pl.program_id(2)

            @pl.when(k == 0)
            def _():
                acc_ref[...] = part

            @pl.when(k > 0)
            def _():
                acc_ref[...] += part

            @pl.when(k == nk - 1)
            def _():
                finish(acc_ref[...])

    a_spec = pl.BlockSpec((tk, tm), lambda i, j, k: (k, i)) if mode == "tn" else pl.BlockSpec((tm, tk), lambda i, j, k: (i, k))
    b_spec = pl.BlockSpec((tn, tk), lambda i, j, k: (j, k)) if mode == "nt" else pl.BlockSpec((tk, tn), lambda i, j, k: (k, j))
    in_specs, args = [a_spec, b_spec], [a, b]
    if has_bias:
        in_specs.append(pl.BlockSpec((1, tn), lambda i, j, k: (0, j)))
        args.append(bias)
    if has_add:
        in_specs.append(pl.BlockSpec((tm, tn), lambda i, j, k: (i, j)))
        args.append(add)
    return _pcall(
        body, name=name, grid=(M // tm, N // tn, nk), in_specs=in_specs,
        out_specs=pl.BlockSpec((tm, tn), lambda i, j, k: (i, j)),
        out_shape=jax.ShapeDtypeStruct((M, N), out_dtype),
        scratch_shapes=[pltpu.VMEM((tm, tn), F32)] if nk > 1 else [],
        compiler_params=_params("parallel", "parallel", "arbitrary"),
    )(*args)


def _colsum(x, *, name):
    T, N = x.shape
    tr = _pick(T, (512, 256, 128, 64, 32, 16))
    tc = _pick(N, (1536, 1024, 512, 256, 128))

    def body(x_ref, o_ref):
        @pl.when(pl.program_id(1) == 0)
        def _():
            o_ref[...] = jnp.zeros_like(o_ref)

        o_ref[...] += jnp.sum(x_ref[...].astype(F32), axis=0, keepdims=True)

    return _pcall(body, name=name, grid=(N // tc, T // tr), in_specs=[pl.BlockSpec((tr, tc), lambda j, i: (i, j))],
                  out_specs=pl.BlockSpec((1, tc), lambda j, i: (0, j)), out_shape=jax.ShapeDtypeStruct((1, N), F32),
                  compiler_params=_params("parallel", "arbitrary"))(x)


def _row_tile(cfg):
    return _pick(cfg.S, (256, 128, 64, 32, 16, 8))


def _ln_mod_fwd(x, shift, scale, cfg, *, name):
    tr = _row_tile(cfg)
    tpb = cfg.S // tr

    def body(x_ref, sh_ref, sc_ref, u_ref):
        xh, _ = _ln_stats(x_ref[...])
        u_ref[...] = (xh * (1.0 + sc_ref[0]) + sh_ref[0]).astype(BF16)

    row = pl.BlockSpec((tr, cfg.D), lambda i: (i, 0))
    per_b = pl.BlockSpec((1, 1, cfg.D), lambda i: (i // tpb, 0, 0))
    return _pcall(body, name=name, grid=(cfg.T // tr,), in_specs=[row, per_b, per_b], out_specs=row,
                  out_shape=jax.ShapeDtypeStruct((cfg.T, cfg.D), BF16), compiler_params=_params("parallel"))(x, shift, scale)


def _res_ln_fwd(xin, br, gate, g, b, cfg, *, name, nxt=None):
    tr = _row_tile(cfg)
    tpb = cfg.S // tr
    alpha = cfg.alpha

    def body(*refs):
        x_ref, br_ref, gt_ref, g_ref, b_ref = refs[:5]
        r = alpha * x_ref[...] + (1.0 + gt_ref[0]) * br_ref[...]
        xh, _ = _ln_stats(r)
        xo = xh * g_ref[...] + b_ref[...]
        if nxt is None:
            refs[5][...] = xo
        else:
            sh_ref, sc_ref, xo_ref, u_ref = refs[5:]
            xo_ref[...] = xo
            uh, _ = _ln_stats(xo)
            u_ref[...] = (uh * (1.0 + sc_ref[0]) + sh_ref[0]).astype(BF16)

    row = pl.BlockSpec((tr, cfg.D), lambda i: (i, 0))
    per_b = pl.BlockSpec((1, 1, cfg.D), lambda i: (i // tpb, 0, 0))
    vec = pl.BlockSpec((1, cfg.D), lambda i: (0, 0))
    in_specs, args = [row, row, per_b, vec, vec], [xin, br, gate, g, b]
    out_specs, out_shape = row, jax.ShapeDtypeStruct((cfg.T, cfg.D), F32)
    if nxt is not None:
        in_specs += [per_b, per_b]
        args += list(nxt)
        out_specs = [row, row]
        out_shape = [out_shape, jax.ShapeDtypeStruct((cfg.T, cfg.D), BF16)]
    return _pcall(body, name=name, grid=(cfg.T // tr,), in_specs=in_specs, out_specs=out_specs, out_shape=out_shape,
                  compiler_params=_params("parallel"))(*args)


def _loss_grad(y, tgt, cfg, *, name):
    tr = _row_tile(cfg)
    nt = cfg.T // tr
    inv_d = 1.0 / cfg.D

    def body(y_ref, t_ref, dy_ref, ls_ref):
        e = y_ref[...] - t_ref[...]
        dy_ref[...] = e * inv_d
        ls_ref[...] = jnp.full((1, 1, LANES), jnp.sum(e * e), F32)

    row = pl.BlockSpec((tr, cfg.D), lambda i: (i, 0))
    return _pcall(body, name=name, grid=(nt,), in_specs=[row, row],
                  out_specs=[row, pl.BlockSpec((1, 1, LANES), lambda i: (i, 0, 0))],
                  out_shape=[jax.ShapeDtypeStruct((cfg.T, cfg.D), F32), jax.ShapeDtypeStruct((nt, 1, LANES), F32)],
                  compiler_params=_params("parallel"))(y, tgt)


def _res_ln_bwd(dy, xin, br, gate, g, cfg, *, name):
    tr = _row_tile(cfg)
    tpb = cfg.S // tr
    alpha = cfg.alpha

    def body(dy_ref, x_ref, br_ref, gt_ref, g_ref, dx_ref, dbr_ref, dg_ref, db_ref, dgt_ref, dbs_ref):
        i = pl.program_id(0)

        @pl.when(i == 0)
        def _():
            dg_ref[...] = jnp.zeros_like(dg_ref)
            db_ref[...] = jnp.zeros_like(db_ref)
            dbs_ref[...] = jnp.zeros_like(dbs_ref)

        @pl.when(i % tpb == 0)
        def _():
            dgt_ref[...] = jnp.zeros_like(dgt_ref)

        dy, brv, one_gate = dy_ref[...], br_ref[...], 1.0 + gt_ref[0]
        xh, rstd = _ln_stats(alpha * x_ref[...] + one_gate * brv)
        dg_ref[...] += jnp.sum(dy * xh, axis=0, keepdims=True)
        db_ref[...] += jnp.sum(dy, axis=0, keepdims=True)
        dr = _ln_bwd(dy * g_ref[...], xh, rstd)
        dx_ref[...] = alpha * dr
        dbr = one_gate * dr
        dbr_ref[...] = dbr.astype(BF16)
        dbs_ref[...] += jnp.sum(dbr, axis=0, keepdims=True)
        dgt_ref[0] += jnp.sum(dr * brv, axis=0, keepdims=True)

    row = pl.BlockSpec((tr, cfg.D), lambda i: (i, 0))
    per_b = pl.BlockSpec((1, 1, cfg.D), lambda i: (i // tpb, 0, 0))
    vec = pl.BlockSpec((1, cfg.D), lambda i: (0, 0))
    vs = jax.ShapeDtypeStruct((1, cfg.D), F32)
    return _pcall(body, name=name, grid=(cfg.T // tr,), in_specs=[row, row, row, per_b, vec],
                  out_specs=[row, row, vec, vec, per_b, vec],
                  out_shape=[jax.ShapeDtypeStruct((cfg.T, cfg.D), F32), jax.ShapeDtypeStruct((cfg.T, cfg.D), BF16), vs, vs,
                             jax.ShapeDtypeStruct((cfg.Bl, 1, cfg.D), F32), vs],
                  compiler_params=_params("arbitrary"))(dy, xin, br, gate, g)


def _ln_mod_bwd(du, xin, scale, dres, cfg, *, name):
    tr = _row_tile(cfg)
    tpb = cfg.S // tr

    def body(du_ref, x_ref, sc_ref, dres_ref, dx_ref, dsc_ref, dsh_ref):
        @pl.when(pl.program_id(0) % tpb == 0)
        def _():
            dsc_ref[...] = jnp.zeros_like(dsc_ref)
            dsh_ref[...] = jnp.zeros_like(dsh_ref)

        du = du_ref[...]
        xh, rstd = _ln_stats(x_ref[...])
        dsc_ref[0] += jnp.sum(du * xh, axis=0, keepdims=True)
        dsh_ref[0] += jnp.sum(du, axis=0, keepdims=True)
        dx_ref[...] = _ln_bwd(du * (1.0 + sc_ref[0]), xh, rstd) + dres_ref[...]

    row = pl.BlockSpec((tr, cfg.D), lambda i: (i, 0))
    per_b = pl.BlockSpec((1, 1, cfg.D), lambda i: (i // tpb, 0, 0))
    bs = jax.ShapeDtypeStruct((cfg.Bl, 1, cfg.D), F32)
    return _pcall(body, name=name, grid=(cfg.T // tr,), in_specs=[row, row, per_b, row], out_specs=[row, per_b, per_b],
                  out_shape=[jax.ShapeDtypeStruct((cfg.T, cfg.D), F32), bs, bs],
                  compiler_params=_params("arbitrary"))(du, xin, scale, dres)


def _merge_tiles(cfg):
    tr = _pick(cfg.T, (512, 256, 128, 64, 32, 16))
    tc = _pick(math.gcd(cfg.g_off, cfg.D), (512, 256, 128))
    return tr, tc


def _merge_fwd(zm, ya, yb, cfg, *, name):
    tr, tc = _merge_tiles(cfg)
    ga0, gb0 = cfg.g_off // tc, (cfg.g_off + cfg.D) // tc

    def body(ga_ref, gb_ref, ya_ref, yb_ref, m_ref):
        m_ref[...] = (_sigmoid(ga_ref[...]) * ya_ref[...] + _sigmoid(gb_ref[...]) * yb_ref[...]).astype(BF16)

    blk = pl.BlockSpec((tr, tc), lambda i, j: (i, j))
    return _pcall(body, name=name, grid=(cfg.T // tr, cfg.D // tc),
                  in_specs=[pl.BlockSpec((tr, tc), lambda i, j: (i, ga0 + j)), pl.BlockSpec((tr, tc), lambda i, j: (i, gb0 + j)), blk, blk],
                  out_specs=blk, out_shape=jax.ShapeDtypeStruct((cfg.T, cfg.D), BF16),
                  compiler_params=_params("parallel", "parallel"))(zm, zm, ya, yb)


def _merge_bwd(dm, zm, ya, yb, cfg, *, name):
    tr, tc = _merge_tiles(cfg)
    ga0, gb0 = cfg.g_off // tc, (cfg.g_off + cfg.D) // tc

    def body(dm_ref, ga_ref, gb_ref, ya_ref, yb_ref, dya_ref, dyb_ref, dga_ref, dgb_ref):
        dm = dm_ref[...]
        ga, gb = _sigmoid(ga_ref[...]), _sigmoid(gb_ref[...])
        dya_ref[...] = (dm * ga).astype(BF16)
        dyb_ref[...] = (dm * gb).astype(BF16)
        dga_ref[...] = (dm * ya_ref[...] * ga * (1.0 - ga)).astype(BF16)
        dgb_ref[...] = (dm * yb_ref[...] * gb * (1.0 - gb)).astype(BF16)

    blk = pl.BlockSpec((tr, tc), lambda i, j: (i, j))
    o = jax.ShapeDtypeStruct((cfg.T, cfg.D), BF16)
    return _pcall(body, name=name, grid=(cfg.T // tr, cfg.D // tc),
                  in_specs=[blk, pl.BlockSpec((tr, tc), lambda i, j: (i, ga0 + j)), pl.BlockSpec((tr, tc), lambda i, j: (i, gb0 + j)), blk, blk],
                  out_specs=[blk] * 4, out_shape=[o] * 4, compiler_params=_params("parallel", "parallel"))(dm, zm, zm, ya, yb)


CONV_A_HALO = 32
CONV_A_CHUNK = 32


def _conv_a_tile(cfg):
    assert cfg.KW - 1 <= CONV_A_HALO
    return _pick(cfg.S, (256, 128, 64, 32))


def _conv_a_fwd(zm, w, cb, g, b, cfg, *, name):
    C, KW, HALO, CH = cfg.C, cfg.KW, CONV_A_HALO, CONV_A_CHUNK
    ts = _conv_a_tile(cfg)
    tpb = cfg.S // ts
    lead = HALO - (KW - 1)

    def body(z_ref, zp_ref, w_ref, cb_ref, g_ref, b_ref, o_ref, a0_s):
        first = pl.program_id(0) % tpb == 0
        prev = zp_ref[:, :C] * _sigmoid(zp_ref[:, C:])
        a0_s[0:HALO, :] = jnp.where(first, 0.0, prev)
        a0_s[HALO:HALO + ts, :] = z_ref[:, :C] * _sigmoid(z_ref[:, C:])
        for r0 in range(0, ts, CH):
            acc = jnp.zeros((CH, C), F32)
            for k in range(KW):
                acc = acc + w_ref[k:k + 1, :] * a0_s[r0 + lead + k:r0 + lead + k + CH, :]
            xh, _ = _ln_stats(acc + cb_ref[...])
            a2 = xh * g_ref[...] + b_ref[...]
            o_ref[r0:r0 + CH, :] = (a2 * _sigmoid(a2)).astype(BF16)

    hb = ts // HALO
    vec = pl.BlockSpec((1, C), lambda i: (0, 0))
    return _pcall(body, name=name, grid=(cfg.T // ts,),
                  in_specs=[pl.BlockSpec((ts, 2 * C), lambda i: (i, 0)),
                            pl.BlockSpec((HALO, 2 * C), lambda i: (jnp.maximum(i * hb - 1, 0), 0)),
                            pl.BlockSpec((32, C), lambda i: (0, 0)), vec, vec, vec],
                  out_specs=pl.BlockSpec((ts, C), lambda i: (i, 0)), out_shape=jax.ShapeDtypeStruct((cfg.T, C), BF16),
                  scratch_shapes=[pltpu.VMEM((HALO + ts, C), F32)], compiler_params=_params("parallel"))(zm, zm, w, cb, g, b)


def _conv_a_bwd(da3, zm, w, cb, g, b, cfg, *, name):
    C, KW, HALO, CH = cfg.C, cfg.KW, CONV_A_HALO, CONV_A_CHUNK
    ts = _conv_a_tile(cfg)
    tpb = cfg.S // ts
    nt = cfg.T // ts
    lead = HALO - (KW - 1)
    ext = ts + HALO

    def body(z_ref, zp_ref, zn_ref, d_ref, dn_ref, w_ref, cb_ref, g_ref, b_ref,
             dz_ref, dw_ref, dcb_ref, dg_ref, db_ref, a0_s, d3_s, da1_s):
        i = pl.program_id(0)
        first, last = i % tpb == 0, i % tpb == tpb - 1

        @pl.when(i == 0)
        def _():
            dw_ref[...] = jnp.zeros_like(dw_ref)
            dcb_ref[...] = jnp.zeros_like(dcb_ref)
            dg_ref[...] = jnp.zeros_like(dg_ref)
            db_ref[...] = jnp.zeros_like(db_ref)

        a0_s[0:HALO, :] = jnp.where(first, 0.0, zp_ref[:, :C] * _sigmoid(zp_ref[:, C:]))
        a0_s[HALO:HALO + ts, :] = z_ref[:, :C] * _sigmoid(z_ref[:, C:])
        a0_s[HALO + ts:HALO + ext, :] = zn_ref[:, :C] * _sigmoid(zn_ref[:, C:])
        d3_s[0:ts, :] = d_ref[...]
        d3_s[ts:ext, :] = jnp.where(last, 0.0, dn_ref[...])
        dcb, dg, db = jnp.zeros((1, C), F32), jnp.zeros((1, C), F32), jnp.zeros((1, C), F32)
        for r0 in range(0, ext, CH):
            acc = jnp.zeros((CH, C), F32)
            for k in range(KW):
                acc = acc + w_ref[k:k + 1, :] * a0_s[r0 + lead + k:r0 + lead + k + CH, :]
            xh, rstd = _ln_stats(acc + cb_ref[...])
            a2 = xh * g_ref[...] + b_ref[...]
            sg = _sigmoid(a2)
            da2 = d3_s[r0:r0 + CH, :] * (sg * (1.0 + a2 * (1.0 - sg)))
            da1 = _ln_bwd(da2 * g_ref[...], xh, rstd)
            da1_s[r0:r0 + CH, :] = da1
            if r0 < ts:
                dg = dg + jnp.sum(da2 * xh, axis=0, keepdims=True)
                db = db + jnp.sum(da2, axis=0, keepdims=True)
                dcb = dcb + jnp.sum(da1, axis=0, keepdims=True)
        dg_ref[...] += dg
        db_ref[...] += db
        dcb_ref[...] += dcb
        for k in range(KW):
            dwk = jnp.zeros((1, C), F32)
            for r0 in range(0, ts, CH):
                dwk = dwk + jnp.sum(da1_s[r0:r0 + CH, :] * a0_s[r0 + lead + k:r0 + lead + k + CH, :], axis=0, keepdims=True)
            dw_ref[k:k + 1, :] += dwk
        for r0 in range(0, ts, CH):
            da0 = jnp.zeros((CH, C), F32)
            for k in range(KW):
                da0 = da0 + w_ref[k:k + 1, :] * da1_s[r0 + KW - 1 - k:r0 + KW - 1 - k + CH, :]
            val, sg = z_ref[r0:r0 + CH, :C], _sigmoid(z_ref[r0:r0 + CH, C:])
            dz_ref[r0:r0 + CH, :C] = (da0 * sg).astype(BF16)
            dz_ref[r0:r0 + CH, C:] = (da0 * val * sg * (1.0 - sg)).astype(BF16)

    hb = ts // HALO
    nhb = cfg.T // HALO
    vec = pl.BlockSpec((1, C), lambda i: (0, 0))
    vs = jax.ShapeDtypeStruct((1, C), F32)
    return _pcall(body, name=name, grid=(nt,),
                  in_specs=[pl.BlockSpec((ts, 2 * C), lambda i: (i, 0)),
                            pl.BlockSpec((HALO, 2 * C), lambda i: (jnp.maximum(i * hb - 1, 0), 0)),
                            pl.BlockSpec((HALO, 2 * C), lambda i: (jnp.minimum((i + 1) * hb, nhb - 1), 0)),
                            pl.BlockSpec((ts, C), lambda i: (i, 0)),
                            pl.BlockSpec((HALO, C), lambda i: (jnp.minimum((i + 1) * hb, nhb - 1), 0)),
                            pl.BlockSpec((32, C), lambda i: (0, 0)), vec, vec, vec],
                  out_specs=[pl.BlockSpec((ts, 2 * C), lambda i: (i, 0)), pl.BlockSpec((32, C), lambda i: (0, 0)), vec, vec, vec],
                  out_shape=[jax.ShapeDtypeStruct((cfg.T, 2 * C), BF16), jax.ShapeDtypeStruct((32, C), F32), vs, vs, vs],
                  scratch_shapes=[pltpu.VMEM((HALO + ext, C), F32), pltpu.VMEM((ext, C), F32), pltpu.VMEM((ext, C), F32)],
                  compiler_params=_params("arbitrary"))(zm, zm, zm, da3, da3, w, cb, g, b)


def _cum_tile(cfg):
    return _pick(cfg.S, (256, 128, 64, 32, 16, 8))


def _fgate_fwd(zf, cfg, *, name):
    tc = _cum_tile(cfg)
    tpb = cfg.S // tc

    def body(z_ref, o_ref, carry):
        @pl.when(pl.program_id(0) % tpb == 0)
        def _():
            carry[...] = jnp.zeros_like(carry)

        z = z_ref[...]
        logf = jnp.minimum(z, 0.0) - jnp.log(1.0 + jnp.exp(-jnp.abs(z)))
        tri = (lax.broadcasted_iota(jnp.int32, (tc, tc), 0) >= lax.broadcasted_iota(jnp.int32, (tc, tc), 1)).astype(F32)
        cum = jnp.dot(tri, logf, precision=lax.Precision.HIGHEST, preferred_element_type=F32) + carry[...]
        o_ref[...] = cum
        carry[...] = cum[tc - 1:tc, :]

    blk = pl.BlockSpec((tc, LANES), lambda i: (i, 0))
    return _pcall(body, name=name, grid=(cfg.T // tc,), in_specs=[blk], out_specs=blk,
                  out_shape=jax.ShapeDtypeStruct((cfg.T, LANES), F32), scratch_shapes=[pltpu.VMEM((1, LANES), F32)],
                  compiler_params=_params("arbitrary"))(zf)


def _fgate_bwd(dcum, zf, cfg, *, name):
    tc = _cum_tile(cfg)
    tpb = cfg.S // tc
    nt = cfg.T // tc

    def body(d_ref, z_ref, o_ref, carry):
        @pl.when(pl.program_id(0) % tpb == 0)
        def _():
            carry[...] = jnp.zeros_like(carry)

        tri = (lax.broadcasted_iota(jnp.int32, (tc, tc), 0) <= lax.broadcasted_iota(jnp.int32, (tc, tc), 1)).astype(F32)
        suf = jnp.dot(tri, d_ref[...], precision=lax.Precision.HIGHEST, preferred_element_type=F32) + carry[...]
        o_ref[...] = (suf * _sigmoid(-z_ref[...])).astype(BF16)
        carry[...] = suf[0:1, :]

    blk = pl.BlockSpec((tc, LANES), lambda i: (nt - 1 - i, 0))
    return _pcall(body, name=name, grid=(nt,), in_specs=[blk, blk], out_specs=blk,
                  out_shape=jax.ShapeDtypeStruct((cfg.T, LANES), BF16), scratch_shapes=[pltpu.VMEM((1, LANES), F32)],
                  compiler_params=_params("arbitrary"))(dcum, zf)


def _attn_tiles(cfg):
    assert LANES % cfg.Dh == 0 and cfg.H % (LANES // cfg.Dh) == 0
    t = _pick(cfg.S, (256, 128))
    return t, LANES // cfg.Dh


def _attn_fwd(zm, cum_t, cfg, *, name):
    S, Dh = cfg.S, cfg.Dh
    t, hp = _attn_tiles(cfg)
    nq, nb = S // t, cfg.H // hp
    qb, kb, vb = cfg.q_off // LANES, (cfg.q_off + cfg.AW) // LANES, (cfg.q_off + 2 * cfg.AW) // LANES
    scale = Dh ** -0.5

    def body(q_ref, k_ref, v_ref, ct_ref, o_ref, lse_ref):
        qi = pl.program_id(2)
        causal = lax.broadcasted_iota(jnp.int32, (t, t), 0) >= lax.broadcasted_iota(jnp.int32, (t, t), 1)
        outs = []
        for hd in range(hp):
            lanes = slice(hd * Dh, (hd + 1) * Dh)
            q = (q_ref[:, lanes] * scale).astype(BF16)

            def chunk(j, carry, masked, lanes=lanes, q=q, hd=hd):
                m, l, acc = carry
                r = pl.multiple_of(j * t, t)
                kc = k_ref[pl.ds(r, t), lanes].astype(BF16)
                vc = v_ref[pl.ds(r, t), lanes].astype(BF16)
                s = lax.dot_general(q, kc, NT, preferred_element_type=F32) - ct_ref[0, 0, hd:hd + 1, pl.ds(r, t)]
                if masked:
                    s = jnp.where(causal, s, NEG)
                m_new = jnp.maximum(m, jnp.max(s, axis=1, keepdims=True))
                a = jnp.exp(m - m_new)
                p = jnp.exp(s - m_new)
                l = a * l + jnp.sum(p, axis=1, keepdims=True)
                acc = a * acc + jnp.dot(p.astype(BF16), vc, preferred_element_type=F32)
                return m_new, l, acc

            init = (jnp.full((t, 1), NEG, F32), jnp.zeros((t, 1), F32), jnp.zeros((t, Dh), F32))
            carry = lax.fori_loop(0, qi, functools.partial(chunk, masked=False), init)
            m, l, acc = chunk(qi, carry, True)
            outs.append(acc / l)
            lse_ref[0, hd] = jnp.broadcast_to(m + jnp.log(l), (t, LANES))
        o_ref[...] = jnp.concatenate(outs, axis=1).astype(BF16)

    return _pcall(body, name=name, grid=(cfg.Bl, nb, nq),
                  in_specs=[pl.BlockSpec((t, LANES), lambda b, h, i: (b * nq + i, qb + h)),
                            pl.BlockSpec((S, LANES), lambda b, h, i: (b, kb + h)),
                            pl.BlockSpec((S, LANES), lambda b, h, i: (b, vb + h)),
                            pl.BlockSpec((1, 1, 8, S), lambda b, h, i: (b, h, 0, 0))],
                  out_specs=[pl.BlockSpec((t, LANES), lambda b, h, i: (b * nq + i, h)),
                             pl.BlockSpec((1, hp, t, LANES), lambda b, h, i: (b, h, i, 0))],
                  out_shape=[jax.ShapeDtypeStruct((cfg.T, cfg.AW), BF16), jax.ShapeDtypeStruct((cfg.Bl, cfg.H, S, LANES), F32)],
                  compiler_params=_params("parallel", "parallel", "arbitrary"))(zm, zm, zm, cum_t)


def _attn_bwd(zm, cum_t, o, do, lse, cfg, *, name):
    S, Dh = cfg.S, cfg.Dh
    t, hp = _attn_tiles(cfg)
    nq, nb = S // t, cfg.H // hp
    qb, kb, vb = cfg.q_off // LANES, (cfg.q_off + cfg.AW) // LANES, (cfg.q_off + 2 * cfg.AW) // LANES
    scale = Dh ** -0.5

    def body(q_ref, k_ref, v_ref, ct_ref, o_ref, do_ref, lse_ref, dq_ref, dk_ref, dv_ref, dct_ref, dq_s, dk_s, dv_s, dd_s):
        dq_s[...] = jnp.zeros_like(dq_s)
        dct_ref[...] = jnp.zeros_like(dct_ref)
        iota_r = lax.broadcasted_iota(jnp.int32, (t, t), 0)
        iota_c = lax.broadcasted_iota(jnp.int32, (t, t), 1)

        def dd_step(i, _):
            r = pl.multiple_of(i * t, t)
            prod = do_ref[pl.ds(r, t), :].astype(F32) * o_ref[pl.ds(r, t), :].astype(F32)
            for hd in range(hp):
                dd_s[hd, pl.ds(r, t), :] = jnp.broadcast_to(
                    jnp.sum(prod[:, hd * Dh:(hd + 1) * Dh], axis=1, keepdims=True), (t, LANES))
            return 0

        lax.fori_loop(0, nq, dd_step, 0)

        def kv_step(j, _):
            rk = pl.multiple_of(j * t, t)
            kcs = [k_ref[pl.ds(rk, t), hd * Dh:(hd + 1) * Dh].astype(BF16) for hd in range(hp)]
            vcs = [v_ref[pl.ds(rk, t), hd * Dh:(hd + 1) * Dh].astype(BF16) for hd in range(hp)]
            cks = [ct_ref[0, 0, hd:hd + 1, pl.ds(rk, t)] for hd in range(hp)]

            def q_step(i, carry):
                rq = pl.multiple_of(i * t, t)
                mask = (rq + iota_r) >= (rk + iota_c)
                new, dq_parts = [], []
                for hd in range(hp):
                    dk_h, dv_h, dc_h = carry[hd]
                    lanes = slice(hd * Dh, (hd + 1) * Dh)
                    q = (q_ref[pl.ds(rq, t), lanes] * scale).astype(BF16)
                    dob = do_ref[pl.ds(rq, t), lanes]
                    s = lax.dot_general(q, kcs[hd], NT, preferred_element_type=F32) - cks[hd]
                    p = jnp.where(mask, jnp.exp(s - lse_ref[0, hd, pl.ds(rq, t), :][:, :1]), 0.0)
                    dp = lax.dot_general(dob, vcs[hd], NT, preferred_element_type=F32)
                    ds = p * (dp - dd_s[hd, pl.ds(rq, t), :][:, :1])
                    dsb = ds.astype(BF16)
                    dv_h = dv_h + lax.dot_general(p.astype(BF16), dob, TN, preferred_element_type=F32)
                    dk_h = dk_h + lax.dot_general(dsb, q, TN, preferred_element_type=F32)
                    dq_parts.append(jnp.dot(dsb, kcs[hd], preferred_element_type=F32) * scale)
                    dc_h = dc_h + jnp.sum(ds, axis=0, keepdims=True)
                    new.append((dk_h, dv_h, dc_h))
                dq_s[pl.ds(rq, t), :] += jnp.concatenate(dq_parts, axis=1)
                return tuple(new)

            zero = tuple((jnp.zeros((t, Dh), F32), jnp.zeros((t, Dh), F32), jnp.zeros((1, t), F32)) for _ in range(hp))
            res = lax.fori_loop(j, nq, q_step, zero)
            dk_s[pl.ds(rk, t), :] = jnp.concatenate([res[hd][0] for hd in range(hp)], axis=1)
            dv_s[pl.ds(rk, t), :] = jnp.concatenate([res[hd][1] for hd in range(hp)], axis=1)
            for hd in range(hp):
                dct_ref[0, 0, hd:hd + 1, pl.ds(rk, t)] = -res[hd][2]
            return 0

        lax.fori_loop(0, nq, kv_step, 0)
        dq_ref[...] = dq_s[...].astype(BF16)
        dk_ref[...] = dk_s[...].astype(BF16)
        dv_ref[...] = dv_s[...].astype(BF16)

    blk = pl.BlockSpec((S, LANES), lambda b, h: (b, h))
    ct = pl.BlockSpec((1, 1, 8, S), lambda b, h: (b, h, 0, 0))
    os_ = jax.ShapeDtypeStruct((cfg.T, cfg.AW), BF16)
    return _pcall(body, name=name, grid=(cfg.Bl, nb),
                  in_specs=[pl.BlockSpec((S, LANES), lambda b, h: (b, qb + h)), pl.BlockSpec((S, LANES), lambda b, h: (b, kb + h)),
                            pl.BlockSpec((S, LANES), lambda b, h: (b, vb + h)), ct, blk, blk,
                            pl.BlockSpec((1, hp, S, LANES), lambda b, h: (b, h, 0, 0))],
                  out_specs=[blk, blk, blk, ct],
                  out_shape=[os_, os_, os_, jax.ShapeDtypeStruct((cfg.Bl, nb, 8, S), F32)],
                  scratch_shapes=[pltpu.VMEM((S, LANES), F32)] * 3 + [pltpu.VMEM((hp, S, LANES), F32)],
                  compiler_params=_params("parallel", "parallel"))(zm, zm, zm, cum_t, o, do, lse)


FFN_HALO = 8


def _ffn_tiles(cfg):
    assert cfg.KF - 1 <= FFN_HALO
    return _pick(cfg.S, (512, 256, 128, 64, 32, 16, 8)), _pick(cfg.F, (256, 128))


def _gelu(x):
    return 0.5 * x * (1.0 + lax.erf(x * (2.0 ** -0.5)))


def _gelu_grad(x):
    return 0.5 * (1.0 + lax.erf(x * (2.0 ** -0.5))) + x * jnp.exp(-0.5 * x * x) * ((2.0 * math.pi) ** -0.5)


def _ffn_conv_fwd(h0, w, cb, cfg, *, name):
    KF, HALO = cfg.KF, FFN_HALO
    ts, tf = _ffn_tiles(cfg)
    tpb, nf = cfg.S // ts, cfg.F // tf
    lead = HALO - (KF - 1)

    def body(g_ref, gp_ref, l_ref, lp_ref, wg_ref, wl_ref, cg_ref, cl_ref, o_ref, g_s, l_s):
        first = pl.program_id(1) % tpb == 0
        g_s[0:HALO, :] = jnp.where(first, 0.0, gp_ref[...])
        l_s[0:HALO, :] = jnp.where(first, 0.0, lp_ref[...])
        g_s[HALO:HALO + ts, :] = g_ref[...]
        l_s[HALO:HALO + ts, :] = l_ref[...]
        hg, hl = cg_ref[...], cl_ref[...]
        for k in range(KF):
            hg = hg + wg_ref[k:k + 1, :] * g_s[lead + k:lead + k + ts, :]
            hl = hl + wl_ref[k:k + 1, :] * l_s[lead + k:lead + k + ts, :]
        o_ref[...] = (_gelu(hg) * hl).astype(BF16)

    hb = ts // HALO
    prev = lambda off: pl.BlockSpec((HALO, tf), lambda j, i: (jnp.maximum(i * hb - 1, 0), off + j))
    main = lambda off: pl.BlockSpec((ts, tf), lambda j, i: (i, off + j))
    wsp = lambda off: pl.BlockSpec((8, tf), lambda j, i: (0, off + j))
    vsp = lambda off: pl.BlockSpec((1, tf), lambda j, i: (0, off + j))
    return _pcall(body, name=name, grid=(nf, cfg.T // ts),
                  in_specs=[main(0), prev(0), main(nf), prev(nf), wsp(0), wsp(nf), vsp(0), vsp(nf)],
                  out_specs=pl.BlockSpec((ts, tf), lambda j, i: (i, j)), out_shape=jax.ShapeDtypeStruct((cfg.T, cfg.F), BF16),
                  scratch_shapes=[pltpu.VMEM((HALO + ts, tf), F32)] * 2,
                  compiler_params=_params("parallel", "parallel"))(h0, h0, h0, h0, w, w, cb, cb)


def _ffn_conv_bwd(df, h0, w, cb, cfg, *, name):
    KF, HALO = cfg.KF, FFN_HALO
    ts, tf = _ffn_tiles(cfg)
    tpb, nf = cfg.S // ts, cfg.F // tf
    lead = HALO - (KF - 1)
    ext = ts + HALO

    def body(g_ref, gp_ref, gn_ref, l_ref, lp_ref, ln_ref, d_ref, dn_ref, wg_ref, wl_ref, cg_ref, cl_ref,
             dg_ref, dl_ref, dwg_ref, dwl_ref, dcg_ref, dcl_ref, g_s, l_s, d_s, dhg_s, dhl_s):
        i = pl.program_id(1)
        first, last = i % tpb == 0, i % tpb == tpb - 1

        @pl.when(i == 0)
        def _():
            dwg_ref[...] = jnp.zeros_like(dwg_ref)
            dwl_ref[...] = jnp.zeros_like(dwl_ref)
            dcg_ref[...] = jnp.zeros_like(dcg_ref)
            dcl_ref[...] = jnp.zeros_like(dcl_ref)

        for s, main, prev, nxt in ((g_s, g_ref, gp_ref, gn_ref), (l_s, l_ref, lp_ref, ln_ref)):
            s[0:HALO, :] = jnp.where(first, 0.0, prev[...])
            s[HALO:HALO + ts, :] = main[...]
            s[HALO + ts:HALO + ext, :] = nxt[...]
        d_s[0:ts, :] = d_ref[...]
        d_s[ts:ext, :] = jnp.where(last, 0.0, dn_ref[...])
        hg, hl = cg_ref[...], cl_ref[...]
        for k in range(KF):
            hg = hg + wg_ref[k:k + 1, :] * g_s[lead + k:lead + k + ext, :]
            hl = hl + wl_ref[k:k + 1, :] * l_s[lead + k:lead + k + ext, :]
        df_e = d_s[...]
        dhg_s[...] = df_e * hl * _gelu_grad(hg)
        dhl_s[...] = df_e * _gelu(hg)
        for dh_s, h_s, w_ref, dx_ref, dw_ref, dc_ref in ((dhg_s, g_s, wg_ref, dg_ref, dwg_ref, dcg_ref),
                                                         (dhl_s, l_s, wl_ref, dl_ref, dwl_ref, dcl_ref)):
            dh = dh_s[0:ts, :]
            dc_ref[...] += jnp.sum(dh, axis=0, keepdims=True)
            dx = jnp.zeros((ts, tf), F32)
            for k in range(KF):
                dw_ref[k:k + 1, :] += jnp.sum(dh * h_s[lead + k:lead + k + ts, :], axis=0, keepdims=True)
                dx = dx + w_ref[k:k + 1, :] * dh_s[KF - 1 - k:KF - 1 - k + ts, :]
            dx_ref[...] = dx.astype(BF16)

    hb = ts // HALO
    nhb = cfg.T // HALO
    main = lambda off: pl.BlockSpec((ts, tf), lambda j, i: (i, off + j))
    prev = lambda off: pl.BlockSpec((HALO, tf), lambda j, i: (jnp.maximum(i * hb - 1, 0), off + j))
    nxt = lambda off: pl.BlockSpec((HALO, tf), lambda j, i: (jnp.minimum((i + 1) * hb, nhb - 1), off + j))
    wsp = lambda off: pl.BlockSpec((8, tf), lambda j, i: (0, off + j))
    vsp = lambda off: pl.BlockSpec((1, tf), lambda j, i: (0, off + j))
    dxs, dws, dcs = (jax.ShapeDtypeStruct((cfg.T, cfg.F), BF16), jax.ShapeDtypeStruct((8, cfg.F), F32),
                     jax.ShapeDtypeStruct((1, cfg.F), F32))
    return _pcall(body, name=name, grid=(nf, cfg.T // ts),
                  in_specs=[main(0), prev(0), nxt(0), main(nf), prev(nf), nxt(nf), main(0), nxt(0),
                            wsp(0), wsp(nf), vsp(0), vsp(nf)],
                  out_specs=[main(0), main(0), wsp(0), wsp(0), vsp(0), vsp(0)],
                  out_shape=[dxs, dxs, dws, dws, dcs, dcs],
                  scratch_shapes=[pltpu.VMEM((HALO + ext, tf), F32)] * 2 + [pltpu.VMEM((ext, tf), F32)] * 3,
                  compiler_params=_params("parallel", "arbitrary"))(h0, h0, h0, h0, h0, h0, df, df, w, w, cb, cb)


def _ada_fwd(c_all, w, b, *, name):
    L, D, n = w.shape
    B = c_all.shape[0]

    def body(c_ref, w_ref, b_ref, o_ref):
        c = c_ref[...]
        act = (c * _sigmoid(c)).astype(BF16)
        o_ref[0] = jnp.dot(act, w_ref[0].astype(BF16), preferred_element_type=F32) + b_ref[0]

    return _pcall(body, name=name, grid=(L,),
                  in_specs=[pl.BlockSpec((B, D), lambda l: (0, 0)), pl.BlockSpec((1, D, n), lambda l: (l, 0, 0)),
                            pl.BlockSpec((1, 1, n), lambda l: (l, 0, 0))],
                  out_specs=pl.BlockSpec((1, B, n), lambda l: (l, 0, 0)), out_shape=jax.ShapeDtypeStruct((L, B, n), F32),
                  compiler_params=_params("parallel"))(c_all, w, b)


def _ada_bwd(c_all, dmod, *, name):
    L, B, n = dmod.shape
    D = c_all.shape[1]

    def body(c_ref, d_ref, o_ref):
        c = c_ref[...]
        act = (c * _sigmoid(c)).astype(BF16)
        o_ref[0] = lax.dot_general(act, d_ref[0].astype(BF16), TN, preferred_element_type=F32)

    return _pcall(body, name=name, grid=(L,),
                  in_specs=[pl.BlockSpec((B, D), lambda l: (0, 0)), pl.BlockSpec((1, B, n), lambda l: (l, 0, 0))],
                  out_specs=pl.BlockSpec((1, D, n), lambda l: (l, 0, 0)), out_shape=jax.ShapeDtypeStruct((L, D, n), F32),
                  compiler_params=_params("parallel"))(c_all, dmod)


def _slot_sum(x, *, name):
    n, R, W = x.shape
    tr = _pick(R, (256, 128, 64, 32, 16, 8))

    def body(x_ref, o_ref):
        acc = x_ref[0].astype(F32)
        for k in range(1, n):
            acc = acc + x_ref[k].astype(F32)
        o_ref[...] = acc

    return _pcall(body, name=name, grid=(R // tr,), in_specs=[pl.BlockSpec((n, tr, W), lambda i: (0, i, 0))],
                  out_specs=pl.BlockSpec((tr, W), lambda i: (i, 0)), out_shape=jax.ShapeDtypeStruct((R, W), F32),
                  compiler_params=_params("parallel"))(x)


def _adamw(gs, w, m, v, *, name):
    n, R, W = gs.shape
    tr = _pick(R, (256, 128, 64, 32, 16, 8))
    c1, c2 = 1.0 - ADAM_B1 ** ADAM_STEP, 1.0 - ADAM_B2 ** ADAM_STEP

    def body(g_ref, w_ref, m_ref, v_ref, go_ref, d_ref, mo_ref, vo_ref):
        g = g_ref[0].astype(F32)
        for k in range(1, n):
            g = g + g_ref[k].astype(F32)
        m2 = ADAM_B1 * m_ref[...] + (1.0 - ADAM_B1) * g
        v2 = ADAM_B2 * v_ref[...] + (1.0 - ADAM_B2) * (g * g)
        go_ref[...] = g
        mo_ref[...] = m2
        vo_ref[...] = v2
        d_ref[...] = -ADAM_LR * ((m2 / c1) / (jnp.sqrt(v2 / c2) + ADAM_EPS) + ADAM_WD * w_ref[...])

    blk = pl.BlockSpec((tr, W), lambda i: (i, 0))
    o = jax.ShapeDtypeStruct((R, W), F32)
    return _pcall(body, name=name, grid=(R // tr,), in_specs=[pl.BlockSpec((n, tr, W), lambda i: (0, i, 0)), blk, blk, blk],
                  out_specs=[blk] * 4, out_shape=[o] * 4, compiler_params=_params("parallel"))(gs, w, m, v)


def _exchange(x, *, all_to_all, name):
    blk = x.shape[1:] if all_to_all else x.shape

    def body(x_ref, o_ref, send_sems, recv_sems, local_sem):
        mx, my, mc = lax.axis_index("x"), lax.axis_index("y"), lax.axis_index("c")
        me = 4 * mx + 2 * my + mc
        src = (lambda idx: x_ref.at[idx]) if all_to_all else (lambda idx: x_ref)
        mine = pltpu.make_async_copy(src(me), o_ref.at[me], local_sem)
        mine.start()
        copies = []
        for k in range(1, N_DEV):
            px, py, pc = mx ^ ((k >> 2) & 1), my ^ ((k >> 1) & 1), mc ^ (k & 1)
            cp = pltpu.make_async_remote_copy(
                src_ref=src(4 * px + 2 * py + pc), dst_ref=o_ref.at[me], send_sem=send_sems.at[k - 1],
                recv_sem=recv_sems.at[k - 1], device_id=(px, py, pc), device_id_type=pl.DeviceIdType.MESH)
            cp.start()
            copies.append(cp)
        for cp in copies:
            cp.wait()
        mine.wait()

    anyspec = pl.BlockSpec(memory_space=pl.ANY)
    return _pcall(body, name=name, in_specs=[anyspec], out_specs=anyspec,
                  out_shape=jax.ShapeDtypeStruct((N_DEV,) + tuple(blk), x.dtype),
                  scratch_shapes=[pltpu.SemaphoreType.DMA((N_DEV - 1,)), pltpu.SemaphoreType.DMA((N_DEV - 1,)),
                                  pltpu.SemaphoreType.DMA(())])(x)


PACK_ROWS = 16


def _pack(arrs, width, dtype, lead=0):
    parts, segs, r = [], [], 0
    for a in arrs:
        lshape, shape = a.shape[:lead], a.shape[lead:]
        n = math.prod(shape)
        rows = -(-n // width)
        rows_p = -(-rows // PACK_ROWS) * PACK_ROWS
        flat = a.reshape(lshape + (n,)).astype(dtype)
        flat = jnp.pad(flat, [(0, 0)] * lead + [(0, rows_p * width - n)])
        parts.append(flat.reshape(lshape + (rows_p, width)))
        segs.append((r, n, shape))
        r += rows_p
    return jnp.concatenate(parts, axis=lead), segs


def _unpack(p, segs):
    lshape, width = p.shape[:-2], p.shape[-1]
    outs = []
    for r, n, shape in segs:
        rows = -(-n // width)
        blk = p[..., r:r + rows, :].reshape(lshape + (rows * width,))
        outs.append(blk[..., :n].reshape(lshape + shape))
    return outs


def _split_cols(a, f_off, h):
    return jnp.concatenate([a[..., :f_off], a[..., f_off + h:]], axis=-1), a[..., f_off:f_off + h]


def _merge_cols(main, f, f_off):
    return jnp.concatenate([main[..., :f_off], f, main[..., f_off:]], axis=-1)


def _pad_to(a, n, axis):
    pad = [(0, 0)] * a.ndim
    pad[axis] = (0, n - a.shape[axis])
    return jnp.pad(a, pad)


def kernel(x, c, w_ada, b_ada, w_in, b_in, conv_a_w, conv_a_b, ln_conv_g, ln_conv_b, w_conv_proj, w_attn_proj, w_mix_out, b_mix_out, ln1_g, ln1_b, w_ffn_up, ffn_conv_w, ffn_conv_b, w_ffn_down, ln2_g, ln2_b, loss_target, m_w_ada, m_b_ada, m_w_in, m_b_in, m_conv_a_w, m_conv_a_b, m_ln_conv_g, m_ln_conv_b, m_w_conv_proj, m_w_attn_proj, m_w_mix_out, m_b_mix_out, m_ln1_g, m_ln1_b, m_w_ffn_up, m_ffn_conv_w, m_ffn_conv_b, m_w_ffn_down, m_ln2_g, m_ln2_b, v_w_ada, v_b_ada, v_w_in, v_b_in, v_conv_a_w, v_conv_a_b, v_ln_conv_g, v_ln_conv_b, v_w_conv_proj, v_w_attn_proj, v_w_mix_out, v_b_mix_out, v_ln1_g, v_ln1_b, v_w_ffn_up, v_ffn_conv_w, v_ffn_conv_b, v_w_ffn_down, v_ln2_g, v_ln2_b):
    L, D = w_ada.shape[0], w_ada.shape[1]
    Bl, S, _ = x.shape
    C, KW, AW = conv_a_b.shape[1], conv_a_w.shape[1], w_attn_proj.shape[1]
    F, KF, n_in_all = ffn_conv_b.shape[1] // 2, ffn_conv_w.shape[1], b_in.shape[1]
    H = n_in_all - 2 * C - 3 * AW - 2 * D
    cfg = Cfg(L=L, Bl=Bl, S=S, D=D, C=C, KW=KW, H=H, Dh=AW // H, F=F, KF=KF)
    T, NM = cfg.T, cfg.NM
    f_off = 2 * C + 3 * AW
    n_ada = w_ada.shape[2]
    me = 4 * lax.axis_index("x") + 2 * lax.axis_index("y") + lax.axis_index("c")
    _, hp = _attn_tiles(cfg)

    def my_cols(a, n):
        return lax.dynamic_slice_in_dim(a, me * n, n, axis=a.ndim - 1)

    def big_list(w_in_, w_cp_, w_ap_, w_mo_, w_up_, w_dn_):
        out = []
        for l in range(L):
            out += [w_in_[l].T, w_cp_[l].T, w_ap_[l].T, w_mo_[l], w_up_[l].T, w_dn_[l]]
        return out

    wpack, wsegs = _pack(big_list(w_in, w_conv_proj, w_attn_proj, w_mix_out, w_ffn_up, w_ffn_down), D, BF16)
    wall = _exchange(wpack, all_to_all=False, name="gather_weights")
    spack, ssegs = _pack([c, conv_a_w, ffn_conv_w], D, F32)
    c_g, caw_g, fcw_g = _unpack(_exchange(spack, all_to_all=False, name="gather_small"), ssegs)
    c_all = c_g.reshape(N_DEV * Bl, D)
    caw = _pad_to(jnp.moveaxis(caw_g, 0, 2).reshape(L, KW, C), 32, 1)
    fcw = _pad_to(jnp.moveaxis(fcw_g, 0, 2).reshape(L, KF, 2 * F), 8, 1)

    wfull = _unpack(wall, wsegs)
    W = []
    for l in range(L):
        w_in_t, w_cp_t, w_ap_t, w_mo, w_up_t, w_dn = (a.reshape((-1, a.shape[-1])) for a in wfull[6 * l:6 * l + 6])
        wm_t, wf_t = _split_cols(w_in_t.T, f_off, H)
        bm, bf = _split_cols(b_in[l], f_off, H)
        W.append(dict(wm_t=wm_t.T, wf_t=_pad_to(wf_t.T, LANES, 0), bm=bm[None], bf=_pad_to(bf, LANES, 0)[None],
                      w_cp_t=w_cp_t, w_ap_t=w_ap_t, w_mo=w_mo, w_up_t=w_up_t, w_dn=w_dn))

    mod_part = _ada_fwd(c_all, w_ada, my_cols(b_ada, n_ada)[:, None, :], name="ada_fwd")
    mod_send = jnp.moveaxis(mod_part.reshape(L, N_DEV, Bl, n_ada), 1, 0).reshape(N_DEV, L * Bl, n_ada)
    mod_recv = _exchange(mod_send, all_to_all=True, name="exchange_mod")
    mod = jnp.moveaxis(mod_recv.reshape(N_DEV, L, Bl, n_ada), 0, 2).reshape(L, Bl, 6, 1, D)
    shift1, scale1, gate1, shift2, scale2, gate2 = (mod[:, :, i] for i in range(6))

    xf = x.reshape(T, D)
    u = _ln_mod_fwd(xf, shift1[0], scale1[0], cfg, name="ln_mod_fwd")
    saved = []
    xin = xf
    for l in range(L):
        w = W[l]
        zm = _matmul(u, w["wm_t"], mode="nt", bias=w["bm"], name=f"in_proj_{l}")
        zf = _matmul(u, w["wf_t"], mode="nt", bias=w["bf"], name=f"in_proj_f_{l}")
        a3 = _conv_a_fwd(zm, caw[l], conv_a_b[l][None], ln_conv_g[l][None], ln_conv_b[l][None], cfg, name=f"conv_a_fwd_{l}")
        cum = _fgate_fwd(zf, cfg, name=f"fgate_fwd_{l}")
        cum_t = jnp.swapaxes(cum.reshape(Bl, S, LANES)[:, :, :H], 1, 2).reshape(Bl, H // hp, hp, S)
        cum_t = _pad_to(cum_t, 8, 2)
        o, lse = _attn_fwd(zm, cum_t, cfg, name=f"attn_fwd_{l}")
        ya = _matmul(a3, w["w_cp_t"], mode="nt", name=f"conv_proj_{l}")
        yb = _matmul(o, w["w_ap_t"], mode="nt", name=f"attn_proj_{l}")
        mg = _merge_fwd(zm, ya, yb, cfg, name=f"merge_fwd_{l}")
        mix = _matmul(mg, w["w_mo"], mode="nn", bias=b_mix_out[l][None], name=f"mix_out_{l}")
        x1, u2 = _res_ln_fwd(xin, mix, gate1[l], ln1_g[l][None], ln1_b[l][None], cfg, name=f"res_ln1_fwd_{l}",
                             nxt=(shift2[l], scale2[l]))
        h0 = _matmul(u2, w["w_up_t"], mode="nt", name=f"ffn_up_{l}")
        fa = _ffn_conv_fwd(h0, fcw[l], ffn_conv_b[l][None], cfg, name=f"ffn_conv_fwd_{l}")
        ffn = _matmul(fa, w["w_dn"], mode="nn", name=f"ffn_down_{l}")
        saved.append(dict(x=xin, u=u, zm=zm, zf=zf, a3=a3, cum_t=cum_t, o=o, lse=lse, ya=ya, yb=yb, mg=mg, mix=mix,
                          x1=x1, u2=u2, h0=h0, fa=fa, ffn=ffn))
        if l + 1 < L:
            xin, u = _res_ln_fwd(x1, ffn, gate2[l], ln2_g[l][None], ln2_b[l][None], cfg, name=f"res_ln2_fwd_{l}",
                                 nxt=(shift1[l + 1], scale1[l + 1]))
        else:
            xin = _res_ln_fwd(x1, ffn, gate2[l], ln2_g[l][None], ln2_b[l][None], cfg, name=f"res_ln2_fwd_{l}")

    dx, loss_tiles = _loss_grad(xin, loss_target.reshape(T, D), cfg, name="loss_grad")
    loss = lax.psum(0.5 / D * jnp.sum(loss_tiles[:, 0, 0]), ("x", "y", "c"))

    gbig = [None] * (6 * L)
    gsm = [dict() for _ in range(L)]
    dmods = [None] * L
    for l in reversed(range(L)):
        w, s = W[l], saved[l]
        dres2, dffn, dg2, db2, dgate2, _ = _res_ln_bwd(dx, s["x1"], s["ffn"], gate2[l], ln2_g[l][None], cfg, name=f"res_ln2_bwd_{l}")
        dfa = _matmul(dffn, w["w_dn"], mode="nt", name=f"d_ffn_act_{l}")
        gbig[6 * l + 5] = _matmul(s["fa"], dffn, mode="tn", name=f"dw_ffn_down_{l}")
        dh0g, dh0l, dwg, dwl, dcg, dcl = _ffn_conv_bwd(dfa, s["h0"], fcw[l], ffn_conv_b[l][None], cfg, name=f"ffn_conv_bwd_{l}")
        dh0 = jnp.concatenate([dh0g, dh0l], axis=1)
        du2 = _matmul(dh0, w["w_up_t"], mode="nn", name=f"d_u2_{l}")
        gbig[6 * l + 4] = _matmul(dh0, s["u2"], mode="tn", name=f"dw_ffn_up_{l}")
        dx1, dscale2, dshift2 = _ln_mod_bwd(du2, s["x1"], scale2[l], dres2, cfg, name=f"ln_mod2_bwd_{l}")
        dres1, dmix, dg1, db1, dgate1, dbmo = _res_ln_bwd(dx1, s["x"], s["mix"], gate1[l], ln1_g[l][None], cfg, name=f"res_ln1_bwd_{l}")
        dmg = _matmul(dmix, w["w_mo"], mode="nt", name=f"d_merge_{l}")
        gbig[6 * l + 3] = _matmul(s["mg"], dmix, mode="tn", name=f"dw_mix_out_{l}")
        dya, dyb, dzga, dzgb = _merge_bwd(dmg, s["zm"], s["ya"], s["yb"], cfg, name=f"merge_bwd_{l}")
        gbig[6 * l + 1] = _matmul(dya, s["a3"], mode="tn", name=f"dw_conv_proj_{l}")
        da3 = _matmul(dya, w["w_cp_t"], mode="nn", name=f"d_a3_{l}")
        gbig[6 * l + 2] = _matmul(dyb, s["o"], mode="tn", name=f"dw_attn_proj_{l}")
        do = _matmul(dyb, w["w_ap_t"], mode="nn", out_dtype=BF16, name=f"d_o_{l}")
        dq, dk, dv, dct = _attn_bwd(s["zm"], s["cum_t"], s["o"], do, s["lse"], cfg, name=f"attn_bwd_{l}")
        dcum = _pad_to(jnp.swapaxes(dct[:, :, :hp, :].reshape(Bl, H, S), 1, 2), LANES, 2).reshape(T, LANES)
        dzf = _fgate_bwd(dcum, s["zf"], cfg, name=f"fgate_bwd_{l}")
        dzglu, dcaw, dcab, dlcg, dlcb = _conv_a_bwd(da3, s["zm"], caw[l], conv_a_b[l][None], ln_conv_g[l][None],
                                                    ln_conv_b[l][None], cfg, name=f"conv_a_bwd_{l}")
        dzm = jnp.concatenate([dzglu, dq, dk, dv, dzga, dzgb], axis=1)
        du1 = _matmul(dzf, w["wf_t"], mode="nn", name=f"d_u1_f_{l}")
        du1 = _matmul(dzm, w["wm_t"], mode="nn", add=du1, name=f"d_u1_{l}")
        dwm_t = _matmul(dzm, s["u"], mode="tn", name=f"dw_in_{l}")
        dwf_t = _matmul(dzf, s["u"], mode="tn", name=f"dw_in_f_{l}")
        gbig[6 * l] = _merge_cols(dwm_t.T, dwf_t[:H].T, f_off).T
        dbm, dbf = _colsum(dzm, name=f"db_in_{l}"), _colsum(dzf, name=f"db_in_f_{l}")
        dx, dscale1, dshift1 = _ln_mod_bwd(du1, s["x"], scale1[l], dres1, cfg, name=f"ln_mod1_bwd_{l}")
        dmods[l] = jnp.concatenate([dshift1, dscale1, dgate1, dshift2, dscale2, dgate2], axis=1).reshape(Bl, 6 * D)
        gsm[l] = dict(b_in=_merge_cols(dbm[0], dbf[0, :H], f_off), conv_a_b=dcab[0], ln_conv_g=dlcg[0], ln_conv_b=dlcb[0],
                      b_mix_out=dbmo[0], ln1_g=dg1[0], ln1_b=db1[0], ffn_conv_b=jnp.concatenate([dcg[0], dcl[0]]),
                      ln2_g=dg2[0], ln2_b=db2[0], conv_a_w=dcaw[:KW], ffn_conv_w=jnp.concatenate([dwg[:KF], dwl[:KF]], axis=1))
    grad_x = dx.reshape(Bl, S, D)

    small_names = ["b_in", "conv_a_b", "ln_conv_g", "ln_conv_b", "b_mix_out", "ln1_g", "ln1_b", "ffn_conv_b", "ln2_g", "ln2_b",
                   "conv_a_w", "ffn_conv_w"]
    gs_list = [jnp.stack(dmods)] + [jnp.stack([gsm[l][n] for l in range(L)]) for n in small_names]
    gspack, gssegs = _pack(gs_list, D, F32)
    gs_all = _exchange(gspack, all_to_all=False, name="gather_small_grads")
    dmod_all = jnp.moveaxis(_unpack(gs_all, gssegs)[0], 0, 1).reshape(L, N_DEV * Bl, 6 * D)
    g_small = dict(zip(small_names, _unpack(_slot_sum(gs_all, name="sum_small_grads"), gssegs)[1:]))
    g_small["conv_a_w"] = my_cols(g_small["conv_a_w"], C // N_DEV)
    g_small["ffn_conv_w"] = my_cols(g_small["ffn_conv_w"], 2 * F // N_DEV)
    g_small["w_ada"] = _ada_bwd(c_all, my_cols(dmod_all, n_ada), name="ada_bwd")
    g_small["b_ada"] = jnp.stack([_colsum(dmod_all[l], name=f"db_ada_{l}")[0] for l in range(L)])

    gsend, gsegs = _pack([g.reshape((N_DEV, g.shape[0] // N_DEV, g.shape[1])) for g in gbig], D, BF16, lead=1)
    grecv = _exchange(gsend, all_to_all=True, name="exchange_grads")
    big_names = ["w_in", "w_conv_proj", "w_attn_proj", "w_mix_out", "w_ffn_up", "w_ffn_down"]
    given = dict(w_in=(w_in, m_w_in, v_w_in), w_conv_proj=(w_conv_proj, m_w_conv_proj, v_w_conv_proj),
                 w_attn_proj=(w_attn_proj, m_w_attn_proj, v_w_attn_proj), w_mix_out=(w_mix_out, m_w_mix_out, v_w_mix_out),
                 w_ffn_up=(w_ffn_up, m_w_ffn_up, v_w_ffn_up), w_ffn_down=(w_ffn_down, m_w_ffn_down, v_w_ffn_down),
                 w_ada=(w_ada, m_w_ada, v_w_ada), b_ada=(b_ada, m_b_ada, v_b_ada), b_in=(b_in, m_b_in, v_b_in),
                 conv_a_w=(conv_a_w, m_conv_a_w, v_conv_a_w), conv_a_b=(conv_a_b, m_conv_a_b, v_conv_a_b),
                 ln_conv_g=(ln_conv_g, m_ln_conv_g, v_ln_conv_g), ln_conv_b=(ln_conv_b, m_ln_conv_b, v_ln_conv_b),
                 b_mix_out=(b_mix_out, m_b_mix_out, v_b_mix_out), ln1_g=(ln1_g, m_ln1_g, v_ln1_g), ln1_b=(ln1_b, m_ln1_b, v_ln1_b),
                 ffn_conv_w=(ffn_conv_w, m_ffn_conv_w, v_ffn_conv_w), ffn_conv_b=(ffn_conv_b, m_ffn_conv_b, v_ffn_conv_b),
                 ln2_g=(ln2_g, m_ln2_g, v_ln2_g), ln2_b=(ln2_b, m_ln2_b, v_ln2_b))
    res = {}
    wmv = [_pack(big_list(*(given[n][i] for n in big_names)), D, F32)[0] for i in range(3)]
    outs = _adamw(grecv, *wmv, name="adamw_big")
    transposed = (True, True, True, False, True, False)
    for kind, packed in zip(("grad", "delta", "new_m", "new_v"), outs):
        parts = _unpack(packed, gsegs)
        for i, n in enumerate(big_names):
            per_l = [parts[6 * l + i].T if transposed[i] else parts[6 * l + i] for l in range(L)]
            res[kind, n] = jnp.stack(per_l)
    loc_names = ["w_ada", "b_ada"] + small_names
    packs = [_pack([g_small[n] for n in loc_names], D, F32)] + [_pack([given[n][i] for n in loc_names], D, F32) for i in range(3)]
    outs = _adamw(packs[0][0][None], packs[1][0], packs[2][0], packs[3][0], name="adamw_small")
    for kind, packed in zip(("grad", "delta", "new_m", "new_v"), outs):
        for n, a in zip(loc_names, _unpack(packed, packs[0][1])):
            res[kind, n] = a

    order = ["w_ada", "b_ada", "w_in", "b_in", "conv_a_w", "conv_a_b", "ln_conv_g", "ln_conv_b", "w_conv_proj", "w_attn_proj",
             "w_mix_out", "b_mix_out", "ln1_g", "ln1_b", "w_ffn_up", "ffn_conv_w", "ffn_conv_b", "w_ffn_down", "ln2_g", "ln2_b"]
    return (loss, grad_x, *[res[k, n] for k in ("grad", "delta", "new_m", "new_v") for n in order])
```

```python
import functools
import math
from typing import NamedTuple

import jax
import jax.numpy as jnp
from jax import lax
from jax.experimental import pallas as pl
from jax.experimental.pallas import tpu as pltpu

F32, BF16 = jnp.float32, jnp.bfloat16
LN_EPS = 1e-5
ADAM_LR, ADAM_B1, ADAM_B2, ADAM_EPS, ADAM_WD, ADAM_STEP = 0.001, 0.9, 0.999, 1e-08, 0.01, 10
N_DEV = 8
LANES = 128
VMEM_LIMIT = 56 * 1024 * 1024
NEG = -1e30
NT = (((1,), (1,)), ((), ()))
TN = (((0,), (0,)), ((), ()))


class Cfg(NamedTuple):
    L: int
    Bl: int
    S: int
    D: int
    C: int
    KW: int
    H: int
    Dh: int
    F: int
    KF: int

    @property
    def T(self): return self.Bl * self.S
    @property
    def AW(self): return self.H * self.Dh
    @property
    def NM(self): return 2 * self.C + 3 * self.AW + 2 * self.D
    @property
    def q_off(self): return 2 * self.C
    @property
    def g_off(self): return 2 * self.C + 3 * self.AW
    @property
    def alpha(self): return (2.0 * self.L) ** 0.25


def _pcall(body, **kw):
    return pl.pallas_call(body, **kw)


def _params(*sem):
    return pltpu.CompilerParams(dimension_semantics=sem, vmem_limit_bytes=VMEM_LIMIT)


def _pick(n, prefs):
    for p in prefs:
        if n % p == 0:
            return p
    return n


def _sigmoid(x):
    return 1.0 / (1.0 + jnp.exp(-x))


def _ln_stats(x):
    mu = jnp.mean(x, axis=-1, keepdims=True)
    xc = x - mu
    var = jnp.mean(xc * xc, axis=-1, keepdims=True)
    rstd = lax.rsqrt(var + LN_EPS)
    return xc * rstd, rstd


def _ln_bwd(dxh, xh, rstd):
    return rstd * (dxh - jnp.mean(dxh, axis=-1, keepdims=True) - xh * jnp.mean(dxh * xh, axis=-1, keepdims=True))


def _matmul(a, b, *, mode, name, bias=None, add=None, out_dtype=F32, tm=None, tn=None, tk=None):
    if mode == "tn":
        K, M = a.shape
    else:
        M, K = a.shape
    N = b.shape[0] if mode == "nt" else b.shape[1]
    lane_tiles = (1536, 1408, 1024, 768, 512, 256, 128)
    tm = tm or _pick(M, lane_tiles if mode == "tn" else (1024, 512, 256, 128, 64, 32, 16, 8))
    tn = tn or _pick(N, lane_tiles)
    tk = tk or _pick(K, (512, 256, 128) if mode == "tn" else lane_tiles)
    nk = K // tk
    dn = {"nn": (((1,), (0,)), ((), ())), "nt": NT, "tn": TN}[mode]
    has_bias, has_add = bias is not None, add is not None

    def body(*refs):
        a_ref, b_ref = refs[0], refs[1]
        pos = 2
        bias_ref = refs[pos] if has_bias else None
        pos += has_bias
        add_ref = refs[pos] if has_add else None
        pos += has_add
        o_ref = refs[pos]
        part = lax.dot_general(a_ref[...], b_ref[...], dn, preferred_element_type=F32)

        def finish(acc):
            if has_bias:
                acc = acc + bias_ref[...]
            if has_add:
                acc = acc + add_ref[...]
            o_ref[...] = acc.astype(out_dtype)

        if nk == 1:
            finish(part)
        else:
            acc_ref = refs[pos + 1]
            k = pl.program_id(2)

            @pl.when(k == 0)
            def _():
                acc_ref[...] = part

            @pl.when(k > 0)
            def _():
                acc_ref[...] += part

            @pl.when(k == nk - 1)
            def _():
                finish(acc_ref[...])

    a_spec = pl.BlockSpec((tk, tm), lambda i, j, k: (k, i)) if mode == "tn" else pl.BlockSpec((tm, tk), lambda i, j, k: (i, k))
    b_spec = pl.BlockSpec((tn, tk), lambda i, j, k: (j, k)) if mode == "nt" else pl.BlockSpec((tk, tn), lambda i, j, k: (k, j))
    in_specs, args = [a_spec, b_spec], [a, b]
    if has_bias:
        in_specs.append(pl.BlockSpec((1, tn), lambda i, j, k: (0, j)))
        args.append(bias)
    if has_add:
        in_specs.append(pl.BlockSpec((tm, tn), lambda i, j, k: (i, j)))
        args.append(add)
    return _pcall(
        body, name=name, grid=(M // tm, N // tn, nk), in_specs=in_specs,
        out_specs=pl.BlockSpec((tm, tn), lambda i, j, k: (i, j)),
        out_shape=jax.ShapeDtypeStruct((M, N), out_dtype),
        scratch_shapes=[pltpu.VMEM((tm, tn), F32)] if nk > 1 else [],
        compiler_params=_params("parallel", "parallel", "arbitrary"),
    )(*args)


def _colsum(x, *, name):
    T, N = x.shape
    tr = _pick(T, (512, 256, 128, 64, 32, 16))
    tc = _pick(N, (1536, 1024, 512, 256, 128))

    def body(x_ref, o_ref):
        @pl.when(pl.program_id(1) == 0)
        def _():
            o_ref[...] = jnp.zeros_like(o_ref)

        o_ref[...] += jnp.sum(x_ref[...].astype(F32), axis=0, keepdims=True)

    return _pcall(body, name=name, grid=(N // tc, T // tr), in_specs=[pl.BlockSpec((tr, tc), lambda j, i: (i, j))],
                  out_specs=pl.BlockSpec((1, tc), lambda j, i: (0, j)), out_shape=jax.ShapeDtypeStruct((1, N), F32),
                  compiler_params=_params("parallel", "arbitrary"))(x)


def _row_tile(cfg):
    return _pick(cfg.S, (256, 128, 64, 32, 16, 8))


def _ln_mod_fwd(x, shift, scale, cfg, *, name):
    tr = _row_tile(cfg)
    tpb = cfg.S // tr

    def body(x_ref, sh_ref, sc_ref, u_ref):
        xh, _ = _ln_stats(x_ref[...])
        u_ref[...] = (xh * (1.0 + sc_ref[0]) + sh_ref[0]).astype(BF16)

    row = pl.BlockSpec((tr, cfg.D), lambda i: (i, 0))
    per_b = pl.BlockSpec((1, 1, cfg.D), lambda i: (i // tpb, 0, 0))
    return _pcall(body, name=name, grid=(cfg.T // tr,), in_specs=[row, per_b, per_b], out_specs=row,
                  out_shape=jax.ShapeDtypeStruct((cfg.T, cfg.D), BF16), compiler_params=_params("parallel"))(x, shift, scale)


def _res_ln_fwd(xin, br, gate, g, b, cfg, *, name, nxt=None):
    tr = _row_tile(cfg)
    tpb = cfg.S // tr
    alpha = cfg.alpha

    def body(*refs):
        x_ref, br_ref, gt_ref, g_ref, b_ref = refs[:5]
        r = alpha * x_ref[...] + (1.0 + gt_ref[0]) * br_ref[...]
        xh, _ = _ln_stats(r)
        xo = xh * g_ref[...] + b_ref[...]
        if nxt is None:
            refs[5][...] = xo
        else:
            sh_ref, sc_ref, xo_ref, u_ref = refs[5:]
            xo_ref[...] = xo
            uh, _ = _ln_stats(xo)
            u_ref[...] = (uh * (1.0 + sc_ref[0]) + sh_ref[0]).astype(BF16)

    row = pl.BlockSpec((tr, cfg.D), lambda i: (i, 0))
    per_b = pl.BlockSpec((1, 1, cfg.D), lambda i: (i // tpb, 0, 0))
    vec = pl.BlockSpec((1, cfg.D), lambda i: (0, 0))
    in_specs, args = [row, row, per_b, vec, vec], [xin, br, gate, g, b]
    out_specs, out_shape = row, jax.ShapeDtypeStruct((cfg.T, cfg.D), F32)
    if nxt is not None:
        in_specs += [per_b, per_b]
        args += list(nxt)
        out_specs = [row, row]
        out_shape = [out_shape, jax.ShapeDtypeStruct((cfg.T, cfg.D), BF16)]
    return _pcall(body, name=name, grid=(cfg.T // tr,), in_specs=in_specs, out_specs=out_specs, out_shape=out_shape,
                  compiler_params=_params("parallel"))(*args)


def _loss_grad(y, tgt, cfg, *, name):
    tr = _row_tile(cfg)
    nt = cfg.T // tr
    inv_d = 1.0 / cfg.D

    def body(y_ref, t_ref, dy_ref, ls_ref):
        e = y_ref[...] - t_ref[...]
        dy_ref[...] = e * inv_d
        ls_ref[...] = jnp.full((1, 1, LANES), jnp.sum(e * e), F32)

    row = pl.BlockSpec((tr, cfg.D), lambda i: (i, 0))
    return _pcall(body, name=name, grid=(nt,), in_specs=[row, row],
                  out_specs=[row, pl.BlockSpec((1, 1, LANES), lambda i: (i, 0, 0))],
                  out_shape=[jax.ShapeDtypeStruct((cfg.T, cfg.D), F32), jax.ShapeDtypeStruct((nt, 1, LANES), F32)],
                  compiler_params=_params("parallel"))(y, tgt)


def _res_ln_bwd(dy, xin, br, gate, g, cfg, *, name):
    tr = _row_tile(cfg)
    tpb = cfg.S // tr
    alpha = cfg.alpha

    def body(dy_ref, x_ref, br_ref, gt_ref, g_ref, dx_ref, dbr_ref, dg_ref, db_ref, dgt_ref, dbs_ref):
        i = pl.program_id(0)

        @pl.when(i == 0)
        def _():
            dg_ref[...] = jnp.zeros_like(dg_ref)
            db_ref[...] = jnp.zeros_like(db_ref)
            dbs_ref[...] = jnp.zeros_like(dbs_ref)

        @pl.when(i % tpb == 0)
        def _():
            dgt_ref[...] = jnp.zeros_like(dgt_ref)

        dy, brv, one_gate = dy_ref[...], br_ref[...], 1.0 + gt_ref[0]
        xh, rstd = _ln_stats(alpha * x_ref[...] + one_gate * brv)
        dg_ref[...] += jnp.sum(dy * xh, axis=0, keepdims=True)
        db_ref[...] += jnp.sum(dy, axis=0, keepdims=True)
        dr = _ln_bwd(dy * g_ref[...], xh, rstd)
        dx_ref[...] = alpha * dr
        dbr = one_gate * dr
        dbr_ref[...] = dbr.astype(BF16)
        dbs_ref[...] += jnp.sum(dbr, axis=0, keepdims=True)
        dgt_ref[0] += jnp.sum(dr * brv, axis=0, keepdims=True)

    row = pl.BlockSpec((tr, cfg.D), lambda i: (i, 0))
    per_b = pl.BlockSpec((1, 1, cfg.D), lambda i: (i // tpb, 0, 0))
    vec = pl.BlockSpec((1, cfg.D), lambda i: (0, 0))
    vs = jax.ShapeDtypeStruct((1, cfg.D), F32)
    return _pcall(body, name=name, grid=(cfg.T // tr,), in_specs=[row, row, row, per_b, vec],
                  out_specs=[row, row, vec, vec, per_b, vec],
                  out_shape=[jax.ShapeDtypeStruct((cfg.T, cfg.D), F32), jax.ShapeDtypeStruct((cfg.T, cfg.D), BF16), vs, vs,
                             jax.ShapeDtypeStruct((cfg.Bl, 1, cfg.D), F32), vs],
                  compiler_params=_params("arbitrary"))(dy, xin, br, gate, g)


def _ln_mod_bwd(du, xin, scale, dres, cfg, *, name):
    tr = _row_tile(cfg)
    tpb = cfg.S // tr

    def body(du_ref, x_ref, sc_ref, dres_ref, dx_ref, dsc_ref, dsh_ref):
        @pl.when(pl.program_id(0) % tpb == 0)
        def _():
            dsc_ref[...] = jnp.zeros_like(dsc_ref)
            dsh_ref[...] = jnp.zeros_like(dsh_ref)

        du = du_ref[...]
        xh, rstd = _ln_stats(x_ref[...])
        dsc_ref[0] += jnp.sum(du * xh, axis=0, keepdims=True)
        dsh_ref[0] += jnp.sum(du, axis=0, keepdims=True)
        dx_ref[...] = _ln_bwd(du * (1.0 + sc_ref[0]), xh, rstd) + dres_ref[...]

    row = pl.BlockSpec((tr, cfg.D), lambda i: (i, 0))
    per_b = pl.BlockSpec((1, 1, cfg.D), lambda i: (i // tpb, 0, 0))
    bs = jax.ShapeDtypeStruct((cfg.Bl, 1, cfg.D), F32)
    return _pcall(body, name=name, grid=(cfg.T // tr,), in_specs=[row, row, per_b, row], out_specs=[row, per_b, per_b],
                  out_shape=[jax.ShapeDtypeStruct((cfg.T, cfg.D), F32), bs, bs],
                  compiler_params=_params("arbitrary"))(du, xin, scale, dres)


def _merge_tiles(cfg):
    tr = _pick(cfg.T, (512, 256, 128, 64, 32, 16))
    tc = _pick(math.gcd(cfg.g_off, cfg.D), (512, 256, 128))
    return tr, tc


def _merge_fwd(zm, ya, yb, cfg, *, name):
    tr, tc = _merge_tiles(cfg)
    ga0, gb0 = cfg.g_off // tc, (cfg.g_off + cfg.D) // tc

    def body(ga_ref, gb_ref, ya_ref, yb_ref, m_ref):
        m_ref[...] = (_sigmoid(ga_ref[...]) * ya_ref[...] + _sigmoid(gb_ref[...]) * yb_ref[...]).astype(BF16)

    blk = pl.BlockSpec((tr, tc), lambda i, j: (i, j))
    return _pcall(body, name=name, grid=(cfg.T // tr, cfg.D // tc),
                  in_specs=[pl.BlockSpec((tr, tc), lambda i, j: (i, ga0 + j)), pl.BlockSpec((tr, tc), lambda i, j: (i, gb0 + j)), blk, blk],
                  out_specs=blk, out_shape=jax.ShapeDtypeStruct((cfg.T, cfg.D), BF16),
                  compiler_params=_params("parallel", "parallel"))(zm, zm, ya, yb)


def _merge_bwd(dm, zm, ya, yb, cfg, *, name):
    tr, tc = _merge_tiles(cfg)
    ga0, gb0 = cfg.g_off // tc, (cfg.g_off + cfg.D) // tc

    def body(dm_ref, ga_ref, gb_ref, ya_ref, yb_ref, dya_ref, dyb_ref, dga_ref, dgb_ref):
        dm = dm_ref[...]
        ga, gb = _sigmoid(ga_ref[...]), _sigmoid(gb_ref[...])
        dya_ref[...] = (dm * ga).astype(BF16)
        dyb_ref[...] = (dm * gb).astype(BF16)
        dga_ref[...] = (dm * ya_ref[...] * ga * (1.0 - ga)).astype(BF16)
        dgb_ref[...] = (dm * yb_ref[...] * gb * (1.0 - gb)).astype(BF16)

    blk = pl.BlockSpec((tr, tc), lambda i, j: (i, j))
    o = jax.ShapeDtypeStruct((cfg.T, cfg.D), BF16)
    return _pcall(body, name=name, grid=(cfg.T // tr, cfg.D // tc),
                  in_specs=[blk, pl.BlockSpec((tr, tc), lambda i, j: (i, ga0 + j)), pl.BlockSpec((tr, tc), lambda i, j: (i, gb0 + j)), blk, blk],
                  out_specs=[blk] * 4, out_shape=[o] * 4, compiler_params=_params("parallel", "parallel"))(dm, zm, zm, ya, yb)


CONV_A_HALO = 32
CONV_A_CHUNK = 32


def _conv_a_tile(cfg):
    assert cfg.KW - 1 <= CONV_A_HALO
    return _pick(cfg.S, (256, 128, 64, 32))


def _conv_a_fwd(zm, w, cb, g, b, cfg, *, name):
    C, KW, HALO, CH = cfg.C, cfg.KW, CONV_A_HALO, CONV_A_CHUNK
    ts = _conv_a_tile(cfg)
    tpb = cfg.S // ts
    lead = HALO - (KW - 1)

    def body(z_ref, zp_ref, w_ref, cb_ref, g_ref, b_ref, o_ref, a0_s):
        first = pl.program_id(0) % tpb == 0
        prev = zp_ref[:, :C] * _sigmoid(zp_ref[:, C:])
        a0_s[0:HALO, :] = jnp.where(first, 0.0, prev)
        a0_s[HALO:HALO + ts, :] = z_ref[:, :C] * _sigmoid(z_ref[:, C:])
        for r0 in range(0, ts, CH):
            acc = jnp.zeros((CH, C), F32)
            for k in range(KW):
                acc = acc + w_ref[k:k + 1, :] * a0_s[r0 + lead + k:r0 + lead + k + CH, :]
            xh, _ = _ln_stats(acc + cb_ref[...])
            a2 = xh * g_ref[...] + b_ref[...]
            o_ref[r0:r0 + CH, :] = (a2 * _sigmoid(a2)).astype(BF16)

    hb = ts // HALO
    vec = pl.BlockSpec((1, C), lambda i: (0, 0))
    return _pcall(body, name=name, grid=(cfg.T // ts,),
                  in_specs=[pl.BlockSpec((ts, 2 * C), lambda i: (i, 0)),
                            pl.BlockSpec((HALO, 2 * C), lambda i: (jnp.maximum(i * hb - 1, 0), 0)),
                            pl.BlockSpec((32, C), lambda i: (0, 0)), vec, vec, vec],
                  out_specs=pl.BlockSpec((ts, C), lambda i: (i, 0)), out_shape=jax.ShapeDtypeStruct((cfg.T, C), BF16),
                  scratch_shapes=[pltpu.VMEM((HALO + ts, C), F32)], compiler_params=_params("parallel"))(zm, zm, w, cb, g, b)


def _conv_a_bwd(da3, zm, w, cb, g, b, cfg, *, name):
    C, KW, HALO, CH = cfg.C, cfg.KW, CONV_A_HALO, CONV_A_CHUNK
    ts = _conv_a_tile(cfg)
    tpb = cfg.S // ts
    nt = cfg.T // ts
    lead = HALO - (KW - 1)
    ext = ts + HALO

    def body(z_ref, zp_ref, zn_ref, d_ref, dn_ref, w_ref, cb_ref, g_ref, b_ref,
             dz_ref, dw_ref, dcb_ref, dg_ref, db_ref, a0_s, d3_s, da1_s):
        i = pl.program_id(0)
        first, last = i % tpb == 0, i % tpb == tpb - 1

        @pl.when(i == 0)
        def _():
            dw_ref[...] = jnp.zeros_like(dw_ref)
            dcb_ref[...] = jnp.zeros_like(dcb_ref)
            dg_ref[...] = jnp.zeros_like(dg_ref)
            db_ref[...] = jnp.zeros_like(db_ref)

        a0_s[0:HALO, :] = jnp.where(first, 0.0, zp_ref[:, :C] * _sigmoid(zp_ref[:, C:]))
        a0_s[HALO:HALO + ts, :] = z_ref[:, :C] * _sigmoid(z_ref[:, C:])
        a0_s[HALO + ts:HALO + ext, :] = zn_ref[:, :C] * _sigmoid(zn_ref[:, C:])
        d3_s[0:ts, :] = d_ref[...]
        d3_s[ts:ext, :] = jnp.where(last, 0.0, dn_ref[...])
        dcb, dg, db = jnp.zeros((1, C), F32), jnp.zeros((1, C), F32), jnp.zeros((1, C), F32)
        for r0 in range(0, ext, CH):
            acc = jnp.zeros((CH, C), F32)
            for k in range(KW):
                acc = acc + w_ref[k:k + 1, :] * a0_s[r0 + lead + k:r0 + lead + k + CH, :]
            xh, rstd = _ln_stats(acc + cb_ref[...])
            a2 = xh * g_ref[...] + b_ref[...]
            sg = _sigmoid(a2)
            da2 = d3_s[r0:r0 + CH, :] * (sg * (1.0 + a2 * (1.0 - sg)))
            da1 = _ln_bwd(da2 * g_ref[...], xh, rstd)
            da1_s[r0:r0 + CH, :] = da1
            if r0 < ts:
                dg = dg + jnp.sum(da2 * xh, axis=0, keepdims=True)
                db = db + jnp.sum(da2, axis=0, keepdims=True)
                dcb = dcb + jnp.sum(da1, axis=0, keepdims=True)
        dg_ref[...] += dg
        db_ref[...] += db
        dcb_ref[...] += dcb
        for k in range(KW):
            dwk = jnp.zeros((1, C), F32)
            for r0 in range(0, ts, CH):
                dwk = dwk + jnp.sum(da1_s[r0:r0 + CH, :] * a0_s[r0 + lead + k:r0 + lead + k + CH, :], axis=0, keepdims=True)
            dw_ref[k:k + 1, :] += dwk
        for r0 in range(0, ts, CH):
            da0 = jnp.zeros((CH, C), F32)
            for k in range(KW):
                da0 = da0 + w_ref[k:k + 1, :] * da1_s[r0 + KW - 1 - k:r0 + KW - 1 - k + CH, :]
            val, sg = z_ref[r0:r0 + CH, :C], _sigmoid(z_ref[r0:r0 + CH, C:])
            dz_ref[r0:r0 + CH, :C] = (da0 * sg).astype(BF16)
            dz_ref[r0:r0 + CH, C:] = (da0 * val * sg * (1.0 - sg)).astype(BF16)

    hb = ts // HALO
    nhb = cfg.T // HALO
    vec = pl.BlockSpec((1, C), lambda i: (0, 0))
    vs = jax.ShapeDtypeStruct((1, C), F32)
    return _pcall(body, name=name, grid=(nt,),
                  in_specs=[pl.BlockSpec((ts, 2 * C), lambda i: (i, 0)),
                            pl.BlockSpec((HALO, 2 * C), lambda i: (jnp.maximum(i * hb - 1, 0), 0)),
                            pl.BlockSpec((HALO, 2 * C), lambda i: (jnp.minimum((i + 1) * hb, nhb - 1), 0)),
                            pl.BlockSpec((ts, C), lambda i: (i, 0)),
                            pl.BlockSpec((HALO, C), lambda i: (jnp.minimum((i + 1) * hb, nhb - 1), 0)),
                            pl.BlockSpec((32, C), lambda i: (0, 0)), vec, vec, vec],
                  out_specs=[pl.BlockSpec((ts, 2 * C), lambda i: (i, 0)), pl.BlockSpec((32, C), lambda i: (0, 0)), vec, vec, vec],
                  out_shape=[jax.ShapeDtypeStruct((cfg.T, 2 * C), BF16), jax.ShapeDtypeStruct((32, C), F32), vs, vs, vs],
                  scratch_shapes=[pltpu.VMEM((HALO + ext, C), F32), pltpu.VMEM((ext, C), F32), pltpu.VMEM((ext, C), F32)],
                  compiler_params=_params("arbitrary"))(zm, zm, zm, da3, da3, w, cb, g, b)


def _cum_tile(cfg):
    return _pick(cfg.S, (256, 128, 64, 32, 16, 8))


def _fgate_fwd(zf, cfg, *, name):
    tc = _cum_tile(cfg)
    tpb = cfg.S // tc
    _, hp = _attn_tiles(cfg)
    nb = cfg.H // hp

    def body(z_ref, o_ref, carry):
        @pl.when(pl.program_id(0) % tpb == 0)
        def _():
            carry[...] = jnp.zeros_like(carry)

        z = z_ref[...]
        logf = jnp.minimum(z, 0.0) - jnp.log(1.0 + jnp.exp(-jnp.abs(z)))
        tri = (lax.broadcasted_iota(jnp.int32, (tc, tc), 0) >= lax.broadcasted_iota(jnp.int32, (tc, tc), 1)).astype(F32)
        cum = jnp.dot(tri, logf, precision=lax.Precision.HIGHEST, preferred_element_type=F32) + carry[...]
        carry[...] = cum[tc - 1:tc, :]
        o_ref[0] = cum
        for b in range(1, nb):
            o_ref[b] = pltpu.roll(cum, LANES - hp * b, axis=1)

    return _pcall(body, name=name, grid=(cfg.T // tc,), in_specs=[pl.BlockSpec((tc, LANES), lambda i: (i, 0))],
                  out_specs=pl.BlockSpec((nb, tc, LANES), lambda i: (0, i, 0)),
                  out_shape=jax.ShapeDtypeStruct((nb, cfg.T, LANES), F32), scratch_shapes=[pltpu.VMEM((1, LANES), F32)],
                  compiler_params=_params("arbitrary"))(zf)


def _fgate_bwd(dcum_c, zf, cfg, *, name):
    tc = _cum_tile(cfg)
    tpb = cfg.S // tc
    nt = cfg.T // tc
    _, hp = _attn_tiles(cfg)
    nb = cfg.H // hp

    def body(d_ref, z_ref, o_ref, carry):
        @pl.when(pl.program_id(0) % tpb == 0)
        def _():
            carry[...] = jnp.zeros_like(carry)

        d = d_ref[0]
        for b in range(1, nb):
            d = d + pltpu.roll(d_ref[b], hp * b, axis=1)
        tri = (lax.broadcasted_iota(jnp.int32, (tc, tc), 0) <= lax.broadcasted_iota(jnp.int32, (tc, tc), 1)).astype(F32)
        suf = jnp.dot(tri, d, precision=lax.Precision.HIGHEST, preferred_element_type=F32) + carry[...]
        o_ref[...] = (suf * _sigmoid(-z_ref[...])).astype(BF16)
        carry[...] = suf[0:1, :]

    blk = pl.BlockSpec((tc, LANES), lambda i: (nt - 1 - i, 0))
    return _pcall(body, name=name, grid=(nt,), in_specs=[pl.BlockSpec((nb, tc, LANES), lambda i: (0, nt - 1 - i, 0)), blk],
                  out_specs=blk, out_shape=jax.ShapeDtypeStruct((cfg.T, LANES), BF16),
                  scratch_shapes=[pltpu.VMEM((1, LANES), F32)], compiler_params=_params("arbitrary"))(dcum_c, zf)


def _attn_tiles(cfg):
    assert LANES % cfg.Dh == 0 and cfg.H % (LANES // cfg.Dh) == 0
    t = _pick(cfg.S, (256, 128))
    return t, LANES // cfg.Dh


BIAS_LANES = 3


def _head_lanes(hd, cfg, hp):
    li = lax.broadcasted_iota(jnp.int32, (1, LANES), 1)
    own = (li >= hd * cfg.Dh) & (li < (hd + 1) * cfg.Dh)
    return own, li, ((hd + 1) % hp) * cfg.Dh


def _q_aug(q, hd, cfg, hp):
    own, li, b0 = _head_lanes(hd, cfg, hp)
    ones = ((li >= b0) & (li < b0 + BIAS_LANES)).astype(F32)
    return jnp.where(own, q * cfg.Dh ** -0.5, ones).astype(BF16)


def _k_aug(k, ck, hd, cfg, hp):
    own, li, b0 = _head_lanes(hd, cfg, hp)
    hi = ck.astype(BF16).astype(F32)
    mid = (ck - hi).astype(BF16).astype(F32)
    lo = ck - hi - mid
    bias = jnp.where(li == b0, -hi, jnp.where(li == b0 + 1, -mid, jnp.where(li == b0 + 2, -lo, 0.0)))
    return jnp.where(own, k, bias).astype(BF16)


def _attn_fwd(zm, cum_c, cfg, *, name):
    S, Dh = cfg.S, cfg.Dh
    t, hp = _attn_tiles(cfg)
    assert hp >= 2
    nq, nb = S // t, cfg.H // hp
    qb, kb, vb = cfg.q_off // LANES, (cfg.q_off + cfg.AW) // LANES, (cfg.q_off + 2 * cfg.AW) // LANES

    def body(q_ref, k_ref, v_ref, cc_ref, o_ref, o32_ref, lse_ref, ka_s, vt_s):
        qi = pl.program_id(2)

        @pl.when(qi == 0)
        def _():
            def prep(c, _):
                r = pl.multiple_of(c * t, t)
                kc = k_ref[pl.ds(r, t), :]
                for hd in range(hp):
                    ka_s[hd, pl.ds(r, t), :] = _k_aug(kc, cc_ref[0, pl.ds(r, t), hd:hd + 1], hd, cfg, hp)
                vt_s[:, pl.ds(r, t)] = v_ref[pl.ds(r, t), :].T.astype(BF16)
                return 0

            lax.fori_loop(0, nq, prep, 0)

        causal = lax.broadcasted_iota(jnp.int32, (t, t), 0) <= lax.broadcasted_iota(jnp.int32, (t, t), 1)
        qf = q_ref[...]
        qa = [_q_aug(qf, hd, cfg, hp) for hd in range(hp)]

        def chunk(j, carry, masked):
            r = pl.multiple_of(j * t, t)
            new = []
            for hd in range(hp):
                m, l, acc = carry[hd]
                s = lax.dot_general(ka_s[hd, pl.ds(r, t), :], qa[hd], NT, preferred_element_type=F32)
                if masked:
                    s = jnp.where(causal, s, NEG)
                m_new = jnp.maximum(m, jnp.max(s, axis=0, keepdims=True))
                a = jnp.exp(m - m_new)
                p = jnp.exp(s - m_new)
                l = a * l + jnp.sum(p, axis=0, keepdims=True)
                p_hi = p.astype(BF16)
                p_lo = (p - p_hi.astype(F32)).astype(BF16)
                vt = vt_s[hd * Dh:(hd + 1) * Dh, pl.ds(r, t)]
                acc = a * acc + (jnp.dot(vt, p_hi, preferred_element_type=F32) + jnp.dot(vt, p_lo, preferred_element_type=F32))
                new.append((m_new, l, acc))
            return tuple(new)

        init = tuple((jnp.full((1, t), NEG, F32), jnp.zeros((1, t), F32), jnp.zeros((Dh, t), F32)) for _ in range(hp))
        res = chunk(qi, lax.fori_loop(0, qi, functools.partial(chunk, masked=False), init), True)
        o = jnp.concatenate([acc / l for _, l, acc in res], axis=0).T
        o_ref[...] = o.astype(BF16)
        o32_ref[...] = o
        lse_ref[...] = jnp.zeros_like(lse_ref)
        for hd in range(hp):
            lse_ref[0, 0, hd:hd + 1, :] = res[hd][0] + jnp.log(res[hd][1])

    return _pcall(body, name=name, grid=(cfg.Bl, nb, nq),
                  in_specs=[pl.BlockSpec((t, LANES), lambda b, h, i: (b * nq + i, qb + h)),
                            pl.BlockSpec((S, LANES), lambda b, h, i: (b, kb + h)),
                            pl.BlockSpec((S, LANES), lambda b, h, i: (b, vb + h)),
                            pl.BlockSpec((1, S, LANES), lambda b, h, i: (h, b, 0))],
                  out_specs=[pl.BlockSpec((t, LANES), lambda b, h, i: (b * nq + i, h)),
                             pl.BlockSpec((t, LANES), lambda b, h, i: (b * nq + i, h)),
                             pl.BlockSpec((1, 1, 8, t), lambda b, h, i: (b, h, 0, i))],
                  out_shape=[jax.ShapeDtypeStruct((cfg.T, cfg.AW), BF16), jax.ShapeDtypeStruct((cfg.T, cfg.AW), F32),
                             jax.ShapeDtypeStruct((cfg.Bl, nb, 8, S), F32)],
                  scratch_shapes=[pltpu.VMEM((hp, S, LANES), BF16), pltpu.VMEM((LANES, S), BF16)],
                  compiler_params=_params("parallel", "parallel", "arbitrary"))(zm, zm, zm, cum_c)


def _attn_bwd(zm, cum_c, o, do, lse, cfg, *, name):
    S, Dh = cfg.S, cfg.Dh
    t, hp = _attn_tiles(cfg)
    nq, nb = S // t, cfg.H // hp
    qb, kb, vb = cfg.q_off // LANES, (cfg.q_off + cfg.AW) // LANES, (cfg.q_off + 2 * cfg.AW) // LANES
    scale = Dh ** -0.5

    def body(q_ref, k_ref, v_ref, cc_ref, o_ref, do_ref, lse_ref, dq_ref, dk_ref, dv_ref, dcc_ref,
             ka_s, qa_s, vz_s, kt_s, dd_s, dqt_s):
        li = lax.broadcasted_iota(jnp.int32, (1, LANES), 1)
        ri = lax.broadcasted_iota(jnp.int32, (LANES, 1), 0)
        causal = lax.broadcasted_iota(jnp.int32, (t, t), 0) <= lax.broadcasted_iota(jnp.int32, (t, t), 1)

        def prep(c, _):
            r = pl.multiple_of(c * t, t)
            kc, vc, qc = k_ref[pl.ds(r, t), :], v_ref[pl.ds(r, t), :], q_ref[pl.ds(r, t), :]
            prod_t = (do_ref[pl.ds(r, t), :].astype(F32) * o_ref[pl.ds(r, t), :].astype(F32)).T
            for hd in range(hp):
                own = _head_lanes(hd, cfg, hp)[0]
                ka_s[hd, pl.ds(r, t), :] = _k_aug(kc, cc_ref[0, pl.ds(r, t), hd:hd + 1], hd, cfg, hp)
                qa_s[hd, pl.ds(r, t), :] = _q_aug(qc, hd, cfg, hp)
                vz_s[hd, pl.ds(r, t), :] = jnp.where(own, vc, 0.0).astype(BF16)
                dd_s[hd:hd + 1, pl.ds(r, t)] = jnp.sum(prod_t[hd * Dh:(hd + 1) * Dh, :], axis=0, keepdims=True)
            kt_s[:, pl.ds(r, t)] = kc.T.astype(BF16)
            dqt_s[:, pl.ds(r, t)] = jnp.zeros((LANES, t), F32)
            return 0

        lax.fori_loop(0, nq, prep, 0)

        def kv_step(j, _):
            rk = pl.multiple_of(j * t, t)

            def tile(i, carry, masked):
                rq = pl.multiple_of(i * t, t)
                dob = do_ref[pl.ds(rq, t), :]
                new, dq_t = [], None
                for hd in range(hp):
                    dk_h, dv_h, dsum_h = carry[hd]
                    qa = qa_s[hd, pl.ds(rq, t), :]
                    s = lax.dot_general(ka_s[hd, pl.ds(rk, t), :], qa, NT, preferred_element_type=F32)
                    p = jnp.exp(s - lse_ref[0, 0, hd:hd + 1, pl.ds(rq, t)])
                    if masked:
                        p = jnp.where(causal, p, 0.0)
                    dp = lax.dot_general(vz_s[hd, pl.ds(rk, t), :], dob, NT, preferred_element_type=F32)
                    ds = p * (dp - dd_s[hd:hd + 1, pl.ds(rq, t)])
                    dsb = ds.astype(BF16)
                    dv_h = dv_h + jnp.dot(p.astype(BF16), dob, preferred_element_type=F32)
                    dk_h = dk_h + jnp.dot(dsb, qa, preferred_element_type=F32)
                    dq_h = jnp.dot(kt_s[:, pl.ds(rk, t)], dsb, preferred_element_type=F32)
                    dq_t = dq_h if hd == 0 else jnp.where((ri >= hd * Dh) & (ri < (hd + 1) * Dh), dq_h, dq_t)
                    for c0 in range(0, t, LANES):
                        dsum_h = dsum_h + ds[:, c0:c0 + LANES]
                    new.append((dk_h, dv_h, dsum_h))
                dqt_s[:, pl.ds(rq, t)] += dq_t * scale
                return tuple(new)

            zero = tuple((jnp.zeros((t, LANES), F32),) * 3 for _ in range(hp))
            res = lax.fori_loop(j + 1, nq, functools.partial(tile, masked=False), tile(j, zero, True))
            dk, dv, dcc = res[0][0], res[0][1], jnp.zeros((t, LANES), F32)
            for hd in range(hp):
                own = _head_lanes(hd, cfg, hp)[0]
                if hd > 0:
                    dk, dv = jnp.where(own, res[hd][0], dk), jnp.where(own, res[hd][1], dv)
                dcc = dcc + jnp.where(li == hd, -jnp.sum(res[hd][2], axis=1, keepdims=True), 0.0)
            dk_ref[pl.ds(rk, t), :] = dk.astype(BF16)
            dv_ref[pl.ds(rk, t), :] = dv.astype(BF16)
            dcc_ref[0, pl.ds(rk, t), :] = dcc
            return 0

        lax.fori_loop(0, nq, kv_step, 0)

        def finish(c, _):
            r = pl.multiple_of(c * t, t)
            dq_ref[pl.ds(r, t), :] = dqt_s[:, pl.ds(r, t)].T.astype(BF16)
            return 0

        lax.fori_loop(0, nq, finish, 0)

    blk = pl.BlockSpec((S, LANES), lambda b, h: (b, h))
    cc = pl.BlockSpec((1, S, LANES), lambda b, h: (h, b, 0))
    os_ = jax.ShapeDtypeStruct((cfg.T, cfg.AW), BF16)
    return _pcall(body, name=name, grid=(cfg.Bl, nb),
                  in_specs=[pl.BlockSpec((S, LANES), lambda b, h: (b, qb + h)), pl.BlockSpec((S, LANES), lambda b, h: (b, kb + h)),
                            pl.BlockSpec((S, LANES), lambda b, h: (b, vb + h)), cc, blk, blk,
                            pl.BlockSpec((1, 1, 8, S), lambda b, h: (b, h, 0, 0))],
                  out_specs=[blk, blk, blk, cc],
                  out_shape=[os_, os_, os_, jax.ShapeDtypeStruct((nb, cfg.T, LANES), F32)],
                  scratch_shapes=[pltpu.VMEM((hp, S, LANES), BF16)] * 3 + [pltpu.VMEM((LANES, S), BF16),
                                  pltpu.VMEM((8, S), F32), pltpu.VMEM((LANES, S), F32)],
                  compiler_params=_params("parallel", "parallel"))(zm, zm, zm, cum_c, o, do, lse)


FFN_HALO = 8


def _ffn_tiles(cfg):
    assert cfg.KF - 1 <= FFN_HALO
    return _pick(cfg.S, (512, 256, 128, 64, 32, 16, 8)), _pick(cfg.F, (256, 128))


def _gelu(x):
    return 0.5 * x * (1.0 + lax.erf(x * (2.0 ** -0.5)))


def _gelu_grad(x):
    return 0.5 * (1.0 + lax.erf(x * (2.0 ** -0.5))) + x * jnp.exp(-0.5 * x * x) * ((2.0 * math.pi) ** -0.5)


def _ffn_conv_fwd(h0, w, cb, cfg, *, name):
    KF, HALO = cfg.KF, FFN_HALO
    ts, tf = _ffn_tiles(cfg)
    tpb, nf = cfg.S // ts, cfg.F // tf
    lead = HALO - (KF - 1)

    def body(g_ref, gp_ref, l_ref, lp_ref, wg_ref, wl_ref, cg_ref, cl_ref, o_ref, g_s, l_s):
        first = pl.program_id(1) % tpb == 0
        g_s[0:HALO, :] = jnp.where(first, 0.0, gp_ref[...])
        l_s[0:HALO, :] = jnp.where(first, 0.0, lp_ref[...])
        g_s[HALO:HALO + ts, :] = g_ref[...]
        l_s[HALO:HALO + ts, :] = l_ref[...]
        hg, hl = cg_ref[...], cl_ref[...]
        for k in range(KF):
            hg = hg + wg_ref[k:k + 1, :] * g_s[lead + k:lead + k + ts, :]
            hl = hl + wl_ref[k:k + 1, :] * l_s[lead + k:lead + k + ts, :]
        o_ref[...] = (_gelu(hg) * hl).astype(BF16)

    hb = ts // HALO
    prev = lambda off: pl.BlockSpec((HALO, tf), lambda j, i: (jnp.maximum(i * hb - 1, 0), off + j))
    main = lambda off: pl.BlockSpec((ts, tf), lambda j, i: (i, off + j))
    wsp = lambda off: pl.BlockSpec((8, tf), lambda j, i: (0, off + j))
    vsp = lambda off: pl.BlockSpec((1, tf), lambda j, i: (0, off + j))
    return _pcall(body, name=name, grid=(nf, cfg.T // ts),
                  in_specs=[main(0), prev(0), main(nf), prev(nf), wsp(0), wsp(nf), vsp(0), vsp(nf)],
                  out_specs=pl.BlockSpec((ts, tf), lambda j, i: (i, j)), out_shape=jax.ShapeDtypeStruct((cfg.T, cfg.F), BF16),
                  scratch_shapes=[pltpu.VMEM((HALO + ts, tf), F32)] * 2,
                  compiler_params=_params("parallel", "parallel"))(h0, h0, h0, h0, w, w, cb, cb)


def _ffn_conv_bwd(df, h0, w, cb, cfg, *, name):
    KF, HALO = cfg.KF, FFN_HALO
    ts, tf = _ffn_tiles(cfg)
    tpb, nf = cfg.S // ts, cfg.F // tf
    lead = HALO - (KF - 1)
    ext = ts + HALO

    def body(g_ref, gp_ref, gn_ref, l_ref, lp_ref, ln_ref, d_ref, dn_ref, wg_ref, wl_ref, cg_ref, cl_ref,
             dg_ref, dl_ref, dwg_ref, dwl_ref, dcg_ref, dcl_ref, g_s, l_s, d_s, dhg_s, dhl_s):
        i = pl.program_id(1)
        first, last = i % tpb == 0, i % tpb == tpb - 1

        @pl.when(i == 0)
        def _():
            dwg_ref[...] = jnp.zeros_like(dwg_ref)
            dwl_ref[...] = jnp.zeros_like(dwl_ref)
            dcg_ref[...] = jnp.zeros_like(dcg_ref)
            dcl_ref[...] = jnp.zeros_like(dcl_ref)

        for s, main, prev, nxt in ((g_s, g_ref, gp_ref, gn_ref), (l_s, l_ref, lp_ref, ln_ref)):
            s[0:HALO, :] = jnp.where(first, 0.0, prev[...])
            s[HALO:HALO + ts, :] = main[...]
            s[HALO + ts:HALO + ext, :] = nxt[...]
        d_s[0:ts, :] = d_ref[...]
        d_s[ts:ext, :] = jnp.where(last, 0.0, dn_ref[...])
        hg, hl = cg_ref[...], cl_ref[...]
        for k in range(KF):
            hg = hg + wg_ref[k:k + 1, :] * g_s[lead + k:lead + k + ext, :]
            hl = hl + wl_ref[k:k + 1, :] * l_s[lead + k:lead + k + ext, :]
        df_e = d_s[...]
        dhg_s[...] = df_e * hl * _gelu_grad(hg)
        dhl_s[...] = df_e * _gelu(hg)
        for dh_s, h_s, w_ref, dx_ref, dw_ref, dc_ref in ((dhg_s, g_s, wg_ref, dg_ref, dwg_ref, dcg_ref),
                                                         (dhl_s, l_s, wl_ref, dl_ref, dwl_ref, dcl_ref)):
            dh = dh_s[0:ts, :]
            dc_ref[...] += jnp.sum(dh, axis=0, keepdims=True)
            dx = jnp.zeros((ts, tf), F32)
            for k in range(KF):
                dw_ref[k:k + 1, :] += jnp.sum(dh * h_s[lead + k:lead + k + ts, :], axis=0, keepdims=True)
                dx = dx + w_ref[k:k + 1, :] * dh_s[KF - 1 - k:KF - 1 - k + ts, :]
            dx_ref[...] = dx.astype(BF16)

    hb = ts // HALO
    nhb = cfg.T // HALO
    main = lambda off: pl.BlockSpec((ts, tf), lambda j, i: (i, off + j))
    prev = lambda off: pl.BlockSpec((HALO, tf), lambda j, i: (jnp.maximum(i * hb - 1, 0), off + j))
    nxt = lambda off: pl.BlockSpec((HALO, tf), lambda j, i: (jnp.minimum((i + 1) * hb, nhb - 1), off + j))
    wsp = lambda off: pl.BlockSpec((8, tf), lambda j, i: (0, off + j))
    vsp = lambda off: pl.BlockSpec((1, tf), lambda j, i: (0, off + j))
    dxs, dws, dcs = (jax.ShapeDtypeStruct((cfg.T, cfg.F), BF16), jax.ShapeDtypeStruct((8, cfg.F), F32),
                     jax.ShapeDtypeStruct((1, cfg.F), F32))
    return _pcall(body, name=name, grid=(nf, cfg.T // ts),
                  in_specs=[main(0), prev(0), nxt(0), main(nf), prev(nf), nxt(nf), main(0), nxt(0),
                            wsp(0), wsp(nf), vsp(0), vsp(nf)],
                  out_specs=[main(0), main(0), wsp(0), wsp(0), vsp(0), vsp(0)],
                  out_shape=[dxs, dxs, dws, dws, dcs, dcs],
                  scratch_shapes=[pltpu.VMEM((HALO + ext, tf), F32)] * 2 + [pltpu.VMEM((ext, tf), F32)] * 3,
                  compiler_params=_params("parallel", "arbitrary"))(h0, h0, h0, h0, h0, h0, df, df, w, w, cb, cb)


def _ada_fwd(c_all, w, b, *, name):
    L, D, n = w.shape
    B = c_all.shape[0]

    def body(c_ref, w_ref, b_ref, o_ref):
        c = c_ref[...]
        act = (c * _sigmoid(c)).astype(BF16)
        o_ref[0] = jnp.dot(act, w_ref[0].astype(BF16), preferred_element_type=F32) + b_ref[0]

    return _pcall(body, name=name, grid=(L,),
                  in_specs=[pl.BlockSpec((B, D), lambda l: (0, 0)), pl.BlockSpec((1, D, n), lambda l: (l, 0, 0)),
                            pl.BlockSpec((1, 1, n), lambda l: (l, 0, 0))],
                  out_specs=pl.BlockSpec((1, B, n), lambda l: (l, 0, 0)), out_shape=jax.ShapeDtypeStruct((L, B, n), F32),
                  compiler_params=_params("parallel"))(c_all, w, b)


def _ada_bwd(c_all, dmod, *, name):
    L, B, n = dmod.shape
    D = c_all.shape[1]

    def body(c_ref, d_ref, o_ref):
        c = c_ref[...]
        act = (c * _sigmoid(c)).astype(BF16)
        o_ref[0] = lax.dot_general(act, d_ref[0].astype(BF16), TN, preferred_element_type=F32)

    return _pcall(body, name=name, grid=(L,),
                  in_specs=[pl.BlockSpec((B, D), lambda l: (0, 0)), pl.BlockSpec((1, B, n), lambda l: (l, 0, 0))],
                  out_specs=pl.BlockSpec((1, D, n), lambda l: (l, 0, 0)), out_shape=jax.ShapeDtypeStruct((L, D, n), F32),
                  compiler_params=_params("parallel"))(c_all, dmod)


def _slot_sum(x, *, name):
    n, R, W = x.shape
    tr = _pick(R, (256, 128, 64, 32, 16, 8))

    def body(x_ref, o_ref):
        acc = x_ref[0].astype(F32)
        for k in range(1, n):
            acc = acc + x_ref[k].astype(F32)
        o_ref[...] = acc

    return _pcall(body, name=name, grid=(R // tr,), in_specs=[pl.BlockSpec((n, tr, W), lambda i: (0, i, 0))],
                  out_specs=pl.BlockSpec((tr, W), lambda i: (i, 0)), out_shape=jax.ShapeDtypeStruct((R, W), F32),
                  compiler_params=_params("parallel"))(x)


def _adamw(gs, w, m, v, *, name):
    n, R, W = gs.shape
    tr = _pick(R, (256, 128, 64, 32, 16, 8))
    c1, c2 = 1.0 - ADAM_B1 ** ADAM_STEP, 1.0 - ADAM_B2 ** ADAM_STEP

    def body(g_ref, w_ref, m_ref, v_ref, go_ref, d_ref, mo_ref, vo_ref):
        g = g_ref[0].astype(F32)
        for k in range(1, n):
            g = g + g_ref[k].astype(F32)
        m2 = ADAM_B1 * m_ref[...] + (1.0 - ADAM_B1) * g
        v2 = ADAM_B2 * v_ref[...] + (1.0 - ADAM_B2) * (g * g)
        go_ref[...] = g
        mo_ref[...] = m2
        vo_ref[...] = v2
        d_ref[...] = -ADAM_LR * ((m2 / c1) / (jnp.sqrt(v2 / c2) + ADAM_EPS) + ADAM_WD * w_ref[...])

    blk = pl.BlockSpec((tr, W), lambda i: (i, 0))
    o = jax.ShapeDtypeStruct((R, W), F32)
    return _pcall(body, name=name, grid=(R // tr,), in_specs=[pl.BlockSpec((n, tr, W), lambda i: (0, i, 0)), blk, blk, blk],
                  out_specs=[blk] * 4, out_shape=[o] * 4, compiler_params=_params("parallel"))(gs, w, m, v)


def _exchange(x, *, all_to_all, name):
    blk = x.shape[1:] if all_to_all else x.shape

    def body(x_ref, o_ref, send_sems, recv_sems, local_sem):
        mx, my, mc = lax.axis_index("x"), lax.axis_index("y"), lax.axis_index("c")
        me = 4 * mx + 2 * my + mc
        src = (lambda idx: x_ref.at[idx]) if all_to_all else (lambda idx: x_ref)
        mine = pltpu.make_async_copy(src(me), o_ref.at[me], local_sem)
        mine.start()
        copies = []
        for k in range(1, N_DEV):
            px, py, pc = mx ^ ((k >> 2) & 1), my ^ ((k >> 1) & 1), mc ^ (k & 1)
            cp = pltpu.make_async_remote_copy(
                src_ref=src(4 * px + 2 * py + pc), dst_ref=o_ref.at[me], send_sem=send_sems.at[k - 1],
                recv_sem=recv_sems.at[k - 1], device_id=(px, py, pc), device_id_type=pl.DeviceIdType.MESH)
            cp.start()
            copies.append(cp)
        for cp in copies:
            cp.wait()
        mine.wait()

    anyspec = pl.BlockSpec(memory_space=pl.ANY)
    return _pcall(body, name=name, in_specs=[anyspec], out_specs=anyspec,
                  out_shape=jax.ShapeDtypeStruct((N_DEV,) + tuple(blk), x.dtype),
                  scratch_shapes=[pltpu.SemaphoreType.DMA((N_DEV - 1,)), pltpu.SemaphoreType.DMA((N_DEV - 1,)),
                                  pltpu.SemaphoreType.DMA(())])(x)


PACK_ROWS = 16


def _pack(arrs, width, dtype, lead=0):
    parts, segs, r = [], [], 0
    for a in arrs:
        lshape, shape = a.shape[:lead], a.shape[lead:]
        n = math.prod(shape)
        rows = -(-n // width)
        rows_p = -(-rows // PACK_ROWS) * PACK_ROWS
        flat = a.reshape(lshape + (n,)).astype(dtype)
        flat = jnp.pad(flat, [(0, 0)] * lead + [(0, rows_p * width - n)])
        parts.append(flat.reshape(lshape + (rows_p, width)))
        segs.append((r, n, shape))
        r += rows_p
    return jnp.concatenate(parts, axis=lead), segs


def _unpack(p, segs):
    lshape, width = p.shape[:-2], p.shape[-1]
    outs = []
    for r, n, shape in segs:
        rows = -(-n // width)
        blk = p[..., r:r + rows, :].reshape(lshape + (rows * width,))
        outs.append(blk[..., :n].reshape(lshape + shape))
    return outs


def _split_cols(a, f_off, h):
    return jnp.concatenate([a[..., :f_off], a[..., f_off + h:]], axis=-1), a[..., f_off:f_off + h]


def _merge_cols(main, f, f_off):
    return jnp.concatenate([main[..., :f_off], f, main[..., f_off:]], axis=-1)


def _pad_to(a, n, axis):
    pad = [(0, 0)] * a.ndim
    pad[axis] = (0, n - a.shape[axis])
    return jnp.pad(a, pad)


def kernel(x, c, w_ada, b_ada, w_in, b_in, conv_a_w, conv_a_b, ln_conv_g, ln_conv_b, w_conv_proj, w_attn_proj, w_mix_out, b_mix_out, ln1_g, ln1_b, w_ffn_up, ffn_conv_w, ffn_conv_b, w_ffn_down, ln2_g, ln2_b, loss_target, m_w_ada, m_b_ada, m_w_in, m_b_in, m_conv_a_w, m_conv_a_b, m_ln_conv_g, m_ln_conv_b, m_w_conv_proj, m_w_attn_proj, m_w_mix_out, m_b_mix_out, m_ln1_g, m_ln1_b, m_w_ffn_up, m_ffn_conv_w, m_ffn_conv_b, m_w_ffn_down, m_ln2_g, m_ln2_b, v_w_ada, v_b_ada, v_w_in, v_b_in, v_conv_a_w, v_conv_a_b, v_ln_conv_g, v_ln_conv_b, v_w_conv_proj, v_w_attn_proj, v_w_mix_out, v_b_mix_out, v_ln1_g, v_ln1_b, v_w_ffn_up, v_ffn_conv_w, v_ffn_conv_b, v_w_ffn_down, v_ln2_g, v_ln2_b):
    L, D = w_ada.shape[0], w_ada.shape[1]
    Bl, S, _ = x.shape
    C, KW, AW = conv_a_b.shape[1], conv_a_w.shape[1], w_attn_proj.shape[1]
    F, KF, n_in_all = ffn_conv_b.shape[1] // 2, ffn_conv_w.shape[1], b_in.shape[1]
    H = n_in_all - 2 * C - 3 * AW - 2 * D
    cfg = Cfg(L=L, Bl=Bl, S=S, D=D, C=C, KW=KW, H=H, Dh=AW // H, F=F, KF=KF)
    T, NM = cfg.T, cfg.NM
    f_off = 2 * C + 3 * AW
    n_ada = w_ada.shape[2]
    me = 4 * lax.axis_index("x") + 2 * lax.axis_index("y") + lax.axis_index("c")

    def my_cols(a, n):
        return lax.dynamic_slice_in_dim(a, me * n, n, axis=a.ndim - 1)

    def big_list(w_in_, w_cp_, w_ap_, w_mo_, w_up_, w_dn_):
        out = []
        for l in range(L):
            out += [w_in_[l].T, w_cp_[l].T, w_ap_[l].T, w_mo_[l], w_up_[l].T, w_dn_[l]]
        return out

    wpack, wsegs = _pack(big_list(w_in, w_conv_proj, w_attn_proj, w_mix_out, w_ffn_up, w_ffn_down), D, BF16)
    wall = _exchange(wpack, all_to_all=False, name="gather_weights")
    spack, ssegs = _pack([c, conv_a_w, ffn_conv_w], D, F32)
    c_g, caw_g, fcw_g = _unpack(_exchange(spack, all_to_all=False, name="gather_small"), ssegs)
    c_all = c_g.reshape(N_DEV * Bl, D)
    caw = _pad_to(jnp.moveaxis(caw_g, 0, 2).reshape(L, KW, C), 32, 1)
    fcw = _pad_to(jnp.moveaxis(fcw_g, 0, 2).reshape(L, KF, 2 * F), 8, 1)

    wfull = _unpack(wall, wsegs)
    W = []
    for l in range(L):
        w_in_t, w_cp_t, w_ap_t, w_mo, w_up_t, w_dn = (a.reshape((-1, a.shape[-1])) for a in wfull[6 * l:6 * l + 6])
        wm_t, wf_t = _split_cols(w_in_t.T, f_off, H)
        bm, bf = _split_cols(b_in[l], f_off, H)
        W.append(dict(wm_t=wm_t.T, wf_t=_pad_to(wf_t.T, LANES, 0), bm=bm[None], bf=_pad_to(bf, LANES, 0)[None],
                      w_cp_t=w_cp_t, w_ap_t=w_ap_t, w_mo=w_mo, w_up_t=w_up_t, w_dn=w_dn))

    mod_part = _ada_fwd(c_all, w_ada, my_cols(b_ada, n_ada)[:, None, :], name="ada_fwd")
    mod_send = jnp.moveaxis(mod_part.reshape(L, N_DEV, Bl, n_ada), 1, 0).reshape(N_DEV, L * Bl, n_ada)
    mod_recv = _exchange(mod_send, all_to_all=True, name="exchange_mod")
    mod = jnp.moveaxis(mod_recv.reshape(N_DEV, L, Bl, n_ada), 0, 2).reshape(L, Bl, 6, 1, D)
    shift1, scale1, gate1, shift2, scale2, gate2 = (mod[:, :, i] for i in range(6))

    xf = x.reshape(T, D)
    u = _ln_mod_fwd(xf, shift1[0], scale1[0], cfg, name="ln_mod_fwd")
    saved = []
    xin = xf
    for l in range(L):
        w = W[l]
        zm = _matmul(u, w["wm_t"], mode="nt", bias=w["bm"], name=f"in_proj_{l}")
        zf = _matmul(u, w["wf_t"], mode="nt", bias=w["bf"], name=f"in_proj_f_{l}")
        a3 = _conv_a_fwd(zm, caw[l], conv_a_b[l][None], ln_conv_g[l][None], ln_conv_b[l][None], cfg, name=f"conv_a_fwd_{l}")
        cum_c = _fgate_fwd(zf, cfg, name=f"fgate_fwd_{l}")
        o, o32, lse = _attn_fwd(zm, cum_c, cfg, name=f"attn_fwd_{l}")
        ya = _matmul(a3, w["w_cp_t"], mode="nt", name=f"conv_proj_{l}")
        yb = _matmul(o, w["w_ap_t"], mode="nt", name=f"attn_proj_{l}")
        mg = _merge_fwd(zm, ya, yb, cfg, name=f"merge_fwd_{l}")
        mix = _matmul(mg, w["w_mo"], mode="nn", bias=b_mix_out[l][None], name=f"mix_out_{l}")
        x1, u2 = _res_ln_fwd(xin, mix, gate1[l], ln1_g[l][None], ln1_b[l][None], cfg, name=f"res_ln1_fwd_{l}",
                             nxt=(shift2[l], scale2[l]))
        h0 = _matmul(u2, w["w_up_t"], mode="nt", name=f"ffn_up_{l}")
        fa = _ffn_conv_fwd(h0, fcw[l], ffn_conv_b[l][None], cfg, name=f"ffn_conv_fwd_{l}")
        ffn = _matmul(fa, w["w_dn"], mode="nn", name=f"ffn_down_{l}")
        saved.append(dict(x=xin, u=u, zm=zm, zf=zf, a3=a3, cum_c=cum_c, o=o, o32=o32, lse=lse, ya=ya, yb=yb, mg=mg, mix=mix,
                          x1=x1, u2=u2, h0=h0, fa=fa, ffn=ffn))
        if l + 1 < L:
            xin, u = _res_ln_fwd(x1, ffn, gate2[l], ln2_g[l][None], ln2_b[l][None], cfg, name=f"res_ln2_fwd_{l}",
                                 nxt=(shift1[l + 1], scale1[l + 1]))
        else:
            xin = _res_ln_fwd(x1, ffn, gate2[l], ln2_g[l][None], ln2_b[l][None], cfg, name=f"res_ln2_fwd_{l}")

    dx, loss_tiles = _loss_grad(xin, loss_target.reshape(T, D), cfg, name="loss_grad")
    loss = lax.psum(0.5 / D * jnp.sum(loss_tiles[:, 0, 0]), ("x", "y", "c"))

    gbig = [None] * (6 * L)
    gsm = [dict() for _ in range(L)]
    dmods = [None] * L
    for l in reversed(range(L)):
        w, s = W[l], saved[l]
        dres2, dffn, dg2, db2, dgate2, _ = _res_ln_bwd(dx, s["x1"], s["ffn"], gate2[l], ln2_g[l][None], cfg, name=f"res_ln2_bwd_{l}")
        dfa = _matmul(dffn, w["w_dn"], mode="nt", name=f"d_ffn_act_{l}")
        gbig[6 * l + 5] = _matmul(s["fa"], dffn, mode="tn", name=f"dw_ffn_down_{l}")
        dh0g, dh0l, dwg, dwl, dcg, dcl = _ffn_conv_bwd(dfa, s["h0"], fcw[l], ffn_conv_b[l][None], cfg, name=f"ffn_conv_bwd_{l}")
        dh0 = jnp.concatenate([dh0g, dh0l], axis=1)
        du2 = _matmul(dh0, w["w_up_t"], mode="nn", name=f"d_u2_{l}")
        gbig[6 * l + 4] = _matmul(dh0, s["u2"], mode="tn", name=f"dw_ffn_up_{l}")
        dx1, dscale2, dshift2 = _ln_mod_bwd(du2, s["x1"], scale2[l], dres2, cfg, name=f"ln_mod2_bwd_{l}")
        dres1, dmix, dg1, db1, dgate1, dbmo = _res_ln_bwd(dx1, s["x"], s["mix"], gate1[l], ln1_g[l][None], cfg, name=f"res_ln1_bwd_{l}")
        dmg = _matmul(dmix, w["w_mo"], mode="nt", name=f"d_merge_{l}")
        gbig[6 * l + 3] = _matmul(s["mg"], dmix, mode="tn", name=f"dw_mix_out_{l}")
        dya, dyb, dzga, dzgb = _merge_bwd(dmg, s["zm"], s["ya"], s["yb"], cfg, name=f"merge_bwd_{l}")
        gbig[6 * l + 1] = _matmul(dya, s["a3"], mode="tn", name=f"dw_conv_proj_{l}")
        da3 = _matmul(dya, w["w_cp_t"], mode="nn", name=f"d_a3_{l}")
        gbig[6 * l + 2] = _matmul(dyb, s["o"], mode="tn", name=f"dw_attn_proj_{l}")
        do = _matmul(dyb, w["w_ap_t"], mode="nn", out_dtype=BF16, name=f"d_o_{l}")
        dq, dk, dv, dcum_c = _attn_bwd(s["zm"], s["cum_c"], s["o32"], do, s["lse"], cfg, name=f"attn_bwd_{l}")
        dzf = _fgate_bwd(dcum_c, s["zf"], cfg, name=f"fgate_bwd_{l}")
        dzglu, dcaw, dcab, dlcg, dlcb = _conv_a_bwd(da3, s["zm"], caw[l], conv_a_b[l][None], ln_conv_g[l][None],
                                                    ln_conv_b[l][None], cfg, name=f"conv_a_bwd_{l}")
        dzm = jnp.concatenate([dzglu, dq, dk, dv, dzga, dzgb], axis=1)
        du1 = _matmul(dzf, w["wf_t"], mode="nn", name=f"d_u1_f_{l}")
        du1 = _matmul(dzm, w["wm_t"], mode="nn", add=du1, name=f"d_u1_{l}")
        dwm_t = _matmul(dzm, s["u"], mode="tn", name=f"dw_in_{l}")
        dwf_t = _matmul(dzf, s["u"], mode="tn", name=f"dw_in_f_{l}")
        gbig[6 * l] = _merge_cols(dwm_t.T, dwf_t[:H].T, f_off).T
        dbm, dbf = _colsum(dzm, name=f"db_in_{l}"), _colsum(dzf, name=f"db_in_f_{l}")
        dx, dscale1, dshift1 = _ln_mod_bwd(du1, s["x"], scale1[l], dres1, cfg, name=f"ln_mod1_bwd_{l}")
        dmods[l] = jnp.concatenate([dshift1, dscale1, dgate1, dshift2, dscale2, dgate2], axis=1).reshape(Bl, 6 * D)
        gsm[l] = dict(b_in=_merge_cols(dbm[0], dbf[0, :H], f_off), conv_a_b=dcab[0], ln_conv_g=dlcg[0], ln_conv_b=dlcb[0],
                      b_mix_out=dbmo[0], ln1_g=dg1[0], ln1_b=db1[0], ffn_conv_b=jnp.concatenate([dcg[0], dcl[0]]),
                      ln2_g=dg2[0], ln2_b=db2[0], conv_a_w=dcaw[:KW], ffn_conv_w=jnp.concatenate([dwg[:KF], dwl[:KF]], axis=1))
    grad_x = dx.reshape(Bl, S, D)

    small_names = ["b_in", "conv_a_b", "ln_conv_g", "ln_conv_b", "b_mix_out", "ln1_g", "ln1_b", "ffn_conv_b", "ln2_g", "ln2_b",
                   "conv_a_w", "ffn_conv_w"]
    gs_list = [jnp.stack(dmods)] + [jnp.stack([gsm[l][n] for l in range(L)]) for n in small_names]
    gspack, gssegs = _pack(gs_list, D, F32)
    gs_all = _exchange(gspack, all_to_all=False, name="gather_small_grads")
    dmod_all = jnp.moveaxis(_unpack(gs_all, gssegs)[0], 0, 1).reshape(L, N_DEV * Bl, 6 * D)
    g_small = dict(zip(small_names, _unpack(_slot_sum(gs_all, name="sum_small_grads"), gssegs)[1:]))
    g_small["conv_a_w"] = my_cols(g_small["conv_a_w"], C // N_DEV)
    g_small["ffn_conv_w"] = my_cols(g_small["ffn_conv_w"], 2 * F // N_DEV)
    g_small["w_ada"] = _ada_bwd(c_all, my_cols(dmod_all, n_ada), name="ada_bwd")
    g_small["b_ada"] = jnp.stack([_colsum(dmod_all[l], name=f"db_ada_{l}")[0] for l in range(L)])

    gsend, gsegs = _pack([g.reshape((N_DEV, g.shape[0] // N_DEV, g.shape[1])) for g in gbig], D, BF16, lead=1)
    grecv = _exchange(gsend, all_to_all=True, name="exchange_grads")
    big_names = ["w_in", "w_conv_proj", "w_attn_proj", "w_mix_out", "w_ffn_up", "w_ffn_down"]
    given = dict(w_in=(w_in, m_w_in, v_w_in), w_conv_proj=(w_conv_proj, m_w_conv_proj, v_w_conv_proj),
                 w_attn_proj=(w_attn_proj, m_w_attn_proj, v_w_attn_proj), w_mix_out=(w_mix_out, m_w_mix_out, v_w_mix_out),
                 w_ffn_up=(w_ffn_up, m_w_ffn_up, v_w_ffn_up), w_ffn_down=(w_ffn_down, m_w_ffn_down, v_w_ffn_down),
                 w_ada=(w_ada, m_w_ada, v_w_ada), b_ada=(b_ada, m_b_ada, v_b_ada), b_in=(b_in, m_b_in, v_b_in),
                 conv_a_w=(conv_a_w, m_conv_a_w, v_conv_a_w), conv_a_b=(conv_a_b, m_conv_a_b, v_conv_a_b),
                 ln_conv_g=(ln_conv_g, m_ln_conv_g, v_ln_conv_g), ln_conv_b=(ln_conv_b, m_ln_conv_b, v_ln_conv_b),
                 b_mix_out=(b_mix_out, m_b_mix_out, v_b_mix_out), ln1_g=(ln1_g, m_ln1_g, v_ln1_g), ln1_b=(ln1_b, m_ln1_b, v_ln1_b),
                 ffn_conv_w=(ffn_conv_w, m_ffn_conv_w, v_ffn_conv_w), ffn_conv_b=(ffn_conv_b, m_ffn_conv_b, v_ffn_conv_b),
                 ln2_g=(ln2_g, m_ln2_g, v_ln2_g), ln2_b=(ln2_b, m_ln2_b, v_ln2_b))
    res = {}
    wmv = [_pack(big_list(*(given[n][i] for n in big_names)), D, F32)[0] for i in range(3)]
    outs = _adamw(grecv, *wmv, name="adamw_big")
    transposed = (True, True, True, False, True, False)
    for kind, packed in zip(("grad", "delta", "new_m", "new_v"), outs):
        parts = _unpack(packed, gsegs)
        for i, n in enumerate(big_names):
            per_l = [parts[6 * l + i].T if transposed[i] else parts[6 * l + i] for l in range(L)]
            res[kind, n] = jnp.stack(per_l)
    loc_names = ["w_ada", "b_ada"] + small_names
    packs = [_pack([g_small[n] for n in loc_names], D, F32)] + [_pack([given[n][i] for n in loc_names], D, F32) for i in range(3)]
    outs = _adamw(packs[0][0][None], packs[1][0], packs[2][0], packs[3][0], name="adamw_small")
    for kind, packed in zip(("grad", "delta", "new_m", "new_v"), outs):
        for n, a in zip(loc_names, _unpack(packed, packs[0][1])):
            res[kind, n] = a

    order = ["w_ada", "b_ada", "w_in", "b_in", "conv_a_w", "conv_a_b", "ln_conv_g", "ln_conv_b", "w_conv_proj", "w_attn_proj",
             "w_mix_out", "b_mix_out", "ln1_g", "ln1_b", "w_ffn_up", "ffn_conv_w", "ffn_conv_b", "w_ffn_down", "ln2_g", "ln2_b"]
    return (loss, grad_x, *[res[k, n] for k in ("grad", "delta", "new_m", "new_v") for n in order])
```

```python
import functools
import math
from typing import NamedTuple

import jax
import jax.numpy as jnp
from jax import lax
from jax.experimental import pallas as pl
from jax.experimental.pallas import tpu as pltpu

F32, BF16 = jnp.float32, jnp.bfloat16
LN_EPS = 1e-5
ADAM_LR, ADAM_B1, ADAM_B2, ADAM_EPS, ADAM_WD, ADAM_STEP = 0.001, 0.9, 0.999, 1e-08, 0.01, 10
N_DEV = 8
LANES = 128
VMEM_LIMIT = 56 * 1024 * 1024
NEG = -1e30
NT = (((1,), (1,)), ((), ()))
TN = (((0,), (0,)), ((), ()))


class Cfg(NamedTuple):
    L: int
    Bl: int
    S: int
    D: int
    C: int
    KW: int
    H: int
    Dh: int
    F: int
    KF: int

    @property
    def T(self): return self.Bl * self.S
    @property
    def AW(self): return self.H * self.Dh
    @property
    def NM(self): return 2 * self.C + 3 * self.AW + 2 * self.D
    @property
    def q_off(self): return 2 * self.C
    @property
    def g_off(self): return 2 * self.C + 3 * self.AW
    @property
    def alpha(self): return (2.0 * self.L) ** 0.25


def _pcall(body, **kw):
    return pl.pallas_call(body, **kw)


def _params(*sem):
    return pltpu.CompilerParams(dimension_semantics=sem, vmem_limit_bytes=VMEM_LIMIT)


def _pick(n, prefs):
    for p in prefs:
        if n % p == 0:
            return p
    return n


def _sigmoid(x):
    return 1.0 / (1.0 + jnp.exp(-x))


def _ln_stats(x):
    mu = jnp.mean(x, axis=-1, keepdims=True)
    xc = x - mu
    var = jnp.mean(xc * xc, axis=-1, keepdims=True)
    rstd = lax.rsqrt(var + LN_EPS)
    return xc * rstd, rstd


def _ln_bwd(dxh, xh, rstd):
    return rstd * (dxh - jnp.mean(dxh, axis=-1, keepdims=True) - xh * jnp.mean(dxh * xh, axis=-1, keepdims=True))


def _matmul(a, b, *, mode, name, bias=None, add=None, out_dtype=F32, tm=None, tn=None, tk=None, after=None):
    if mode == "tn":
        K, M = a.shape
    else:
        M, K = a.shape
    N = b.shape[0] if mode == "nt" else b.shape[1]
    lane_tiles = (1536, 1408, 1024, 768, 512, 256, 128)
    tm = tm or _pick(M, lane_tiles if mode == "tn" else (1024, 512, 256, 128, 64, 32, 16, 8))
    tn = tn or _pick(N, lane_tiles)
    tk = tk or _pick(K, (512, 256, 128) if mode == "tn" else lane_tiles)
    nk = K // tk
    dn = {"nn": (((1,), (0,)), ((), ())), "nt": NT, "tn": TN}[mode]
    has_bias, has_add, has_after = bias is not None, add is not None, after is not None

    def body(*refs):
        a_ref, b_ref = refs[0], refs[1]
        pos = 2
        bias_ref = refs[pos] if has_bias else None
        pos += has_bias
        add_ref = refs[pos] if has_add else None
        pos += has_add + has_after
        o_ref = refs[pos]
        part = lax.dot_general(a_ref[...], b_ref[...], dn, preferred_element_type=F32)

        def finish(acc):
            if has_bias:
                acc = acc + bias_ref[...]
            if has_add:
                acc = acc + add_ref[...]
            o_ref[...] = acc.astype(out_dtype)

        if nk == 1:
            finish(part)
        else:
            acc_ref = refs[pos + 1]
            k = pl.program_id(2)

            @pl.when(k == 0)
            def _():
                acc_ref[...] = part

            @pl.when(k > 0)
            def _():
                acc_ref[...] += part

            @pl.when(k == nk - 1)
            def _():
                finish(acc_ref[...])

    a_spec = pl.BlockSpec((tk, tm), lambda i, j, k: (k, i)) if mode == "tn" else pl.BlockSpec((tm, tk), lambda i, j, k: (i, k))
    b_spec = pl.BlockSpec((tn, tk), lambda i, j, k: (j, k)) if mode == "nt" else pl.BlockSpec((tk, tn), lambda i, j, k: (k, j))
    in_specs, args = [a_spec, b_spec], [a, b]
    if has_bias:
        in_specs.append(pl.BlockSpec((1, tn), lambda i, j, k: (0, j)))
        args.append(bias)
    if has_add:
        in_specs.append(pl.BlockSpec((tm, tn), lambda i, j, k: (i, j)))
        args.append(add)
    if has_after:
        in_specs.append(pl.BlockSpec(memory_space=pl.ANY))
        args.append(after)
    return _pcall(
        body, name=name, grid=(M // tm, N // tn, nk), in_specs=in_specs,
        out_specs=pl.BlockSpec((tm, tn), lambda i, j, k: (i, j)),
        out_shape=jax.ShapeDtypeStruct((M, N), out_dtype),
        scratch_shapes=[pltpu.VMEM((tm, tn), F32)] if nk > 1 else [],
        compiler_params=_params("parallel", "parallel", "arbitrary"),
    )(*args)


def _colsum(x, *, name):
    T, N = x.shape
    tr = _pick(T, (512, 256, 128, 64, 32, 16))
    tc = _pick(N, (1536, 1024, 512, 256, 128))

    def body(x_ref, o_ref):
        @pl.when(pl.program_id(1) == 0)
        def _():
            o_ref[...] = jnp.zeros_like(o_ref)

        o_ref[...] += jnp.sum(x_ref[...].astype(F32), axis=0, keepdims=True)

    return _pcall(body, name=name, grid=(N // tc, T // tr), in_specs=[pl.BlockSpec((tr, tc), lambda j, i: (i, j))],
                  out_specs=pl.BlockSpec((1, tc), lambda j, i: (0, j)), out_shape=jax.ShapeDtypeStruct((1, N), F32),
                  compiler_params=_params("parallel", "arbitrary"))(x)


def _row_tile(cfg):
    return _pick(cfg.S, (256, 128, 64, 32, 16, 8))


def _ln_mod_fwd(x, shift, scale, cfg, *, name):
    tr = _row_tile(cfg)
    tpb = cfg.S // tr

    def body(x_ref, sh_ref, sc_ref, u_ref):
        xh, _ = _ln_stats(x_ref[...])
        u_ref[...] = (xh * (1.0 + sc_ref[0]) + sh_ref[0]).astype(BF16)

    row = pl.BlockSpec((tr, cfg.D), lambda i: (i, 0))
    per_b = pl.BlockSpec((1, 1, cfg.D), lambda i: (i // tpb, 0, 0))
    return _pcall(body, name=name, grid=(cfg.T // tr,), in_specs=[row, per_b, per_b], out_specs=row,
                  out_shape=jax.ShapeDtypeStruct((cfg.T, cfg.D), BF16), compiler_params=_params("parallel"))(x, shift, scale)


def _res_ln_fwd(xin, br, gate, g, b, cfg, *, name, nxt=None):
    tr = _row_tile(cfg)
    tpb = cfg.S // tr
    alpha = cfg.alpha

    def body(*refs):
        x_ref, br_ref, gt_ref, g_ref, b_ref = refs[:5]
        r = alpha * x_ref[...] + (1.0 + gt_ref[0]) * br_ref[...]
        xh, _ = _ln_stats(r)
        xo = xh * g_ref[...] + b_ref[...]
        if nxt is None:
            refs[5][...] = xo
        else:
            sh_ref, sc_ref, xo_ref, u_ref = refs[5:]
            xo_ref[...] = xo
            uh, _ = _ln_stats(xo)
            u_ref[...] = (uh * (1.0 + sc_ref[0]) + sh_ref[0]).astype(BF16)

    row = pl.BlockSpec((tr, cfg.D), lambda i: (i, 0))
    per_b = pl.BlockSpec((1, 1, cfg.D), lambda i: (i // tpb, 0, 0))
    vec = pl.BlockSpec((1, cfg.D), lambda i: (0, 0))
    in_specs, args = [row, row, per_b, vec, vec], [xin, br, gate, g, b]
    out_specs, out_shape = row, jax.ShapeDtypeStruct((cfg.T, cfg.D), F32)
    if nxt is not None:
        in_specs += [per_b, per_b]
        args += list(nxt)
        out_specs = [row, row]
        out_shape = [out_shape, jax.ShapeDtypeStruct((cfg.T, cfg.D), BF16)]
    return _pcall(body, name=name, grid=(cfg.T // tr,), in_specs=in_specs, out_specs=out_specs, out_shape=out_shape,
                  compiler_params=_params("parallel"))(*args)


def _loss_grad(y, tgt, cfg, *, name):
    tr = _row_tile(cfg)
    nt = cfg.T // tr
    inv_d = 1.0 / cfg.D

    def body(y_ref, t_ref, dy_ref, ls_ref):
        e = y_ref[...] - t_ref[...]
        dy_ref[...] = e * inv_d
        ls_ref[...] = jnp.full((1, 1, LANES), jnp.sum(e * e), F32)

    row = pl.BlockSpec((tr, cfg.D), lambda i: (i, 0))
    return _pcall(body, name=name, grid=(nt,), in_specs=[row, row],
                  out_specs=[row, pl.BlockSpec((1, 1, LANES), lambda i: (i, 0, 0))],
                  out_shape=[jax.ShapeDtypeStruct((cfg.T, cfg.D), F32), jax.ShapeDtypeStruct((nt, 1, LANES), F32)],
                  compiler_params=_params("parallel"))(y, tgt)


def _res_ln_bwd(dy, xin, br, gate, g, cfg, *, name):
    tr = _row_tile(cfg)
    tpb = cfg.S // tr
    alpha = cfg.alpha

    def body(dy_ref, x_ref, br_ref, gt_ref, g_ref, dx_ref, dbr_ref, dg_ref, db_ref, dgt_ref, dbs_ref):
        i = pl.program_id(0)

        @pl.when(i == 0)
        def _():
            dg_ref[...] = jnp.zeros_like(dg_ref)
            db_ref[...] = jnp.zeros_like(db_ref)
            dbs_ref[...] = jnp.zeros_like(dbs_ref)

        @pl.when(i % tpb == 0)
        def _():
            dgt_ref[...] = jnp.zeros_like(dgt_ref)

        dy, brv, one_gate = dy_ref[...], br_ref[...], 1.0 + gt_ref[0]
        xh, rstd = _ln_stats(alpha * x_ref[...] + one_gate * brv)
        dg_ref[...] += jnp.sum(dy * xh, axis=0, keepdims=True)
        db_ref[...] += jnp.sum(dy, axis=0, keepdims=True)
        dr = _ln_bwd(dy * g_ref[...], xh, rstd)
        dx_ref[...] = alpha * dr
        dbr = one_gate * dr
        dbr_ref[...] = dbr.astype(BF16)
        dbs_ref[...] += jnp.sum(dbr, axis=0, keepdims=True)
        dgt_ref[0] += jnp.sum(dr * brv, axis=0, keepdims=True)

    row = pl.BlockSpec((tr, cfg.D), lambda i: (i, 0))
    per_b = pl.BlockSpec((1, 1, cfg.D), lambda i: (i // tpb, 0, 0))
    vec = pl.BlockSpec((1, cfg.D), lambda i: (0, 0))
    vs = jax.ShapeDtypeStruct((1, cfg.D), F32)
    return _pcall(body, name=name, grid=(cfg.T // tr,), in_specs=[row, row, row, per_b, vec],
                  out_specs=[row, row, vec, vec, per_b, vec],
                  out_shape=[jax.ShapeDtypeStruct((cfg.T, cfg.D), F32), jax.ShapeDtypeStruct((cfg.T, cfg.D), BF16), vs, vs,
                             jax.ShapeDtypeStruct((cfg.Bl, 1, cfg.D), F32), vs],
                  compiler_params=_params("arbitrary"))(dy, xin, br, gate, g)


def _ln_mod_bwd(du, xin, scale, dres, cfg, *, name):
    tr = _row_tile(cfg)
    tpb = cfg.S // tr

    def body(du_ref, x_ref, sc_ref, dres_ref, dx_ref, dsc_ref, dsh_ref):
        @pl.when(pl.program_id(0) % tpb == 0)
        def _():
            dsc_ref[...] = jnp.zeros_like(dsc_ref)
            dsh_ref[...] = jnp.zeros_like(dsh_ref)

        du = du_ref[...]
        xh, rstd = _ln_stats(x_ref[...])
        dsc_ref[0] += jnp.sum(du * xh, axis=0, keepdims=True)
        dsh_ref[0] += jnp.sum(du, axis=0, keepdims=True)
        dx_ref[...] = _ln_bwd(du * (1.0 + sc_ref[0]), xh, rstd) + dres_ref[...]

    row = pl.BlockSpec((tr, cfg.D), lambda i: (i, 0))
    per_b = pl.BlockSpec((1, 1, cfg.D), lambda i: (i // tpb, 0, 0))
    bs = jax.ShapeDtypeStruct((cfg.Bl, 1, cfg.D), F32)
    return _pcall(body, name=name, grid=(cfg.T // tr,), in_specs=[row, row, per_b, row], out_specs=[row, per_b, per_b],
                  out_shape=[jax.ShapeDtypeStruct((cfg.T, cfg.D), F32), bs, bs],
                  compiler_params=_params("arbitrary"))(du, xin, scale, dres)


def _merge_tiles(cfg):
    tr = _pick(cfg.T, (512, 256, 128, 64, 32, 16))
    tc = _pick(math.gcd(cfg.g_off, cfg.D), (512, 256, 128))
    return tr, tc


def _merge_fwd(zm, ya, yb, cfg, *, name):
    tr, tc = _merge_tiles(cfg)
    ga0, gb0 = cfg.g_off // tc, (cfg.g_off + cfg.D) // tc

    def body(ga_ref, gb_ref, ya_ref, yb_ref, m_ref):
        m_ref[...] = (_sigmoid(ga_ref[...]) * ya_ref[...] + _sigmoid(gb_ref[...]) * yb_ref[...]).astype(BF16)

    blk = pl.BlockSpec((tr, tc), lambda i, j: (i, j))
    return _pcall(body, name=name, grid=(cfg.T // tr, cfg.D // tc),
                  in_specs=[pl.BlockSpec((tr, tc), lambda i, j: (i, ga0 + j)), pl.BlockSpec((tr, tc), lambda i, j: (i, gb0 + j)), blk, blk],
                  out_specs=blk, out_shape=jax.ShapeDtypeStruct((cfg.T, cfg.D), BF16),
                  compiler_params=_params("parallel", "parallel"))(zm, zm, ya, yb)


def _merge_bwd(dm, zm, ya, yb, cfg, *, name):
    tr, tc = _merge_tiles(cfg)
    ga0, gb0 = cfg.g_off // tc, (cfg.g_off + cfg.D) // tc

    def body(dm_ref, ga_ref, gb_ref, ya_ref, yb_ref, dya_ref, dyb_ref, dga_ref, dgb_ref):
        dm = dm_ref[...]
        ga, gb = _sigmoid(ga_ref[...]), _sigmoid(gb_ref[...])
        dya_ref[...] = (dm * ga).astype(BF16)
        dyb_ref[...] = (dm * gb).astype(BF16)
        dga_ref[...] = (dm * ya_ref[...] * ga * (1.0 - ga)).astype(BF16)
        dgb_ref[...] = (dm * yb_ref[...] * gb * (1.0 - gb)).astype(BF16)

    blk = pl.BlockSpec((tr, tc), lambda i, j: (i, j))
    o = jax.ShapeDtypeStruct((cfg.T, cfg.D), BF16)
    return _pcall(body, name=name, grid=(cfg.T // tr, cfg.D // tc),
                  in_specs=[blk, pl.BlockSpec((tr, tc), lambda i, j: (i, ga0 + j)), pl.BlockSpec((tr, tc), lambda i, j: (i, gb0 + j)), blk, blk],
                  out_specs=[blk] * 4, out_shape=[o] * 4, compiler_params=_params("parallel", "parallel"))(dm, zm, zm, ya, yb)


CONV_A_HALO = 32
CONV_A_CHUNK = 32


def _conv_a_tile(cfg):
    assert cfg.KW - 1 <= CONV_A_HALO
    return _pick(cfg.S, (256, 128, 64, 32))


def _conv_a_fwd(zm, w, cb, g, b, cfg, *, name):
    C, KW, HALO, CH = cfg.C, cfg.KW, CONV_A_HALO, CONV_A_CHUNK
    ts = _conv_a_tile(cfg)
    tpb = cfg.S // ts
    lead = HALO - (KW - 1)

    def body(z_ref, zp_ref, w_ref, cb_ref, g_ref, b_ref, o_ref, a0_s):
        first = pl.program_id(0) % tpb == 0
        prev = zp_ref[:, :C] * _sigmoid(zp_ref[:, C:])
        a0_s[0:HALO, :] = jnp.where(first, 0.0, prev)
        a0_s[HALO:HALO + ts, :] = z_ref[:, :C] * _sigmoid(z_ref[:, C:])
        for r0 in range(0, ts, CH):
            acc = jnp.zeros((CH, C), F32)
            for k in range(KW):
                acc = acc + w_ref[k:k + 1, :] * a0_s[r0 + lead + k:r0 + lead + k + CH, :]
            xh, _ = _ln_stats(acc + cb_ref[...])
            a2 = xh * g_ref[...] + b_ref[...]
            o_ref[r0:r0 + CH, :] = (a2 * _sigmoid(a2)).astype(BF16)

    hb = ts // HALO
    vec = pl.BlockSpec((1, C), lambda i: (0, 0))
    return _pcall(body, name=name, grid=(cfg.T // ts,),
                  in_specs=[pl.BlockSpec((ts, 2 * C), lambda i: (i, 0)),
                            pl.BlockSpec((HALO, 2 * C), lambda i: (jnp.maximum(i * hb - 1, 0), 0)),
                            pl.BlockSpec((32, C), lambda i: (0, 0)), vec, vec, vec],
                  out_specs=pl.BlockSpec((ts, C), lambda i: (i, 0)), out_shape=jax.ShapeDtypeStruct((cfg.T, C), BF16),
                  scratch_shapes=[pltpu.VMEM((HALO + ts, C), F32)], compiler_params=_params("parallel"))(zm, zm, w, cb, g, b)


def _conv_a_bwd(da3, zm, w, cb, g, b, cfg, *, name):
    C, KW, HALO, CH = cfg.C, cfg.KW, CONV_A_HALO, CONV_A_CHUNK
    ts = _conv_a_tile(cfg)
    tpb = cfg.S // ts
    nt = cfg.T // ts
    lead = HALO - (KW - 1)
    ext = ts + HALO

    def body(z_ref, zp_ref, zn_ref, d_ref, dn_ref, w_ref, cb_ref, g_ref, b_ref,
             dz_ref, dw_ref, dcb_ref, dg_ref, db_ref, a0_s, d3_s, da1_s):
        i = pl.program_id(0)
        first, last = i % tpb == 0, i % tpb == tpb - 1

        @pl.when(i == 0)
        def _():
            dw_ref[...] = jnp.zeros_like(dw_ref)
            dcb_ref[...] = jnp.zeros_like(dcb_ref)
            dg_ref[...] = jnp.zeros_like(dg_ref)
            db_ref[...] = jnp.zeros_like(db_ref)

        a0_s[0:HALO, :] = jnp.where(first, 0.0, zp_ref[:, :C] * _sigmoid(zp_ref[:, C:]))
        a0_s[HALO:HALO + ts, :] = z_ref[:, :C] * _sigmoid(z_ref[:, C:])
        a0_s[HALO + ts:HALO + ext, :] = zn_ref[:, :C] * _sigmoid(zn_ref[:, C:])
        d3_s[0:ts, :] = d_ref[...]
        d3_s[ts:ext, :] = jnp.where(last, 0.0, dn_ref[...])
        dcb, dg, db = jnp.zeros((1, C), F32), jnp.zeros((1, C), F32), jnp.zeros((1, C), F32)
        for r0 in range(0, ext, CH):
            acc = jnp.zeros((CH, C), F32)
            for k in range(KW):
                acc = acc + w_ref[k:k + 1, :] * a0_s[r0 + lead + k:r0 + lead + k + CH, :]
            xh, rstd = _ln_stats(acc + cb_ref[...])
            a2 = xh * g_ref[...] + b_ref[...]
            sg = _sigmoid(a2)
            da2 = d3_s[r0:r0 + CH, :] * (sg * (1.0 + a2 * (1.0 - sg)))
            da1 = _ln_bwd(da2 * g_ref[...], xh, rstd)
            da1_s[r0:r0 + CH, :] = da1
            if r0 < ts:
                dg = dg + jnp.sum(da2 * xh, axis=0, keepdims=True)
                db = db + jnp.sum(da2, axis=0, keepdims=True)
                dcb = dcb + jnp.sum(da1, axis=0, keepdims=True)
        dg_ref[...] += dg
        db_ref[...] += db
        dcb_ref[...] += dcb
        for k in range(KW):
            dwk = jnp.zeros((1, C), F32)
            for r0 in range(0, ts, CH):
                dwk = dwk + jnp.sum(da1_s[r0:r0 + CH, :] * a0_s[r0 + lead + k:r0 + lead + k + CH, :], axis=0, keepdims=True)
            dw_ref[k:k + 1, :] += dwk
        for r0 in range(0, ts, CH):
            da0 = jnp.zeros((CH, C), F32)
            for k in range(KW):
                da0 = da0 + w_ref[k:k + 1, :] * da1_s[r0 + KW - 1 - k:r0 + KW - 1 - k + CH, :]
            val, sg = z_ref[r0:r0 + CH, :C], _sigmoid(z_ref[r0:r0 + CH, C:])
            dz_ref[r0:r0 + CH, :C] = (da0 * sg).astype(BF16)
            dz_ref[r0:r0 + CH, C:] = (da0 * val * sg * (1.0 - sg)).astype(BF16)

    hb = ts // HALO
    nhb = cfg.T // HALO
    vec = pl.BlockSpec((1, C), lambda i: (0, 0))
    vs = jax.ShapeDtypeStruct((1, C), F32)
    return _pcall(body, name=name, grid=(nt,),
                  in_specs=[pl.BlockSpec((ts, 2 * C), lambda i: (i, 0)),
                            pl.BlockSpec((HALO, 2 * C), lambda i: (jnp.maximum(i * hb - 1, 0), 0)),
                            pl.BlockSpec((HALO, 2 * C), lambda i: (jnp.minimum((i + 1) * hb, nhb - 1), 0)),
                            pl.BlockSpec((ts, C), lambda i: (i, 0)),
                            pl.BlockSpec((HALO, C), lambda i: (jnp.minimum((i + 1) * hb, nhb - 1), 0)),
                            pl.BlockSpec((32, C), lambda i: (0, 0)), vec, vec, vec],
                  out_specs=[pl.BlockSpec((ts, 2 * C), lambda i: (i, 0)), pl.BlockSpec((32, C), lambda i: (0, 0)), vec, vec, vec],
                  out_shape=[jax.ShapeDtypeStruct((cfg.T, 2 * C), BF16), jax.ShapeDtypeStruct((32, C), F32), vs, vs, vs],
                  scratch_shapes=[pltpu.VMEM((HALO + ext, C), F32), pltpu.VMEM((ext, C), F32), pltpu.VMEM((ext, C), F32)],
                  compiler_params=_params("arbitrary"))(zm, zm, zm, da3, da3, w, cb, g, b)


def _cum_tile(cfg):
    return _pick(cfg.S, (256, 128, 64, 32, 16, 8))


def _fgate_fwd(zf, cfg, *, name):
    tc = _cum_tile(cfg)
    tpb = cfg.S // tc
    _, hp = _attn_tiles(cfg)
    nb = cfg.H // hp

    def body(z_ref, o_ref, carry):
        @pl.when(pl.program_id(0) % tpb == 0)
        def _():
            carry[...] = jnp.zeros_like(carry)

        z = z_ref[...]
        logf = jnp.minimum(z, 0.0) - jnp.log(1.0 + jnp.exp(-jnp.abs(z)))
        tri = (lax.broadcasted_iota(jnp.int32, (tc, tc), 0) >= lax.broadcasted_iota(jnp.int32, (tc, tc), 1)).astype(F32)
        cum = jnp.dot(tri, logf, precision=lax.Precision.HIGHEST, preferred_element_type=F32) + carry[...]
        carry[...] = cum[tc - 1:tc, :]
        o_ref[0] = cum
        for b in range(1, nb):
            o_ref[b] = pltpu.roll(cum, LANES - hp * b, axis=1)

    return _pcall(body, name=name, grid=(cfg.T // tc,), in_specs=[pl.BlockSpec((tc, LANES), lambda i: (i, 0))],
                  out_specs=pl.BlockSpec((nb, tc, LANES), lambda i: (0, i, 0)),
                  out_shape=jax.ShapeDtypeStruct((nb, cfg.T, LANES), F32), scratch_shapes=[pltpu.VMEM((1, LANES), F32)],
                  compiler_params=_params("arbitrary"))(zf)


def _fgate_bwd(dcum_c, zf, cfg, *, name):
    tc = _cum_tile(cfg)
    tpb = cfg.S // tc
    nt = cfg.T // tc
    _, hp = _attn_tiles(cfg)
    nb = cfg.H // hp

    def body(d_ref, z_ref, o_ref, carry):
        @pl.when(pl.program_id(0) % tpb == 0)
        def _():
            carry[...] = jnp.zeros_like(carry)

        d = d_ref[0]
        for b in range(1, nb):
            d = d + pltpu.roll(d_ref[b], hp * b, axis=1)
        tri = (lax.broadcasted_iota(jnp.int32, (tc, tc), 0) <= lax.broadcasted_iota(jnp.int32, (tc, tc), 1)).astype(F32)
        suf = jnp.dot(tri, d, precision=lax.Precision.HIGHEST, preferred_element_type=F32) + carry[...]
        o_ref[...] = (suf * _sigmoid(-z_ref[...])).astype(BF16)
        carry[...] = suf[0:1, :]

    blk = pl.BlockSpec((tc, LANES), lambda i: (nt - 1 - i, 0))
    return _pcall(body, name=name, grid=(nt,), in_specs=[pl.BlockSpec((nb, tc, LANES), lambda i: (0, nt - 1 - i, 0)), blk],
                  out_specs=blk, out_shape=jax.ShapeDtypeStruct((cfg.T, LANES), BF16),
                  scratch_shapes=[pltpu.VMEM((1, LANES), F32)], compiler_params=_params("arbitrary"))(dcum_c, zf)


def _attn_tiles(cfg):
    assert LANES % cfg.Dh == 0 and cfg.H % (LANES // cfg.Dh) == 0
    t = _pick(cfg.S, (256, 128))
    return t, LANES // cfg.Dh


BIAS_LANES = 3


def _head_lanes(hd, cfg, hp):
    li = lax.broadcasted_iota(jnp.int32, (1, LANES), 1)
    own = (li >= hd * cfg.Dh) & (li < (hd + 1) * cfg.Dh)
    return own, li, ((hd + 1) % hp) * cfg.Dh


def _q_aug(q, hd, cfg, hp):
    own, li, b0 = _head_lanes(hd, cfg, hp)
    ones = ((li >= b0) & (li < b0 + BIAS_LANES)).astype(F32)
    return jnp.where(own, q * cfg.Dh ** -0.5, ones).astype(BF16)


def _k_aug(k, ck, hd, cfg, hp):
    own, li, b0 = _head_lanes(hd, cfg, hp)
    hi = ck.astype(BF16).astype(F32)
    mid = (ck - hi).astype(BF16).astype(F32)
    lo = ck - hi - mid
    bias = jnp.where(li == b0, -hi, jnp.where(li == b0 + 1, -mid, jnp.where(li == b0 + 2, -lo, 0.0)))
    return jnp.where(own, k, bias).astype(BF16)


def _attn_fwd(zm, cum_c, cfg, *, name):
    S, Dh = cfg.S, cfg.Dh
    t, hp = _attn_tiles(cfg)
    assert hp >= 2
    nq, nb = S // t, cfg.H // hp
    qb, kb, vb = cfg.q_off // LANES, (cfg.q_off + cfg.AW) // LANES, (cfg.q_off + 2 * cfg.AW) // LANES

    def body(q_ref, k_ref, v_ref, cc_ref, o_ref, o32_ref, lse_ref, ka_s, vt_s):
        qi = pl.program_id(2)

        @pl.when(qi == 0)
        def _():
            def prep(c, _):
                r = pl.multiple_of(c * t, t)
                kc = k_ref[pl.ds(r, t), :]
                for hd in range(hp):
                    ka_s[hd, pl.ds(r, t), :] = _k_aug(kc, cc_ref[0, pl.ds(r, t), hd:hd + 1], hd, cfg, hp)
                vt_s[:, pl.ds(r, t)] = v_ref[pl.ds(r, t), :].T.astype(BF16)
                return 0

            lax.fori_loop(0, nq, prep, 0)

        causal = lax.broadcasted_iota(jnp.int32, (t, t), 0) <= lax.broadcasted_iota(jnp.int32, (t, t), 1)
        qf = q_ref[...]
        qa = [_q_aug(qf, hd, cfg, hp) for hd in range(hp)]

        def chunk(j, carry, masked):
            r = pl.multiple_of(j * t, t)
            new = []
            for hd in range(hp):
                m, l, acc = carry[hd]
                s = lax.dot_general(ka_s[hd, pl.ds(r, t), :], qa[hd], NT, preferred_element_type=F32)
                if masked:
                    s = jnp.where(causal, s, NEG)
                m_new = jnp.maximum(m, jnp.max(s, axis=0, keepdims=True))
                a = jnp.exp(m - m_new)
                p = jnp.exp(s - m_new)
                l = a * l + jnp.sum(p, axis=0, keepdims=True)
                p_hi = p.astype(BF16)
                p_lo = (p - p_hi.astype(F32)).astype(BF16)
                vt = vt_s[hd * Dh:(hd + 1) * Dh, pl.ds(r, t)]
                acc = a * acc + (jnp.dot(vt, p_hi, preferred_element_type=F32) + jnp.dot(vt, p_lo, preferred_element_type=F32))
                new.append((m_new, l, acc))
            return tuple(new)

        init = tuple((jnp.full((1, t), NEG, F32), jnp.zeros((1, t), F32), jnp.zeros((Dh, t), F32)) for _ in range(hp))
        res = chunk(qi, lax.fori_loop(0, qi, functools.partial(chunk, masked=False), init), True)
        o = jnp.concatenate([acc / l for _, l, acc in res], axis=0).T
        o_ref[...] = o.astype(BF16)
        o32_ref[...] = o
        lse_ref[...] = jnp.zeros_like(lse_ref)
        for hd in range(hp):
            lse_ref[0, 0, hd:hd + 1, :] = res[hd][0] + jnp.log(res[hd][1])

    return _pcall(body, name=name, grid=(cfg.Bl, nb, nq),
                  in_specs=[pl.BlockSpec((t, LANES), lambda b, h, i: (b * nq + i, qb + h)),
                            pl.BlockSpec((S, LANES), lambda b, h, i: (b, kb + h)),
                            pl.BlockSpec((S, LANES), lambda b, h, i: (b, vb + h)),
                            pl.BlockSpec((1, S, LANES), lambda b, h, i: (h, b, 0))],
                  out_specs=[pl.BlockSpec((t, LANES), lambda b, h, i: (b * nq + i, h)),
                             pl.BlockSpec((t, LANES), lambda b, h, i: (b * nq + i, h)),
                             pl.BlockSpec((1, 1, 8, t), lambda b, h, i: (b, h, 0, i))],
                  out_shape=[jax.ShapeDtypeStruct((cfg.T, cfg.AW), BF16), jax.ShapeDtypeStruct((cfg.T, cfg.AW), F32),
                             jax.ShapeDtypeStruct((cfg.Bl, nb, 8, S), F32)],
                  scratch_shapes=[pltpu.VMEM((hp, S, LANES), BF16), pltpu.VMEM((LANES, S), BF16)],
                  compiler_params=_params("parallel", "parallel", "arbitrary"))(zm, zm, zm, cum_c)


def _attn_bwd(zm, cum_c, o, do, lse, cfg, *, name):
    S, Dh = cfg.S, cfg.Dh
    t, hp = _attn_tiles(cfg)
    nq, nb = S // t, cfg.H // hp
    qb, kb, vb = cfg.q_off // LANES, (cfg.q_off + cfg.AW) // LANES, (cfg.q_off + 2 * cfg.AW) // LANES
    scale = Dh ** -0.5

    def body(q_ref, k_ref, v_ref, cc_ref, o_ref, do_ref, lse_ref, dq_ref, dk_ref, dv_ref, dcc_ref,
             ka_s, qa_s, vz_s, kt_s, dd_s, dqt_s):
        li = lax.broadcasted_iota(jnp.int32, (1, LANES), 1)
        ri = lax.broadcasted_iota(jnp.int32, (LANES, 1), 0)
        causal = lax.broadcasted_iota(jnp.int32, (t, t), 0) <= lax.broadcasted_iota(jnp.int32, (t, t), 1)

        def prep(c, _):
            r = pl.multiple_of(c * t, t)
            kc, vc, qc = k_ref[pl.ds(r, t), :], v_ref[pl.ds(r, t), :], q_ref[pl.ds(r, t), :]
            prod_t = (do_ref[pl.ds(r, t), :].astype(F32) * o_ref[pl.ds(r, t), :].astype(F32)).T
            for hd in range(hp):
                own = _head_lanes(hd, cfg, hp)[0]
                ka_s[hd, pl.ds(r, t), :] = _k_aug(kc, cc_ref[0, pl.ds(r, t), hd:hd + 1], hd, cfg, hp)
                qa_s[hd, pl.ds(r, t), :] = _q_aug(qc, hd, cfg, hp)
                vz_s[hd, pl.ds(r, t), :] = jnp.where(own, vc, 0.0).astype(BF16)
                dd_s[hd:hd + 1, pl.ds(r, t)] = jnp.sum(prod_t[hd * Dh:(hd + 1) * Dh, :], axis=0, keepdims=True)
            kt_s[:, pl.ds(r, t)] = kc.T.astype(BF16)
            dqt_s[:, pl.ds(r, t)] = jnp.zeros((LANES, t), F32)
            return 0

        lax.fori_loop(0, nq, prep, 0)

        def kv_step(j, _):
            rk = pl.multiple_of(j * t, t)

            def tile(i, carry, masked):
                rq = pl.multiple_of(i * t, t)
                dob = do_ref[pl.ds(rq, t), :]
                new, dq_t = [], None
                for hd in range(hp):
                    dk_h, dv_h, dsum_h = carry[hd]
                    qa = qa_s[hd, pl.ds(rq, t), :]
                    s = lax.dot_general(ka_s[hd, pl.ds(rk, t), :], qa, NT, preferred_element_type=F32)
                    p = jnp.exp(s - lse_ref[0, 0, hd:hd + 1, pl.ds(rq, t)])
                    if masked:
                        p = jnp.where(causal, p, 0.0)
                    dp = lax.dot_general(vz_s[hd, pl.ds(rk, t), :], dob, NT, preferred_element_type=F32)
                    ds = p * (dp - dd_s[hd:hd + 1, pl.ds(rq, t)])
                    dsb = ds.astype(BF16)
                    dv_h = dv_h + jnp.dot(p.astype(BF16), dob, preferred_element_type=F32)
                    dk_h = dk_h + jnp.dot(dsb, qa, preferred_element_type=F32)
                    dq_h = jnp.dot(kt_s[:, pl.ds(rk, t)], dsb, preferred_element_type=F32)
                    dq_t = dq_h if hd == 0 else jnp.where((ri >= hd * Dh) & (ri < (hd + 1) * Dh), dq_h, dq_t)
                    for c0 in range(0, t, LANES):
                        dsum_h = dsum_h + ds[:, c0:c0 + LANES]
                    new.append((dk_h, dv_h, dsum_h))
                dqt_s[:, pl.ds(rq, t)] += dq_t * scale
                return tuple(new)

            zero = tuple((jnp.zeros((t, LANES), F32),) * 3 for _ in range(hp))
            res = lax.fori_loop(j + 1, nq, functools.partial(tile, masked=False), tile(j, zero, True))
            dk, dv, dcc = res[0][0], res[0][1], jnp.zeros((t, LANES), F32)
            for hd in range(hp):
                own = _head_lanes(hd, cfg, hp)[0]
                if hd > 0:
                    dk, dv = jnp.where(own, res[hd][0], dk), jnp.where(own, res[hd][1], dv)
                dcc = dcc + jnp.where(li == hd, -jnp.sum(res[hd][2], axis=1, keepdims=True), 0.0)
            dk_ref[pl.ds(rk, t), :] = dk.astype(BF16)
            dv_ref[pl.ds(rk, t), :] = dv.astype(BF16)
            dcc_ref[0, pl.ds(rk, t), :] = dcc
            return 0

        lax.fori_loop(0, nq, kv_step, 0)

        def finish(c, _):
            r = pl.multiple_of(c * t, t)
            dq_ref[pl.ds(r, t), :] = dqt_s[:, pl.ds(r, t)].T.astype(BF16)
            return 0

        lax.fori_loop(0, nq, finish, 0)

    blk = pl.BlockSpec((S, LANES), lambda b, h: (b, h))
    cc = pl.BlockSpec((1, S, LANES), lambda b, h: (h, b, 0))
    os_ = jax.ShapeDtypeStruct((cfg.T, cfg.AW), BF16)
    return _pcall(body, name=name, grid=(cfg.Bl, nb),
                  in_specs=[pl.BlockSpec((S, LANES), lambda b, h: (b, qb + h)), pl.BlockSpec((S, LANES), lambda b, h: (b, kb + h)),
                            pl.BlockSpec((S, LANES), lambda b, h: (b, vb + h)), cc, blk, blk,
                            pl.BlockSpec((1, 1, 8, S), lambda b, h: (b, h, 0, 0))],
                  out_specs=[blk, blk, blk, cc],
                  out_shape=[os_, os_, os_, jax.ShapeDtypeStruct((nb, cfg.T, LANES), F32)],
                  scratch_shapes=[pltpu.VMEM((hp, S, LANES), BF16)] * 3 + [pltpu.VMEM((LANES, S), BF16),
                                  pltpu.VMEM((8, S), F32), pltpu.VMEM((LANES, S), F32)],
                  compiler_params=_params("parallel", "parallel"))(zm, zm, zm, cum_c, o, do, lse)


FFN_HALO = 8


def _ffn_tiles(cfg):
    assert cfg.KF - 1 <= FFN_HALO
    return _pick(cfg.S, (512, 256, 128, 64, 32, 16, 8)), _pick(cfg.F, (256, 128))


def _gelu(x):
    return 0.5 * x * (1.0 + lax.erf(x * (2.0 ** -0.5)))


def _gelu_grad(x):
    return 0.5 * (1.0 + lax.erf(x * (2.0 ** -0.5))) + x * jnp.exp(-0.5 * x * x) * ((2.0 * math.pi) ** -0.5)


def _ffn_conv_fwd(h0, w, cb, cfg, *, name):
    KF, HALO = cfg.KF, FFN_HALO
    ts, tf = _ffn_tiles(cfg)
    tpb, nf = cfg.S // ts, cfg.F // tf
    lead = HALO - (KF - 1)

    def body(g_ref, gp_ref, l_ref, lp_ref, wg_ref, wl_ref, cg_ref, cl_ref, o_ref, g_s, l_s):
        first = pl.program_id(1) % tpb == 0
        g_s[0:HALO, :] = jnp.where(first, 0.0, gp_ref[...])
        l_s[0:HALO, :] = jnp.where(first, 0.0, lp_ref[...])
        g_s[HALO:HALO + ts, :] = g_ref[...]
        l_s[HALO:HALO + ts, :] = l_ref[...]
        hg, hl = cg_ref[...], cl_ref[...]
        for k in range(KF):
            hg = hg + wg_ref[k:k + 1, :] * g_s[lead + k:lead + k + ts, :]
            hl = hl + wl_ref[k:k + 1, :] * l_s[lead + k:lead + k + ts, :]
        o_ref[...] = (_gelu(hg) * hl).astype(BF16)

    hb = ts // HALO
    prev = lambda off: pl.BlockSpec((HALO, tf), lambda j, i: (jnp.maximum(i * hb - 1, 0), off + j))
    main = lambda off: pl.BlockSpec((ts, tf), lambda j, i: (i, off + j))
    wsp = lambda off: pl.BlockSpec((8, tf), lambda j, i: (0, off + j))
    vsp = lambda off: pl.BlockSpec((1, tf), lambda j, i: (0, off + j))
    return _pcall(body, name=name, grid=(nf, cfg.T // ts),
                  in_specs=[main(0), prev(0), main(nf), prev(nf), wsp(0), wsp(nf), vsp(0), vsp(nf)],
                  out_specs=pl.BlockSpec((ts, tf), lambda j, i: (i, j)), out_shape=jax.ShapeDtypeStruct((cfg.T, cfg.F), BF16),
                  scratch_shapes=[pltpu.VMEM((HALO + ts, tf), F32)] * 2,
                  compiler_params=_params("parallel", "parallel"))(h0, h0, h0, h0, w, w, cb, cb)


def _ffn_conv_bwd(df, h0, w, cb, cfg, *, name):
    KF, HALO = cfg.KF, FFN_HALO
    ts, tf = _ffn_tiles(cfg)
    tpb, nf = cfg.S // ts, cfg.F // tf
    lead = HALO - (KF - 1)
    ext = ts + HALO

    def body(g_ref, gp_ref, gn_ref, l_ref, lp_ref, ln_ref, d_ref, dn_ref, wg_ref, wl_ref, cg_ref, cl_ref,
             dg_ref, dl_ref, dwg_ref, dwl_ref, dcg_ref, dcl_ref, g_s, l_s, d_s, dhg_s, dhl_s):
        i = pl.program_id(1)
        first, last = i % tpb == 0, i % tpb == tpb - 1

        @pl.when(i == 0)
        def _():
            dwg_ref[...] = jnp.zeros_like(dwg_ref)
            dwl_ref[...] = jnp.zeros_like(dwl_ref)
            dcg_ref[...] = jnp.zeros_like(dcg_ref)
            dcl_ref[...] = jnp.zeros_like(dcl_ref)

        for s, main, prev, nxt in ((g_s, g_ref, gp_ref, gn_ref), (l_s, l_ref, lp_ref, ln_ref)):
            s[0:HALO, :] = jnp.where(first, 0.0, prev[...])
            s[HALO:HALO + ts, :] = main[...]
            s[HALO + ts:HALO + ext, :] = nxt[...]
        d_s[0:ts, :] = d_ref[...]
        d_s[ts:ext, :] = jnp.where(last, 0.0, dn_ref[...])
        hg, hl = cg_ref[...], cl_ref[...]
        for k in range(KF):
            hg = hg + wg_ref[k:k + 1, :] * g_s[lead + k:lead + k + ext, :]
            hl = hl + wl_ref[k:k + 1, :] * l_s[lead + k:lead + k + ext, :]
        df_e = d_s[...]
        dhg_s[...] = df_e * hl * _gelu_grad(hg)
        dhl_s[...] = df_e * _gelu(hg)
        for dh_s, h_s, w_ref, dx_ref, dw_ref, dc_ref in ((dhg_s, g_s, wg_ref, dg_ref, dwg_ref, dcg_ref),
                                                         (dhl_s, l_s, wl_ref, dl_ref, dwl_ref, dcl_ref)):
            dh = dh_s[0:ts, :]
            dc_ref[...] += jnp.sum(dh, axis=0, keepdims=True)
            dx = jnp.zeros((ts, tf), F32)
            for k in range(KF):
                dw_ref[k:k + 1, :] += jnp.sum(dh * h_s[lead + k:lead + k + ts, :], axis=0, keepdims=True)
                dx = dx + w_ref[k:k + 1, :] * dh_s[KF - 1 - k:KF - 1 - k + ts, :]
            dx_ref[...] = dx.astype(BF16)

    hb = ts // HALO
    nhb = cfg.T // HALO
    main = lambda off: pl.BlockSpec((ts, tf), lambda j, i: (i, off + j))
    prev = lambda off: pl.BlockSpec((HALO, tf), lambda j, i: (jnp.maximum(i * hb - 1, 0), off + j))
    nxt = lambda off: pl.BlockSpec((HALO, tf), lambda j, i: (jnp.minimum((i + 1) * hb, nhb - 1), off + j))
    wsp = lambda off: pl.BlockSpec((8, tf), lambda j, i: (0, off + j))
    vsp = lambda off: pl.BlockSpec((1, tf), lambda j, i: (0, off + j))
    dxs, dws, dcs = (jax.ShapeDtypeStruct((cfg.T, cfg.F), BF16), jax.ShapeDtypeStruct((8, cfg.F), F32),
                     jax.ShapeDtypeStruct((1, cfg.F), F32))
    return _pcall(body, name=name, grid=(nf, cfg.T // ts),
                  in_specs=[main(0), prev(0), nxt(0), main(nf), prev(nf), nxt(nf), main(0), nxt(0),
                            wsp(0), wsp(nf), vsp(0), vsp(nf)],
                  out_specs=[main(0), main(0), wsp(0), wsp(0), vsp(0), vsp(0)],
                  out_shape=[dxs, dxs, dws, dws, dcs, dcs],
                  scratch_shapes=[pltpu.VMEM((HALO + ext, tf), F32)] * 2 + [pltpu.VMEM((ext, tf), F32)] * 3,
                  compiler_params=_params("parallel", "arbitrary"))(h0, h0, h0, h0, h0, h0, df, df, w, w, cb, cb)


def _ada_fwd(c_all, w, b, *, name):
    L, D, n = w.shape
    B = c_all.shape[0]

    def body(c_ref, w_ref, b_ref, o_ref):
        c = c_ref[...]
        act = (c * _sigmoid(c)).astype(BF16)
        o_ref[0] = jnp.dot(act, w_ref[0].astype(BF16), preferred_element_type=F32) + b_ref[0]

    return _pcall(body, name=name, grid=(L,),
                  in_specs=[pl.BlockSpec((B, D), lambda l: (0, 0)), pl.BlockSpec((1, D, n), lambda l: (l, 0, 0)),
                            pl.BlockSpec((1, 1, n), lambda l: (l, 0, 0))],
                  out_specs=pl.BlockSpec((1, B, n), lambda l: (l, 0, 0)), out_shape=jax.ShapeDtypeStruct((L, B, n), F32),
                  compiler_params=_params("parallel"))(c_all, w, b)


def _ada_bwd(c_all, dmod, *, name):
    L, B, n = dmod.shape
    D = c_all.shape[1]

    def body(c_ref, d_ref, o_ref):
        c = c_ref[...]
        act = (c * _sigmoid(c)).astype(BF16)
        o_ref[0] = lax.dot_general(act, d_ref[0].astype(BF16), TN, preferred_element_type=F32)

    return _pcall(body, name=name, grid=(L,),
                  in_specs=[pl.BlockSpec((B, D), lambda l: (0, 0)), pl.BlockSpec((1, B, n), lambda l: (l, 0, 0))],
                  out_specs=pl.BlockSpec((1, D, n), lambda l: (l, 0, 0)), out_shape=jax.ShapeDtypeStruct((L, D, n), F32),
                  compiler_params=_params("parallel"))(c_all, dmod)


def _slot_sum(x, *, name):
    n, R, W = x.shape
    tr = _pick(R, (256, 128, 64, 32, 16, 8))

    def body(x_ref, o_ref):
        acc = x_ref[0].astype(F32)
        for k in range(1, n):
            acc = acc + x_ref[k].astype(F32)
        o_ref[...] = acc

    return _pcall(body, name=name, grid=(R // tr,), in_specs=[pl.BlockSpec((n, tr, W), lambda i: (0, i, 0))],
                  out_specs=pl.BlockSpec((tr, W), lambda i: (i, 0)), out_shape=jax.ShapeDtypeStruct((R, W), F32),
                  compiler_params=_params("parallel"))(x)


def _adamw(gs, w, m, v, *, name):
    n, R, W = gs.shape
    tr = _pick(R, (256, 128, 64, 32, 16, 8))
    c1, c2 = 1.0 - ADAM_B1 ** ADAM_STEP, 1.0 - ADAM_B2 ** ADAM_STEP

    def body(g_ref, w_ref, m_ref, v_ref, go_ref, d_ref, mo_ref, vo_ref):
        g = g_ref[0].astype(F32)
        for k in range(1, n):
            g = g + g_ref[k].astype(F32)
        m2 = ADAM_B1 * m_ref[...] + (1.0 - ADAM_B1) * g
        v2 = ADAM_B2 * v_ref[...] + (1.0 - ADAM_B2) * (g * g)
        go_ref[...] = g
        mo_ref[...] = m2
        vo_ref[...] = v2
        d_ref[...] = -ADAM_LR * ((m2 / c1) / (jnp.sqrt(v2 / c2) + ADAM_EPS) + ADAM_WD * w_ref[...])

    blk = pl.BlockSpec((tr, W), lambda i: (i, 0))
    o = jax.ShapeDtypeStruct((R, W), F32)
    return _pcall(body, name=name, grid=(R // tr,), in_specs=[pl.BlockSpec((n, tr, W), lambda i: (0, i, 0)), blk, blk, blk],
                  out_specs=[blk] * 4, out_shape=[o] * 4, compiler_params=_params("parallel"))(gs, w, m, v)


def _peer_copies(x_ref, land_ref, send_sems, recv_sems, all_to_all):
    mx, my, mc = lax.axis_index("x"), lax.axis_index("y"), lax.axis_index("c")
    me = 4 * mx + 2 * my + mc
    copies = []
    for k in range(1, N_DEV):
        px, py, pc = mx ^ ((k >> 2) & 1), my ^ ((k >> 1) & 1), mc ^ (k & 1)
        copies.append(pltpu.make_async_remote_copy(
            src_ref=x_ref.at[4 * px + 2 * py + pc] if all_to_all else x_ref, dst_ref=land_ref.at[me],
            send_sem=send_sems.at[k - 1], recv_sem=recv_sems.at[k - 1], device_id=(px, py, pc),
            device_id_type=pl.DeviceIdType.MESH))
    return copies


_HBM = pl.BlockSpec(memory_space=pltpu.HBM)
_SEM = pl.BlockSpec(memory_space=pltpu.SEMAPHORE)
_EFFECT = pltpu.SideEffectType.DATAFLOW_SIDE_EFFECTING


def _exchange_start(x, *, all_to_all, name, after=None):
    blk = x.shape[1:] if all_to_all else x.shape
    land = lax.empty((N_DEV,) + tuple(blk), x.dtype)
    has_after = after is not None

    def body(*refs):
        x_ref, land_ref = refs[0], refs[1]
        send_sems, recv_sems, _, _, token = refs[2 + has_after:]
        for cp in _peer_copies(x_ref, land_ref, send_sems, recv_sems, all_to_all):
            cp.start()
        token[...] = jnp.zeros_like(token)

    n_sem = pltpu.SemaphoreType.DMA((N_DEV - 1,))
    args = [pltpu.with_memory_space_constraint(x, pltpu.HBM), pltpu.with_memory_space_constraint(land, pltpu.HBM)]
    in_specs = [_HBM, _HBM]
    if has_after:
        args.append(after)
        in_specs.append(pl.BlockSpec(memory_space=pl.ANY))
    send_sems, recv_sems, x_thru, land_thru, token = _pcall(
        body, name=name, in_specs=in_specs,
        out_shape=(n_sem, n_sem, pltpu.HBM(x.shape, x.dtype), pltpu.HBM(land.shape, land.dtype),
                   jax.ShapeDtypeStruct((8, LANES), F32)),
        out_specs=(_SEM, _SEM, _HBM, _HBM, pl.BlockSpec(memory_space=pltpu.VMEM)), input_output_aliases={0: 2, 1: 3},
        compiler_params=pltpu.CompilerParams(has_side_effects=_EFFECT))(*args)
    return (send_sems, recv_sems, x_thru, land_thru, all_to_all), token


def _exchange_wait(state, after, *, name):
    send_sems, recv_sems, x_thru, land_thru, all_to_all = state

    def body(x_ref, land_ref, send_sems, recv_sems, after_ref, x_dead, landed):
        for cp in _peer_copies(x_ref, land_ref, send_sems, recv_sems, all_to_all):
            cp.wait_send()
            cp.wait_recv()

    return _pcall(
        body, name=name, in_specs=(_HBM, _HBM, _SEM, _SEM, pl.BlockSpec(memory_space=pl.ANY)),
        out_shape=(pltpu.HBM(x_thru.shape, x_thru.dtype), pltpu.HBM(land_thru.shape, land_thru.dtype)),
        out_specs=(_HBM, _HBM), input_output_aliases={0: 0, 1: 1},
        compiler_params=pltpu.CompilerParams(has_side_effects=_EFFECT))(x_thru, land_thru, send_sems, recv_sems, after)[1]


def _exchange(x, *, all_to_all, name, after=None):
    blk = x.shape[1:] if all_to_all else x.shape

    def body(x_ref, *rest):
        o_ref, send_sems, recv_sems, local_sem = rest[-4:]
        me = 4 * lax.axis_index("x") + 2 * lax.axis_index("y") + lax.axis_index("c")
        mine = pltpu.make_async_copy(x_ref.at[me] if all_to_all else x_ref, o_ref.at[me], local_sem)
        mine.start()
        copies = _peer_copies(x_ref, o_ref, send_sems, recv_sems, all_to_all)
        for cp in copies:
            cp.start()
        for cp in copies:
            cp.wait()
        mine.wait()

    anyspec = pl.BlockSpec(memory_space=pl.ANY)
    args = [x] if after is None else [x, after]
    return _pcall(body, name=name, in_specs=[anyspec] * len(args), out_specs=anyspec,
                  out_shape=jax.ShapeDtypeStruct((N_DEV,) + tuple(blk), x.dtype),
                  scratch_shapes=[pltpu.SemaphoreType.DMA((N_DEV - 1,)), pltpu.SemaphoreType.DMA((N_DEV - 1,)),
                                  pltpu.SemaphoreType.DMA(())])(*args)


PACK_ROWS = 16


def _pack(arrs, width, dtype, lead=0):
    parts, segs, r = [], [], 0
    for a in arrs:
        lshape, shape = a.shape[:lead], a.shape[lead:]
        n = math.prod(shape)
        rows = -(-n // width)
        rows_p = -(-rows // PACK_ROWS) * PACK_ROWS
        flat = a.reshape(lshape + (n,)).astype(dtype)
        flat = jnp.pad(flat, [(0, 0)] * lead + [(0, rows_p * width - n)])
        parts.append(flat.reshape(lshape + (rows_p, width)))
        segs.append((r, n, shape))
        r += rows_p
    return jnp.concatenate(parts, axis=lead), segs


def _unpack(p, segs):
    lshape, width = p.shape[:-2], p.shape[-1]
    outs = []
    for r, n, shape in segs:
        rows = -(-n // width)
        blk = p[..., r:r + rows, :].reshape(lshape + (rows * width,))
        outs.append(blk[..., :n].reshape(lshape + shape))
    return outs


def _split_cols(a, f_off, h):
    return jnp.concatenate([a[..., :f_off], a[..., f_off + h:]], axis=-1), a[..., f_off:f_off + h]


def _merge_cols(main, f, f_off):
    return jnp.concatenate([main[..., :f_off], f, main[..., f_off:]], axis=-1)


def _pad_to(a, n, axis):
    pad = [(0, 0)] * a.ndim
    pad[axis] = (0, n - a.shape[axis])
    return jnp.pad(a, pad)


def kernel(x, c, w_ada, b_ada, w_in, b_in, conv_a_w, conv_a_b, ln_conv_g, ln_conv_b, w_conv_proj, w_attn_proj, w_mix_out, b_mix_out, ln1_g, ln1_b, w_ffn_up, ffn_conv_w, ffn_conv_b, w_ffn_down, ln2_g, ln2_b, loss_target, m_w_ada, m_b_ada, m_w_in, m_b_in, m_conv_a_w, m_conv_a_b, m_ln_conv_g, m_ln_conv_b, m_w_conv_proj, m_w_attn_proj, m_w_mix_out, m_b_mix_out, m_ln1_g, m_ln1_b, m_w_ffn_up, m_ffn_conv_w, m_ffn_conv_b, m_w_ffn_down, m_ln2_g, m_ln2_b, v_w_ada, v_b_ada, v_w_in, v_b_in, v_conv_a_w, v_conv_a_b, v_ln_conv_g, v_ln_conv_b, v_w_conv_proj, v_w_attn_proj, v_w_mix_out, v_b_mix_out, v_ln1_g, v_ln1_b, v_w_ffn_up, v_ffn_conv_w, v_ffn_conv_b, v_w_ffn_down, v_ln2_g, v_ln2_b):
    L, D = w_ada.shape[0], w_ada.shape[1]
    Bl, S, _ = x.shape
    C, KW, AW = conv_a_b.shape[1], conv_a_w.shape[1], w_attn_proj.shape[1]
    F, KF, n_in_all = ffn_conv_b.shape[1] // 2, ffn_conv_w.shape[1], b_in.shape[1]
    H = n_in_all - 2 * C - 3 * AW - 2 * D
    cfg = Cfg(L=L, Bl=Bl, S=S, D=D, C=C, KW=KW, H=H, Dh=AW // H, F=F, KF=KF)
    T, NM = cfg.T, cfg.NM
    f_off = 2 * C + 3 * AW
    n_ada = w_ada.shape[2]
    me = 4 * lax.axis_index("x") + 2 * lax.axis_index("y") + lax.axis_index("c")

    def my_cols(a, n):
        return lax.dynamic_slice_in_dim(a, me * n, n, axis=a.ndim - 1)

    big_names = ["w_in", "w_conv_proj", "w_attn_proj", "w_mix_out", "w_ffn_up", "w_ffn_down"]
    transposed = (True, True, True, False, True, False)

    def shard_items(arrs, grp):
        return [arrs[i][l].T if transposed[i] else arrs[i][l] for l, i in grp]

    w_groups = [[(0, 0)], [(0, i) for i in range(1, 6)]] + [[(l, i) for i in range(6)] for l in range(1, L)]
    w_state, token = [], None
    for gi, grp in enumerate(w_groups):
        pack, segs = _pack(shard_items((w_in, w_conv_proj, w_attn_proj, w_mix_out, w_ffn_up, w_ffn_down), grp), D, BF16)
        state, token = _exchange_start(pack, all_to_all=False, name=f"gather_weights_start_{gi}", after=token)
        w_state.append((state, pack, segs, grp))
    W = [dict() for _ in range(L)]

    def wait_weights(gi, after):
        state, pack, segs, grp = w_state[gi]
        landed = _exchange_wait(state, after, name=f"gather_weights_wait_{gi}")
        landed = lax.dynamic_update_index_in_dim(landed, pack, me, 0)
        for (l, i), a in zip(grp, _unpack(landed, segs)):
            a = a.reshape((-1, a.shape[-1]))
            if i == 0:
                wm_t, wf_t = _split_cols(a.T, f_off, H)
                bm, bf = _split_cols(b_in[l], f_off, H)
                W[l].update(wm_t=wm_t.T, wf_t=_pad_to(wf_t.T, LANES, 0), bm=bm[None], bf=_pad_to(bf, LANES, 0)[None])
            else:
                W[l][("w_cp_t", "w_ap_t", "w_mo", "w_up_t", "w_dn")[i - 1]] = a

    spack, ssegs = _pack([c, conv_a_w, ffn_conv_w], D, F32)
    c_g, caw_g, fcw_g = _unpack(_exchange(spack, all_to_all=False, name="gather_small", after=token), ssegs)
    c_all = c_g.reshape(N_DEV * Bl, D)
    caw = _pad_to(jnp.moveaxis(caw_g, 0, 2).reshape(L, KW, C), 32, 1)
    fcw = _pad_to(jnp.moveaxis(fcw_g, 0, 2).reshape(L, KF, 2 * F), 8, 1)

    mod_part = _ada_fwd(c_all, w_ada, my_cols(b_ada, n_ada)[:, None, :], name="ada_fwd")
    mod_send = jnp.moveaxis(mod_part.reshape(L, N_DEV, Bl, n_ada), 1, 0).reshape(N_DEV, L * Bl, n_ada)
    mod_recv = _exchange(mod_send, all_to_all=True, name="exchange_mod")
    mod = jnp.moveaxis(mod_recv.reshape(N_DEV, L, Bl, n_ada), 0, 2).reshape(L, Bl, 6, 1, D)
    shift1, scale1, gate1, shift2, scale2, gate2 = (mod[:, :, i] for i in range(6))

    xf = x.reshape(T, D)
    u = _ln_mod_fwd(xf, shift1[0], scale1[0], cfg, name="ln_mod_fwd")
    saved = []
    xin = xf
    for l in range(L):
        w = W[l]
        wait_weights(0 if l == 0 else l + 1, u)
        zm = _matmul(u, w["wm_t"], mode="nt", bias=w["bm"], name=f"in_proj_{l}")
        zf = _matmul(u, w["wf_t"], mode="nt", bias=w["bf"], name=f"in_proj_f_{l}")
        a3 = _conv_a_fwd(zm, caw[l], conv_a_b[l][None], ln_conv_g[l][None], ln_conv_b[l][None], cfg, name=f"conv_a_fwd_{l}")
        cum_c = _fgate_fwd(zf, cfg, name=f"fgate_fwd_{l}")
        o, o32, lse = _attn_fwd(zm, cum_c, cfg, name=f"attn_fwd_{l}")
        if l == 0:
            wait_weights(1, o)
        ya = _matmul(a3, w["w_cp_t"], mode="nt", name=f"conv_proj_{l}")
        yb = _matmul(o, w["w_ap_t"], mode="nt", name=f"attn_proj_{l}")
        mg = _merge_fwd(zm, ya, yb, cfg, name=f"merge_fwd_{l}")
        mix = _matmul(mg, w["w_mo"], mode="nn", bias=b_mix_out[l][None], name=f"mix_out_{l}")
        x1, u2 = _res_ln_fwd(xin, mix, gate1[l], ln1_g[l][None], ln1_b[l][None], cfg, name=f"res_ln1_fwd_{l}",
                             nxt=(shift2[l], scale2[l]))
        h0 = _matmul(u2, w["w_up_t"], mode="nt", name=f"ffn_up_{l}")
        fa = _ffn_conv_fwd(h0, fcw[l], ffn_conv_b[l][None], cfg, name=f"ffn_conv_fwd_{l}")
        ffn = _matmul(fa, w["w_dn"], mode="nn", name=f"ffn_down_{l}")
        saved.append(dict(x=xin, u=u, zm=zm, zf=zf, a3=a3, cum_c=cum_c, o=o, o32=o32, lse=lse, ya=ya, yb=yb, mg=mg, mix=mix,
                          x1=x1, u2=u2, h0=h0, fa=fa, ffn=ffn))
        if l + 1 < L:
            xin, u = _res_ln_fwd(x1, ffn, gate2[l], ln2_g[l][None], ln2_b[l][None], cfg, name=f"res_ln2_fwd_{l}",
                                 nxt=(shift1[l + 1], scale1[l + 1]))
        else:
            xin = _res_ln_fwd(x1, ffn, gate2[l], ln2_g[l][None], ln2_b[l][None], cfg, name=f"res_ln2_fwd_{l}")

    dx, loss_tiles = _loss_grad(xin, loss_target.reshape(T, D), cfg, name="loss_grad")
    loss = lax.psum(0.5 / D * jnp.sum(loss_tiles[:, 0, 0]), ("x", "y", "c"))

    gbig = {}
    g_groups = [[(l, i) for i in range(6)] for l in reversed(range(1, L))] + [[(0, 4), (0, 5)], [(0, 1), (0, 2), (0, 3)], [(0, 0)]]
    g_state = []

    def start_grads():
        grp = g_groups[len(g_state)]
        send, segs = _pack([gbig[k].reshape((N_DEV, -1, gbig[k].shape[1])) for k in grp], D, BF16, lead=1)
        state, tok = _exchange_start(send, all_to_all=True, name=f"exchange_grads_start_{len(g_state)}")
        g_state.append((state, send, segs, grp))
        return tok

    gsm = [dict() for _ in range(L)]
    dmods = [None] * L
    token = None
    for l in reversed(range(L)):
        w, s = W[l], saved[l]
        dres2, dffn, dg2, db2, dgate2, _ = _res_ln_bwd(dx, s["x1"], s["ffn"], gate2[l], ln2_g[l][None], cfg, name=f"res_ln2_bwd_{l}")
        dfa = _matmul(dffn, w["w_dn"], mode="nt", name=f"d_ffn_act_{l}", after=token)
        gbig[l, 5] = _matmul(s["fa"], dffn, mode="tn", name=f"dw_ffn_down_{l}")
        dh0g, dh0l, dwg, dwl, dcg, dcl = _ffn_conv_bwd(dfa, s["h0"], fcw[l], ffn_conv_b[l][None], cfg, name=f"ffn_conv_bwd_{l}")
        dh0 = jnp.concatenate([dh0g, dh0l], axis=1)
        du2 = _matmul(dh0, w["w_up_t"], mode="nn", name=f"d_u2_{l}")
        gbig[l, 4] = _matmul(dh0, s["u2"], mode="tn", name=f"dw_ffn_up_{l}")
        token = start_grads() if l == 0 else None
        dx1, dscale2, dshift2 = _ln_mod_bwd(du2, s["x1"], scale2[l], dres2, cfg, name=f"ln_mod2_bwd_{l}")
        dres1, dmix, dg1, db1, dgate1, dbmo = _res_ln_bwd(dx1, s["x"], s["mix"], gate1[l], ln1_g[l][None], cfg, name=f"res_ln1_bwd_{l}")
        dmg = _matmul(dmix, w["w_mo"], mode="nt", name=f"d_merge_{l}", after=token)
        gbig[l, 3] = _matmul(s["mg"], dmix, mode="tn", name=f"dw_mix_out_{l}")
        dya, dyb, dzga, dzgb = _merge_bwd(dmg, s["zm"], s["ya"], s["yb"], cfg, name=f"merge_bwd_{l}")
        gbig[l, 1] = _matmul(dya, s["a3"], mode="tn", name=f"dw_conv_proj_{l}")
        da3 = _matmul(dya, w["w_cp_t"], mode="nn", name=f"d_a3_{l}")
        gbig[l, 2] = _matmul(dyb, s["o"], mode="tn", name=f"dw_attn_proj_{l}")
        token = start_grads() if l == 0 else None
        do = _matmul(dyb, w["w_ap_t"], mode="nn", out_dtype=BF16, name=f"d_o_{l}", after=token)
        dq, dk, dv, dcum_c = _attn_bwd(s["zm"], s["cum_c"], s["o32"], do, s["lse"], cfg, name=f"attn_bwd_{l}")
        dzf = _fgate_bwd(dcum_c, s["zf"], cfg, name=f"fgate_bwd_{l}")
        dzglu, dcaw, dcab, dlcg, dlcb = _conv_a_bwd(da3, s["zm"], caw[l], conv_a_b[l][None], ln_conv_g[l][None],
                                                    ln_conv_b[l][None], cfg, name=f"conv_a_bwd_{l}")
        dzm = jnp.concatenate([dzglu, dq, dk, dv, dzga, dzgb], axis=1)
        du1 = _matmul(dzf, w["wf_t"], mode="nn", name=f"d_u1_f_{l}")
        du1 = _matmul(dzm, w["wm_t"], mode="nn", add=du1, name=f"d_u1_{l}")
        dwm_t = _matmul(dzm, s["u"], mode="tn", name=f"dw_in_{l}")
        dwf_t = _matmul(dzf, s["u"], mode="tn", name=f"dw_in_f_{l}")
        gbig[l, 0] = _merge_cols(dwm_t.T, dwf_t[:H].T, f_off).T
        token = start_grads()
        dbm, dbf = _colsum(dzm, name=f"db_in_{l}"), _colsum(dzf, name=f"db_in_f_{l}")
        dx, dscale1, dshift1 = _ln_mod_bwd(du1, s["x"], scale1[l], dres1, cfg, name=f"ln_mod1_bwd_{l}")
        dmods[l] = jnp.concatenate([dshift1, dscale1, dgate1, dshift2, dscale2, dgate2], axis=1).reshape(Bl, 6 * D)
        gsm[l] = dict(b_in=_merge_cols(dbm[0], dbf[0, :H], f_off), conv_a_b=dcab[0], ln_conv_g=dlcg[0], ln_conv_b=dlcb[0],
                      b_mix_out=dbmo[0], ln1_g=dg1[0], ln1_b=db1[0], ffn_conv_b=jnp.concatenate([dcg[0], dcl[0]]),
                      ln2_g=dg2[0], ln2_b=db2[0], conv_a_w=dcaw[:KW], ffn_conv_w=jnp.concatenate([dwg[:KF], dwl[:KF]], axis=1))
    grad_x = dx.reshape(Bl, S, D)

    small_names = ["b_in", "conv_a_b", "ln_conv_g", "ln_conv_b", "b_mix_out", "ln1_g", "ln1_b", "ffn_conv_b", "ln2_g", "ln2_b",
                   "conv_a_w", "ffn_conv_w"]
    gs_list = [jnp.stack(dmods)] + [jnp.stack([gsm[l][n] for l in range(L)]) for n in small_names]
    gspack, gssegs = _pack(gs_list, D, F32)
    gs_all = _exchange(gspack, all_to_all=False, name="gather_small_grads", after=token)
    dmod_all = jnp.moveaxis(_unpack(gs_all, gssegs)[0], 0, 1).reshape(L, N_DEV * Bl, 6 * D)
    g_small = dict(zip(small_names, _unpack(_slot_sum(gs_all, name="sum_small_grads"), gssegs)[1:]))
    g_small["conv_a_w"] = my_cols(g_small["conv_a_w"], C // N_DEV)
    g_small["ffn_conv_w"] = my_cols(g_small["ffn_conv_w"], 2 * F // N_DEV)
    g_small["w_ada"] = _ada_bwd(c_all, my_cols(dmod_all, n_ada), name="ada_bwd")
    g_small["b_ada"] = jnp.stack([_colsum(dmod_all[l], name=f"db_ada_{l}")[0] for l in range(L)])

    given = dict(w_in=(w_in, m_w_in, v_w_in), w_conv_proj=(w_conv_proj, m_w_conv_proj, v_w_conv_proj),
                 w_attn_proj=(w_attn_proj, m_w_attn_proj, v_w_attn_proj), w_mix_out=(w_mix_out, m_w_mix_out, v_w_mix_out),
                 w_ffn_up=(w_ffn_up, m_w_ffn_up, v_w_ffn_up), w_ffn_down=(w_ffn_down, m_w_ffn_down, v_w_ffn_down),
                 w_ada=(w_ada, m_w_ada, v_w_ada), b_ada=(b_ada, m_b_ada, v_b_ada), b_in=(b_in, m_b_in, v_b_in),
                 conv_a_w=(conv_a_w, m_conv_a_w, v_conv_a_w), conv_a_b=(conv_a_b, m_conv_a_b, v_conv_a_b),
                 ln_conv_g=(ln_conv_g, m_ln_conv_g, v_ln_conv_g), ln_conv_b=(ln_conv_b, m_ln_conv_b, v_ln_conv_b),
                 b_mix_out=(b_mix_out, m_b_mix_out, v_b_mix_out), ln1_g=(ln1_g, m_ln1_g, v_ln1_g), ln1_b=(ln1_b, m_ln1_b, v_ln1_b),
                 ffn_conv_w=(ffn_conv_w, m_ffn_conv_w, v_ffn_conv_w), ffn_conv_b=(ffn_conv_b, m_ffn_conv_b, v_ffn_conv_b),
                 ln2_g=(ln2_g, m_ln2_g, v_ln2_g), ln2_b=(ln2_b, m_ln2_b, v_ln2_b))
    res, kinds = {}, ("grad", "delta", "new_m", "new_v")
    loc_names = ["w_ada", "b_ada"] + small_names
    packs = [_pack([g_small[n] for n in loc_names], D, F32)] + [_pack([given[n][i] for n in loc_names], D, F32) for i in range(3)]
    outs = _adamw(packs[0][0][None], packs[1][0], packs[2][0], packs[3][0], name="adamw_small")
    for kind, packed in zip(kinds, outs):
        for n, a in zip(loc_names, _unpack(packed, packs[0][1])):
            res[kind, n] = a

    big_parts = {}
    after = outs[0]
    for gi, (state, send, segs, grp) in enumerate(g_state):
        landed = _exchange_wait(state, after, name=f"exchange_grads_wait_{gi}")
        landed = lax.dynamic_update_index_in_dim(landed, lax.dynamic_index_in_dim(send, me, 0, keepdims=False), me, 0)
        wmv = [_pack(shard_items([given[n][j] for n in big_names], grp), D, F32)[0] for j in range(3)]
        outs = _adamw(landed, *wmv, name=f"adamw_big_{gi}")
        for kind, packed in zip(kinds, outs):
            for (l, i), a in zip(grp, _unpack(packed, segs)):
                big_parts[kind, l, i] = a.T if transposed[i] else a
        after = outs[0]
    for kind in kinds:
        for i, n in enumerate(big_names):
            res[kind, n] = jnp.stack([big_parts[kind, l, i] for l in range(L)])

    order = ["w_ada", "b_ada", "w_in", "b_in", "conv_a_w", "conv_a_b", "ln_conv_g", "ln_conv_b", "w_conv_proj", "w_attn_proj",
             "w_mix_out", "b_mix_out", "ln1_g", "ln1_b", "w_ffn_up", "ffn_conv_w", "ffn_conv_b", "w_ffn_down", "ln2_g", "ln2_b"]
    return (loss, grad_x, *[res[k, n] for k in ("grad", "delta", "new_m", "new_v") for n in order])
```

```python
import functools
import math
from typing import NamedTuple

import jax
import jax.numpy as jnp
from jax import lax
from jax.experimental import pallas as pl
from jax.experimental.pallas import tpu as pltpu

F32, BF16 = jnp.float32, jnp.bfloat16
LN_EPS = 1e-5
ADAM_LR, ADAM_B1, ADAM_B2, ADAM_EPS, ADAM_WD, ADAM_STEP = 0.001, 0.9, 0.999, 1e-08, 0.01, 10
N_DEV = 8
LANES = 128
VMEM_LIMIT = 56 * 1024 * 1024
NEG = -1e30
NT = (((1,), (1,)), ((), ()))
TN = (((0,), (0,)), ((), ()))


class Cfg(NamedTuple):
    L: int
    Bl: int
    S: int
    D: int
    C: int
    KW: int
    H: int
    Dh: int
    F: int
    KF: int

    @property
    def T(self): return self.Bl * self.S
    @property
    def AW(self): return self.H * self.Dh
    @property
    def NM(self): return 2 * self.C + 3 * self.AW + 2 * self.D
    @property
    def q_off(self): return 2 * self.C
    @property
    def g_off(self): return 2 * self.C + 3 * self.AW
    @property
    def alpha(self): return (2.0 * self.L) ** 0.25


def _pcall(body, **kw):
    return pl.pallas_call(body, **kw)


def _params(*sem):
    return pltpu.CompilerParams(dimension_semantics=sem, vmem_limit_bytes=VMEM_LIMIT)


def _pick(n, prefs):
    for p in prefs:
        if n % p == 0:
            return p
    return n


def _sigmoid(x):
    return 1.0 / (1.0 + jnp.exp(-x))


def _ln_stats(x):
    mu = jnp.mean(x, axis=-1, keepdims=True)
    xc = x - mu
    var = jnp.mean(xc * xc, axis=-1, keepdims=True)
    rstd = lax.rsqrt(var + LN_EPS)
    return xc * rstd, rstd


def _ln_bwd(dxh, xh, rstd):
    return rstd * (dxh - jnp.mean(dxh, axis=-1, keepdims=True) - xh * jnp.mean(dxh * xh, axis=-1, keepdims=True))


def _matmul(a, b, *, mode, name, bias=None, add=None, out_dtype=F32, tm=None, tn=None, tk=None, after=None):
    if mode == "tn":
        K, M = a.shape
    else:
        M, K = a.shape
    N = b.shape[0] if mode == "nt" else b.shape[1]
    lane_tiles = (1536, 1408, 1024, 768, 512, 256, 128)
    tm = tm or _pick(M, lane_tiles if mode == "tn" else (1024, 512, 256, 128, 64, 32, 16, 8))
    tn = tn or _pick(N, lane_tiles)
    tk = tk or _pick(K, (512, 256, 128) if mode == "tn" else lane_tiles)
    nk = K // tk
    dn = {"nn": (((1,), (0,)), ((), ())), "nt": NT, "tn": TN}[mode]
    has_bias, has_add, has_after = bias is not None, add is not None, after is not None

    def body(*refs):
        a_ref, b_ref = refs[0], refs[1]
        pos = 2
        bias_ref = refs[pos] if has_bias else None
        pos += has_bias
        add_ref = refs[pos] if has_add else None
        pos += has_add + has_after
        o_ref = refs[pos]
        part = lax.dot_general(a_ref[...], b_ref[...], dn, preferred_element_type=F32)

        def finish(acc):
            if has_bias:
                acc = acc + bias_ref[...]
            if has_add:
                acc = acc + add_ref[...]
            o_ref[...] = acc.astype(out_dtype)

        if nk == 1:
            finish(part)
        else:
            acc_ref = refs[pos + 1]
            k = pl.program_id(2)

            @pl.when(k == 0)
            def _():
                acc_ref[...] = part

            @pl.when(k > 0)
            def _():
                acc_ref[...] += part

            @pl.when(k == nk - 1)
            def _():
                finish(acc_ref[...])

    a_spec = pl.BlockSpec((tk, tm), lambda i, j, k: (k, i)) if mode == "tn" else pl.BlockSpec((tm, tk), lambda i, j, k: (i, k))
    b_spec = pl.BlockSpec((tn, tk), lambda i, j, k: (j, k)) if mode == "nt" else pl.BlockSpec((tk, tn), lambda i, j, k: (k, j))
    in_specs, args = [a_spec, b_spec], [a, b]
    if has_bias:
        in_specs.append(pl.BlockSpec((1, tn), lambda i, j, k: (0, j)))
        args.append(bias)
    if has_add:
        in_specs.append(pl.BlockSpec((tm, tn), lambda i, j, k: (i, j)))
        args.append(add)
    if has_after:
        in_specs.append(pl.BlockSpec(memory_space=pl.ANY))
        args.append(after)
    return _pcall(
        body, name=name, grid=(M // tm, N // tn, nk), in_specs=in_specs,
        out_specs=pl.BlockSpec((tm, tn), lambda i, j, k: (i, j)),
        out_shape=jax.ShapeDtypeStruct((M, N), out_dtype),
        scratch_shapes=[pltpu.VMEM((tm, tn), F32)] if nk > 1 else [],
        compiler_params=_params("parallel", "parallel", "arbitrary"),
    )(*args)


def _colsum(x, *, name):
    T, N = x.shape
    tr = _pick(T, (512, 256, 128, 64, 32, 16))
    tc = _pick(N, (1536, 1024, 512, 256, 128))

    def body(x_ref, o_ref):
        @pl.when(pl.program_id(1) == 0)
        def _():
            o_ref[...] = jnp.zeros_like(o_ref)

        o_ref[...] += jnp.sum(x_ref[...].astype(F32), axis=0, keepdims=True)

    return _pcall(body, name=name, grid=(N // tc, T // tr), in_specs=[pl.BlockSpec((tr, tc), lambda j, i: (i, j))],
                  out_specs=pl.BlockSpec((1, tc), lambda j, i: (0, j)), out_shape=jax.ShapeDtypeStruct((1, N), F32),
                  compiler_params=_params("parallel", "arbitrary"))(x)


def _row_tile(cfg):
    return _pick(cfg.S, (256, 128, 64, 32, 16, 8))


def _ln_mod_fwd(x, shift, scale, cfg, *, name):
    tr = _row_tile(cfg)
    tpb = cfg.S // tr

    def body(x_ref, sh_ref, sc_ref, u_ref):
        xh, _ = _ln_stats(x_ref[...])
        u_ref[...] = (xh * (1.0 + sc_ref[0]) + sh_ref[0]).astype(BF16)

    row = pl.BlockSpec((tr, cfg.D), lambda i: (i, 0))
    per_b = pl.BlockSpec((1, 1, cfg.D), lambda i: (i // tpb, 0, 0))
    return _pcall(body, name=name, grid=(cfg.T // tr,), in_specs=[row, per_b, per_b], out_specs=row,
                  out_shape=jax.ShapeDtypeStruct((cfg.T, cfg.D), BF16), compiler_params=_params("parallel"))(x, shift, scale)


def _res_ln_fwd(xin, br, gate, g, b, cfg, *, name, nxt=None):
    tr = _row_tile(cfg)
    tpb = cfg.S // tr
    alpha = cfg.alpha

    def body(*refs):
        x_ref, br_ref, gt_ref, g_ref, b_ref = refs[:5]
        r = alpha * x_ref[...] + (1.0 + gt_ref[0]) * br_ref[...]
        xh, _ = _ln_stats(r)
        xo = xh * g_ref[...] + b_ref[...]
        if nxt is None:
            refs[5][...] = xo
        else:
            sh_ref, sc_ref, xo_ref, u_ref = refs[5:]
            xo_ref[...] = xo
            uh, _ = _ln_stats(xo)
            u_ref[...] = (uh * (1.0 + sc_ref[0]) + sh_ref[0]).astype(BF16)

    row = pl.BlockSpec((tr, cfg.D), lambda i: (i, 0))
    per_b = pl.BlockSpec((1, 1, cfg.D), lambda i: (i // tpb, 0, 0))
    vec = pl.BlockSpec((1, cfg.D), lambda i: (0, 0))
    in_specs, args = [row, row, per_b, vec, vec], [xin, br, gate, g, b]
    out_specs, out_shape = row, jax.ShapeDtypeStruct((cfg.T, cfg.D), F32)
    if nxt is not None:
        in_specs += [per_b, per_b]
        args += list(nxt)
        out_specs = [row, row]
        out_shape = [out_shape, jax.ShapeDtypeStruct((cfg.T, cfg.D), BF16)]
    return _pcall(body, name=name, grid=(cfg.T // tr,), in_specs=in_specs, out_specs=out_specs, out_shape=out_shape,
                  compiler_params=_params("parallel"))(*args)


def _loss_grad(y, tgt, cfg, *, name):
    tr = _row_tile(cfg)
    nt = cfg.T // tr
    inv_d = 1.0 / cfg.D

    def body(y_ref, t_ref, dy_ref, ls_ref):
        e = y_ref[...] - t_ref[...]
        dy_ref[...] = e * inv_d
        ls_ref[...] = jnp.full((1, 1, LANES), jnp.sum(e * e), F32)

    row = pl.BlockSpec((tr, cfg.D), lambda i: (i, 0))
    return _pcall(body, name=name, grid=(nt,), in_specs=[row, row],
                  out_specs=[row, pl.BlockSpec((1, 1, LANES), lambda i: (i, 0, 0))],
                  out_shape=[jax.ShapeDtypeStruct((cfg.T, cfg.D), F32), jax.ShapeDtypeStruct((nt, 1, LANES), F32)],
                  compiler_params=_params("parallel"))(y, tgt)


def _res_ln_bwd(dy, xin, br, gate, g, cfg, *, name):
    tr = _row_tile(cfg)
    tpb = cfg.S // tr
    alpha = cfg.alpha

    def body(dy_ref, x_ref, br_ref, gt_ref, g_ref, dx_ref, dbr_ref, dg_ref, db_ref, dgt_ref, dbs_ref):
        i = pl.program_id(0)

        @pl.when(i == 0)
        def _():
            dg_ref[...] = jnp.zeros_like(dg_ref)
            db_ref[...] = jnp.zeros_like(db_ref)
            dbs_ref[...] = jnp.zeros_like(dbs_ref)

        @pl.when(i % tpb == 0)
        def _():
            dgt_ref[...] = jnp.zeros_like(dgt_ref)

        dy, brv, one_gate = dy_ref[...], br_ref[...], 1.0 + gt_ref[0]
        xh, rstd = _ln_stats(alpha * x_ref[...] + one_gate * brv)
        dg_ref[...] += jnp.sum(dy * xh, axis=0, keepdims=True)
        db_ref[...] += jnp.sum(dy, axis=0, keepdims=True)
        dr = _ln_bwd(dy * g_ref[...], xh, rstd)
        dx_ref[...] = alpha * dr
        dbr = one_gate * dr
        dbr_ref[...] = dbr.astype(BF16)
        dbs_ref[...] += jnp.sum(dbr, axis=0, keepdims=True)
        dgt_ref[0] += jnp.sum(dr * brv, axis=0, keepdims=True)

    row = pl.BlockSpec((tr, cfg.D), lambda i: (i, 0))
    per_b = pl.BlockSpec((1, 1, cfg.D), lambda i: (i // tpb, 0, 0))
    vec = pl.BlockSpec((1, cfg.D), lambda i: (0, 0))
    vs = jax.ShapeDtypeStruct((1, cfg.D), F32)
    return _pcall(body, name=name, grid=(cfg.T // tr,), in_specs=[row, row, row, per_b, vec],
                  out_specs=[row, row, vec, vec, per_b, vec],
                  out_shape=[jax.ShapeDtypeStruct((cfg.T, cfg.D), F32), jax.ShapeDtypeStruct((cfg.T, cfg.D), BF16), vs, vs,
                             jax.ShapeDtypeStruct((cfg.Bl, 1, cfg.D), F32), vs],
                  compiler_params=_params("arbitrary"))(dy, xin, br, gate, g)


def _ln_mod_bwd(du, xin, scale, dres, cfg, *, name):
    tr = _row_tile(cfg)
    tpb = cfg.S // tr

    def body(du_ref, x_ref, sc_ref, dres_ref, dx_ref, dsc_ref, dsh_ref):
        @pl.when(pl.program_id(0) % tpb == 0)
        def _():
            dsc_ref[...] = jnp.zeros_like(dsc_ref)
            dsh_ref[...] = jnp.zeros_like(dsh_ref)

        du = du_ref[...]
        xh, rstd = _ln_stats(x_ref[...])
        dsc_ref[0] += jnp.sum(du * xh, axis=0, keepdims=True)
        dsh_ref[0] += jnp.sum(du, axis=0, keepdims=True)
        dx_ref[...] = _ln_bwd(du * (1.0 + sc_ref[0]), xh, rstd) + dres_ref[...]

    row = pl.BlockSpec((tr, cfg.D), lambda i: (i, 0))
    per_b = pl.BlockSpec((1, 1, cfg.D), lambda i: (i // tpb, 0, 0))
    bs = jax.ShapeDtypeStruct((cfg.Bl, 1, cfg.D), F32)
    return _pcall(body, name=name, grid=(cfg.T // tr,), in_specs=[row, row, per_b, row], out_specs=[row, per_b, per_b],
                  out_shape=[jax.ShapeDtypeStruct((cfg.T, cfg.D), F32), bs, bs],
                  compiler_params=_params("arbitrary"))(du, xin, scale, dres)


def _merge_tiles(cfg):
    tr = _pick(cfg.T, (512, 256, 128, 64, 32, 16))
    tc = _pick(math.gcd(cfg.g_off, cfg.D), (512, 256, 128))
    return tr, tc


def _merge_fwd(zm, ya, yb, cfg, *, name):
    tr, tc = _merge_tiles(cfg)
    ga0, gb0 = cfg.g_off // tc, (cfg.g_off + cfg.D) // tc

    def body(ga_ref, gb_ref, ya_ref, yb_ref, m_ref):
        m_ref[...] = (_sigmoid(ga_ref[...]) * ya_ref[...] + _sigmoid(gb_ref[...]) * yb_ref[...]).astype(BF16)

    blk = pl.BlockSpec((tr, tc), lambda i, j: (i, j))
    return _pcall(body, name=name, grid=(cfg.T // tr, cfg.D // tc),
                  in_specs=[pl.BlockSpec((tr, tc), lambda i, j: (i, ga0 + j)), pl.BlockSpec((tr, tc), lambda i, j: (i, gb0 + j)), blk, blk],
                  out_specs=blk, out_shape=jax.ShapeDtypeStruct((cfg.T, cfg.D), BF16),
                  compiler_params=_params("parallel", "parallel"))(zm, zm, ya, yb)


def _merge_bwd(dm, zm, ya, yb, cfg, *, name):
    tr, tc = _merge_tiles(cfg)
    ga0, gb0 = cfg.g_off // tc, (cfg.g_off + cfg.D) // tc

    def body(dm_ref, ga_ref, gb_ref, ya_ref, yb_ref, dya_ref, dyb_ref, dga_ref, dgb_ref):
        dm = dm_ref[...]
        ga, gb = _sigmoid(ga_ref[...]), _sigmoid(gb_ref[...])
        dya_ref[...] = (dm * ga).astype(BF16)
        dyb_ref[...] = (dm * gb).astype(BF16)
        dga_ref[...] = (dm * ya_ref[...] * ga * (1.0 - ga)).astype(BF16)
        dgb_ref[...] = (dm * yb_ref[...] * gb * (1.0 - gb)).astype(BF16)

    blk = pl.BlockSpec((tr, tc), lambda i, j: (i, j))
    o = jax.ShapeDtypeStruct((cfg.T, cfg.D), BF16)
    return _pcall(body, name=name, grid=(cfg.T // tr, cfg.D // tc),
                  in_specs=[blk, pl.BlockSpec((tr, tc), lambda i, j: (i, ga0 + j)), pl.BlockSpec((tr, tc), lambda i, j: (i, gb0 + j)), blk, blk],
                  out_specs=[blk] * 4, out_shape=[o] * 4, compiler_params=_params("parallel", "parallel"))(dm, zm, zm, ya, yb)


CONV_A_HALO = 32
CONV_A_CHUNK = 32


def _conv_a_tile(cfg):
    assert cfg.KW - 1 <= CONV_A_HALO
    return _pick(cfg.S, (256, 128, 64, 32))


def _conv_a_fwd(zm, w, cb, g, b, cfg, *, name):
    C, KW, HALO, CH = cfg.C, cfg.KW, CONV_A_HALO, CONV_A_CHUNK
    ts = _conv_a_tile(cfg)
    tpb = cfg.S // ts
    lead = HALO - (KW - 1)

    def body(z_ref, zp_ref, w_ref, cb_ref, g_ref, b_ref, o_ref, a0_s):
        first = pl.program_id(0) % tpb == 0
        prev = zp_ref[:, :C] * _sigmoid(zp_ref[:, C:])
        a0_s[0:HALO, :] = jnp.where(first, 0.0, prev)
        a0_s[HALO:HALO + ts, :] = z_ref[:, :C] * _sigmoid(z_ref[:, C:])
        for r0 in range(0, ts, CH):
            acc = jnp.zeros((CH, C), F32)
            for k in range(KW):
                acc = acc + w_ref[k:k + 1, :] * a0_s[r0 + lead + k:r0 + lead + k + CH, :]
            xh, _ = _ln_stats(acc + cb_ref[...])
            a2 = xh * g_ref[...] + b_ref[...]
            o_ref[r0:r0 + CH, :] = (a2 * _sigmoid(a2)).astype(BF16)

    hb = ts // HALO
    vec = pl.BlockSpec((1, C), lambda i: (0, 0))
    return _pcall(body, name=name, grid=(cfg.T // ts,),
                  in_specs=[pl.BlockSpec((ts, 2 * C), lambda i: (i, 0)),
                            pl.BlockSpec((HALO, 2 * C), lambda i: (jnp.maximum(i * hb - 1, 0), 0)),
                            pl.BlockSpec((32, C), lambda i: (0, 0)), vec, vec, vec],
                  out_specs=pl.BlockSpec((ts, C), lambda i: (i, 0)), out_shape=jax.ShapeDtypeStruct((cfg.T, C), BF16),
                  scratch_shapes=[pltpu.VMEM((HALO + ts, C), F32)], compiler_params=_params("parallel"))(zm, zm, w, cb, g, b)


def _conv_a_bwd(da3, zm, w, cb, g, b, cfg, *, name):
    C, KW, HALO, CH = cfg.C, cfg.KW, CONV_A_HALO, CONV_A_CHUNK
    ts = _conv_a_tile(cfg)
    tpb = cfg.S // ts
    nt = cfg.T // ts
    lead = HALO - (KW - 1)
    ext = ts + HALO

    def body(z_ref, zp_ref, zn_ref, d_ref, dn_ref, w_ref, cb_ref, g_ref, b_ref,
             dz_ref, dw_ref, dcb_ref, dg_ref, db_ref, a0_s, d3_s, da1_s):
        i = pl.program_id(0)
        first, last = i % tpb == 0, i % tpb == tpb - 1

        @pl.when(i == 0)
        def _():
            dw_ref[...] = jnp.zeros_like(dw_ref)
            dcb_ref[...] = jnp.zeros_like(dcb_ref)
            dg_ref[...] = jnp.zeros_like(dg_ref)
            db_ref[...] = jnp.zeros_like(db_ref)

        a0_s[0:HALO, :] = jnp.where(first, 0.0, zp_ref[:, :C] * _sigmoid(zp_ref[:, C:]))
        a0_s[HALO:HALO + ts, :] = z_ref[:, :C] * _sigmoid(z_ref[:, C:])
        a0_s[HALO + ts:HALO + ext, :] = zn_ref[:, :C] * _sigmoid(zn_ref[:, C:])
        d3_s[0:ts, :] = d_ref[...]
        d3_s[ts:ext, :] = jnp.where(last, 0.0, dn_ref[...])
        dcb, dg, db = jnp.zeros((1, C), F32), jnp.zeros((1, C), F32), jnp.zeros((1, C), F32)
        for r0 in range(0, ext, CH):
            acc = jnp.zeros((CH, C), F32)
            for k in range(KW):
                acc = acc + w_ref[k:k + 1, :] * a0_s[r0 + lead + k:r0 + lead + k + CH, :]
            xh, rstd = _ln_stats(acc + cb_ref[...])
            a2 = xh * g_ref[...] + b_ref[...]
            sg = _sigmoid(a2)
            da2 = d3_s[r0:r0 + CH, :] * (sg * (1.0 + a2 * (1.0 - sg)))
            da1 = _ln_bwd(da2 * g_ref[...], xh, rstd)
            da1_s[r0:r0 + CH, :] = da1
            if r0 < ts:
                dg = dg + jnp.sum(da2 * xh, axis=0, keepdims=True)
                db = db + jnp.sum(da2, axis=0, keepdims=True)
                dcb = dcb + jnp.sum(da1, axis=0, keepdims=True)
        dg_ref[...] += dg
        db_ref[...] += db
        dcb_ref[...] += dcb
        for k in range(KW):
            dwk = jnp.zeros((1, C), F32)
            for r0 in range(0, ts, CH):
                dwk = dwk + jnp.sum(da1_s[r0:r0 + CH, :] * a0_s[r0 + lead + k:r0 + lead + k + CH, :], axis=0, keepdims=True)
            dw_ref[k:k + 1, :] += dwk
        for r0 in range(0, ts, CH):
            da0 = jnp.zeros((CH, C), F32)
            for k in range(KW):
                da0 = da0 + w_ref[k:k + 1, :] * da1_s[r0 + KW - 1 - k:r0 + KW - 1 - k + CH, :]
            val, sg = z_ref[r0:r0 + CH, :C], _sigmoid(z_ref[r0:r0 + CH, C:])
            dz_ref[r0:r0 + CH, :C] = (da0 * sg).astype(BF16)
            dz_ref[r0:r0 + CH, C:] = (da0 * val * sg * (1.0 - sg)).astype(BF16)

    hb = ts // HALO
    nhb = cfg.T // HALO
    vec = pl.BlockSpec((1, C), lambda i: (0, 0))
    vs = jax.ShapeDtypeStruct((1, C), F32)
    return _pcall(body, name=name, grid=(nt,),
                  in_specs=[pl.BlockSpec((ts, 2 * C), lambda i: (i, 0)),
                            pl.BlockSpec((HALO, 2 * C), lambda i: (jnp.maximum(i * hb - 1, 0), 0)),
                            pl.BlockSpec((HALO, 2 * C), lambda i: (jnp.minimum((i + 1) * hb, nhb - 1), 0)),
                            pl.BlockSpec((ts, C), lambda i: (i, 0)),
                            pl.BlockSpec((HALO, C), lambda i: (jnp.minimum((i + 1) * hb, nhb - 1), 0)),
                            pl.BlockSpec((32, C), lambda i: (0, 0)), vec, vec, vec],
                  out_specs=[pl.BlockSpec((ts, 2 * C), lambda i: (i, 0)), pl.BlockSpec((32, C), lambda i: (0, 0)), vec, vec, vec],
                  out_shape=[jax.ShapeDtypeStruct((cfg.T, 2 * C), BF16), jax.ShapeDtypeStruct((32, C), F32), vs, vs, vs],
                  scratch_shapes=[pltpu.VMEM((HALO + ext, C), F32), pltpu.VMEM((ext, C), F32), pltpu.VMEM((ext, C), F32)],
                  compiler_params=_params("arbitrary"))(zm, zm, zm, da3, da3, w, cb, g, b)


def _cum_tile(cfg):
    return _pick(cfg.S, (256, 128, 64, 32, 16, 8))


def _fgate_fwd(zf, cfg, *, name):
    tc = _cum_tile(cfg)
    tpb = cfg.S // tc
    _, hp = _attn_tiles(cfg)
    nb = cfg.H // hp

    def body(z_ref, o_ref, carry):
        @pl.when(pl.program_id(0) % tpb == 0)
        def _():
            carry[...] = jnp.zeros_like(carry)

        z = z_ref[...]
        logf = jnp.minimum(z, 0.0) - jnp.log(1.0 + jnp.exp(-jnp.abs(z)))
        tri = (lax.broadcasted_iota(jnp.int32, (tc, tc), 0) >= lax.broadcasted_iota(jnp.int32, (tc, tc), 1)).astype(F32)
        cum = jnp.dot(tri, logf, precision=lax.Precision.HIGHEST, preferred_element_type=F32) + carry[...]
        carry[...] = cum[tc - 1:tc, :]
        o_ref[0] = cum
        for b in range(1, nb):
            o_ref[b] = pltpu.roll(cum, LANES - hp * b, axis=1)

    return _pcall(body, name=name, grid=(cfg.T // tc,), in_specs=[pl.BlockSpec((tc, LANES), lambda i: (i, 0))],
                  out_specs=pl.BlockSpec((nb, tc, LANES), lambda i: (0, i, 0)),
                  out_shape=jax.ShapeDtypeStruct((nb, cfg.T, LANES), F32), scratch_shapes=[pltpu.VMEM((1, LANES), F32)],
                  compiler_params=_params("arbitrary"))(zf)


def _fgate_bwd(dcum_c, zf, cfg, *, name):
    tc = _cum_tile(cfg)
    tpb = cfg.S // tc
    nt = cfg.T // tc
    _, hp = _attn_tiles(cfg)
    nb = cfg.H // hp

    def body(d_ref, z_ref, o_ref, carry):
        @pl.when(pl.program_id(0) % tpb == 0)
        def _():
            carry[...] = jnp.zeros_like(carry)

        d = d_ref[0]
        for b in range(1, nb):
            d = d + pltpu.roll(d_ref[b], hp * b, axis=1)
        tri = (lax.broadcasted_iota(jnp.int32, (tc, tc), 0) <= lax.broadcasted_iota(jnp.int32, (tc, tc), 1)).astype(F32)
        suf = jnp.dot(tri, d, precision=lax.Precision.HIGHEST, preferred_element_type=F32) + carry[...]
        o_ref[...] = (suf * _sigmoid(-z_ref[...])).astype(BF16)
        carry[...] = suf[0:1, :]

    blk = pl.BlockSpec((tc, LANES), lambda i: (nt - 1 - i, 0))
    return _pcall(body, name=name, grid=(nt,), in_specs=[pl.BlockSpec((nb, tc, LANES), lambda i: (0, nt - 1 - i, 0)), blk],
                  out_specs=blk, out_shape=jax.ShapeDtypeStruct((cfg.T, LANES), BF16),
                  scratch_shapes=[pltpu.VMEM((1, LANES), F32)], compiler_params=_params("arbitrary"))(dcum_c, zf)


def _attn_tiles(cfg):
    assert LANES % cfg.Dh == 0 and cfg.H % (LANES // cfg.Dh) == 0
    t = _pick(cfg.S, (256, 128))
    return t, LANES // cfg.Dh


BIAS_LANES = 3


def _head_lanes(hd, cfg, hp):
    li = lax.broadcasted_iota(jnp.int32, (1, LANES), 1)
    own = (li >= hd * cfg.Dh) & (li < (hd + 1) * cfg.Dh)
    return own, li, ((hd + 1) % hp) * cfg.Dh


def _q_aug(q, hd, cfg, hp):
    own, li, b0 = _head_lanes(hd, cfg, hp)
    ones = ((li >= b0) & (li < b0 + BIAS_LANES)).astype(F32)
    return jnp.where(own, q * cfg.Dh ** -0.5, ones).astype(BF16)


def _k_aug(k, ck, hd, cfg, hp):
    own, li, b0 = _head_lanes(hd, cfg, hp)
    hi = ck.astype(BF16).astype(F32)
    mid = (ck - hi).astype(BF16).astype(F32)
    lo = ck - hi - mid
    bias = jnp.where(li == b0, -hi, jnp.where(li == b0 + 1, -mid, jnp.where(li == b0 + 2, -lo, 0.0)))
    return jnp.where(own, k, bias).astype(BF16)


def _attn_fwd(zm, cum_c, cfg, *, name):
    S, Dh = cfg.S, cfg.Dh
    t, hp = _attn_tiles(cfg)
    assert hp >= 2
    nq, nb = S // t, cfg.H // hp
    qb, kb, vb = cfg.q_off // LANES, (cfg.q_off + cfg.AW) // LANES, (cfg.q_off + 2 * cfg.AW) // LANES

    def body(q_ref, k_ref, v_ref, cc_ref, o_ref, o32_ref, lse_ref, ka_s, vt_s):
        qi = pl.program_id(2)

        @pl.when(qi == 0)
        def _():
            def prep(c, _):
                r = pl.multiple_of(c * t, t)
                kc = k_ref[pl.ds(r, t), :]
                for hd in range(hp):
                    ka_s[hd, pl.ds(r, t), :] = _k_aug(kc, cc_ref[0, pl.ds(r, t), hd:hd + 1], hd, cfg, hp)
                vt_s[:, pl.ds(r, t)] = v_ref[pl.ds(r, t), :].T.astype(BF16)
                return 0

            lax.fori_loop(0, nq, prep, 0)

        causal = lax.broadcasted_iota(jnp.int32, (t, t), 0) <= lax.broadcasted_iota(jnp.int32, (t, t), 1)
        qf = q_ref[...]
        qa = [_q_aug(qf, hd, cfg, hp) for hd in range(hp)]

        def chunk(j, carry, masked):
            r = pl.multiple_of(j * t, t)
            new = []
            for hd in range(hp):
                m, l, acc = carry[hd]
                s = lax.dot_general(ka_s[hd, pl.ds(r, t), :], qa[hd], NT, preferred_element_type=F32)
                if masked:
                    s = jnp.where(causal, s, NEG)
                m_new = jnp.maximum(m, jnp.max(s, axis=0, keepdims=True))
                a = jnp.exp(m - m_new)
                p = jnp.exp(s - m_new)
                l = a * l + jnp.sum(p, axis=0, keepdims=True)
                p_hi = p.astype(BF16)
                p_lo = (p - p_hi.astype(F32)).astype(BF16)
                vt = vt_s[hd * Dh:(hd + 1) * Dh, pl.ds(r, t)]
                acc = a * acc + (jnp.dot(vt, p_hi, preferred_element_type=F32) + jnp.dot(vt, p_lo, preferred_element_type=F32))
                new.append((m_new, l, acc))
            return tuple(new)

        init = tuple((jnp.full((1, t), NEG, F32), jnp.zeros((1, t), F32), jnp.zeros((Dh, t), F32)) for _ in range(hp))
        res = chunk(qi, lax.fori_loop(0, qi, functools.partial(chunk, masked=False), init), True)
        o = jnp.concatenate([acc / l for _, l, acc in res], axis=0).T
        o_ref[...] = o.astype(BF16)
        o32_ref[...] = o
        lse_ref[...] = jnp.zeros_like(lse_ref)
        for hd in range(hp):
            lse_ref[0, 0, hd:hd + 1, :] = res[hd][0] + jnp.log(res[hd][1])

    return _pcall(body, name=name, grid=(cfg.Bl, nb, nq),
                  in_specs=[pl.BlockSpec((t, LANES), lambda b, h, i: (b * nq + i, qb + h)),
                            pl.BlockSpec((S, LANES), lambda b, h, i: (b, kb + h)),
                            pl.BlockSpec((S, LANES), lambda b, h, i: (b, vb + h)),
                            pl.BlockSpec((1, S, LANES), lambda b, h, i: (h, b, 0))],
                  out_specs=[pl.BlockSpec((t, LANES), lambda b, h, i: (b * nq + i, h)),
                             pl.BlockSpec((t, LANES), lambda b, h, i: (b * nq + i, h)),
                             pl.BlockSpec((1, 1, 8, t), lambda b, h, i: (b, h, 0, i))],
                  out_shape=[jax.ShapeDtypeStruct((cfg.T, cfg.AW), BF16), jax.ShapeDtypeStruct((cfg.T, cfg.AW), F32),
                             jax.ShapeDtypeStruct((cfg.Bl, nb, 8, S), F32)],
                  scratch_shapes=[pltpu.VMEM((hp, S, LANES), BF16), pltpu.VMEM((LANES, S), BF16)],
                  compiler_params=_params("parallel", "parallel", "arbitrary"))(zm, zm, zm, cum_c)


def _attn_bwd(zm, cum_c, o, do, lse, cfg, *, name):
    S, Dh = cfg.S, cfg.Dh
    t, hp = _attn_tiles(cfg)
    nq, nb = S // t, cfg.H // hp
    qb, kb, vb = cfg.q_off // LANES, (cfg.q_off + cfg.AW) // LANES, (cfg.q_off + 2 * cfg.AW) // LANES
    scale = Dh ** -0.5

    def body(q_ref, k_ref, v_ref, cc_ref, o_ref, do_ref, lse_ref, dq_ref, dk_ref, dv_ref, dcc_ref,
             ka_s, qa_s, vz_s, kt_s, dd_s, dqt_s):
        li = lax.broadcasted_iota(jnp.int32, (1, LANES), 1)
        ri = lax.broadcasted_iota(jnp.int32, (LANES, 1), 0)
        causal = lax.broadcasted_iota(jnp.int32, (t, t), 0) <= lax.broadcasted_iota(jnp.int32, (t, t), 1)

        def prep(c, _):
            r = pl.multiple_of(c * t, t)
            kc, vc, qc = k_ref[pl.ds(r, t), :], v_ref[pl.ds(r, t), :], q_ref[pl.ds(r, t), :]
            prod_t = (do_ref[pl.ds(r, t), :].astype(F32) * o_ref[pl.ds(r, t), :].astype(F32)).T
            for hd in range(hp):
                own = _head_lanes(hd, cfg, hp)[0]
                ka_s[hd, pl.ds(r, t), :] = _k_aug(kc, cc_ref[0, pl.ds(r, t), hd:hd + 1], hd, cfg, hp)
                qa_s[hd, pl.ds(r, t), :] = _q_aug(qc, hd, cfg, hp)
                vz_s[hd, pl.ds(r, t), :] = jnp.where(own, vc, 0.0).astype(BF16)
                dd_s[hd:hd + 1, pl.ds(r, t)] = jnp.sum(prod_t[hd * Dh:(hd + 1) * Dh, :], axis=0, keepdims=True)
            kt_s[:, pl.ds(r, t)] = kc.T.astype(BF16)
            dqt_s[:, pl.ds(r, t)] = jnp.zeros((LANES, t), F32)
            return 0

        lax.fori_loop(0, nq, prep, 0)

        def kv_step(j, _):
            rk = pl.multiple_of(j * t, t)

            def tile(i, carry, masked):
                rq = pl.multiple_of(i * t, t)
                dob = do_ref[pl.ds(rq, t), :]
                new, dq_t = [], None
                for hd in range(hp):
                    dk_h, dv_h, dsum_h = carry[hd]
                    qa = qa_s[hd, pl.ds(rq, t), :]
                    s = lax.dot_general(ka_s[hd, pl.ds(rk, t), :], qa, NT, preferred_element_type=F32)
                    p = jnp.exp(s - lse_ref[0, 0, hd:hd + 1, pl.ds(rq, t)])
                    if masked:
                        p = jnp.where(causal, p, 0.0)
                    dp = lax.dot_general(vz_s[hd, pl.ds(rk, t), :], dob, NT, preferred_element_type=F32)
                    ds = p * (dp - dd_s[hd:hd + 1, pl.ds(rq, t)])
                    dsb = ds.astype(BF16)
                    dv_h = dv_h + jnp.dot(p.astype(BF16), dob, preferred_element_type=F32)
                    dk_h = dk_h + jnp.dot(dsb, qa, preferred_element_type=F32)
                    dq_h = jnp.dot(kt_s[:, pl.ds(rk, t)], dsb, preferred_element_type=F32)
                    dq_t = dq_h if hd == 0 else jnp.where((ri >= hd * Dh) & (ri < (hd + 1) * Dh), dq_h, dq_t)
                    for c0 in range(0, t, LANES):
                        dsum_h = dsum_h + ds[:, c0:c0 + LANES]
                    new.append((dk_h, dv_h, dsum_h))
                dqt_s[:, pl.ds(rq, t)] += dq_t * scale
                return tuple(new)

            zero = tuple((jnp.zeros((t, LANES), F32),) * 3 for _ in range(hp))
            res = lax.fori_loop(j + 1, nq, functools.partial(tile, masked=False), tile(j, zero, True))
            dk, dv, dcc = res[0][0], res[0][1], jnp.zeros((t, LANES), F32)
            for hd in range(hp):
                own = _head_lanes(hd, cfg, hp)[0]
                if hd > 0:
                    dk, dv = jnp.where(own, res[hd][0], dk), jnp.where(own, res[hd][1], dv)
                dcc = dcc + jnp.where(li == hd, -jnp.sum(res[hd][2], axis=1, keepdims=True), 0.0)
            dk_ref[pl.ds(rk, t), :] = dk.astype(BF16)
            dv_ref[pl.ds(rk, t), :] = dv.astype(BF16)
            dcc_ref[0, pl.ds(rk, t), :] = dcc
            return 0

        lax.fori_loop(0, nq, kv_step, 0)

        def finish(c, _):
            r = pl.multiple_of(c * t, t)
            dq_ref[pl.ds(r, t), :] = dqt_s[:, pl.ds(r, t)].T.astype(BF16)
            return 0

        lax.fori_loop(0, nq, finish, 0)

    blk = pl.BlockSpec((S, LANES), lambda b, h: (b, h))
    cc = pl.BlockSpec((1, S, LANES), lambda b, h: (h, b, 0))
    os_ = jax.ShapeDtypeStruct((cfg.T, cfg.AW), BF16)
    return _pcall(body, name=name, grid=(cfg.Bl, nb),
                  in_specs=[pl.BlockSpec((S, LANES), lambda b, h: (b, qb + h)), pl.BlockSpec((S, LANES), lambda b, h: (b, kb + h)),
                            pl.BlockSpec((S, LANES), lambda b, h: (b, vb + h)), cc, blk, blk,
                            pl.BlockSpec((1, 1, 8, S), lambda b, h: (b, h, 0, 0))],
                  out_specs=[blk, blk, blk, cc],
                  out_shape=[os_, os_, os_, jax.ShapeDtypeStruct((nb, cfg.T, LANES), F32)],
                  scratch_shapes=[pltpu.VMEM((hp, S, LANES), BF16)] * 3 + [pltpu.VMEM((LANES, S), BF16),
                                  pltpu.VMEM((8, S), F32), pltpu.VMEM((LANES, S), F32)],
                  compiler_params=_params("parallel", "parallel"))(zm, zm, zm, cum_c, o, do, lse)


FFN_HALO = 8


def _ffn_tiles(cfg):
    assert cfg.KF - 1 <= FFN_HALO
    return _pick(cfg.S, (512, 256, 128, 64, 32, 16, 8)), _pick(cfg.F, (256, 128))


def _gelu(x):
    return 0.5 * x * (1.0 + lax.erf(x * (2.0 ** -0.5)))


def _gelu_grad(x):
    return 0.5 * (1.0 + lax.erf(x * (2.0 ** -0.5))) + x * jnp.exp(-0.5 * x * x) * ((2.0 * math.pi) ** -0.5)


def _ffn_conv_fwd(h0, w, cb, cfg, *, name):
    KF, HALO = cfg.KF, FFN_HALO
    ts, tf = _ffn_tiles(cfg)
    tpb, nf = cfg.S // ts, cfg.F // tf
    lead = HALO - (KF - 1)

    def body(g_ref, gp_ref, l_ref, lp_ref, wg_ref, wl_ref, cg_ref, cl_ref, o_ref, g_s, l_s):
        first = pl.program_id(1) % tpb == 0
        g_s[0:HALO, :] = jnp.where(first, 0.0, gp_ref[...])
        l_s[0:HALO, :] = jnp.where(first, 0.0, lp_ref[...])
        g_s[HALO:HALO + ts, :] = g_ref[...]
        l_s[HALO:HALO + ts, :] = l_ref[...]
        hg, hl = cg_ref[...], cl_ref[...]
        for k in range(KF):
            hg = hg + wg_ref[k:k + 1, :] * g_s[lead + k:lead + k + ts, :]
            hl = hl + wl_ref[k:k + 1, :] * l_s[lead + k:lead + k + ts, :]
        o_ref[...] = (_gelu(hg) * hl).astype(BF16)

    hb = ts // HALO
    prev = lambda off: pl.BlockSpec((HALO, tf), lambda j, i: (jnp.maximum(i * hb - 1, 0), off + j))
    main = lambda off: pl.BlockSpec((ts, tf), lambda j, i: (i, off + j))
    wsp = lambda off: pl.BlockSpec((8, tf), lambda j, i: (0, off + j))
    vsp = lambda off: pl.BlockSpec((1, tf), lambda j, i: (0, off + j))
    return _pcall(body, name=name, grid=(nf, cfg.T // ts),
                  in_specs=[main(0), prev(0), main(nf), prev(nf), wsp(0), wsp(nf), vsp(0), vsp(nf)],
                  out_specs=pl.BlockSpec((ts, tf), lambda j, i: (i, j)), out_shape=jax.ShapeDtypeStruct((cfg.T, cfg.F), BF16),
                  scratch_shapes=[pltpu.VMEM((HALO + ts, tf), F32)] * 2,
                  compiler_params=_params("parallel", "parallel"))(h0, h0, h0, h0, w, w, cb, cb)


def _ffn_conv_bwd(df, h0, w, cb, cfg, *, name):
    KF, HALO = cfg.KF, FFN_HALO
    ts, tf = _ffn_tiles(cfg)
    tpb, nf = cfg.S // ts, cfg.F // tf
    lead = HALO - (KF - 1)
    ext = ts + HALO

    def body(g_ref, gp_ref, gn_ref, l_ref, lp_ref, ln_ref, d_ref, dn_ref, wg_ref, wl_ref, cg_ref, cl_ref,
             dg_ref, dl_ref, dwg_ref, dwl_ref, dcg_ref, dcl_ref, g_s, l_s, d_s, dhg_s, dhl_s):
        i = pl.program_id(1)
        first, last = i % tpb == 0, i % tpb == tpb - 1

        @pl.when(i == 0)
        def _():
            dwg_ref[...] = jnp.zeros_like(dwg_ref)
            dwl_ref[...] = jnp.zeros_like(dwl_ref)
            dcg_ref[...] = jnp.zeros_like(dcg_ref)
            dcl_ref[...] = jnp.zeros_like(dcl_ref)

        for s, main, prev, nxt in ((g_s, g_ref, gp_ref, gn_ref), (l_s, l_ref, lp_ref, ln_ref)):
            s[0:HALO, :] = jnp.where(first, 0.0, prev[...])
            s[HALO:HALO + ts, :] = main[...]
            s[HALO + ts:HALO + ext, :] = nxt[...]
        d_s[0:ts, :] = d_ref[...]
        d_s[ts:ext, :] = jnp.where(last, 0.0, dn_ref[...])
        hg, hl = cg_ref[...], cl_ref[...]
        for k in range(KF):
            hg = hg + wg_ref[k:k + 1, :] * g_s[lead + k:lead + k + ext, :]
            hl = hl + wl_ref[k:k + 1, :] * l_s[lead + k:lead + k + ext, :]
        df_e = d_s[...]
        dhg_s[...] = df_e * hl * _gelu_grad(hg)
        dhl_s[...] = df_e * _gelu(hg)
        for dh_s, h_s, w_ref, dx_ref, dw_ref, dc_ref in ((dhg_s, g_s, wg_ref, dg_ref, dwg_ref, dcg_ref),
                                                         (dhl_s, l_s, wl_ref, dl_ref, dwl_ref, dcl_ref)):
            dh = dh_s[0:ts, :]
            dc_ref[...] += jnp.sum(dh, axis=0, keepdims=True)
            dx = jnp.zeros((ts, tf), F32)
            for k in range(KF):
                dw_ref[k:k + 1, :] += jnp.sum(dh * h_s[lead + k:lead + k + ts, :], axis=0, keepdims=True)
                dx = dx + w_ref[k:k + 1, :] * dh_s[KF - 1 - k:KF - 1 - k + ts, :]
            dx_ref[...] = dx.astype(BF16)

    hb = ts // HALO
    nhb = cfg.T // HALO
    main = lambda off: pl.BlockSpec((ts, tf), lambda j, i: (i, off + j))
    prev = lambda off: pl.BlockSpec((HALO, tf), lambda j, i: (jnp.maximum(i * hb - 1, 0), off + j))
    nxt = lambda off: pl.BlockSpec((HALO, tf), lambda j, i: (jnp.minimum((i + 1) * hb, nhb - 1), off + j))
    wsp = lambda off: pl.BlockSpec((8, tf), lambda j, i: (0, off + j))
    vsp = lambda off: pl.BlockSpec((1, tf), lambda j, i: (0, off + j))
    dxs, dws, dcs = (jax.ShapeDtypeStruct((cfg.T, cfg.F), BF16), jax.ShapeDtypeStruct((8, cfg.F), F32),
                     jax.ShapeDtypeStruct((1, cfg.F), F32))
    return _pcall(body, name=name, grid=(nf, cfg.T // ts),
                  in_specs=[main(0), prev(0), nxt(0), main(nf), prev(nf), nxt(nf), main(0), nxt(0),
                            wsp(0), wsp(nf), vsp(0), vsp(nf)],
                  out_specs=[main(0), main(0), wsp(0), wsp(0), vsp(0), vsp(0)],
                  out_shape=[dxs, dxs, dws, dws, dcs, dcs],
                  scratch_shapes=[pltpu.VMEM((HALO + ext, tf), F32)] * 2 + [pltpu.VMEM((ext, tf), F32)] * 3,
                  compiler_params=_params("parallel", "arbitrary"))(h0, h0, h0, h0, h0, h0, df, df, w, w, cb, cb)


def _ada_fwd(c_all, w, b, *, name):
    L, D, n = w.shape
    B = c_all.shape[0]

    def body(c_ref, w_ref, b_ref, o_ref):
        c = c_ref[...]
        act = (c * _sigmoid(c)).astype(BF16)
        o_ref[0] = jnp.dot(act, w_ref[0].astype(BF16), preferred_element_type=F32) + b_ref[0]

    return _pcall(body, name=name, grid=(L,),
                  in_specs=[pl.BlockSpec((B, D), lambda l: (0, 0)), pl.BlockSpec((1, D, n), lambda l: (l, 0, 0)),
                            pl.BlockSpec((1, 1, n), lambda l: (l, 0, 0))],
                  out_specs=pl.BlockSpec((1, B, n), lambda l: (l, 0, 0)), out_shape=jax.ShapeDtypeStruct((L, B, n), F32),
                  compiler_params=_params("parallel"))(c_all, w, b)


def _ada_bwd(c_all, dmod, *, name):
    L, B, n = dmod.shape
    D = c_all.shape[1]

    def body(c_ref, d_ref, o_ref):
        c = c_ref[...]
        act = (c * _sigmoid(c)).astype(BF16)
        o_ref[0] = lax.dot_general(act, d_ref[0].astype(BF16), TN, preferred_element_type=F32)

    return _pcall(body, name=name, grid=(L,),
                  in_specs=[pl.BlockSpec((B, D), lambda l: (0, 0)), pl.BlockSpec((1, B, n), lambda l: (l, 0, 0))],
                  out_specs=pl.BlockSpec((1, D, n), lambda l: (l, 0, 0)), out_shape=jax.ShapeDtypeStruct((L, D, n), F32),
                  compiler_params=_params("parallel"))(c_all, dmod)


def _slot_sum(x, *, name):
    n, R, W = x.shape
    tr = _pick(R, (256, 128, 64, 32, 16, 8))

    def body(x_ref, o_ref):
        acc = x_ref[0].astype(F32)
        for k in range(1, n):
            acc = acc + x_ref[k].astype(F32)
        o_ref[...] = acc

    return _pcall(body, name=name, grid=(R // tr,), in_specs=[pl.BlockSpec((n, tr, W), lambda i: (0, i, 0))],
                  out_specs=pl.BlockSpec((tr, W), lambda i: (i, 0)), out_shape=jax.ShapeDtypeStruct((R, W), F32),
                  compiler_params=_params("parallel"))(x)


def _adamw(gs, w, m, v, *, name):
    n, R, W = gs.shape
    tr = _pick(R, (256, 128, 64, 32, 16, 8))
    c1, c2 = 1.0 - ADAM_B1 ** ADAM_STEP, 1.0 - ADAM_B2 ** ADAM_STEP

    def body(g_ref, w_ref, m_ref, v_ref, go_ref, d_ref, mo_ref, vo_ref):
        g = g_ref[0].astype(F32)
        for k in range(1, n):
            g = g + g_ref[k].astype(F32)
        m2 = ADAM_B1 * m_ref[...] + (1.0 - ADAM_B1) * g
        v2 = ADAM_B2 * v_ref[...] + (1.0 - ADAM_B2) * (g * g)
        go_ref[...] = g
        mo_ref[...] = m2
        vo_ref[...] = v2
        d_ref[...] = -ADAM_LR * ((m2 / c1) / (jnp.sqrt(v2 / c2) + ADAM_EPS) + ADAM_WD * w_ref[...])

    blk = pl.BlockSpec((tr, W), lambda i: (i, 0))
    o = jax.ShapeDtypeStruct((R, W), F32)
    return _pcall(body, name=name, grid=(R // tr,), in_specs=[pl.BlockSpec((n, tr, W), lambda i: (0, i, 0)), blk, blk, blk],
                  out_specs=[blk] * 4, out_shape=[o] * 4, compiler_params=_params("parallel"))(gs, w, m, v)


def _peer_copies(x_ref, land_ref, send_sems, recv_sems, all_to_all):
    mx, my, mc = lax.axis_index("x"), lax.axis_index("y"), lax.axis_index("c")
    me = 4 * mx + 2 * my + mc
    copies = []
    for k in range(1, N_DEV):
        px, py, pc = mx ^ ((k >> 2) & 1), my ^ ((k >> 1) & 1), mc ^ (k & 1)
        copies.append(pltpu.make_async_remote_copy(
            src_ref=x_ref.at[4 * px + 2 * py + pc] if all_to_all else x_ref, dst_ref=land_ref.at[me],
            send_sem=send_sems.at[k - 1], recv_sem=recv_sems.at[k - 1], device_id=(px, py, pc),
            device_id_type=pl.DeviceIdType.MESH))
    return copies


_HBM = pl.BlockSpec(memory_space=pltpu.HBM)
_SEM = pl.BlockSpec(memory_space=pltpu.SEMAPHORE)
_EFFECT = pltpu.SideEffectType.DATAFLOW_SIDE_EFFECTING


def _exchange_start(x, *, all_to_all, name, after=None):
    blk = x.shape[1:] if all_to_all else x.shape
    land = lax.empty((N_DEV,) + tuple(blk), x.dtype)
    has_after = after is not None

    def body(*refs):
        x_ref, land_ref = refs[0], refs[1]
        send_sems, recv_sems, _, _, token = refs[2 + has_after:]
        for cp in _peer_copies(x_ref, land_ref, send_sems, recv_sems, all_to_all):
            cp.start()
        token[...] = jnp.zeros_like(token)

    n_sem = pltpu.SemaphoreType.DMA((N_DEV - 1,))
    args = [pltpu.with_memory_space_constraint(x, pltpu.HBM), pltpu.with_memory_space_constraint(land, pltpu.HBM)]
    in_specs = [_HBM, _HBM]
    if has_after:
        args.append(after)
        in_specs.append(pl.BlockSpec(memory_space=pl.ANY))
    send_sems, recv_sems, x_thru, land_thru, token = _pcall(
        body, name=name, in_specs=in_specs,
        out_shape=(n_sem, n_sem, pltpu.HBM(x.shape, x.dtype), pltpu.HBM(land.shape, land.dtype),
                   jax.ShapeDtypeStruct((8, LANES), F32)),
        out_specs=(_SEM, _SEM, _HBM, _HBM, pl.BlockSpec(memory_space=pltpu.VMEM)), input_output_aliases={0: 2, 1: 3},
        compiler_params=pltpu.CompilerParams(has_side_effects=_EFFECT))(*args)
    return (send_sems, recv_sems, x_thru, land_thru, all_to_all), token


def _exchange_wait(state, after, *, name):
    send_sems, recv_sems, x_thru, land_thru, all_to_all = state

    def body(x_ref, land_ref, send_sems, recv_sems, after_ref, x_dead, landed):
        for cp in _peer_copies(x_ref, land_ref, send_sems, recv_sems, all_to_all):
            cp.wait_send()
            cp.wait_recv()

    return _pcall(
        body, name=name, in_specs=(_HBM, _HBM, _SEM, _SEM, pl.BlockSpec(memory_space=pl.ANY)),
        out_shape=(pltpu.HBM(x_thru.shape, x_thru.dtype), pltpu.HBM(land_thru.shape, land_thru.dtype)),
        out_specs=(_HBM, _HBM), input_output_aliases={0: 0, 1: 1},
        compiler_params=pltpu.CompilerParams(has_side_effects=_EFFECT))(x_thru, land_thru, send_sems, recv_sems, after)[1]


def _exchange(x, *, all_to_all, name, after=None):
    blk = x.shape[1:] if all_to_all else x.shape

    def body(x_ref, *rest):
        o_ref, send_sems, recv_sems, local_sem = rest[-4:]
        me = 4 * lax.axis_index("x") + 2 * lax.axis_index("y") + lax.axis_index("c")
        mine = pltpu.make_async_copy(x_ref.at[me] if all_to_all else x_ref, o_ref.at[me], local_sem)
        mine.start()
        copies = _peer_copies(x_ref, o_ref, send_sems, recv_sems, all_to_all)
        for cp in copies:
            cp.start()
        for cp in copies:
            cp.wait()
        mine.wait()

    anyspec = pl.BlockSpec(memory_space=pl.ANY)
    args = [x] if after is None else [x, after]
    return _pcall(body, name=name, in_specs=[anyspec] * len(args), out_specs=anyspec,
                  out_shape=jax.ShapeDtypeStruct((N_DEV,) + tuple(blk), x.dtype),
                  scratch_shapes=[pltpu.SemaphoreType.DMA((N_DEV - 1,)), pltpu.SemaphoreType.DMA((N_DEV - 1,)),
                                  pltpu.SemaphoreType.DMA(())])(*args)


PACK_ROWS = 16


def _pack(arrs, width, dtype, lead=0):
    parts, segs, r = [], [], 0
    for a in arrs:
        lshape, shape = a.shape[:lead], a.shape[lead:]
        n = math.prod(shape)
        rows = -(-n // width)
        rows_p = -(-rows // PACK_ROWS) * PACK_ROWS
        flat = a.reshape(lshape + (n,)).astype(dtype)
        flat = jnp.pad(flat, [(0, 0)] * lead + [(0, rows_p * width - n)])
        parts.append(flat.reshape(lshape + (rows_p, width)))
        segs.append((r, n, shape))
        r += rows_p
    return jnp.concatenate(parts, axis=lead), segs


def _unpack(p, segs):
    lshape, width = p.shape[:-2], p.shape[-1]
    outs = []
    for r, n, shape in segs:
        rows = -(-n // width)
        blk = p[..., r:r + rows, :].reshape(lshape + (rows * width,))
        outs.append(blk[..., :n].reshape(lshape + shape))
    return outs


def _split_cols(a, f_off, h):
    return jnp.concatenate([a[..., :f_off], a[..., f_off + h:]], axis=-1), a[..., f_off:f_off + h]


def _merge_cols(main, f, f_off):
    return jnp.concatenate([main[..., :f_off], f, main[..., f_off:]], axis=-1)


def _pad_to(a, n, axis):
    pad = [(0, 0)] * a.ndim
    pad[axis] = (0, n - a.shape[axis])
    return jnp.pad(a, pad)


def kernel(x, c, w_ada, b_ada, w_in, b_in, conv_a_w, conv_a_b, ln_conv_g, ln_conv_b, w_conv_proj, w_attn_proj, w_mix_out, b_mix_out, ln1_g, ln1_b, w_ffn_up, ffn_conv_w, ffn_conv_b, w_ffn_down, ln2_g, ln2_b, loss_target, m_w_ada, m_b_ada, m_w_in, m_b_in, m_conv_a_w, m_conv_a_b, m_ln_conv_g, m_ln_conv_b, m_w_conv_proj, m_w_attn_proj, m_w_mix_out, m_b_mix_out, m_ln1_g, m_ln1_b, m_w_ffn_up, m_ffn_conv_w, m_ffn_conv_b, m_w_ffn_down, m_ln2_g, m_ln2_b, v_w_ada, v_b_ada, v_w_in, v_b_in, v_conv_a_w, v_conv_a_b, v_ln_conv_g, v_ln_conv_b, v_w_conv_proj, v_w_attn_proj, v_w_mix_out, v_b_mix_out, v_ln1_g, v_ln1_b, v_w_ffn_up, v_ffn_conv_w, v_ffn_conv_b, v_w_ffn_down, v_ln2_g, v_ln2_b):
    L, D = w_ada.shape[0], w_ada.shape[1]
    Bl, S, _ = x.shape
    C, KW, AW = conv_a_b.shape[1], conv_a_w.shape[1], w_attn_proj.shape[1]
    F, KF, n_in_all = ffn_conv_b.shape[1] // 2, ffn_conv_w.shape[1], b_in.shape[1]
    H = n_in_all - 2 * C - 3 * AW - 2 * D
    cfg = Cfg(L=L, Bl=Bl, S=S, D=D, C=C, KW=KW, H=H, Dh=AW // H, F=F, KF=KF)
    T, NM = cfg.T, cfg.NM
    f_off = 2 * C + 3 * AW
    n_ada = w_ada.shape[2]
    me = 4 * lax.axis_index("x") + 2 * lax.axis_index("y") + lax.axis_index("c")

    def my_cols(a, n):
        return lax.dynamic_slice_in_dim(a, me * n, n, axis=a.ndim - 1)

    spack, ssegs = _pack([c, conv_a_w, ffn_conv_w], D, F32)
    c_g, caw_g, fcw_g = _unpack(_exchange(spack, all_to_all=False, name="gather_small"), ssegs)
    c_all = c_g.reshape(N_DEV * Bl, D)
    caw = _pad_to(jnp.moveaxis(caw_g, 0, 2).reshape(L, KW, C), 32, 1)
    fcw = _pad_to(jnp.moveaxis(fcw_g, 0, 2).reshape(L, KF, 2 * F), 8, 1)

    mod_part = _ada_fwd(c_all, w_ada, my_cols(b_ada, n_ada)[:, None, :], name="ada_fwd")
    mod_send = jnp.moveaxis(mod_part.reshape(L, N_DEV, Bl, n_ada), 1, 0).reshape(N_DEV, L * Bl, n_ada)
    mod_recv = _exchange(mod_send, all_to_all=True, name="exchange_mod")
    mod = jnp.moveaxis(mod_recv.reshape(N_DEV, L, Bl, n_ada), 0, 2).reshape(L, Bl, 6, 1, D)
    shift1, scale1, gate1, shift2, scale2, gate2 = (mod[:, :, i] for i in range(6))

    big_names = ["w_in", "w_conv_proj", "w_attn_proj", "w_mix_out", "w_ffn_up", "w_ffn_down"]
    transposed = (True, True, True, False, True, False)

    def shard_items(arrs, grp):
        return [arrs[i][l].T if transposed[i] else arrs[i][l] for l, i in grp]

    w_groups = [[(0, 0)], [(0, i) for i in range(1, 6)]] + [[(l, i) for i in range(6)] for l in range(1, L)]
    w_state, token = [], mod_recv
    for gi, grp in enumerate(w_groups):
        pack, segs = _pack(shard_items((w_in, w_conv_proj, w_attn_proj, w_mix_out, w_ffn_up, w_ffn_down), grp), D, BF16)
        state, token = _exchange_start(pack, all_to_all=False, name=f"gather_weights_start_{gi}", after=token)
        w_state.append((state, pack, segs, grp))
    W = [dict() for _ in range(L)]

    def wait_weights(gi, after):
        state, pack, segs, grp = w_state[gi]
        landed = _exchange_wait(state, after, name=f"gather_weights_wait_{gi}")
        landed = lax.dynamic_update_index_in_dim(landed, pack, me, 0)
        for (l, i), a in zip(grp, _unpack(landed, segs)):
            a = a.reshape((-1, a.shape[-1]))
            if i == 0:
                wm_t, wf_t = _split_cols(a.T, f_off, H)
                bm, bf = _split_cols(b_in[l], f_off, H)
                W[l].update(wm_t=wm_t.T, wf_t=_pad_to(wf_t.T, LANES, 0), bm=bm[None], bf=_pad_to(bf, LANES, 0)[None])
            else:
                W[l][("w_cp_t", "w_ap_t", "w_mo", "w_up_t", "w_dn")[i - 1]] = a

    xf = x.reshape(T, D)
    u = _ln_mod_fwd(xf, shift1[0], scale1[0], cfg, name="ln_mod_fwd")
    saved = []
    xin = xf
    for l in range(L):
        w = W[l]
        wait_weights(0 if l == 0 else l + 1, u)
        zm = _matmul(u, w["wm_t"], mode="nt", bias=w["bm"], name=f"in_proj_{l}")
        zf = _matmul(u, w["wf_t"], mode="nt", bias=w["bf"], name=f"in_proj_f_{l}")
        a3 = _conv_a_fwd(zm, caw[l], conv_a_b[l][None], ln_conv_g[l][None], ln_conv_b[l][None], cfg, name=f"conv_a_fwd_{l}")
        cum_c = _fgate_fwd(zf, cfg, name=f"fgate_fwd_{l}")
        o, o32, lse = _attn_fwd(zm, cum_c, cfg, name=f"attn_fwd_{l}")
        if l == 0:
            wait_weights(1, o)
        ya = _matmul(a3, w["w_cp_t"], mode="nt", name=f"conv_proj_{l}")
        yb = _matmul(o, w["w_ap_t"], mode="nt", name=f"attn_proj_{l}")
        mg = _merge_fwd(zm, ya, yb, cfg, name=f"merge_fwd_{l}")
        mix = _matmul(mg, w["w_mo"], mode="nn", bias=b_mix_out[l][None], name=f"mix_out_{l}")
        x1, u2 = _res_ln_fwd(xin, mix, gate1[l], ln1_g[l][None], ln1_b[l][None], cfg, name=f"res_ln1_fwd_{l}",
                             nxt=(shift2[l], scale2[l]))
        h0 = _matmul(u2, w["w_up_t"], mode="nt", name=f"ffn_up_{l}")
        fa = _ffn_conv_fwd(h0, fcw[l], ffn_conv_b[l][None], cfg, name=f"ffn_conv_fwd_{l}")
        ffn = _matmul(fa, w["w_dn"], mode="nn", name=f"ffn_down_{l}")
        saved.append(dict(x=xin, u=u, zm=zm, zf=zf, a3=a3, cum_c=cum_c, o=o, o32=o32, lse=lse, ya=ya, yb=yb, mg=mg, mix=mix,
                          x1=x1, u2=u2, h0=h0, fa=fa, ffn=ffn))
        if l + 1 < L:
            xin, u = _res_ln_fwd(x1, ffn, gate2[l], ln2_g[l][None], ln2_b[l][None], cfg, name=f"res_ln2_fwd_{l}",
                                 nxt=(shift1[l + 1], scale1[l + 1]))
        else:
            xin = _res_ln_fwd(x1, ffn, gate2[l], ln2_g[l][None], ln2_b[l][None], cfg, name=f"res_ln2_fwd_{l}")

    dx, loss_tiles = _loss_grad(xin, loss_target.reshape(T, D), cfg, name="loss_grad")
    loss = lax.psum(0.5 / D * jnp.sum(loss_tiles[:, 0, 0]), ("x", "y", "c"))

    gbig = {}
    g_groups = [[(l, i) for i in range(6)] for l in reversed(range(1, L))] + [[(0, 4), (0, 5)], [(0, 1), (0, 2), (0, 3)], [(0, 0)]]
    g_state = []

    def start_grads(after=None):
        grp = g_groups[len(g_state)]
        send, segs = _pack([gbig[k].reshape((N_DEV, -1, gbig[k].shape[1])) for k in grp], D, BF16, lead=1)
        state, tok = _exchange_start(send, all_to_all=True, name=f"exchange_grads_start_{len(g_state)}", after=after)
        g_state.append((state, send, segs, grp))
        return tok

    gsm = [dict() for _ in range(L)]
    dmods = [None] * L
    token = None
    for l in reversed(range(L)):
        w, s = W[l], saved[l]
        dres2, dffn, dg2, db2, dgate2, _ = _res_ln_bwd(dx, s["x1"], s["ffn"], gate2[l], ln2_g[l][None], cfg, name=f"res_ln2_bwd_{l}")
        dfa = _matmul(dffn, w["w_dn"], mode="nt", name=f"d_ffn_act_{l}", after=token)
        gbig[l, 5] = _matmul(s["fa"], dffn, mode="tn", name=f"dw_ffn_down_{l}")
        dh0g, dh0l, dwg, dwl, dcg, dcl = _ffn_conv_bwd(dfa, s["h0"], fcw[l], ffn_conv_b[l][None], cfg, name=f"ffn_conv_bwd_{l}")
        dh0 = jnp.concatenate([dh0g, dh0l], axis=1)
        du2 = _matmul(dh0, w["w_up_t"], mode="nn", name=f"d_u2_{l}")
        gbig[l, 4] = _matmul(dh0, s["u2"], mode="tn", name=f"dw_ffn_up_{l}")
        token = start_grads() if l == 0 else None
        dx1, dscale2, dshift2 = _ln_mod_bwd(du2, s["x1"], scale2[l], dres2, cfg, name=f"ln_mod2_bwd_{l}")
        dres1, dmix, dg1, db1, dgate1, dbmo = _res_ln_bwd(dx1, s["x"], s["mix"], gate1[l], ln1_g[l][None], cfg, name=f"res_ln1_bwd_{l}")
        dmg = _matmul(dmix, w["w_mo"], mode="nt", name=f"d_merge_{l}", after=token)
        gbig[l, 3] = _matmul(s["mg"], dmix, mode="tn", name=f"dw_mix_out_{l}")
        dya, dyb, dzga, dzgb = _merge_bwd(dmg, s["zm"], s["ya"], s["yb"], cfg, name=f"merge_bwd_{l}")
        gbig[l, 1] = _matmul(dya, s["a3"], mode="tn", name=f"dw_conv_proj_{l}")
        da3 = _matmul(dya, w["w_cp_t"], mode="nn", name=f"d_a3_{l}")
        gbig[l, 2] = _matmul(dyb, s["o"], mode="tn", name=f"dw_attn_proj_{l}")
        token = start_grads() if l == 0 else None
        do = _matmul(dyb, w["w_ap_t"], mode="nn", out_dtype=BF16, name=f"d_o_{l}", after=token)
        dq, dk, dv, dcum_c = _attn_bwd(s["zm"], s["cum_c"], s["o32"], do, s["lse"], cfg, name=f"attn_bwd_{l}")
        dzf = _fgate_bwd(dcum_c, s["zf"], cfg, name=f"fgate_bwd_{l}")
        dzglu, dcaw, dcab, dlcg, dlcb = _conv_a_bwd(da3, s["zm"], caw[l], conv_a_b[l][None], ln_conv_g[l][None],
                                                    ln_conv_b[l][None], cfg, name=f"conv_a_bwd_{l}")
        dzm = jnp.concatenate([dzglu, dq, dk, dv, dzga, dzgb], axis=1)
        du1 = _matmul(dzf, w["wf_t"], mode="nn", name=f"d_u1_f_{l}")
        du1 = _matmul(dzm, w["wm_t"], mode="nn", add=du1, name=f"d_u1_{l}")
        dwm_t = _matmul(dzm, s["u"], mode="tn", name=f"dw_in_{l}")
        dwf_t = _matmul(dzf, s["u"], mode="tn", name=f"dw_in_f_{l}")
        gbig[l, 0] = _merge_cols(dwm_t.T, dwf_t[:H].T, f_off).T
        token = start_grads() if l > 0 else None
        dbm, dbf = _colsum(dzm, name=f"db_in_{l}"), _colsum(dzf, name=f"db_in_f_{l}")
        dx, dscale1, dshift1 = _ln_mod_bwd(du1, s["x"], scale1[l], dres1, cfg, name=f"ln_mod1_bwd_{l}")
        dmods[l] = jnp.concatenate([dshift1, dscale1, dgate1, dshift2, dscale2, dgate2], axis=1).reshape(Bl, 6 * D)
        gsm[l] = dict(b_in=_merge_cols(dbm[0], dbf[0, :H], f_off), conv_a_b=dcab[0], ln_conv_g=dlcg[0], ln_conv_b=dlcb[0],
                      b_mix_out=dbmo[0], ln1_g=dg1[0], ln1_b=db1[0], ffn_conv_b=jnp.concatenate([dcg[0], dcl[0]]),
                      ln2_g=dg2[0], ln2_b=db2[0], conv_a_w=dcaw[:KW], ffn_conv_w=jnp.concatenate([dwg[:KF], dwl[:KF]], axis=1))
    grad_x = dx.reshape(Bl, S, D)

    small_names = ["b_in", "conv_a_b", "ln_conv_g", "ln_conv_b", "b_mix_out", "ln1_g", "ln1_b", "ffn_conv_b", "ln2_g", "ln2_b",
                   "conv_a_w", "ffn_conv_w"]
    gs_list = [jnp.stack(dmods)] + [jnp.stack([gsm[l][n] for l in range(L)]) for n in small_names]
    gspack, gssegs = _pack(gs_list, D, F32)
    gs_all = _exchange(gspack, all_to_all=False, name="gather_small_grads")
    start_grads(after=gs_all)
    dmod_all = jnp.moveaxis(_unpack(gs_all, gssegs)[0], 0, 1).reshape(L, N_DEV * Bl, 6 * D)
    g_small = dict(zip(small_names, _unpack(_slot_sum(gs_all, name="sum_small_grads"), gssegs)[1:]))
    g_small["conv_a_w"] = my_cols(g_small["conv_a_w"], C // N_DEV)
    g_small["ffn_conv_w"] = my_cols(g_small["ffn_conv_w"], 2 * F // N_DEV)
    g_small["w_ada"] = _ada_bwd(c_all, my_cols(dmod_all, n_ada), name="ada_bwd")
    g_small["b_ada"] = jnp.stack([_colsum(dmod_all[l], name=f"db_ada_{l}")[0] for l in range(L)])

    given = dict(w_in=(w_in, m_w_in, v_w_in), w_conv_proj=(w_conv_proj, m_w_conv_proj, v_w_conv_proj),
                 w_attn_proj=(w_attn_proj, m_w_attn_proj, v_w_attn_proj), w_mix_out=(w_mix_out, m_w_mix_out, v_w_mix_out),
                 w_ffn_up=(w_ffn_up, m_w_ffn_up, v_w_ffn_up), w_ffn_down=(w_ffn_down, m_w_ffn_down, v_w_ffn_down),
                 w_ada=(w_ada, m_w_ada, v_w_ada), b_ada=(b_ada, m_b_ada, v_b_ada), b_in=(b_in, m_b_in, v_b_in),
                 conv_a_w=(conv_a_w, m_conv_a_w, v_conv_a_w), conv_a_b=(conv_a_b, m_conv_a_b, v_conv_a_b),
                 ln_conv_g=(ln_conv_g, m_ln_conv_g, v_ln_conv_g), ln_conv_b=(ln_conv_b, m_ln_conv_b, v_ln_conv_b),
                 b_mix_out=(b_mix_out, m_b_mix_out, v_b_mix_out), ln1_g=(ln1_g, m_ln1_g, v_ln1_g), ln1_b=(ln1_b, m_ln1_b, v_ln1_b),
                 ffn_conv_w=(ffn_conv_w, m_ffn_conv_w, v_ffn_conv_w), ffn_conv_b=(ffn_conv_b, m_ffn_conv_b, v_ffn_conv_b),
                 ln2_g=(ln2_g, m_ln2_g, v_ln2_g), ln2_b=(ln2_b, m_ln2_b, v_ln2_b))
    res, kinds = {}, ("grad", "delta", "new_m", "new_v")
    loc_names = ["w_ada", "b_ada"] + small_names
    packs = [_pack([g_small[n] for n in loc_names], D, F32)] + [_pack([given[n][i] for n in loc_names], D, F32) for i in range(3)]
    outs = _adamw(packs[0][0][None], packs[1][0], packs[2][0], packs[3][0], name="adamw_small")
    for kind, packed in zip(kinds, outs):
        for n, a in zip(loc_names, _unpack(packed, packs[0][1])):
            res[kind, n] = a

    big_parts = {}
    after = outs[0]
    for gi, (state, send, segs, grp) in enumerate(g_state):
        landed = _exchange_wait(state, after, name=f"exchange_grads_wait_{gi}")
        landed = lax.dynamic_update_index_in_dim(landed, lax.dynamic_index_in_dim(send, me, 0, keepdims=False), me, 0)
        wmv = [_pack(shard_items([given[n][j] for n in big_names], grp), D, F32)[0] for j in range(3)]
        outs = _adamw(landed, *wmv, name=f"adamw_big_{gi}")
        for kind, packed in zip(kinds, outs):
            for (l, i), a in zip(grp, _unpack(packed, segs)):
                big_parts[kind, l, i] = a.T if transposed[i] else a
        after = outs[0]
    for kind in kinds:
        for i, n in enumerate(big_names):
            res[kind, n] = jnp.stack([big_parts[kind, l, i] for l in range(L)])

    order = ["w_ada", "b_ada", "w_in", "b_in", "conv_a_w", "conv_a_b", "ln_conv_g", "ln_conv_b", "w_conv_proj", "w_attn_proj",
             "w_mix_out", "b_mix_out", "ln1_g", "ln1_b", "w_ffn_up", "ffn_conv_w", "ffn_conv_b", "w_ffn_down", "ln2_g", "ln2_b"]
    return (loss, grad_x, *[res[k, n] for k in ("grad", "delta", "new_m", "new_v") for n in order])
```

```python
import functools
import math
from typing import NamedTuple

import jax
import jax.numpy as jnp
from jax import lax
from jax.experimental import pallas as pl
from jax.experimental.pallas import tpu as pltpu

F32, BF16 = jnp.float32, jnp.bfloat16
LN_EPS = 1e-5
ADAM_LR, ADAM_B1, ADAM_B2, ADAM_EPS, ADAM_WD, ADAM_STEP = 0.001, 0.9, 0.999, 1e-08, 0.01, 10
N_DEV = 8
LANES = 128
VMEM_LIMIT = 56 * 1024 * 1024
NEG = -1e30
NT = (((1,), (1,)), ((), ()))
TN = (((0,), (0,)), ((), ()))


class Cfg(NamedTuple):
    L: int
    Bl: int
    S: int
    D: int
    C: int
    KW: int
    H: int
    Dh: int
    F: int
    KF: int

    @property
    def T(self): return self.Bl * self.S
    @property
    def AW(self): return self.H * self.Dh
    @property
    def NM(self): return 2 * self.C + 3 * self.AW + 2 * self.D
    @property
    def q_off(self): return 2 * self.C
    @property
    def g_off(self): return 2 * self.C + 3 * self.AW
    @property
    def alpha(self): return (2.0 * self.L) ** 0.25


def _pcall(body, **kw):
    return pl.pallas_call(body, **kw)


def _params(*sem):
    return pltpu.CompilerParams(dimension_semantics=sem, vmem_limit_bytes=VMEM_LIMIT)


def _pick(n, prefs):
    for p in prefs:
        if n % p == 0:
            return p
    return n


def _sigmoid(x):
    return 1.0 / (1.0 + jnp.exp(-x))


def _ln_stats(x):
    mu = jnp.mean(x, axis=-1, keepdims=True)
    xc = x - mu
    var = jnp.mean(xc * xc, axis=-1, keepdims=True)
    rstd = lax.rsqrt(var + LN_EPS)
    return xc * rstd, rstd


def _ln_bwd(dxh, xh, rstd):
    return rstd * (dxh - jnp.mean(dxh, axis=-1, keepdims=True) - xh * jnp.mean(dxh * xh, axis=-1, keepdims=True))


def _matmul(a, b, *, mode, name, bias=None, add=None, out_dtype=F32, tm=None, tn=None, tk=None, after=None):
    if mode == "tn":
        K, M = a.shape
    else:
        M, K = a.shape
    N = b.shape[0] if mode == "nt" else b.shape[1]
    lane_tiles = (1536, 1408, 1024, 768, 512, 256, 128)
    tm = tm or _pick(M, lane_tiles if mode == "tn" else (1024, 512, 256, 128, 64, 32, 16, 8))
    tn = tn or _pick(N, lane_tiles)
    tk = tk or _pick(K, (512, 256, 128) if mode == "tn" else lane_tiles)
    nk = K // tk
    dn = {"nn": (((1,), (0,)), ((), ())), "nt": NT, "tn": TN}[mode]
    has_bias, has_add, has_after = bias is not None, add is not None, after is not None

    def body(*refs):
        a_ref, b_ref = refs[0], refs[1]
        pos = 2
        bias_ref = refs[pos] if has_bias else None
        pos += has_bias
        add_ref = refs[pos] if has_add else None
        pos += has_add + has_after
        o_ref = refs[pos]
        part = lax.dot_general(a_ref[...], b_ref[...], dn, preferred_element_type=F32)

        def finish(acc):
            if has_bias:
                acc = acc + bias_ref[...]
            if has_add:
                acc = acc + add_ref[...]
            o_ref[...] = acc.astype(out_dtype)

        if nk == 1:
            finish(part)
        else:
            acc_ref = refs[pos + 1]
            k = pl.program_id(2)

            @pl.when(k == 0)
            def _():
                acc_ref[...] = part

            @pl.when(k > 0)
            def _():
                acc_ref[...] += part

            @pl.when(k == nk - 1)
            def _():
                finish(acc_ref[...])

    a_spec = pl.BlockSpec((tk, tm), lambda i, j, k: (k, i)) if mode == "tn" else pl.BlockSpec((tm, tk), lambda i, j, k: (i, k))
    b_spec = pl.BlockSpec((tn, tk), lambda i, j, k: (j, k)) if mode == "nt" else pl.BlockSpec((tk, tn), lambda i, j, k: (k, j))
    in_specs, args = [a_spec, b_spec], [a, b]
    if has_bias:
        in_specs.append(pl.BlockSpec((1, tn), lambda i, j, k: (0, j)))
        args.append(bias)
    if has_add:
        in_specs.append(pl.BlockSpec((tm, tn), lambda i, j, k: (i, j)))
        args.append(add)
    if has_after:
        in_specs.append(pl.BlockSpec(memory_space=pl.ANY))
        args.append(after)
    return _pcall(
        body, name=name, grid=(M // tm, N // tn, nk), in_specs=in_specs,
        out_specs=pl.BlockSpec((tm, tn), lambda i, j, k: (i, j)),
        out_shape=jax.ShapeDtypeStruct((M, N), out_dtype),
        scratch_shapes=[pltpu.VMEM((tm, tn), F32)] if nk > 1 else [],
        compiler_params=_params("parallel", "parallel", "arbitrary"),
    )(*args)


def _colsum(x, *, name):
    T, N = x.shape
    tr = _pick(T, (512, 256, 128, 64, 32, 16))
    tc = _pick(N, (1536, 1024, 512, 256, 128))

    def body(x_ref, o_ref):
        @pl.when(pl.program_id(1) == 0)
        def _():
            o_ref[...] = jnp.zeros_like(o_ref)

        o_ref[...] += jnp.sum(x_ref[...].astype(F32), axis=0, keepdims=True)

    return _pcall(body, name=name, grid=(N // tc, T // tr), in_specs=[pl.BlockSpec((tr, tc), lambda j, i: (i, j))],
                  out_specs=pl.BlockSpec((1, tc), lambda j, i: (0, j)), out_shape=jax.ShapeDtypeStruct((1, N), F32),
                  compiler_params=_params("parallel", "arbitrary"))(x)


def _row_tile(cfg):
    return _pick(cfg.S, (256, 128, 64, 32, 16, 8))


def _ln_mod_fwd(x, shift, scale, cfg, *, name):
    tr = _row_tile(cfg)
    tpb = cfg.S // tr

    def body(x_ref, sh_ref, sc_ref, u_ref):
        xh, _ = _ln_stats(x_ref[...])
        u_ref[...] = (xh * (1.0 + sc_ref[0]) + sh_ref[0]).astype(BF16)

    row = pl.BlockSpec((tr, cfg.D), lambda i: (i, 0))
    per_b = pl.BlockSpec((1, 1, cfg.D), lambda i: (i // tpb, 0, 0))
    return _pcall(body, name=name, grid=(cfg.T // tr,), in_specs=[row, per_b, per_b], out_specs=row,
                  out_shape=jax.ShapeDtypeStruct((cfg.T, cfg.D), BF16), compiler_params=_params("parallel"))(x, shift, scale)


def _res_ln_fwd(xin, br, gate, g, b, cfg, *, name, nxt=None):
    tr = _row_tile(cfg)
    tpb = cfg.S // tr
    alpha = cfg.alpha

    def body(*refs):
        x_ref, br_ref, gt_ref, g_ref, b_ref = refs[:5]
        r = alpha * x_ref[...] + (1.0 + gt_ref[0]) * br_ref[...]
        xh, _ = _ln_stats(r)
        xo = xh * g_ref[...] + b_ref[...]
        if nxt is None:
            refs[5][...] = xo
        else:
            sh_ref, sc_ref, xo_ref, u_ref = refs[5:]
            xo_ref[...] = xo
            uh, _ = _ln_stats(xo)
            u_ref[...] = (uh * (1.0 + sc_ref[0]) + sh_ref[0]).astype(BF16)

    row = pl.BlockSpec((tr, cfg.D), lambda i: (i, 0))
    per_b = pl.BlockSpec((1, 1, cfg.D), lambda i: (i // tpb, 0, 0))
    vec = pl.BlockSpec((1, cfg.D), lambda i: (0, 0))
    in_specs, args = [row, row, per_b, vec, vec], [xin, br, gate, g, b]
    out_specs, out_shape = row, jax.ShapeDtypeStruct((cfg.T, cfg.D), F32)
    if nxt is not None:
        in_specs += [per_b, per_b]
        args += list(nxt)
        out_specs = [row, row]
        out_shape = [out_shape, jax.ShapeDtypeStruct((cfg.T, cfg.D), BF16)]
    return _pcall(body, name=name, grid=(cfg.T // tr,), in_specs=in_specs, out_specs=out_specs, out_shape=out_shape,
                  compiler_params=_params("parallel"))(*args)


def _loss_grad(y, tgt, cfg, *, name):
    tr = _row_tile(cfg)
    nt = cfg.T // tr
    inv_d = 1.0 / cfg.D

    def body(y_ref, t_ref, dy_ref, ls_ref):
        e = y_ref[...] - t_ref[...]
        dy_ref[...] = e * inv_d
        ls_ref[...] = jnp.full((1, 1, LANES), jnp.sum(e * e), F32)

    row = pl.BlockSpec((tr, cfg.D), lambda i: (i, 0))
    return _pcall(body, name=name, grid=(nt,), in_specs=[row, row],
                  out_specs=[row, pl.BlockSpec((1, 1, LANES), lambda i: (i, 0, 0))],
                  out_shape=[jax.ShapeDtypeStruct((cfg.T, cfg.D), F32), jax.ShapeDtypeStruct((nt, 1, LANES), F32)],
                  compiler_params=_params("parallel"))(y, tgt)


def _res_ln_bwd(dy, xin, br, gate, g, cfg, *, name):
    tr = _row_tile(cfg)
    tpb = cfg.S // tr
    alpha = cfg.alpha

    def body(dy_ref, x_ref, br_ref, gt_ref, g_ref, dx_ref, dbr_ref, dg_ref, db_ref, dgt_ref, dbs_ref):
        i = pl.program_id(0)

        @pl.when(i == 0)
        def _():
            dg_ref[...] = jnp.zeros_like(dg_ref)
            db_ref[...] = jnp.zeros_like(db_ref)
            dbs_ref[...] = jnp.zeros_like(dbs_ref)

        @pl.when(i % tpb == 0)
        def _():
            dgt_ref[...] = jnp.zeros_like(dgt_ref)

        dy, brv, one_gate = dy_ref[...], br_ref[...], 1.0 + gt_ref[0]
        xh, rstd = _ln_stats(alpha * x_ref[...] + one_gate * brv)
        dg_ref[...] += jnp.sum(dy * xh, axis=0, keepdims=True)
        db_ref[...] += jnp.sum(dy, axis=0, keepdims=True)
        dr = _ln_bwd(dy * g_ref[...], xh, rstd)
        dx_ref[...] = alpha * dr
        dbr = one_gate * dr
        dbr_ref[...] = dbr.astype(BF16)
        dbs_ref[...] += jnp.sum(dbr, axis=0, keepdims=True)
        dgt_ref[0] += jnp.sum(dr * brv, axis=0, keepdims=True)

    row = pl.BlockSpec((tr, cfg.D), lambda i: (i, 0))
    per_b = pl.BlockSpec((1, 1, cfg.D), lambda i: (i // tpb, 0, 0))
    vec = pl.BlockSpec((1, cfg.D), lambda i: (0, 0))
    vs = jax.ShapeDtypeStruct((1, cfg.D), F32)
    return _pcall(body, name=name, grid=(cfg.T // tr,), in_specs=[row, row, row, per_b, vec],
                  out_specs=[row, row, vec, vec, per_b, vec],
                  out_shape=[jax.ShapeDtypeStruct((cfg.T, cfg.D), F32), jax.ShapeDtypeStruct((cfg.T, cfg.D), BF16), vs, vs,
                             jax.ShapeDtypeStruct((cfg.Bl, 1, cfg.D), F32), vs],
                  compiler_params=_params("arbitrary"))(dy, xin, br, gate, g)


def _ln_mod_bwd(du, xin, scale, dres, cfg, *, name):
    tr = _row_tile(cfg)
    tpb = cfg.S // tr

    def body(du_ref, x_ref, sc_ref, dres_ref, dx_ref, dsc_ref, dsh_ref):
        @pl.when(pl.program_id(0) % tpb == 0)
        def _():
            dsc_ref[...] = jnp.zeros_like(dsc_ref)
            dsh_ref[...] = jnp.zeros_like(dsh_ref)

        du = du_ref[...]
        xh, rstd = _ln_stats(x_ref[...])
        dsc_ref[0] += jnp.sum(du * xh, axis=0, keepdims=True)
        dsh_ref[0] += jnp.sum(du, axis=0, keepdims=True)
        dx_ref[...] = _ln_bwd(du * (1.0 + sc_ref[0]), xh, rstd) + dres_ref[...]

    row = pl.BlockSpec((tr, cfg.D), lambda i: (i, 0))
    per_b = pl.BlockSpec((1, 1, cfg.D), lambda i: (i // tpb, 0, 0))
    bs = jax.ShapeDtypeStruct((cfg.Bl, 1, cfg.D), F32)
    return _pcall(body, name=name, grid=(cfg.T // tr,), in_specs=[row, row, per_b, row], out_specs=[row, per_b, per_b],
                  out_shape=[jax.ShapeDtypeStruct((cfg.T, cfg.D), F32), bs, bs],
                  compiler_params=_params("arbitrary"))(du, xin, scale, dres)


def _merge_tiles(cfg):
    tr = _pick(cfg.T, (512, 256, 128, 64, 32, 16))
    tc = _pick(math.gcd(cfg.g_off, cfg.D), (512, 256, 128))
    return tr, tc


def _merge_fwd(zm, ya, yb, cfg, *, name):
    tr, tc = _merge_tiles(cfg)
    ga0, gb0 = cfg.g_off // tc, (cfg.g_off + cfg.D) // tc

    def body(ga_ref, gb_ref, ya_ref, yb_ref, m_ref):
        m_ref[...] = (_sigmoid(ga_ref[...]) * ya_ref[...] + _sigmoid(gb_ref[...]) * yb_ref[...]).astype(BF16)

    blk = pl.BlockSpec((tr, tc), lambda i, j: (i, j))
    return _pcall(body, name=name, grid=(cfg.T // tr, cfg.D // tc),
                  in_specs=[pl.BlockSpec((tr, tc), lambda i, j: (i, ga0 + j)), pl.BlockSpec((tr, tc), lambda i, j: (i, gb0 + j)), blk, blk],
                  out_specs=blk, out_shape=jax.ShapeDtypeStruct((cfg.T, cfg.D), BF16),
                  compiler_params=_params("parallel", "parallel"))(zm, zm, ya, yb)


def _merge_bwd(dm, zm, ya, yb, cfg, *, name):
    tr, tc = _merge_tiles(cfg)
    ga0, gb0 = cfg.g_off // tc, (cfg.g_off + cfg.D) // tc

    def body(dm_ref, ga_ref, gb_ref, ya_ref, yb_ref, dya_ref, dyb_ref, dga_ref, dgb_ref):
        dm = dm_ref[...]
        ga, gb = _sigmoid(ga_ref[...]), _sigmoid(gb_ref[...])
        dya_ref[...] = (dm * ga).astype(BF16)
        dyb_ref[...] = (dm * gb).astype(BF16)
        dga_ref[...] = (dm * ya_ref[...] * ga * (1.0 - ga)).astype(BF16)
        dgb_ref[...] = (dm * yb_ref[...] * gb * (1.0 - gb)).astype(BF16)

    blk = pl.BlockSpec((tr, tc), lambda i, j: (i, j))
    o = jax.ShapeDtypeStruct((cfg.T, cfg.D), BF16)
    return _pcall(body, name=name, grid=(cfg.T // tr, cfg.D // tc),
                  in_specs=[blk, pl.BlockSpec((tr, tc), lambda i, j: (i, ga0 + j)), pl.BlockSpec((tr, tc), lambda i, j: (i, gb0 + j)), blk, blk],
                  out_specs=[blk] * 4, out_shape=[o] * 4, compiler_params=_params("parallel", "parallel"))(dm, zm, zm, ya, yb)


CONV_A_HALO = 32
CONV_A_CHUNK = 32


def _conv_a_tile(cfg):
    assert cfg.KW - 1 <= CONV_A_HALO
    return _pick(cfg.S, (256, 128, 64, 32))


def _conv_a_fwd(zm, w, cb, g, b, cfg, *, name):
    C, KW, HALO, CH = cfg.C, cfg.KW, CONV_A_HALO, CONV_A_CHUNK
    ts = _conv_a_tile(cfg)
    tpb = cfg.S // ts
    lead = HALO - (KW - 1)

    def body(z_ref, zp_ref, w_ref, cb_ref, g_ref, b_ref, o_ref, a0_s):
        first = pl.program_id(0) % tpb == 0
        prev = zp_ref[:, :C] * _sigmoid(zp_ref[:, C:])
        a0_s[0:HALO, :] = jnp.where(first, 0.0, prev)
        a0_s[HALO:HALO + ts, :] = z_ref[:, :C] * _sigmoid(z_ref[:, C:])
        for r0 in range(0, ts, CH):
            acc = jnp.zeros((CH, C), F32)
            for k in range(KW):
                acc = acc + w_ref[k:k + 1, :] * a0_s[r0 + lead + k:r0 + lead + k + CH, :]
            xh, _ = _ln_stats(acc + cb_ref[...])
            a2 = xh * g_ref[...] + b_ref[...]
            o_ref[r0:r0 + CH, :] = (a2 * _sigmoid(a2)).astype(BF16)

    hb = ts // HALO
    vec = pl.BlockSpec((1, C), lambda i: (0, 0))
    return _pcall(body, name=name, grid=(cfg.T // ts,),
                  in_specs=[pl.BlockSpec((ts, 2 * C), lambda i: (i, 0)),
                            pl.BlockSpec((HALO, 2 * C), lambda i: (jnp.maximum(i * hb - 1, 0), 0)),
                            pl.BlockSpec((32, C), lambda i: (0, 0)), vec, vec, vec],
                  out_specs=pl.BlockSpec((ts, C), lambda i: (i, 0)), out_shape=jax.ShapeDtypeStruct((cfg.T, C), BF16),
                  scratch_shapes=[pltpu.VMEM((HALO + ts, C), F32)], compiler_params=_params("parallel"))(zm, zm, w, cb, g, b)


def _conv_a_bwd(da3, zm, w, cb, g, b, cfg, *, name):
    C, KW, HALO, CH = cfg.C, cfg.KW, CONV_A_HALO, CONV_A_CHUNK
    ts = _conv_a_tile(cfg)
    tpb = cfg.S // ts
    nt = cfg.T // ts
    lead = HALO - (KW - 1)
    ext = ts + HALO

    def body(z_ref, zp_ref, zn_ref, d_ref, dn_ref, w_ref, cb_ref, g_ref, b_ref,
             dz_ref, dw_ref, dcb_ref, dg_ref, db_ref, a0_s, d3_s, da1_s):
        i = pl.program_id(0)
        first, last = i % tpb == 0, i % tpb == tpb - 1

        @pl.when(i == 0)
        def _():
            dw_ref[...] = jnp.zeros_like(dw_ref)
            dcb_ref[...] = jnp.zeros_like(dcb_ref)
            dg_ref[...] = jnp.zeros_like(dg_ref)
            db_ref[...] = jnp.zeros_like(db_ref)

        a0_s[0:HALO, :] = jnp.where(first, 0.0, zp_ref[:, :C] * _sigmoid(zp_ref[:, C:]))
        a0_s[HALO:HALO + ts, :] = z_ref[:, :C] * _sigmoid(z_ref[:, C:])
        a0_s[HALO + ts:HALO + ext, :] = zn_ref[:, :C] * _sigmoid(zn_ref[:, C:])
        d3_s[0:ts, :] = d_ref[...]
        d3_s[ts:ext, :] = jnp.where(last, 0.0, dn_ref[...])
        dcb, dg, db = jnp.zeros((1, C), F32), jnp.zeros((1, C), F32), jnp.zeros((1, C), F32)
        for r0 in range(0, ext, CH):
            acc = jnp.zeros((CH, C), F32)
            for k in range(KW):
                acc = acc + w_ref[k:k + 1, :] * a0_s[r0 + lead + k:r0 + lead + k + CH, :]
            xh, rstd = _ln_stats(acc + cb_ref[...])
            a2 = xh * g_ref[...] + b_ref[...]
            sg = _sigmoid(a2)
            da2 = d3_s[r0:r0 + CH, :] * (sg * (1.0 + a2 * (1.0 - sg)))
            da1 = _ln_bwd(da2 * g_ref[...], xh, rstd)
            da1_s[r0:r0 + CH, :] = da1
            if r0 < ts:
                dg = dg + jnp.sum(da2 * xh, axis=0, keepdims=True)
                db = db + jnp.sum(da2, axis=0, keepdims=True)
                dcb = dcb + jnp.sum(da1, axis=0, keepdims=True)
        dg_ref[...] += dg
        db_ref[...] += db
        dcb_ref[...] += dcb
        for k in range(KW):
            dwk = jnp.zeros((1, C), F32)
            for r0 in range(0, ts, CH):
                dwk = dwk + jnp.sum(da1_s[r0:r0 + CH, :] * a0_s[r0 + lead + k:r0 + lead + k + CH, :], axis=0, keepdims=True)
            dw_ref[k:k + 1, :] += dwk
        for r0 in range(0, ts, CH):
            da0 = jnp.zeros((CH, C), F32)
            for k in range(KW):
                da0 = da0 + w_ref[k:k + 1, :] * da1_s[r0 + KW - 1 - k:r0 + KW - 1 - k + CH, :]
            val, sg = z_ref[r0:r0 + CH, :C], _sigmoid(z_ref[r0:r0 + CH, C:])
            dz_ref[r0:r0 + CH, :C] = (da0 * sg).astype(BF16)
            dz_ref[r0:r0 + CH, C:] = (da0 * val * sg * (1.0 - sg)).astype(BF16)

    hb = ts // HALO
    nhb = cfg.T // HALO
    vec = pl.BlockSpec((1, C), lambda i: (0, 0))
    vs = jax.ShapeDtypeStruct((1, C), F32)
    return _pcall(body, name=name, grid=(nt,),
                  in_specs=[pl.BlockSpec((ts, 2 * C), lambda i: (i, 0)),
                            pl.BlockSpec((HALO, 2 * C), lambda i: (jnp.maximum(i * hb - 1, 0), 0)),
                            pl.BlockSpec((HALO, 2 * C), lambda i: (jnp.minimum((i + 1) * hb, nhb - 1), 0)),
                            pl.BlockSpec((ts, C), lambda i: (i, 0)),
                            pl.BlockSpec((HALO, C), lambda i: (jnp.minimum((i + 1) * hb, nhb - 1), 0)),
                            pl.BlockSpec((32, C), lambda i: (0, 0)), vec, vec, vec],
                  out_specs=[pl.BlockSpec((ts, 2 * C), lambda i: (i, 0)), pl.BlockSpec((32, C), lambda i: (0, 0)), vec, vec, vec],
                  out_shape=[jax.ShapeDtypeStruct((cfg.T, 2 * C), BF16), jax.ShapeDtypeStruct((32, C), F32), vs, vs, vs],
                  scratch_shapes=[pltpu.VMEM((HALO + ext, C), F32), pltpu.VMEM((ext, C), F32), pltpu.VMEM((ext, C), F32)],
                  compiler_params=_params("arbitrary"))(zm, zm, zm, da3, da3, w, cb, g, b)


def _cum_tile(cfg):
    return _pick(cfg.S, (256, 128, 64, 32, 16, 8))


def _fgate_fwd(zf, cfg, *, name):
    tc = _cum_tile(cfg)
    tpb = cfg.S // tc
    _, hp = _attn_tiles(cfg)
    nb = cfg.H // hp

    def body(z_ref, o_ref, carry):
        @pl.when(pl.program_id(0) % tpb == 0)
        def _():
            carry[...] = jnp.zeros_like(carry)

        z = z_ref[...]
        logf = jnp.minimum(z, 0.0) - jnp.log(1.0 + jnp.exp(-jnp.abs(z)))
        tri = (lax.broadcasted_iota(jnp.int32, (tc, tc), 0) >= lax.broadcasted_iota(jnp.int32, (tc, tc), 1)).astype(F32)
        cum = jnp.dot(tri, logf, precision=lax.Precision.HIGHEST, preferred_element_type=F32) + carry[...]
        carry[...] = cum[tc - 1:tc, :]
        o_ref[0] = cum
        for b in range(1, nb):
            o_ref[b] = pltpu.roll(cum, LANES - hp * b, axis=1)

    return _pcall(body, name=name, grid=(cfg.T // tc,), in_specs=[pl.BlockSpec((tc, LANES), lambda i: (i, 0))],
                  out_specs=pl.BlockSpec((nb, tc, LANES), lambda i: (0, i, 0)),
                  out_shape=jax.ShapeDtypeStruct((nb, cfg.T, LANES), F32), scratch_shapes=[pltpu.VMEM((1, LANES), F32)],
                  compiler_params=_params("arbitrary"))(zf)


def _fgate_bwd(dcum_c, zf, cfg, *, name):
    tc = _cum_tile(cfg)
    tpb = cfg.S // tc
    nt = cfg.T // tc
    _, hp = _attn_tiles(cfg)
    nb = cfg.H // hp

    def body(d_ref, z_ref, o_ref, carry):
        @pl.when(pl.program_id(0) % tpb == 0)
        def _():
            carry[...] = jnp.zeros_like(carry)

        d = d_ref[0]
        for b in range(1, nb):
            d = d + pltpu.roll(d_ref[b], hp * b, axis=1)
        tri = (lax.broadcasted_iota(jnp.int32, (tc, tc), 0) <= lax.broadcasted_iota(jnp.int32, (tc, tc), 1)).astype(F32)
        suf = jnp.dot(tri, d, precision=lax.Precision.HIGHEST, preferred_element_type=F32) + carry[...]
        o_ref[...] = (suf * _sigmoid(-z_ref[...])).astype(BF16)
        carry[...] = suf[0:1, :]

    blk = pl.BlockSpec((tc, LANES), lambda i: (nt - 1 - i, 0))
    return _pcall(body, name=name, grid=(nt,), in_specs=[pl.BlockSpec((nb, tc, LANES), lambda i: (0, nt - 1 - i, 0)), blk],
                  out_specs=blk, out_shape=jax.ShapeDtypeStruct((cfg.T, LANES), BF16),
                  scratch_shapes=[pltpu.VMEM((1, LANES), F32)], compiler_params=_params("arbitrary"))(dcum_c, zf)


def _attn_tiles(cfg):
    assert LANES % cfg.Dh == 0 and cfg.H % (LANES // cfg.Dh) == 0
    t = _pick(cfg.S, (256, 128))
    return t, LANES // cfg.Dh


BIAS_LANES = 3


def _head_lanes(hd, cfg, hp):
    li = lax.broadcasted_iota(jnp.int32, (1, LANES), 1)
    own = (li >= hd * cfg.Dh) & (li < (hd + 1) * cfg.Dh)
    return own, li, ((hd + 1) % hp) * cfg.Dh


def _q_aug(q, hd, cfg, hp):
    own, li, b0 = _head_lanes(hd, cfg, hp)
    ones = ((li >= b0) & (li < b0 + BIAS_LANES)).astype(F32)
    return jnp.where(own, q * cfg.Dh ** -0.5, ones).astype(BF16)


def _k_aug(k, ck, hd, cfg, hp):
    own, li, b0 = _head_lanes(hd, cfg, hp)
    hi = ck.astype(BF16).astype(F32)
    mid = (ck - hi).astype(BF16).astype(F32)
    lo = ck - hi - mid
    bias = jnp.where(li == b0, -hi, jnp.where(li == b0 + 1, -mid, jnp.where(li == b0 + 2, -lo, 0.0)))
    return jnp.where(own, k, bias).astype(BF16)


def _attn_fwd(zm, cum_c, cfg, *, name):
    S, Dh = cfg.S, cfg.Dh
    t, hp = _attn_tiles(cfg)
    assert hp >= 2
    nq, nb = S // t, cfg.H // hp
    qb, kb, vb = cfg.q_off // LANES, (cfg.q_off + cfg.AW) // LANES, (cfg.q_off + 2 * cfg.AW) // LANES

    def body(q_ref, k_ref, v_ref, cc_ref, o_ref, o32_ref, lse_ref, ka_s, vt_s):
        qi = pl.program_id(2)

        @pl.when(qi == 0)
        def _():
            def prep(c, _):
                r = pl.multiple_of(c * t, t)
                kc = k_ref[pl.ds(r, t), :]
                for hd in range(hp):
                    ka_s[hd, pl.ds(r, t), :] = _k_aug(kc, cc_ref[0, pl.ds(r, t), hd:hd + 1], hd, cfg, hp)
                vt_s[:, pl.ds(r, t)] = v_ref[pl.ds(r, t), :].T.astype(BF16)
                return 0

            lax.fori_loop(0, nq, prep, 0)

        causal = lax.broadcasted_iota(jnp.int32, (t, t), 0) <= lax.broadcasted_iota(jnp.int32, (t, t), 1)
        qf = q_ref[...]
        qa = [_q_aug(qf, hd, cfg, hp) for hd in range(hp)]

        def chunk(j, carry, masked):
            r = pl.multiple_of(j * t, t)
            new = []
            for hd in range(hp):
                m, l, acc = carry[hd]
                s = lax.dot_general(ka_s[hd, pl.ds(r, t), :], qa[hd], NT, preferred_element_type=F32)
                if masked:
                    s = jnp.where(causal, s, NEG)
                m_new = jnp.maximum(m, jnp.max(s, axis=0, keepdims=True))
                a = jnp.exp(m - m_new)
                p = jnp.exp(s - m_new)
                l = a * l + jnp.sum(p, axis=0, keepdims=True)
                p_hi = p.astype(BF16)
                p_lo = (p - p_hi.astype(F32)).astype(BF16)
                vt = vt_s[hd * Dh:(hd + 1) * Dh, pl.ds(r, t)]
                acc = a * acc + (jnp.dot(vt, p_hi, preferred_element_type=F32) + jnp.dot(vt, p_lo, preferred_element_type=F32))
                new.append((m_new, l, acc))
            return tuple(new)

        init = tuple((jnp.full((1, t), NEG, F32), jnp.zeros((1, t), F32), jnp.zeros((Dh, t), F32)) for _ in range(hp))
        res = chunk(qi, lax.fori_loop(0, qi, functools.partial(chunk, masked=False), init), True)
        o = jnp.concatenate([acc / l for _, l, acc in res], axis=0).T
        o_ref[...] = o.astype(BF16)
        o32_ref[...] = o
        lse_ref[...] = jnp.zeros_like(lse_ref)
        for hd in range(hp):
            lse_ref[0, 0, hd:hd + 1, :] = res[hd][0] + jnp.log(res[hd][1])

    return _pcall(body, name=name, grid=(cfg.Bl, nb, nq),
                  in_specs=[pl.BlockSpec((t, LANES), lambda b, h, i: (b * nq + i, qb + h)),
                            pl.BlockSpec((S, LANES), lambda b, h, i: (b, kb + h)),
                            pl.BlockSpec((S, LANES), lambda b, h, i: (b, vb + h)),
                            pl.BlockSpec((1, S, LANES), lambda b, h, i: (h, b, 0))],
                  out_specs=[pl.BlockSpec((t, LANES), lambda b, h, i: (b * nq + i, h)),
                             pl.BlockSpec((t, LANES), lambda b, h, i: (b * nq + i, h)),
                             pl.BlockSpec((1, 1, 8, t), lambda b, h, i: (b, h, 0, i))],
                  out_shape=[jax.ShapeDtypeStruct((cfg.T, cfg.AW), BF16), jax.ShapeDtypeStruct((cfg.T, cfg.AW), F32),
                             jax.ShapeDtypeStruct((cfg.Bl, nb, 8, S), F32)],
                  scratch_shapes=[pltpu.VMEM((hp, S, LANES), BF16), pltpu.VMEM((LANES, S), BF16)],
                  compiler_params=_params("parallel", "parallel", "arbitrary"))(zm, zm, zm, cum_c)


def _attn_bwd(zm, cum_c, o, do, lse, cfg, *, name):
    S, Dh = cfg.S, cfg.Dh
    t, hp = _attn_tiles(cfg)
    nq, nb = S // t, cfg.H // hp
    qb, kb, vb = cfg.q_off // LANES, (cfg.q_off + cfg.AW) // LANES, (cfg.q_off + 2 * cfg.AW) // LANES
    scale = Dh ** -0.5

    def body(q_ref, k_ref, v_ref, cc_ref, o_ref, do_ref, lse_ref, dq_ref, dk_ref, dv_ref, dcc_ref,
             ka_s, qa_s, vz_s, kt_s, dd_s, dqt_s):
        li = lax.broadcasted_iota(jnp.int32, (1, LANES), 1)
        ri = lax.broadcasted_iota(jnp.int32, (LANES, 1), 0)
        causal = lax.broadcasted_iota(jnp.int32, (t, t), 0) <= lax.broadcasted_iota(jnp.int32, (t, t), 1)

        def prep(c, _):
            r = pl.multiple_of(c * t, t)
            kc, vc, qc = k_ref[pl.ds(r, t), :], v_ref[pl.ds(r, t), :], q_ref[pl.ds(r, t), :]
            prod_t = (do_ref[pl.ds(r, t), :].astype(F32) * o_ref[pl.ds(r, t), :].astype(F32)).T
            for hd in range(hp):
                own = _head_lanes(hd, cfg, hp)[0]
                ka_s[hd, pl.ds(r, t), :] = _k_aug(kc, cc_ref[0, pl.ds(r, t), hd:hd + 1], hd, cfg, hp)
                qa_s[hd, pl.ds(r, t), :] = _q_aug(qc, hd, cfg, hp)
                vz_s[hd, pl.ds(r, t), :] = jnp.where(own, vc, 0.0).astype(BF16)
                dd_s[hd:hd + 1, pl.ds(r, t)] = jnp.sum(prod_t[hd * Dh:(hd + 1) * Dh, :], axis=0, keepdims=True)
            kt_s[:, pl.ds(r, t)] = kc.T.astype(BF16)
            dqt_s[:, pl.ds(r, t)] = jnp.zeros((LANES, t), F32)
            return 0

        lax.fori_loop(0, nq, prep, 0)

        def kv_step(j, _):
            rk = pl.multiple_of(j * t, t)

            def tile(i, carry, masked):
                rq = pl.multiple_of(i * t, t)
                dob = do_ref[pl.ds(rq, t), :]
                new, dq_t = [], None
                for hd in range(hp):
                    dk_h, dv_h, dsum_h = carry[hd]
                    qa = qa_s[hd, pl.ds(rq, t), :]
                    s = lax.dot_general(ka_s[hd, pl.ds(rk, t), :], qa, NT, preferred_element_type=F32)
                    p = jnp.exp(s - lse_ref[0, 0, hd:hd + 1, pl.ds(rq, t)])
                    if masked:
                        p = jnp.where(causal, p, 0.0)
                    dp = lax.dot_general(vz_s[hd, pl.ds(rk, t), :], dob, NT, preferred_element_type=F32)
                    ds = p * (dp - dd_s[hd:hd + 1, pl.ds(rq, t)])
                    dsb = ds.astype(BF16)
                    dv_h = dv_h + jnp.dot(p.astype(BF16), dob, preferred_element_type=F32)
                    dk_h = dk_h + jnp.dot(dsb, qa, preferred_element_type=F32)
                    dq_h = jnp.dot(kt_s[:, pl.ds(rk, t)], dsb, preferred_element_type=F32)
                    dq_t = dq_h if hd == 0 else jnp.where((ri >= hd * Dh) & (ri < (hd + 1) * Dh), dq_h, dq_t)
                    for c0 in range(0, t, LANES):
                        dsum_h = dsum_h + ds[:, c0:c0 + LANES]
                    new.append((dk_h, dv_h, dsum_h))
                dqt_s[:, pl.ds(rq, t)] += dq_t * scale
                return tuple(new)

            zero = tuple((jnp.zeros((t, LANES), F32),) * 3 for _ in range(hp))
            res = lax.fori_loop(j + 1, nq, functools.partial(tile, masked=False), tile(j, zero, True))
            dk, dv, dcc = res[0][0], res[0][1], jnp.zeros((t, LANES), F32)
            for hd in range(hp):
                own = _head_lanes(hd, cfg, hp)[0]
                if hd > 0:
                    dk, dv = jnp.where(own, res[hd][0], dk), jnp.where(own, res[hd][1], dv)
                dcc = dcc + jnp.where(li == hd, -jnp.sum(res[hd][2], axis=1, keepdims=True), 0.0)
            dk_ref[pl.ds(rk, t), :] = dk.astype(BF16)
            dv_ref[pl.ds(rk, t), :] = dv.astype(BF16)
            dcc_ref[0, pl.ds(rk, t), :] = dcc
            return 0

        lax.fori_loop(0, nq, kv_step, 0)

        def finish(c, _):
            r = pl.multiple_of(c * t, t)
            dq_ref[pl.ds(r, t), :] = dqt_s[:, pl.ds(r, t)].T.astype(BF16)
            return 0

        lax.fori_loop(0, nq, finish, 0)

    blk = pl.BlockSpec((S, LANES), lambda b, h: (b, h))
    cc = pl.BlockSpec((1, S, LANES), lambda b, h: (h, b, 0))
    os_ = jax.ShapeDtypeStruct((cfg.T, cfg.AW), BF16)
    return _pcall(body, name=name, grid=(cfg.Bl, nb),
                  in_specs=[pl.BlockSpec((S, LANES), lambda b, h: (b, qb + h)), pl.BlockSpec((S, LANES), lambda b, h: (b, kb + h)),
                            pl.BlockSpec((S, LANES), lambda b, h: (b, vb + h)), cc, blk, blk,
                            pl.BlockSpec((1, 1, 8, S), lambda b, h: (b, h, 0, 0))],
                  out_specs=[blk, blk, blk, cc],
                  out_shape=[os_, os_, os_, jax.ShapeDtypeStruct((nb, cfg.T, LANES), F32)],
                  scratch_shapes=[pltpu.VMEM((hp, S, LANES), BF16)] * 3 + [pltpu.VMEM((LANES, S), BF16),
                                  pltpu.VMEM((8, S), F32), pltpu.VMEM((LANES, S), F32)],
                  compiler_params=_params("parallel", "parallel"))(zm, zm, zm, cum_c, o, do, lse)


FFN_HALO = 8
FFN_CHUNK = 16


def _ffn_tiles(cfg):
    assert cfg.KF - 1 <= FFN_HALO
    return _pick(cfg.S, (512, 256, 128, 64, 32, 16, 8)), _pick(cfg.F, (256, 128))


def _gelu(x):
    return 0.5 * x * (1.0 + lax.erf(x * (2.0 ** -0.5)))


def _gelu_grad(x):
    return 0.5 * (1.0 + lax.erf(x * (2.0 ** -0.5))) + x * jnp.exp(-0.5 * x * x) * ((2.0 * math.pi) ** -0.5)


def _ffn_conv_fwd(h0, w, cb, cfg, *, name):
    KF, HALO = cfg.KF, FFN_HALO
    ts, tf = _ffn_tiles(cfg)
    tpb, nf = cfg.S // ts, cfg.F // tf
    lead = HALO - (KF - 1)

    CH = FFN_CHUNK

    def body(g_ref, gp_ref, l_ref, lp_ref, wg_ref, wl_ref, cg_ref, cl_ref, o_ref, g_s, l_s):
        first = pl.program_id(1) % tpb == 0
        for s, main, prev in ((g_s, g_ref, gp_ref), (l_s, l_ref, lp_ref)):
            s[0:HALO, :] = jnp.where(first, 0.0, prev[...])
            s[HALO:HALO + CH, :] = main[0:CH, :]
        wg, wl = [wg_ref[k:k + 1, :] for k in range(KF)], [wl_ref[k:k + 1, :] for k in range(KF)]
        for r0 in range(0, ts, CH):
            hg, hl = cg_ref[...], cl_ref[...]
            for k in range(KF):
                if r0 == 0:
                    xg, xl = g_s[lead + k:lead + k + CH, :], l_s[lead + k:lead + k + CH, :]
                else:
                    a = r0 - (KF - 1) + k
                    xg, xl = g_ref[a:a + CH, :], l_ref[a:a + CH, :]
                hg, hl = hg + wg[k] * xg, hl + wl[k] * xl
            o_ref[r0:r0 + CH, :] = (_gelu(hg) * hl).astype(BF16)

    hb = ts // HALO
    prev = lambda off: pl.BlockSpec((HALO, tf), lambda j, i: (jnp.maximum(i * hb - 1, 0), off + j))
    main = lambda off: pl.BlockSpec((ts, tf), lambda j, i: (i, off + j))
    wsp = lambda off: pl.BlockSpec((8, tf), lambda j, i: (0, off + j))
    vsp = lambda off: pl.BlockSpec((1, tf), lambda j, i: (0, off + j))
    return _pcall(body, name=name, grid=(nf, cfg.T // ts),
                  in_specs=[main(0), prev(0), main(nf), prev(nf), wsp(0), wsp(nf), vsp(0), vsp(nf)],
                  out_specs=pl.BlockSpec((ts, tf), lambda j, i: (i, j)), out_shape=jax.ShapeDtypeStruct((cfg.T, cfg.F), BF16),
                  scratch_shapes=[pltpu.VMEM((HALO + CH, tf), F32)] * 2,
                  compiler_params=_params("parallel", "parallel"))(h0, h0, h0, h0, w, w, cb, cb)


def _ffn_conv_bwd(df, h0, w, cb, cfg, *, name):
    KF, HALO = cfg.KF, FFN_HALO
    ts, tf = _ffn_tiles(cfg)
    tpb, nf = cfg.S // ts, cfg.F // tf
    lead = HALO - (KF - 1)
    ext = ts + HALO

    CH = FFN_CHUNK

    def body(g_ref, gp_ref, gn_ref, l_ref, lp_ref, ln_ref, d_ref, dn_ref, wg_ref, wl_ref, cg_ref, cl_ref,
             dg_ref, dl_ref, dwg_ref, dwl_ref, dcg_ref, dcl_ref, gh_s, lh_s, gt_s, lt_s, dhg_s, dhl_s):
        i = pl.program_id(1)
        first, last = i % tpb == 0, i % tpb == tpb - 1

        @pl.when(i == 0)
        def _():
            dwg_ref[...] = jnp.zeros_like(dwg_ref)
            dwl_ref[...] = jnp.zeros_like(dwl_ref)
            dcg_ref[...] = jnp.zeros_like(dcg_ref)
            dcl_ref[...] = jnp.zeros_like(dcl_ref)

        for head, tail, main, prev, nxt in ((gh_s, gt_s, g_ref, gp_ref, gn_ref), (lh_s, lt_s, l_ref, lp_ref, ln_ref)):
            head[0:HALO, :] = jnp.where(first, 0.0, prev[...])
            head[HALO:HALO + CH, :] = main[0:CH, :]
            tail[0:HALO, :] = main[ts - HALO:ts, :]
            tail[HALO:2 * HALO, :] = nxt[...]
        wg, wl = [wg_ref[k:k + 1, :] for k in range(KF)], [wl_ref[k:k + 1, :] for k in range(KF)]

        def grads(hg, hl, d):
            return d * hl * _gelu_grad(hg), d * _gelu(hg)

        for r0 in range(0, ts, CH):
            hg, hl = cg_ref[...], cl_ref[...]
            for k in range(KF):
                if r0 == 0:
                    xg, xl = gh_s[lead + k:lead + k + CH, :], lh_s[lead + k:lead + k + CH, :]
                else:
                    a = r0 - (KF - 1) + k
                    xg, xl = g_ref[a:a + CH, :], l_ref[a:a + CH, :]
                hg, hl = hg + wg[k] * xg, hl + wl[k] * xl
            dhg_s[r0:r0 + CH, :], dhl_s[r0:r0 + CH, :] = grads(hg, hl, d_ref[r0:r0 + CH, :])
        hg, hl = cg_ref[...], cl_ref[...]
        for k in range(KF):
            hg, hl = hg + wg[k] * gt_s[lead + k:lead + k + HALO, :], hl + wl[k] * lt_s[lead + k:lead + k + HALO, :]
        dhg_s[ts:ext, :], dhl_s[ts:ext, :] = grads(hg, hl, jnp.where(last, 0.0, dn_ref[...]))

        for dh_s, x_ref, wk, dx_ref, dw_ref, dc_ref in ((dhg_s, g_ref, wg, dg_ref, dwg_ref, dcg_ref),
                                                        (dhl_s, l_ref, wl, dl_ref, dwl_ref, dcl_ref)):
            dw_acc = [jnp.zeros((CH, tf), F32) for _ in range(KF)]
            for r0 in range(0, ts, CH):
                x = x_ref[r0:r0 + CH, :]
                dx = jnp.zeros((CH, tf), F32)
                for k in range(KF):
                    dhk = dh_s[r0 + KF - 1 - k:r0 + KF - 1 - k + CH, :]
                    dx = dx + wk[k] * dhk
                    dw_acc[k] = dw_acc[k] + x * dhk
                    if k == KF - 1:
                        dc_acc = dhk if r0 == 0 else dc_acc + dhk
                dx_ref[r0:r0 + CH, :] = dx.astype(BF16)
            for k in range(KF):
                dw_ref[k:k + 1, :] += jnp.sum(dw_acc[k], axis=0, keepdims=True)
            dc_ref[...] += jnp.sum(dc_acc, axis=0, keepdims=True)

    hb = ts // HALO
    nhb = cfg.T // HALO
    main = lambda off: pl.BlockSpec((ts, tf), lambda j, i: (i, off + j))
    prev = lambda off: pl.BlockSpec((HALO, tf), lambda j, i: (jnp.maximum(i * hb - 1, 0), off + j))
    nxt = lambda off: pl.BlockSpec((HALO, tf), lambda j, i: (jnp.minimum((i + 1) * hb, nhb - 1), off + j))
    wsp = lambda off: pl.BlockSpec((8, tf), lambda j, i: (0, off + j))
    vsp = lambda off: pl.BlockSpec((1, tf), lambda j, i: (0, off + j))
    dxs, dws, dcs = (jax.ShapeDtypeStruct((cfg.T, cfg.F), BF16), jax.ShapeDtypeStruct((8, cfg.F), F32),
                     jax.ShapeDtypeStruct((1, cfg.F), F32))
    return _pcall(body, name=name, grid=(nf, cfg.T // ts),
                  in_specs=[main(0), prev(0), nxt(0), main(nf), prev(nf), nxt(nf), main(0), nxt(0),
                            wsp(0), wsp(nf), vsp(0), vsp(nf)],
                  out_specs=[main(0), main(0), wsp(0), wsp(0), vsp(0), vsp(0)],
                  out_shape=[dxs, dxs, dws, dws, dcs, dcs],
                  scratch_shapes=[pltpu.VMEM((HALO + CH, tf), F32)] * 2 + [pltpu.VMEM((2 * HALO, tf), F32)] * 2
                  + [pltpu.VMEM((ext, tf), F32)] * 2,
                  compiler_params=_params("parallel", "arbitrary"))(h0, h0, h0, h0, h0, h0, df, df, w, w, cb, cb)


def _ada_fwd(c_all, w, b, *, name):
    L, D, n = w.shape
    B = c_all.shape[0]

    def body(c_ref, w_ref, b_ref, o_ref):
        c = c_ref[...]
        act = (c * _sigmoid(c)).astype(BF16)
        o_ref[0] = jnp.dot(act, w_ref[0].astype(BF16), preferred_element_type=F32) + b_ref[0]

    return _pcall(body, name=name, grid=(L,),
                  in_specs=[pl.BlockSpec((B, D), lambda l: (0, 0)), pl.BlockSpec((1, D, n), lambda l: (l, 0, 0)),
                            pl.BlockSpec((1, 1, n), lambda l: (l, 0, 0))],
                  out_specs=pl.BlockSpec((1, B, n), lambda l: (l, 0, 0)), out_shape=jax.ShapeDtypeStruct((L, B, n), F32),
                  compiler_params=_params("parallel"))(c_all, w, b)


def _ada_bwd(c_all, dmod, *, name):
    L, B, n = dmod.shape
    D = c_all.shape[1]

    def body(c_ref, d_ref, o_ref):
        c = c_ref[...]
        act = (c * _sigmoid(c)).astype(BF16)
        o_ref[0] = lax.dot_general(act, d_ref[0].astype(BF16), TN, preferred_element_type=F32)

    return _pcall(body, name=name, grid=(L,),
                  in_specs=[pl.BlockSpec((B, D), lambda l: (0, 0)), pl.BlockSpec((1, B, n), lambda l: (l, 0, 0))],
                  out_specs=pl.BlockSpec((1, D, n), lambda l: (l, 0, 0)), out_shape=jax.ShapeDtypeStruct((L, D, n), F32),
                  compiler_params=_params("parallel"))(c_all, dmod)


def _slot_sum(x, *, name):
    n, R, W = x.shape
    tr = _pick(R, (256, 128, 64, 32, 16, 8))

    def body(x_ref, o_ref):
        acc = x_ref[0].astype(F32)
        for k in range(1, n):
            acc = acc + x_ref[k].astype(F32)
        o_ref[...] = acc

    return _pcall(body, name=name, grid=(R // tr,), in_specs=[pl.BlockSpec((n, tr, W), lambda i: (0, i, 0))],
                  out_specs=pl.BlockSpec((tr, W), lambda i: (i, 0)), out_shape=jax.ShapeDtypeStruct((R, W), F32),
                  compiler_params=_params("parallel"))(x)


def _adamw(gs, w, m, v, *, name):
    n, R, W = gs.shape
    tr = _pick(R, (256, 128, 64, 32, 16, 8))
    c1, c2 = 1.0 - ADAM_B1 ** ADAM_STEP, 1.0 - ADAM_B2 ** ADAM_STEP

    def body(g_ref, w_ref, m_ref, v_ref, go_ref, d_ref, mo_ref, vo_ref):
        g = g_ref[0].astype(F32)
        for k in range(1, n):
            g = g + g_ref[k].astype(F32)
        m2 = ADAM_B1 * m_ref[...] + (1.0 - ADAM_B1) * g
        v2 = ADAM_B2 * v_ref[...] + (1.0 - ADAM_B2) * (g * g)
        go_ref[...] = g
        mo_ref[...] = m2
        vo_ref[...] = v2
        d_ref[...] = -ADAM_LR * ((m2 / c1) / (jnp.sqrt(v2 / c2) + ADAM_EPS) + ADAM_WD * w_ref[...])

    blk = pl.BlockSpec((tr, W), lambda i: (i, 0))
    o = jax.ShapeDtypeStruct((R, W), F32)
    return _pcall(body, name=name, grid=(R // tr,), in_specs=[pl.BlockSpec((n, tr, W), lambda i: (0, i, 0)), blk, blk, blk],
                  out_specs=[blk] * 4, out_shape=[o] * 4, compiler_params=_params("parallel"))(gs, w, m, v)


def _peer_copies(x_ref, land_ref, send_sems, recv_sems, all_to_all):
    mx, my, mc = lax.axis_index("x"), lax.axis_index("y"), lax.axis_index("c")
    me = 4 * mx + 2 * my + mc
    copies = []
    for k in range(1, N_DEV):
        px, py, pc = mx ^ ((k >> 2) & 1), my ^ ((k >> 1) & 1), mc ^ (k & 1)
        copies.append(pltpu.make_async_remote_copy(
            src_ref=x_ref.at[4 * px + 2 * py + pc] if all_to_all else x_ref, dst_ref=land_ref.at[me],
            send_sem=send_sems.at[k - 1], recv_sem=recv_sems.at[k - 1], device_id=(px, py, pc),
            device_id_type=pl.DeviceIdType.MESH))
    return copies


def _gather_two_level(x, *, name, after=None):
    def body(x_ref, *rest):
        o_ref, send_sems, recv_sems, local_sem = rest[-4:]
        mx, my, mc = lax.axis_index("x"), lax.axis_index("y"), lax.axis_index("c")
        me, sibling = (mx, my, mc), (mx, my, 1 - mc)
        chips = [(1 - mx, my), (mx, 1 - my), (1 - mx, 1 - my)]

        def slot(px, py, pc):
            return o_ref.at[4 * px + 2 * py + pc]

        def copy(k, block, to, src=None):
            return pltpu.make_async_remote_copy(
                src_ref=slot(*block) if src is None else src, dst_ref=slot(*block), send_sem=send_sems.at[k],
                recv_sem=recv_sems.at[k], device_id=to, device_id_type=pl.DeviceIdType.MESH)

        mine = pltpu.make_async_copy(x_ref, slot(*me), local_sem)
        mine.start()
        first = [copy(0, me, sibling, src=x_ref)] + [copy(1 + j, me, (*chip, mc), src=x_ref) for j, chip in enumerate(chips)]
        for cp in first:
            cp.start()
        passed = [copy(4 + j, (*chip, mc), sibling) for j, chip in enumerate(chips)]
        for j, chip in enumerate(chips):
            copy(1 + j, (*chip, mc), me).wait_recv()
            passed[j].start()
        copy(0, sibling, me).wait_recv()
        for j, chip in enumerate(chips):
            copy(4 + j, (*chip, 1 - mc), me).wait_recv()
        for cp in first + passed:
            cp.wait_send()
        mine.wait()

    anyspec = pl.BlockSpec(memory_space=pl.ANY)
    args = [x] if after is None else [x, after]
    return _pcall(body, name=name, in_specs=[anyspec] * len(args), out_specs=anyspec,
                  out_shape=jax.ShapeDtypeStruct((N_DEV,) + tuple(x.shape), x.dtype),
                  scratch_shapes=[pltpu.SemaphoreType.DMA((N_DEV - 1,)), pltpu.SemaphoreType.DMA((N_DEV - 1,)),
                                  pltpu.SemaphoreType.DMA(())])(*args)


_HBM = pl.BlockSpec(memory_space=pltpu.HBM)
_SEM = pl.BlockSpec(memory_space=pltpu.SEMAPHORE)
_EFFECT = pltpu.SideEffectType.DATAFLOW_SIDE_EFFECTING


def _exchange_start(x, *, all_to_all, name, after=None):
    blk = x.shape[1:] if all_to_all else x.shape
    land = lax.empty((N_DEV,) + tuple(blk), x.dtype)
    has_after = after is not None

    def body(*refs):
        x_ref, land_ref = refs[0], refs[1]
        send_sems, recv_sems, _, _, token, local_sem = refs[2 + has_after:]
        for cp in _peer_copies(x_ref, land_ref, send_sems, recv_sems, all_to_all):
            cp.start()
        me = 4 * lax.axis_index("x") + 2 * lax.axis_index("y") + lax.axis_index("c")
        mine = pltpu.make_async_copy(x_ref.at[me] if all_to_all else x_ref, land_ref.at[me], local_sem)
        mine.start()
        mine.wait()
        token[...] = jnp.zeros_like(token)

    n_sem = pltpu.SemaphoreType.DMA((N_DEV - 1,))
    args = [pltpu.with_memory_space_constraint(x, pltpu.HBM), pltpu.with_memory_space_constraint(land, pltpu.HBM)]
    in_specs = [_HBM, _HBM]
    if has_after:
        args.append(after)
        in_specs.append(pl.BlockSpec(memory_space=pl.ANY))
    send_sems, recv_sems, x_thru, land_thru, token = _pcall(
        body, name=name, in_specs=in_specs,
        out_shape=(n_sem, n_sem, pltpu.HBM(x.shape, x.dtype), pltpu.HBM(land.shape, land.dtype),
                   jax.ShapeDtypeStruct((8, LANES), F32)),
        out_specs=(_SEM, _SEM, _HBM, _HBM, pl.BlockSpec(memory_space=pltpu.VMEM)), input_output_aliases={0: 2, 1: 3},
        scratch_shapes=[pltpu.SemaphoreType.DMA(())],
        compiler_params=pltpu.CompilerParams(has_side_effects=_EFFECT))(*args)
    return (send_sems, recv_sems, x_thru, land_thru, all_to_all), token


def _exchange_wait(state, after, *, name):
    send_sems, recv_sems, x_thru, land_thru, all_to_all = state

    def body(x_ref, land_ref, send_sems, recv_sems, after_ref, x_dead, landed):
        for cp in _peer_copies(x_ref, land_ref, send_sems, recv_sems, all_to_all):
            cp.wait_send()
            cp.wait_recv()

    return _pcall(
        body, name=name, in_specs=(_HBM, _HBM, _SEM, _SEM, pl.BlockSpec(memory_space=pl.ANY)),
        out_shape=(pltpu.HBM(x_thru.shape, x_thru.dtype), pltpu.HBM(land_thru.shape, land_thru.dtype)),
        out_specs=(_HBM, _HBM), input_output_aliases={0: 0, 1: 1},
        compiler_params=pltpu.CompilerParams(has_side_effects=_EFFECT))(x_thru, land_thru, send_sems, recv_sems, after)[1]


def _exchange(x, *, all_to_all, name, after=None):
    blk = x.shape[1:] if all_to_all else x.shape

    def body(x_ref, *rest):
        o_ref, send_sems, recv_sems, local_sem = rest[-4:]
        me = 4 * lax.axis_index("x") + 2 * lax.axis_index("y") + lax.axis_index("c")
        mine = pltpu.make_async_copy(x_ref.at[me] if all_to_all else x_ref, o_ref.at[me], local_sem)
        mine.start()
        copies = _peer_copies(x_ref, o_ref, send_sems, recv_sems, all_to_all)
        for cp in copies:
            cp.start()
        for cp in copies:
            cp.wait()
        mine.wait()

    anyspec = pl.BlockSpec(memory_space=pl.ANY)
    args = [x] if after is None else [x, after]
    return _pcall(body, name=name, in_specs=[anyspec] * len(args), out_specs=anyspec,
                  out_shape=jax.ShapeDtypeStruct((N_DEV,) + tuple(blk), x.dtype),
                  scratch_shapes=[pltpu.SemaphoreType.DMA((N_DEV - 1,)), pltpu.SemaphoreType.DMA((N_DEV - 1,)),
                                  pltpu.SemaphoreType.DMA(())])(*args)


PACK_ROWS = 16


def _pack(arrs, width, dtype, lead=0):
    parts, segs, r = [], [], 0
    for a in arrs:
        lshape, shape = a.shape[:lead], a.shape[lead:]
        n = math.prod(shape)
        rows = -(-n // width)
        rows_p = -(-rows // PACK_ROWS) * PACK_ROWS
        if n == rows * width:
            blk = a.reshape(lshape + (rows, width)).astype(dtype)
            parts.append(jnp.pad(blk, [(0, 0)] * lead + [(0, rows_p - rows), (0, 0)]) if rows_p > rows else blk)
        else:
            flat = jnp.pad(a.reshape(lshape + (n,)).astype(dtype), [(0, 0)] * lead + [(0, rows_p * width - n)])
            parts.append(flat.reshape(lshape + (rows_p, width)))
        segs.append((r, n, shape))
        r += rows_p
    return jnp.concatenate(parts, axis=lead), segs


def _unpack(p, segs):
    lshape, width = p.shape[:-2], p.shape[-1]
    outs = []
    for r, n, shape in segs:
        rows = -(-n // width)
        blk = p[..., r:r + rows, :]
        if n != rows * width:
            blk = blk.reshape(lshape + (rows * width,))[..., :n]
        outs.append(blk.reshape(lshape + shape))
    return outs


def _split_cols(a, f_off, h):
    return jnp.concatenate([a[..., :f_off], a[..., f_off + h:]], axis=-1), a[..., f_off:f_off + h]


def _merge_cols(main, f, f_off):
    return jnp.concatenate([main[..., :f_off], f, main[..., f_off:]], axis=-1)


def _pad_to(a, n, axis):
    pad = [(0, 0)] * a.ndim
    pad[axis] = (0, n - a.shape[axis])
    return jnp.pad(a, pad)


def kernel(x, c, w_ada, b_ada, w_in, b_in, conv_a_w, conv_a_b, ln_conv_g, ln_conv_b, w_conv_proj, w_attn_proj, w_mix_out, b_mix_out, ln1_g, ln1_b, w_ffn_up, ffn_conv_w, ffn_conv_b, w_ffn_down, ln2_g, ln2_b, loss_target, m_w_ada, m_b_ada, m_w_in, m_b_in, m_conv_a_w, m_conv_a_b, m_ln_conv_g, m_ln_conv_b, m_w_conv_proj, m_w_attn_proj, m_w_mix_out, m_b_mix_out, m_ln1_g, m_ln1_b, m_w_ffn_up, m_ffn_conv_w, m_ffn_conv_b, m_w_ffn_down, m_ln2_g, m_ln2_b, v_w_ada, v_b_ada, v_w_in, v_b_in, v_conv_a_w, v_conv_a_b, v_ln_conv_g, v_ln_conv_b, v_w_conv_proj, v_w_attn_proj, v_w_mix_out, v_b_mix_out, v_ln1_g, v_ln1_b, v_w_ffn_up, v_ffn_conv_w, v_ffn_conv_b, v_w_ffn_down, v_ln2_g, v_ln2_b):
    L, D = w_ada.shape[0], w_ada.shape[1]
    Bl, S, _ = x.shape
    C, KW, AW = conv_a_b.shape[1], conv_a_w.shape[1], w_attn_proj.shape[1]
    F, KF, n_in_all = ffn_conv_b.shape[1] // 2, ffn_conv_w.shape[1], b_in.shape[1]
    H = n_in_all - 2 * C - 3 * AW - 2 * D
    cfg = Cfg(L=L, Bl=Bl, S=S, D=D, C=C, KW=KW, H=H, Dh=AW // H, F=F, KF=KF)
    T, NM = cfg.T, cfg.NM
    f_off = 2 * C + 3 * AW
    n_ada = w_ada.shape[2]
    me = 4 * lax.axis_index("x") + 2 * lax.axis_index("y") + lax.axis_index("c")

    def my_cols(a, n):
        return lax.dynamic_slice_in_dim(a, me * n, n, axis=a.ndim - 1)

    spack, ssegs = _pack([c, conv_a_w, ffn_conv_w], D, F32)
    c_g, caw_g, fcw_g = _unpack(_exchange(spack, all_to_all=False, name="gather_small"), ssegs)
    c_all = c_g.reshape(N_DEV * Bl, D)
    caw = _pad_to(jnp.moveaxis(caw_g, 0, 2).reshape(L, KW, C), 32, 1)
    fcw = _pad_to(jnp.moveaxis(fcw_g, 0, 2).reshape(L, KF, 2 * F), 8, 1)

    mod_part = _ada_fwd(c_all, w_ada, my_cols(b_ada, n_ada)[:, None, :], name="ada_fwd")
    mod_send = jnp.moveaxis(mod_part.reshape(L, N_DEV, Bl, n_ada), 1, 0).reshape(N_DEV, L * Bl, n_ada)
    mod_recv = _exchange(mod_send, all_to_all=True, name="exchange_mod")
    mod = jnp.moveaxis(mod_recv.reshape(N_DEV, L, Bl, n_ada), 0, 2).reshape(L, Bl, 6, 1, D)
    shift1, scale1, gate1, shift2, scale2, gate2 = (mod[:, :, i] for i in range(6))

    big_names = ["w_in", "w_conv_proj", "w_attn_proj", "w_mix_out", "w_ffn_up", "w_ffn_down"]
    transposed = (True, True, True, False, True, False)

    def shard_items(arrs, grp):
        return [arrs[i][l].T if transposed[i] else arrs[i][l] for l, i in grp]

    W = [dict() for _ in range(L)]

    def set_weights(landed, segs, grp):
        for (l, i), a in zip(grp, _unpack(landed, segs)):
            a = a.reshape((-1, a.shape[-1]))
            if i == 0:
                wm_t, wf_t = _split_cols(a.T, f_off, H)
                bm, bf = _split_cols(b_in[l], f_off, H)
                W[l].update(wm_t=wm_t.T, wf_t=_pad_to(wf_t.T, LANES, 0), bm=bm[None], bf=_pad_to(bf, LANES, 0)[None])
            else:
                W[l][("w_cp_t", "w_ap_t", "w_mo", "w_up_t", "w_dn")[i - 1]] = a

    big_w = (w_in, w_conv_proj, w_attn_proj, w_mix_out, w_ffn_up, w_ffn_down)
    w_groups = [[(l, i) for i in range(6)] for l in range(L)]
    pack, segs = _pack(shard_items(big_w, w_groups[0]), D, BF16)
    landed0 = _gather_two_level(pack, name="gather_weights_0", after=mod_recv)
    set_weights(landed0, segs, w_groups[0])
    w_state, token = {}, landed0
    for l in range(1, L):
        pack, segs = _pack(shard_items(big_w, w_groups[l]), D, BF16)
        state, token = _exchange_start(pack, all_to_all=False, name=f"gather_weights_start_{l}", after=token)
        w_state[l] = (state, pack, segs)

    def wait_weights(l, after):
        state, pack, segs = w_state[l]
        set_weights(_exchange_wait(state, after, name=f"gather_weights_wait_{l}"), segs, w_groups[l])

    xf = x.reshape(T, D)
    u = _ln_mod_fwd(xf, shift1[0], scale1[0], cfg, name="ln_mod_fwd")
    saved = []
    xin = xf
    for l in range(L):
        w = W[l]
        if l > 0:
            wait_weights(l, u)
        zm = _matmul(u, w["wm_t"], mode="nt", bias=w["bm"], name=f"in_proj_{l}", after=token if l == 0 else None)
        zf = _matmul(u, w["wf_t"], mode="nt", bias=w["bf"], name=f"in_proj_f_{l}")
        a3 = _conv_a_fwd(zm, caw[l], conv_a_b[l][None], ln_conv_g[l][None], ln_conv_b[l][None], cfg, name=f"conv_a_fwd_{l}")
        cum_c = _fgate_fwd(zf, cfg, name=f"fgate_fwd_{l}")
        o, o32, lse = _attn_fwd(zm, cum_c, cfg, name=f"attn_fwd_{l}")
        ya =_matmul(a3, w["w_cp_t"], mode="nt", name=f"conv_proj_{l}")
        yb = _matmul(o, w["w_ap_t"], mode="nt", name=f"attn_proj_{l}")
        mg = _merge_fwd(zm, ya, yb, cfg, name=f"merge_fwd_{l}")
        mix = _matmul(mg, w["w_mo"], mode="nn", bias=b_mix_out[l][None], name=f"mix_out_{l}")
        x1, u2 = _res_ln_fwd(xin, mix, gate1[l], ln1_g[l][None], ln1_b[l][None], cfg, name=f"res_ln1_fwd_{l}",
                             nxt=(shift2[l], scale2[l]))
        h0 = _matmul(u2, w["w_up_t"], mode="nt", name=f"ffn_up_{l}")
        fa = _ffn_conv_fwd(h0, fcw[l], ffn_conv_b[l][None], cfg, name=f"ffn_conv_fwd_{l}")
        ffn = _matmul(fa, w["w_dn"], mode="nn", name=f"ffn_down_{l}")
        saved.append(dict(x=xin, u=u, zm=zm, zf=zf, a3=a3, cum_c=cum_c, o=o, o32=o32, lse=lse, ya=ya, yb=yb, mg=mg, mix=mix,
                          x1=x1, u2=u2, h0=h0, fa=fa, ffn=ffn))
        if l + 1 < L:
            xin, u = _res_ln_fwd(x1, ffn, gate2[l], ln2_g[l][None], ln2_b[l][None], cfg, name=f"res_ln2_fwd_{l}",
                                 nxt=(shift1[l + 1], scale1[l + 1]))
        else:
            xin = _res_ln_fwd(x1, ffn, gate2[l], ln2_g[l][None], ln2_b[l][None], cfg, name=f"res_ln2_fwd_{l}")

    dx, loss_tiles = _loss_grad(xin, loss_target.reshape(T, D), cfg, name="loss_grad")
    loss = lax.psum(0.5 / D * jnp.sum(loss_tiles[:, 0, 0]), ("x", "y", "c"))

    gbig = {}
    g_groups = [[(l, i) for i in range(6)] for l in reversed(range(1, L))] + [[(0, 4), (0, 5)], [(0, 1), (0, 2), (0, 3)], [(0, 0)]]
    g_state = []

    def start_grads(after=None):
        grp = g_groups[len(g_state)]
        send, segs = _pack([gbig[k].reshape((N_DEV, -1, gbig[k].shape[1])) for k in grp], D, BF16, lead=1)
        state, tok = _exchange_start(send, all_to_all=True, name=f"exchange_grads_start_{len(g_state)}", after=after)
        g_state.append((state, send, segs, grp))
        return tok

    gsm = [dict() for _ in range(L)]
    dmods = [None] * L
    token = None
    for l in reversed(range(L)):
        w, s = W[l], saved[l]
        dres2, dffn, dg2, db2, dgate2, _ = _res_ln_bwd(dx, s["x1"], s["ffn"], gate2[l], ln2_g[l][None], cfg, name=f"res_ln2_bwd_{l}")
        dfa = _matmul(dffn, w["w_dn"], mode="nt", name=f"d_ffn_act_{l}", after=token)
        gbig[l, 5] = _matmul(s["fa"], dffn, mode="tn", name=f"dw_ffn_down_{l}")
        dh0g, dh0l, dwg, dwl, dcg, dcl = _ffn_conv_bwd(dfa, s["h0"], fcw[l], ffn_conv_b[l][None], cfg, name=f"ffn_conv_bwd_{l}")
        dh0 = jnp.concatenate([dh0g, dh0l], axis=1)
        du2 = _matmul(dh0, w["w_up_t"], mode="nn", name=f"d_u2_{l}")
        gbig[l, 4] = _matmul(dh0, s["u2"], mode="tn", name=f"dw_ffn_up_{l}")
        token = start_grads() if l == 0 else None
        dx1, dscale2, dshift2 = _ln_mod_bwd(du2, s["x1"], scale2[l], dres2, cfg, name=f"ln_mod2_bwd_{l}")
        dres1, dmix, dg1, db1, dgate1, dbmo = _res_ln_bwd(dx1, s["x"], s["mix"], gate1[l], ln1_g[l][None], cfg, name=f"res_ln1_bwd_{l}")
        dmg = _matmul(dmix, w["w_mo"], mode="nt", name=f"d_merge_{l}", after=token)
        gbig[l, 3] = _matmul(s["mg"], dmix, mode="tn", name=f"dw_mix_out_{l}")
        dya, dyb, dzga, dzgb = _merge_bwd(dmg, s["zm"], s["ya"], s["yb"], cfg, name=f"merge_bwd_{l}")
        gbig[l, 1] = _matmul(dya, s["a3"], mode="tn", name=f"dw_conv_proj_{l}")
        da3 = _matmul(dya, w["w_cp_t"], mode="nn", name=f"d_a3_{l}")
        gbig[l, 2] = _matmul(dyb, s["o"], mode="tn", name=f"dw_attn_proj_{l}")
        token = start_grads() if l == 0 else None
        do = _matmul(dyb, w["w_ap_t"], mode="nn", out_dtype=BF16, name=f"d_o_{l}", after=token)
        dq, dk, dv, dcum_c = _attn_bwd(s["zm"], s["cum_c"], s["o32"], do, s["lse"], cfg, name=f"attn_bwd_{l}")
        dzf = _fgate_bwd(dcum_c, s["zf"], cfg, name=f"fgate_bwd_{l}")
        dzglu, dcaw, dcab, dlcg, dlcb = _conv_a_bwd(da3, s["zm"], caw[l], conv_a_b[l][None], ln_conv_g[l][None],
                                                    ln_conv_b[l][None], cfg, name=f"conv_a_bwd_{l}")
        dzm = jnp.concatenate([dzglu, dq, dk, dv, dzga, dzgb], axis=1)
        du1 = _matmul(dzf, w["wf_t"], mode="nn", name=f"d_u1_f_{l}")
        du1 = _matmul(dzm, w["wm_t"], mode="nn", add=du1, name=f"d_u1_{l}")
        dwm_t = _matmul(dzm, s["u"], mode="tn", name=f"dw_in_{l}")
        dwf_t = _matmul(dzf, s["u"], mode="tn", name=f"dw_in_f_{l}")
        gbig[l, 0] = _merge_cols(dwm_t.T, dwf_t[:H].T, f_off).T
        token = start_grads() if l > 0 else None
        dbm, dbf = _colsum(dzm, name=f"db_in_{l}"), _colsum(dzf, name=f"db_in_f_{l}")
        dx, dscale1, dshift1 = _ln_mod_bwd(du1, s["x"], scale1[l], dres1, cfg, name=f"ln_mod1_bwd_{l}")
        dmods[l] = jnp.concatenate([dshift1, dscale1, dgate1, dshift2, dscale2, dgate2], axis=1).reshape(Bl, 6 * D)
        gsm[l] = dict(b_in=_merge_cols(dbm[0], dbf[0, :H], f_off), conv_a_b=dcab[0], ln_conv_g=dlcg[0], ln_conv_b=dlcb[0],
                      b_mix_out=dbmo[0], ln1_g=dg1[0], ln1_b=db1[0], ffn_conv_b=jnp.concatenate([dcg[0], dcl[0]]),
                      ln2_g=dg2[0], ln2_b=db2[0], conv_a_w=dcaw[:KW], ffn_conv_w=jnp.concatenate([dwg[:KF], dwl[:KF]], axis=1))
    grad_x = dx.reshape(Bl, S, D)

    small_names = ["b_in", "conv_a_b", "ln_conv_g", "ln_conv_b", "b_mix_out", "ln1_g", "ln1_b", "ffn_conv_b", "ln2_g", "ln2_b",
                   "conv_a_w", "ffn_conv_w"]
    gs_list = [jnp.stack(dmods)] + [jnp.stack([gsm[l][n] for l in range(L)]) for n in small_names]
    gspack, gssegs = _pack(gs_list, D, F32)
    gs_all = _exchange(gspack, all_to_all=False, name="gather_small_grads")
    start_grads(after=gs_all)
    dmod_all = jnp.moveaxis(_unpack(gs_all, gssegs)[0], 0, 1).reshape(L, N_DEV * Bl, 6 * D)
    g_small = dict(zip(small_names, _unpack(_slot_sum(gs_all, name="sum_small_grads"), gssegs)[1:]))
    g_small["conv_a_w"] = my_cols(g_small["conv_a_w"], C // N_DEV)
    g_small["ffn_conv_w"] = my_cols(g_small["ffn_conv_w"], 2 * F // N_DEV)
    g_small["w_ada"] = _ada_bwd(c_all, my_cols(dmod_all, n_ada), name="ada_bwd")
    g_small["b_ada"] = jnp.stack([_colsum(dmod_all[l], name=f"db_ada_{l}")[0] for l in range(L)])

    given = dict(w_in=(w_in, m_w_in, v_w_in), w_conv_proj=(w_conv_proj, m_w_conv_proj, v_w_conv_proj),
                 w_attn_proj=(w_attn_proj, m_w_attn_proj, v_w_attn_proj), w_mix_out=(w_mix_out, m_w_mix_out, v_w_mix_out),
                 w_ffn_up=(w_ffn_up, m_w_ffn_up, v_w_ffn_up), w_ffn_down=(w_ffn_down, m_w_ffn_down, v_w_ffn_down),
                 w_ada=(w_ada, m_w_ada, v_w_ada), b_ada=(b_ada, m_b_ada, v_b_ada), b_in=(b_in, m_b_in, v_b_in),
                 conv_a_w=(conv_a_w, m_conv_a_w, v_conv_a_w), conv_a_b=(conv_a_b, m_conv_a_b, v_conv_a_b),
                 ln_conv_g=(ln_conv_g, m_ln_conv_g, v_ln_conv_g), ln_conv_b=(ln_conv_b, m_ln_conv_b, v_ln_conv_b),
                 b_mix_out=(b_mix_out, m_b_mix_out, v_b_mix_out), ln1_g=(ln1_g, m_ln1_g, v_ln1_g), ln1_b=(ln1_b, m_ln1_b, v_ln1_b),
                 ffn_conv_w=(ffn_conv_w, m_ffn_conv_w, v_ffn_conv_w), ffn_conv_b=(ffn_conv_b, m_ffn_conv_b, v_ffn_conv_b),
                 ln2_g=(ln2_g, m_ln2_g, v_ln2_g), ln2_b=(ln2_b, m_ln2_b, v_ln2_b))
    res, kinds = {}, ("grad", "delta", "new_m", "new_v")
    loc_names = ["w_ada", "b_ada"] + small_names
    packs = [_pack([g_small[n] for n in loc_names], D, F32)] + [_pack([given[n][i] for n in loc_names], D, F32) for i in range(3)]
    outs = _adamw(packs[0][0][None], packs[1][0], packs[2][0], packs[3][0], name="adamw_small")
    for kind, packed in zip(kinds, outs):
        for n, a in zip(loc_names, _unpack(packed, packs[0][1])):
            res[kind, n] = a

    big_parts = {}
    after = outs[0]
    for gi, (state, send, segs, grp) in enumerate(g_state):
        landed = _exchange_wait(state, after, name=f"exchange_grads_wait_{gi}")
        wmv = [_pack(shard_items([given[n][j] for n in big_names], grp), D, F32)[0] for j in range(3)]
        outs = _adamw(landed, *wmv, name=f"adamw_big_{gi}")
        for kind, packed in zip(kinds, outs):
            for (l, i), a in zip(grp, _unpack(packed, segs)):
                big_parts[kind, l, i] = a.T if transposed[i] else a
        after = outs[0]
    for kind in kinds:
        for i, n in enumerate(big_names):
            res[kind, n] = jnp.stack([big_parts[kind, l, i] for l in range(L)])

    order = ["w_ada", "b_ada", "w_in", "b_in", "conv_a_w", "conv_a_b", "ln_conv_g", "ln_conv_b", "w_conv_proj", "w_attn_proj",
             "w_mix_out", "b_mix_out", "ln1_g", "ln1_b", "w_ffn_up", "ffn_conv_w", "ffn_conv_b", "w_ffn_down", "ln2_g", "ln2_b"]
    return (loss, grad_x, *[res[k, n] for k in ("grad", "delta", "new_m", "new_v") for n in order])
```

```python
import functools
import math
from typing import NamedTuple

import jax
import jax.numpy as jnp
from jax import lax
from jax.experimental import pallas as pl
from jax.experimental.pallas import tpu as pltpu

F32, BF16 = jnp.float32, jnp.bfloat16
LN_EPS = 1e-5
ADAM_LR, ADAM_B1, ADAM_B2, ADAM_EPS, ADAM_WD, ADAM_STEP = 0.001, 0.9, 0.999, 1e-08, 0.01, 10
N_DEV = 8
LANES = 128
VMEM_LIMIT = 56 * 1024 * 1024
NEG = -1e30
NT = (((1,), (1,)), ((), ()))
TN = (((0,), (0,)), ((), ()))


class Cfg(NamedTuple):
    L: int
    Bl: int
    S: int
    D: int
    C: int
    KW: int
    H: int
    Dh: int
    F: int
    KF: int

    @property
    def T(self): return self.Bl * self.S
    @property
    def AW(self): return self.H * self.Dh
    @property
    def NM(self): return 2 * self.C + 3 * self.AW + 2 * self.D
    @property
    def q_off(self): return 2 * self.C
    @property
    def g_off(self): return 2 * self.C + 3 * self.AW
    @property
    def alpha(self): return (2.0 * self.L) ** 0.25


def _pcall(body, **kw):
    return pl.pallas_call(body, **kw)


def _params(*sem):
    return pltpu.CompilerParams(dimension_semantics=sem, vmem_limit_bytes=VMEM_LIMIT)


def _pick(n, prefs):
    for p in prefs:
        if n % p == 0:
            return p
    return n


def _sigmoid(x):
    return 1.0 / (1.0 + jnp.exp(-x))


def _ln_stats(x):
    mu = jnp.mean(x, axis=-1, keepdims=True)
    xc = x - mu
    var = jnp.mean(xc * xc, axis=-1, keepdims=True)
    rstd = lax.rsqrt(var + LN_EPS)
    return xc * rstd, rstd


def _ln_bwd(dxh, xh, rstd):
    return rstd * (dxh - jnp.mean(dxh, axis=-1, keepdims=True) - xh * jnp.mean(dxh * xh, axis=-1, keepdims=True))


def _matmul(a, b, *, mode, name, bias=None, add=None, out_dtype=F32, tm=None, tn=None, tk=None, after=None):
    if mode == "tn":
        K, M = a.shape
    else:
        M, K = a.shape
    N = b.shape[0] if mode == "nt" else b.shape[1]
    lane_tiles = (1536, 1408, 1024, 768, 512, 256, 128)
    tm = tm or _pick(M, lane_tiles if mode == "tn" else (1024, 512, 256, 128, 64, 32, 16, 8))
    tn = tn or _pick(N, lane_tiles)
    tk = tk or _pick(K, (1024, 512, 256, 128) if mode == "tn" else lane_tiles)
    nk = K // tk
    dn = {"nn": (((1,), (0,)), ((), ())), "nt": NT, "tn": TN}[mode]
    has_bias, has_add, has_after = bias is not None, add is not None, after is not None

    def body(*refs):
        a_ref, b_ref = refs[0], refs[1]
        pos = 2
        bias_ref = refs[pos] if has_bias else None
        pos += has_bias
        add_ref = refs[pos] if has_add else None
        pos += has_add + has_after
        o_ref = refs[pos]
        part = lax.dot_general(a_ref[...], b_ref[...], dn, preferred_element_type=F32)

        def finish(acc):
            if has_bias:
                acc = acc + bias_ref[...]
            if has_add:
                acc = acc + add_ref[...]
            o_ref[...] = acc.astype(out_dtype)

        if nk == 1:
            finish(part)
        else:
            acc_ref = refs[pos + 1]
            k = pl.program_id(2)

            @pl.when(k == 0)
            def _():
                acc_ref[...] = part

            @pl.when(k > 0)
            def _():
                acc_ref[...] += part

            @pl.when(k == nk - 1)
            def _():
                finish(acc_ref[...])

    a_spec = pl.BlockSpec((tk, tm), lambda i, j, k: (k, i)) if mode == "tn" else pl.BlockSpec((tm, tk), lambda i, j, k: (i, k))
    b_spec = pl.BlockSpec((tn, tk), lambda i, j, k: (j, k)) if mode == "nt" else pl.BlockSpec((tk, tn), lambda i, j, k: (k, j))
    in_specs, args = [a_spec, b_spec], [a, b]
    if has_bias:
        in_specs.append(pl.BlockSpec((1, tn), lambda i, j, k: (0, j)))
        args.append(bias)
    if has_add:
        in_specs.append(pl.BlockSpec((tm, tn), lambda i, j, k: (i, j)))
        args.append(add)
    if has_after:
        in_specs.append(pl.BlockSpec(memory_space=pl.ANY))
        args.append(after)
    return _pcall(
        body, name=name, grid=(M // tm, N // tn, nk), in_specs=in_specs,
        out_specs=pl.BlockSpec((tm, tn), lambda i, j, k: (i, j)),
        out_shape=jax.ShapeDtypeStruct((M, N), out_dtype),
        scratch_shapes=[pltpu.VMEM((tm, tn), F32)] if nk > 1 else [],
        compiler_params=_params("parallel", "parallel", "arbitrary"),
    )(*args)


def _colsum(x, *, name):
    T, N = x.shape
    tr = _pick(T, (512, 256, 128, 64, 32, 16))
    tc = _pick(N, (1536, 1024, 512, 256, 128))

    def body(x_ref, o_ref):
        @pl.when(pl.program_id(1) == 0)
        def _():
            o_ref[...] = jnp.zeros_like(o_ref)

        o_ref[...] += jnp.sum(x_ref[...].astype(F32), axis=0, keepdims=True)

    return _pcall(body, name=name, grid=(N // tc, T // tr), in_specs=[pl.BlockSpec((tr, tc), lambda j, i: (i, j))],
                  out_specs=pl.BlockSpec((1, tc), lambda j, i: (0, j)), out_shape=jax.ShapeDtypeStruct((1, N), F32),
                  compiler_params=_params("parallel", "arbitrary"))(x)


def _row_tile(cfg):
    return _pick(cfg.S, (256, 128, 64, 32, 16, 8))


def _ln_mod_fwd(x, shift, scale, cfg, *, name):
    tr = _row_tile(cfg)
    tpb = cfg.S // tr

    def body(x_ref, sh_ref, sc_ref, u_ref):
        xh, _ = _ln_stats(x_ref[...])
        u_ref[...] = (xh * (1.0 + sc_ref[0]) + sh_ref[0]).astype(BF16)

    row = pl.BlockSpec((tr, cfg.D), lambda i: (i, 0))
    per_b = pl.BlockSpec((1, 1, cfg.D), lambda i: (i // tpb, 0, 0))
    return _pcall(body, name=name, grid=(cfg.T // tr,), in_specs=[row, per_b, per_b], out_specs=row,
                  out_shape=jax.ShapeDtypeStruct((cfg.T, cfg.D), BF16), compiler_params=_params("parallel"))(x, shift, scale)


def _res_ln_fwd(xin, br, gate, g, b, cfg, *, name, nxt=None):
    tr = _row_tile(cfg)
    tpb = cfg.S // tr
    alpha = cfg.alpha

    def body(*refs):
        x_ref, br_ref, gt_ref, g_ref, b_ref = refs[:5]
        r = alpha * x_ref[...] + (1.0 + gt_ref[0]) * br_ref[...]
        xh, _ = _ln_stats(r)
        xo = xh * g_ref[...] + b_ref[...]
        if nxt is None:
            refs[5][...] = xo
        else:
            sh_ref, sc_ref, xo_ref, u_ref = refs[5:]
            xo_ref[...] = xo
            uh, _ = _ln_stats(xo)
            u_ref[...] = (uh * (1.0 + sc_ref[0]) + sh_ref[0]).astype(BF16)

    row = pl.BlockSpec((tr, cfg.D), lambda i: (i, 0))
    per_b = pl.BlockSpec((1, 1, cfg.D), lambda i: (i // tpb, 0, 0))
    vec = pl.BlockSpec((1, cfg.D), lambda i: (0, 0))
    in_specs, args = [row, row, per_b, vec, vec], [xin, br, gate, g, b]
    out_specs, out_shape = row, jax.ShapeDtypeStruct((cfg.T, cfg.D), F32)
    if nxt is not None:
        in_specs += [per_b, per_b]
        args += list(nxt)
        out_specs = [row, row]
        out_shape = [out_shape, jax.ShapeDtypeStruct((cfg.T, cfg.D), BF16)]
    return _pcall(body, name=name, grid=(cfg.T // tr,), in_specs=in_specs, out_specs=out_specs, out_shape=out_shape,
                  compiler_params=_params("parallel"))(*args)


def _loss_grad(y, tgt, cfg, *, name):
    tr = _row_tile(cfg)
    nt = cfg.T // tr
    inv_d = 1.0 / cfg.D

    def body(y_ref, t_ref, dy_ref, ls_ref):
        e = y_ref[...] - t_ref[...]
        dy_ref[...] = e * inv_d
        ls_ref[...] = jnp.full((1, 1, LANES), jnp.sum(e * e), F32)

    row = pl.BlockSpec((tr, cfg.D), lambda i: (i, 0))
    return _pcall(body, name=name, grid=(nt,), in_specs=[row, row],
                  out_specs=[row, pl.BlockSpec((1, 1, LANES), lambda i: (i, 0, 0))],
                  out_shape=[jax.ShapeDtypeStruct((cfg.T, cfg.D), F32), jax.ShapeDtypeStruct((nt, 1, LANES), F32)],
                  compiler_params=_params("parallel"))(y, tgt)


def _res_ln_bwd(dy, xin, br, gate, g, cfg, *, name):
    tr = _row_tile(cfg)
    tpb = cfg.S // tr
    alpha = cfg.alpha

    def body(dy_ref, x_ref, br_ref, gt_ref, g_ref, dx_ref, dbr_ref, dg_ref, db_ref, dgt_ref, dbs_ref):
        i = pl.program_id(0)

        @pl.when(i == 0)
        def _():
            dg_ref[...] = jnp.zeros_like(dg_ref)
            db_ref[...] = jnp.zeros_like(db_ref)
            dbs_ref[...] = jnp.zeros_like(dbs_ref)

        @pl.when(i % tpb == 0)
        def _():
            dgt_ref[...] = jnp.zeros_like(dgt_ref)

        dy, brv, one_gate = dy_ref[...], br_ref[...], 1.0 + gt_ref[0]
        xh, rstd = _ln_stats(alpha * x_ref[...] + one_gate * brv)
        dg_ref[...] += jnp.sum(dy * xh, axis=0, keepdims=True)
        db_ref[...] += jnp.sum(dy, axis=0, keepdims=True)
        dr = _ln_bwd(dy * g_ref[...], xh, rstd)
        dx_ref[...] = alpha * dr
        dbr = one_gate * dr
        dbr_ref[...] = dbr.astype(BF16)
        dbs_ref[...] += jnp.sum(dbr, axis=0, keepdims=True)
        dgt_ref[0] += jnp.sum(dr * brv, axis=0, keepdims=True)

    row = pl.BlockSpec((tr, cfg.D), lambda i: (i, 0))
    per_b = pl.BlockSpec((1, 1, cfg.D), lambda i: (i // tpb, 0, 0))
    vec = pl.BlockSpec((1, cfg.D), lambda i: (0, 0))
    vs = jax.ShapeDtypeStruct((1, cfg.D), F32)
    return _pcall(body, name=name, grid=(cfg.T // tr,), in_specs=[row, row, row, per_b, vec],
                  out_specs=[row, row, vec, vec, per_b, vec],
                  out_shape=[jax.ShapeDtypeStruct((cfg.T, cfg.D), F32), jax.ShapeDtypeStruct((cfg.T, cfg.D), BF16), vs, vs,
                             jax.ShapeDtypeStruct((cfg.Bl, 1, cfg.D), F32), vs],
                  compiler_params=_params("arbitrary"))(dy, xin, br, gate, g)


def _ln_mod_bwd(du, xin, scale, dres, cfg, *, name):
    tr = _row_tile(cfg)
    tpb = cfg.S // tr

    def body(du_ref, x_ref, sc_ref, dres_ref, dx_ref, dsc_ref, dsh_ref):
        @pl.when(pl.program_id(0) % tpb == 0)
        def _():
            dsc_ref[...] = jnp.zeros_like(dsc_ref)
            dsh_ref[...] = jnp.zeros_like(dsh_ref)

        du = du_ref[...]
        xh, rstd = _ln_stats(x_ref[...])
        dsc_ref[0] += jnp.sum(du * xh, axis=0, keepdims=True)
        dsh_ref[0] += jnp.sum(du, axis=0, keepdims=True)
        dx_ref[...] = _ln_bwd(du * (1.0 + sc_ref[0]), xh, rstd) + dres_ref[...]

    row = pl.BlockSpec((tr, cfg.D), lambda i: (i, 0))
    per_b = pl.BlockSpec((1, 1, cfg.D), lambda i: (i // tpb, 0, 0))
    bs = jax.ShapeDtypeStruct((cfg.Bl, 1, cfg.D), F32)
    return _pcall(body, name=name, grid=(cfg.T // tr,), in_specs=[row, row, per_b, row], out_specs=[row, per_b, per_b],
                  out_shape=[jax.ShapeDtypeStruct((cfg.T, cfg.D), F32), bs, bs],
                  compiler_params=_params("arbitrary"))(du, xin, scale, dres)


def _merge_tiles(cfg):
    tr = _pick(cfg.T, (512, 256, 128, 64, 32, 16))
    tc = _pick(math.gcd(cfg.g_off, cfg.D), (512, 256, 128))
    return tr, tc


def _merge_fwd(zm, ya, yb, cfg, *, name):
    tr, tc = _merge_tiles(cfg)
    ga0, gb0 = cfg.g_off // tc, (cfg.g_off + cfg.D) // tc

    def body(ga_ref, gb_ref, ya_ref, yb_ref, m_ref):
        m_ref[...] = (_sigmoid(ga_ref[...]) * ya_ref[...] + _sigmoid(gb_ref[...]) * yb_ref[...]).astype(BF16)

    blk = pl.BlockSpec((tr, tc), lambda i, j: (i, j))
    return _pcall(body, name=name, grid=(cfg.T // tr, cfg.D // tc),
                  in_specs=[pl.BlockSpec((tr, tc), lambda i, j: (i, ga0 + j)), pl.BlockSpec((tr, tc), lambda i, j: (i, gb0 + j)), blk, blk],
                  out_specs=blk, out_shape=jax.ShapeDtypeStruct((cfg.T, cfg.D), BF16),
                  compiler_params=_params("parallel", "parallel"))(zm, zm, ya, yb)


def _merge_bwd(dm, zm, ya, yb, cfg, *, name):
    tr, tc = _merge_tiles(cfg)
    ga0, gb0 = cfg.g_off // tc, (cfg.g_off + cfg.D) // tc

    def body(dm_ref, ga_ref, gb_ref, ya_ref, yb_ref, dya_ref, dyb_ref, dga_ref, dgb_ref):
        dm = dm_ref[...]
        ga, gb = _sigmoid(ga_ref[...]), _sigmoid(gb_ref[...])
        dya_ref[...] = (dm * ga).astype(BF16)
        dyb_ref[...] = (dm * gb).astype(BF16)
        dga_ref[...] = (dm * ya_ref[...] * ga * (1.0 - ga)).astype(BF16)
        dgb_ref[...] = (dm * yb_ref[...] * gb * (1.0 - gb)).astype(BF16)

    blk = pl.BlockSpec((tr, tc), lambda i, j: (i, j))
    o = jax.ShapeDtypeStruct((cfg.T, cfg.D), BF16)
    return _pcall(body, name=name, grid=(cfg.T // tr, cfg.D // tc),
                  in_specs=[blk, pl.BlockSpec((tr, tc), lambda i, j: (i, ga0 + j)), pl.BlockSpec((tr, tc), lambda i, j: (i, gb0 + j)), blk, blk],
                  out_specs=[blk] * 4, out_shape=[o] * 4, compiler_params=_params("parallel", "parallel"))(dm, zm, zm, ya, yb)


CONV_A_HALO = 32
CONV_A_CHUNK = 32


def _conv_a_tile(cfg):
    assert cfg.KW - 1 <= CONV_A_HALO
    return _pick(cfg.S, (256, 128, 64, 32))


def _conv_a_fwd(zm, w, cb, g, b, cfg, *, name):
    C, KW, HALO, CH = cfg.C, cfg.KW, CONV_A_HALO, CONV_A_CHUNK
    ts = _conv_a_tile(cfg)
    tpb = cfg.S // ts
    lead = HALO - (KW - 1)

    def body(z_ref, zp_ref, w_ref, cb_ref, g_ref, b_ref, o_ref, a0_s):
        first = pl.program_id(0) % tpb == 0
        prev = zp_ref[:, :C] * _sigmoid(zp_ref[:, C:])
        a0_s[0:HALO, :] = jnp.where(first, 0.0, prev)
        a0_s[HALO:HALO + ts, :] = z_ref[:, :C] * _sigmoid(z_ref[:, C:])
        for r0 in range(0, ts, CH):
            acc = jnp.zeros((CH, C), F32)
            for k in range(KW):
                acc = acc + w_ref[k:k + 1, :] * a0_s[r0 + lead + k:r0 + lead + k + CH, :]
            xh, _ = _ln_stats(acc + cb_ref[...])
            a2 = xh * g_ref[...] + b_ref[...]
            o_ref[r0:r0 + CH, :] = (a2 * _sigmoid(a2)).astype(BF16)

    hb = ts // HALO
    vec = pl.BlockSpec((1, C), lambda i: (0, 0))
    return _pcall(body, name=name, grid=(cfg.T // ts,),
                  in_specs=[pl.BlockSpec((ts, 2 * C), lambda i: (i, 0)),
                            pl.BlockSpec((HALO, 2 * C), lambda i: (jnp.maximum(i * hb - 1, 0), 0)),
                            pl.BlockSpec((32, C), lambda i: (0, 0)), vec, vec, vec],
                  out_specs=pl.BlockSpec((ts, C), lambda i: (i, 0)), out_shape=jax.ShapeDtypeStruct((cfg.T, C), BF16),
                  scratch_shapes=[pltpu.VMEM((HALO + ts, C), F32)], compiler_params=_params("parallel"))(zm, zm, w, cb, g, b)


def _conv_a_bwd(da3, zm, w, cb, g, b, cfg, *, name):
    C, KW, HALO, CH = cfg.C, cfg.KW, CONV_A_HALO, CONV_A_CHUNK
    ts = _conv_a_tile(cfg)
    tpb = cfg.S // ts
    nt = cfg.T // ts
    lead = HALO - (KW - 1)
    ext = ts + HALO

    def body(z_ref, zp_ref, zn_ref, d_ref, dn_ref, w_ref, cb_ref, g_ref, b_ref,
             dz_ref, dw_ref, dcb_ref, dg_ref, db_ref, a0_s, d3_s, da1_s):
        i = pl.program_id(0)
        first, last = i % tpb == 0, i % tpb == tpb - 1

        @pl.when(i == 0)
        def _():
            dw_ref[...] = jnp.zeros_like(dw_ref)
            dcb_ref[...] = jnp.zeros_like(dcb_ref)
            dg_ref[...] = jnp.zeros_like(dg_ref)
            db_ref[...] = jnp.zeros_like(db_ref)

        a0_s[0:HALO, :] = jnp.where(first, 0.0, zp_ref[:, :C] * _sigmoid(zp_ref[:, C:]))
        a0_s[HALO:HALO + ts, :] = z_ref[:, :C] * _sigmoid(z_ref[:, C:])
        a0_s[HALO + ts:HALO + ext, :] = zn_ref[:, :C] * _sigmoid(zn_ref[:, C:])
        d3_s[0:ts, :] = d_ref[...]
        d3_s[ts:ext, :] = jnp.where(last, 0.0, dn_ref[...])
        dcb, dg, db = jnp.zeros((1, C), F32), jnp.zeros((1, C), F32), jnp.zeros((1, C), F32)
        for r0 in range(0, ext, CH):
            acc = jnp.zeros((CH, C), F32)
            for k in range(KW):
                acc = acc + w_ref[k:k + 1, :] * a0_s[r0 + lead + k:r0 + lead + k + CH, :]
            xh, rstd = _ln_stats(acc + cb_ref[...])
            a2 = xh * g_ref[...] + b_ref[...]
            sg = _sigmoid(a2)
            da2 = d3_s[r0:r0 + CH, :] * (sg * (1.0 + a2 * (1.0 - sg)))
            da1 = _ln_bwd(da2 * g_ref[...], xh, rstd)
            da1_s[r0:r0 + CH, :] = da1
            if r0 < ts:
                dg = dg + jnp.sum(da2 * xh, axis=0, keepdims=True)
                db = db + jnp.sum(da2, axis=0, keepdims=True)
                dcb = dcb + jnp.sum(da1, axis=0, keepdims=True)
        dg_ref[...] += dg
        db_ref[...] += db
        dcb_ref[...] += dcb
        for k in range(KW):
            dwk = jnp.zeros((1, C), F32)
            for r0 in range(0, ts, CH):
                dwk = dwk + jnp.sum(da1_s[r0:r0 + CH, :] * a0_s[r0 + lead + k:r0 + lead + k + CH, :], axis=0, keepdims=True)
            dw_ref[k:k + 1, :] += dwk
        for r0 in range(0, ts, CH):
            da0 = jnp.zeros((CH, C), F32)
            for k in range(KW):
                da0 = da0 + w_ref[k:k + 1, :] * da1_s[r0 + KW - 1 - k:r0 + KW - 1 - k + CH, :]
            val, sg = z_ref[r0:r0 + CH, :C], _sigmoid(z_ref[r0:r0 + CH, C:])
            dz_ref[r0:r0 + CH, :C] = (da0 * sg).astype(BF16)
            dz_ref[r0:r0 + CH, C:] = (da0 * val * sg * (1.0 - sg)).astype(BF16)

    hb = ts // HALO
    nhb = cfg.T // HALO
    vec = pl.BlockSpec((1, C), lambda i: (0, 0))
    vs = jax.ShapeDtypeStruct((1, C), F32)
    return _pcall(body, name=name, grid=(nt,),
                  in_specs=[pl.BlockSpec((ts, 2 * C), lambda i: (i, 0)),
                            pl.BlockSpec((HALO, 2 * C), lambda i: (jnp.maximum(i * hb - 1, 0), 0)),
                            pl.BlockSpec((HALO, 2 * C), lambda i: (jnp.minimum((i + 1) * hb, nhb - 1), 0)),
                            pl.BlockSpec((ts, C), lambda i: (i, 0)),
                            pl.BlockSpec((HALO, C), lambda i: (jnp.minimum((i + 1) * hb, nhb - 1), 0)),
                            pl.BlockSpec((32, C), lambda i: (0, 0)), vec, vec, vec],
                  out_specs=[pl.BlockSpec((ts, 2 * C), lambda i: (i, 0)), pl.BlockSpec((32, C), lambda i: (0, 0)), vec, vec, vec],
                  out_shape=[jax.ShapeDtypeStruct((cfg.T, 2 * C), BF16), jax.ShapeDtypeStruct((32, C), F32), vs, vs, vs],
                  scratch_shapes=[pltpu.VMEM((HALO + ext, C), F32), pltpu.VMEM((ext, C), F32), pltpu.VMEM((ext, C), F32)],
                  compiler_params=_params("arbitrary"))(zm, zm, zm, da3, da3, w, cb, g, b)


def _cum_tile(cfg):
    return _pick(cfg.S, (256, 128, 64, 32, 16, 8))


def _fgate_fwd(zf, cfg, *, name):
    tc = _cum_tile(cfg)
    tpb = cfg.S // tc
    hp = _attn_tiles(cfg)[2]
    nb = cfg.H // hp

    def body(z_ref, o_ref, carry):
        @pl.when(pl.program_id(0) % tpb == 0)
        def _():
            carry[...] = jnp.zeros_like(carry)

        z = z_ref[...]
        logf = jnp.minimum(z, 0.0) - jnp.log(1.0 + jnp.exp(-jnp.abs(z)))
        tri = (lax.broadcasted_iota(jnp.int32, (tc, tc), 0) >= lax.broadcasted_iota(jnp.int32, (tc, tc), 1)).astype(F32)
        cum = jnp.dot(tri, logf, precision=lax.Precision.HIGHEST, preferred_element_type=F32) + carry[...]
        carry[...] = cum[tc - 1:tc, :]
        o_ref[0] = cum
        for b in range(1, nb):
            o_ref[b] = pltpu.roll(cum, LANES - hp * b, axis=1)

    return _pcall(body, name=name, grid=(cfg.T // tc,), in_specs=[pl.BlockSpec((tc, LANES), lambda i: (i, 0))],
                  out_specs=pl.BlockSpec((nb, tc, LANES), lambda i: (0, i, 0)),
                  out_shape=jax.ShapeDtypeStruct((nb, cfg.T, LANES), F32), scratch_shapes=[pltpu.VMEM((1, LANES), F32)],
                  compiler_params=_params("arbitrary"))(zf)


def _fgate_bwd(dcum_c, zf, cfg, *, name):
    tc = _cum_tile(cfg)
    tpb = cfg.S // tc
    nt = cfg.T // tc
    hp = _attn_tiles(cfg)[2]
    nb = cfg.H // hp

    def body(d_ref, z_ref, o_ref, carry):
        @pl.when(pl.program_id(0) % tpb == 0)
        def _():
            carry[...] = jnp.zeros_like(carry)

        d = d_ref[0]
        for b in range(1, nb):
            d = d + pltpu.roll(d_ref[b], hp * b, axis=1)
        tri = (lax.broadcasted_iota(jnp.int32, (tc, tc), 0) <= lax.broadcasted_iota(jnp.int32, (tc, tc), 1)).astype(F32)
        suf = jnp.dot(tri, d, precision=lax.Precision.HIGHEST, preferred_element_type=F32) + carry[...]
        o_ref[...] = (suf * _sigmoid(-z_ref[...])).astype(BF16)
        carry[...] = suf[0:1, :]

    blk = pl.BlockSpec((tc, LANES), lambda i: (nt - 1 - i, 0))
    return _pcall(body, name=name, grid=(nt,), in_specs=[pl.BlockSpec((nb, tc, LANES), lambda i: (0, nt - 1 - i, 0)), blk],
                  out_specs=blk, out_shape=jax.ShapeDtypeStruct((cfg.T, LANES), BF16),
                  scratch_shapes=[pltpu.VMEM((1, LANES), F32)], compiler_params=_params("arbitrary"))(dcum_c, zf)


def _attn_tiles(cfg):
    assert LANES % cfg.Dh == 0 and cfg.H % (LANES // cfg.Dh) == 0
    tk = _pick(cfg.S, (256, 128))
    tq = _pick(cfg.S, (2 * tk, tk))
    return tq, tk, LANES // cfg.Dh


BIAS_LANES = 3


def _head_lanes(hd, cfg, hp):
    li = lax.broadcasted_iota(jnp.int32, (1, LANES), 1)
    own = (li >= hd * cfg.Dh) & (li < (hd + 1) * cfg.Dh)
    return own, li, ((hd + 1) % hp) * cfg.Dh


def _q_aug(q, hd, cfg, hp):
    own, li, b0 = _head_lanes(hd, cfg, hp)
    ones = ((li >= b0) & (li < b0 + BIAS_LANES)).astype(F32)
    return jnp.where(own, q * cfg.Dh ** -0.5, ones).astype(BF16)


def _k_aug(k, ck, hd, cfg, hp):
    own, li, b0 = _head_lanes(hd, cfg, hp)
    hi = ck.astype(BF16).astype(F32)
    mid = (ck - hi).astype(BF16).astype(F32)
    lo = ck - hi - mid
    bias = jnp.where(li == b0, -hi, jnp.where(li == b0 + 1, -mid, jnp.where(li == b0 + 2, -lo, 0.0)))
    return jnp.where(own, k, bias).astype(BF16)


def _attn_fwd(zm, cum_c, cfg, *, name):
    S, Dh = cfg.S, cfg.Dh
    tq, tk, hp = _attn_tiles(cfg)
    assert hp >= 2
    nq, nb, per = S // tq, cfg.H // hp, tq // tk
    qb, kb, vb = cfg.q_off // LANES, (cfg.q_off + cfg.AW) // LANES, (cfg.q_off + 2 * cfg.AW) // LANES

    def body(q_ref, k_ref, v_ref, cc_ref, o_ref, o32_ref, lse_ref, ka_s, vt_s):
        qi = pl.program_id(2)

        @pl.when(qi == 0)
        def _():
            def prep(c, _):
                r = pl.multiple_of(c * tk, tk)
                kc = k_ref[pl.ds(r, tk), :]
                for hd in range(hp):
                    ka_s[hd, pl.ds(r, tk), :] = _k_aug(kc, cc_ref[0, pl.ds(r, tk), hd:hd + 1], hd, cfg, hp)
                vt_s[:, pl.ds(r, tk)] = v_ref[pl.ds(r, tk), :].T.astype(BF16)
                return 0

            lax.fori_loop(0, S // tk, prep, 0)

        key_i = lax.broadcasted_iota(jnp.int32, (tk, tq), 0)
        qry_i = lax.broadcasted_iota(jnp.int32, (tk, tq), 1)
        qf = q_ref[...]
        qa = [_q_aug(qf, hd, cfg, hp) for hd in range(hp)]

        def chunk(j, carry, diag=None):
            r = pl.multiple_of(j * tk, tk)
            new = []
            for hd in range(hp):
                m, l, acc = carry[hd]
                s = lax.dot_general(ka_s[hd, pl.ds(r, tk), :], qa[hd], NT, preferred_element_type=F32)
                if diag is not None:
                    s = jnp.where(key_i + diag * tk <= qry_i, s, NEG)
                m_new = jnp.maximum(m, jnp.max(s, axis=0, keepdims=True))
                a = jnp.exp(m - m_new)
                p = jnp.exp(s - m_new)
                l = a * l + jnp.sum(p, axis=0, keepdims=True)
                p_hi = p.astype(BF16)
                p_lo = (p - p_hi.astype(F32)).astype(BF16)
                vt = vt_s[hd * Dh:(hd + 1) * Dh, pl.ds(r, tk)]
                acc = a * acc + (jnp.dot(vt, p_hi, preferred_element_type=F32) + jnp.dot(vt, p_lo, preferred_element_type=F32))
                new.append((m_new, l, acc))
            return tuple(new)

        init = tuple((jnp.full((1, tq), NEG, F32), jnp.zeros((1, tq), F32), jnp.zeros((Dh, tq), F32)) for _ in range(hp))
        res = lax.fori_loop(0, qi * per, chunk, init)
        for d in range(per):
            res = chunk(qi * per + d, res, diag=d)
        o = jnp.concatenate([acc / l for _, l, acc in res], axis=0).T
        o_ref[...] = o.astype(BF16)
        o32_ref[...] = o
        lse_ref[...] = jnp.zeros_like(lse_ref)
        for hd in range(hp):
            lse_ref[0, 0, hd:hd + 1, :] = res[hd][0] + jnp.log(res[hd][1])

    return _pcall(body, name=name, grid=(cfg.Bl, nb, nq),
                  in_specs=[pl.BlockSpec((tq, LANES), lambda b, h, i: (b * nq + i, qb + h)),
                            pl.BlockSpec((S, LANES), lambda b, h, i: (b, kb + h)),
                            pl.BlockSpec((S, LANES), lambda b, h, i: (b, vb + h)),
                            pl.BlockSpec((1, S, LANES), lambda b, h, i: (h, b, 0))],
                  out_specs=[pl.BlockSpec((tq, LANES), lambda b, h, i: (b * nq + i, h)),
                             pl.BlockSpec((tq, LANES), lambda b, h, i: (b * nq + i, h)),
                             pl.BlockSpec((1, 1, 8, tq), lambda b, h, i: (b, h, 0, i))],
                  out_shape=[jax.ShapeDtypeStruct((cfg.T, cfg.AW), BF16), jax.ShapeDtypeStruct((cfg.T, cfg.AW), F32),
                             jax.ShapeDtypeStruct((cfg.Bl, nb, 8, S), F32)],
                  scratch_shapes=[pltpu.VMEM((hp, S, LANES), BF16), pltpu.VMEM((LANES, S), BF16)],
                  compiler_params=_params("parallel", "parallel", "arbitrary"))(zm, zm, zm, cum_c)


def _attn_bwd(zm, cum_c, o, do, lse, cfg, *, name):
    S, Dh = cfg.S, cfg.Dh
    tq, t, hp = _attn_tiles(cfg)
    nq, nk, nb, per = S // tq, S // t, cfg.H // hp, tq // t
    qb, kb, vb = cfg.q_off // LANES, (cfg.q_off + cfg.AW) // LANES, (cfg.q_off + 2 * cfg.AW) // LANES
    scale = Dh ** -0.5

    def body(q_ref, k_ref, v_ref, cc_ref, o_ref, do_ref, lse_ref, dq_ref, dk_ref, dv_ref, dcc_ref,
             ka_s, qa_s, vz_s, kt_s, dd_s, dqt_s):
        li = lax.broadcasted_iota(jnp.int32, (1, LANES), 1)
        ri = lax.broadcasted_iota(jnp.int32, (LANES, 1), 0)
        key_i = lax.broadcasted_iota(jnp.int32, (t, tq), 0)
        qry_i = lax.broadcasted_iota(jnp.int32, (t, tq), 1)

        def prep(c, _):
            r = pl.multiple_of(c * t, t)
            kc, vc, qc = k_ref[pl.ds(r, t), :], v_ref[pl.ds(r, t), :], q_ref[pl.ds(r, t), :]
            prod_t = (do_ref[pl.ds(r, t), :].astype(F32) * o_ref[pl.ds(r, t), :].astype(F32)).T
            for hd in range(hp):
                own = _head_lanes(hd, cfg, hp)[0]
                ka_s[hd, pl.ds(r, t), :] = _k_aug(kc, cc_ref[0, pl.ds(r, t), hd:hd + 1], hd, cfg, hp)
                qa_s[hd, pl.ds(r, t), :] = _q_aug(qc, hd, cfg, hp)
                vz_s[hd, pl.ds(r, t), :] = jnp.where(own, vc, 0.0).astype(BF16)
                dd_s[hd:hd + 1, pl.ds(r, t)] = jnp.sum(prod_t[hd * Dh:(hd + 1) * Dh, :], axis=0, keepdims=True)
            kt_s[:, pl.ds(r, t)] = kc.T.astype(BF16)
            dqt_s[:, pl.ds(r, t)] = jnp.zeros((LANES, t), F32)
            return 0

        lax.fori_loop(0, nk, prep, 0)

        def kv_step(j, _):
            rk = pl.multiple_of(j * t, t)
            i0 = j // per

            def tile(i, carry, masked):
                rq = pl.multiple_of(i * tq, tq)
                dob = do_ref[pl.ds(rq, tq), :]
                new, dq_t = [], None
                for hd in range(hp):
                    dk_h, dv_h, dsum_h = carry[hd]
                    qa = qa_s[hd, pl.ds(rq, tq), :]
                    s = lax.dot_general(ka_s[hd, pl.ds(rk, t), :], qa, NT, preferred_element_type=F32)
                    p = jnp.exp(s - lse_ref[0, 0, hd:hd + 1, pl.ds(rq, tq)])
                    if masked:
                        p = jnp.where(key_i + (rk - rq) <= qry_i, p, 0.0)
                    dp = lax.dot_general(vz_s[hd, pl.ds(rk, t), :], dob, NT, preferred_element_type=F32)
                    ds = p * (dp - dd_s[hd:hd + 1, pl.ds(rq, tq)])
                    dsb = ds.astype(BF16)
                    dv_h = dv_h + jnp.dot(p.astype(BF16), dob, preferred_element_type=F32)
                    dk_h = dk_h + jnp.dot(dsb, qa, preferred_element_type=F32)
                    dq_h = jnp.dot(kt_s[:, pl.ds(rk, t)], dsb, preferred_element_type=F32)
                    dq_t = dq_h if hd == 0 else jnp.where((ri >= hd * Dh) & (ri < (hd + 1) * Dh), dq_h, dq_t)
                    for c0 in range(0, tq, LANES):
                        dsum_h = dsum_h + ds[:, c0:c0 + LANES]
                    new.append((dk_h, dv_h, dsum_h))
                dqt_s[:, pl.ds(rq, tq)] += dq_t * scale
                return tuple(new)

            zero = tuple((jnp.zeros((t, LANES), F32),) * 3 for _ in range(hp))
            res = lax.fori_loop(i0 + 1, nq, functools.partial(tile, masked=False), tile(i0, zero, True))
            dk, dv, dcc = res[0][0], res[0][1], jnp.zeros((t, LANES), F32)
            for hd in range(hp):
                own = _head_lanes(hd, cfg, hp)[0]
                if hd > 0:
                    dk, dv = jnp.where(own, res[hd][0], dk), jnp.where(own, res[hd][1], dv)
                dcc = dcc + jnp.where(li == hd, -jnp.sum(res[hd][2], axis=1, keepdims=True), 0.0)
            dk_ref[pl.ds(rk, t), :] = dk.astype(BF16)
            dv_ref[pl.ds(rk, t), :] = dv.astype(BF16)
            dcc_ref[0, pl.ds(rk, t), :] = dcc
            return 0

        lax.fori_loop(0, nk, kv_step, 0)

        def finish(c, _):
            r = pl.multiple_of(c * t, t)
            dq_ref[pl.ds(r, t), :] = dqt_s[:, pl.ds(r, t)].T.astype(BF16)
            return 0

        lax.fori_loop(0, nk, finish, 0)

    blk = pl.BlockSpec((S, LANES), lambda b, h: (b, h))
    cc = pl.BlockSpec((1, S, LANES), lambda b, h: (h, b, 0))
    os_ = jax.ShapeDtypeStruct((cfg.T, cfg.AW), BF16)
    return _pcall(body, name=name, grid=(cfg.Bl, nb),
                  in_specs=[pl.BlockSpec((S, LANES), lambda b, h: (b, qb + h)), pl.BlockSpec((S, LANES), lambda b, h: (b, kb + h)),
                            pl.BlockSpec((S, LANES), lambda b, h: (b, vb + h)), cc, blk, blk,
                            pl.BlockSpec((1, 1, 8, S), lambda b, h: (b, h, 0, 0))],
                  out_specs=[blk, blk, blk, cc],
                  out_shape=[os_, os_, os_, jax.ShapeDtypeStruct((nb, cfg.T, LANES), F32)],
                  scratch_shapes=[pltpu.VMEM((hp, S, LANES), BF16)] * 3 + [pltpu.VMEM((LANES, S), BF16),
                                  pltpu.VMEM((8, S), F32), pltpu.VMEM((LANES, S), F32)],
                  compiler_params=_params("parallel", "parallel"))(zm, zm, zm, cum_c, o, do, lse)


FFN_HALO = 8
FFN_CHUNK = 16


def _ffn_tiles(cfg):
    assert cfg.KF - 1 <= FFN_HALO
    return _pick(cfg.S, (512, 256, 128, 64, 32, 16, 8)), _pick(cfg.F, (256, 128))


def _gelu(x):
    return 0.5 * x * (1.0 + lax.erf(x * (2.0 ** -0.5)))


def _gelu_grad(x):
    return 0.5 * (1.0 + lax.erf(x * (2.0 ** -0.5))) + x * jnp.exp(-0.5 * x * x) * ((2.0 * math.pi) ** -0.5)


def _ffn_conv_fwd(h0, w, cb, cfg, *, name):
    KF, HALO = cfg.KF, FFN_HALO
    ts, tf = _ffn_tiles(cfg)
    tpb, nf = cfg.S // ts, cfg.F // tf
    lead = HALO - (KF - 1)

    CH = FFN_CHUNK

    def body(g_ref, gp_ref, l_ref, lp_ref, wg_ref, wl_ref, cg_ref, cl_ref, o_ref, g_s, l_s):
        first = pl.program_id(1) % tpb == 0
        for s, main, prev in ((g_s, g_ref, gp_ref), (l_s, l_ref, lp_ref)):
            s[0:HALO, :] = jnp.where(first, 0.0, prev[...])
            s[HALO:HALO + CH, :] = main[0:CH, :]
        wg, wl = [wg_ref[k:k + 1, :] for k in range(KF)], [wl_ref[k:k + 1, :] for k in range(KF)]
        for r0 in range(0, ts, CH):
            hg, hl = cg_ref[...], cl_ref[...]
            for k in range(KF):
                if r0 == 0:
                    xg, xl = g_s[lead + k:lead + k + CH, :], l_s[lead + k:lead + k + CH, :]
                else:
                    a = r0 - (KF - 1) + k
                    xg, xl = g_ref[a:a + CH, :], l_ref[a:a + CH, :]
                hg, hl = hg + wg[k] * xg, hl + wl[k] * xl
            o_ref[r0:r0 + CH, :] = (_gelu(hg) * hl).astype(BF16)

    hb = ts // HALO
    prev = lambda off: pl.BlockSpec((HALO, tf), lambda j, i: (jnp.maximum(i * hb - 1, 0), off + j))
    main = lambda off: pl.BlockSpec((ts, tf), lambda j, i: (i, off + j))
    wsp = lambda off: pl.BlockSpec((8, tf), lambda j, i: (0, off + j))
    vsp = lambda off: pl.BlockSpec((1, tf), lambda j, i: (0, off + j))
    return _pcall(body, name=name, grid=(nf, cfg.T // ts),
                  in_specs=[main(0), prev(0), main(nf), prev(nf), wsp(0), wsp(nf), vsp(0), vsp(nf)],
                  out_specs=pl.BlockSpec((ts, tf), lambda j, i: (i, j)), out_shape=jax.ShapeDtypeStruct((cfg.T, cfg.F), BF16),
                  scratch_shapes=[pltpu.VMEM((HALO + CH, tf), F32)] * 2,
                  compiler_params=_params("parallel", "parallel"))(h0, h0, h0, h0, w, w, cb, cb)


def _ffn_conv_bwd(df, h0, w, cb, cfg, *, name):
    KF, HALO = cfg.KF, FFN_HALO
    ts, tf = _ffn_tiles(cfg)
    tpb, nf = cfg.S // ts, cfg.F // tf
    lead = HALO - (KF - 1)
    ext = ts + HALO

    CH = FFN_CHUNK

    def body(g_ref, gp_ref, gn_ref, l_ref, lp_ref, ln_ref, d_ref, dn_ref, wg_ref, wl_ref, cg_ref, cl_ref,
             dg_ref, dl_ref, dwg_ref, dwl_ref, dcg_ref, dcl_ref, gh_s, lh_s, gt_s, lt_s, dhg_s, dhl_s):
        i = pl.program_id(1)
        first, last = i % tpb == 0, i % tpb == tpb - 1

        @pl.when(i == 0)
        def _():
            dwg_ref[...] = jnp.zeros_like(dwg_ref)
            dwl_ref[...] = jnp.zeros_like(dwl_ref)
            dcg_ref[...] = jnp.zeros_like(dcg_ref)
            dcl_ref[...] = jnp.zeros_like(dcl_ref)

        for head, tail, main, prev, nxt in ((gh_s, gt_s, g_ref, gp_ref, gn_ref), (lh_s, lt_s, l_ref, lp_ref, ln_ref)):
            head[0:HALO, :] = jnp.where(first, 0.0, prev[...])
            head[HALO:HALO + CH, :] = main[0:CH, :]
            tail[0:HALO, :] = main[ts - HALO:ts, :]
            tail[HALO:2 * HALO, :] = nxt[...]
        wg, wl = [wg_ref[k:k + 1, :] for k in range(KF)], [wl_ref[k:k + 1, :] for k in range(KF)]

        def grads(hg, hl, d):
            return d * hl * _gelu_grad(hg), d * _gelu(hg)

        for r0 in range(0, ts, CH):
            hg, hl = cg_ref[...], cl_ref[...]
            for k in range(KF):
                if r0 == 0:
                    xg, xl = gh_s[lead + k:lead + k + CH, :], lh_s[lead + k:lead + k + CH, :]
                else:
                    a = r0 - (KF - 1) + k
                    xg, xl = g_ref[a:a + CH, :], l_ref[a:a + CH, :]
                hg, hl = hg + wg[k] * xg, hl + wl[k] * xl
            dhg_s[r0:r0 + CH, :], dhl_s[r0:r0 + CH, :] = grads(hg, hl, d_ref[r0:r0 + CH, :])
        hg, hl = cg_ref[...], cl_ref[...]
        for k in range(KF):
            hg, hl = hg + wg[k] * gt_s[lead + k:lead + k + HALO, :], hl + wl[k] * lt_s[lead + k:lead + k + HALO, :]
        dhg_s[ts:ext, :], dhl_s[ts:ext, :] = grads(hg, hl, jnp.where(last, 0.0, dn_ref[...]))

        for dh_s, x_ref, wk, dx_ref, dw_ref, dc_ref in ((dhg_s, g_ref, wg, dg_ref, dwg_ref, dcg_ref),
                                                        (dhl_s, l_ref, wl, dl_ref, dwl_ref, dcl_ref)):
            dw_acc = [jnp.zeros((CH, tf), F32) for _ in range(KF)]
            for r0 in range(0, ts, CH):
                x = x_ref[r0:r0 + CH, :]
                dx = jnp.zeros((CH, tf), F32)
                for k in range(KF):
                    dhk = dh_s[r0 + KF - 1 - k:r0 + KF - 1 - k + CH, :]
                    dx = dx + wk[k] * dhk
                    dw_acc[k] = dw_acc[k] + x * dhk
                    if k == KF - 1:
                        dc_acc = dhk if r0 == 0 else dc_acc + dhk
                dx_ref[r0:r0 + CH, :] = dx.astype(BF16)
            for k in range(KF):
                dw_ref[k:k + 1, :] += jnp.sum(dw_acc[k], axis=0, keepdims=True)
            dc_ref[...] += jnp.sum(dc_acc, axis=0, keepdims=True)

    hb = ts // HALO
    nhb = cfg.T // HALO
    main = lambda off: pl.BlockSpec((ts, tf), lambda j, i: (i, off + j))
    prev = lambda off: pl.BlockSpec((HALO, tf), lambda j, i: (jnp.maximum(i * hb - 1, 0), off + j))
    nxt = lambda off: pl.BlockSpec((HALO, tf), lambda j, i: (jnp.minimum((i + 1) * hb, nhb - 1), off + j))
    wsp = lambda off: pl.BlockSpec((8, tf), lambda j, i: (0, off + j))
    vsp = lambda off: pl.BlockSpec((1, tf), lambda j, i: (0, off + j))
    dxs, dws, dcs = (jax.ShapeDtypeStruct((cfg.T, cfg.F), BF16), jax.ShapeDtypeStruct((8, cfg.F), F32),
                     jax.ShapeDtypeStruct((1, cfg.F), F32))
    return _pcall(body, name=name, grid=(nf, cfg.T // ts),
                  in_specs=[main(0), prev(0), nxt(0), main(nf), prev(nf), nxt(nf), main(0), nxt(0),
                            wsp(0), wsp(nf), vsp(0), vsp(nf)],
                  out_specs=[main(0), main(0), wsp(0), wsp(0), vsp(0), vsp(0)],
                  out_shape=[dxs, dxs, dws, dws, dcs, dcs],
                  scratch_shapes=[pltpu.VMEM((HALO + CH, tf), F32)] * 2 + [pltpu.VMEM((2 * HALO, tf), F32)] * 2
                  + [pltpu.VMEM((ext, tf), F32)] * 2,
                  compiler_params=_params("parallel", "arbitrary"))(h0, h0, h0, h0, h0, h0, df, df, w, w, cb, cb)


def _ada_fwd(c_all, w, b, *, name):
    L, D, n = w.shape
    B = c_all.shape[0]

    def body(c_ref, w_ref, b_ref, o_ref):
        c = c_ref[...]
        act = (c * _sigmoid(c)).astype(BF16)
        o_ref[0] = jnp.dot(act, w_ref[0].astype(BF16), preferred_element_type=F32) + b_ref[0]

    return _pcall(body, name=name, grid=(L,),
                  in_specs=[pl.BlockSpec((B, D), lambda l: (0, 0)), pl.BlockSpec((1, D, n), lambda l: (l, 0, 0)),
                            pl.BlockSpec((1, 1, n), lambda l: (l, 0, 0))],
                  out_specs=pl.BlockSpec((1, B, n), lambda l: (l, 0, 0)), out_shape=jax.ShapeDtypeStruct((L, B, n), F32),
                  compiler_params=_params("parallel"))(c_all, w, b)


def _ada_bwd(c_all, dmod, *, name):
    L, B, n = dmod.shape
    D = c_all.shape[1]

    def body(c_ref, d_ref, o_ref):
        c = c_ref[...]
        act = (c * _sigmoid(c)).astype(BF16)
        o_ref[0] = lax.dot_general(act, d_ref[0].astype(BF16), TN, preferred_element_type=F32)

    return _pcall(body, name=name, grid=(L,),
                  in_specs=[pl.BlockSpec((B, D), lambda l: (0, 0)), pl.BlockSpec((1, B, n), lambda l: (l, 0, 0))],
                  out_specs=pl.BlockSpec((1, D, n), lambda l: (l, 0, 0)), out_shape=jax.ShapeDtypeStruct((L, D, n), F32),
                  compiler_params=_params("parallel"))(c_all, dmod)


def _slot_sum(x, *, name):
    n, R, W = x.shape
    tr = _pick(R, (256, 128, 64, 32, 16, 8))

    def body(x_ref, o_ref):
        acc = x_ref[0].astype(F32)
        for k in range(1, n):
            acc = acc + x_ref[k].astype(F32)
        o_ref[...] = acc

    return _pcall(body, name=name, grid=(R // tr,), in_specs=[pl.BlockSpec((n, tr, W), lambda i: (0, i, 0))],
                  out_specs=pl.BlockSpec((tr, W), lambda i: (i, 0)), out_shape=jax.ShapeDtypeStruct((R, W), F32),
                  compiler_params=_params("parallel"))(x)


def _adamw(gs, w, m, v, *, name):
    n, R, W = gs.shape
    tr = _pick(R, (256, 128, 64, 32, 16, 8))
    c1, c2 = 1.0 - ADAM_B1 ** ADAM_STEP, 1.0 - ADAM_B2 ** ADAM_STEP

    def body(g_ref, w_ref, m_ref, v_ref, go_ref, d_ref, mo_ref, vo_ref):
        g = g_ref[0].astype(F32)
        for k in range(1, n):
            g = g + g_ref[k].astype(F32)
        m2 = ADAM_B1 * m_ref[...] + (1.0 - ADAM_B1) * g
        v2 = ADAM_B2 * v_ref[...] + (1.0 - ADAM_B2) * (g * g)
        go_ref[...] = g
        mo_ref[...] = m2
        vo_ref[...] = v2
        d_ref[...] = -ADAM_LR * ((m2 / c1) / (jnp.sqrt(v2 / c2) + ADAM_EPS) + ADAM_WD * w_ref[...])

    blk = pl.BlockSpec((tr, W), lambda i: (i, 0))
    o = jax.ShapeDtypeStruct((R, W), F32)
    return _pcall(body, name=name, grid=(R // tr,), in_specs=[pl.BlockSpec((n, tr, W), lambda i: (0, i, 0)), blk, blk, blk],
                  out_specs=[blk] * 4, out_shape=[o] * 4, compiler_params=_params("parallel"))(gs, w, m, v)


def _peer_copies(x_ref, land_ref, send_sems, recv_sems, all_to_all):
    mx, my, mc = lax.axis_index("x"), lax.axis_index("y"), lax.axis_index("c")
    me = 4 * mx + 2 * my + mc
    copies = []
    for k in range(1, N_DEV):
        px, py, pc = mx ^ ((k >> 2) & 1), my ^ ((k >> 1) & 1), mc ^ (k & 1)
        copies.append(pltpu.make_async_remote_copy(
            src_ref=x_ref.at[4 * px + 2 * py + pc] if all_to_all else x_ref, dst_ref=land_ref.at[me],
            send_sem=send_sems.at[k - 1], recv_sem=recv_sems.at[k - 1], device_id=(px, py, pc),
            device_id_type=pl.DeviceIdType.MESH))
    return copies


def _gather_two_level(x, *, name, after=None):
    def body(x_ref, *rest):
        o_ref, send_sems, recv_sems, local_sem = rest[-4:]
        mx, my, mc = lax.axis_index("x"), lax.axis_index("y"), lax.axis_index("c")
        me, sibling = (mx, my, mc), (mx, my, 1 - mc)
        chips = [(1 - mx, my), (mx, 1 - my), (1 - mx, 1 - my)]

        def slot(px, py, pc):
            return o_ref.at[4 * px + 2 * py + pc]

        def copy(k, block, to, src=None):
            return pltpu.make_async_remote_copy(
                src_ref=slot(*block) if src is None else src, dst_ref=slot(*block), send_sem=send_sems.at[k],
                recv_sem=recv_sems.at[k], device_id=to, device_id_type=pl.DeviceIdType.MESH)

        mine = pltpu.make_async_copy(x_ref, slot(*me), local_sem)
        mine.start()
        first = [copy(0, me, sibling, src=x_ref)] + [copy(1 + j, me, (*chip, mc), src=x_ref) for j, chip in enumerate(chips)]
        for cp in first:
            cp.start()
        passed = [copy(4 + j, (*chip, mc), sibling) for j, chip in enumerate(chips)]
        for j, chip in enumerate(chips):
            copy(1 + j, (*chip, mc), me).wait_recv()
            passed[j].start()
        copy(0, sibling, me).wait_recv()
        for j, chip in enumerate(chips):
            copy(4 + j, (*chip, 1 - mc), me).wait_recv()
        for cp in first + passed:
            cp.wait_send()
        mine.wait()

    anyspec = pl.BlockSpec(memory_space=pl.ANY)
    args = [x] if after is None else [x, after]
    return _pcall(body, name=name, in_specs=[anyspec] * len(args), out_specs=anyspec,
                  out_shape=jax.ShapeDtypeStruct((N_DEV,) + tuple(x.shape), x.dtype),
                  scratch_shapes=[pltpu.SemaphoreType.DMA((N_DEV - 1,)), pltpu.SemaphoreType.DMA((N_DEV - 1,)),
                                  pltpu.SemaphoreType.DMA(())])(*args)


_HBM = pl.BlockSpec(memory_space=pltpu.HBM)
_SEM = pl.BlockSpec(memory_space=pltpu.SEMAPHORE)
_EFFECT = pltpu.SideEffectType.DATAFLOW_SIDE_EFFECTING


def _exchange_start(x, *, all_to_all, name, after=None):
    blk = x.shape[1:] if all_to_all else x.shape
    land = lax.empty((N_DEV,) + tuple(blk), x.dtype)
    has_after = after is not None

    def body(*refs):
        x_ref, land_ref = refs[0], refs[1]
        send_sems, recv_sems, _, _, token, local_sem = refs[2 + has_after:]
        me = 4 * lax.axis_index("x") + 2 * lax.axis_index("y") + lax.axis_index("c")
        mine = pltpu.make_async_copy(x_ref.at[me] if all_to_all else x_ref, land_ref.at[me], local_sem)
        mine.start()
        mine.wait()
        for cp in _peer_copies(x_ref, land_ref, send_sems, recv_sems, all_to_all):
            cp.start()
        token[...] = jnp.zeros_like(token)

    n_sem = pltpu.SemaphoreType.DMA((N_DEV - 1,))
    args = [pltpu.with_memory_space_constraint(x, pltpu.HBM), pltpu.with_memory_space_constraint(land, pltpu.HBM)]
    in_specs = [_HBM, _HBM]
    if has_after:
        args.append(after)
        in_specs.append(pl.BlockSpec(memory_space=pl.ANY))
    send_sems, recv_sems, x_thru, land_thru, token = _pcall(
        body, name=name, in_specs=in_specs,
        out_shape=(n_sem, n_sem, pltpu.HBM(x.shape, x.dtype), pltpu.HBM(land.shape, land.dtype),
                   jax.ShapeDtypeStruct((8, LANES), F32)),
        out_specs=(_SEM, _SEM, _HBM, _HBM, pl.BlockSpec(memory_space=pltpu.VMEM)), input_output_aliases={0: 2, 1: 3},
        scratch_shapes=[pltpu.SemaphoreType.DMA(())],
        compiler_params=pltpu.CompilerParams(has_side_effects=_EFFECT))(*args)
    return (send_sems, recv_sems, x_thru, land_thru, all_to_all), token


def _exchange_wait(state, after, *, name):
    send_sems, recv_sems, x_thru, land_thru, all_to_all = state

    def body(x_ref, land_ref, send_sems, recv_sems, after_ref, x_dead, landed):
        for cp in _peer_copies(x_ref, land_ref, send_sems, recv_sems, all_to_all):
            cp.wait_send()
            cp.wait_recv()

    return _pcall(
        body, name=name, in_specs=(_HBM, _HBM, _SEM, _SEM, pl.BlockSpec(memory_space=pl.ANY)),
        out_shape=(pltpu.HBM(x_thru.shape, x_thru.dtype), pltpu.HBM(land_thru.shape, land_thru.dtype)),
        out_specs=(_HBM, _HBM), input_output_aliases={0: 0, 1: 1},
        compiler_params=pltpu.CompilerParams(has_side_effects=_EFFECT))(x_thru, land_thru, send_sems, recv_sems, after)[1]


def _exchange(x, *, all_to_all, name, after=None):
    blk = x.shape[1:] if all_to_all else x.shape

    def body(x_ref, *rest):
        o_ref, send_sems, recv_sems, local_sem = rest[-4:]
        me = 4 * lax.axis_index("x") + 2 * lax.axis_index("y") + lax.axis_index("c")
        mine = pltpu.make_async_copy(x_ref.at[me] if all_to_all else x_ref, o_ref.at[me], local_sem)
        mine.start()
        copies = _peer_copies(x_ref, o_ref, send_sems, recv_sems, all_to_all)
        for cp in copies:
            cp.start()
        for cp in copies:
            cp.wait()
        mine.wait()

    anyspec = pl.BlockSpec(memory_space=pl.ANY)
    args = [x] if after is None else [x, after]
    return _pcall(body, name=name, in_specs=[anyspec] * len(args), out_specs=anyspec,
                  out_shape=jax.ShapeDtypeStruct((N_DEV,) + tuple(blk), x.dtype),
                  scratch_shapes=[pltpu.SemaphoreType.DMA((N_DEV - 1,)), pltpu.SemaphoreType.DMA((N_DEV - 1,)),
                                  pltpu.SemaphoreType.DMA(())])(*args)


PACK_ROWS = 16


def _pack(arrs, width, dtype, lead=0):
    parts, segs, r = [], [], 0
    for a in arrs:
        lshape, shape = a.shape[:lead], a.shape[lead:]
        n = math.prod(shape)
        rows = -(-n // width)
        rows_p = -(-rows // PACK_ROWS) * PACK_ROWS
        if n == rows * width:
            blk = a.reshape(lshape + (rows, width)).astype(dtype)
            parts.append(jnp.pad(blk, [(0, 0)] * lead + [(0, rows_p - rows), (0, 0)]) if rows_p > rows else blk)
        else:
            flat = jnp.pad(a.reshape(lshape + (n,)).astype(dtype), [(0, 0)] * lead + [(0, rows_p * width - n)])
            parts.append(flat.reshape(lshape + (rows_p, width)))
        segs.append((r, n, shape))
        r += rows_p
    return jnp.concatenate(parts, axis=lead), segs


def _unpack(p, segs):
    lshape, width = p.shape[:-2], p.shape[-1]
    outs = []
    for r, n, shape in segs:
        rows = -(-n // width)
        blk = p[..., r:r + rows, :]
        if n != rows * width:
            blk = blk.reshape(lshape + (rows * width,))[..., :n]
        outs.append(blk.reshape(lshape + shape))
    return outs


def _split_cols(a, f_off, h):
    return jnp.concatenate([a[..., :f_off], a[..., f_off + h:]], axis=-1), a[..., f_off:f_off + h]


def _merge_cols(main, f, f_off):
    return jnp.concatenate([main[..., :f_off], f, main[..., f_off:]], axis=-1)


def _pad_to(a, n, axis):
    pad = [(0, 0)] * a.ndim
    pad[axis] = (0, n - a.shape[axis])
    return jnp.pad(a, pad)


def kernel(x, c, w_ada, b_ada, w_in, b_in, conv_a_w, conv_a_b, ln_conv_g, ln_conv_b, w_conv_proj, w_attn_proj, w_mix_out, b_mix_out, ln1_g, ln1_b, w_ffn_up, ffn_conv_w, ffn_conv_b, w_ffn_down, ln2_g, ln2_b, loss_target, m_w_ada, m_b_ada, m_w_in, m_b_in, m_conv_a_w, m_conv_a_b, m_ln_conv_g, m_ln_conv_b, m_w_conv_proj, m_w_attn_proj, m_w_mix_out, m_b_mix_out, m_ln1_g, m_ln1_b, m_w_ffn_up, m_ffn_conv_w, m_ffn_conv_b, m_w_ffn_down, m_ln2_g, m_ln2_b, v_w_ada, v_b_ada, v_w_in, v_b_in, v_conv_a_w, v_conv_a_b, v_ln_conv_g, v_ln_conv_b, v_w_conv_proj, v_w_attn_proj, v_w_mix_out, v_b_mix_out, v_ln1_g, v_ln1_b, v_w_ffn_up, v_ffn_conv_w, v_ffn_conv_b, v_w_ffn_down, v_ln2_g, v_ln2_b):
    L, D = w_ada.shape[0], w_ada.shape[1]
    Bl, S, _ = x.shape
    C, KW, AW = conv_a_b.shape[1], conv_a_w.shape[1], w_attn_proj.shape[1]
    F, KF, n_in_all = ffn_conv_b.shape[1] // 2, ffn_conv_w.shape[1], b_in.shape[1]
    H = n_in_all - 2 * C - 3 * AW - 2 * D
    cfg = Cfg(L=L, Bl=Bl, S=S, D=D, C=C, KW=KW, H=H, Dh=AW // H, F=F, KF=KF)
    T, NM = cfg.T, cfg.NM
    f_off = 2 * C + 3 * AW
    n_ada = w_ada.shape[2]
    me = 4 * lax.axis_index("x") + 2 * lax.axis_index("y") + lax.axis_index("c")

    def my_cols(a, n):
        return lax.dynamic_slice_in_dim(a, me * n, n, axis=a.ndim - 1)

    spack, ssegs = _pack([c, conv_a_w, ffn_conv_w], D, F32)
    c_g, caw_g, fcw_g = _unpack(_exchange(spack, all_to_all=False, name="gather_small"), ssegs)
    c_all = c_g.reshape(N_DEV * Bl, D)
    caw = _pad_to(jnp.moveaxis(caw_g, 0, 2).reshape(L, KW, C), 32, 1)
    fcw = _pad_to(jnp.moveaxis(fcw_g, 0, 2).reshape(L, KF, 2 * F), 8, 1)

    mod_part = _ada_fwd(c_all, w_ada, my_cols(b_ada, n_ada)[:, None, :], name="ada_fwd")
    mod_send = jnp.moveaxis(mod_part.reshape(L, N_DEV, Bl, n_ada), 1, 0).reshape(N_DEV, L * Bl, n_ada)
    mod_recv = _exchange(mod_send, all_to_all=True, name="exchange_mod")
    mod = jnp.moveaxis(mod_recv.reshape(N_DEV, L, Bl, n_ada), 0, 2).reshape(L, Bl, 6, 1, D)
    shift1, scale1, gate1, shift2, scale2, gate2 = (mod[:, :, i] for i in range(6))

    big_names = ["w_in", "w_conv_proj", "w_attn_proj", "w_mix_out", "w_ffn_up", "w_ffn_down"]
    transposed = (True, True, True, False, True, False)

    def shard_items(arrs, grp):
        return [arrs[i][l].T if transposed[i] else arrs[i][l] for l, i in grp]

    W = [dict() for _ in range(L)]

    def set_weights(landed, segs, grp):
        for (l, i), a in zip(grp, _unpack(landed, segs)):
            a = a.reshape((-1, a.shape[-1]))
            if i == 0:
                wm_t, wf_t = _split_cols(a.T, f_off, H)
                bm, bf = _split_cols(b_in[l], f_off, H)
                W[l].update(wm_t=wm_t.T, wf_t=_pad_to(wf_t.T, LANES, 0), bm=bm[None], bf=_pad_to(bf, LANES, 0)[None])
            else:
                W[l][("w_cp_t", "w_ap_t", "w_mo", "w_up_t", "w_dn")[i - 1]] = a

    big_w = (w_in, w_conv_proj, w_attn_proj, w_mix_out, w_ffn_up, w_ffn_down)
    w_groups = [[(l, i) for i in range(6)] for l in range(L)]
    pack, segs = _pack(shard_items(big_w, w_groups[0]), D, BF16)
    landed0 = _gather_two_level(pack, name="gather_weights_0", after=mod_recv)
    set_weights(landed0, segs, w_groups[0])
    w_state, token = {}, landed0
    for l in range(1, L):
        pack, segs = _pack(shard_items(big_w, w_groups[l]), D, BF16)
        state, token = _exchange_start(pack, all_to_all=False, name=f"gather_weights_start_{l}", after=token)
        w_state[l] = (state, pack, segs)

    def wait_weights(l, after):
        state, pack, segs = w_state[l]
        set_weights(_exchange_wait(state, after, name=f"gather_weights_wait_{l}"), segs, w_groups[l])

    xf = x.reshape(T, D)
    u = _ln_mod_fwd(xf, shift1[0], scale1[0], cfg, name="ln_mod_fwd")
    saved = []
    xin = xf
    for l in range(L):
        w = W[l]
        if l > 0:
            wait_weights(l, u)
        zm = _matmul(u, w["wm_t"], mode="nt", bias=w["bm"], name=f"in_proj_{l}", after=token if l == 0 else None)
        zf = _matmul(u, w["wf_t"], mode="nt", bias=w["bf"], name=f"in_proj_f_{l}")
        a3 = _conv_a_fwd(zm, caw[l], conv_a_b[l][None], ln_conv_g[l][None], ln_conv_b[l][None], cfg, name=f"conv_a_fwd_{l}")
        cum_c = _fgate_fwd(zf, cfg, name=f"fgate_fwd_{l}")
        o, o32, lse = _attn_fwd(zm, cum_c, cfg, name=f"attn_fwd_{l}")
        ya =_matmul(a3, w["w_cp_t"], mode="nt", name=f"conv_proj_{l}")
        yb = _matmul(o, w["w_ap_t"], mode="nt", name=f"attn_proj_{l}")
        mg = _merge_fwd(zm, ya, yb, cfg, name=f"merge_fwd_{l}")
        mix = _matmul(mg, w["w_mo"], mode="nn", bias=b_mix_out[l][None], name=f"mix_out_{l}")
        x1, u2 = _res_ln_fwd(xin, mix, gate1[l], ln1_g[l][None], ln1_b[l][None], cfg, name=f"res_ln1_fwd_{l}",
                             nxt=(shift2[l], scale2[l]))
        h0 = _matmul(u2, w["w_up_t"], mode="nt", name=f"ffn_up_{l}")
        fa = _ffn_conv_fwd(h0, fcw[l], ffn_conv_b[l][None], cfg, name=f"ffn_conv_fwd_{l}")
        ffn = _matmul(fa, w["w_dn"], mode="nn", name=f"ffn_down_{l}")
        saved.append(dict(x=xin, u=u, zm=zm, zf=zf, a3=a3, cum_c=cum_c, o=o, o32=o32, lse=lse, ya=ya, yb=yb, mg=mg, mix=mix,
                          x1=x1, u2=u2, h0=h0, fa=fa, ffn=ffn))
        if l + 1 < L:
            xin, u = _res_ln_fwd(x1, ffn, gate2[l], ln2_g[l][None], ln2_b[l][None], cfg, name=f"res_ln2_fwd_{l}",
                                 nxt=(shift1[l + 1], scale1[l + 1]))
        else:
            xin = _res_ln_fwd(x1, ffn, gate2[l], ln2_g[l][None], ln2_b[l][None], cfg, name=f"res_ln2_fwd_{l}")

    dx, loss_tiles = _loss_grad(xin, loss_target.reshape(T, D), cfg, name="loss_grad")
    loss = lax.psum(0.5 / D * jnp.sum(loss_tiles[:, 0, 0]), ("x", "y", "c"))

    gbig = {}
    g_groups = [[(l, i) for i in range(6)] for l in reversed(range(1, L))] + [[(0, 4), (0, 5)], [(0, 1), (0, 2), (0, 3)], [(0, 0)]]
    g_state = []

    def start_grads(after=None):
        grp = g_groups[len(g_state)]
        send, segs = _pack([gbig[k].reshape((N_DEV, -1, gbig[k].shape[1])) for k in grp], D, BF16, lead=1)
        state, tok = _exchange_start(send, all_to_all=True, name=f"exchange_grads_start_{len(g_state)}", after=after)
        g_state.append((state, send, segs, grp))
        return tok

    gsm = [dict() for _ in range(L)]
    dmods = [None] * L
    token = None
    for l in reversed(range(L)):
        w, s = W[l], saved[l]
        dres2, dffn, dg2, db2, dgate2, _ = _res_ln_bwd(dx, s["x1"], s["ffn"], gate2[l], ln2_g[l][None], cfg, name=f"res_ln2_bwd_{l}")
        dfa = _matmul(dffn, w["w_dn"], mode="nt", name=f"d_ffn_act_{l}", after=token)
        gbig[l, 5] = _matmul(s["fa"], dffn, mode="tn", name=f"dw_ffn_down_{l}")
        dh0g, dh0l, dwg, dwl, dcg, dcl = _ffn_conv_bwd(dfa, s["h0"], fcw[l], ffn_conv_b[l][None], cfg, name=f"ffn_conv_bwd_{l}")
        dh0 = jnp.concatenate([dh0g, dh0l], axis=1)
        du2 = _matmul(dh0, w["w_up_t"], mode="nn", name=f"d_u2_{l}")
        gbig[l, 4] = _matmul(dh0, s["u2"], mode="tn", name=f"dw_ffn_up_{l}")
        token = start_grads() if l == 0 else None
        dx1, dscale2, dshift2 = _ln_mod_bwd(du2, s["x1"], scale2[l], dres2, cfg, name=f"ln_mod2_bwd_{l}")
        dres1, dmix, dg1, db1, dgate1, dbmo = _res_ln_bwd(dx1, s["x"], s["mix"], gate1[l], ln1_g[l][None], cfg, name=f"res_ln1_bwd_{l}")
        dmg = _matmul(dmix, w["w_mo"], mode="nt", name=f"d_merge_{l}", after=token)
        gbig[l, 3] = _matmul(s["mg"], dmix, mode="tn", name=f"dw_mix_out_{l}")
        dya, dyb, dzga, dzgb = _merge_bwd(dmg, s["zm"], s["ya"], s["yb"], cfg, name=f"merge_bwd_{l}")
        gbig[l, 1] = _matmul(dya, s["a3"], mode="tn", name=f"dw_conv_proj_{l}")
        da3 = _matmul(dya, w["w_cp_t"], mode="nn", name=f"d_a3_{l}")
        gbig[l, 2] = _matmul(dyb, s["o"], mode="tn", name=f"dw_attn_proj_{l}")
        token = start_grads() if l == 0 else None
        do = _matmul(dyb, w["w_ap_t"], mode="nn", out_dtype=BF16, name=f"d_o_{l}", after=token)
        dq, dk, dv, dcum_c = _attn_bwd(s["zm"], s["cum_c"], s["o32"], do, s["lse"], cfg, name=f"attn_bwd_{l}")
        dzf = _fgate_bwd(dcum_c, s["zf"], cfg, name=f"fgate_bwd_{l}")
        dzglu, dcaw, dcab, dlcg, dlcb = _conv_a_bwd(da3, s["zm"], caw[l], conv_a_b[l][None], ln_conv_g[l][None],
                                                    ln_conv_b[l][None], cfg, name=f"conv_a_bwd_{l}")
        dzm = jnp.concatenate([dzglu, dq, dk, dv, dzga, dzgb], axis=1)
        du1 = _matmul(dzf, w["wf_t"], mode="nn", name=f"d_u1_f_{l}")
        du1 = _matmul(dzm, w["wm_t"], mode="nn", add=du1, name=f"d_u1_{l}")
        dwm_t = _matmul(dzm, s["u"], mode="tn", name=f"dw_in_{l}")
        dwf_t = _matmul(dzf, s["u"], mode="tn", name=f"dw_in_f_{l}")
        gbig[l, 0] = _merge_cols(dwm_t.T, dwf_t[:H].T, f_off).T
        token = start_grads() if l > 0 else None
        dbm, dbf = _colsum(dzm, name=f"db_in_{l}"), _colsum(dzf, name=f"db_in_f_{l}")
        dx, dscale1, dshift1 = _ln_mod_bwd(du1, s["x"], scale1[l], dres1, cfg, name=f"ln_mod1_bwd_{l}")
        dmods[l] = jnp.concatenate([dshift1, dscale1, dgate1, dshift2, dscale2, dgate2], axis=1).reshape(Bl, 6 * D)
        gsm[l] = dict(b_in=_merge_cols(dbm[0], dbf[0, :H], f_off), conv_a_b=dcab[0], ln_conv_g=dlcg[0], ln_conv_b=dlcb[0],
                      b_mix_out=dbmo[0], ln1_g=dg1[0], ln1_b=db1[0], ffn_conv_b=jnp.concatenate([dcg[0], dcl[0]]),
                      ln2_g=dg2[0], ln2_b=db2[0], conv_a_w=dcaw[:KW], ffn_conv_w=jnp.concatenate([dwg[:KF], dwl[:KF]], axis=1))
    grad_x = dx.reshape(Bl, S, D)

    small_names = ["b_in", "conv_a_b", "ln_conv_g", "ln_conv_b", "b_mix_out", "ln1_g", "ln1_b", "ffn_conv_b", "ln2_g", "ln2_b",
                   "conv_a_w", "ffn_conv_w"]
    gs_list = [jnp.stack(dmods)] + [jnp.stack([gsm[l][n] for l in range(L)]) for n in small_names]
    gspack, gssegs = _pack(gs_list, D, F32)
    gs_all = _exchange(gspack, all_to_all=False, name="gather_small_grads")
    start_grads(after=gs_all)
    dmod_all = jnp.moveaxis(_unpack(gs_all, gssegs)[0], 0, 1).reshape(L, N_DEV * Bl, 6 * D)
    g_small = dict(zip(small_names, _unpack(_slot_sum(gs_all, name="sum_small_grads"), gssegs)[1:]))
    g_small["conv_a_w"] = my_cols(g_small["conv_a_w"], C // N_DEV)
    g_small["ffn_conv_w"] = my_cols(g_small["ffn_conv_w"], 2 * F // N_DEV)
    g_small["w_ada"] = _ada_bwd(c_all, my_cols(dmod_all, n_ada), name="ada_bwd")
    g_small["b_ada"] = jnp.stack([_colsum(dmod_all[l], name=f"db_ada_{l}")[0] for l in range(L)])

    given = dict(w_in=(w_in, m_w_in, v_w_in), w_conv_proj=(w_conv_proj, m_w_conv_proj, v_w_conv_proj),
                 w_attn_proj=(w_attn_proj, m_w_attn_proj, v_w_attn_proj), w_mix_out=(w_mix_out, m_w_mix_out, v_w_mix_out),
                 w_ffn_up=(w_ffn_up, m_w_ffn_up, v_w_ffn_up), w_ffn_down=(w_ffn_down, m_w_ffn_down, v_w_ffn_down),
                 w_ada=(w_ada, m_w_ada, v_w_ada), b_ada=(b_ada, m_b_ada, v_b_ada), b_in=(b_in, m_b_in, v_b_in),
                 conv_a_w=(conv_a_w, m_conv_a_w, v_conv_a_w), conv_a_b=(conv_a_b, m_conv_a_b, v_conv_a_b),
                 ln_conv_g=(ln_conv_g, m_ln_conv_g, v_ln_conv_g), ln_conv_b=(ln_conv_b, m_ln_conv_b, v_ln_conv_b),
                 b_mix_out=(b_mix_out, m_b_mix_out, v_b_mix_out), ln1_g=(ln1_g, m_ln1_g, v_ln1_g), ln1_b=(ln1_b, m_ln1_b, v_ln1_b),
                 ffn_conv_w=(ffn_conv_w, m_ffn_conv_w, v_ffn_conv_w), ffn_conv_b=(ffn_conv_b, m_ffn_conv_b, v_ffn_conv_b),
                 ln2_g=(ln2_g, m_ln2_g, v_ln2_g), ln2_b=(ln2_b, m_ln2_b, v_ln2_b))
    res, kinds = {}, ("grad", "delta", "new_m", "new_v")
    loc_names = ["w_ada", "b_ada"] + small_names
    packs = [_pack([g_small[n] for n in loc_names], D, F32)] + [_pack([given[n][i] for n in loc_names], D, F32) for i in range(3)]
    outs = _adamw(packs[0][0][None], packs[1][0], packs[2][0], packs[3][0], name="adamw_small")
    for kind, packed in zip(kinds, outs):
        for n, a in zip(loc_names, _unpack(packed, packs[0][1])):
            res[kind, n] = a

    big_parts = {}
    after = outs[0]
    for gi, (state, send, segs, grp) in enumerate(g_state):
        landed = _exchange_wait(state, after, name=f"exchange_grads_wait_{gi}")
        wmv = [_pack(shard_items([given[n][j] for n in big_names], grp), D, F32)[0] for j in range(3)]
        outs = _adamw(landed, *wmv, name=f"adamw_big_{gi}")
        for kind, packed in zip(kinds, outs):
            for (l, i), a in zip(grp, _unpack(packed, segs)):
                big_parts[kind, l, i] = a.T if transposed[i] else a
        after = outs[0]
    for kind in kinds:
        for i, n in enumerate(big_names):
            res[kind, n] = jnp.stack([big_parts[kind, l, i] for l in range(L)])

    order = ["w_ada", "b_ada", "w_in", "b_in", "conv_a_w", "conv_a_b", "ln_conv_g", "ln_conv_b", "w_conv_proj", "w_attn_proj",
             "w_mix_out", "b_mix_out", "ln1_g", "ln1_b", "w_ffn_up", "ffn_conv_w", "ffn_conv_b", "w_ffn_down", "ln2_g", "ln2_b"]
    return (loss, grad_x, *[res[k, n] for k in ("grad", "delta", "new_m", "new_v") for n in order])
```

```python
import functools
import math
from typing import NamedTuple

import jax
import jax.numpy as jnp
from jax import lax
from jax.experimental import pallas as pl
from jax.experimental.pallas import tpu as pltpu

F32, BF16 = jnp.float32, jnp.bfloat16
LN_EPS = 1e-5
ADAM_LR, ADAM_B1, ADAM_B2, ADAM_EPS, ADAM_WD, ADAM_STEP = 0.001, 0.9, 0.999, 1e-08, 0.01, 10
N_DEV = 8
LANES = 128
VMEM_LIMIT = 56 * 1024 * 1024
NEG = -1e30
NT = (((1,), (1,)), ((), ()))
TN = (((0,), (0,)), ((), ()))


class Cfg(NamedTuple):
    L: int
    Bl: int
    S: int
    D: int
    C: int
    KW: int
    H: int
    Dh: int
    F: int
    KF: int

    @property
    def T(self): return self.Bl * self.S
    @property
    def AW(self): return self.H * self.Dh
    @property
    def NM(self): return 2 * self.C + 3 * self.AW + 2 * self.D
    @property
    def q_off(self): return 2 * self.C
    @property
    def g_off(self): return 2 * self.C + 3 * self.AW
    @property
    def alpha(self): return (2.0 * self.L) ** 0.25


def _pcall(body, **kw):
    return pl.pallas_call(body, **kw)


def _params(*sem):
    return pltpu.CompilerParams(dimension_semantics=sem, vmem_limit_bytes=VMEM_LIMIT)


def _pick(n, prefs):
    for p in prefs:
        if n % p == 0:
            return p
    return n


def _sigmoid(x):
    return 1.0 / (1.0 + jnp.exp(-x))


def _ln_stats(x):
    mu = jnp.mean(x, axis=-1, keepdims=True)
    xc = x - mu
    var = jnp.mean(xc * xc, axis=-1, keepdims=True)
    rstd = lax.rsqrt(var + LN_EPS)
    return xc * rstd, rstd


def _ln_bwd(dxh, xh, rstd):
    return rstd * (dxh - jnp.mean(dxh, axis=-1, keepdims=True) - xh * jnp.mean(dxh * xh, axis=-1, keepdims=True))


def _matmul(a, b, *, mode, name, bias=None, add=None, out_dtype=F32, tm=None, tn=None, tk=None, after=None):
    if mode == "tn":
        K, M = a.shape
    else:
        M, K = a.shape
    N = b.shape[0] if mode == "nt" else b.shape[1]
    lane_tiles = (1536, 1408, 1024, 768, 512, 256, 128)
    tm = tm or _pick(M, lane_tiles if mode == "tn" else (1024, 512, 256, 128, 64, 32, 16, 8))
    tn = tn or _pick(N, lane_tiles)
    tk = tk or _pick(K, (1024, 512, 256, 128) if mode == "tn" else lane_tiles)
    nk = K // tk
    dn = {"nn": (((1,), (0,)), ((), ())), "nt": NT, "tn": TN}[mode]
    has_bias, has_add, has_after = bias is not None, add is not None, after is not None

    def body(*refs):
        a_ref, b_ref = refs[0], refs[1]
        pos = 2
        bias_ref = refs[pos] if has_bias else None
        pos += has_bias
        add_ref = refs[pos] if has_add else None
        pos += has_add + has_after
        o_ref = refs[pos]
        part = lax.dot_general(a_ref[...], b_ref[...], dn, preferred_element_type=F32)

        def finish(acc):
            if has_bias:
                acc = acc + bias_ref[...]
            if has_add:
                acc = acc + add_ref[...]
            o_ref[...] = acc.astype(out_dtype)

        if nk == 1:
            finish(part)
        else:
            acc_ref = refs[pos + 1]
            k = pl.program_id(2)

            @pl.when(k == 0)
            def _():
                acc_ref[...] = part

            @pl.when(k > 0)
            def _():
                acc_ref[...] += part

            @pl.when(k == nk - 1)
            def _():
                finish(acc_ref[...])

    a_spec = pl.BlockSpec((tk, tm), lambda i, j, k: (k, i)) if mode == "tn" else pl.BlockSpec((tm, tk), lambda i, j, k: (i, k))
    b_spec = pl.BlockSpec((tn, tk), lambda i, j, k: (j, k)) if mode == "nt" else pl.BlockSpec((tk, tn), lambda i, j, k: (k, j))
    in_specs, args = [a_spec, b_spec], [a, b]
    if has_bias:
        in_specs.append(pl.BlockSpec((1, tn), lambda i, j, k: (0, j)))
        args.append(bias)
    if has_add:
        in_specs.append(pl.BlockSpec((tm, tn), lambda i, j, k: (i, j)))
        args.append(add)
    if has_after:
        in_specs.append(pl.BlockSpec(memory_space=pl.ANY))
        args.append(after)
    return _pcall(
        body, name=name, grid=(M // tm, N // tn, nk), in_specs=in_specs,
        out_specs=pl.BlockSpec((tm, tn), lambda i, j, k: (i, j)),
        out_shape=jax.ShapeDtypeStruct((M, N), out_dtype),
        scratch_shapes=[pltpu.VMEM((tm, tn), F32)] if nk > 1 else [],
        compiler_params=_params("parallel", "parallel", "arbitrary"),
    )(*args)


def _colsum(x, *, name):
    T, N = x.shape
    tr = _pick(T, (512, 256, 128, 64, 32, 16))
    tc = _pick(N, (1536, 1024, 512, 256, 128))

    def body(x_ref, o_ref):
        @pl.when(pl.program_id(1) == 0)
        def _():
            o_ref[...] = jnp.zeros_like(o_ref)

        o_ref[...] += jnp.sum(x_ref[...].astype(F32), axis=0, keepdims=True)

    return _pcall(body, name=name, grid=(N // tc, T // tr), in_specs=[pl.BlockSpec((tr, tc), lambda j, i: (i, j))],
                  out_specs=pl.BlockSpec((1, tc), lambda j, i: (0, j)), out_shape=jax.ShapeDtypeStruct((1, N), F32),
                  compiler_params=_params("parallel", "arbitrary"))(x)


def _row_tile(cfg):
    return _pick(cfg.S, (256, 128, 64, 32, 16, 8))


def _ln_mod_fwd(x, shift, scale, cfg, *, name):
    tr = _row_tile(cfg)
    tpb = cfg.S // tr

    def body(x_ref, sh_ref, sc_ref, u_ref):
        xh, _ = _ln_stats(x_ref[...])
        u_ref[...] = (xh * (1.0 + sc_ref[0]) + sh_ref[0]).astype(BF16)

    row = pl.BlockSpec((tr, cfg.D), lambda i: (i, 0))
    per_b = pl.BlockSpec((1, 1, cfg.D), lambda i: (i // tpb, 0, 0))
    return _pcall(body, name=name, grid=(cfg.T // tr,), in_specs=[row, per_b, per_b], out_specs=row,
                  out_shape=jax.ShapeDtypeStruct((cfg.T, cfg.D), BF16), compiler_params=_params("parallel"))(x, shift, scale)


def _res_ln_fwd(xin, br, gate, g, b, cfg, *, name, nxt=None):
    tr = _row_tile(cfg)
    tpb = cfg.S // tr
    alpha = cfg.alpha

    def body(*refs):
        x_ref, br_ref, gt_ref, g_ref, b_ref = refs[:5]
        r = alpha * x_ref[...] + (1.0 + gt_ref[0]) * br_ref[...]
        xh, _ = _ln_stats(r)
        xo = xh * g_ref[...] + b_ref[...]
        if nxt is None:
            refs[5][...] = xo
        else:
            sh_ref, sc_ref, xo_ref, u_ref = refs[5:]
            xo_ref[...] = xo
            uh, _ = _ln_stats(xo)
            u_ref[...] = (uh * (1.0 + sc_ref[0]) + sh_ref[0]).astype(BF16)

    row = pl.BlockSpec((tr, cfg.D), lambda i: (i, 0))
    per_b = pl.BlockSpec((1, 1, cfg.D), lambda i: (i // tpb, 0, 0))
    vec = pl.BlockSpec((1, cfg.D), lambda i: (0, 0))
    in_specs, args = [row, row, per_b, vec, vec], [xin, br, gate, g, b]
    out_specs, out_shape = row, jax.ShapeDtypeStruct((cfg.T, cfg.D), F32)
    if nxt is not None:
        in_specs += [per_b, per_b]
        args += list(nxt)
        out_specs = [row, row]
        out_shape = [out_shape, jax.ShapeDtypeStruct((cfg.T, cfg.D), BF16)]
    return _pcall(body, name=name, grid=(cfg.T // tr,), in_specs=in_specs, out_specs=out_specs, out_shape=out_shape,
                  compiler_params=_params("parallel"))(*args)


def _loss_grad(y, tgt, cfg, *, name):
    tr = _row_tile(cfg)
    nt = cfg.T // tr
    inv_d = 1.0 / cfg.D

    def body(y_ref, t_ref, dy_ref, ls_ref):
        e = y_ref[...] - t_ref[...]
        dy_ref[...] = e * inv_d
        ls_ref[...] = jnp.full((1, 1, LANES), jnp.sum(e * e), F32)

    row = pl.BlockSpec((tr, cfg.D), lambda i: (i, 0))
    return _pcall(body, name=name, grid=(nt,), in_specs=[row, row],
                  out_specs=[row, pl.BlockSpec((1, 1, LANES), lambda i: (i, 0, 0))],
                  out_shape=[jax.ShapeDtypeStruct((cfg.T, cfg.D), F32), jax.ShapeDtypeStruct((nt, 1, LANES), F32)],
                  compiler_params=_params("parallel"))(y, tgt)


def _res_ln_bwd(dy, xin, br, gate, g, cfg, *, name):
    tr = _row_tile(cfg)
    tpb = cfg.S // tr
    alpha = cfg.alpha

    def body(dy_ref, x_ref, br_ref, gt_ref, g_ref, dx_ref, dbr_ref, dg_ref, db_ref, dgt_ref, dbs_ref):
        i = pl.program_id(0)

        @pl.when(i == 0)
        def _():
            dg_ref[...] = jnp.zeros_like(dg_ref)
            db_ref[...] = jnp.zeros_like(db_ref)
            dbs_ref[...] = jnp.zeros_like(dbs_ref)

        @pl.when(i % tpb == 0)
        def _():
            dgt_ref[...] = jnp.zeros_like(dgt_ref)

        dy, brv, one_gate = dy_ref[...], br_ref[...], 1.0 + gt_ref[0]
        xh, rstd = _ln_stats(alpha * x_ref[...] + one_gate * brv)
        dg_ref[...] += jnp.sum(dy * xh, axis=0, keepdims=True)
        db_ref[...] += jnp.sum(dy, axis=0, keepdims=True)
        dr = _ln_bwd(dy * g_ref[...], xh, rstd)
        dx_ref[...] = alpha * dr
        dbr = one_gate * dr
        dbr_ref[...] = dbr.astype(BF16)
        dbs_ref[...] += jnp.sum(dbr, axis=0, keepdims=True)
        dgt_ref[0] += jnp.sum(dr * brv, axis=0, keepdims=True)

    row = pl.BlockSpec((tr, cfg.D), lambda i: (i, 0))
    per_b = pl.BlockSpec((1, 1, cfg.D), lambda i: (i // tpb, 0, 0))
    vec = pl.BlockSpec((1, cfg.D), lambda i: (0, 0))
    vs = jax.ShapeDtypeStruct((1, cfg.D), F32)
    return _pcall(body, name=name, grid=(cfg.T // tr,), in_specs=[row, row, row, per_b, vec],
                  out_specs=[row, row, vec, vec, per_b, vec],
                  out_shape=[jax.ShapeDtypeStruct((cfg.T, cfg.D), F32), jax.ShapeDtypeStruct((cfg.T, cfg.D), BF16), vs, vs,
                             jax.ShapeDtypeStruct((cfg.Bl, 1, cfg.D), F32), vs],
                  compiler_params=_params("arbitrary"))(dy, xin, br, gate, g)


def _ln_mod_bwd(du, xin, scale, dres, cfg, *, name):
    tr = _row_tile(cfg)
    tpb = cfg.S // tr

    def body(du_ref, x_ref, sc_ref, dres_ref, dx_ref, dsc_ref, dsh_ref):
        @pl.when(pl.program_id(0) % tpb == 0)
        def _():
            dsc_ref[...] = jnp.zeros_like(dsc_ref)
            dsh_ref[...] = jnp.zeros_like(dsh_ref)

        du = du_ref[...]
        xh, rstd = _ln_stats(x_ref[...])
        dsc_ref[0] += jnp.sum(du * xh, axis=0, keepdims=True)
        dsh_ref[0] += jnp.sum(du, axis=0, keepdims=True)
        dx_ref[...] = _ln_bwd(du * (1.0 + sc_ref[0]), xh, rstd) + dres_ref[...]

    row = pl.BlockSpec((tr, cfg.D), lambda i: (i, 0))
    per_b = pl.BlockSpec((1, 1, cfg.D), lambda i: (i // tpb, 0, 0))
    bs = jax.ShapeDtypeStruct((cfg.Bl, 1, cfg.D), F32)
    return _pcall(body, name=name, grid=(cfg.T // tr,), in_specs=[row, row, per_b, row], out_specs=[row, per_b, per_b],
                  out_shape=[jax.ShapeDtypeStruct((cfg.T, cfg.D), F32), bs, bs],
                  compiler_params=_params("arbitrary"))(du, xin, scale, dres)


def _merge_tiles(cfg):
    tr = _pick(cfg.T, (512, 256, 128, 64, 32, 16))
    tc = _pick(math.gcd(cfg.g_off, cfg.D), (512, 256, 128))
    return tr, tc


def _merge_fwd(zm, ya, yb, cfg, *, name):
    tr, tc = _merge_tiles(cfg)
    ga0, gb0 = cfg.g_off // tc, (cfg.g_off + cfg.D) // tc

    def body(ga_ref, gb_ref, ya_ref, yb_ref, m_ref):
        m_ref[...] = (_sigmoid(ga_ref[...]) * ya_ref[...] + _sigmoid(gb_ref[...]) * yb_ref[...]).astype(BF16)

    blk = pl.BlockSpec((tr, tc), lambda i, j: (i, j))
    return _pcall(body, name=name, grid=(cfg.T // tr, cfg.D // tc),
                  in_specs=[pl.BlockSpec((tr, tc), lambda i, j: (i, ga0 + j)), pl.BlockSpec((tr, tc), lambda i, j: (i, gb0 + j)), blk, blk],
                  out_specs=blk, out_shape=jax.ShapeDtypeStruct((cfg.T, cfg.D), BF16),
                  compiler_params=_params("parallel", "parallel"))(zm, zm, ya, yb)


def _merge_bwd(dm, zm, ya, yb, cfg, *, name):
    tr, tc = _merge_tiles(cfg)
    ga0, gb0 = cfg.g_off // tc, (cfg.g_off + cfg.D) // tc

    def body(dm_ref, ga_ref, gb_ref, ya_ref, yb_ref, dya_ref, dyb_ref, dga_ref, dgb_ref):
        dm = dm_ref[...]
        ga, gb = _sigmoid(ga_ref[...]), _sigmoid(gb_ref[...])
        dya_ref[...] = (dm * ga).astype(BF16)
        dyb_ref[...] = (dm * gb).astype(BF16)
        dga_ref[...] = (dm * ya_ref[...] * ga * (1.0 - ga)).astype(BF16)
        dgb_ref[...] = (dm * yb_ref[...] * gb * (1.0 - gb)).astype(BF16)

    blk = pl.BlockSpec((tr, tc), lambda i, j: (i, j))
    o = jax.ShapeDtypeStruct((cfg.T, cfg.D), BF16)
    return _pcall(body, name=name, grid=(cfg.T // tr, cfg.D // tc),
                  in_specs=[blk, pl.BlockSpec((tr, tc), lambda i, j: (i, ga0 + j)), pl.BlockSpec((tr, tc), lambda i, j: (i, gb0 + j)), blk, blk],
                  out_specs=[blk] * 4, out_shape=[o] * 4, compiler_params=_params("parallel", "parallel"))(dm, zm, zm, ya, yb)


CONV_A_HALO = 32
CONV_A_CHUNK = 32


SUBLANES = 8


def _conv_a_tile(cfg):
    assert cfg.KW - 1 <= CONV_A_HALO
    return _pick(cfg.S, (256, 128, 64, 32))


def _shift_copies(src_s, sh_s):
    rows = src_s.shape[0] - SUBLANES
    for b in range(1, SUBLANES):
        sh_s[b - 1, :, :] = src_s[b:b + rows, :]


def _rows(src_s, sh_s, start, n):
    a, b = divmod(start, SUBLANES)
    return src_s[start:start + n, :] if b == 0 else sh_s[b - 1, SUBLANES * a:SUBLANES * a + n, :]


def _conv_a_fwd(zm, w, cb, g, b, cfg, *, name):
    C, KW, HALO, CH = cfg.C, cfg.KW, CONV_A_HALO, CONV_A_CHUNK
    ts = _conv_a_tile(cfg)
    tpb = cfg.S // ts
    lead = HALO - (KW - 1)

    def body(z_ref, zp_ref, w_ref, cb_ref, g_ref, b_ref, o_ref, a0_s, a0_sh):
        first = pl.program_id(0) % tpb == 0
        prev = zp_ref[:, :C] * _sigmoid(zp_ref[:, C:])
        a0_s[0:HALO, :] = jnp.where(first, 0.0, prev)
        a0_s[HALO:HALO + ts, :] = z_ref[:, :C] * _sigmoid(z_ref[:, C:])
        _shift_copies(a0_s, a0_sh)
        for r0 in range(0, ts, CH):
            acc = jnp.zeros((CH, C), F32)
            for k in range(KW):
                acc = acc + w_ref[k:k + 1, :] * _rows(a0_s, a0_sh, r0 + lead + k, CH)
            xh, _ = _ln_stats(acc + cb_ref[...])
            a2 = xh * g_ref[...] + b_ref[...]
            o_ref[r0:r0 + CH, :] = (a2 * _sigmoid(a2)).astype(BF16)

    hb = ts // HALO
    vec = pl.BlockSpec((1, C), lambda i: (0, 0))
    return _pcall(body, name=name, grid=(cfg.T // ts,),
                  in_specs=[pl.BlockSpec((ts, 2 * C), lambda i: (i, 0)),
                            pl.BlockSpec((HALO, 2 * C), lambda i: (jnp.maximum(i * hb - 1, 0), 0)),
                            pl.BlockSpec((32, C), lambda i: (0, 0)), vec, vec, vec],
                  out_specs=pl.BlockSpec((ts, C), lambda i: (i, 0)), out_shape=jax.ShapeDtypeStruct((cfg.T, C), BF16),
                  scratch_shapes=[pltpu.VMEM((HALO + ts, C), F32), pltpu.VMEM((SUBLANES - 1, HALO + ts - SUBLANES, C), F32)],
                  compiler_params=_params("parallel"))(zm, zm, w, cb, g, b)


def _conv_a_bwd(da3, zm, w, cb, g, b, cfg, *, name):
    C, KW, HALO, CH = cfg.C, cfg.KW, CONV_A_HALO, CONV_A_CHUNK
    ts = _conv_a_tile(cfg)
    tpb = cfg.S // ts
    nt = cfg.T // ts
    lead = HALO - (KW - 1)
    ext = ts + HALO

    def body(z_ref, zp_ref, zn_ref, d_ref, dn_ref, w_ref, cb_ref, g_ref, b_ref,
             dz_ref, dw_ref, dcb_ref, dg_ref, db_ref, a0_s, d3_s, da1_s, a0_sh, da1_sh):
        i = pl.program_id(0)
        first, last = i % tpb == 0, i % tpb == tpb - 1

        @pl.when(i == 0)
        def _():
            dw_ref[...] = jnp.zeros_like(dw_ref)
            dcb_ref[...] = jnp.zeros_like(dcb_ref)
            dg_ref[...] = jnp.zeros_like(dg_ref)
            db_ref[...] = jnp.zeros_like(db_ref)

        a0_s[0:HALO, :] = jnp.where(first, 0.0, zp_ref[:, :C] * _sigmoid(zp_ref[:, C:]))
        a0_s[HALO:HALO + ts, :] = z_ref[:, :C] * _sigmoid(z_ref[:, C:])
        a0_s[HALO + ts:HALO + ext, :] = zn_ref[:, :C] * _sigmoid(zn_ref[:, C:])
        d3_s[0:ts, :] = d_ref[...]
        d3_s[ts:ext, :] = jnp.where(last, 0.0, dn_ref[...])
        _shift_copies(a0_s, a0_sh)
        dcb, dg, db = jnp.zeros((1, C), F32), jnp.zeros((1, C), F32), jnp.zeros((1, C), F32)
        for r0 in range(0, ext, CH):
            acc = jnp.zeros((CH, C), F32)
            for k in range(KW):
                acc = acc + w_ref[k:k + 1, :] * _rows(a0_s, a0_sh, r0 + lead + k, CH)
            xh, rstd = _ln_stats(acc + cb_ref[...])
            a2 = xh * g_ref[...] + b_ref[...]
            sg = _sigmoid(a2)
            da2 = d3_s[r0:r0 + CH, :] * (sg * (1.0 + a2 * (1.0 - sg)))
            da1 = _ln_bwd(da2 * g_ref[...], xh, rstd)
            da1_s[r0:r0 + CH, :] = da1
            if r0 < ts:
                dg = dg + jnp.sum(da2 * xh, axis=0, keepdims=True)
                db = db + jnp.sum(da2, axis=0, keepdims=True)
                dcb = dcb + jnp.sum(da1, axis=0, keepdims=True)
        dg_ref[...] += dg
        db_ref[...] += db
        dcb_ref[...] += dcb
        _shift_copies(da1_s, da1_sh)
        for k in range(KW):
            dwk = jnp.zeros((CH, C), F32)
            for r0 in range(0, ts, CH):
                dwk = dwk + da1_s[r0:r0 + CH, :] * _rows(a0_s, a0_sh, r0 + lead + k, CH)
            dw_ref[k:k + 1, :] += jnp.sum(dwk, axis=0, keepdims=True)
        for r0 in range(0, ts, CH):
            da0 = jnp.zeros((CH, C), F32)
            for k in range(KW):
                da0 = da0 + w_ref[k:k + 1, :] * _rows(da1_s, da1_sh, r0 + KW - 1 - k, CH)
            val, sg = z_ref[r0:r0 + CH, :C], _sigmoid(z_ref[r0:r0 + CH, C:])
            dz_ref[r0:r0 + CH, :C] = (da0 * sg).astype(BF16)
            dz_ref[r0:r0 + CH, C:] = (da0 * val * sg * (1.0 - sg)).astype(BF16)

    hb = ts // HALO
    nhb = cfg.T // HALO
    vec = pl.BlockSpec((1, C), lambda i: (0, 0))
    vs = jax.ShapeDtypeStruct((1, C), F32)
    return _pcall(body, name=name, grid=(nt,),
                  in_specs=[pl.BlockSpec((ts, 2 * C), lambda i: (i, 0)),
                            pl.BlockSpec((HALO, 2 * C), lambda i: (jnp.maximum(i * hb - 1, 0), 0)),
                            pl.BlockSpec((HALO, 2 * C), lambda i: (jnp.minimum((i + 1) * hb, nhb - 1), 0)),
                            pl.BlockSpec((ts, C), lambda i: (i, 0)),
                            pl.BlockSpec((HALO, C), lambda i: (jnp.minimum((i + 1) * hb, nhb - 1), 0)),
                            pl.BlockSpec((32, C), lambda i: (0, 0)), vec, vec, vec],
                  out_specs=[pl.BlockSpec((ts, 2 * C), lambda i: (i, 0)), pl.BlockSpec((32, C), lambda i: (0, 0)), vec, vec, vec],
                  out_shape=[jax.ShapeDtypeStruct((cfg.T, 2 * C), BF16), jax.ShapeDtypeStruct((32, C), F32), vs, vs, vs],
                  scratch_shapes=[pltpu.VMEM((HALO + ext, C), F32), pltpu.VMEM((ext, C), F32), pltpu.VMEM((ext, C), F32),
                                  pltpu.VMEM((SUBLANES - 1, HALO + ext - SUBLANES, C), F32),
                                  pltpu.VMEM((SUBLANES - 1, ext - SUBLANES, C), F32)],
                  compiler_params=_params("arbitrary"))(zm, zm, zm, da3, da3, w, cb, g, b)


def _cum_tile(cfg):
    return _pick(cfg.S, (256, 128, 64, 32, 16, 8))


def _fgate_fwd(zf, cfg, *, name):
    tc = _cum_tile(cfg)
    tpb = cfg.S // tc
    hp = _attn_tiles(cfg)[2]
    nb = cfg.H // hp

    def body(z_ref, o_ref, carry):
        @pl.when(pl.program_id(0) % tpb == 0)
        def _():
            carry[...] = jnp.zeros_like(carry)

        z = z_ref[...]
        logf = jnp.minimum(z, 0.0) - jnp.log(1.0 + jnp.exp(-jnp.abs(z)))
        tri = (lax.broadcasted_iota(jnp.int32, (tc, tc), 0) >= lax.broadcasted_iota(jnp.int32, (tc, tc), 1)).astype(F32)
        cum = jnp.dot(tri, logf, precision=lax.Precision.HIGHEST, preferred_element_type=F32) + carry[...]
        carry[...] = cum[tc - 1:tc, :]
        o_ref[0] = cum
        for b in range(1, nb):
            o_ref[b] = pltpu.roll(cum, LANES - hp * b, axis=1)

    return _pcall(body, name=name, grid=(cfg.T // tc,), in_specs=[pl.BlockSpec((tc, LANES), lambda i: (i, 0))],
                  out_specs=pl.BlockSpec((nb, tc, LANES), lambda i: (0, i, 0)),
                  out_shape=jax.ShapeDtypeStruct((nb, cfg.T, LANES), F32), scratch_shapes=[pltpu.VMEM((1, LANES), F32)],
                  compiler_params=_params("arbitrary"))(zf)


def _fgate_bwd(dcum_c, zf, cfg, *, name):
    tc = _cum_tile(cfg)
    tpb = cfg.S // tc
    nt = cfg.T // tc
    hp = _attn_tiles(cfg)[2]
    nb = cfg.H // hp

    def body(d_ref, z_ref, o_ref, carry):
        @pl.when(pl.program_id(0) % tpb == 0)
        def _():
            carry[...] = jnp.zeros_like(carry)

        d = d_ref[0]
        for b in range(1, nb):
            d = d + pltpu.roll(d_ref[b], hp * b, axis=1)
        tri = (lax.broadcasted_iota(jnp.int32, (tc, tc), 0) <= lax.broadcasted_iota(jnp.int32, (tc, tc), 1)).astype(F32)
        suf = jnp.dot(tri, d, precision=lax.Precision.HIGHEST, preferred_element_type=F32) + carry[...]
        o_ref[...] = (suf * _sigmoid(-z_ref[...])).astype(BF16)
        carry[...] = suf[0:1, :]

    blk = pl.BlockSpec((tc, LANES), lambda i: (nt - 1 - i, 0))
    return _pcall(body, name=name, grid=(nt,), in_specs=[pl.BlockSpec((nb, tc, LANES), lambda i: (0, nt - 1 - i, 0)), blk],
                  out_specs=blk, out_shape=jax.ShapeDtypeStruct((cfg.T, LANES), BF16),
                  scratch_shapes=[pltpu.VMEM((1, LANES), F32)], compiler_params=_params("arbitrary"))(dcum_c, zf)


def _attn_tiles(cfg):
    assert LANES % cfg.Dh == 0 and cfg.H % (LANES // cfg.Dh) == 0
    tk = _pick(cfg.S, (256, 128))
    tq = _pick(cfg.S, (2 * tk, tk))
    return tq, tk, LANES // cfg.Dh


BIAS_LANES = 3


def _head_lanes(hd, cfg, hp):
    li = lax.broadcasted_iota(jnp.int32, (1, LANES), 1)
    own = (li >= hd * cfg.Dh) & (li < (hd + 1) * cfg.Dh)
    return own, li, ((hd + 1) % hp) * cfg.Dh


def _q_aug(q, hd, cfg, hp):
    own, li, b0 = _head_lanes(hd, cfg, hp)
    ones = ((li >= b0) & (li < b0 + BIAS_LANES)).astype(F32)
    return jnp.where(own, q * cfg.Dh ** -0.5, ones).astype(BF16)


def _k_aug(k, ck, hd, cfg, hp):
    own, li, b0 = _head_lanes(hd, cfg, hp)
    hi = ck.astype(BF16).astype(F32)
    mid = (ck - hi).astype(BF16).astype(F32)
    lo = ck - hi - mid
    bias = jnp.where(li == b0, -hi, jnp.where(li == b0 + 1, -mid, jnp.where(li == b0 + 2, -lo, 0.0)))
    return jnp.where(own, k, bias).astype(BF16)


def _attn_fwd(zm, cum_c, cfg, *, name):
    S, Dh = cfg.S, cfg.Dh
    tq, tk, hp = _attn_tiles(cfg)
    assert hp >= 2
    nq, nb, per = S // tq, cfg.H // hp, tq // tk
    qb, kb, vb = cfg.q_off // LANES, (cfg.q_off + cfg.AW) // LANES, (cfg.q_off + 2 * cfg.AW) // LANES

    def body(q_ref, k_ref, v_ref, cc_ref, o_ref, o32_ref, lse_ref, ka_s, vt_s):
        qi = pl.program_id(2)

        @pl.when(qi == 0)
        def _():
            def prep(c, _):
                r = pl.multiple_of(c * tk, tk)
                kc = k_ref[pl.ds(r, tk), :]
                for hd in range(hp):
                    ka_s[hd, pl.ds(r, tk), :] = _k_aug(kc, cc_ref[0, pl.ds(r, tk), hd:hd + 1], hd, cfg, hp)
                vt_s[:, pl.ds(r, tk)] = v_ref[pl.ds(r, tk), :].T.astype(BF16)
                return 0

            lax.fori_loop(0, S // tk, prep, 0)

        key_i = lax.broadcasted_iota(jnp.int32, (tk, tq), 0)
        qry_i = lax.broadcasted_iota(jnp.int32, (tk, tq), 1)
        qf = q_ref[...]
        qa = [_q_aug(qf, hd, cfg, hp) for hd in range(hp)]

        def chunk(j, carry, diag=None):
            r = pl.multiple_of(j * tk, tk)
            new = []
            for hd in range(hp):
                m, l, acc = carry[hd]
                s = lax.dot_general(ka_s[hd, pl.ds(r, tk), :], qa[hd], NT, preferred_element_type=F32)
                if diag is not None:
                    s = jnp.where(key_i + diag * tk <= qry_i, s, NEG)
                m_new = jnp.maximum(m, jnp.max(s, axis=0, keepdims=True))
                a = jnp.exp(m - m_new)
                p = jnp.exp(s - m_new)
                l = a * l + jnp.sum(p, axis=0, keepdims=True)
                p_hi = p.astype(BF16)
                p_lo = (p - p_hi.astype(F32)).astype(BF16)
                vt = vt_s[hd * Dh:(hd + 1) * Dh, pl.ds(r, tk)]
                acc = a * acc + (jnp.dot(vt, p_hi, preferred_element_type=F32) + jnp.dot(vt, p_lo, preferred_element_type=F32))
                new.append((m_new, l, acc))
            return tuple(new)

        init = tuple((jnp.full((1, tq), NEG, F32), jnp.zeros((1, tq), F32), jnp.zeros((Dh, tq), F32)) for _ in range(hp))
        res = lax.fori_loop(0, qi * per, chunk, init)
        for d in range(per):
            res = chunk(qi * per + d, res, diag=d)
        o = jnp.concatenate([acc / l for _, l, acc in res], axis=0).T
        o_ref[...] = o.astype(BF16)
        o32_ref[...] = o
        lse_ref[...] = jnp.zeros_like(lse_ref)
        for hd in range(hp):
            lse_ref[0, 0, hd:hd + 1, :] = res[hd][0] + jnp.log(res[hd][1])

    return _pcall(body, name=name, grid=(cfg.Bl, nb, nq),
                  in_specs=[pl.BlockSpec((tq, LANES), lambda b, h, i: (b * nq + i, qb + h)),
                            pl.BlockSpec((S, LANES), lambda b, h, i: (b, kb + h)),
                            pl.BlockSpec((S, LANES), lambda b, h, i: (b, vb + h)),
                            pl.BlockSpec((1, S, LANES), lambda b, h, i: (h, b, 0))],
                  out_specs=[pl.BlockSpec((tq, LANES), lambda b, h, i: (b * nq + i, h)),
                             pl.BlockSpec((tq, LANES), lambda b, h, i: (b * nq + i, h)),
                             pl.BlockSpec((1, 1, 8, tq), lambda b, h, i: (b, h, 0, i))],
                  out_shape=[jax.ShapeDtypeStruct((cfg.T, cfg.AW), BF16), jax.ShapeDtypeStruct((cfg.T, cfg.AW), F32),
                             jax.ShapeDtypeStruct((cfg.Bl, nb, 8, S), F32)],
                  scratch_shapes=[pltpu.VMEM((hp, S, LANES), BF16), pltpu.VMEM((LANES, S), BF16)],
                  compiler_params=_params("parallel", "parallel", "arbitrary"))(zm, zm, zm, cum_c)


def _attn_bwd(zm, cum_c, o, do, lse, cfg, *, name):
    S, Dh = cfg.S, cfg.Dh
    tq, t, hp = _attn_tiles(cfg)
    nq, nk, nb, per = S // tq, S // t, cfg.H // hp, tq // t
    qb, kb, vb = cfg.q_off // LANES, (cfg.q_off + cfg.AW) // LANES, (cfg.q_off + 2 * cfg.AW) // LANES
    scale = Dh ** -0.5

    def body(q_ref, k_ref, v_ref, cc_ref, o_ref, do_ref, lse_ref, dq_ref, dk_ref, dv_ref, dcc_ref,
             ka_s, qa_s, vz_s, kt_s, dd_s, dqt_s):
        li = lax.broadcasted_iota(jnp.int32, (1, LANES), 1)
        ri = lax.broadcasted_iota(jnp.int32, (LANES, 1), 0)
        key_i = lax.broadcasted_iota(jnp.int32, (t, tq), 0)
        qry_i = lax.broadcasted_iota(jnp.int32, (t, tq), 1)

        def prep(c, _):
            r = pl.multiple_of(c * t, t)
            kc, vc, qc = k_ref[pl.ds(r, t), :], v_ref[pl.ds(r, t), :], q_ref[pl.ds(r, t), :]
            prod_t = (do_ref[pl.ds(r, t), :].astype(F32) * o_ref[pl.ds(r, t), :].astype(F32)).T
            for hd in range(hp):
                own = _head_lanes(hd, cfg, hp)[0]
                ka_s[hd, pl.ds(r, t), :] = _k_aug(kc, cc_ref[0, pl.ds(r, t), hd:hd + 1], hd, cfg, hp)
                qa_s[hd, pl.ds(r, t), :] = _q_aug(qc, hd, cfg, hp)
                vz_s[hd, pl.ds(r, t), :] = jnp.where(own, vc, 0.0).astype(BF16)
                dd_s[hd:hd + 1, pl.ds(r, t)] = jnp.sum(prod_t[hd * Dh:(hd + 1) * Dh, :], axis=0, keepdims=True)
            kt_s[:, pl.ds(r, t)] = kc.T.astype(BF16)
            dqt_s[:, pl.ds(r, t)] = jnp.zeros((LANES, t), F32)
            return 0

        lax.fori_loop(0, nk, prep, 0)

        def kv_step(j, _):
            rk = pl.multiple_of(j * t, t)
            i0 = j // per

            def tile(i, carry, masked):
                rq = pl.multiple_of(i * tq, tq)
                dob = do_ref[pl.ds(rq, tq), :]
                new, dq_t = [], None
                for hd in range(hp):
                    dk_h, dv_h, dsum_h = carry[hd]
                    qa = qa_s[hd, pl.ds(rq, tq), :]
                    s = lax.dot_general(ka_s[hd, pl.ds(rk, t), :], qa, NT, preferred_element_type=F32)
                    p = jnp.exp(s - lse_ref[0, 0, hd:hd + 1, pl.ds(rq, tq)])
                    if masked:
                        p = jnp.where(key_i + (rk - rq) <= qry_i, p, 0.0)
                    dp = lax.dot_general(vz_s[hd, pl.ds(rk, t), :], dob, NT, preferred_element_type=F32)
                    ds = p * (dp - dd_s[hd:hd + 1, pl.ds(rq, tq)])
                    dsb = ds.astype(BF16)
                    dv_h = dv_h + jnp.dot(p.astype(BF16), dob, preferred_element_type=F32)
                    dk_h = dk_h + jnp.dot(dsb, qa, preferred_element_type=F32)
                    dq_h = jnp.dot(kt_s[:, pl.ds(rk, t)], dsb, preferred_element_type=F32)
                    dq_t = dq_h if hd == 0 else jnp.where((ri >= hd * Dh) & (ri < (hd + 1) * Dh), dq_h, dq_t)
                    for c0 in range(0, tq, LANES):
                        dsum_h = dsum_h + ds[:, c0:c0 + LANES]
                    new.append((dk_h, dv_h, dsum_h))
                dqt_s[:, pl.ds(rq, tq)] += dq_t * scale
                return tuple(new)

            zero = tuple((jnp.zeros((t, LANES), F32),) * 3 for _ in range(hp))
            res = lax.fori_loop(i0 + 1, nq, functools.partial(tile, masked=False), tile(i0, zero, True))
            dk, dv, dcc = res[0][0], res[0][1], jnp.zeros((t, LANES), F32)
            for hd in range(hp):
                own = _head_lanes(hd, cfg, hp)[0]
                if hd > 0:
                    dk, dv = jnp.where(own, res[hd][0], dk), jnp.where(own, res[hd][1], dv)
                dcc = dcc + jnp.where(li == hd, -jnp.sum(res[hd][2], axis=1, keepdims=True), 0.0)
            dk_ref[pl.ds(rk, t), :] = dk.astype(BF16)
            dv_ref[pl.ds(rk, t), :] = dv.astype(BF16)
            dcc_ref[0, pl.ds(rk, t), :] = dcc
            return 0

        lax.fori_loop(0, nk, kv_step, 0)

        def finish(c, _):
            r = pl.multiple_of(c * t, t)
            dq_ref[pl.ds(r, t), :] = dqt_s[:, pl.ds(r, t)].T.astype(BF16)
            return 0

        lax.fori_loop(0, nk, finish, 0)

    blk = pl.BlockSpec((S, LANES), lambda b, h: (b, h))
    cc = pl.BlockSpec((1, S, LANES), lambda b, h: (h, b, 0))
    os_ = jax.ShapeDtypeStruct((cfg.T, cfg.AW), BF16)
    return _pcall(body, name=name, grid=(cfg.Bl, nb),
                  in_specs=[pl.BlockSpec((S, LANES), lambda b, h: (b, qb + h)), pl.BlockSpec((S, LANES), lambda b, h: (b, kb + h)),
                            pl.BlockSpec((S, LANES), lambda b, h: (b, vb + h)), cc, blk, blk,
                            pl.BlockSpec((1, 1, 8, S), lambda b, h: (b, h, 0, 0))],
                  out_specs=[blk, blk, blk, cc],
                  out_shape=[os_, os_, os_, jax.ShapeDtypeStruct((nb, cfg.T, LANES), F32)],
                  scratch_shapes=[pltpu.VMEM((hp, S, LANES), BF16)] * 3 + [pltpu.VMEM((LANES, S), BF16),
                                  pltpu.VMEM((8, S), F32), pltpu.VMEM((LANES, S), F32)],
                  compiler_params=_params("parallel", "parallel"))(zm, zm, zm, cum_c, o, do, lse)


FFN_HALO = 8
FFN_CHUNK = 16


def _ffn_tiles(cfg):
    assert cfg.KF - 1 <= FFN_HALO
    return _pick(cfg.S, (512, 256, 128, 64, 32, 16, 8)), _pick(cfg.F, (256, 128))


def _gelu(x):
    return 0.5 * x * (1.0 + lax.erf(x * (2.0 ** -0.5)))


def _gelu_grad(x):
    return 0.5 * (1.0 + lax.erf(x * (2.0 ** -0.5))) + x * jnp.exp(-0.5 * x * x) * ((2.0 * math.pi) ** -0.5)


def _ffn_conv_fwd(h0, w, cb, cfg, *, name):
    KF, HALO = cfg.KF, FFN_HALO
    ts, tf = _ffn_tiles(cfg)
    tpb, nf = cfg.S // ts, cfg.F // tf
    lead = HALO - (KF - 1)

    CH = FFN_CHUNK

    def body(g_ref, gp_ref, l_ref, lp_ref, wg_ref, wl_ref, cg_ref, cl_ref, o_ref, g_s, l_s):
        first = pl.program_id(1) % tpb == 0
        for s, main, prev in ((g_s, g_ref, gp_ref), (l_s, l_ref, lp_ref)):
            s[0:HALO, :] = jnp.where(first, 0.0, prev[...])
            s[HALO:HALO + CH, :] = main[0:CH, :]
        wg, wl = [wg_ref[k:k + 1, :] for k in range(KF)], [wl_ref[k:k + 1, :] for k in range(KF)]
        for r0 in range(0, ts, CH):
            hg, hl = cg_ref[...], cl_ref[...]
            for k in range(KF):
                if r0 == 0:
                    xg, xl = g_s[lead + k:lead + k + CH, :], l_s[lead + k:lead + k + CH, :]
                else:
                    a = r0 - (KF - 1) + k
                    xg, xl = g_ref[a:a + CH, :], l_ref[a:a + CH, :]
                hg, hl = hg + wg[k] * xg, hl + wl[k] * xl
            o_ref[r0:r0 + CH, :] = (_gelu(hg) * hl).astype(BF16)

    hb = ts // HALO
    prev = lambda off: pl.BlockSpec((HALO, tf), lambda j, i: (jnp.maximum(i * hb - 1, 0), off + j))
    main = lambda off: pl.BlockSpec((ts, tf), lambda j, i: (i, off + j))
    wsp = lambda off: pl.BlockSpec((8, tf), lambda j, i: (0, off + j))
    vsp = lambda off: pl.BlockSpec((1, tf), lambda j, i: (0, off + j))
    return _pcall(body, name=name, grid=(nf, cfg.T // ts),
                  in_specs=[main(0), prev(0), main(nf), prev(nf), wsp(0), wsp(nf), vsp(0), vsp(nf)],
                  out_specs=pl.BlockSpec((ts, tf), lambda j, i: (i, j)), out_shape=jax.ShapeDtypeStruct((cfg.T, cfg.F), BF16),
                  scratch_shapes=[pltpu.VMEM((HALO + CH, tf), F32)] * 2,
                  compiler_params=_params("parallel", "parallel"))(h0, h0, h0, h0, w, w, cb, cb)


def _ffn_conv_bwd(df, h0, w, cb, cfg, *, name):
    KF, HALO = cfg.KF, FFN_HALO
    ts, tf = _ffn_tiles(cfg)
    tpb, nf = cfg.S // ts, cfg.F // tf
    lead = HALO - (KF - 1)
    ext = ts + HALO

    CH = FFN_CHUNK

    def body(g_ref, gp_ref, gn_ref, l_ref, lp_ref, ln_ref, d_ref, dn_ref, wg_ref, wl_ref, cg_ref, cl_ref,
             dg_ref, dl_ref, dwg_ref, dwl_ref, dcg_ref, dcl_ref, gh_s, lh_s, gt_s, lt_s, dhg_s, dhl_s):
        i = pl.program_id(1)
        first, last = i % tpb == 0, i % tpb == tpb - 1

        @pl.when(i == 0)
        def _():
            dwg_ref[...] = jnp.zeros_like(dwg_ref)
            dwl_ref[...] = jnp.zeros_like(dwl_ref)
            dcg_ref[...] = jnp.zeros_like(dcg_ref)
            dcl_ref[...] = jnp.zeros_like(dcl_ref)

        for head, tail, main, prev, nxt in ((gh_s, gt_s, g_ref, gp_ref, gn_ref), (lh_s, lt_s, l_ref, lp_ref, ln_ref)):
            head[0:HALO, :] = jnp.where(first, 0.0, prev[...])
            head[HALO:HALO + CH, :] = main[0:CH, :]
            tail[0:HALO, :] = main[ts - HALO:ts, :]
            tail[HALO:2 * HALO, :] = nxt[...]
        wg, wl = [wg_ref[k:k + 1, :] for k in range(KF)], [wl_ref[k:k + 1, :] for k in range(KF)]

        def grads(hg, hl, d):
            return d * hl * _gelu_grad(hg), d * _gelu(hg)

        for r0 in range(0, ts, CH):
            hg, hl = cg_ref[...], cl_ref[...]
            for k in range(KF):
                if r0 == 0:
                    xg, xl = gh_s[lead + k:lead + k + CH, :], lh_s[lead + k:lead + k + CH, :]
                else:
                    a = r0 - (KF - 1) + k
                    xg, xl = g_ref[a:a + CH, :], l_ref[a:a + CH, :]
                hg, hl = hg + wg[k] * xg, hl + wl[k] * xl
            dhg_s[r0:r0 + CH, :], dhl_s[r0:r0 + CH, :] = grads(hg, hl, d_ref[r0:r0 + CH, :])
        hg, hl = cg_ref[...], cl_ref[...]
        for k in range(KF):
            hg, hl = hg + wg[k] * gt_s[lead + k:lead + k + HALO, :], hl + wl[k] * lt_s[lead + k:lead + k + HALO, :]
        dhg_s[ts:ext, :], dhl_s[ts:ext, :] = grads(hg, hl, jnp.where(last, 0.0, dn_ref[...]))

        for dh_s, x_ref, wk, dx_ref, dw_ref, dc_ref in ((dhg_s, g_ref, wg, dg_ref, dwg_ref, dcg_ref),
                                                        (dhl_s, l_ref, wl, dl_ref, dwl_ref, dcl_ref)):
            dw_acc = [jnp.zeros((CH, tf), F32) for _ in range(KF)]
            for r0 in range(0, ts, CH):
                x = x_ref[r0:r0 + CH, :]
                dx = jnp.zeros((CH, tf), F32)
                for k in range(KF):
                    dhk = dh_s[r0 + KF - 1 - k:r0 + KF - 1 - k + CH, :]
                    dx = dx + wk[k] * dhk
                    dw_acc[k] = dw_acc[k] + x * dhk
                    if k == KF - 1:
                        dc_acc = dhk if r0 == 0 else dc_acc + dhk
                dx_ref[r0:r0 + CH, :] = dx.astype(BF16)
            for k in range(KF):
                dw_ref[k:k + 1, :] += jnp.sum(dw_acc[k], axis=0, keepdims=True)
            dc_ref[...] += jnp.sum(dc_acc, axis=0, keepdims=True)

    hb = ts // HALO
    nhb = cfg.T // HALO
    main = lambda off: pl.BlockSpec((ts, tf), lambda j, i: (i, off + j))
    prev = lambda off: pl.BlockSpec((HALO, tf), lambda j, i: (jnp.maximum(i * hb - 1, 0), off + j))
    nxt = lambda off: pl.BlockSpec((HALO, tf), lambda j, i: (jnp.minimum((i + 1) * hb, nhb - 1), off + j))
    wsp = lambda off: pl.BlockSpec((8, tf), lambda j, i: (0, off + j))
    vsp = lambda off: pl.BlockSpec((1, tf), lambda j, i: (0, off + j))
    dxs, dws, dcs = (jax.ShapeDtypeStruct((cfg.T, cfg.F), BF16), jax.ShapeDtypeStruct((8, cfg.F), F32),
                     jax.ShapeDtypeStruct((1, cfg.F), F32))
    return _pcall(body, name=name, grid=(nf, cfg.T // ts),
                  in_specs=[main(0), prev(0), nxt(0), main(nf), prev(nf), nxt(nf), main(0), nxt(0),
                            wsp(0), wsp(nf), vsp(0), vsp(nf)],
                  out_specs=[main(0), main(0), wsp(0), wsp(0), vsp(0), vsp(0)],
                  out_shape=[dxs, dxs, dws, dws, dcs, dcs],
                  scratch_shapes=[pltpu.VMEM((HALO + CH, tf), F32)] * 2 + [pltpu.VMEM((2 * HALO, tf), F32)] * 2
                  + [pltpu.VMEM((ext, tf), F32)] * 2,
                  compiler_params=_params("parallel", "arbitrary"))(h0, h0, h0, h0, h0, h0, df, df, w, w, cb, cb)


def _ada_fwd(c_all, w, b, *, name):
    L, D, n = w.shape
    B = c_all.shape[0]

    def body(c_ref, w_ref, b_ref, o_ref):
        c = c_ref[...]
        act = (c * _sigmoid(c)).astype(BF16)
        o_ref[0] = jnp.dot(act, w_ref[0].astype(BF16), preferred_element_type=F32) + b_ref[0]

    return _pcall(body, name=name, grid=(L,),
                  in_specs=[pl.BlockSpec((B, D), lambda l: (0, 0)), pl.BlockSpec((1, D, n), lambda l: (l, 0, 0)),
                            pl.BlockSpec((1, 1, n), lambda l: (l, 0, 0))],
                  out_specs=pl.BlockSpec((1, B, n), lambda l: (l, 0, 0)), out_shape=jax.ShapeDtypeStruct((L, B, n), F32),
                  compiler_params=_params("parallel"))(c_all, w, b)


def _ada_bwd(c_all, dmod, *, name):
    L, B, n = dmod.shape
    D = c_all.shape[1]

    def body(c_ref, d_ref, o_ref):
        c = c_ref[...]
        act = (c * _sigmoid(c)).astype(BF16)
        o_ref[0] = lax.dot_general(act, d_ref[0].astype(BF16), TN, preferred_element_type=F32)

    return _pcall(body, name=name, grid=(L,),
                  in_specs=[pl.BlockSpec((B, D), lambda l: (0, 0)), pl.BlockSpec((1, B, n), lambda l: (l, 0, 0))],
                  out_specs=pl.BlockSpec((1, D, n), lambda l: (l, 0, 0)), out_shape=jax.ShapeDtypeStruct((L, D, n), F32),
                  compiler_params=_params("parallel"))(c_all, dmod)


def _slot_sum(x, *, name):
    n, R, W = x.shape
    tr = _pick(R, (256, 128, 64, 32, 16, 8))

    def body(x_ref, o_ref):
        acc = x_ref[0].astype(F32)
        for k in range(1, n):
            acc = acc + x_ref[k].astype(F32)
        o_ref[...] = acc

    return _pcall(body, name=name, grid=(R // tr,), in_specs=[pl.BlockSpec((n, tr, W), lambda i: (0, i, 0))],
                  out_specs=pl.BlockSpec((tr, W), lambda i: (i, 0)), out_shape=jax.ShapeDtypeStruct((R, W), F32),
                  compiler_params=_params("parallel"))(x)


def _adamw_math(g, w, m, v):
    c1, c2 = 1.0 - ADAM_B1 ** ADAM_STEP, 1.0 - ADAM_B2 ** ADAM_STEP
    m2 = ADAM_B1 * m + (1.0 - ADAM_B1) * g
    v2 = ADAM_B2 * v + (1.0 - ADAM_B2) * (g * g)
    return -ADAM_LR * ((m2 / c1) / (jnp.sqrt(v2 / c2) + ADAM_EPS) + ADAM_WD * w), m2, v2


def _adamw_many(gs, ws, ms, vs, *, name):
    n = len(gs)

    def body(*refs):
        ins, outs = refs[:4 * n], refs[4 * n:]
        for i in range(n):
            d, m2, v2 = _adamw_math(*(ins[j * n + i][...] for j in range(4)))
            outs[i][...], outs[n + i][...], outs[2 * n + i][...] = d, m2, v2

    vm = pl.BlockSpec(memory_space=pltpu.VMEM)
    outs = _pcall(body, name=name, in_specs=[vm] * (4 * n), out_specs=[vm] * (3 * n),
                  out_shape=[jax.ShapeDtypeStruct(a.shape, F32) for _ in range(3) for a in ws],
                  compiler_params=pltpu.CompilerParams(vmem_limit_bytes=VMEM_LIMIT))(*gs, *ws, *ms, *vs)
    return outs[:n], outs[n:2 * n], outs[2 * n:]


def _adamw(gs, w, m, v, *, name):
    n, R, W = gs.shape
    tr = _pick(R, (256, 128, 64, 32, 16, 8))

    def body(g_ref, w_ref, m_ref, v_ref, go_ref, d_ref, mo_ref, vo_ref):
        g = g_ref[0].astype(F32)
        for k in range(1, n):
            g = g + g_ref[k].astype(F32)
        go_ref[...] = g
        d_ref[...], mo_ref[...], vo_ref[...] = _adamw_math(g, w_ref[...], m_ref[...], v_ref[...])

    blk = pl.BlockSpec((tr, W), lambda i: (i, 0))
    o = jax.ShapeDtypeStruct((R, W), F32)
    return _pcall(body, name=name, grid=(R // tr,), in_specs=[pl.BlockSpec((n, tr, W), lambda i: (0, i, 0)), blk, blk, blk],
                  out_specs=[blk] * 4, out_shape=[o] * 4, compiler_params=_params("parallel"))(gs, w, m, v)


def _peer_copies(x_ref, land_ref, send_sems, recv_sems, all_to_all):
    mx, my, mc = lax.axis_index("x"), lax.axis_index("y"), lax.axis_index("c")
    me = 4 * mx + 2 * my + mc
    copies = []
    for k in range(1, N_DEV):
        px, py, pc = mx ^ ((k >> 2) & 1), my ^ ((k >> 1) & 1), mc ^ (k & 1)
        copies.append(pltpu.make_async_remote_copy(
            src_ref=x_ref.at[4 * px + 2 * py + pc] if all_to_all else x_ref, dst_ref=land_ref.at[me],
            send_sem=send_sems.at[k - 1], recv_sem=recv_sems.at[k - 1], device_id=(px, py, pc),
            device_id_type=pl.DeviceIdType.MESH))
    return copies


def _gather_two_level(x, *, name, after=None):
    def body(x_ref, *rest):
        o_ref, send_sems, recv_sems, local_sem = rest[-4:]
        mx, my, mc = lax.axis_index("x"), lax.axis_index("y"), lax.axis_index("c")
        me, sibling = (mx, my, mc), (mx, my, 1 - mc)
        chips = [(1 - mx, my), (mx, 1 - my), (1 - mx, 1 - my)]

        def slot(px, py, pc):
            return o_ref.at[4 * px + 2 * py + pc]

        def copy(k, block, to, src=None):
            return pltpu.make_async_remote_copy(
                src_ref=slot(*block) if src is None else src, dst_ref=slot(*block), send_sem=send_sems.at[k],
                recv_sem=recv_sems.at[k], device_id=to, device_id_type=pl.DeviceIdType.MESH)

        mine = pltpu.make_async_copy(x_ref, slot(*me), local_sem)
        mine.start()
        first = [copy(0, me, sibling, src=x_ref)] + [copy(1 + j, me, (*chip, mc), src=x_ref) for j, chip in enumerate(chips)]
        for cp in first:
            cp.start()
        passed = [copy(4 + j, (*chip, mc), sibling) for j, chip in enumerate(chips)]
        for j, chip in enumerate(chips):
            copy(1 + j, (*chip, mc), me).wait_recv()
            passed[j].start()
        copy(0, sibling, me).wait_recv()
        for j, chip in enumerate(chips):
            copy(4 + j, (*chip, 1 - mc), me).wait_recv()
        for cp in first + passed:
            cp.wait_send()
        mine.wait()

    anyspec = pl.BlockSpec(memory_space=pl.ANY)
    args = [x] if after is None else [x, after]
    return _pcall(body, name=name, in_specs=[anyspec] * len(args), out_specs=anyspec,
                  out_shape=jax.ShapeDtypeStruct((N_DEV,) + tuple(x.shape), x.dtype),
                  scratch_shapes=[pltpu.SemaphoreType.DMA((N_DEV - 1,)), pltpu.SemaphoreType.DMA((N_DEV - 1,)),
                                  pltpu.SemaphoreType.DMA(())])(*args)


_HBM = pl.BlockSpec(memory_space=pltpu.HBM)
_SEM = pl.BlockSpec(memory_space=pltpu.SEMAPHORE)
_EFFECT = pltpu.SideEffectType.DATAFLOW_SIDE_EFFECTING


def _exchange_start(x, *, all_to_all, name, after=None):
    blk = x.shape[1:] if all_to_all else x.shape
    land = lax.empty((N_DEV,) + tuple(blk), x.dtype)
    has_after = after is not None

    def body(*refs):
        x_ref, land_ref = refs[0], refs[1]
        send_sems, recv_sems, _, _, token, local_sem = refs[2 + has_after:]
        me = 4 * lax.axis_index("x") + 2 * lax.axis_index("y") + lax.axis_index("c")
        mine = pltpu.make_async_copy(x_ref.at[me] if all_to_all else x_ref, land_ref.at[me], local_sem)
        mine.start()
        mine.wait()
        for cp in _peer_copies(x_ref, land_ref, send_sems, recv_sems, all_to_all):
            cp.start()
        token[...] = jnp.zeros_like(token)

    n_sem = pltpu.SemaphoreType.DMA((N_DEV - 1,))
    args = [pltpu.with_memory_space_constraint(x, pltpu.HBM), pltpu.with_memory_space_constraint(land, pltpu.HBM)]
    in_specs = [_HBM, _HBM]
    if has_after:
        args.append(after)
        in_specs.append(pl.BlockSpec(memory_space=pl.ANY))
    send_sems, recv_sems, x_thru, land_thru, token = _pcall(
        body, name=name, in_specs=in_specs,
        out_shape=(n_sem, n_sem, pltpu.HBM(x.shape, x.dtype), pltpu.HBM(land.shape, land.dtype),
                   jax.ShapeDtypeStruct((8, LANES), F32)),
        out_specs=(_SEM, _SEM, _HBM, _HBM, pl.BlockSpec(memory_space=pltpu.VMEM)), input_output_aliases={0: 2, 1: 3},
        scratch_shapes=[pltpu.SemaphoreType.DMA(())],
        compiler_params=pltpu.CompilerParams(has_side_effects=_EFFECT))(*args)
    return (send_sems, recv_sems, x_thru, land_thru, all_to_all), token


def _exchange_wait(state, after, *, name):
    send_sems, recv_sems, x_thru, land_thru, all_to_all = state

    def body(x_ref, land_ref, send_sems, recv_sems, after_ref, x_dead, landed):
        for cp in _peer_copies(x_ref, land_ref, send_sems, recv_sems, all_to_all):
            cp.wait_send()
            cp.wait_recv()

    return _pcall(
        body, name=name, in_specs=(_HBM, _HBM, _SEM, _SEM, pl.BlockSpec(memory_space=pl.ANY)),
        out_shape=(pltpu.HBM(x_thru.shape, x_thru.dtype), pltpu.HBM(land_thru.shape, land_thru.dtype)),
        out_specs=(_HBM, _HBM), input_output_aliases={0: 0, 1: 1},
        compiler_params=pltpu.CompilerParams(has_side_effects=_EFFECT))(x_thru, land_thru, send_sems, recv_sems, after)[1]


def _exchange(x, *, all_to_all, name, after=None):
    blk = x.shape[1:] if all_to_all else x.shape

    def body(x_ref, *rest):
        o_ref, send_sems, recv_sems, local_sem = rest[-4:]
        me = 4 * lax.axis_index("x") + 2 * lax.axis_index("y") + lax.axis_index("c")
        mine = pltpu.make_async_copy(x_ref.at[me] if all_to_all else x_ref, o_ref.at[me], local_sem)
        mine.start()
        copies = _peer_copies(x_ref, o_ref, send_sems, recv_sems, all_to_all)
        for cp in copies:
            cp.start()
        for cp in copies:
            cp.wait()
        mine.wait()

    anyspec = pl.BlockSpec(memory_space=pl.ANY)
    args = [x] if after is None else [x, after]
    return _pcall(body, name=name, in_specs=[anyspec] * len(args), out_specs=anyspec,
                  out_shape=jax.ShapeDtypeStruct((N_DEV,) + tuple(blk), x.dtype),
                  scratch_shapes=[pltpu.SemaphoreType.DMA((N_DEV - 1,)), pltpu.SemaphoreType.DMA((N_DEV - 1,)),
                                  pltpu.SemaphoreType.DMA(())])(*args)


PACK_ROWS = 16


def _pack(arrs, width, dtype, lead=0):
    parts, segs, r = [], [], 0
    for a in arrs:
        lshape, shape = a.shape[:lead], a.shape[lead:]
        n = math.prod(shape)
        rows = -(-n // width)
        rows_p = -(-rows // PACK_ROWS) * PACK_ROWS
        if n == rows * width:
            blk = a.reshape(lshape + (rows, width)).astype(dtype)
            parts.append(jnp.pad(blk, [(0, 0)] * lead + [(0, rows_p - rows), (0, 0)]) if rows_p > rows else blk)
        else:
            flat = jnp.pad(a.reshape(lshape + (n,)).astype(dtype), [(0, 0)] * lead + [(0, rows_p * width - n)])
            parts.append(flat.reshape(lshape + (rows_p, width)))
        segs.append((r, n, shape))
        r += rows_p
    return jnp.concatenate(parts, axis=lead), segs


def _unpack(p, segs):
    lshape, width = p.shape[:-2], p.shape[-1]
    outs = []
    for r, n, shape in segs:
        rows = -(-n // width)
        blk = p[..., r:r + rows, :]
        if n != rows * width:
            blk = blk.reshape(lshape + (rows * width,))[..., :n]
        outs.append(blk.reshape(lshape + shape))
    return outs


def _split_cols(a, f_off, h):
    return jnp.concatenate([a[..., :f_off], a[..., f_off + h:]], axis=-1), a[..., f_off:f_off + h]


def _merge_cols(main, f, f_off):
    return jnp.concatenate([main[..., :f_off], f, main[..., f_off:]], axis=-1)


def _pad_to(a, n, axis):
    pad = [(0, 0)] * a.ndim
    pad[axis] = (0, n - a.shape[axis])
    return jnp.pad(a, pad)


def kernel(x, c, w_ada, b_ada, w_in, b_in, conv_a_w, conv_a_b, ln_conv_g, ln_conv_b, w_conv_proj, w_attn_proj, w_mix_out, b_mix_out, ln1_g, ln1_b, w_ffn_up, ffn_conv_w, ffn_conv_b, w_ffn_down, ln2_g, ln2_b, loss_target, m_w_ada, m_b_ada, m_w_in, m_b_in, m_conv_a_w, m_conv_a_b, m_ln_conv_g, m_ln_conv_b, m_w_conv_proj, m_w_attn_proj, m_w_mix_out, m_b_mix_out, m_ln1_g, m_ln1_b, m_w_ffn_up, m_ffn_conv_w, m_ffn_conv_b, m_w_ffn_down, m_ln2_g, m_ln2_b, v_w_ada, v_b_ada, v_w_in, v_b_in, v_conv_a_w, v_conv_a_b, v_ln_conv_g, v_ln_conv_b, v_w_conv_proj, v_w_attn_proj, v_w_mix_out, v_b_mix_out, v_ln1_g, v_ln1_b, v_w_ffn_up, v_ffn_conv_w, v_ffn_conv_b, v_w_ffn_down, v_ln2_g, v_ln2_b):
    L, D = w_ada.shape[0], w_ada.shape[1]
    Bl, S, _ = x.shape
    C, KW, AW = conv_a_b.shape[1], conv_a_w.shape[1], w_attn_proj.shape[1]
    F, KF, n_in_all = ffn_conv_b.shape[1] // 2, ffn_conv_w.shape[1], b_in.shape[1]
    H = n_in_all - 2 * C - 3 * AW - 2 * D
    cfg = Cfg(L=L, Bl=Bl, S=S, D=D, C=C, KW=KW, H=H, Dh=AW // H, F=F, KF=KF)
    T, NM = cfg.T, cfg.NM
    f_off = 2 * C + 3 * AW
    n_ada = w_ada.shape[2]
    me = 4 * lax.axis_index("x") + 2 * lax.axis_index("y") + lax.axis_index("c")

    def my_cols(a, n):
        return lax.dynamic_slice_in_dim(a, me * n, n, axis=a.ndim - 1)

    spack, ssegs = _pack([c, conv_a_w, ffn_conv_w], D, F32)
    c_g, caw_g, fcw_g = _unpack(_exchange(spack, all_to_all=False, name="gather_small"), ssegs)
    c_all = c_g.reshape(N_DEV * Bl, D)
    caw = _pad_to(jnp.moveaxis(caw_g, 0, 2).reshape(L, KW, C), 32, 1)
    fcw = _pad_to(jnp.moveaxis(fcw_g, 0, 2).reshape(L, KF, 2 * F), 8, 1)

    mod_part = _ada_fwd(c_all, w_ada, my_cols(b_ada, n_ada)[:, None, :], name="ada_fwd")
    mod_send = jnp.moveaxis(mod_part.reshape(L, N_DEV, Bl, n_ada), 1, 0).reshape(N_DEV, L * Bl, n_ada)
    mod_recv = _exchange(mod_send, all_to_all=True, name="exchange_mod")
    mod = jnp.moveaxis(mod_recv.reshape(N_DEV, L, Bl, n_ada), 0, 2).reshape(L, Bl, 6, 1, D)
    shift1, scale1, gate1, shift2, scale2, gate2 = (mod[:, :, i] for i in range(6))

    big_names = ["w_in", "w_conv_proj", "w_attn_proj", "w_mix_out", "w_ffn_up", "w_ffn_down"]
    transposed = (True, True, True, False, True, False)

    def shard_items(arrs, grp):
        return [arrs[i][l].T if transposed[i] else arrs[i][l] for l, i in grp]

    W = [dict() for _ in range(L)]

    def set_weights(landed, segs, grp):
        for (l, i), a in zip(grp, _unpack(landed, segs)):
            a = a.reshape((-1, a.shape[-1]))
            if i == 0:
                wm_t, wf_t = _split_cols(a.T, f_off, H)
                bm, bf = _split_cols(b_in[l], f_off, H)
                W[l].update(wm_t=wm_t.T, wf_t=_pad_to(wf_t.T, LANES, 0), bm=bm[None], bf=_pad_to(bf, LANES, 0)[None])
            else:
                W[l][("w_cp_t", "w_ap_t", "w_mo", "w_up_t", "w_dn")[i - 1]] = a

    big_w = (w_in, w_conv_proj, w_attn_proj, w_mix_out, w_ffn_up, w_ffn_down)
    w_groups = [[(l, i) for i in range(6)] for l in range(L)]
    pack, segs = _pack(shard_items(big_w, w_groups[0]), D, BF16)
    landed0 = _gather_two_level(pack, name="gather_weights_0", after=mod_recv)
    set_weights(landed0, segs, w_groups[0])
    w_state, token = {}, landed0
    for l in range(1, L):
        pack, segs = _pack(shard_items(big_w, w_groups[l]), D, BF16)
        state, token = _exchange_start(pack, all_to_all=False, name=f"gather_weights_start_{l}", after=token)
        w_state[l] = (state, pack, segs)

    def wait_weights(l, after):
        state, pack, segs = w_state[l]
        set_weights(_exchange_wait(state, after, name=f"gather_weights_wait_{l}"), segs, w_groups[l])

    xf = x.reshape(T, D)
    u = _ln_mod_fwd(xf, shift1[0], scale1[0], cfg, name="ln_mod_fwd")
    saved = []
    xin = xf
    for l in range(L):
        w = W[l]
        if l > 0:
            wait_weights(l, u)
        zm = _matmul(u, w["wm_t"], mode="nt", bias=w["bm"], name=f"in_proj_{l}", after=token if l == 0 else None)
        zf = _matmul(u, w["wf_t"], mode="nt", bias=w["bf"], name=f"in_proj_f_{l}")
        a3 = _conv_a_fwd(zm, caw[l], conv_a_b[l][None], ln_conv_g[l][None], ln_conv_b[l][None], cfg, name=f"conv_a_fwd_{l}")
        cum_c = _fgate_fwd(zf, cfg, name=f"fgate_fwd_{l}")
        o, o32, lse = _attn_fwd(zm, cum_c, cfg, name=f"attn_fwd_{l}")
        ya =_matmul(a3, w["w_cp_t"], mode="nt", name=f"conv_proj_{l}")
        yb = _matmul(o, w["w_ap_t"], mode="nt", name=f"attn_proj_{l}")
        mg = _merge_fwd(zm, ya, yb, cfg, name=f"merge_fwd_{l}")
        mix = _matmul(mg, w["w_mo"], mode="nn", bias=b_mix_out[l][None], name=f"mix_out_{l}")
        x1, u2 = _res_ln_fwd(xin, mix, gate1[l], ln1_g[l][None], ln1_b[l][None], cfg, name=f"res_ln1_fwd_{l}",
                             nxt=(shift2[l], scale2[l]))
        h0 = _matmul(u2, w["w_up_t"], mode="nt", name=f"ffn_up_{l}")
        fa = _ffn_conv_fwd(h0, fcw[l], ffn_conv_b[l][None], cfg, name=f"ffn_conv_fwd_{l}")
        ffn = _matmul(fa, w["w_dn"], mode="nn", name=f"ffn_down_{l}")
        saved.append(dict(x=xin, u=u, zm=zm, zf=zf, a3=a3, cum_c=cum_c, o=o, o32=o32, lse=lse, ya=ya, yb=yb, mg=mg, mix=mix,
                          x1=x1, u2=u2, h0=h0, fa=fa, ffn=ffn))
        if l + 1 < L:
            xin, u = _res_ln_fwd(x1, ffn, gate2[l], ln2_g[l][None], ln2_b[l][None], cfg, name=f"res_ln2_fwd_{l}",
                                 nxt=(shift1[l + 1], scale1[l + 1]))
        else:
            xin = _res_ln_fwd(x1, ffn, gate2[l], ln2_g[l][None], ln2_b[l][None], cfg, name=f"res_ln2_fwd_{l}")

    dx, loss_tiles = _loss_grad(xin, loss_target.reshape(T, D), cfg, name="loss_grad")
    loss = lax.psum(0.5 / D * jnp.sum(loss_tiles[:, 0, 0]), ("x", "y", "c"))

    gbig = {}
    g_groups = [[(l, i) for i in range(6)] for l in reversed(range(1, L))] + [[(0, 4), (0, 5)], [(0, 1), (0, 2), (0, 3)], [(0, 0)]]
    g_state = []

    def start_grads(after=None):
        grp = g_groups[len(g_state)]
        send, segs = _pack([gbig[k].reshape((N_DEV, -1, gbig[k].shape[1])) for k in grp], D, BF16, lead=1)
        state, tok = _exchange_start(send, all_to_all=True, name=f"exchange_grads_start_{len(g_state)}", after=after)
        g_state.append((state, send, segs, grp))
        return tok

    gsm = [dict() for _ in range(L)]
    dmods = [None] * L
    token = None
    for l in reversed(range(L)):
        w, s = W[l], saved[l]
        dres2, dffn, dg2, db2, dgate2, _ = _res_ln_bwd(dx, s["x1"], s["ffn"], gate2[l], ln2_g[l][None], cfg, name=f"res_ln2_bwd_{l}")
        dfa = _matmul(dffn, w["w_dn"], mode="nt", name=f"d_ffn_act_{l}", after=token)
        gbig[l, 5] = _matmul(s["fa"], dffn, mode="tn", name=f"dw_ffn_down_{l}")
        dh0g, dh0l, dwg, dwl, dcg, dcl = _ffn_conv_bwd(dfa, s["h0"], fcw[l], ffn_conv_b[l][None], cfg, name=f"ffn_conv_bwd_{l}")
        dh0 = jnp.concatenate([dh0g, dh0l], axis=1)
        du2 = _matmul(dh0, w["w_up_t"], mode="nn", name=f"d_u2_{l}")
        gbig[l, 4] = _matmul(dh0, s["u2"], mode="tn", name=f"dw_ffn_up_{l}")
        token = start_grads() if l == 0 else None
        dx1, dscale2, dshift2 = _ln_mod_bwd(du2, s["x1"], scale2[l], dres2, cfg, name=f"ln_mod2_bwd_{l}")
        dres1, dmix, dg1, db1, dgate1, dbmo = _res_ln_bwd(dx1, s["x"], s["mix"], gate1[l], ln1_g[l][None], cfg, name=f"res_ln1_bwd_{l}")
        dmg = _matmul(dmix, w["w_mo"], mode="nt", name=f"d_merge_{l}", after=token)
        gbig[l, 3] = _matmul(s["mg"], dmix, mode="tn", name=f"dw_mix_out_{l}")
        dya, dyb, dzga, dzgb = _merge_bwd(dmg, s["zm"], s["ya"], s["yb"], cfg, name=f"merge_bwd_{l}")
        gbig[l, 1] = _matmul(dya, s["a3"], mode="tn", name=f"dw_conv_proj_{l}")
        da3 = _matmul(dya, w["w_cp_t"], mode="nn", name=f"d_a3_{l}")
        gbig[l, 2] = _matmul(dyb, s["o"], mode="tn", name=f"dw_attn_proj_{l}")
        token = start_grads() if l == 0 else None
        do = _matmul(dyb, w["w_ap_t"], mode="nn", out_dtype=BF16, name=f"d_o_{l}", after=token)
        dq, dk, dv, dcum_c = _attn_bwd(s["zm"], s["cum_c"], s["o32"], do, s["lse"], cfg, name=f"attn_bwd_{l}")
        dzf = _fgate_bwd(dcum_c, s["zf"], cfg, name=f"fgate_bwd_{l}")
        dzglu, dcaw, dcab, dlcg, dlcb = _conv_a_bwd(da3, s["zm"], caw[l], conv_a_b[l][None], ln_conv_g[l][None],
                                                    ln_conv_b[l][None], cfg, name=f"conv_a_bwd_{l}")
        dzm = jnp.concatenate([dzglu, dq, dk, dv, dzga, dzgb], axis=1)
        du1 = _matmul(dzf, w["wf_t"], mode="nn", name=f"d_u1_f_{l}")
        du1 = _matmul(dzm, w["wm_t"], mode="nn", add=du1, name=f"d_u1_{l}")
        dwm_t = _matmul(dzm, s["u"], mode="tn", name=f"dw_in_{l}")
        dwf_t = _matmul(dzf, s["u"], mode="tn", name=f"dw_in_f_{l}")
        gbig[l, 0] = _merge_cols(dwm_t.T, dwf_t[:H].T, f_off).T
        token = start_grads() if l > 0 else None
        dbm, dbf = _colsum(dzm, name=f"db_in_{l}"), _colsum(dzf, name=f"db_in_f_{l}")
        dx, dscale1, dshift1 = _ln_mod_bwd(du1, s["x"], scale1[l], dres1, cfg, name=f"ln_mod1_bwd_{l}")
        dmods[l] = jnp.concatenate([dshift1, dscale1, dgate1, dshift2, dscale2, dgate2], axis=1).reshape(Bl, 6 * D)
        gsm[l] = dict(b_in=_merge_cols(dbm[0], dbf[0, :H], f_off), conv_a_b=dcab[0], ln_conv_g=dlcg[0], ln_conv_b=dlcb[0],
                      b_mix_out=dbmo[0], ln1_g=dg1[0], ln1_b=db1[0], ffn_conv_b=jnp.concatenate([dcg[0], dcl[0]]),
                      ln2_g=dg2[0], ln2_b=db2[0], conv_a_w=dcaw[:KW], ffn_conv_w=jnp.concatenate([dwg[:KF], dwl[:KF]], axis=1))
    grad_x = dx.reshape(Bl, S, D)

    small_names = ["b_in", "conv_a_b", "ln_conv_g", "ln_conv_b", "b_mix_out", "ln1_g", "ln1_b", "ffn_conv_b", "ln2_g", "ln2_b",
                   "conv_a_w", "ffn_conv_w"]
    gs_list = [jnp.stack(dmods)] + [jnp.stack([gsm[l][n] for l in range(L)]) for n in small_names]
    gspack, gssegs = _pack(gs_list, D, F32)
    gs_all = _exchange(gspack, all_to_all=False, name="gather_small_grads")
    start_grads(after=gs_all)
    dmod_all = jnp.moveaxis(_unpack(gs_all, gssegs)[0], 0, 1).reshape(L, N_DEV * Bl, 6 * D)
    g_small = dict(zip(small_names, _unpack(_slot_sum(gs_all, name="sum_small_grads"), gssegs)[1:]))
    g_small["conv_a_w"] = my_cols(g_small["conv_a_w"], C // N_DEV)
    g_small["ffn_conv_w"] = my_cols(g_small["ffn_conv_w"], 2 * F // N_DEV)
    g_small["w_ada"] = _ada_bwd(c_all, my_cols(dmod_all, n_ada), name="ada_bwd")
    g_small["b_ada"] = jnp.stack([_colsum(dmod_all[l], name=f"db_ada_{l}")[0] for l in range(L)])

    given = dict(w_in=(w_in, m_w_in, v_w_in), w_conv_proj=(w_conv_proj, m_w_conv_proj, v_w_conv_proj),
                 w_attn_proj=(w_attn_proj, m_w_attn_proj, v_w_attn_proj), w_mix_out=(w_mix_out, m_w_mix_out, v_w_mix_out),
                 w_ffn_up=(w_ffn_up, m_w_ffn_up, v_w_ffn_up), w_ffn_down=(w_ffn_down, m_w_ffn_down, v_w_ffn_down),
                 w_ada=(w_ada, m_w_ada, v_w_ada), b_ada=(b_ada, m_b_ada, v_b_ada), b_in=(b_in, m_b_in, v_b_in),
                 conv_a_w=(conv_a_w, m_conv_a_w, v_conv_a_w), conv_a_b=(conv_a_b, m_conv_a_b, v_conv_a_b),
                 ln_conv_g=(ln_conv_g, m_ln_conv_g, v_ln_conv_g), ln_conv_b=(ln_conv_b, m_ln_conv_b, v_ln_conv_b),
                 b_mix_out=(b_mix_out, m_b_mix_out, v_b_mix_out), ln1_g=(ln1_g, m_ln1_g, v_ln1_g), ln1_b=(ln1_b, m_ln1_b, v_ln1_b),
                 ffn_conv_w=(ffn_conv_w, m_ffn_conv_w, v_ffn_conv_w), ffn_conv_b=(ffn_conv_b, m_ffn_conv_b, v_ffn_conv_b),
                 ln2_g=(ln2_g, m_ln2_g, v_ln2_g), ln2_b=(ln2_b, m_ln2_b, v_ln2_b))
    res, kinds = {}, ("grad", "delta", "new_m", "new_v")
    loc_names = ["b_ada"] + small_names
    deltas, new_ms, new_vs = _adamw_many([g_small[n] for n in loc_names], *([given[n][j] for n in loc_names] for j in range(3)),
                                         name="adamw_small")
    for n, d, m2, v2 in zip(loc_names, deltas, new_ms, new_vs):
        res["grad", n], res["delta", n], res["new_m", n], res["new_v", n] = g_small[n], d, m2, v2
    rows_ada = (L * D * n_ada // D, D)
    outs = _adamw(g_small["w_ada"].reshape((1,) + rows_ada), *(a.reshape(rows_ada) for a in given["w_ada"]), name="adamw_w_ada")
    for kind, a in zip(kinds, outs):
        res[kind, "w_ada"] = a.reshape(w_ada.shape)

    big_parts = {}
    after = outs[0]
    for gi, (state, send, segs, grp) in enumerate(g_state):
        landed = _exchange_wait(state, after, name=f"exchange_grads_wait_{gi}")
        wmv = [_pack(shard_items([given[n][j] for n in big_names], grp), D, F32)[0] for j in range(3)]
        outs = _adamw(landed, *wmv, name=f"adamw_big_{gi}")
        for kind, packed in zip(kinds, outs):
            for (l, i), a in zip(grp, _unpack(packed, segs)):
                big_parts[kind, l, i] = a.T if transposed[i] else a
        after = outs[0]
    for kind in kinds:
        for i, n in enumerate(big_names):
            res[kind, n] = jnp.stack([big_parts[kind, l, i] for l in range(L)])

    order = ["w_ada", "b_ada", "w_in", "b_in", "conv_a_w", "conv_a_b", "ln_conv_g", "ln_conv_b", "w_conv_proj", "w_attn_proj",
             "w_mix_out", "b_mix_out", "ln1_g", "ln1_b", "w_ffn_up", "ffn_conv_w", "ffn_conv_b", "w_ffn_down", "ln2_g", "ln2_b"]
    return (loss, grad_x, *[res[k, n] for k in ("grad", "delta", "new_m", "new_v") for n in order])
```

```python
import functools
import math
from typing import NamedTuple

import jax
import jax.numpy as jnp
from jax import lax
from jax.experimental import pallas as pl
from jax.experimental.pallas import tpu as pltpu

F32, BF16 = jnp.float32, jnp.bfloat16
LN_EPS = 1e-5
ADAM_LR, ADAM_B1, ADAM_B2, ADAM_EPS, ADAM_WD, ADAM_STEP = 0.001, 0.9, 0.999, 1e-08, 0.01, 10
N_DEV = 8
LANES = 128
VMEM_LIMIT = 56 * 1024 * 1024
NEG = -1e30
NT = (((1,), (1,)), ((), ()))
TN = (((0,), (0,)), ((), ()))


class Cfg(NamedTuple):
    L: int
    Bl: int
    S: int
    D: int
    C: int
    KW: int
    H: int
    Dh: int
    F: int
    KF: int

    @property
    def T(self): return self.Bl * self.S
    @property
    def AW(self): return self.H * self.Dh
    @property
    def NM(self): return 2 * self.C + 3 * self.AW + 2 * self.D
    @property
    def q_off(self): return 2 * self.C
    @property
    def g_off(self): return 2 * self.C + 3 * self.AW
    @property
    def alpha(self): return (2.0 * self.L) ** 0.25


def _pcall(body, **kw):
    return pl.pallas_call(body, **kw)


def _params(*sem):
    return pltpu.CompilerParams(dimension_semantics=sem, vmem_limit_bytes=VMEM_LIMIT)


def _pick(n, prefs):
    for p in prefs:
        if n % p == 0:
            return p
    return n


def _sigmoid(x):
    return 1.0 / (1.0 + jnp.exp(-x))


def _ln_stats(x):
    mu = jnp.mean(x, axis=-1, keepdims=True)
    xc = x - mu
    var = jnp.mean(xc * xc, axis=-1, keepdims=True)
    rstd = lax.rsqrt(var + LN_EPS)
    return xc * rstd, rstd


def _ln_bwd(dxh, xh, rstd):
    return rstd * (dxh - jnp.mean(dxh, axis=-1, keepdims=True) - xh * jnp.mean(dxh * xh, axis=-1, keepdims=True))


def _matmul(a, b, *, mode, name, bias=None, add=None, out_dtype=F32, tm=None, tn=None, tk=None, after=None):
    if mode == "tn":
        K, M = a.shape
    else:
        M, K = a.shape
    N = b.shape[0] if mode == "nt" else b.shape[1]
    lane_tiles = (1536, 1408, 1024, 768, 512, 256, 128)
    tm = tm or _pick(M, lane_tiles if mode == "tn" else (1024, 512, 256, 128, 64, 32, 16, 8))
    tn = tn or _pick(N, lane_tiles)
    tk = tk or _pick(K, (1024, 512, 256, 128) if mode == "tn" else lane_tiles)
    nk = K // tk
    dn = {"nn": (((1,), (0,)), ((), ())), "nt": NT, "tn": TN}[mode]
    has_bias, has_add, has_after = bias is not None, add is not None, after is not None

    def body(*refs):
        a_ref, b_ref = refs[0], refs[1]
        pos = 2
        bias_ref = refs[pos] if has_bias else None
        pos += has_bias
        add_ref = refs[pos] if has_add else None
        pos += has_add + has_after
        o_ref = refs[pos]
        part = lax.dot_general(a_ref[...], b_ref[...], dn, preferred_element_type=F32)

        def finish(acc):
            if has_bias:
                acc = acc + bias_ref[...]
            if has_add:
                acc = acc + add_ref[...]
            o_ref[...] = acc.astype(out_dtype)

        if nk == 1:
            finish(part)
        else:
            acc_ref = refs[pos + 1]
            k = pl.program_id(2)

            @pl.when(k == 0)
            def _():
                acc_ref[...] = part

            @pl.when(k > 0)
            def _():
                acc_ref[...] += part

            @pl.when(k == nk - 1)
            def _():
                finish(acc_ref[...])

    a_spec = pl.BlockSpec((tk, tm), lambda i, j, k: (k, i)) if mode == "tn" else pl.BlockSpec((tm, tk), lambda i, j, k: (i, k))
    b_spec = pl.BlockSpec((tn, tk), lambda i, j, k: (j, k)) if mode == "nt" else pl.BlockSpec((tk, tn), lambda i, j, k: (k, j))
    in_specs, args = [a_spec, b_spec], [a, b]
    if has_bias:
        in_specs.append(pl.BlockSpec((1, tn), lambda i, j, k: (0, j)))
        args.append(bias)
    if has_add:
        in_specs.append(pl.BlockSpec((tm, tn), lambda i, j, k: (i, j)))
        args.append(add)
    if has_after:
        in_specs.append(pl.BlockSpec(memory_space=pl.ANY))
        args.append(after)
    return _pcall(
        body, name=name, grid=(M // tm, N // tn, nk), in_specs=in_specs,
        out_specs=pl.BlockSpec((tm, tn), lambda i, j, k: (i, j)),
        out_shape=jax.ShapeDtypeStruct((M, N), out_dtype),
        scratch_shapes=[pltpu.VMEM((tm, tn), F32)] if nk > 1 else [],
        compiler_params=_params("parallel", "parallel", "arbitrary"),
    )(*args)


def _colsum(x, *, name):
    T, N = x.shape
    tr = _pick(T, (512, 256, 128, 64, 32, 16))
    tc = _pick(N, (1536, 1024, 512, 256, 128))

    def body(x_ref, o_ref):
        @pl.when(pl.program_id(1) == 0)
        def _():
            o_ref[...] = jnp.zeros_like(o_ref)

        o_ref[...] += jnp.sum(x_ref[...].astype(F32), axis=0, keepdims=True)

    return _pcall(body, name=name, grid=(N // tc, T // tr), in_specs=[pl.BlockSpec((tr, tc), lambda j, i: (i, j))],
                  out_specs=pl.BlockSpec((1, tc), lambda j, i: (0, j)), out_shape=jax.ShapeDtypeStruct((1, N), F32),
                  compiler_params=_params("parallel", "arbitrary"))(x)


def _row_tile(cfg):
    return _pick(cfg.S, (256, 128, 64, 32, 16, 8))


def _ln_mod_fwd(x, shift, scale, cfg, *, name):
    tr = _row_tile(cfg)
    tpb = cfg.S // tr

    def body(x_ref, sh_ref, sc_ref, u_ref):
        xh, _ = _ln_stats(x_ref[...])
        u_ref[...] = (xh * (1.0 + sc_ref[0]) + sh_ref[0]).astype(BF16)

    row = pl.BlockSpec((tr, cfg.D), lambda i: (i, 0))
    per_b = pl.BlockSpec((1, 1, cfg.D), lambda i: (i // tpb, 0, 0))
    return _pcall(body, name=name, grid=(cfg.T // tr,), in_specs=[row, per_b, per_b], out_specs=row,
                  out_shape=jax.ShapeDtypeStruct((cfg.T, cfg.D), BF16), compiler_params=_params("parallel"))(x, shift, scale)


def _res_ln_fwd(xin, br, gate, g, b, cfg, *, name, nxt=None):
    tr = _row_tile(cfg)
    tpb = cfg.S // tr
    alpha = cfg.alpha

    def body(*refs):
        x_ref, br_ref, gt_ref, g_ref, b_ref = refs[:5]
        r = alpha * x_ref[...] + (1.0 + gt_ref[0]) * br_ref[...]
        xh, _ = _ln_stats(r)
        xo = xh * g_ref[...] + b_ref[...]
        if nxt is None:
            refs[5][...] = xo
        else:
            sh_ref, sc_ref, xo_ref, u_ref = refs[5:]
            xo_ref[...] = xo
            uh, _ = _ln_stats(xo)
            u_ref[...] = (uh * (1.0 + sc_ref[0]) + sh_ref[0]).astype(BF16)

    row = pl.BlockSpec((tr, cfg.D), lambda i: (i, 0))
    per_b = pl.BlockSpec((1, 1, cfg.D), lambda i: (i // tpb, 0, 0))
    vec = pl.BlockSpec((1, cfg.D), lambda i: (0, 0))
    in_specs, args = [row, row, per_b, vec, vec], [xin, br, gate, g, b]
    out_specs, out_shape = row, jax.ShapeDtypeStruct((cfg.T, cfg.D), F32)
    if nxt is not None:
        in_specs += [per_b, per_b]
        args += list(nxt)
        out_specs = [row, row]
        out_shape = [out_shape, jax.ShapeDtypeStruct((cfg.T, cfg.D), BF16)]
    return _pcall(body, name=name, grid=(cfg.T // tr,), in_specs=in_specs, out_specs=out_specs, out_shape=out_shape,
                  compiler_params=_params("parallel"))(*args)


def _loss_grad(y, tgt, cfg, *, name):
    tr = _row_tile(cfg)
    nt = cfg.T // tr
    inv_d = 1.0 / cfg.D

    def body(y_ref, t_ref, dy_ref, ls_ref):
        e = y_ref[...] - t_ref[...]
        dy_ref[...] = e * inv_d
        ls_ref[...] = jnp.full((1, 1, LANES), jnp.sum(e * e), F32)

    row = pl.BlockSpec((tr, cfg.D), lambda i: (i, 0))
    return _pcall(body, name=name, grid=(nt,), in_specs=[row, row],
                  out_specs=[row, pl.BlockSpec((1, 1, LANES), lambda i: (i, 0, 0))],
                  out_shape=[jax.ShapeDtypeStruct((cfg.T, cfg.D), F32), jax.ShapeDtypeStruct((nt, 1, LANES), F32)],
                  compiler_params=_params("parallel"))(y, tgt)


def _res_ln_bwd(dy, xin, br, gate, g, cfg, *, name):
    tr = _row_tile(cfg)
    tpb = cfg.S // tr
    alpha = cfg.alpha

    def body(dy_ref, x_ref, br_ref, gt_ref, g_ref, dx_ref, dbr_ref, dg_ref, db_ref, dgt_ref, dbs_ref):
        i = pl.program_id(0)

        @pl.when(i == 0)
        def _():
            dg_ref[...] = jnp.zeros_like(dg_ref)
            db_ref[...] = jnp.zeros_like(db_ref)
            dbs_ref[...] = jnp.zeros_like(dbs_ref)

        @pl.when(i % tpb == 0)
        def _():
            dgt_ref[...] = jnp.zeros_like(dgt_ref)

        dy, brv, one_gate = dy_ref[...], br_ref[...], 1.0 + gt_ref[0]
        xh, rstd = _ln_stats(alpha * x_ref[...] + one_gate * brv)
        dg_ref[...] += jnp.sum(dy * xh, axis=0, keepdims=True)
        db_ref[...] += jnp.sum(dy, axis=0, keepdims=True)
        dr = _ln_bwd(dy * g_ref[...], xh, rstd)
        dx_ref[...] = alpha * dr
        dbr = one_gate * dr
        dbr_ref[...] = dbr.astype(BF16)
        dbs_ref[...] += jnp.sum(dbr, axis=0, keepdims=True)
        dgt_ref[0] += jnp.sum(dr * brv, axis=0, keepdims=True)

    row = pl.BlockSpec((tr, cfg.D), lambda i: (i, 0))
    per_b = pl.BlockSpec((1, 1, cfg.D), lambda i: (i // tpb, 0, 0))
    vec = pl.BlockSpec((1, cfg.D), lambda i: (0, 0))
    vs = jax.ShapeDtypeStruct((1, cfg.D), F32)
    return _pcall(body, name=name, grid=(cfg.T // tr,), in_specs=[row, row, row, per_b, vec],
                  out_specs=[row, row, vec, vec, per_b, vec],
                  out_shape=[jax.ShapeDtypeStruct((cfg.T, cfg.D), F32), jax.ShapeDtypeStruct((cfg.T, cfg.D), BF16), vs, vs,
                             jax.ShapeDtypeStruct((cfg.Bl, 1, cfg.D), F32), vs],
                  compiler_params=_params("arbitrary"))(dy, xin, br, gate, g)


def _ln_mod_bwd(du, xin, scale, dres, cfg, *, name):
    tr = _row_tile(cfg)
    tpb = cfg.S // tr

    def body(du_ref, x_ref, sc_ref, dres_ref, dx_ref, dsc_ref, dsh_ref):
        @pl.when(pl.program_id(0) % tpb == 0)
        def _():
            dsc_ref[...] = jnp.zeros_like(dsc_ref)
            dsh_ref[...] = jnp.zeros_like(dsh_ref)

        du = du_ref[...]
        xh, rstd = _ln_stats(x_ref[...])
        dsc_ref[0] += jnp.sum(du * xh, axis=0, keepdims=True)
        dsh_ref[0] += jnp.sum(du, axis=0, keepdims=True)
        dx_ref[...] = _ln_bwd(du * (1.0 + sc_ref[0]), xh, rstd) + dres_ref[...]

    row = pl.BlockSpec((tr, cfg.D), lambda i: (i, 0))
    per_b = pl.BlockSpec((1, 1, cfg.D), lambda i: (i // tpb, 0, 0))
    bs = jax.ShapeDtypeStruct((cfg.Bl, 1, cfg.D), F32)
    return _pcall(body, name=name, grid=(cfg.T // tr,), in_specs=[row, row, per_b, row], out_specs=[row, per_b, per_b],
                  out_shape=[jax.ShapeDtypeStruct((cfg.T, cfg.D), F32), bs, bs],
                  compiler_params=_params("arbitrary"))(du, xin, scale, dres)


def _merge_tiles(cfg):
    tr = _pick(cfg.T, (512, 256, 128, 64, 32, 16))
    tc = _pick(math.gcd(cfg.g_off, cfg.D), (512, 256, 128))
    return tr, tc


def _merge_fwd(zm, ya, yb, cfg, *, name):
    tr, tc = _merge_tiles(cfg)
    ga0, gb0 = cfg.g_off // tc, (cfg.g_off + cfg.D) // tc

    def body(ga_ref, gb_ref, ya_ref, yb_ref, m_ref):
        m_ref[...] = (_sigmoid(ga_ref[...]) * ya_ref[...] + _sigmoid(gb_ref[...]) * yb_ref[...]).astype(BF16)

    blk = pl.BlockSpec((tr, tc), lambda i, j: (i, j))
    return _pcall(body, name=name, grid=(cfg.T // tr, cfg.D // tc),
                  in_specs=[pl.BlockSpec((tr, tc), lambda i, j: (i, ga0 + j)), pl.BlockSpec((tr, tc), lambda i, j: (i, gb0 + j)), blk, blk],
                  out_specs=blk, out_shape=jax.ShapeDtypeStruct((cfg.T, cfg.D), BF16),
                  compiler_params=_params("parallel", "parallel"))(zm, zm, ya, yb)


def _merge_bwd(dm, zm, ya, yb, cfg, *, name):
    tr, tc = _merge_tiles(cfg)
    ga0, gb0 = cfg.g_off // tc, (cfg.g_off + cfg.D) // tc

    def body(dm_ref, ga_ref, gb_ref, ya_ref, yb_ref, dya_ref, dyb_ref, dga_ref, dgb_ref):
        dm = dm_ref[...]
        ga, gb = _sigmoid(ga_ref[...]), _sigmoid(gb_ref[...])
        dya_ref[...] = (dm * ga).astype(BF16)
        dyb_ref[...] = (dm * gb).astype(BF16)
        dga_ref[...] = (dm * ya_ref[...] * ga * (1.0 - ga)).astype(BF16)
        dgb_ref[...] = (dm * yb_ref[...] * gb * (1.0 - gb)).astype(BF16)

    blk = pl.BlockSpec((tr, tc), lambda i, j: (i, j))
    o = jax.ShapeDtypeStruct((cfg.T, cfg.D), BF16)
    return _pcall(body, name=name, grid=(cfg.T // tr, cfg.D // tc),
                  in_specs=[blk, pl.BlockSpec((tr, tc), lambda i, j: (i, ga0 + j)), pl.BlockSpec((tr, tc), lambda i, j: (i, gb0 + j)), blk, blk],
                  out_specs=[blk] * 4, out_shape=[o] * 4, compiler_params=_params("parallel", "parallel"))(dm, zm, zm, ya, yb)


CONV_A_HALO = 32
CONV_A_CHUNK = 32


SUBLANES = 8


def _conv_a_tile(cfg):
    assert cfg.KW - 1 <= CONV_A_HALO
    return _pick(cfg.S, (256, 128, 64, 32))


def _shift_copies(src_s, sh_s):
    rows = src_s.shape[0] - SUBLANES
    for b in range(1, SUBLANES):
        sh_s[b - 1, :, :] = src_s[b:b + rows, :]


def _rows(src_s, sh_s, start, n):
    a, b = divmod(start, SUBLANES)
    return src_s[start:start + n, :] if b == 0 else sh_s[b - 1, SUBLANES * a:SUBLANES * a + n, :]


def _conv_a_fwd(zm, w, cb, g, b, cfg, *, name):
    C, KW, HALO, CH = cfg.C, cfg.KW, CONV_A_HALO, CONV_A_CHUNK
    ts = _conv_a_tile(cfg)
    tpb = cfg.S // ts
    lead = HALO - (KW - 1)

    def body(z_ref, zp_ref, w_ref, cb_ref, g_ref, b_ref, o_ref, a0_s, a0_sh):
        first = pl.program_id(0) % tpb == 0
        prev = zp_ref[:, :C] * _sigmoid(zp_ref[:, C:])
        a0_s[0:HALO, :] = jnp.where(first, 0.0, prev)
        a0_s[HALO:HALO + ts, :] = z_ref[:, :C] * _sigmoid(z_ref[:, C:])
        _shift_copies(a0_s, a0_sh)
        for r0 in range(0, ts, CH):
            acc = jnp.zeros((CH, C), F32)
            for k in range(KW):
                acc = acc + w_ref[k:k + 1, :] * _rows(a0_s, a0_sh, r0 + lead + k, CH)
            xh, _ = _ln_stats(acc + cb_ref[...])
            a2 = xh * g_ref[...] + b_ref[...]
            o_ref[r0:r0 + CH, :] = (a2 * _sigmoid(a2)).astype(BF16)

    hb = ts // HALO
    vec = pl.BlockSpec((1, C), lambda i: (0, 0))
    return _pcall(body, name=name, grid=(cfg.T // ts,),
                  in_specs=[pl.BlockSpec((ts, 2 * C), lambda i: (i, 0)),
                            pl.BlockSpec((HALO, 2 * C), lambda i: (jnp.maximum(i * hb - 1, 0), 0)),
                            pl.BlockSpec((32, C), lambda i: (0, 0)), vec, vec, vec],
                  out_specs=pl.BlockSpec((ts, C), lambda i: (i, 0)), out_shape=jax.ShapeDtypeStruct((cfg.T, C), BF16),
                  scratch_shapes=[pltpu.VMEM((HALO + ts, C), F32), pltpu.VMEM((SUBLANES - 1, HALO + ts - SUBLANES, C), F32)],
                  compiler_params=_params("parallel"))(zm, zm, w, cb, g, b)


def _conv_a_bwd(da3, zm, w, cb, g, b, cfg, *, name):
    C, KW, HALO, CH = cfg.C, cfg.KW, CONV_A_HALO, CONV_A_CHUNK
    ts = _conv_a_tile(cfg)
    tpb = cfg.S // ts
    nt = cfg.T // ts
    lead = HALO - (KW - 1)
    ext = ts + HALO

    def body(z_ref, zp_ref, zn_ref, d_ref, dn_ref, w_ref, cb_ref, g_ref, b_ref,
             dz_ref, dw_ref, dcb_ref, dg_ref, db_ref, a0_s, d3_s, da1_s, a0_sh, da1_sh):
        i = pl.program_id(0)
        first, last = i % tpb == 0, i % tpb == tpb - 1

        @pl.when(i == 0)
        def _():
            dw_ref[...] = jnp.zeros_like(dw_ref)
            dcb_ref[...] = jnp.zeros_like(dcb_ref)
            dg_ref[...] = jnp.zeros_like(dg_ref)
            db_ref[...] = jnp.zeros_like(db_ref)

        a0_s[0:HALO, :] = jnp.where(first, 0.0, zp_ref[:, :C] * _sigmoid(zp_ref[:, C:]))
        a0_s[HALO:HALO + ts, :] = z_ref[:, :C] * _sigmoid(z_ref[:, C:])
        a0_s[HALO + ts:HALO + ext, :] = zn_ref[:, :C] * _sigmoid(zn_ref[:, C:])
        d3_s[0:ts, :] = d_ref[...]
        d3_s[ts:ext, :] = jnp.where(last, 0.0, dn_ref[...])
        _shift_copies(a0_s, a0_sh)
        dcb, dg, db = jnp.zeros((1, C), F32), jnp.zeros((1, C), F32), jnp.zeros((1, C), F32)
        for r0 in range(0, ext, CH):
            acc = jnp.zeros((CH, C), F32)
            for k in range(KW):
                acc = acc + w_ref[k:k + 1, :] * _rows(a0_s, a0_sh, r0 + lead + k, CH)
            xh, rstd = _ln_stats(acc + cb_ref[...])
            a2 = xh * g_ref[...] + b_ref[...]
            sg = _sigmoid(a2)
            da2 = d3_s[r0:r0 + CH, :] * (sg * (1.0 + a2 * (1.0 - sg)))
            da1 = _ln_bwd(da2 * g_ref[...], xh, rstd)
            da1_s[r0:r0 + CH, :] = da1
            if r0 < ts:
                dg = dg + jnp.sum(da2 * xh, axis=0, keepdims=True)
                db = db + jnp.sum(da2, axis=0, keepdims=True)
                dcb = dcb + jnp.sum(da1, axis=0, keepdims=True)
        dg_ref[...] += dg
        db_ref[...] += db
        dcb_ref[...] += dcb
        _shift_copies(da1_s, da1_sh)
        for k in range(KW):
            dwk = jnp.zeros((CH, C), F32)
            for r0 in range(0, ts, CH):
                dwk = dwk + da1_s[r0:r0 + CH, :] * _rows(a0_s, a0_sh, r0 + lead + k, CH)
            dw_ref[k:k + 1, :] += jnp.sum(dwk, axis=0, keepdims=True)
        for r0 in range(0, ts, CH):
            da0 = jnp.zeros((CH, C), F32)
            for k in range(KW):
                da0 = da0 + w_ref[k:k + 1, :] * _rows(da1_s, da1_sh, r0 + KW - 1 - k, CH)
            val, sg = z_ref[r0:r0 + CH, :C], _sigmoid(z_ref[r0:r0 + CH, C:])
            dz_ref[r0:r0 + CH, :C] = (da0 * sg).astype(BF16)
            dz_ref[r0:r0 + CH, C:] = (da0 * val * sg * (1.0 - sg)).astype(BF16)

    hb = ts // HALO
    nhb = cfg.T // HALO
    vec = pl.BlockSpec((1, C), lambda i: (0, 0))
    vs = jax.ShapeDtypeStruct((1, C), F32)
    return _pcall(body, name=name, grid=(nt,),
                  in_specs=[pl.BlockSpec((ts, 2 * C), lambda i: (i, 0)),
                            pl.BlockSpec((HALO, 2 * C), lambda i: (jnp.maximum(i * hb - 1, 0), 0)),
                            pl.BlockSpec((HALO, 2 * C), lambda i: (jnp.minimum((i + 1) * hb, nhb - 1), 0)),
                            pl.BlockSpec((ts, C), lambda i: (i, 0)),
                            pl.BlockSpec((HALO, C), lambda i: (jnp.minimum((i + 1) * hb, nhb - 1), 0)),
                            pl.BlockSpec((32, C), lambda i: (0, 0)), vec, vec, vec],
                  out_specs=[pl.BlockSpec((ts, 2 * C), lambda i: (i, 0)), pl.BlockSpec((32, C), lambda i: (0, 0)), vec, vec, vec],
                  out_shape=[jax.ShapeDtypeStruct((cfg.T, 2 * C), BF16), jax.ShapeDtypeStruct((32, C), F32), vs, vs, vs],
                  scratch_shapes=[pltpu.VMEM((HALO + ext, C), F32), pltpu.VMEM((ext, C), F32), pltpu.VMEM((ext, C), F32),
                                  pltpu.VMEM((SUBLANES - 1, HALO + ext - SUBLANES, C), F32),
                                  pltpu.VMEM((SUBLANES - 1, ext - SUBLANES, C), F32)],
                  compiler_params=_params("arbitrary"))(zm, zm, zm, da3, da3, w, cb, g, b)


def _cum_tile(cfg):
    return _pick(cfg.S, (256, 128, 64, 32, 16, 8))


def _fgate_fwd(zf, cfg, *, name):
    tc = _cum_tile(cfg)
    tpb = cfg.S // tc
    hp = _attn_tiles(cfg)[2]
    nb = cfg.H // hp

    def body(z_ref, o_ref, carry):
        @pl.when(pl.program_id(0) % tpb == 0)
        def _():
            carry[...] = jnp.zeros_like(carry)

        z = z_ref[...]
        logf = jnp.minimum(z, 0.0) - jnp.log(1.0 + jnp.exp(-jnp.abs(z)))
        tri = (lax.broadcasted_iota(jnp.int32, (tc, tc), 0) >= lax.broadcasted_iota(jnp.int32, (tc, tc), 1)).astype(F32)
        cum = jnp.dot(tri, logf, precision=lax.Precision.HIGHEST, preferred_element_type=F32) + carry[...]
        carry[...] = cum[tc - 1:tc, :]
        o_ref[0] = cum
        for b in range(1, nb):
            o_ref[b] = pltpu.roll(cum, LANES - hp * b, axis=1)

    return _pcall(body, name=name, grid=(cfg.T // tc,), in_specs=[pl.BlockSpec((tc, LANES), lambda i: (i, 0))],
                  out_specs=pl.BlockSpec((nb, tc, LANES), lambda i: (0, i, 0)),
                  out_shape=jax.ShapeDtypeStruct((nb, cfg.T, LANES), F32), scratch_shapes=[pltpu.VMEM((1, LANES), F32)],
                  compiler_params=_params("arbitrary"))(zf)


def _fgate_bwd(dcum_c, zf, cfg, *, name):
    tc = _cum_tile(cfg)
    tpb = cfg.S // tc
    nt = cfg.T // tc
    hp = _attn_tiles(cfg)[2]
    nb = cfg.H // hp

    def body(d_ref, z_ref, o_ref, carry):
        @pl.when(pl.program_id(0) % tpb == 0)
        def _():
            carry[...] = jnp.zeros_like(carry)

        d = d_ref[0]
        for b in range(1, nb):
            d = d + pltpu.roll(d_ref[b], hp * b, axis=1)
        tri = (lax.broadcasted_iota(jnp.int32, (tc, tc), 0) <= lax.broadcasted_iota(jnp.int32, (tc, tc), 1)).astype(F32)
        suf = jnp.dot(tri, d, precision=lax.Precision.HIGHEST, preferred_element_type=F32) + carry[...]
        o_ref[...] = (suf * _sigmoid(-z_ref[...])).astype(BF16)
        carry[...] = suf[0:1, :]

    blk = pl.BlockSpec((tc, LANES), lambda i: (nt - 1 - i, 0))
    return _pcall(body, name=name, grid=(nt,), in_specs=[pl.BlockSpec((nb, tc, LANES), lambda i: (0, nt - 1 - i, 0)), blk],
                  out_specs=blk, out_shape=jax.ShapeDtypeStruct((cfg.T, LANES), BF16),
                  scratch_shapes=[pltpu.VMEM((1, LANES), F32)], compiler_params=_params("arbitrary"))(dcum_c, zf)


def _attn_tiles(cfg):
    assert LANES % cfg.Dh == 0 and cfg.H % (LANES // cfg.Dh) == 0
    tk = _pick(cfg.S, (256, 128))
    tq = _pick(cfg.S, (2 * tk, tk))
    return tq, tk, LANES // cfg.Dh


BIAS_LANES = 3


def _head_lanes(hd, cfg, hp):
    li = lax.broadcasted_iota(jnp.int32, (1, LANES), 1)
    own = (li >= hd * cfg.Dh) & (li < (hd + 1) * cfg.Dh)
    return own, li, ((hd + 1) % hp) * cfg.Dh


def _q_aug(q, hd, cfg, hp):
    own, li, b0 = _head_lanes(hd, cfg, hp)
    ones = ((li >= b0) & (li < b0 + BIAS_LANES)).astype(F32)
    return jnp.where(own, q * cfg.Dh ** -0.5, ones).astype(BF16)


def _k_aug(k, ck, hd, cfg, hp):
    own, li, b0 = _head_lanes(hd, cfg, hp)
    hi = ck.astype(BF16).astype(F32)
    mid = (ck - hi).astype(BF16).astype(F32)
    lo = ck - hi - mid
    bias = jnp.where(li == b0, -hi, jnp.where(li == b0 + 1, -mid, jnp.where(li == b0 + 2, -lo, 0.0)))
    return jnp.where(own, k, bias).astype(BF16)


def _attn_fwd(zm, cum_c, cfg, *, name):
    S, Dh = cfg.S, cfg.Dh
    tq, tk, hp = _attn_tiles(cfg)
    assert hp >= 2
    nq, nb, per = S // tq, cfg.H // hp, tq // tk
    qb, kb, vb = cfg.q_off // LANES, (cfg.q_off + cfg.AW) // LANES, (cfg.q_off + 2 * cfg.AW) // LANES

    def body(q_ref, k_ref, v_ref, cc_ref, o_ref, o32_ref, lse_ref, ka_s, vt_s):
        qi = pl.program_id(2)

        @pl.when(qi == 0)
        def _():
            def prep(c, _):
                r = pl.multiple_of(c * tk, tk)
                kc = k_ref[pl.ds(r, tk), :]
                for hd in range(hp):
                    ka_s[hd, pl.ds(r, tk), :] = _k_aug(kc, cc_ref[0, pl.ds(r, tk), hd:hd + 1], hd, cfg, hp)
                vt_s[:, pl.ds(r, tk)] = v_ref[pl.ds(r, tk), :].T.astype(BF16)
                return 0

            lax.fori_loop(0, S // tk, prep, 0)

        key_i = lax.broadcasted_iota(jnp.int32, (tk, tq), 0)
        qry_i = lax.broadcasted_iota(jnp.int32, (tk, tq), 1)
        qf = q_ref[...]
        qa = [_q_aug(qf, hd, cfg, hp) for hd in range(hp)]

        def scores(j):
            r = pl.multiple_of(j * tk, tk)
            return tuple(lax.dot_general(ka_s[hd, pl.ds(r, tk), :], qa[hd], NT, preferred_element_type=F32) for hd in range(hp))

        def chunk(j, s_all, carry, diag=None):
            r = pl.multiple_of(j * tk, tk)
            new = []
            for hd in range(hp):
                m, l, acc = carry[hd]
                s = s_all[hd]
                if diag is not None:
                    s = jnp.where(key_i + diag * tk <= qry_i, s, NEG)
                m_new = jnp.maximum(m, jnp.max(s, axis=0, keepdims=True))
                a = jnp.exp(m - m_new)
                p = jnp.exp(s - m_new)
                l = a * l + jnp.sum(p, axis=0, keepdims=True)
                p_hi = p.astype(BF16)
                p_lo = (p - p_hi.astype(F32)).astype(BF16)
                vt = vt_s[hd * Dh:(hd + 1) * Dh, pl.ds(r, tk)]
                acc = a * acc + (jnp.dot(vt, p_hi, preferred_element_type=F32) + jnp.dot(vt, p_lo, preferred_element_type=F32))
                new.append((m_new, l, acc))
            return tuple(new)

        init = tuple((jnp.full((1, tq), NEG, F32), jnp.zeros((1, tq), F32), jnp.zeros((Dh, tq), F32)) for _ in range(hp))
        n_full = qi * per

        def step(j, c):
            stats, s_cur = c
            s_next = scores(j + 1)
            return chunk(j, s_cur, stats), s_next

        res, s_cur = lax.fori_loop(0, n_full, step, (init, scores(0)))
        for d in range(per):
            s_next = scores(n_full + d + 1) if d + 1 < per else None
            res = chunk(n_full + d, s_cur, res, diag=d)
            s_cur = s_next
        o = jnp.concatenate([acc / l for _, l, acc in res], axis=0).T
        o_ref[...] = o.astype(BF16)
        o32_ref[...] = o
        lse_ref[...] = jnp.zeros_like(lse_ref)
        for hd in range(hp):
            lse_ref[0, 0, hd:hd + 1, :] = res[hd][0] + jnp.log(res[hd][1])

    return _pcall(body, name=name, grid=(cfg.Bl, nb, nq),
                  in_specs=[pl.BlockSpec((tq, LANES), lambda b, h, i: (b * nq + i, qb + h)),
                            pl.BlockSpec((S, LANES), lambda b, h, i: (b, kb + h)),
                            pl.BlockSpec((S, LANES), lambda b, h, i: (b, vb + h)),
                            pl.BlockSpec((1, S, LANES), lambda b, h, i: (h, b, 0))],
                  out_specs=[pl.BlockSpec((tq, LANES), lambda b, h, i: (b * nq + i, h)),
                             pl.BlockSpec((tq, LANES), lambda b, h, i: (b * nq + i, h)),
                             pl.BlockSpec((1, 1, 8, tq), lambda b, h, i: (b, h, 0, i))],
                  out_shape=[jax.ShapeDtypeStruct((cfg.T, cfg.AW), BF16), jax.ShapeDtypeStruct((cfg.T, cfg.AW), F32),
                             jax.ShapeDtypeStruct((cfg.Bl, nb, 8, S), F32)],
                  scratch_shapes=[pltpu.VMEM((hp, S, LANES), BF16), pltpu.VMEM((LANES, S), BF16)],
                  compiler_params=_params("parallel", "parallel", "arbitrary"))(zm, zm, zm, cum_c)


def _attn_bwd(zm, cum_c, o, do, lse, cfg, *, name):
    S, Dh = cfg.S, cfg.Dh
    tq, t, hp = _attn_tiles(cfg)
    nq, nk, nb, per = S // tq, S // t, cfg.H // hp, tq // t
    qb, kb, vb = cfg.q_off // LANES, (cfg.q_off + cfg.AW) // LANES, (cfg.q_off + 2 * cfg.AW) // LANES
    scale = Dh ** -0.5

    def body(q_ref, k_ref, v_ref, cc_ref, o_ref, do_ref, lse_ref, dq_ref, dk_ref, dv_ref, dcc_ref,
             ka_s, qa_s, vz_s, kt_s, dd_s, dqt_s):
        li = lax.broadcasted_iota(jnp.int32, (1, LANES), 1)
        ri = lax.broadcasted_iota(jnp.int32, (LANES, 1), 0)
        key_i = lax.broadcasted_iota(jnp.int32, (t, tq), 0)
        qry_i = lax.broadcasted_iota(jnp.int32, (t, tq), 1)

        def prep(c, _):
            r = pl.multiple_of(c * t, t)
            kc, vc, qc = k_ref[pl.ds(r, t), :], v_ref[pl.ds(r, t), :], q_ref[pl.ds(r, t), :]
            prod_t = (do_ref[pl.ds(r, t), :].astype(F32) * o_ref[pl.ds(r, t), :].astype(F32)).T
            for hd in range(hp):
                own = _head_lanes(hd, cfg, hp)[0]
                ka_s[hd, pl.ds(r, t), :] = _k_aug(kc, cc_ref[0, pl.ds(r, t), hd:hd + 1], hd, cfg, hp)
                qa_s[hd, pl.ds(r, t), :] = _q_aug(qc, hd, cfg, hp)
                vz_s[hd, pl.ds(r, t), :] = jnp.where(own, vc, 0.0).astype(BF16)
                dd_s[hd:hd + 1, pl.ds(r, t)] = jnp.sum(prod_t[hd * Dh:(hd + 1) * Dh, :], axis=0, keepdims=True)
            kt_s[:, pl.ds(r, t)] = kc.T.astype(BF16)
            dqt_s[:, pl.ds(r, t)] = jnp.zeros((LANES, t), F32)
            return 0

        lax.fori_loop(0, nk, prep, 0)

        def kv_step(j, _):
            rk = pl.multiple_of(j * t, t)
            i0 = j // per

            def tile(i, carry, masked):
                rq = pl.multiple_of(i * tq, tq)
                dob = do_ref[pl.ds(rq, tq), :]
                new, dq_t = [], None
                for hd in range(hp):
                    dk_h, dv_h, dsum_h = carry[hd]
                    qa = qa_s[hd, pl.ds(rq, tq), :]
                    s = lax.dot_general(ka_s[hd, pl.ds(rk, t), :], qa, NT, preferred_element_type=F32)
                    p = jnp.exp(s - lse_ref[0, 0, hd:hd + 1, pl.ds(rq, tq)])
                    if masked:
                        p = jnp.where(key_i + (rk - rq) <= qry_i, p, 0.0)
                    dp = lax.dot_general(vz_s[hd, pl.ds(rk, t), :], dob, NT, preferred_element_type=F32)
                    ds = p * (dp - dd_s[hd:hd + 1, pl.ds(rq, tq)])
                    dsb = ds.astype(BF16)
                    dv_h = dv_h + jnp.dot(p.astype(BF16), dob, preferred_element_type=F32)
                    dk_h = dk_h + jnp.dot(dsb, qa, preferred_element_type=F32)
                    dq_h = jnp.dot(kt_s[:, pl.ds(rk, t)], dsb, preferred_element_type=F32)
                    dq_t = dq_h if hd == 0 else jnp.where((ri >= hd * Dh) & (ri < (hd + 1) * Dh), dq_h, dq_t)
                    for c0 in range(0, tq, LANES):
                        dsum_h = dsum_h + ds[:, c0:c0 + LANES]
                    new.append((dk_h, dv_h, dsum_h))
                dqt_s[:, pl.ds(rq, tq)] += dq_t * scale
                return tuple(new)

            zero = tuple((jnp.zeros((t, LANES), F32),) * 3 for _ in range(hp))
            res = lax.fori_loop(i0 + 1, nq, functools.partial(tile, masked=False), tile(i0, zero, True))
            dk, dv, dcc = res[0][0], res[0][1], jnp.zeros((t, LANES), F32)
            for hd in range(hp):
                own = _head_lanes(hd, cfg, hp)[0]
                if hd > 0:
                    dk, dv = jnp.where(own, res[hd][0], dk), jnp.where(own, res[hd][1], dv)
                dcc = dcc + jnp.where(li == hd, -jnp.sum(res[hd][2], axis=1, keepdims=True), 0.0)
            dk_ref[pl.ds(rk, t), :] = dk.astype(BF16)
            dv_ref[pl.ds(rk, t), :] = dv.astype(BF16)
            dcc_ref[0, pl.ds(rk, t), :] = dcc
            return 0

        lax.fori_loop(0, nk, kv_step, 0)

        def finish(c, _):
            r = pl.multiple_of(c * t, t)
            dq_ref[pl.ds(r, t), :] = dqt_s[:, pl.ds(r, t)].T.astype(BF16)
            return 0

        lax.fori_loop(0, nk, finish, 0)

    blk = pl.BlockSpec((S, LANES), lambda b, h: (b, h))
    cc = pl.BlockSpec((1, S, LANES), lambda b, h: (h, b, 0))
    os_ = jax.ShapeDtypeStruct((cfg.T, cfg.AW), BF16)
    return _pcall(body, name=name, grid=(cfg.Bl, nb),
                  in_specs=[pl.BlockSpec((S, LANES), lambda b, h: (b, qb + h)), pl.BlockSpec((S, LANES), lambda b, h: (b, kb + h)),
                            pl.BlockSpec((S, LANES), lambda b, h: (b, vb + h)), cc, blk, blk,
                            pl.BlockSpec((1, 1, 8, S), lambda b, h: (b, h, 0, 0))],
                  out_specs=[blk, blk, blk, cc],
                  out_shape=[os_, os_, os_, jax.ShapeDtypeStruct((nb, cfg.T, LANES), F32)],
                  scratch_shapes=[pltpu.VMEM((hp, S, LANES), BF16)] * 3 + [pltpu.VMEM((LANES, S), BF16),
                                  pltpu.VMEM((8, S), F32), pltpu.VMEM((LANES, S), F32)],
                  compiler_params=_params("parallel", "parallel"))(zm, zm, zm, cum_c, o, do, lse)


FFN_HALO = 8
FFN_CHUNK = 16


def _ffn_tiles(cfg):
    assert cfg.KF - 1 <= FFN_HALO
    return _pick(cfg.S, (512, 256, 128, 64, 32, 16, 8)), _pick(cfg.F, (256, 128))


def _gelu(x):
    return 0.5 * x * (1.0 + lax.erf(x * (2.0 ** -0.5)))


def _gelu_grad(x):
    return 0.5 * (1.0 + lax.erf(x * (2.0 ** -0.5))) + x * jnp.exp(-0.5 * x * x) * ((2.0 * math.pi) ** -0.5)


def _ffn_conv_fwd(h0, w, cb, cfg, *, name):
    KF, HALO = cfg.KF, FFN_HALO
    ts, tf = _ffn_tiles(cfg)
    tpb, nf = cfg.S // ts, cfg.F // tf
    lead = HALO - (KF - 1)

    CH = FFN_CHUNK

    def body(g_ref, gp_ref, l_ref, lp_ref, wg_ref, wl_ref, cg_ref, cl_ref, o_ref, g_s, l_s):
        first = pl.program_id(1) % tpb == 0
        for s, main, prev in ((g_s, g_ref, gp_ref), (l_s, l_ref, lp_ref)):
            s[0:HALO, :] = jnp.where(first, 0.0, prev[...])
            s[HALO:HALO + CH, :] = main[0:CH, :]
        wg, wl = [wg_ref[k:k + 1, :] for k in range(KF)], [wl_ref[k:k + 1, :] for k in range(KF)]
        for r0 in range(0, ts, CH):
            hg, hl = cg_ref[...], cl_ref[...]
            for k in range(KF):
                if r0 == 0:
                    xg, xl = g_s[lead + k:lead + k + CH, :], l_s[lead + k:lead + k + CH, :]
                else:
                    a = r0 - (KF - 1) + k
                    xg, xl = g_ref[a:a + CH, :], l_ref[a:a + CH, :]
                hg, hl = hg + wg[k] * xg, hl + wl[k] * xl
            o_ref[r0:r0 + CH, :] = (_gelu(hg) * hl).astype(BF16)

    hb = ts // HALO
    prev = lambda off: pl.BlockSpec((HALO, tf), lambda j, i: (jnp.maximum(i * hb - 1, 0), off + j))
    main = lambda off: pl.BlockSpec((ts, tf), lambda j, i: (i, off + j))
    wsp = lambda off: pl.BlockSpec((8, tf), lambda j, i: (0, off + j))
    vsp = lambda off: pl.BlockSpec((1, tf), lambda j, i: (0, off + j))
    return _pcall(body, name=name, grid=(nf, cfg.T // ts),
                  in_specs=[main(0), prev(0), main(nf), prev(nf), wsp(0), wsp(nf), vsp(0), vsp(nf)],
                  out_specs=pl.BlockSpec((ts, tf), lambda j, i: (i, j)), out_shape=jax.ShapeDtypeStruct((cfg.T, cfg.F), BF16),
                  scratch_shapes=[pltpu.VMEM((HALO + CH, tf), F32)] * 2,
                  compiler_params=_params("parallel", "parallel"))(h0, h0, h0, h0, w, w, cb, cb)


def _ffn_conv_bwd(df, h0, w, cb, cfg, *, name):
    KF, HALO = cfg.KF, FFN_HALO
    ts, tf = _ffn_tiles(cfg)
    tpb, nf = cfg.S // ts, cfg.F // tf
    lead = HALO - (KF - 1)
    ext = ts + HALO

    CH = FFN_CHUNK

    def body(g_ref, gp_ref, gn_ref, l_ref, lp_ref, ln_ref, d_ref, dn_ref, wg_ref, wl_ref, cg_ref, cl_ref,
             dg_ref, dl_ref, dwg_ref, dwl_ref, dcg_ref, dcl_ref, gh_s, lh_s, gt_s, lt_s, dhg_s, dhl_s):
        i = pl.program_id(1)
        first, last = i % tpb == 0, i % tpb == tpb - 1

        @pl.when(i == 0)
        def _():
            dwg_ref[...] = jnp.zeros_like(dwg_ref)
            dwl_ref[...] = jnp.zeros_like(dwl_ref)
            dcg_ref[...] = jnp.zeros_like(dcg_ref)
            dcl_ref[...] = jnp.zeros_like(dcl_ref)

        for head, tail, main, prev, nxt in ((gh_s, gt_s, g_ref, gp_ref, gn_ref), (lh_s, lt_s, l_ref, lp_ref, ln_ref)):
            head[0:HALO, :] = jnp.where(first, 0.0, prev[...])
            head[HALO:HALO + CH, :] = main[0:CH, :]
            tail[0:HALO, :] = main[ts - HALO:ts, :]
            tail[HALO:2 * HALO, :] = nxt[...]
        wg, wl = [wg_ref[k:k + 1, :] for k in range(KF)], [wl_ref[k:k + 1, :] for k in range(KF)]

        def grads(hg, hl, d):
            return d * hl * _gelu_grad(hg), d * _gelu(hg)

        for r0 in range(0, ts, CH):
            hg, hl = cg_ref[...], cl_ref[...]
            for k in range(KF):
                if r0 == 0:
                    xg, xl = gh_s[lead + k:lead + k + CH, :], lh_s[lead + k:lead + k + CH, :]
                else:
                    a = r0 - (KF - 1) + k
                    xg, xl = g_ref[a:a + CH, :], l_ref[a:a + CH, :]
                hg, hl = hg + wg[k] * xg, hl + wl[k] * xl
            dhg_s[r0:r0 + CH, :], dhl_s[r0:r0 + CH, :] = grads(hg, hl, d_ref[r0:r0 + CH, :])
        hg, hl = cg_ref[...], cl_ref[...]
        for k in range(KF):
            hg, hl = hg + wg[k] * gt_s[lead + k:lead + k + HALO, :], hl + wl[k] * lt_s[lead + k:lead + k + HALO, :]
        dhg_s[ts:ext, :], dhl_s[ts:ext, :] = grads(hg, hl, jnp.where(last, 0.0, dn_ref[...]))

        for dh_s, x_ref, wk, dx_ref, dw_ref, dc_ref in ((dhg_s, g_ref, wg, dg_ref, dwg_ref, dcg_ref),
                                                        (dhl_s, l_ref, wl, dl_ref, dwl_ref, dcl_ref)):
            dw_acc = [jnp.zeros((CH, tf), F32) for _ in range(KF)]
            for r0 in range(0, ts, CH):
                x = x_ref[r0:r0 + CH, :]
                dx = jnp.zeros((CH, tf), F32)
                for k in range(KF):
                    dhk = dh_s[r0 + KF - 1 - k:r0 + KF - 1 - k + CH, :]
                    dx = dx + wk[k] * dhk
                    dw_acc[k] = dw_acc[k] + x * dhk
                    if k == KF - 1:
                        dc_acc = dhk if r0 == 0 else dc_acc + dhk
                dx_ref[r0:r0 + CH, :] = dx.astype(BF16)
            for k in range(KF):
                dw_ref[k:k + 1, :] += jnp.sum(dw_acc[k], axis=0, keepdims=True)
            dc_ref[...] += jnp.sum(dc_acc, axis=0, keepdims=True)

    hb = ts // HALO
    nhb = cfg.T // HALO
    main = lambda off: pl.BlockSpec((ts, tf), lambda j, i: (i, off + j))
    prev = lambda off: pl.BlockSpec((HALO, tf), lambda j, i: (jnp.maximum(i * hb - 1, 0), off + j))
    nxt = lambda off: pl.BlockSpec((HALO, tf), lambda j, i: (jnp.minimum((i + 1) * hb, nhb - 1), off + j))
    wsp = lambda off: pl.BlockSpec((8, tf), lambda j, i: (0, off + j))
    vsp = lambda off: pl.BlockSpec((1, tf), lambda j, i: (0, off + j))
    dxs, dws, dcs = (jax.ShapeDtypeStruct((cfg.T, cfg.F), BF16), jax.ShapeDtypeStruct((8, cfg.F), F32),
                     jax.ShapeDtypeStruct((1, cfg.F), F32))
    return _pcall(body, name=name, grid=(nf, cfg.T // ts),
                  in_specs=[main(0), prev(0), nxt(0), main(nf), prev(nf), nxt(nf), main(0), nxt(0),
                            wsp(0), wsp(nf), vsp(0), vsp(nf)],
                  out_specs=[main(0), main(0), wsp(0), wsp(0), vsp(0), vsp(0)],
                  out_shape=[dxs, dxs, dws, dws, dcs, dcs],
                  scratch_shapes=[pltpu.VMEM((HALO + CH, tf), F32)] * 2 + [pltpu.VMEM((2 * HALO, tf), F32)] * 2
                  + [pltpu.VMEM((ext, tf), F32)] * 2,
                  compiler_params=_params("parallel", "arbitrary"))(h0, h0, h0, h0, h0, h0, df, df, w, w, cb, cb)


def _ada_fwd(c_all, w, b, *, name):
    L, D, n = w.shape
    B = c_all.shape[0]

    def body(c_ref, w_ref, b_ref, o_ref):
        c = c_ref[...]
        act = (c * _sigmoid(c)).astype(BF16)
        o_ref[0] = jnp.dot(act, w_ref[0].astype(BF16), preferred_element_type=F32) + b_ref[0]

    return _pcall(body, name=name, grid=(L,),
                  in_specs=[pl.BlockSpec((B, D), lambda l: (0, 0)), pl.BlockSpec((1, D, n), lambda l: (l, 0, 0)),
                            pl.BlockSpec((1, 1, n), lambda l: (l, 0, 0))],
                  out_specs=pl.BlockSpec((1, B, n), lambda l: (l, 0, 0)), out_shape=jax.ShapeDtypeStruct((L, B, n), F32),
                  compiler_params=_params("parallel"))(c_all, w, b)


def _ada_bwd(c_all, dmod, *, name):
    L, B, n = dmod.shape
    D = c_all.shape[1]

    def body(c_ref, d_ref, o_ref):
        c = c_ref[...]
        act = (c * _sigmoid(c)).astype(BF16)
        o_ref[0] = lax.dot_general(act, d_ref[0].astype(BF16), TN, preferred_element_type=F32)

    return _pcall(body, name=name, grid=(L,),
                  in_specs=[pl.BlockSpec((B, D), lambda l: (0, 0)), pl.BlockSpec((1, B, n), lambda l: (l, 0, 0))],
                  out_specs=pl.BlockSpec((1, D, n), lambda l: (l, 0, 0)), out_shape=jax.ShapeDtypeStruct((L, D, n), F32),
                  compiler_params=_params("parallel"))(c_all, dmod)


def _slot_sum(x, *, name):
    n, R, W = x.shape
    tr = _pick(R, (256, 128, 64, 32, 16, 8))

    def body(x_ref, o_ref):
        acc = x_ref[0].astype(F32)
        for k in range(1, n):
            acc = acc + x_ref[k].astype(F32)
        o_ref[...] = acc

    return _pcall(body, name=name, grid=(R // tr,), in_specs=[pl.BlockSpec((n, tr, W), lambda i: (0, i, 0))],
                  out_specs=pl.BlockSpec((tr, W), lambda i: (i, 0)), out_shape=jax.ShapeDtypeStruct((R, W), F32),
                  compiler_params=_params("parallel"))(x)


def _adamw_math(g, w, m, v):
    c1, c2 = 1.0 - ADAM_B1 ** ADAM_STEP, 1.0 - ADAM_B2 ** ADAM_STEP
    m2 = ADAM_B1 * m + (1.0 - ADAM_B1) * g
    v2 = ADAM_B2 * v + (1.0 - ADAM_B2) * (g * g)
    return -ADAM_LR * ((m2 / c1) / (jnp.sqrt(v2 / c2) + ADAM_EPS) + ADAM_WD * w), m2, v2


def _adamw_many(gs, ws, ms, vs, *, name):
    n = len(gs)

    def body(*refs):
        ins, outs = refs[:4 * n], refs[4 * n:]
        for i in range(n):
            d, m2, v2 = _adamw_math(*(ins[j * n + i][...] for j in range(4)))
            outs[i][...], outs[n + i][...], outs[2 * n + i][...] = d, m2, v2

    vm = pl.BlockSpec(memory_space=pltpu.VMEM)
    outs = _pcall(body, name=name, in_specs=[vm] * (4 * n), out_specs=[vm] * (3 * n),
                  out_shape=[jax.ShapeDtypeStruct(a.shape, F32) for _ in range(3) for a in ws],
                  compiler_params=pltpu.CompilerParams(vmem_limit_bytes=VMEM_LIMIT))(*gs, *ws, *ms, *vs)
    return outs[:n], outs[n:2 * n], outs[2 * n:]


def _adamw(gs, w, m, v, *, name):
    n, R, W = gs.shape
    tr = _pick(R, (256, 128, 64, 32, 16, 8))

    def body(g_ref, w_ref, m_ref, v_ref, go_ref, d_ref, mo_ref, vo_ref):
        g = g_ref[0].astype(F32)
        for k in range(1, n):
            g = g + g_ref[k].astype(F32)
        go_ref[...] = g
        d_ref[...], mo_ref[...], vo_ref[...] = _adamw_math(g, w_ref[...], m_ref[...], v_ref[...])

    blk = pl.BlockSpec((tr, W), lambda i: (i, 0))
    o = jax.ShapeDtypeStruct((R, W), F32)
    return _pcall(body, name=name, grid=(R // tr,), in_specs=[pl.BlockSpec((n, tr, W), lambda i: (0, i, 0)), blk, blk, blk],
                  out_specs=[blk] * 4, out_shape=[o] * 4, compiler_params=_params("parallel"))(gs, w, m, v)


def _peer_copies(x_ref, land_ref, send_sems, recv_sems, all_to_all):
    mx, my, mc = lax.axis_index("x"), lax.axis_index("y"), lax.axis_index("c")
    me = 4 * mx + 2 * my + mc
    copies = []
    for k in range(1, N_DEV):
        px, py, pc = mx ^ ((k >> 2) & 1), my ^ ((k >> 1) & 1), mc ^ (k & 1)
        copies.append(pltpu.make_async_remote_copy(
            src_ref=x_ref.at[4 * px + 2 * py + pc] if all_to_all else x_ref, dst_ref=land_ref.at[me],
            send_sem=send_sems.at[k - 1], recv_sem=recv_sems.at[k - 1], device_id=(px, py, pc),
            device_id_type=pl.DeviceIdType.MESH))
    return copies


def _gather_two_level(x, *, name, after=None):
    def body(x_ref, *rest):
        o_ref, send_sems, recv_sems, local_sem = rest[-4:]
        mx, my, mc = lax.axis_index("x"), lax.axis_index("y"), lax.axis_index("c")
        me, sibling = (mx, my, mc), (mx, my, 1 - mc)
        chips = [(1 - mx, my), (mx, 1 - my), (1 - mx, 1 - my)]

        def slot(px, py, pc):
            return o_ref.at[4 * px + 2 * py + pc]

        def copy(k, block, to, src=None):
            return pltpu.make_async_remote_copy(
                src_ref=slot(*block) if src is None else src, dst_ref=slot(*block), send_sem=send_sems.at[k],
                recv_sem=recv_sems.at[k], device_id=to, device_id_type=pl.DeviceIdType.MESH)

        mine = pltpu.make_async_copy(x_ref, slot(*me), local_sem)
        mine.start()
        first = [copy(0, me, sibling, src=x_ref)] + [copy(1 + j, me, (*chip, mc), src=x_ref) for j, chip in enumerate(chips)]
        for cp in first:
            cp.start()
        passed = [copy(4 + j, (*chip, mc), sibling) for j, chip in enumerate(chips)]
        for j, chip in enumerate(chips):
            copy(1 + j, (*chip, mc), me).wait_recv()
            passed[j].start()
        copy(0, sibling, me).wait_recv()
        for j, chip in enumerate(chips):
            copy(4 + j, (*chip, 1 - mc), me).wait_recv()
        for cp in first + passed:
            cp.wait_send()
        mine.wait()

    anyspec = pl.BlockSpec(memory_space=pl.ANY)
    args = [x] if after is None else [x, after]
    return _pcall(body, name=name, in_specs=[anyspec] * len(args), out_specs=anyspec,
                  out_shape=jax.ShapeDtypeStruct((N_DEV,) + tuple(x.shape), x.dtype),
                  scratch_shapes=[pltpu.SemaphoreType.DMA((N_DEV - 1,)), pltpu.SemaphoreType.DMA((N_DEV - 1,)),
                                  pltpu.SemaphoreType.DMA(())])(*args)


_HBM = pl.BlockSpec(memory_space=pltpu.HBM)
_SEM = pl.BlockSpec(memory_space=pltpu.SEMAPHORE)
_EFFECT = pltpu.SideEffectType.DATAFLOW_SIDE_EFFECTING


def _exchange_start(x, *, all_to_all, name, after=None):
    blk = x.shape[1:] if all_to_all else x.shape
    land = lax.empty((N_DEV,) + tuple(blk), x.dtype)
    has_after = after is not None

    def body(*refs):
        x_ref, land_ref = refs[0], refs[1]
        send_sems, recv_sems, _, _, token, local_sem = refs[2 + has_after:]
        me = 4 * lax.axis_index("x") + 2 * lax.axis_index("y") + lax.axis_index("c")
        mine = pltpu.make_async_copy(x_ref.at[me] if all_to_all else x_ref, land_ref.at[me], local_sem)
        mine.start()
        mine.wait()
        for cp in _peer_copies(x_ref, land_ref, send_sems, recv_sems, all_to_all):
            cp.start()
        token[...] = jnp.zeros_like(token)

    n_sem = pltpu.SemaphoreType.DMA((N_DEV - 1,))
    args = [pltpu.with_memory_space_constraint(x, pltpu.HBM), pltpu.with_memory_space_constraint(land, pltpu.HBM)]
    in_specs = [_HBM, _HBM]
    if has_after:
        args.append(after)
        in_specs.append(pl.BlockSpec(memory_space=pl.ANY))
    send_sems, recv_sems, x_thru, land_thru, token = _pcall(
        body, name=name, in_specs=in_specs,
        out_shape=(n_sem, n_sem, pltpu.HBM(x.shape, x.dtype), pltpu.HBM(land.shape, land.dtype),
                   jax.ShapeDtypeStruct((8, LANES), F32)),
        out_specs=(_SEM, _SEM, _HBM, _HBM, pl.BlockSpec(memory_space=pltpu.VMEM)), input_output_aliases={0: 2, 1: 3},
        scratch_shapes=[pltpu.SemaphoreType.DMA(())],
        compiler_params=pltpu.CompilerParams(has_side_effects=_EFFECT))(*args)
    return (send_sems, recv_sems, x_thru, land_thru, all_to_all), token


def _exchange_wait(state, after, *, name):
    send_sems, recv_sems, x_thru, land_thru, all_to_all = state

    def body(x_ref, land_ref, send_sems, recv_sems, after_ref, x_dead, landed):
        for cp in _peer_copies(x_ref, land_ref, send_sems, recv_sems, all_to_all):
            cp.wait_send()
            cp.wait_recv()

    return _pcall(
        body, name=name, in_specs=(_HBM, _HBM, _SEM, _SEM, pl.BlockSpec(memory_space=pl.ANY)),
        out_shape=(pltpu.HBM(x_thru.shape, x_thru.dtype), pltpu.HBM(land_thru.shape, land_thru.dtype)),
        out_specs=(_HBM, _HBM), input_output_aliases={0: 0, 1: 1},
        compiler_params=pltpu.CompilerParams(has_side_effects=_EFFECT))(x_thru, land_thru, send_sems, recv_sems, after)[1]


def _exchange(x, *, all_to_all, name, after=None):
    blk = x.shape[1:] if all_to_all else x.shape

    def body(x_ref, *rest):
        o_ref, send_sems, recv_sems, local_sem = rest[-4:]
        me = 4 * lax.axis_index("x") + 2 * lax.axis_index("y") + lax.axis_index("c")
        mine = pltpu.make_async_copy(x_ref.at[me] if all_to_all else x_ref, o_ref.at[me], local_sem)
        mine.start()
        copies = _peer_copies(x_ref, o_ref, send_sems, recv_sems, all_to_all)
        for cp in copies:
            cp.start()
        for cp in copies:
            cp.wait()
        mine.wait()

    anyspec = pl.BlockSpec(memory_space=pl.ANY)
    args = [x] if after is None else [x, after]
    return _pcall(body, name=name, in_specs=[anyspec] * len(args), out_specs=anyspec,
                  out_shape=jax.ShapeDtypeStruct((N_DEV,) + tuple(blk), x.dtype),
                  scratch_shapes=[pltpu.SemaphoreType.DMA((N_DEV - 1,)), pltpu.SemaphoreType.DMA((N_DEV - 1,)),
                                  pltpu.SemaphoreType.DMA(())])(*args)


PACK_ROWS = 16


def _pack(arrs, width, dtype, lead=0):
    parts, segs, r = [], [], 0
    for a in arrs:
        lshape, shape = a.shape[:lead], a.shape[lead:]
        n = math.prod(shape)
        rows = -(-n // width)
        rows_p = -(-rows // PACK_ROWS) * PACK_ROWS
        if n == rows * width:
            blk = a.reshape(lshape + (rows, width)).astype(dtype)
            parts.append(jnp.pad(blk, [(0, 0)] * lead + [(0, rows_p - rows), (0, 0)]) if rows_p > rows else blk)
        else:
            flat = jnp.pad(a.reshape(lshape + (n,)).astype(dtype), [(0, 0)] * lead + [(0, rows_p * width - n)])
            parts.append(flat.reshape(lshape + (rows_p, width)))
        segs.append((r, n, shape))
        r += rows_p
    return jnp.concatenate(parts, axis=lead), segs


def _unpack(p, segs):
    lshape, width = p.shape[:-2], p.shape[-1]
    outs = []
    for r, n, shape in segs:
        rows = -(-n // width)
        blk = p[..., r:r + rows, :]
        if n != rows * width:
            blk = blk.reshape(lshape + (rows * width,))[..., :n]
        outs.append(blk.reshape(lshape + shape))
    return outs


def _split_cols(a, f_off, h):
    return jnp.concatenate([a[..., :f_off], a[..., f_off + h:]], axis=-1), a[..., f_off:f_off + h]


def _merge_cols(main, f, f_off):
    return jnp.concatenate([main[..., :f_off], f, main[..., f_off:]], axis=-1)


def _pad_to(a, n, axis):
    pad = [(0, 0)] * a.ndim
    pad[axis] = (0, n - a.shape[axis])
    return jnp.pad(a, pad)


def kernel(x, c, w_ada, b_ada, w_in, b_in, conv_a_w, conv_a_b, ln_conv_g, ln_conv_b, w_conv_proj, w_attn_proj, w_mix_out, b_mix_out, ln1_g, ln1_b, w_ffn_up, ffn_conv_w, ffn_conv_b, w_ffn_down, ln2_g, ln2_b, loss_target, m_w_ada, m_b_ada, m_w_in, m_b_in, m_conv_a_w, m_conv_a_b, m_ln_conv_g, m_ln_conv_b, m_w_conv_proj, m_w_attn_proj, m_w_mix_out, m_b_mix_out, m_ln1_g, m_ln1_b, m_w_ffn_up, m_ffn_conv_w, m_ffn_conv_b, m_w_ffn_down, m_ln2_g, m_ln2_b, v_w_ada, v_b_ada, v_w_in, v_b_in, v_conv_a_w, v_conv_a_b, v_ln_conv_g, v_ln_conv_b, v_w_conv_proj, v_w_attn_proj, v_w_mix_out, v_b_mix_out, v_ln1_g, v_ln1_b, v_w_ffn_up, v_ffn_conv_w, v_ffn_conv_b, v_w_ffn_down, v_ln2_g, v_ln2_b):
    L, D = w_ada.shape[0], w_ada.shape[1]
    Bl, S, _ = x.shape
    C, KW, AW = conv_a_b.shape[1], conv_a_w.shape[1], w_attn_proj.shape[1]
    F, KF, n_in_all = ffn_conv_b.shape[1] // 2, ffn_conv_w.shape[1], b_in.shape[1]
    H = n_in_all - 2 * C - 3 * AW - 2 * D
    cfg = Cfg(L=L, Bl=Bl, S=S, D=D, C=C, KW=KW, H=H, Dh=AW // H, F=F, KF=KF)
    T, NM = cfg.T, cfg.NM
    f_off = 2 * C + 3 * AW
    n_ada = w_ada.shape[2]
    me = 4 * lax.axis_index("x") + 2 * lax.axis_index("y") + lax.axis_index("c")

    def my_cols(a, n):
        return lax.dynamic_slice_in_dim(a, me * n, n, axis=a.ndim - 1)

    spack, ssegs = _pack([c, conv_a_w, ffn_conv_w], D, F32)
    c_g, caw_g, fcw_g = _unpack(_exchange(spack, all_to_all=False, name="gather_small"), ssegs)
    c_all = c_g.reshape(N_DEV * Bl, D)
    caw = _pad_to(jnp.moveaxis(caw_g, 0, 2).reshape(L, KW, C), 32, 1)
    fcw = _pad_to(jnp.moveaxis(fcw_g, 0, 2).reshape(L, KF, 2 * F), 8, 1)

    mod_part = _ada_fwd(c_all, w_ada, my_cols(b_ada, n_ada)[:, None, :], name="ada_fwd")
    mod_send = jnp.moveaxis(mod_part.reshape(L, N_DEV, Bl, n_ada), 1, 0).reshape(N_DEV, L * Bl, n_ada)
    mod_recv = _exchange(mod_send, all_to_all=True, name="exchange_mod")
    mod = jnp.moveaxis(mod_recv.reshape(N_DEV, L, Bl, n_ada), 0, 2).reshape(L, Bl, 6, 1, D)
    shift1, scale1, gate1, shift2, scale2, gate2 = (mod[:, :, i] for i in range(6))

    big_names = ["w_in", "w_conv_proj", "w_attn_proj", "w_mix_out", "w_ffn_up", "w_ffn_down"]
    transposed = (True, True, True, False, True, False)

    def shard_items(arrs, grp):
        return [arrs[i][l].T if transposed[i] else arrs[i][l] for l, i in grp]

    W = [dict() for _ in range(L)]

    def set_weights(landed, segs, grp):
        for (l, i), a in zip(grp, _unpack(landed, segs)):
            a = a.reshape((-1, a.shape[-1]))
            if i == 0:
                wm_t, wf_t = _split_cols(a.T, f_off, H)
                bm, bf = _split_cols(b_in[l], f_off, H)
                W[l].update(wm_t=wm_t.T, wf_t=_pad_to(wf_t.T, LANES, 0), bm=bm[None], bf=_pad_to(bf, LANES, 0)[None])
            else:
                W[l][("w_cp_t", "w_ap_t", "w_mo", "w_up_t", "w_dn")[i - 1]] = a

    big_w = (w_in, w_conv_proj, w_attn_proj, w_mix_out, w_ffn_up, w_ffn_down)
    w_groups = [[(l, i) for i in range(6)] for l in range(L)]
    pack, segs = _pack(shard_items(big_w, w_groups[0]), D, BF16)
    landed0 = _gather_two_level(pack, name="gather_weights_0", after=mod_recv)
    set_weights(landed0, segs, w_groups[0])
    w_state, token = {}, landed0
    for l in range(1, L):
        pack, segs = _pack(shard_items(big_w, w_groups[l]), D, BF16)
        state, token = _exchange_start(pack, all_to_all=False, name=f"gather_weights_start_{l}", after=token)
        w_state[l] = (state, pack, segs)

    def wait_weights(l, after):
        state, pack, segs = w_state[l]
        set_weights(_exchange_wait(state, after, name=f"gather_weights_wait_{l}"), segs, w_groups[l])

    xf = x.reshape(T, D)
    u = _ln_mod_fwd(xf, shift1[0], scale1[0], cfg, name="ln_mod_fwd")
    saved = []
    xin = xf
    for l in range(L):
        w = W[l]
        if l > 0:
            wait_weights(l, u)
        zm = _matmul(u, w["wm_t"], mode="nt", bias=w["bm"], name=f"in_proj_{l}", after=token if l == 0 else None)
        zf = _matmul(u, w["wf_t"], mode="nt", bias=w["bf"], name=f"in_proj_f_{l}")
        a3 = _conv_a_fwd(zm, caw[l], conv_a_b[l][None], ln_conv_g[l][None], ln_conv_b[l][None], cfg, name=f"conv_a_fwd_{l}")
        cum_c = _fgate_fwd(zf, cfg, name=f"fgate_fwd_{l}")
        o, o32, lse = _attn_fwd(zm, cum_c, cfg, name=f"attn_fwd_{l}")
        ya =_matmul(a3, w["w_cp_t"], mode="nt", name=f"conv_proj_{l}")
        yb = _matmul(o, w["w_ap_t"], mode="nt", name=f"attn_proj_{l}")
        mg = _merge_fwd(zm, ya, yb, cfg, name=f"merge_fwd_{l}")
        mix = _matmul(mg, w["w_mo"], mode="nn", bias=b_mix_out[l][None], name=f"mix_out_{l}")
        x1, u2 = _res_ln_fwd(xin, mix, gate1[l], ln1_g[l][None], ln1_b[l][None], cfg, name=f"res_ln1_fwd_{l}",
                             nxt=(shift2[l], scale2[l]))
        h0 = _matmul(u2, w["w_up_t"], mode="nt", name=f"ffn_up_{l}")
        fa = _ffn_conv_fwd(h0, fcw[l], ffn_conv_b[l][None], cfg, name=f"ffn_conv_fwd_{l}")
        ffn = _matmul(fa, w["w_dn"], mode="nn", name=f"ffn_down_{l}")
        saved.append(dict(x=xin, u=u, zm=zm, zf=zf, a3=a3, cum_c=cum_c, o=o, o32=o32, lse=lse, ya=ya, yb=yb, mg=mg, mix=mix,
                          x1=x1, u2=u2, h0=h0, fa=fa, ffn=ffn))
        if l + 1 < L:
            xin, u = _res_ln_fwd(x1, ffn, gate2[l], ln2_g[l][None], ln2_b[l][None], cfg, name=f"res_ln2_fwd_{l}",
                                 nxt=(shift1[l + 1], scale1[l + 1]))
        else:
            xin = _res_ln_fwd(x1, ffn, gate2[l], ln2_g[l][None], ln2_b[l][None], cfg, name=f"res_ln2_fwd_{l}")

    dx, loss_tiles = _loss_grad(xin, loss_target.reshape(T, D), cfg, name="loss_grad")
    loss = lax.psum(0.5 / D * jnp.sum(loss_tiles[:, 0, 0]), ("x", "y", "c"))

    gbig = {}
    g_groups = [[(l, i) for i in range(6)] for l in reversed(range(1, L))] + [[(0, 4), (0, 5)], [(0, 1), (0, 2), (0, 3)], [(0, 0)]]
    g_state = []

    def start_grads(after=None):
        grp = g_groups[len(g_state)]
        send, segs = _pack([gbig[k].reshape((N_DEV, -1, gbig[k].shape[1])) for k in grp], D, BF16, lead=1)
        state, tok = _exchange_start(send, all_to_all=True, name=f"exchange_grads_start_{len(g_state)}", after=after)
        g_state.append((state, send, segs, grp))
        return tok

    gsm = [dict() for _ in range(L)]
    dmods = [None] * L
    token = None
    for l in reversed(range(L)):
        w, s = W[l], saved[l]
        dres2, dffn, dg2, db2, dgate2, _ = _res_ln_bwd(dx, s["x1"], s["ffn"], gate2[l], ln2_g[l][None], cfg, name=f"res_ln2_bwd_{l}")
        dfa = _matmul(dffn, w["w_dn"], mode="nt", name=f"d_ffn_act_{l}", after=token)
        gbig[l, 5] = _matmul(s["fa"], dffn, mode="tn", name=f"dw_ffn_down_{l}")
        dh0g, dh0l, dwg, dwl, dcg, dcl = _ffn_conv_bwd(dfa, s["h0"], fcw[l], ffn_conv_b[l][None], cfg, name=f"ffn_conv_bwd_{l}")
        dh0 = jnp.concatenate([dh0g, dh0l], axis=1)
        du2 = _matmul(dh0, w["w_up_t"], mode="nn", name=f"d_u2_{l}")
        gbig[l, 4] = _matmul(dh0, s["u2"], mode="tn", name=f"dw_ffn_up_{l}")
        token = start_grads() if l == 0 else None
        dx1, dscale2, dshift2 = _ln_mod_bwd(du2, s["x1"], scale2[l], dres2, cfg, name=f"ln_mod2_bwd_{l}")
        dres1, dmix, dg1, db1, dgate1, dbmo = _res_ln_bwd(dx1, s["x"], s["mix"], gate1[l], ln1_g[l][None], cfg, name=f"res_ln1_bwd_{l}")
        dmg = _matmul(dmix, w["w_mo"], mode="nt", name=f"d_merge_{l}", after=token)
        gbig[l, 3] = _matmul(s["mg"], dmix, mode="tn", name=f"dw_mix_out_{l}")
        dya, dyb, dzga, dzgb = _merge_bwd(dmg, s["zm"], s["ya"], s["yb"], cfg, name=f"merge_bwd_{l}")
        gbig[l, 1] = _matmul(dya, s["a3"], mode="tn", name=f"dw_conv_proj_{l}")
        da3 = _matmul(dya, w["w_cp_t"], mode="nn", name=f"d_a3_{l}")
        gbig[l, 2] = _matmul(dyb, s["o"], mode="tn", name=f"dw_attn_proj_{l}")
        token = start_grads() if l == 0 else None
        do = _matmul(dyb, w["w_ap_t"], mode="nn", out_dtype=BF16, name=f"d_o_{l}", after=token)
        dq, dk, dv, dcum_c = _attn_bwd(s["zm"], s["cum_c"], s["o32"], do, s["lse"], cfg, name=f"attn_bwd_{l}")
        dzf = _fgate_bwd(dcum_c, s["zf"], cfg, name=f"fgate_bwd_{l}")
        dzglu, dcaw, dcab, dlcg, dlcb = _conv_a_bwd(da3, s["zm"], caw[l], conv_a_b[l][None], ln_conv_g[l][None],
                                                    ln_conv_b[l][None], cfg, name=f"conv_a_bwd_{l}")
        dzm = jnp.concatenate([dzglu, dq, dk, dv, dzga, dzgb], axis=1)
        du1 = _matmul(dzf, w["wf_t"], mode="nn", name=f"d_u1_f_{l}")
        du1 = _matmul(dzm, w["wm_t"], mode="nn", add=du1, name=f"d_u1_{l}")
        dwm_t = _matmul(dzm, s["u"], mode="tn", name=f"dw_in_{l}")
        dwf_t = _matmul(dzf, s["u"], mode="tn", name=f"dw_in_f_{l}")
        gbig[l, 0] = _merge_cols(dwm_t.T, dwf_t[:H].T, f_off).T
        token = start_grads() if l > 0 else None
        dbm, dbf = _colsum(dzm, name=f"db_in_{l}"), _colsum(dzf, name=f"db_in_f_{l}")
        dx, dscale1, dshift1 = _ln_mod_bwd(du1, s["x"], scale1[l], dres1, cfg, name=f"ln_mod1_bwd_{l}")
        dmods[l] = jnp.concatenate([dshift1, dscale1, dgate1, dshift2, dscale2, dgate2], axis=1).reshape(Bl, 6 * D)
        gsm[l] = dict(b_in=_merge_cols(dbm[0], dbf[0, :H], f_off), conv_a_b=dcab[0], ln_conv_g=dlcg[0], ln_conv_b=dlcb[0],
                      b_mix_out=dbmo[0], ln1_g=dg1[0], ln1_b=db1[0], ffn_conv_b=jnp.concatenate([dcg[0], dcl[0]]),
                      ln2_g=dg2[0], ln2_b=db2[0], conv_a_w=dcaw[:KW], ffn_conv_w=jnp.concatenate([dwg[:KF], dwl[:KF]], axis=1))
    grad_x = dx.reshape(Bl, S, D)

    small_names = ["b_in", "conv_a_b", "ln_conv_g", "ln_conv_b", "b_mix_out", "ln1_g", "ln1_b", "ffn_conv_b", "ln2_g", "ln2_b",
                   "conv_a_w", "ffn_conv_w"]
    gs_list = [jnp.stack(dmods)] + [jnp.stack([gsm[l][n] for l in range(L)]) for n in small_names]
    gspack, gssegs = _pack(gs_list, D, F32)
    gs_all = _exchange(gspack, all_to_all=False, name="gather_small_grads")
    start_grads(after=gs_all)
    dmod_all = jnp.moveaxis(_unpack(gs_all, gssegs)[0], 0, 1).reshape(L, N_DEV * Bl, 6 * D)
    g_small = dict(zip(small_names, _unpack(_slot_sum(gs_all, name="sum_small_grads"), gssegs)[1:]))
    g_small["conv_a_w"] = my_cols(g_small["conv_a_w"], C // N_DEV)
    g_small["ffn_conv_w"] = my_cols(g_small["ffn_conv_w"], 2 * F // N_DEV)
    g_small["w_ada"] = _ada_bwd(c_all, my_cols(dmod_all, n_ada), name="ada_bwd")
    g_small["b_ada"] = jnp.stack([_colsum(dmod_all[l], name=f"db_ada_{l}")[0] for l in range(L)])

    given = dict(w_in=(w_in, m_w_in, v_w_in), w_conv_proj=(w_conv_proj, m_w_conv_proj, v_w_conv_proj),
                 w_attn_proj=(w_attn_proj, m_w_attn_proj, v_w_attn_proj), w_mix_out=(w_mix_out, m_w_mix_out, v_w_mix_out),
                 w_ffn_up=(w_ffn_up, m_w_ffn_up, v_w_ffn_up), w_ffn_down=(w_ffn_down, m_w_ffn_down, v_w_ffn_down),
                 w_ada=(w_ada, m_w_ada, v_w_ada), b_ada=(b_ada, m_b_ada, v_b_ada), b_in=(b_in, m_b_in, v_b_in),
                 conv_a_w=(conv_a_w, m_conv_a_w, v_conv_a_w), conv_a_b=(conv_a_b, m_conv_a_b, v_conv_a_b),
                 ln_conv_g=(ln_conv_g, m_ln_conv_g, v_ln_conv_g), ln_conv_b=(ln_conv_b, m_ln_conv_b, v_ln_conv_b),
                 b_mix_out=(b_mix_out, m_b_mix_out, v_b_mix_out), ln1_g=(ln1_g, m_ln1_g, v_ln1_g), ln1_b=(ln1_b, m_ln1_b, v_ln1_b),
                 ffn_conv_w=(ffn_conv_w, m_ffn_conv_w, v_ffn_conv_w), ffn_conv_b=(ffn_conv_b, m_ffn_conv_b, v_ffn_conv_b),
                 ln2_g=(ln2_g, m_ln2_g, v_ln2_g), ln2_b=(ln2_b, m_ln2_b, v_ln2_b))
    res, kinds = {}, ("grad", "delta", "new_m", "new_v")
    loc_names = ["b_ada"] + small_names
    deltas, new_ms, new_vs = _adamw_many([g_small[n] for n in loc_names], *([given[n][j] for n in loc_names] for j in range(3)),
                                         name="adamw_small")
    for n, d, m2, v2 in zip(loc_names, deltas, new_ms, new_vs):
        res["grad", n], res["delta", n], res["new_m", n], res["new_v", n] = g_small[n], d, m2, v2
    rows_ada = (L * D * n_ada // D, D)
    outs = _adamw(g_small["w_ada"].reshape((1,) + rows_ada), *(a.reshape(rows_ada) for a in given["w_ada"]), name="adamw_w_ada")
    for kind, a in zip(kinds, outs):
        res[kind, "w_ada"] = a.reshape(w_ada.shape)

    big_parts = {}
    after = outs[0]
    for gi, (state, send, segs, grp) in enumerate(g_state):
        landed = _exchange_wait(state, after, name=f"exchange_grads_wait_{gi}")
        wmv = [_pack(shard_items([given[n][j] for n in big_names], grp), D, F32)[0] for j in range(3)]
        outs = _adamw(landed, *wmv, name=f"adamw_big_{gi}")
        for kind, packed in zip(kinds, outs):
            for (l, i), a in zip(grp, _unpack(packed, segs)):
                big_parts[kind, l, i] = a.T if transposed[i] else a
        after = outs[0]
    for kind in kinds:
        for i, n in enumerate(big_names):
            res[kind, n] = jnp.stack([big_parts[kind, l, i] for l in range(L)])

    order = ["w_ada", "b_ada", "w_in", "b_in", "conv_a_w", "conv_a_b", "ln_conv_g", "ln_conv_b", "w_conv_proj", "w_attn_proj",
             "w_mix_out", "b_mix_out", "ln1_g", "ln1_b", "w_ffn_up", "ffn_conv_w", "ffn_conv_b", "w_ffn_down", "ln2_g", "ln2_b"]
    return (loss, grad_x, *[res[k, n] for k in ("grad", "delta", "new_m", "new_v") for n in order])
```

```python
import functools
import math
from typing import NamedTuple

import jax
import jax.numpy as jnp
from jax import lax
from jax.experimental import pallas as pl
from jax.experimental.pallas import tpu as pltpu

F32, BF16 = jnp.float32, jnp.bfloat16
LN_EPS = 1e-5
ADAM_LR, ADAM_B1, ADAM_B2, ADAM_EPS, ADAM_WD, ADAM_STEP = 0.001, 0.9, 0.999, 1e-08, 0.01, 10
N_DEV = 8
LANES = 128
VMEM_LIMIT = 56 * 1024 * 1024
NEG = -1e30
NT = (((1,), (1,)), ((), ()))
TN = (((0,), (0,)), ((), ()))


class Cfg(NamedTuple):
    L: int
    Bl: int
    S: int
    D: int
    C: int
    KW: int
    H: int
    Dh: int
    F: int
    KF: int

    @property
    def T(self): return self.Bl * self.S
    @property
    def AW(self): return self.H * self.Dh
    @property
    def NM(self): return 2 * self.C + 3 * self.AW + 2 * self.D
    @property
    def q_off(self): return 2 * self.C
    @property
    def g_off(self): return 2 * self.C + 3 * self.AW
    @property
    def alpha(self): return (2.0 * self.L) ** 0.25


def _pcall(body, **kw):
    return pl.pallas_call(body, **kw)


def _params(*sem):
    return pltpu.CompilerParams(dimension_semantics=sem, vmem_limit_bytes=VMEM_LIMIT)


def _pick(n, prefs):
    for p in prefs:
        if n % p == 0:
            return p
    return n


def _sigmoid(x):
    return 1.0 / (1.0 + jnp.exp(-x))


def _ln_stats(x):
    mu = jnp.mean(x, axis=-1, keepdims=True)
    xc = x - mu
    var = jnp.mean(xc * xc, axis=-1, keepdims=True)
    rstd = lax.rsqrt(var + LN_EPS)
    return xc * rstd, rstd


def _ln_bwd(dxh, xh, rstd):
    return rstd * (dxh - jnp.mean(dxh, axis=-1, keepdims=True) - xh * jnp.mean(dxh * xh, axis=-1, keepdims=True))


def _matmul(a, b, *, mode, name, bias=None, add=None, out_dtype=F32, tm=None, tn=None, tk=None, after=None):
    parts = tuple(a) if isinstance(a, (tuple, list)) else (a,)
    P = len(parts)
    if mode == "tn":
        K, Mp = parts[0].shape
        M, Kp = P * Mp, K
    else:
        M, Kp = parts[0].shape
        K, Mp = P * Kp, M
    N = b.shape[0] if mode == "nt" else b.shape[1]
    lane_tiles = (1536, 1408, 1024, 768, 512, 256, 128)
    tm = tm or _pick(Mp, lane_tiles if mode == "tn" else (1024, 512, 256, 128, 64, 32, 16, 8))
    tn = tn or _pick(N, lane_tiles)
    tk = tk or _pick(Kp, (1024, 512, 256, 128) if mode == "tn" else lane_tiles)
    nk = K // tk
    per = Mp // tm if mode == "tn" else Kp // tk
    dn = {"nn": (((1,), (0,)), ((), ())), "nt": NT, "tn": TN}[mode]
    has_bias, has_add, has_after = bias is not None, add is not None, after is not None

    def body(*refs):
        a_refs, b_ref = refs[:P], refs[P]
        pos = P + 1
        bias_ref = refs[pos] if has_bias else None
        pos += has_bias
        add_ref = refs[pos] if has_add else None
        pos += has_add + has_after
        o_ref = refs[pos]
        acc_ref = refs[pos + 1] if nk > 1 else None
        k = pl.program_id(2)

        def finish(acc):
            if has_bias:
                acc = acc + bias_ref[...]
            if has_add:
                acc = acc + add_ref[...]
            o_ref[...] = acc.astype(out_dtype)

        def accumulate(a_ref):
            part = lax.dot_general(a_ref[...], b_ref[...], dn, preferred_element_type=F32)
            if nk == 1:
                finish(part)
            else:
                @pl.when(k == 0)
                def _():
                    acc_ref[...] = part

                @pl.when(k > 0)
                def _():
                    acc_ref[...] += part

        if P == 1:
            accumulate(a_refs[0])
        else:
            step = pl.program_id(0 if mode == "tn" else 2)
            for p in range(P):
                pl.when(step // per == p)(functools.partial(accumulate, a_refs[p]))
        if nk > 1:
            @pl.when(k == nk - 1)
            def _():
                finish(acc_ref[...])

    def a_spec(p):
        if mode == "tn":
            return pl.BlockSpec((tk, tm), lambda i, j, k: (k, jnp.clip(i - p * per, 0, per - 1)))
        return pl.BlockSpec((tm, tk), lambda i, j, k: (i, jnp.clip(k - p * per, 0, per - 1)))

    b_spec = pl.BlockSpec((tn, tk), lambda i, j, k: (j, k)) if mode == "nt" else pl.BlockSpec((tk, tn), lambda i, j, k: (k, j))
    in_specs, args = [a_spec(p) for p in range(P)] + [b_spec], list(parts) + [b]
    if has_bias:
        in_specs.append(pl.BlockSpec((1, tn), lambda i, j, k: (0, j)))
        args.append(bias)
    if has_add:
        in_specs.append(pl.BlockSpec((tm, tn), lambda i, j, k: (i, j)))
        args.append(add)
    if has_after:
        in_specs.append(pl.BlockSpec(memory_space=pl.ANY))
        args.append(after)
    return _pcall(
        body, name=name, grid=(M // tm, N // tn, nk), in_specs=in_specs,
        out_specs=pl.BlockSpec((tm, tn), lambda i, j, k: (i, j)),
        out_shape=jax.ShapeDtypeStruct((M, N), out_dtype),
        scratch_shapes=[pltpu.VMEM((tm, tn), F32)] if nk > 1 else [],
        compiler_params=_params("parallel", "parallel", "arbitrary"),
    )(*args)


def _colsum(x, *, name):
    T, N = x.shape
    tr = _pick(T, (512, 256, 128, 64, 32, 16))
    tc = _pick(N, (1536, 1024, 512, 256, 128))

    def body(x_ref, o_ref):
        @pl.when(pl.program_id(1) == 0)
        def _():
            o_ref[...] = jnp.zeros_like(o_ref)

        o_ref[...] += jnp.sum(x_ref[...].astype(F32), axis=0, keepdims=True)

    return _pcall(body, name=name, grid=(N // tc, T // tr), in_specs=[pl.BlockSpec((tr, tc), lambda j, i: (i, j))],
                  out_specs=pl.BlockSpec((1, tc), lambda j, i: (0, j)), out_shape=jax.ShapeDtypeStruct((1, N), F32),
                  compiler_params=_params("parallel", "arbitrary"))(x)


def _row_tile(cfg):
    return _pick(cfg.S, (256, 128, 64, 32, 16, 8))


def _ln_mod_fwd(x, shift, scale, cfg, *, name):
    tr = _row_tile(cfg)
    tpb = cfg.S // tr

    def body(x_ref, sh_ref, sc_ref, u_ref):
        xh, _ = _ln_stats(x_ref[...])
        u_ref[...] = (xh * (1.0 + sc_ref[0]) + sh_ref[0]).astype(BF16)

    row = pl.BlockSpec((tr, cfg.D), lambda i: (i, 0))
    per_b = pl.BlockSpec((1, 1, cfg.D), lambda i: (i // tpb, 0, 0))
    return _pcall(body, name=name, grid=(cfg.T // tr,), in_specs=[row, per_b, per_b], out_specs=row,
                  out_shape=jax.ShapeDtypeStruct((cfg.T, cfg.D), BF16), compiler_params=_params("parallel"))(x, shift, scale)


def _res_ln_fwd(xin, br, gate, g, b, cfg, *, name, nxt=None):
    tr = _row_tile(cfg)
    tpb = cfg.S // tr
    alpha = cfg.alpha

    def body(*refs):
        x_ref, br_ref, gt_ref, g_ref, b_ref = refs[:5]
        r = alpha * x_ref[...] + (1.0 + gt_ref[0]) * br_ref[...]
        xh, _ = _ln_stats(r)
        xo = xh * g_ref[...] + b_ref[...]
        if nxt is None:
            refs[5][...] = xo
        else:
            sh_ref, sc_ref, xo_ref, u_ref = refs[5:]
            xo_ref[...] = xo
            uh, _ = _ln_stats(xo)
            u_ref[...] = (uh * (1.0 + sc_ref[0]) + sh_ref[0]).astype(BF16)

    row = pl.BlockSpec((tr, cfg.D), lambda i: (i, 0))
    per_b = pl.BlockSpec((1, 1, cfg.D), lambda i: (i // tpb, 0, 0))
    vec = pl.BlockSpec((1, cfg.D), lambda i: (0, 0))
    in_specs, args = [row, row, per_b, vec, vec], [xin, br, gate, g, b]
    out_specs, out_shape = row, jax.ShapeDtypeStruct((cfg.T, cfg.D), F32)
    if nxt is not None:
        in_specs += [per_b, per_b]
        args += list(nxt)
        out_specs = [row, row]
        out_shape = [out_shape, jax.ShapeDtypeStruct((cfg.T, cfg.D), BF16)]
    return _pcall(body, name=name, grid=(cfg.T // tr,), in_specs=in_specs, out_specs=out_specs, out_shape=out_shape,
                  compiler_params=_params("parallel"))(*args)


def _loss_grad(y, tgt, cfg, *, name):
    tr = _row_tile(cfg)
    nt = cfg.T // tr
    inv_d = 1.0 / cfg.D

    def body(y_ref, t_ref, dy_ref, ls_ref):
        e = y_ref[...] - t_ref[...]
        dy_ref[...] = e * inv_d
        ls_ref[...] = jnp.full((1, 1, LANES), jnp.sum(e * e), F32)

    row = pl.BlockSpec((tr, cfg.D), lambda i: (i, 0))
    return _pcall(body, name=name, grid=(nt,), in_specs=[row, row],
                  out_specs=[row, pl.BlockSpec((1, 1, LANES), lambda i: (i, 0, 0))],
                  out_shape=[jax.ShapeDtypeStruct((cfg.T, cfg.D), F32), jax.ShapeDtypeStruct((nt, 1, LANES), F32)],
                  compiler_params=_params("parallel"))(y, tgt)


def _res_ln_bwd(dy, xin, br, gate, g, cfg, *, name):
    tr = _row_tile(cfg)
    tpb = cfg.S // tr
    alpha = cfg.alpha

    def body(dy_ref, x_ref, br_ref, gt_ref, g_ref, dx_ref, dbr_ref, dg_ref, db_ref, dgt_ref, dbs_ref):
        i = pl.program_id(0)

        @pl.when(i == 0)
        def _():
            dg_ref[...] = jnp.zeros_like(dg_ref)
            db_ref[...] = jnp.zeros_like(db_ref)
            dbs_ref[...] = jnp.zeros_like(dbs_ref)

        @pl.when(i % tpb == 0)
        def _():
            dgt_ref[...] = jnp.zeros_like(dgt_ref)

        dy, brv, one_gate = dy_ref[...], br_ref[...], 1.0 + gt_ref[0]
        xh, rstd = _ln_stats(alpha * x_ref[...] + one_gate * brv)
        dg_ref[...] += jnp.sum(dy * xh, axis=0, keepdims=True)
        db_ref[...] += jnp.sum(dy, axis=0, keepdims=True)
        dr = _ln_bwd(dy * g_ref[...], xh, rstd)
        dx_ref[...] = alpha * dr
        dbr = one_gate * dr
        dbr_ref[...] = dbr.astype(BF16)
        dbs_ref[...] += jnp.sum(dbr, axis=0, keepdims=True)
        dgt_ref[0] += jnp.sum(dr * brv, axis=0, keepdims=True)

    row = pl.BlockSpec((tr, cfg.D), lambda i: (i, 0))
    per_b = pl.BlockSpec((1, 1, cfg.D), lambda i: (i // tpb, 0, 0))
    vec = pl.BlockSpec((1, cfg.D), lambda i: (0, 0))
    vs = jax.ShapeDtypeStruct((1, cfg.D), F32)
    return _pcall(body, name=name, grid=(cfg.T // tr,), in_specs=[row, row, row, per_b, vec],
                  out_specs=[row, row, vec, vec, per_b, vec],
                  out_shape=[jax.ShapeDtypeStruct((cfg.T, cfg.D), F32), jax.ShapeDtypeStruct((cfg.T, cfg.D), BF16), vs, vs,
                             jax.ShapeDtypeStruct((cfg.Bl, 1, cfg.D), F32), vs],
                  compiler_params=_params("arbitrary"))(dy, xin, br, gate, g)


def _ln_mod_bwd(du, xin, scale, dres, cfg, *, name):
    tr = _row_tile(cfg)
    tpb = cfg.S // tr

    def body(du_ref, x_ref, sc_ref, dres_ref, dx_ref, dsc_ref, dsh_ref):
        @pl.when(pl.program_id(0) % tpb == 0)
        def _():
            dsc_ref[...] = jnp.zeros_like(dsc_ref)
            dsh_ref[...] = jnp.zeros_like(dsh_ref)

        du = du_ref[...]
        xh, rstd = _ln_stats(x_ref[...])
        dsc_ref[0] += jnp.sum(du * xh, axis=0, keepdims=True)
        dsh_ref[0] += jnp.sum(du, axis=0, keepdims=True)
        dx_ref[...] = _ln_bwd(du * (1.0 + sc_ref[0]), xh, rstd) + dres_ref[...]

    row = pl.BlockSpec((tr, cfg.D), lambda i: (i, 0))
    per_b = pl.BlockSpec((1, 1, cfg.D), lambda i: (i // tpb, 0, 0))
    bs = jax.ShapeDtypeStruct((cfg.Bl, 1, cfg.D), F32)
    return _pcall(body, name=name, grid=(cfg.T // tr,), in_specs=[row, row, per_b, row], out_specs=[row, per_b, per_b],
                  out_shape=[jax.ShapeDtypeStruct((cfg.T, cfg.D), F32), bs, bs],
                  compiler_params=_params("arbitrary"))(du, xin, scale, dres)


def _merge_tiles(cfg):
    tr = _pick(cfg.T, (512, 256, 128, 64, 32, 16))
    tc = _pick(math.gcd(cfg.g_off, cfg.D), (512, 256, 128))
    return tr, tc


def _merge_fwd(zm, ya, yb, cfg, *, name):
    tr, tc = _merge_tiles(cfg)
    ga0, gb0 = cfg.g_off // tc, (cfg.g_off + cfg.D) // tc

    def body(ga_ref, gb_ref, ya_ref, yb_ref, m_ref):
        m_ref[...] = (_sigmoid(ga_ref[...]) * ya_ref[...] + _sigmoid(gb_ref[...]) * yb_ref[...]).astype(BF16)

    blk = pl.BlockSpec((tr, tc), lambda i, j: (i, j))
    return _pcall(body, name=name, grid=(cfg.T // tr, cfg.D // tc),
                  in_specs=[pl.BlockSpec((tr, tc), lambda i, j: (i, ga0 + j)), pl.BlockSpec((tr, tc), lambda i, j: (i, gb0 + j)), blk, blk],
                  out_specs=blk, out_shape=jax.ShapeDtypeStruct((cfg.T, cfg.D), BF16),
                  compiler_params=_params("parallel", "parallel"))(zm, zm, ya, yb)


def _merge_bwd(dm, zm, ya, yb, cfg, *, name):
    tr, tc = _merge_tiles(cfg)
    ga0, gb0 = cfg.g_off // tc, (cfg.g_off + cfg.D) // tc

    def body(dm_ref, ga_ref, gb_ref, ya_ref, yb_ref, dya_ref, dyb_ref, dga_ref, dgb_ref):
        dm = dm_ref[...]
        ga, gb = _sigmoid(ga_ref[...]), _sigmoid(gb_ref[...])
        dya_ref[...] = (dm * ga).astype(BF16)
        dyb_ref[...] = (dm * gb).astype(BF16)
        dga_ref[...] = (dm * ya_ref[...] * ga * (1.0 - ga)).astype(BF16)
        dgb_ref[...] = (dm * yb_ref[...] * gb * (1.0 - gb)).astype(BF16)

    blk = pl.BlockSpec((tr, tc), lambda i, j: (i, j))
    o = jax.ShapeDtypeStruct((cfg.T, cfg.D), BF16)
    return _pcall(body, name=name, grid=(cfg.T // tr, cfg.D // tc),
                  in_specs=[blk, pl.BlockSpec((tr, tc), lambda i, j: (i, ga0 + j)), pl.BlockSpec((tr, tc), lambda i, j: (i, gb0 + j)), blk, blk],
                  out_specs=[blk] * 4, out_shape=[o] * 4, compiler_params=_params("parallel", "parallel"))(dm, zm, zm, ya, yb)


CONV_A_HALO = 32
CONV_A_CHUNK = 32


SUBLANES = 8


def _conv_a_tile(cfg):
    assert cfg.KW - 1 <= CONV_A_HALO
    return _pick(cfg.S, (256, 128, 64, 32))


def _shift_copies(src_s, sh_s):
    rows = src_s.shape[0] - SUBLANES
    for b in range(1, SUBLANES):
        sh_s[b - 1, :, :] = src_s[b:b + rows, :]


def _rows(src_s, sh_s, start, n):
    a, b = divmod(start, SUBLANES)
    return src_s[start:start + n, :] if b == 0 else sh_s[b - 1, SUBLANES * a:SUBLANES * a + n, :]


def _conv_a_fwd(zm, w, cb, g, b, cfg, *, name):
    C, KW, HALO, CH = cfg.C, cfg.KW, CONV_A_HALO, CONV_A_CHUNK
    ts = _conv_a_tile(cfg)
    tpb = cfg.S // ts
    lead = HALO - (KW - 1)

    def body(z_ref, zp_ref, w_ref, cb_ref, g_ref, b_ref, o_ref, a0_s, a0_sh):
        first = pl.program_id(0) % tpb == 0
        prev = zp_ref[:, :C] * _sigmoid(zp_ref[:, C:])
        a0_s[0:HALO, :] = jnp.where(first, 0.0, prev)
        a0_s[HALO:HALO + ts, :] = z_ref[:, :C] * _sigmoid(z_ref[:, C:])
        _shift_copies(a0_s, a0_sh)
        for r0 in range(0, ts, CH):
            acc = jnp.zeros((CH, C), F32)
            for k in range(KW):
                acc = acc + w_ref[k:k + 1, :] * _rows(a0_s, a0_sh, r0 + lead + k, CH)
            xh, _ = _ln_stats(acc + cb_ref[...])
            a2 = xh * g_ref[...] + b_ref[...]
            o_ref[r0:r0 + CH, :] = (a2 * _sigmoid(a2)).astype(BF16)

    hb = ts // HALO
    vec = pl.BlockSpec((1, C), lambda i: (0, 0))
    return _pcall(body, name=name, grid=(cfg.T // ts,),
                  in_specs=[pl.BlockSpec((ts, 2 * C), lambda i: (i, 0)),
                            pl.BlockSpec((HALO, 2 * C), lambda i: (jnp.maximum(i * hb - 1, 0), 0)),
                            pl.BlockSpec((32, C), lambda i: (0, 0)), vec, vec, vec],
                  out_specs=pl.BlockSpec((ts, C), lambda i: (i, 0)), out_shape=jax.ShapeDtypeStruct((cfg.T, C), BF16),
                  scratch_shapes=[pltpu.VMEM((HALO + ts, C), F32), pltpu.VMEM((SUBLANES - 1, HALO + ts - SUBLANES, C), F32)],
                  compiler_params=_params("parallel"))(zm, zm, w, cb, g, b)


def _conv_a_bwd(da3, zm, w, cb, g, b, cfg, *, name):
    C, KW, HALO, CH = cfg.C, cfg.KW, CONV_A_HALO, CONV_A_CHUNK
    ts = _conv_a_tile(cfg)
    tpb = cfg.S // ts
    nt = cfg.T // ts
    lead = HALO - (KW - 1)
    ext = ts + HALO

    def body(z_ref, zp_ref, zn_ref, d_ref, dn_ref, w_ref, cb_ref, g_ref, b_ref,
             dz_ref, dw_ref, dcb_ref, dg_ref, db_ref, a0_s, d3_s, da1_s, a0_sh, da1_sh):
        i = pl.program_id(0)
        first, last = i % tpb == 0, i % tpb == tpb - 1

        @pl.when(i == 0)
        def _():
            dw_ref[...] = jnp.zeros_like(dw_ref)
            dcb_ref[...] = jnp.zeros_like(dcb_ref)
            dg_ref[...] = jnp.zeros_like(dg_ref)
            db_ref[...] = jnp.zeros_like(db_ref)

        a0_s[0:HALO, :] = jnp.where(first, 0.0, zp_ref[:, :C] * _sigmoid(zp_ref[:, C:]))
        a0_s[HALO:HALO + ts, :] = z_ref[:, :C] * _sigmoid(z_ref[:, C:])
        a0_s[HALO + ts:HALO + ext, :] = zn_ref[:, :C] * _sigmoid(zn_ref[:, C:])
        d3_s[0:ts, :] = d_ref[...]
        d3_s[ts:ext, :] = jnp.where(last, 0.0, dn_ref[...])
        _shift_copies(a0_s, a0_sh)
        dcb, dg, db = jnp.zeros((1, C), F32), jnp.zeros((1, C), F32), jnp.zeros((1, C), F32)
        for r0 in range(0, ext, CH):
            acc = jnp.zeros((CH, C), F32)
            for k in range(KW):
                acc = acc + w_ref[k:k + 1, :] * _rows(a0_s, a0_sh, r0 + lead + k, CH)
            xh, rstd = _ln_stats(acc + cb_ref[...])
            a2 = xh * g_ref[...] + b_ref[...]
            sg = _sigmoid(a2)
            da2 = d3_s[r0:r0 + CH, :] * (sg * (1.0 + a2 * (1.0 - sg)))
            da1 = _ln_bwd(da2 * g_ref[...], xh, rstd)
            da1_s[r0:r0 + CH, :] = da1
            if r0 < ts:
                dg = dg + jnp.sum(da2 * xh, axis=0, keepdims=True)
                db = db + jnp.sum(da2, axis=0, keepdims=True)
                dcb = dcb + jnp.sum(da1, axis=0, keepdims=True)
        dg_ref[...] += dg
        db_ref[...] += db
        dcb_ref[...] += dcb
        _shift_copies(da1_s, da1_sh)
        for k in range(KW):
            dwk = jnp.zeros((CH, C), F32)
            for r0 in range(0, ts, CH):
                dwk = dwk + da1_s[r0:r0 + CH, :] * _rows(a0_s, a0_sh, r0 + lead + k, CH)
            dw_ref[k:k + 1, :] += jnp.sum(dwk, axis=0, keepdims=True)
        for r0 in range(0, ts, CH):
            da0 = jnp.zeros((CH, C), F32)
            for k in range(KW):
                da0 = da0 + w_ref[k:k + 1, :] * _rows(da1_s, da1_sh, r0 + KW - 1 - k, CH)
            val, sg = z_ref[r0:r0 + CH, :C], _sigmoid(z_ref[r0:r0 + CH, C:])
            dz_ref[r0:r0 + CH, :C] = (da0 * sg).astype(BF16)
            dz_ref[r0:r0 + CH, C:] = (da0 * val * sg * (1.0 - sg)).astype(BF16)

    hb = ts // HALO
    nhb = cfg.T // HALO
    vec = pl.BlockSpec((1, C), lambda i: (0, 0))
    vs = jax.ShapeDtypeStruct((1, C), F32)
    return _pcall(body, name=name, grid=(nt,),
                  in_specs=[pl.BlockSpec((ts, 2 * C), lambda i: (i, 0)),
                            pl.BlockSpec((HALO, 2 * C), lambda i: (jnp.maximum(i * hb - 1, 0), 0)),
                            pl.BlockSpec((HALO, 2 * C), lambda i: (jnp.minimum((i + 1) * hb, nhb - 1), 0)),
                            pl.BlockSpec((ts, C), lambda i: (i, 0)),
                            pl.BlockSpec((HALO, C), lambda i: (jnp.minimum((i + 1) * hb, nhb - 1), 0)),
                            pl.BlockSpec((32, C), lambda i: (0, 0)), vec, vec, vec],
                  out_specs=[pl.BlockSpec((ts, 2 * C), lambda i: (i, 0)), pl.BlockSpec((32, C), lambda i: (0, 0)), vec, vec, vec],
                  out_shape=[jax.ShapeDtypeStruct((cfg.T, 2 * C), BF16), jax.ShapeDtypeStruct((32, C), F32), vs, vs, vs],
                  scratch_shapes=[pltpu.VMEM((HALO + ext, C), F32), pltpu.VMEM((ext, C), F32), pltpu.VMEM((ext, C), F32),
                                  pltpu.VMEM((SUBLANES - 1, HALO + ext - SUBLANES, C), F32),
                                  pltpu.VMEM((SUBLANES - 1, ext - SUBLANES, C), F32)],
                  compiler_params=_params("arbitrary"))(zm, zm, zm, da3, da3, w, cb, g, b)


def _cum_tile(cfg):
    return _pick(cfg.S, (256, 128, 64, 32, 16, 8))


def _fgate_fwd(zf, cfg, *, name):
    tc = _cum_tile(cfg)
    tpb = cfg.S // tc
    hp = _attn_tiles(cfg)[2]
    nb = cfg.H // hp

    def body(z_ref, o_ref, carry):
        @pl.when(pl.program_id(0) % tpb == 0)
        def _():
            carry[...] = jnp.zeros_like(carry)

        z = z_ref[...]
        logf = jnp.minimum(z, 0.0) - jnp.log(1.0 + jnp.exp(-jnp.abs(z)))
        tri = (lax.broadcasted_iota(jnp.int32, (tc, tc), 0) >= lax.broadcasted_iota(jnp.int32, (tc, tc), 1)).astype(F32)
        cum = jnp.dot(tri, logf, precision=lax.Precision.HIGHEST, preferred_element_type=F32) + carry[...]
        carry[...] = cum[tc - 1:tc, :]
        o_ref[0] = cum
        for b in range(1, nb):
            o_ref[b] = pltpu.roll(cum, LANES - hp * b, axis=1)

    return _pcall(body, name=name, grid=(cfg.T // tc,), in_specs=[pl.BlockSpec((tc, LANES), lambda i: (i, 0))],
                  out_specs=pl.BlockSpec((nb, tc, LANES), lambda i: (0, i, 0)),
                  out_shape=jax.ShapeDtypeStruct((nb, cfg.T, LANES), F32), scratch_shapes=[pltpu.VMEM((1, LANES), F32)],
                  compiler_params=_params("arbitrary"))(zf)


def _fgate_bwd(dcum_c, zf, cfg, *, name):
    tc = _cum_tile(cfg)
    tpb = cfg.S // tc
    nt = cfg.T // tc
    hp = _attn_tiles(cfg)[2]
    nb = cfg.H // hp

    def body(d_ref, z_ref, o_ref, carry):
        @pl.when(pl.program_id(0) % tpb == 0)
        def _():
            carry[...] = jnp.zeros_like(carry)

        d = d_ref[0]
        for b in range(1, nb):
            d = d + pltpu.roll(d_ref[b], hp * b, axis=1)
        tri = (lax.broadcasted_iota(jnp.int32, (tc, tc), 0) <= lax.broadcasted_iota(jnp.int32, (tc, tc), 1)).astype(F32)
        suf = jnp.dot(tri, d, precision=lax.Precision.HIGHEST, preferred_element_type=F32) + carry[...]
        o_ref[...] = (suf * _sigmoid(-z_ref[...])).astype(BF16)
        carry[...] = suf[0:1, :]

    blk = pl.BlockSpec((tc, LANES), lambda i: (nt - 1 - i, 0))
    return _pcall(body, name=name, grid=(nt,), in_specs=[pl.BlockSpec((nb, tc, LANES), lambda i: (0, nt - 1 - i, 0)), blk],
                  out_specs=blk, out_shape=jax.ShapeDtypeStruct((cfg.T, LANES), BF16),
                  scratch_shapes=[pltpu.VMEM((1, LANES), F32)], compiler_params=_params("arbitrary"))(dcum_c, zf)


def _attn_tiles(cfg):
    assert LANES % cfg.Dh == 0 and cfg.H % (LANES // cfg.Dh) == 0
    tk = _pick(cfg.S, (256, 128))
    tq = _pick(cfg.S, (2 * tk, tk))
    return tq, tk, LANES // cfg.Dh


BIAS_LANES = 3


def _head_lanes(hd, cfg, hp):
    li = lax.broadcasted_iota(jnp.int32, (1, LANES), 1)
    own = (li >= hd * cfg.Dh) & (li < (hd + 1) * cfg.Dh)
    return own, li, ((hd + 1) % hp) * cfg.Dh


def _q_aug(q, hd, cfg, hp):
    own, li, b0 = _head_lanes(hd, cfg, hp)
    ones = ((li >= b0) & (li < b0 + BIAS_LANES)).astype(F32)
    return jnp.where(own, q * cfg.Dh ** -0.5, ones).astype(BF16)


def _k_aug(k, ck, hd, cfg, hp):
    own, li, b0 = _head_lanes(hd, cfg, hp)
    hi = ck.astype(BF16).astype(F32)
    mid = (ck - hi).astype(BF16).astype(F32)
    lo = ck - hi - mid
    bias = jnp.where(li == b0, -hi, jnp.where(li == b0 + 1, -mid, jnp.where(li == b0 + 2, -lo, 0.0)))
    return jnp.where(own, k, bias).astype(BF16)


def _attn_fwd(zm, cum_c, cfg, *, name):
    S, Dh = cfg.S, cfg.Dh
    tq, tk, hp = _attn_tiles(cfg)
    assert hp >= 2
    nq, nb, per = S // tq, cfg.H // hp, tq // tk
    qb, kb, vb = cfg.q_off // LANES, (cfg.q_off + cfg.AW) // LANES, (cfg.q_off + 2 * cfg.AW) // LANES

    def body(q_ref, k_ref, v_ref, cc_ref, o_ref, o32_ref, lse_ref, ka_s, vt_s):
        qi = pl.program_id(2)

        @pl.when(qi == 0)
        def _():
            def prep(c, _):
                r = pl.multiple_of(c * tk, tk)
                kc = k_ref[pl.ds(r, tk), :]
                for hd in range(hp):
                    ka_s[hd, pl.ds(r, tk), :] = _k_aug(kc, cc_ref[0, pl.ds(r, tk), hd:hd + 1], hd, cfg, hp)
                vt_s[:, pl.ds(r, tk)] = v_ref[pl.ds(r, tk), :].T.astype(BF16)
                return 0

            lax.fori_loop(0, S // tk, prep, 0)

        key_i = lax.broadcasted_iota(jnp.int32, (tk, tq), 0)
        qry_i = lax.broadcasted_iota(jnp.int32, (tk, tq), 1)
        qf = q_ref[...]
        qa = [_q_aug(qf, hd, cfg, hp) for hd in range(hp)]

        def scores(j):
            r = pl.multiple_of(j * tk, tk)
            return tuple(lax.dot_general(ka_s[hd, pl.ds(r, tk), :], qa[hd], NT, preferred_element_type=F32) for hd in range(hp))

        def chunk(j, s_all, carry, diag=None):
            r = pl.multiple_of(j * tk, tk)
            new = []
            for hd in range(hp):
                m, l, acc = carry[hd]
                s = s_all[hd]
                if diag is not None:
                    s = jnp.where(key_i + diag * tk <= qry_i, s, NEG)
                m_new = jnp.maximum(m, jnp.max(s, axis=0, keepdims=True))
                a = jnp.exp(m - m_new)
                p = jnp.exp(s - m_new)
                l = a * l + jnp.sum(p, axis=0, keepdims=True)
                p_hi = p.astype(BF16)
                p_lo = (p - p_hi.astype(F32)).astype(BF16)
                vt = vt_s[hd * Dh:(hd + 1) * Dh, pl.ds(r, tk)]
                acc = a * acc + (jnp.dot(vt, p_hi, preferred_element_type=F32) + jnp.dot(vt, p_lo, preferred_element_type=F32))
                new.append((m_new, l, acc))
            return tuple(new)

        init = tuple((jnp.full((1, tq), NEG, F32), jnp.zeros((1, tq), F32), jnp.zeros((Dh, tq), F32)) for _ in range(hp))
        n_full = qi * per

        def step(j, c):
            stats, s_cur = c
            s_next = scores(j + 1)
            return chunk(j, s_cur, stats), s_next

        res, s_cur = lax.fori_loop(0, n_full, step, (init, scores(0)))
        for d in range(per):
            s_next = scores(n_full + d + 1) if d + 1 < per else None
            res = chunk(n_full + d, s_cur, res, diag=d)
            s_cur = s_next
        o = jnp.concatenate([acc / l for _, l, acc in res], axis=0).T
        o_ref[...] = o.astype(BF16)
        o32_ref[...] = o
        lse_ref[...] = jnp.zeros_like(lse_ref)
        for hd in range(hp):
            lse_ref[0, 0, hd:hd + 1, :] = res[hd][0] + jnp.log(res[hd][1])

    return _pcall(body, name=name, grid=(cfg.Bl, nb, nq),
                  in_specs=[pl.BlockSpec((tq, LANES), lambda b, h, i: (b * nq + i, qb + h)),
                            pl.BlockSpec((S, LANES), lambda b, h, i: (b, kb + h)),
                            pl.BlockSpec((S, LANES), lambda b, h, i: (b, vb + h)),
                            pl.BlockSpec((1, S, LANES), lambda b, h, i: (h, b, 0))],
                  out_specs=[pl.BlockSpec((tq, LANES), lambda b, h, i: (b * nq + i, h)),
                             pl.BlockSpec((tq, LANES), lambda b, h, i: (b * nq + i, h)),
                             pl.BlockSpec((1, 1, 8, tq), lambda b, h, i: (b, h, 0, i))],
                  out_shape=[jax.ShapeDtypeStruct((cfg.T, cfg.AW), BF16), jax.ShapeDtypeStruct((cfg.T, cfg.AW), F32),
                             jax.ShapeDtypeStruct((cfg.Bl, nb, 8, S), F32)],
                  scratch_shapes=[pltpu.VMEM((hp, S, LANES), BF16), pltpu.VMEM((LANES, S), BF16)],
                  compiler_params=_params("parallel", "parallel", "arbitrary"))(zm, zm, zm, cum_c)


def _attn_bwd(zm, cum_c, o, do, lse, cfg, *, name):
    S, Dh = cfg.S, cfg.Dh
    tq, t, hp = _attn_tiles(cfg)
    nq, nk, nb, per = S // tq, S // t, cfg.H // hp, tq // t
    qb, kb, vb = cfg.q_off // LANES, (cfg.q_off + cfg.AW) // LANES, (cfg.q_off + 2 * cfg.AW) // LANES
    scale = Dh ** -0.5

    def body(q_ref, k_ref, v_ref, cc_ref, o_ref, do_ref, lse_ref, dq_ref, dk_ref, dv_ref, dcc_ref,
             ka_s, qa_s, vz_s, kt_s, dd_s, dqt_s):
        li = lax.broadcasted_iota(jnp.int32, (1, LANES), 1)
        ri = lax.broadcasted_iota(jnp.int32, (LANES, 1), 0)
        key_i = lax.broadcasted_iota(jnp.int32, (t, tq), 0)
        qry_i = lax.broadcasted_iota(jnp.int32, (t, tq), 1)

        def prep(c, _):
            r = pl.multiple_of(c * t, t)
            kc, vc, qc = k_ref[pl.ds(r, t), :], v_ref[pl.ds(r, t), :], q_ref[pl.ds(r, t), :]
            prod_t = (do_ref[pl.ds(r, t), :].astype(F32) * o_ref[pl.ds(r, t), :].astype(F32)).T
            for hd in range(hp):
                own = _head_lanes(hd, cfg, hp)[0]
                ka_s[hd, pl.ds(r, t), :] = _k_aug(kc, cc_ref[0, pl.ds(r, t), hd:hd + 1], hd, cfg, hp)
                qa_s[hd, pl.ds(r, t), :] = _q_aug(qc, hd, cfg, hp)
                vz_s[hd, pl.ds(r, t), :] = jnp.where(own, vc, 0.0).astype(BF16)
                dd_s[hd:hd + 1, pl.ds(r, t)] = jnp.sum(prod_t[hd * Dh:(hd + 1) * Dh, :], axis=0, keepdims=True)
            kt_s[:, pl.ds(r, t)] = kc.T.astype(BF16)
            dqt_s[:, pl.ds(r, t)] = jnp.zeros((LANES, t), F32)
            return 0

        lax.fori_loop(0, nk, prep, 0)

        def kv_step(j, _):
            rk = pl.multiple_of(j * t, t)
            i0 = j // per

            def tile(i, carry, masked):
                rq = pl.multiple_of(i * tq, tq)
                dob = do_ref[pl.ds(rq, tq), :]
                new, dq_t = [], None
                for hd in range(hp):
                    dk_h, dv_h, dsum_h = carry[hd]
                    qa = qa_s[hd, pl.ds(rq, tq), :]
                    s = lax.dot_general(ka_s[hd, pl.ds(rk, t), :], qa, NT, preferred_element_type=F32)
                    p = jnp.exp(s - lse_ref[0, 0, hd:hd + 1, pl.ds(rq, tq)])
                    if masked:
                        p = jnp.where(key_i + (rk - rq) <= qry_i, p, 0.0)
                    dp = lax.dot_general(vz_s[hd, pl.ds(rk, t), :], dob, NT, preferred_element_type=F32)
                    ds = p * (dp - dd_s[hd:hd + 1, pl.ds(rq, tq)])
                    dsb = ds.astype(BF16)
                    dv_h = dv_h + jnp.dot(p.astype(BF16), dob, preferred_element_type=F32)
                    dk_h = dk_h + jnp.dot(dsb, qa, preferred_element_type=F32)
                    dq_h = jnp.dot(kt_s[:, pl.ds(rk, t)], dsb, preferred_element_type=F32)
                    dq_t = dq_h if hd == 0 else jnp.where((ri >= hd * Dh) & (ri < (hd + 1) * Dh), dq_h, dq_t)
                    for c0 in range(0, tq, LANES):
                        dsum_h = dsum_h + ds[:, c0:c0 + LANES]
                    new.append((dk_h, dv_h, dsum_h))
                dqt_s[:, pl.ds(rq, tq)] += dq_t * scale
                return tuple(new)

            zero = tuple((jnp.zeros((t, LANES), F32),) * 3 for _ in range(hp))
            res = lax.fori_loop(i0 + 1, nq, functools.partial(tile, masked=False), tile(i0, zero, True))
            dk, dv, dcc = res[0][0], res[0][1], jnp.zeros((t, LANES), F32)
            for hd in range(hp):
                own = _head_lanes(hd, cfg, hp)[0]
                if hd > 0:
                    dk, dv = jnp.where(own, res[hd][0], dk), jnp.where(own, res[hd][1], dv)
                dcc = dcc + jnp.where(li == hd, -jnp.sum(res[hd][2], axis=1, keepdims=True), 0.0)
            dk_ref[pl.ds(rk, t), :] = dk.astype(BF16)
            dv_ref[pl.ds(rk, t), :] = dv.astype(BF16)
            dcc_ref[0, pl.ds(rk, t), :] = dcc
            return 0

        lax.fori_loop(0, nk, kv_step, 0)

        def finish(c, _):
            r = pl.multiple_of(c * t, t)
            dq_ref[pl.ds(r, t), :] = dqt_s[:, pl.ds(r, t)].T.astype(BF16)
            return 0

        lax.fori_loop(0, nk, finish, 0)

    blk = pl.BlockSpec((S, LANES), lambda b, h: (b, h))
    cc = pl.BlockSpec((1, S, LANES), lambda b, h: (h, b, 0))
    os_ = jax.ShapeDtypeStruct((cfg.T, cfg.AW), BF16)
    return _pcall(body, name=name, grid=(cfg.Bl, nb),
                  in_specs=[pl.BlockSpec((S, LANES), lambda b, h: (b, qb + h)), pl.BlockSpec((S, LANES), lambda b, h: (b, kb + h)),
                            pl.BlockSpec((S, LANES), lambda b, h: (b, vb + h)), cc, blk, blk,
                            pl.BlockSpec((1, 1, 8, S), lambda b, h: (b, h, 0, 0))],
                  out_specs=[blk, blk, blk, cc],
                  out_shape=[os_, os_, os_, jax.ShapeDtypeStruct((nb, cfg.T, LANES), F32)],
                  scratch_shapes=[pltpu.VMEM((hp, S, LANES), BF16)] * 3 + [pltpu.VMEM((LANES, S), BF16),
                                  pltpu.VMEM((8, S), F32), pltpu.VMEM((LANES, S), F32)],
                  compiler_params=_params("parallel", "parallel"))(zm, zm, zm, cum_c, o, do, lse)


FFN_HALO = 8
FFN_CHUNK = 16


def _ffn_tiles(cfg):
    assert cfg.KF - 1 <= FFN_HALO
    return _pick(cfg.S, (512, 256, 128, 64, 32, 16, 8)), _pick(cfg.F, (256, 128))


def _gelu(x):
    return 0.5 * x * (1.0 + lax.erf(x * (2.0 ** -0.5)))


def _gelu_grad(x):
    return 0.5 * (1.0 + lax.erf(x * (2.0 ** -0.5))) + x * jnp.exp(-0.5 * x * x) * ((2.0 * math.pi) ** -0.5)


def _ffn_conv_fwd(h0, w, cb, cfg, *, name):
    KF, HALO = cfg.KF, FFN_HALO
    ts, tf = _ffn_tiles(cfg)
    tpb, nf = cfg.S // ts, cfg.F // tf
    lead = HALO - (KF - 1)

    CH = FFN_CHUNK

    def body(g_ref, gp_ref, l_ref, lp_ref, wg_ref, wl_ref, cg_ref, cl_ref, o_ref, hg_ref, hl_ref, g_s, l_s):
        first = pl.program_id(1) % tpb == 0
        for s, main, prev in ((g_s, g_ref, gp_ref), (l_s, l_ref, lp_ref)):
            s[0:HALO, :] = jnp.where(first, 0.0, prev[...])
            s[HALO:HALO + CH, :] = main[0:CH, :]
        wg, wl = [wg_ref[k:k + 1, :] for k in range(KF)], [wl_ref[k:k + 1, :] for k in range(KF)]
        for r0 in range(0, ts, CH):
            hg, hl = cg_ref[...], cl_ref[...]
            for k in range(KF):
                if r0 == 0:
                    xg, xl = g_s[lead + k:lead + k + CH, :], l_s[lead + k:lead + k + CH, :]
                else:
                    a = r0 - (KF - 1) + k
                    xg, xl = g_ref[a:a + CH, :], l_ref[a:a + CH, :]
                hg, hl = hg + wg[k] * xg, hl + wl[k] * xl
            o_ref[r0:r0 + CH, :] = (_gelu(hg) * hl).astype(BF16)
            hg_ref[r0:r0 + CH, :], hl_ref[r0:r0 + CH, :] = hg, hl

    hb = ts // HALO
    prev = lambda off: pl.BlockSpec((HALO, tf), lambda j, i: (jnp.maximum(i * hb - 1, 0), off + j))
    main = lambda off: pl.BlockSpec((ts, tf), lambda j, i: (i, off + j))
    wsp = lambda off: pl.BlockSpec((8, tf), lambda j, i: (0, off + j))
    vsp = lambda off: pl.BlockSpec((1, tf), lambda j, i: (0, off + j))
    hs = jax.ShapeDtypeStruct((cfg.T, cfg.F), F32)
    return _pcall(body, name=name, grid=(nf, cfg.T // ts),
                  in_specs=[main(0), prev(0), main(nf), prev(nf), wsp(0), wsp(nf), vsp(0), vsp(nf)],
                  out_specs=[main(0)] * 3, out_shape=[jax.ShapeDtypeStruct((cfg.T, cfg.F), BF16), hs, hs],
                  scratch_shapes=[pltpu.VMEM((HALO + CH, tf), F32)] * 2,
                  compiler_params=_params("parallel", "parallel"))(h0, h0, h0, h0, w, w, cb, cb)


def _ffn_conv_bwd(df, h0, hg, hl, w, cfg, *, name):
    KF, HALO = cfg.KF, FFN_HALO
    ts, tf = _ffn_tiles(cfg)
    tpb, nf = cfg.S // ts, cfg.F // tf
    ext = ts + HALO

    CH = FFN_CHUNK

    def body(g_ref, l_ref, hg_ref, hgn_ref, hl_ref, hln_ref, d_ref, dn_ref, wg_ref, wl_ref,
             dg_ref, dl_ref, dwg_ref, dwl_ref, dcg_ref, dcl_ref, dhg_s, dhl_s):
        i = pl.program_id(1)
        last = i % tpb == tpb - 1

        @pl.when(i == 0)
        def _():
            dwg_ref[...] = jnp.zeros_like(dwg_ref)
            dwl_ref[...] = jnp.zeros_like(dwl_ref)
            dcg_ref[...] = jnp.zeros_like(dcg_ref)
            dcl_ref[...] = jnp.zeros_like(dcl_ref)

        wg, wl = [wg_ref[k:k + 1, :] for k in range(KF)], [wl_ref[k:k + 1, :] for k in range(KF)]

        def grads(hg, hl, d):
            return d * hl * _gelu_grad(hg), d * _gelu(hg)

        for r0 in range(0, ts, CH):
            dhg_s[r0:r0 + CH, :], dhl_s[r0:r0 + CH, :] = grads(hg_ref[r0:r0 + CH, :], hl_ref[r0:r0 + CH, :], d_ref[r0:r0 + CH, :])
        dhg_s[ts:ext, :], dhl_s[ts:ext, :] = grads(hgn_ref[...], hln_ref[...], jnp.where(last, 0.0, dn_ref[...]))

        for dh_s, x_ref, wk, dx_ref, dw_ref, dc_ref in ((dhg_s, g_ref, wg, dg_ref, dwg_ref, dcg_ref),
                                                        (dhl_s, l_ref, wl, dl_ref, dwl_ref, dcl_ref)):
            dw_acc = [jnp.zeros((CH, tf), F32) for _ in range(KF)]
            for r0 in range(0, ts, CH):
                x = x_ref[r0:r0 + CH, :]
                dx = jnp.zeros((CH, tf), F32)
                for k in range(KF):
                    dhk = dh_s[r0 + KF - 1 - k:r0 + KF - 1 - k + CH, :]
                    dx = dx + wk[k] * dhk
                    dw_acc[k] = dw_acc[k] + x * dhk
                    if k == KF - 1:
                        dc_acc = dhk if r0 == 0 else dc_acc + dhk
                dx_ref[r0:r0 + CH, :] = dx.astype(BF16)
            for k in range(KF):
                dw_ref[k:k + 1, :] += jnp.sum(dw_acc[k], axis=0, keepdims=True)
            dc_ref[...] += jnp.sum(dc_acc, axis=0, keepdims=True)

    hb = ts // HALO
    nhb = cfg.T // HALO
    main = lambda off: pl.BlockSpec((ts, tf), lambda j, i: (i, off + j))
    nxt = pl.BlockSpec((HALO, tf), lambda j, i: (jnp.minimum((i + 1) * hb, nhb - 1), j))
    wsp = lambda off: pl.BlockSpec((8, tf), lambda j, i: (0, off + j))
    vsp = pl.BlockSpec((1, tf), lambda j, i: (0, j))
    dxs, dws, dcs = (jax.ShapeDtypeStruct((cfg.T, cfg.F), BF16), jax.ShapeDtypeStruct((8, cfg.F), F32),
                     jax.ShapeDtypeStruct((1, cfg.F), F32))
    return _pcall(body, name=name, grid=(nf, cfg.T // ts),
                  in_specs=[main(0), main(nf), main(0), nxt, main(0), nxt, main(0), nxt, wsp(0), wsp(nf)],
                  out_specs=[main(0), main(0), wsp(0), wsp(0), vsp, vsp],
                  out_shape=[dxs, dxs, dws, dws, dcs, dcs],
                  scratch_shapes=[pltpu.VMEM((ext, tf), F32)] * 2,
                  compiler_params=_params("parallel", "arbitrary"))(h0, h0, hg, hg, hl, hl, df, df, w, w)


def _ada_fwd(c_all, w, b, *, name):
    L, D, n = w.shape
    B = c_all.shape[0]

    def body(c_ref, w_ref, b_ref, o_ref):
        c = c_ref[...]
        act = (c * _sigmoid(c)).astype(BF16)
        o_ref[0] = jnp.dot(act, w_ref[0].astype(BF16), preferred_element_type=F32) + b_ref[0]

    return _pcall(body, name=name, grid=(L,),
                  in_specs=[pl.BlockSpec((B, D), lambda l: (0, 0)), pl.BlockSpec((1, D, n), lambda l: (l, 0, 0)),
                            pl.BlockSpec((1, 1, n), lambda l: (l, 0, 0))],
                  out_specs=pl.BlockSpec((1, B, n), lambda l: (l, 0, 0)), out_shape=jax.ShapeDtypeStruct((L, B, n), F32),
                  compiler_params=_params("parallel"))(c_all, w, b)


def _ada_bwd(c_all, dmod, *, name):
    L, B, n = dmod.shape
    D = c_all.shape[1]

    def body(c_ref, d_ref, o_ref):
        c = c_ref[...]
        act = (c * _sigmoid(c)).astype(BF16)
        o_ref[0] = lax.dot_general(act, d_ref[0].astype(BF16), TN, preferred_element_type=F32)

    return _pcall(body, name=name, grid=(L,),
                  in_specs=[pl.BlockSpec((B, D), lambda l: (0, 0)), pl.BlockSpec((1, B, n), lambda l: (l, 0, 0))],
                  out_specs=pl.BlockSpec((1, D, n), lambda l: (l, 0, 0)), out_shape=jax.ShapeDtypeStruct((L, D, n), F32),
                  compiler_params=_params("parallel"))(c_all, dmod)


def _slot_sum(x, *, name):
    n, R, W = x.shape
    tr = _pick(R, (256, 128, 64, 32, 16, 8))

    def body(x_ref, o_ref):
        acc = x_ref[0].astype(F32)
        for k in range(1, n):
            acc = acc + x_ref[k].astype(F32)
        o_ref[...] = acc

    return _pcall(body, name=name, grid=(R // tr,), in_specs=[pl.BlockSpec((n, tr, W), lambda i: (0, i, 0))],
                  out_specs=pl.BlockSpec((tr, W), lambda i: (i, 0)), out_shape=jax.ShapeDtypeStruct((R, W), F32),
                  compiler_params=_params("parallel"))(x)


def _adamw_math(g, w, m, v):
    c1, c2 = 1.0 - ADAM_B1 ** ADAM_STEP, 1.0 - ADAM_B2 ** ADAM_STEP
    m2 = ADAM_B1 * m + (1.0 - ADAM_B1) * g
    v2 = ADAM_B2 * v + (1.0 - ADAM_B2) * (g * g)
    return -ADAM_LR * ((m2 / c1) / (jnp.sqrt(v2 / c2) + ADAM_EPS) + ADAM_WD * w), m2, v2


def _adamw_many(gs, ws, ms, vs, *, name):
    n = len(gs)

    def body(*refs):
        ins, outs = refs[:4 * n], refs[4 * n:]
        for i in range(n):
            d, m2, v2 = _adamw_math(*(ins[j * n + i][...] for j in range(4)))
            outs[i][...], outs[n + i][...], outs[2 * n + i][...] = d, m2, v2

    vm = pl.BlockSpec(memory_space=pltpu.VMEM)
    outs = _pcall(body, name=name, in_specs=[vm] * (4 * n), out_specs=[vm] * (3 * n),
                  out_shape=[jax.ShapeDtypeStruct(a.shape, F32) for _ in range(3) for a in ws],
                  compiler_params=pltpu.CompilerParams(vmem_limit_bytes=VMEM_LIMIT))(*gs, *ws, *ms, *vs)
    return outs[:n], outs[n:2 * n], outs[2 * n:]


def _adamw(gs, w, m, v, *, name):
    n, R, W = gs.shape
    tr = _pick(R, (256, 128, 64, 32, 16, 8))

    def body(g_ref, w_ref, m_ref, v_ref, go_ref, d_ref, mo_ref, vo_ref):
        g = g_ref[0].astype(F32)
        for k in range(1, n):
            g = g + g_ref[k].astype(F32)
        go_ref[...] = g
        d_ref[...], mo_ref[...], vo_ref[...] = _adamw_math(g, w_ref[...], m_ref[...], v_ref[...])

    blk = pl.BlockSpec((tr, W), lambda i: (i, 0))
    o = jax.ShapeDtypeStruct((R, W), F32)
    return _pcall(body, name=name, grid=(R // tr,), in_specs=[pl.BlockSpec((n, tr, W), lambda i: (0, i, 0)), blk, blk, blk],
                  out_specs=[blk] * 4, out_shape=[o] * 4, compiler_params=_params("parallel"))(gs, w, m, v)


def _peer_copies(x_ref, land_ref, send_sems, recv_sems, all_to_all):
    mx, my, mc = lax.axis_index("x"), lax.axis_index("y"), lax.axis_index("c")
    me = 4 * mx + 2 * my + mc
    copies = []
    for k in range(1, N_DEV):
        px, py, pc = mx ^ ((k >> 2) & 1), my ^ ((k >> 1) & 1), mc ^ (k & 1)
        copies.append(pltpu.make_async_remote_copy(
            src_ref=x_ref.at[4 * px + 2 * py + pc] if all_to_all else x_ref, dst_ref=land_ref.at[me],
            send_sem=send_sems.at[k - 1], recv_sem=recv_sems.at[k - 1], device_id=(px, py, pc),
            device_id_type=pl.DeviceIdType.MESH))
    return copies


def _gather_two_level(x, *, name, after=None):
    def body(x_ref, *rest):
        o_ref, send_sems, recv_sems, local_sem = rest[-4:]
        mx, my, mc = lax.axis_index("x"), lax.axis_index("y"), lax.axis_index("c")
        me, sibling = (mx, my, mc), (mx, my, 1 - mc)
        chips = [(1 - mx, my), (mx, 1 - my), (1 - mx, 1 - my)]

        def slot(px, py, pc):
            return o_ref.at[4 * px + 2 * py + pc]

        def copy(k, block, to, src=None):
            return pltpu.make_async_remote_copy(
                src_ref=slot(*block) if src is None else src, dst_ref=slot(*block), send_sem=send_sems.at[k],
                recv_sem=recv_sems.at[k], device_id=to, device_id_type=pl.DeviceIdType.MESH)

        mine = pltpu.make_async_copy(x_ref, slot(*me), local_sem)
        mine.start()
        first = [copy(0, me, sibling, src=x_ref)] + [copy(1 + j, me, (*chip, mc), src=x_ref) for j, chip in enumerate(chips)]
        for cp in first:
            cp.start()
        passed = [copy(4 + j, (*chip, mc), sibling) for j, chip in enumerate(chips)]
        for j, chip in enumerate(chips):
            copy(1 + j, (*chip, mc), me).wait_recv()
            passed[j].start()
        copy(0, sibling, me).wait_recv()
        for j, chip in enumerate(chips):
            copy(4 + j, (*chip, 1 - mc), me).wait_recv()
        for cp in first + passed:
            cp.wait_send()
        mine.wait()

    anyspec = pl.BlockSpec(memory_space=pl.ANY)
    args = [x] if after is None else [x, after]
    return _pcall(body, name=name, in_specs=[anyspec] * len(args), out_specs=anyspec,
                  out_shape=jax.ShapeDtypeStruct((N_DEV,) + tuple(x.shape), x.dtype),
                  scratch_shapes=[pltpu.SemaphoreType.DMA((N_DEV - 1,)), pltpu.SemaphoreType.DMA((N_DEV - 1,)),
                                  pltpu.SemaphoreType.DMA(())])(*args)


_HBM = pl.BlockSpec(memory_space=pltpu.HBM)
_SEM = pl.BlockSpec(memory_space=pltpu.SEMAPHORE)
_EFFECT = pltpu.SideEffectType.DATAFLOW_SIDE_EFFECTING


def _exchange_start(x, *, all_to_all, name, after=None):
    blk = x.shape[1:] if all_to_all else x.shape
    land = lax.empty((N_DEV,) + tuple(blk), x.dtype)
    has_after = after is not None

    def body(*refs):
        x_ref, land_ref = refs[0], refs[1]
        send_sems, recv_sems, _, _, token, local_sem = refs[2 + has_after:]
        me = 4 * lax.axis_index("x") + 2 * lax.axis_index("y") + lax.axis_index("c")
        mine = pltpu.make_async_copy(x_ref.at[me] if all_to_all else x_ref, land_ref.at[me], local_sem)
        mine.start()
        mine.wait()
        for cp in _peer_copies(x_ref, land_ref, send_sems, recv_sems, all_to_all):
            cp.start()
        token[...] = jnp.zeros_like(token)

    n_sem = pltpu.SemaphoreType.DMA((N_DEV - 1,))
    args = [pltpu.with_memory_space_constraint(x, pltpu.HBM), pltpu.with_memory_space_constraint(land, pltpu.HBM)]
    in_specs = [_HBM, _HBM]
    if has_after:
        args.append(after)
        in_specs.append(pl.BlockSpec(memory_space=pl.ANY))
    send_sems, recv_sems, x_thru, land_thru, token = _pcall(
        body, name=name, in_specs=in_specs,
        out_shape=(n_sem, n_sem, pltpu.HBM(x.shape, x.dtype), pltpu.HBM(land.shape, land.dtype),
                   jax.ShapeDtypeStruct((8, LANES), F32)),
        out_specs=(_SEM, _SEM, _HBM, _HBM, pl.BlockSpec(memory_space=pltpu.VMEM)), input_output_aliases={0: 2, 1: 3},
        scratch_shapes=[pltpu.SemaphoreType.DMA(())],
        compiler_params=pltpu.CompilerParams(has_side_effects=_EFFECT))(*args)
    return (send_sems, recv_sems, x_thru, land_thru, all_to_all), token


def _exchange_wait(state, after, *, name):
    send_sems, recv_sems, x_thru, land_thru, all_to_all = state

    def body(x_ref, land_ref, send_sems, recv_sems, after_ref, x_dead, landed):
        for cp in _peer_copies(x_ref, land_ref, send_sems, recv_sems, all_to_all):
            cp.wait_send()
            cp.wait_recv()

    return _pcall(
        body, name=name, in_specs=(_HBM, _HBM, _SEM, _SEM, pl.BlockSpec(memory_space=pl.ANY)),
        out_shape=(pltpu.HBM(x_thru.shape, x_thru.dtype), pltpu.HBM(land_thru.shape, land_thru.dtype)),
        out_specs=(_HBM, _HBM), input_output_aliases={0: 0, 1: 1},
        compiler_params=pltpu.CompilerParams(has_side_effects=_EFFECT))(x_thru, land_thru, send_sems, recv_sems, after)[1]


def _exchange(x, *, all_to_all, name, after=None):
    blk = x.shape[1:] if all_to_all else x.shape

    def body(x_ref, *rest):
        o_ref, send_sems, recv_sems, local_sem = rest[-4:]
        me = 4 * lax.axis_index("x") + 2 * lax.axis_index("y") + lax.axis_index("c")
        mine = pltpu.make_async_copy(x_ref.at[me] if all_to_all else x_ref, o_ref.at[me], local_sem)
        mine.start()
        copies = _peer_copies(x_ref, o_ref, send_sems, recv_sems, all_to_all)
        for cp in copies:
            cp.start()
        for cp in copies:
            cp.wait()
        mine.wait()

    anyspec = pl.BlockSpec(memory_space=pl.ANY)
    args = [x] if after is None else [x, after]
    return _pcall(body, name=name, in_specs=[anyspec] * len(args), out_specs=anyspec,
                  out_shape=jax.ShapeDtypeStruct((N_DEV,) + tuple(blk), x.dtype),
                  scratch_shapes=[pltpu.SemaphoreType.DMA((N_DEV - 1,)), pltpu.SemaphoreType.DMA((N_DEV - 1,)),
                                  pltpu.SemaphoreType.DMA(())])(*args)


PACK_ROWS = 16


def _pack(arrs, width, dtype, lead=0):
    parts, segs, r = [], [], 0
    for a in arrs:
        lshape, shape = a.shape[:lead], a.shape[lead:]
        n = math.prod(shape)
        rows = -(-n // width)
        rows_p = -(-rows // PACK_ROWS) * PACK_ROWS
        if n == rows * width:
            blk = a.reshape(lshape + (rows, width)).astype(dtype)
            parts.append(jnp.pad(blk, [(0, 0)] * lead + [(0, rows_p - rows), (0, 0)]) if rows_p > rows else blk)
        else:
            flat = jnp.pad(a.reshape(lshape + (n,)).astype(dtype), [(0, 0)] * lead + [(0, rows_p * width - n)])
            parts.append(flat.reshape(lshape + (rows_p, width)))
        segs.append((r, n, shape))
        r += rows_p
    return jnp.concatenate(parts, axis=lead), segs


def _unpack(p, segs):
    lshape, width = p.shape[:-2], p.shape[-1]
    outs = []
    for r, n, shape in segs:
        rows = -(-n // width)
        blk = p[..., r:r + rows, :]
        if n != rows * width:
            blk = blk.reshape(lshape + (rows * width,))[..., :n]
        outs.append(blk.reshape(lshape + shape))
    return outs


def _split_cols(a, f_off, h):
    return jnp.concatenate([a[..., :f_off], a[..., f_off + h:]], axis=-1), a[..., f_off:f_off + h]


def _merge_cols(main, f, f_off):
    return jnp.concatenate([main[..., :f_off], f, main[..., f_off:]], axis=-1)


def _pad_to(a, n, axis):
    pad = [(0, 0)] * a.ndim
    pad[axis] = (0, n - a.shape[axis])
    return jnp.pad(a, pad)


def kernel(x, c, w_ada, b_ada, w_in, b_in, conv_a_w, conv_a_b, ln_conv_g, ln_conv_b, w_conv_proj, w_attn_proj, w_mix_out, b_mix_out, ln1_g, ln1_b, w_ffn_up, ffn_conv_w, ffn_conv_b, w_ffn_down, ln2_g, ln2_b, loss_target, m_w_ada, m_b_ada, m_w_in, m_b_in, m_conv_a_w, m_conv_a_b, m_ln_conv_g, m_ln_conv_b, m_w_conv_proj, m_w_attn_proj, m_w_mix_out, m_b_mix_out, m_ln1_g, m_ln1_b, m_w_ffn_up, m_ffn_conv_w, m_ffn_conv_b, m_w_ffn_down, m_ln2_g, m_ln2_b, v_w_ada, v_b_ada, v_w_in, v_b_in, v_conv_a_w, v_conv_a_b, v_ln_conv_g, v_ln_conv_b, v_w_conv_proj, v_w_attn_proj, v_w_mix_out, v_b_mix_out, v_ln1_g, v_ln1_b, v_w_ffn_up, v_ffn_conv_w, v_ffn_conv_b, v_w_ffn_down, v_ln2_g, v_ln2_b):
    L, D = w_ada.shape[0], w_ada.shape[1]
    Bl, S, _ = x.shape
    C, KW, AW = conv_a_b.shape[1], conv_a_w.shape[1], w_attn_proj.shape[1]
    F, KF, n_in_all = ffn_conv_b.shape[1] // 2, ffn_conv_w.shape[1], b_in.shape[1]
    H = n_in_all - 2 * C - 3 * AW - 2 * D
    cfg = Cfg(L=L, Bl=Bl, S=S, D=D, C=C, KW=KW, H=H, Dh=AW // H, F=F, KF=KF)
    T, NM = cfg.T, cfg.NM
    f_off = 2 * C + 3 * AW
    n_ada = w_ada.shape[2]
    me = 4 * lax.axis_index("x") + 2 * lax.axis_index("y") + lax.axis_index("c")

    def my_cols(a, n):
        return lax.dynamic_slice_in_dim(a, me * n, n, axis=a.ndim - 1)

    spack, ssegs = _pack([c, conv_a_w, ffn_conv_w], D, F32)
    c_g, caw_g, fcw_g = _unpack(_exchange(spack, all_to_all=False, name="gather_small"), ssegs)
    c_all = c_g.reshape(N_DEV * Bl, D)
    caw = _pad_to(jnp.moveaxis(caw_g, 0, 2).reshape(L, KW, C), 32, 1)
    fcw = _pad_to(jnp.moveaxis(fcw_g, 0, 2).reshape(L, KF, 2 * F), 8, 1)

    mod_part = _ada_fwd(c_all, w_ada, my_cols(b_ada, n_ada)[:, None, :], name="ada_fwd")
    mod_send = jnp.moveaxis(mod_part.reshape(L, N_DEV, Bl, n_ada), 1, 0).reshape(N_DEV, L * Bl, n_ada)
    mod_recv = _exchange(mod_send, all_to_all=True, name="exchange_mod")
    mod = jnp.moveaxis(mod_recv.reshape(N_DEV, L, Bl, n_ada), 0, 2).reshape(L, Bl, 6, 1, D)
    shift1, scale1, gate1, shift2, scale2, gate2 = (mod[:, :, i] for i in range(6))

    big_names = ["w_in", "w_conv_proj", "w_attn_proj", "w_mix_out", "w_ffn_up", "w_ffn_down"]
    transposed = (True, True, True, False, True, False)

    def shard_items(arrs, grp):
        return [arrs[i][l].T if transposed[i] else arrs[i][l] for l, i in grp]

    W = [dict() for _ in range(L)]

    def set_weights(landed, segs, grp):
        for (l, i), a in zip(grp, _unpack(landed, segs)):
            a = a.reshape((-1, a.shape[-1]))
            if i == 0:
                wm_t, wf_t = _split_cols(a.T, f_off, H)
                bm, bf = _split_cols(b_in[l], f_off, H)
                W[l].update(wm_t=wm_t.T, wf_t=_pad_to(wf_t.T, LANES, 0), bm=bm[None], bf=_pad_to(bf, LANES, 0)[None])
            else:
                W[l][("w_cp_t", "w_ap_t", "w_mo", "w_up_t", "w_dn")[i - 1]] = a

    big_w = (w_in, w_conv_proj, w_attn_proj, w_mix_out, w_ffn_up, w_ffn_down)
    w_groups = [[(l, i) for i in range(6)] for l in range(L)]
    pack, segs = _pack(shard_items(big_w, w_groups[0]), D, BF16)
    landed0 = _gather_two_level(pack, name="gather_weights_0", after=mod_recv)
    set_weights(landed0, segs, w_groups[0])
    w_state, token = {}, landed0
    for l in range(1, L):
        pack, segs = _pack(shard_items(big_w, w_groups[l]), D, BF16)
        state, token = _exchange_start(pack, all_to_all=False, name=f"gather_weights_start_{l}", after=token)
        w_state[l] = (state, pack, segs)

    def wait_weights(l, after):
        state, pack, segs = w_state[l]
        set_weights(_exchange_wait(state, after, name=f"gather_weights_wait_{l}"), segs, w_groups[l])

    xf = x.reshape(T, D)
    u = _ln_mod_fwd(xf, shift1[0], scale1[0], cfg, name="ln_mod_fwd")
    saved = []
    xin = xf
    for l in range(L):
        w = W[l]
        if l > 0:
            wait_weights(l, u)
        zm = _matmul(u, w["wm_t"], mode="nt", bias=w["bm"], name=f"in_proj_{l}", after=token if l == 0 else None)
        zf = _matmul(u, w["wf_t"], mode="nt", bias=w["bf"], name=f"in_proj_f_{l}")
        a3 = _conv_a_fwd(zm, caw[l], conv_a_b[l][None], ln_conv_g[l][None], ln_conv_b[l][None], cfg, name=f"conv_a_fwd_{l}")
        cum_c = _fgate_fwd(zf, cfg, name=f"fgate_fwd_{l}")
        o, o32, lse = _attn_fwd(zm, cum_c, cfg, name=f"attn_fwd_{l}")
        ya =_matmul(a3, w["w_cp_t"], mode="nt", name=f"conv_proj_{l}")
        yb = _matmul(o, w["w_ap_t"], mode="nt", name=f"attn_proj_{l}")
        mg = _merge_fwd(zm, ya, yb, cfg, name=f"merge_fwd_{l}")
        mix = _matmul(mg, w["w_mo"], mode="nn", bias=b_mix_out[l][None], name=f"mix_out_{l}")
        x1, u2 = _res_ln_fwd(xin, mix, gate1[l], ln1_g[l][None], ln1_b[l][None], cfg, name=f"res_ln1_fwd_{l}",
                             nxt=(shift2[l], scale2[l]))
        h0 = _matmul(u2, w["w_up_t"], mode="nt", name=f"ffn_up_{l}")
        fa, hg, hl = _ffn_conv_fwd(h0, fcw[l], ffn_conv_b[l][None], cfg, name=f"ffn_conv_fwd_{l}")
        ffn = _matmul(fa, w["w_dn"], mode="nn", name=f"ffn_down_{l}")
        saved.append(dict(x=xin, u=u, zm=zm, zf=zf, a3=a3, cum_c=cum_c, o=o, o32=o32, lse=lse, ya=ya, yb=yb, mg=mg, mix=mix,
                          x1=x1, u2=u2, h0=h0, hg=hg, hl=hl, fa=fa, ffn=ffn))
        if l + 1 < L:
            xin, u = _res_ln_fwd(x1, ffn, gate2[l], ln2_g[l][None], ln2_b[l][None], cfg, name=f"res_ln2_fwd_{l}",
                                 nxt=(shift1[l + 1], scale1[l + 1]))
        else:
            xin = _res_ln_fwd(x1, ffn, gate2[l], ln2_g[l][None], ln2_b[l][None], cfg, name=f"res_ln2_fwd_{l}")

    dx, loss_tiles = _loss_grad(xin, loss_target.reshape(T, D), cfg, name="loss_grad")
    loss = lax.psum(0.5 / D * jnp.sum(loss_tiles[:, 0, 0]), ("x", "y", "c"))

    gbig = {}
    g_groups = [[(l, i) for i in range(6)] for l in reversed(range(1, L))] + [[(0, 4), (0, 5)], [(0, 1), (0, 2), (0, 3)], [(0, 0)]]
    g_state = []

    def start_grads(after=None):
        grp = g_groups[len(g_state)]
        send, segs = _pack([gbig[k].reshape((N_DEV, -1, gbig[k].shape[1])) for k in grp], D, BF16, lead=1)
        state, tok = _exchange_start(send, all_to_all=True, name=f"exchange_grads_start_{len(g_state)}", after=after)
        g_state.append((state, send, segs, grp))
        return tok

    gsm = [dict() for _ in range(L)]
    dmods = [None] * L
    token = None
    for l in reversed(range(L)):
        w, s = W[l], saved[l]
        dres2, dffn, dg2, db2, dgate2, _ = _res_ln_bwd(dx, s["x1"], s["ffn"], gate2[l], ln2_g[l][None], cfg, name=f"res_ln2_bwd_{l}")
        dfa = _matmul(dffn, w["w_dn"], mode="nt", name=f"d_ffn_act_{l}", after=token)
        gbig[l, 5] = _matmul(s["fa"], dffn, mode="tn", name=f"dw_ffn_down_{l}")
        dh0g, dh0l, dwg, dwl, dcg, dcl = _ffn_conv_bwd(dfa, s["h0"], s["hg"], s["hl"], fcw[l], cfg, name=f"ffn_conv_bwd_{l}")
        du2 = _matmul((dh0g, dh0l), w["w_up_t"], mode="nn", name=f"d_u2_{l}")
        gbig[l, 4] = _matmul((dh0g, dh0l), s["u2"], mode="tn", name=f"dw_ffn_up_{l}")
        token = start_grads() if l == 0 else None
        dx1, dscale2, dshift2 = _ln_mod_bwd(du2, s["x1"], scale2[l], dres2, cfg, name=f"ln_mod2_bwd_{l}")
        dres1, dmix, dg1, db1, dgate1, dbmo = _res_ln_bwd(dx1, s["x"], s["mix"], gate1[l], ln1_g[l][None], cfg, name=f"res_ln1_bwd_{l}")
        dmg = _matmul(dmix, w["w_mo"], mode="nt", name=f"d_merge_{l}", after=token)
        gbig[l, 3] = _matmul(s["mg"], dmix, mode="tn", name=f"dw_mix_out_{l}")
        dya, dyb, dzga, dzgb = _merge_bwd(dmg, s["zm"], s["ya"], s["yb"], cfg, name=f"merge_bwd_{l}")
        gbig[l, 1] = _matmul(dya, s["a3"], mode="tn", name=f"dw_conv_proj_{l}")
        da3 = _matmul(dya, w["w_cp_t"], mode="nn", name=f"d_a3_{l}")
        gbig[l, 2] = _matmul(dyb, s["o"], mode="tn", name=f"dw_attn_proj_{l}")
        token = start_grads() if l == 0 else None
        do = _matmul(dyb, w["w_ap_t"], mode="nn", out_dtype=BF16, name=f"d_o_{l}", after=token)
        dq, dk, dv, dcum_c = _attn_bwd(s["zm"], s["cum_c"], s["o32"], do, s["lse"], cfg, name=f"attn_bwd_{l}")
        dzf = _fgate_bwd(dcum_c, s["zf"], cfg, name=f"fgate_bwd_{l}")
        dzglu, dcaw, dcab, dlcg, dlcb = _conv_a_bwd(da3, s["zm"], caw[l], conv_a_b[l][None], ln_conv_g[l][None],
                                                    ln_conv_b[l][None], cfg, name=f"conv_a_bwd_{l}")
        dzm = jnp.concatenate([dzglu, dq, dk, dv, dzga, dzgb], axis=1)
        du1 = _matmul(dzf, w["wf_t"], mode="nn", name=f"d_u1_f_{l}")
        du1 = _matmul(dzm, w["wm_t"], mode="nn", add=du1, name=f"d_u1_{l}")
        dwm_t = _matmul(dzm, s["u"], mode="tn", name=f"dw_in_{l}")
        dwf_t = _matmul(dzf, s["u"], mode="tn", name=f"dw_in_f_{l}")
        gbig[l, 0] = _merge_cols(dwm_t.T, dwf_t[:H].T, f_off).T
        token = start_grads() if l > 0 else None
        dbm, dbf = _colsum(dzm, name=f"db_in_{l}"), _colsum(dzf, name=f"db_in_f_{l}")
        dx, dscale1, dshift1 = _ln_mod_bwd(du1, s["x"], scale1[l], dres1, cfg, name=f"ln_mod1_bwd_{l}")
        dmods[l] = jnp.concatenate([dshift1, dscale1, dgate1, dshift2, dscale2, dgate2], axis=1).reshape(Bl, 6 * D)
        gsm[l] = dict(b_in=_merge_cols(dbm[0], dbf[0, :H], f_off), conv_a_b=dcab[0], ln_conv_g=dlcg[0], ln_conv_b=dlcb[0],
                      b_mix_out=dbmo[0], ln1_g=dg1[0], ln1_b=db1[0], ffn_conv_b=jnp.concatenate([dcg[0], dcl[0]]),
                      ln2_g=dg2[0], ln2_b=db2[0], conv_a_w=dcaw[:KW], ffn_conv_w=jnp.concatenate([dwg[:KF], dwl[:KF]], axis=1))
    grad_x = dx.reshape(Bl, S, D)

    small_names = ["b_in", "conv_a_b", "ln_conv_g", "ln_conv_b", "b_mix_out", "ln1_g", "ln1_b", "ffn_conv_b", "ln2_g", "ln2_b",
                   "conv_a_w", "ffn_conv_w"]
    gs_list = [jnp.stack(dmods)] + [jnp.stack([gsm[l][n] for l in range(L)]) for n in small_names]
    gspack, gssegs = _pack(gs_list, D, F32)
    gs_all = _exchange(gspack, all_to_all=False, name="gather_small_grads")
    start_grads(after=gs_all)
    dmod_all = jnp.moveaxis(_unpack(gs_all, gssegs)[0], 0, 1).reshape(L, N_DEV * Bl, 6 * D)
    g_small = dict(zip(small_names, _unpack(_slot_sum(gs_all, name="sum_small_grads"), gssegs)[1:]))
    g_small["conv_a_w"] = my_cols(g_small["conv_a_w"], C // N_DEV)
    g_small["ffn_conv_w"] = my_cols(g_small["ffn_conv_w"], 2 * F // N_DEV)
    g_small["w_ada"] = _ada_bwd(c_all, my_cols(dmod_all, n_ada), name="ada_bwd")
    g_small["b_ada"] = jnp.stack([_colsum(dmod_all[l], name=f"db_ada_{l}")[0] for l in range(L)])

    given = dict(w_in=(w_in, m_w_in, v_w_in), w_conv_proj=(w_conv_proj, m_w_conv_proj, v_w_conv_proj),
                 w_attn_proj=(w_attn_proj, m_w_attn_proj, v_w_attn_proj), w_mix_out=(w_mix_out, m_w_mix_out, v_w_mix_out),
                 w_ffn_up=(w_ffn_up, m_w_ffn_up, v_w_ffn_up), w_ffn_down=(w_ffn_down, m_w_ffn_down, v_w_ffn_down),
                 w_ada=(w_ada, m_w_ada, v_w_ada), b_ada=(b_ada, m_b_ada, v_b_ada), b_in=(b_in, m_b_in, v_b_in),
                 conv_a_w=(conv_a_w, m_conv_a_w, v_conv_a_w), conv_a_b=(conv_a_b, m_conv_a_b, v_conv_a_b),
                 ln_conv_g=(ln_conv_g, m_ln_conv_g, v_ln_conv_g), ln_conv_b=(ln_conv_b, m_ln_conv_b, v_ln_conv_b),
                 b_mix_out=(b_mix_out, m_b_mix_out, v_b_mix_out), ln1_g=(ln1_g, m_ln1_g, v_ln1_g), ln1_b=(ln1_b, m_ln1_b, v_ln1_b),
                 ffn_conv_w=(ffn_conv_w, m_ffn_conv_w, v_ffn_conv_w), ffn_conv_b=(ffn_conv_b, m_ffn_conv_b, v_ffn_conv_b),
                 ln2_g=(ln2_g, m_ln2_g, v_ln2_g), ln2_b=(ln2_b, m_ln2_b, v_ln2_b))
    res, kinds = {}, ("grad", "delta", "new_m", "new_v")
    loc_names = ["b_ada"] + small_names
    deltas, new_ms, new_vs = _adamw_many([g_small[n] for n in loc_names], *([given[n][j] for n in loc_names] for j in range(3)),
                                         name="adamw_small")
    for n, d, m2, v2 in zip(loc_names, deltas, new_ms, new_vs):
        res["grad", n], res["delta", n], res["new_m", n], res["new_v", n] = g_small[n], d, m2, v2
    rows_ada = (L * D * n_ada // D, D)
    outs = _adamw(g_small["w_ada"].reshape((1,) + rows_ada), *(a.reshape(rows_ada) for a in given["w_ada"]), name="adamw_w_ada")
    for kind, a in zip(kinds, outs):
        res[kind, "w_ada"] = a.reshape(w_ada.shape)

    big_parts = {}
    after = outs[0]
    for gi, (state, send, segs, grp) in enumerate(g_state):
        landed = _exchange_wait(state, after, name=f"exchange_grads_wait_{gi}")
        wmv = [_pack(shard_items([given[n][j] for n in big_names], grp), D, F32)[0] for j in range(3)]
        outs = _adamw(landed, *wmv, name=f"adamw_big_{gi}")
        for kind, packed in zip(kinds, outs):
            for (l, i), a in zip(grp, _unpack(packed, segs)):
                big_parts[kind, l, i] = a.T if transposed[i] else a
        after = outs[0]
    for kind in kinds:
        for i, n in enumerate(big_names):
            res[kind, n] = jnp.stack([big_parts[kind, l, i] for l in range(L)])

    order = ["w_ada", "b_ada", "w_in", "b_in", "conv_a_w", "conv_a_b", "ln_conv_g", "ln_conv_b", "w_conv_proj", "w_attn_proj",
             "w_mix_out", "b_mix_out", "ln1_g", "ln1_b", "w_ffn_up", "ffn_conv_w", "ffn_conv_b", "w_ffn_down", "ln2_g", "ln2_b"]
    return (loss, grad_x, *[res[k, n] for k in ("grad", "delta", "new_m", "new_v") for n in order])
```

```python
import functools
import math
from typing import NamedTuple

import jax
import jax.numpy as jnp
from jax import lax
from jax.experimental import pallas as pl
from jax.experimental.pallas import tpu as pltpu

F32, BF16 = jnp.float32, jnp.bfloat16
LN_EPS = 1e-5
ADAM_LR, ADAM_B1, ADAM_B2, ADAM_EPS, ADAM_WD, ADAM_STEP = 0.001, 0.9, 0.999, 1e-08, 0.01, 10
N_DEV = 8
LANES = 128
VMEM_LIMIT = 56 * 1024 * 1024
NEG = -1e30
NT = (((1,), (1,)), ((), ()))
TN = (((0,), (0,)), ((), ()))


class Cfg(NamedTuple):
    L: int
    Bl: int
    S: int
    D: int
    C: int
    KW: int
    H: int
    Dh: int
    F: int
    KF: int

    @property
    def T(self): return self.Bl * self.S
    @property
    def AW(self): return self.H * self.Dh
    @property
    def NM(self): return 2 * self.C + 3 * self.AW + 2 * self.D
    @property
    def q_off(self): return 2 * self.C
    @property
    def g_off(self): return 2 * self.C + 3 * self.AW
    @property
    def alpha(self): return (2.0 * self.L) ** 0.25


def _pcall(body, **kw):
    return pl.pallas_call(body, **kw)


def _params(*sem):
    return pltpu.CompilerParams(dimension_semantics=sem, vmem_limit_bytes=VMEM_LIMIT)


def _pick(n, prefs):
    for p in prefs:
        if n % p == 0:
            return p
    return n


def _sigmoid(x):
    return 1.0 / (1.0 + jnp.exp(-x))


def _ln_stats(x):
    mu = jnp.mean(x, axis=-1, keepdims=True)
    xc = x - mu
    var = jnp.mean(xc * xc, axis=-1, keepdims=True)
    rstd = lax.rsqrt(var + LN_EPS)
    return xc * rstd, rstd


def _ln_bwd(dxh, xh, rstd):
    return rstd * (dxh - jnp.mean(dxh, axis=-1, keepdims=True) - xh * jnp.mean(dxh * xh, axis=-1, keepdims=True))


def _matmul(a, b, *, mode, name, bias=None, add=None, out_dtype=F32, tm=None, tn=None, tk=None, after=None):
    parts = tuple(a) if isinstance(a, (tuple, list)) else (a,)
    P = len(parts)
    if mode == "tn":
        K, Mp = parts[0].shape
        M, Kp = P * Mp, K
    else:
        M, Kp = parts[0].shape
        K, Mp = P * Kp, M
    N = b.shape[0] if mode == "nt" else b.shape[1]
    lane_tiles = (1536, 1408, 1024, 768, 512, 256, 128)
    tm = tm or _pick(Mp, lane_tiles if mode == "tn" else (1024, 512, 256, 128, 64, 32, 16, 8))
    tn = tn or _pick(N, lane_tiles)
    tk = tk or _pick(Kp, (1024, 512, 256, 128) if mode == "tn" else lane_tiles)
    nk = K // tk
    per = Mp // tm if mode == "tn" else Kp // tk
    dn = {"nn": (((1,), (0,)), ((), ())), "nt": NT, "tn": TN}[mode]
    has_bias, has_add, has_after = bias is not None, add is not None, after is not None

    def body(*refs):
        a_refs, b_ref = refs[:P], refs[P]
        pos = P + 1
        bias_ref = refs[pos] if has_bias else None
        pos += has_bias
        add_ref = refs[pos] if has_add else None
        pos += has_add + has_after
        o_ref = refs[pos]
        acc_ref = refs[pos + 1] if nk > 1 else None
        k = pl.program_id(2)

        def finish(acc):
            if has_bias:
                acc = acc + bias_ref[...]
            if has_add:
                acc = acc + add_ref[...]
            o_ref[...] = acc.astype(out_dtype)

        def accumulate(a_ref):
            part = lax.dot_general(a_ref[...], b_ref[...], dn, preferred_element_type=F32)
            if nk == 1:
                finish(part)
            else:
                @pl.when(k == 0)
                def _():
                    acc_ref[...] = part

                @pl.when(k > 0)
                def _():
                    acc_ref[...] += part

        if P == 1:
            accumulate(a_refs[0])
        else:
            step = pl.program_id(0 if mode == "tn" else 2)
            for p in range(P):
                pl.when(step // per == p)(functools.partial(accumulate, a_refs[p]))
        if nk > 1:
            @pl.when(k == nk - 1)
            def _():
                finish(acc_ref[...])

    def a_spec(p):
        if mode == "tn":
            return pl.BlockSpec((tk, tm), lambda i, j, k: (k, jnp.clip(i - p * per, 0, per - 1)))
        return pl.BlockSpec((tm, tk), lambda i, j, k: (i, jnp.clip(k - p * per, 0, per - 1)))

    b_spec = pl.BlockSpec((tn, tk), lambda i, j, k: (j, k)) if mode == "nt" else pl.BlockSpec((tk, tn), lambda i, j, k: (k, j))
    in_specs, args = [a_spec(p) for p in range(P)] + [b_spec], list(parts) + [b]
    if has_bias:
        in_specs.append(pl.BlockSpec((1, tn), lambda i, j, k: (0, j)))
        args.append(bias)
    if has_add:
        in_specs.append(pl.BlockSpec((tm, tn), lambda i, j, k: (i, j)))
        args.append(add)
    if has_after:
        in_specs.append(pl.BlockSpec(memory_space=pl.ANY))
        args.append(after)
    return _pcall(
        body, name=name, grid=(M // tm, N // tn, nk), in_specs=in_specs,
        out_specs=pl.BlockSpec((tm, tn), lambda i, j, k: (i, j)),
        out_shape=jax.ShapeDtypeStruct((M, N), out_dtype),
        scratch_shapes=[pltpu.VMEM((tm, tn), F32)] if nk > 1 else [],
        compiler_params=_params("parallel", "parallel", "arbitrary"),
    )(*args)


def _colsum(x, *, name):
    T, N = x.shape
    tr = _pick(T, (512, 256, 128, 64, 32, 16))
    tc = _pick(N, (1536, 1024, 512, 256, 128))

    def body(x_ref, o_ref):
        @pl.when(pl.program_id(1) == 0)
        def _():
            o_ref[...] = jnp.zeros_like(o_ref)

        o_ref[...] += jnp.sum(x_ref[...].astype(F32), axis=0, keepdims=True)

    return _pcall(body, name=name, grid=(N // tc, T // tr), in_specs=[pl.BlockSpec((tr, tc), lambda j, i: (i, j))],
                  out_specs=pl.BlockSpec((1, tc), lambda j, i: (0, j)), out_shape=jax.ShapeDtypeStruct((1, N), F32),
                  compiler_params=_params("parallel", "arbitrary"))(x)


def _row_tile(cfg):
    return _pick(cfg.S, (512, 256, 128, 64, 32, 16, 8))


def _ln_mod_fwd(x, shift, scale, cfg, *, name):
    tr = _row_tile(cfg)
    tpb = cfg.S // tr

    def body(x_ref, sh_ref, sc_ref, u_ref):
        xh, _ = _ln_stats(x_ref[...])
        u_ref[...] = (xh * (1.0 + sc_ref[0]) + sh_ref[0]).astype(BF16)

    row = pl.BlockSpec((tr, cfg.D), lambda i: (i, 0))
    per_b = pl.BlockSpec((1, 1, cfg.D), lambda i: (i // tpb, 0, 0))
    return _pcall(body, name=name, grid=(cfg.T // tr,), in_specs=[row, per_b, per_b], out_specs=row,
                  out_shape=jax.ShapeDtypeStruct((cfg.T, cfg.D), BF16), compiler_params=_params("parallel"))(x, shift, scale)


def _res_ln_fwd(xin, br, gate, g, b, cfg, *, name, nxt=None):
    tr = _row_tile(cfg)
    tpb = cfg.S // tr
    alpha = cfg.alpha

    def body(*refs):
        x_ref, br_ref, gt_ref, g_ref, b_ref = refs[:5]
        r = alpha * x_ref[...] + (1.0 + gt_ref[0]) * br_ref[...]
        xh, _ = _ln_stats(r)
        xo = xh * g_ref[...] + b_ref[...]
        if nxt is None:
            refs[5][...] = xo
        else:
            sh_ref, sc_ref, xo_ref, u_ref = refs[5:]
            xo_ref[...] = xo
            uh, _ = _ln_stats(xo)
            u_ref[...] = (uh * (1.0 + sc_ref[0]) + sh_ref[0]).astype(BF16)

    row = pl.BlockSpec((tr, cfg.D), lambda i: (i, 0))
    per_b = pl.BlockSpec((1, 1, cfg.D), lambda i: (i // tpb, 0, 0))
    vec = pl.BlockSpec((1, cfg.D), lambda i: (0, 0))
    in_specs, args = [row, row, per_b, vec, vec], [xin, br, gate, g, b]
    out_specs, out_shape = row, jax.ShapeDtypeStruct((cfg.T, cfg.D), F32)
    if nxt is not None:
        in_specs += [per_b, per_b]
        args += list(nxt)
        out_specs = [row, row]
        out_shape = [out_shape, jax.ShapeDtypeStruct((cfg.T, cfg.D), BF16)]
    return _pcall(body, name=name, grid=(cfg.T // tr,), in_specs=in_specs, out_specs=out_specs, out_shape=out_shape,
                  compiler_params=_params("parallel"))(*args)


def _loss_grad(y, tgt, cfg, *, name):
    tr = _row_tile(cfg)
    nt = cfg.T // tr
    inv_d = 1.0 / cfg.D

    def body(y_ref, t_ref, dy_ref, ls_ref):
        e = y_ref[...] - t_ref[...]
        dy_ref[...] = e * inv_d
        ls_ref[...] = jnp.full((1, 1, LANES), jnp.sum(e * e), F32)

    row = pl.BlockSpec((tr, cfg.D), lambda i: (i, 0))
    return _pcall(body, name=name, grid=(nt,), in_specs=[row, row],
                  out_specs=[row, pl.BlockSpec((1, 1, LANES), lambda i: (i, 0, 0))],
                  out_shape=[jax.ShapeDtypeStruct((cfg.T, cfg.D), F32), jax.ShapeDtypeStruct((nt, 1, LANES), F32)],
                  compiler_params=_params("parallel"))(y, tgt)


def _res_ln_bwd(dy, xin, br, gate, g, cfg, *, name):
    tr = _row_tile(cfg)
    tpb = cfg.S // tr
    alpha = cfg.alpha

    def body(dy_ref, x_ref, br_ref, gt_ref, g_ref, dx_ref, dbr_ref, dg_ref, db_ref, dgt_ref, dbs_ref):
        i = pl.program_id(0)

        @pl.when(i == 0)
        def _():
            dg_ref[...] = jnp.zeros_like(dg_ref)
            db_ref[...] = jnp.zeros_like(db_ref)
            dbs_ref[...] = jnp.zeros_like(dbs_ref)

        @pl.when(i % tpb == 0)
        def _():
            dgt_ref[...] = jnp.zeros_like(dgt_ref)

        dy, brv, one_gate = dy_ref[...], br_ref[...], 1.0 + gt_ref[0]
        xh, rstd = _ln_stats(alpha * x_ref[...] + one_gate * brv)
        dg_ref[...] += jnp.sum(dy * xh, axis=0, keepdims=True)
        db_ref[...] += jnp.sum(dy, axis=0, keepdims=True)
        dr = _ln_bwd(dy * g_ref[...], xh, rstd)
        dx_ref[...] = alpha * dr
        dbr = one_gate * dr
        dbr_ref[...] = dbr.astype(BF16)
        dbs_ref[...] += jnp.sum(dbr, axis=0, keepdims=True)
        dgt_ref[0] += jnp.sum(dr * brv, axis=0, keepdims=True)

    row = pl.BlockSpec((tr, cfg.D), lambda i: (i, 0))
    per_b = pl.BlockSpec((1, 1, cfg.D), lambda i: (i // tpb, 0, 0))
    vec = pl.BlockSpec((1, cfg.D), lambda i: (0, 0))
    vs = jax.ShapeDtypeStruct((1, cfg.D), F32)
    return _pcall(body, name=name, grid=(cfg.T // tr,), in_specs=[row, row, row, per_b, vec],
                  out_specs=[row, row, vec, vec, per_b, vec],
                  out_shape=[jax.ShapeDtypeStruct((cfg.T, cfg.D), F32), jax.ShapeDtypeStruct((cfg.T, cfg.D), BF16), vs, vs,
                             jax.ShapeDtypeStruct((cfg.Bl, 1, cfg.D), F32), vs],
                  compiler_params=_params("arbitrary"))(dy, xin, br, gate, g)


def _ln_mod_bwd(du, xin, scale, dres, cfg, *, name):
    tr = _row_tile(cfg)
    tpb = cfg.S // tr

    def body(du_ref, x_ref, sc_ref, dres_ref, dx_ref, dsc_ref, dsh_ref):
        @pl.when(pl.program_id(0) % tpb == 0)
        def _():
            dsc_ref[...] = jnp.zeros_like(dsc_ref)
            dsh_ref[...] = jnp.zeros_like(dsh_ref)

        du = du_ref[...]
        xh, rstd = _ln_stats(x_ref[...])
        dsc_ref[0] += jnp.sum(du * xh, axis=0, keepdims=True)
        dsh_ref[0] += jnp.sum(du, axis=0, keepdims=True)
        dx_ref[...] = _ln_bwd(du * (1.0 + sc_ref[0]), xh, rstd) + dres_ref[...]

    row = pl.BlockSpec((tr, cfg.D), lambda i: (i, 0))
    per_b = pl.BlockSpec((1, 1, cfg.D), lambda i: (i // tpb, 0, 0))
    bs = jax.ShapeDtypeStruct((cfg.Bl, 1, cfg.D), F32)
    return _pcall(body, name=name, grid=(cfg.T // tr,), in_specs=[row, row, per_b, row], out_specs=[row, per_b, per_b],
                  out_shape=[jax.ShapeDtypeStruct((cfg.T, cfg.D), F32), bs, bs],
                  compiler_params=_params("arbitrary"))(du, xin, scale, dres)


def _merge_tiles(cfg):
    tr = _pick(cfg.T, (512, 256, 128, 64, 32, 16))
    tc = _pick(math.gcd(cfg.g_off, cfg.D), (512, 256, 128))
    return tr, tc


def _merge_fwd(zm, ya, yb, cfg, *, name):
    tr, tc = _merge_tiles(cfg)
    ga0, gb0 = cfg.g_off // tc, (cfg.g_off + cfg.D) // tc

    def body(ga_ref, gb_ref, ya_ref, yb_ref, m_ref):
        m_ref[...] = (_sigmoid(ga_ref[...]) * ya_ref[...] + _sigmoid(gb_ref[...]) * yb_ref[...]).astype(BF16)

    blk = pl.BlockSpec((tr, tc), lambda i, j: (i, j))
    return _pcall(body, name=name, grid=(cfg.T // tr, cfg.D // tc),
                  in_specs=[pl.BlockSpec((tr, tc), lambda i, j: (i, ga0 + j)), pl.BlockSpec((tr, tc), lambda i, j: (i, gb0 + j)), blk, blk],
                  out_specs=blk, out_shape=jax.ShapeDtypeStruct((cfg.T, cfg.D), BF16),
                  compiler_params=_params("parallel", "parallel"))(zm, zm, ya, yb)


def _merge_bwd(dm, zm, ya, yb, cfg, *, name):
    tr, tc = _merge_tiles(cfg)
    ga0, gb0 = cfg.g_off // tc, (cfg.g_off + cfg.D) // tc

    def body(dm_ref, ga_ref, gb_ref, ya_ref, yb_ref, dya_ref, dyb_ref, dga_ref, dgb_ref):
        dm = dm_ref[...]
        ga, gb = _sigmoid(ga_ref[...]), _sigmoid(gb_ref[...])
        dya_ref[...] = (dm * ga).astype(BF16)
        dyb_ref[...] = (dm * gb).astype(BF16)
        dga_ref[...] = (dm * ya_ref[...] * ga * (1.0 - ga)).astype(BF16)
        dgb_ref[...] = (dm * yb_ref[...] * gb * (1.0 - gb)).astype(BF16)

    blk = pl.BlockSpec((tr, tc), lambda i, j: (i, j))
    o = jax.ShapeDtypeStruct((cfg.T, cfg.D), BF16)
    return _pcall(body, name=name, grid=(cfg.T // tr, cfg.D // tc),
                  in_specs=[blk, pl.BlockSpec((tr, tc), lambda i, j: (i, ga0 + j)), pl.BlockSpec((tr, tc), lambda i, j: (i, gb0 + j)), blk, blk],
                  out_specs=[blk] * 4, out_shape=[o] * 4, compiler_params=_params("parallel", "parallel"))(dm, zm, zm, ya, yb)


CONV_A_HALO = 32
CONV_A_CHUNK = 32


SUBLANES = 8


def _conv_a_tile(cfg):
    assert cfg.KW - 1 <= CONV_A_HALO
    return _pick(cfg.S, (256, 128, 64, 32))


def _shift_copies(src_s, sh_s):
    rows = src_s.shape[0] - SUBLANES
    for b in range(1, SUBLANES):
        sh_s[b - 1, :, :] = src_s[b:b + rows, :]


def _rows(src_s, sh_s, start, n):
    a, b = divmod(start, SUBLANES)
    return src_s[start:start + n, :] if b == 0 else sh_s[b - 1, SUBLANES * a:SUBLANES * a + n, :]


def _conv_a_fwd(zm, w, cb, g, b, cfg, *, name):
    C, KW, HALO, CH = cfg.C, cfg.KW, CONV_A_HALO, CONV_A_CHUNK
    ts = _conv_a_tile(cfg)
    tpb = cfg.S // ts
    lead = HALO - (KW - 1)

    def body(z_ref, zp_ref, w_ref, cb_ref, g_ref, b_ref, o_ref, a0_s, a0_sh):
        first = pl.program_id(0) % tpb == 0
        prev = zp_ref[:, :C] * _sigmoid(zp_ref[:, C:])
        a0_s[0:HALO, :] = jnp.where(first, 0.0, prev)
        a0_s[HALO:HALO + ts, :] = z_ref[:, :C] * _sigmoid(z_ref[:, C:])
        _shift_copies(a0_s, a0_sh)
        for r0 in range(0, ts, CH):
            acc = jnp.zeros((CH, C), F32)
            for k in range(KW):
                acc = acc + w_ref[k:k + 1, :] * _rows(a0_s, a0_sh, r0 + lead + k, CH)
            xh, _ = _ln_stats(acc + cb_ref[...])
            a2 = xh * g_ref[...] + b_ref[...]
            o_ref[r0:r0 + CH, :] = (a2 * _sigmoid(a2)).astype(BF16)

    hb = ts // HALO
    vec = pl.BlockSpec((1, C), lambda i: (0, 0))
    return _pcall(body, name=name, grid=(cfg.T // ts,),
                  in_specs=[pl.BlockSpec((ts, 2 * C), lambda i: (i, 0)),
                            pl.BlockSpec((HALO, 2 * C), lambda i: (jnp.maximum(i * hb - 1, 0), 0)),
                            pl.BlockSpec((32, C), lambda i: (0, 0)), vec, vec, vec],
                  out_specs=pl.BlockSpec((ts, C), lambda i: (i, 0)), out_shape=jax.ShapeDtypeStruct((cfg.T, C), BF16),
                  scratch_shapes=[pltpu.VMEM((HALO + ts, C), F32), pltpu.VMEM((SUBLANES - 1, HALO + ts - SUBLANES, C), F32)],
                  compiler_params=_params("parallel"))(zm, zm, w, cb, g, b)


def _conv_a_bwd(da3, zm, w, cb, g, b, cfg, *, name):
    C, KW, HALO, CH = cfg.C, cfg.KW, CONV_A_HALO, CONV_A_CHUNK
    ts = _conv_a_tile(cfg)
    tpb = cfg.S // ts
    nt = cfg.T // ts
    lead = HALO - (KW - 1)
    ext = ts + HALO

    def body(z_ref, zp_ref, zn_ref, d_ref, dn_ref, w_ref, cb_ref, g_ref, b_ref,
             dz_ref, dw_ref, dcb_ref, dg_ref, db_ref, a0_s, d3_s, da1_s, a0_sh, da1_sh):
        i = pl.program_id(0)
        first, last = i % tpb == 0, i % tpb == tpb - 1

        @pl.when(i == 0)
        def _():
            dw_ref[...] = jnp.zeros_like(dw_ref)
            dcb_ref[...] = jnp.zeros_like(dcb_ref)
            dg_ref[...] = jnp.zeros_like(dg_ref)
            db_ref[...] = jnp.zeros_like(db_ref)

        a0_s[0:HALO, :] = jnp.where(first, 0.0, zp_ref[:, :C] * _sigmoid(zp_ref[:, C:]))
        a0_s[HALO:HALO + ts, :] = z_ref[:, :C] * _sigmoid(z_ref[:, C:])
        a0_s[HALO + ts:HALO + ext, :] = zn_ref[:, :C] * _sigmoid(zn_ref[:, C:])
        d3_s[0:ts, :] = d_ref[...]
        d3_s[ts:ext, :] = jnp.where(last, 0.0, dn_ref[...])
        _shift_copies(a0_s, a0_sh)
        dcb, dg, db = jnp.zeros((1, C), F32), jnp.zeros((1, C), F32), jnp.zeros((1, C), F32)
        for r0 in range(0, ext, CH):
            acc = jnp.zeros((CH, C), F32)
            for k in range(KW):
                acc = acc + w_ref[k:k + 1, :] * _rows(a0_s, a0_sh, r0 + lead + k, CH)
            xh, rstd = _ln_stats(acc + cb_ref[...])
            a2 = xh * g_ref[...] + b_ref[...]
            sg = _sigmoid(a2)
            da2 = d3_s[r0:r0 + CH, :] * (sg * (1.0 + a2 * (1.0 - sg)))
            da1 = _ln_bwd(da2 * g_ref[...], xh, rstd)
            da1_s[r0:r0 + CH, :] = da1
            if r0 < ts:
                dg = dg + jnp.sum(da2 * xh, axis=0, keepdims=True)
                db = db + jnp.sum(da2, axis=0, keepdims=True)
                dcb = dcb + jnp.sum(da1, axis=0, keepdims=True)
        dg_ref[...] += dg
        db_ref[...] += db
        dcb_ref[...] += dcb
        _shift_copies(da1_s, da1_sh)
        for k in range(KW):
            dwk = jnp.zeros((CH, C), F32)
            for r0 in range(0, ts, CH):
                dwk = dwk + da1_s[r0:r0 + CH, :] * _rows(a0_s, a0_sh, r0 + lead + k, CH)
            dw_ref[k:k + 1, :] += jnp.sum(dwk, axis=0, keepdims=True)
        for r0 in range(0, ts, CH):
            da0 = jnp.zeros((CH, C), F32)
            for k in range(KW):
                da0 = da0 + w_ref[k:k + 1, :] * _rows(da1_s, da1_sh, r0 + KW - 1 - k, CH)
            val, sg = z_ref[r0:r0 + CH, :C], _sigmoid(z_ref[r0:r0 + CH, C:])
            dz_ref[r0:r0 + CH, :C] = (da0 * sg).astype(BF16)
            dz_ref[r0:r0 + CH, C:] = (da0 * val * sg * (1.0 - sg)).astype(BF16)

    hb = ts // HALO
    nhb = cfg.T // HALO
    vec = pl.BlockSpec((1, C), lambda i: (0, 0))
    vs = jax.ShapeDtypeStruct((1, C), F32)
    return _pcall(body, name=name, grid=(nt,),
                  in_specs=[pl.BlockSpec((ts, 2 * C), lambda i: (i, 0)),
                            pl.BlockSpec((HALO, 2 * C), lambda i: (jnp.maximum(i * hb - 1, 0), 0)),
                            pl.BlockSpec((HALO, 2 * C), lambda i: (jnp.minimum((i + 1) * hb, nhb - 1), 0)),
                            pl.BlockSpec((ts, C), lambda i: (i, 0)),
                            pl.BlockSpec((HALO, C), lambda i: (jnp.minimum((i + 1) * hb, nhb - 1), 0)),
                            pl.BlockSpec((32, C), lambda i: (0, 0)), vec, vec, vec],
                  out_specs=[pl.BlockSpec((ts, 2 * C), lambda i: (i, 0)), pl.BlockSpec((32, C), lambda i: (0, 0)), vec, vec, vec],
                  out_shape=[jax.ShapeDtypeStruct((cfg.T, 2 * C), BF16), jax.ShapeDtypeStruct((32, C), F32), vs, vs, vs],
                  scratch_shapes=[pltpu.VMEM((HALO + ext, C), F32), pltpu.VMEM((ext, C), F32), pltpu.VMEM((ext, C), F32),
                                  pltpu.VMEM((SUBLANES - 1, HALO + ext - SUBLANES, C), F32),
                                  pltpu.VMEM((SUBLANES - 1, ext - SUBLANES, C), F32)],
                  compiler_params=_params("arbitrary"))(zm, zm, zm, da3, da3, w, cb, g, b)


def _cum_tile(cfg):
    return _pick(cfg.S, (256, 128, 64, 32, 16, 8))


def _fgate_fwd(zf, cfg, *, name):
    tc = _cum_tile(cfg)
    tpb = cfg.S // tc
    hp = _attn_tiles(cfg)[2]
    nb = cfg.H // hp

    def body(z_ref, o_ref, carry):
        @pl.when(pl.program_id(0) % tpb == 0)
        def _():
            carry[...] = jnp.zeros_like(carry)

        z = z_ref[...]
        logf = jnp.minimum(z, 0.0) - jnp.log(1.0 + jnp.exp(-jnp.abs(z)))
        tri = (lax.broadcasted_iota(jnp.int32, (tc, tc), 0) >= lax.broadcasted_iota(jnp.int32, (tc, tc), 1)).astype(F32)
        cum = jnp.dot(tri, logf, precision=lax.Precision.HIGHEST, preferred_element_type=F32) + carry[...]
        carry[...] = cum[tc - 1:tc, :]
        o_ref[0] = cum
        for b in range(1, nb):
            o_ref[b] = pltpu.roll(cum, LANES - hp * b, axis=1)

    return _pcall(body, name=name, grid=(cfg.T // tc,), in_specs=[pl.BlockSpec((tc, LANES), lambda i: (i, 0))],
                  out_specs=pl.BlockSpec((nb, tc, LANES), lambda i: (0, i, 0)),
                  out_shape=jax.ShapeDtypeStruct((nb, cfg.T, LANES), F32), scratch_shapes=[pltpu.VMEM((1, LANES), F32)],
                  compiler_params=_params("arbitrary"))(zf)


def _fgate_bwd(dcum_c, zf, cfg, *, name):
    tc = _cum_tile(cfg)
    tpb = cfg.S // tc
    nt = cfg.T // tc
    hp = _attn_tiles(cfg)[2]
    nb = cfg.H // hp

    def body(d_ref, z_ref, o_ref, carry):
        @pl.when(pl.program_id(0) % tpb == 0)
        def _():
            carry[...] = jnp.zeros_like(carry)

        d = d_ref[0]
        for b in range(1, nb):
            d = d + pltpu.roll(d_ref[b], hp * b, axis=1)
        tri = (lax.broadcasted_iota(jnp.int32, (tc, tc), 0) <= lax.broadcasted_iota(jnp.int32, (tc, tc), 1)).astype(F32)
        suf = jnp.dot(tri, d, precision=lax.Precision.HIGHEST, preferred_element_type=F32) + carry[...]
        o_ref[...] = (suf * _sigmoid(-z_ref[...])).astype(BF16)
        carry[...] = suf[0:1, :]

    blk = pl.BlockSpec((tc, LANES), lambda i: (nt - 1 - i, 0))
    return _pcall(body, name=name, grid=(nt,), in_specs=[pl.BlockSpec((nb, tc, LANES), lambda i: (0, nt - 1 - i, 0)), blk],
                  out_specs=blk, out_shape=jax.ShapeDtypeStruct((cfg.T, LANES), BF16),
                  scratch_shapes=[pltpu.VMEM((1, LANES), F32)], compiler_params=_params("arbitrary"))(dcum_c, zf)


def _attn_tiles(cfg):
    assert LANES % cfg.Dh == 0 and cfg.H % (LANES // cfg.Dh) == 0
    tk = _pick(cfg.S, (256, 128))
    tq = _pick(cfg.S, (2 * tk, tk))
    return tq, tk, LANES // cfg.Dh


BIAS_LANES = 3


def _head_lanes(hd, cfg, hp):
    li = lax.broadcasted_iota(jnp.int32, (1, LANES), 1)
    own = (li >= hd * cfg.Dh) & (li < (hd + 1) * cfg.Dh)
    return own, li, ((hd + 1) % hp) * cfg.Dh


def _q_aug(q, hd, cfg, hp):
    own, li, b0 = _head_lanes(hd, cfg, hp)
    ones = ((li >= b0) & (li < b0 + BIAS_LANES)).astype(F32)
    return jnp.where(own, q * cfg.Dh ** -0.5, ones).astype(BF16)


def _k_aug(k, ck, hd, cfg, hp):
    own, li, b0 = _head_lanes(hd, cfg, hp)
    hi = ck.astype(BF16).astype(F32)
    mid = (ck - hi).astype(BF16).astype(F32)
    lo = ck - hi - mid
    bias = jnp.where(li == b0, -hi, jnp.where(li == b0 + 1, -mid, jnp.where(li == b0 + 2, -lo, 0.0)))
    return jnp.where(own, k, bias).astype(BF16)


def _attn_fwd(zm, cum_c, cfg, *, name):
    S, Dh = cfg.S, cfg.Dh
    tq, tk, hp = _attn_tiles(cfg)
    assert hp >= 2
    nq, nb, per = S // tq, cfg.H // hp, tq // tk
    qb, kb, vb = cfg.q_off // LANES, (cfg.q_off + cfg.AW) // LANES, (cfg.q_off + 2 * cfg.AW) // LANES

    def body(q_ref, k_ref, v_ref, cc_ref, o_ref, o32_ref, lse_ref, ka_s, vt_s):
        qi = pl.program_id(2)

        @pl.when(qi == 0)
        def _():
            def prep(c, _):
                r = pl.multiple_of(c * tk, tk)
                kc = k_ref[pl.ds(r, tk), :]
                for hd in range(hp):
                    ka_s[hd, pl.ds(r, tk), :] = _k_aug(kc, cc_ref[0, pl.ds(r, tk), hd:hd + 1], hd, cfg, hp)
                vt_s[:, pl.ds(r, tk)] = v_ref[pl.ds(r, tk), :].T.astype(BF16)
                return 0

            lax.fori_loop(0, S // tk, prep, 0)

        key_i = lax.broadcasted_iota(jnp.int32, (tk, tq), 0)
        qry_i = lax.broadcasted_iota(jnp.int32, (tk, tq), 1)
        qf = q_ref[...]
        qa = [_q_aug(qf, hd, cfg, hp) for hd in range(hp)]

        def scores(j):
            r = pl.multiple_of(j * tk, tk)
            return tuple(lax.dot_general(ka_s[hd, pl.ds(r, tk), :], qa[hd], NT, preferred_element_type=F32) for hd in range(hp))

        def chunk(j, s_all, carry, diag=None):
            r = pl.multiple_of(j * tk, tk)
            new = []
            for hd in range(hp):
                m, l, acc = carry[hd]
                s = s_all[hd]
                if diag is not None:
                    s = jnp.where(key_i + diag * tk <= qry_i, s, NEG)
                m_new = jnp.maximum(m, jnp.max(s, axis=0, keepdims=True))
                a = jnp.exp(m - m_new)
                p = jnp.exp(s - m_new)
                l = a * l + jnp.sum(p, axis=0, keepdims=True)
                p_hi = p.astype(BF16)
                p_lo = (p - p_hi.astype(F32)).astype(BF16)
                vt = vt_s[hd * Dh:(hd + 1) * Dh, pl.ds(r, tk)]
                acc = a * acc + (jnp.dot(vt, p_hi, preferred_element_type=F32) + jnp.dot(vt, p_lo, preferred_element_type=F32))
                new.append((m_new, l, acc))
            return tuple(new)

        init = tuple((jnp.full((1, tq), NEG, F32), jnp.zeros((1, tq), F32), jnp.zeros((Dh, tq), F32)) for _ in range(hp))
        n_full = qi * per

        def step(j, c):
            stats, s_cur = c
            s_next = scores(j + 1)
            return chunk(j, s_cur, stats), s_next

        res, s_cur = lax.fori_loop(0, n_full, step, (init, scores(0)))
        for d in range(per):
            s_next = scores(n_full + d + 1) if d + 1 < per else None
            res = chunk(n_full + d, s_cur, res, diag=d)
            s_cur = s_next
        o = jnp.concatenate([acc / l for _, l, acc in res], axis=0).T
        o_ref[...] = o.astype(BF16)
        o32_ref[...] = o
        lse_ref[...] = jnp.zeros_like(lse_ref)
        for hd in range(hp):
            lse_ref[0, 0, hd:hd + 1, :] = res[hd][0] + jnp.log(res[hd][1])

    return _pcall(body, name=name, grid=(cfg.Bl, nb, nq),
                  in_specs=[pl.BlockSpec((tq, LANES), lambda b, h, i: (b * nq + i, qb + h)),
                            pl.BlockSpec((S, LANES), lambda b, h, i: (b, kb + h)),
                            pl.BlockSpec((S, LANES), lambda b, h, i: (b, vb + h)),
                            pl.BlockSpec((1, S, LANES), lambda b, h, i: (h, b, 0))],
                  out_specs=[pl.BlockSpec((tq, LANES), lambda b, h, i: (b * nq + i, h)),
                             pl.BlockSpec((tq, LANES), lambda b, h, i: (b * nq + i, h)),
                             pl.BlockSpec((1, 1, 8, tq), lambda b, h, i: (b, h, 0, i))],
                  out_shape=[jax.ShapeDtypeStruct((cfg.T, cfg.AW), BF16), jax.ShapeDtypeStruct((cfg.T, cfg.AW), F32),
                             jax.ShapeDtypeStruct((cfg.Bl, nb, 8, S), F32)],
                  scratch_shapes=[pltpu.VMEM((hp, S, LANES), BF16), pltpu.VMEM((LANES, S), BF16)],
                  compiler_params=_params("parallel", "parallel", "arbitrary"))(zm, zm, zm, cum_c)


def _attn_bwd(zm, cum_c, o, do, lse, cfg, *, name):
    S, Dh = cfg.S, cfg.Dh
    tq, t, hp = _attn_tiles(cfg)
    nq, nk, nb, per = S // tq, S // t, cfg.H // hp, tq // t
    qb, kb, vb = cfg.q_off // LANES, (cfg.q_off + cfg.AW) // LANES, (cfg.q_off + 2 * cfg.AW) // LANES
    scale = Dh ** -0.5

    def body(q_ref, k_ref, v_ref, cc_ref, o_ref, do_ref, lse_ref, dq_ref, dk_ref, dv_ref, dcc_ref,
             ka_s, qa_s, vz_s, kt_s, dd_s, dqt_s):
        li = lax.broadcasted_iota(jnp.int32, (1, LANES), 1)
        ri = lax.broadcasted_iota(jnp.int32, (LANES, 1), 0)
        key_i = lax.broadcasted_iota(jnp.int32, (t, tq), 0)
        qry_i = lax.broadcasted_iota(jnp.int32, (t, tq), 1)

        def prep(c, _):
            r = pl.multiple_of(c * t, t)
            kc, vc, qc = k_ref[pl.ds(r, t), :], v_ref[pl.ds(r, t), :], q_ref[pl.ds(r, t), :]
            prod_t = (do_ref[pl.ds(r, t), :].astype(F32) * o_ref[pl.ds(r, t), :].astype(F32)).T
            for hd in range(hp):
                own = _head_lanes(hd, cfg, hp)[0]
                ka_s[hd, pl.ds(r, t), :] = _k_aug(kc, cc_ref[0, pl.ds(r, t), hd:hd + 1], hd, cfg, hp)
                qa_s[hd, pl.ds(r, t), :] = _q_aug(qc, hd, cfg, hp)
                vz_s[hd, pl.ds(r, t), :] = jnp.where(own, vc, 0.0).astype(BF16)
                dd_s[hd:hd + 1, pl.ds(r, t)] = jnp.sum(prod_t[hd * Dh:(hd + 1) * Dh, :], axis=0, keepdims=True)
            kt_s[:, pl.ds(r, t)] = kc.T.astype(BF16)
            dqt_s[:, pl.ds(r, t)] = jnp.zeros((LANES, t), F32)
            return 0

        lax.fori_loop(0, nk, prep, 0)

        def kv_step(j, _):
            rk = pl.multiple_of(j * t, t)
            i0 = j // per

            def tile(i, carry, masked):
                rq = pl.multiple_of(i * tq, tq)
                dob = do_ref[pl.ds(rq, tq), :]
                new, dq_t = [], None
                for hd in range(hp):
                    dk_h, dv_h, dsum_h = carry[hd]
                    qa = qa_s[hd, pl.ds(rq, tq), :]
                    s = lax.dot_general(ka_s[hd, pl.ds(rk, t), :], qa, NT, preferred_element_type=F32)
                    p = jnp.exp(s - lse_ref[0, 0, hd:hd + 1, pl.ds(rq, tq)])
                    if masked:
                        p = jnp.where(key_i + (rk - rq) <= qry_i, p, 0.0)
                    dp = lax.dot_general(vz_s[hd, pl.ds(rk, t), :], dob, NT, preferred_element_type=F32)
                    ds = p * (dp - dd_s[hd:hd + 1, pl.ds(rq, tq)])
                    dsb = ds.astype(BF16)
                    dv_h = dv_h + jnp.dot(p.astype(BF16), dob, preferred_element_type=F32)
                    dk_h = dk_h + jnp.dot(dsb, qa, preferred_element_type=F32)
                    dq_h = jnp.dot(kt_s[:, pl.ds(rk, t)], dsb, preferred_element_type=F32)
                    dq_t = dq_h if hd == 0 else jnp.where((ri >= hd * Dh) & (ri < (hd + 1) * Dh), dq_h, dq_t)
                    for c0 in range(0, tq, LANES):
                        dsum_h = dsum_h + ds[:, c0:c0 + LANES]
                    new.append((dk_h, dv_h, dsum_h))
                dqt_s[:, pl.ds(rq, tq)] += dq_t * scale
                return tuple(new)

            zero = tuple((jnp.zeros((t, LANES), F32),) * 3 for _ in range(hp))
            res = lax.fori_loop(i0 + 1, nq, functools.partial(tile, masked=False), tile(i0, zero, True))
            dk, dv, dcc = res[0][0], res[0][1], jnp.zeros((t, LANES), F32)
            for hd in range(hp):
                own = _head_lanes(hd, cfg, hp)[0]
                if hd > 0:
                    dk, dv = jnp.where(own, res[hd][0], dk), jnp.where(own, res[hd][1], dv)
                dcc = dcc + jnp.where(li == hd, -jnp.sum(res[hd][2], axis=1, keepdims=True), 0.0)
            dk_ref[pl.ds(rk, t), :] = dk.astype(BF16)
            dv_ref[pl.ds(rk, t), :] = dv.astype(BF16)
            dcc_ref[0, pl.ds(rk, t), :] = dcc
            return 0

        lax.fori_loop(0, nk, kv_step, 0)

        def finish(c, _):
            r = pl.multiple_of(c * t, t)
            dq_ref[pl.ds(r, t), :] = dqt_s[:, pl.ds(r, t)].T.astype(BF16)
            return 0

        lax.fori_loop(0, nk, finish, 0)

    blk = pl.BlockSpec((S, LANES), lambda b, h: (b, h))
    cc = pl.BlockSpec((1, S, LANES), lambda b, h: (h, b, 0))
    os_ = jax.ShapeDtypeStruct((cfg.T, cfg.AW), BF16)
    return _pcall(body, name=name, grid=(cfg.Bl, nb),
                  in_specs=[pl.BlockSpec((S, LANES), lambda b, h: (b, qb + h)), pl.BlockSpec((S, LANES), lambda b, h: (b, kb + h)),
                            pl.BlockSpec((S, LANES), lambda b, h: (b, vb + h)), cc, blk, blk,
                            pl.BlockSpec((1, 1, 8, S), lambda b, h: (b, h, 0, 0))],
                  out_specs=[blk, blk, blk, cc],
                  out_shape=[os_, os_, os_, jax.ShapeDtypeStruct((nb, cfg.T, LANES), F32)],
                  scratch_shapes=[pltpu.VMEM((hp, S, LANES), BF16)] * 3 + [pltpu.VMEM((LANES, S), BF16),
                                  pltpu.VMEM((8, S), F32), pltpu.VMEM((LANES, S), F32)],
                  compiler_params=_params("parallel", "parallel"))(zm, zm, zm, cum_c, o, do, lse)


FFN_HALO = 8
FFN_CHUNK = 16


def _ffn_tiles(cfg):
    assert cfg.KF - 1 <= FFN_HALO
    return _pick(cfg.S, (512, 256, 128, 64, 32, 16, 8)), _pick(cfg.F, (256, 128))


def _gelu(x):
    return 0.5 * x * (1.0 + lax.erf(x * (2.0 ** -0.5)))


def _gelu_grad(x):
    return 0.5 * (1.0 + lax.erf(x * (2.0 ** -0.5))) + x * jnp.exp(-0.5 * x * x) * ((2.0 * math.pi) ** -0.5)


def _ffn_conv_fwd(h0, w, cb, cfg, *, name):
    KF, HALO = cfg.KF, FFN_HALO
    ts, tf = _ffn_tiles(cfg)
    tpb, nf = cfg.S // ts, cfg.F // tf
    lead = HALO - (KF - 1)

    CH = FFN_CHUNK

    def body(g_ref, gp_ref, l_ref, lp_ref, wg_ref, wl_ref, cg_ref, cl_ref, o_ref, hg_ref, hl_ref, g_s, l_s):
        first = pl.program_id(1) % tpb == 0
        for s, main, prev in ((g_s, g_ref, gp_ref), (l_s, l_ref, lp_ref)):
            s[0:HALO, :] = jnp.where(first, 0.0, prev[...])
            s[HALO:HALO + CH, :] = main[0:CH, :]
        wg, wl = [wg_ref[k:k + 1, :] for k in range(KF)], [wl_ref[k:k + 1, :] for k in range(KF)]
        for r0 in range(0, ts, CH):
            hg, hl = cg_ref[...], cl_ref[...]
            for k in range(KF):
                if r0 == 0:
                    xg, xl = g_s[lead + k:lead + k + CH, :], l_s[lead + k:lead + k + CH, :]
                else:
                    a = r0 - (KF - 1) + k
                    xg, xl = g_ref[a:a + CH, :], l_ref[a:a + CH, :]
                hg, hl = hg + wg[k] * xg, hl + wl[k] * xl
            o_ref[r0:r0 + CH, :] = (_gelu(hg) * hl).astype(BF16)
            hg_ref[r0:r0 + CH, :], hl_ref[r0:r0 + CH, :] = hg, hl

    hb = ts // HALO
    prev = lambda off: pl.BlockSpec((HALO, tf), lambda j, i: (jnp.maximum(i * hb - 1, 0), off + j))
    main = lambda off: pl.BlockSpec((ts, tf), lambda j, i: (i, off + j))
    wsp = lambda off: pl.BlockSpec((8, tf), lambda j, i: (0, off + j))
    vsp = lambda off: pl.BlockSpec((1, tf), lambda j, i: (0, off + j))
    hs = jax.ShapeDtypeStruct((cfg.T, cfg.F), F32)
    return _pcall(body, name=name, grid=(nf, cfg.T // ts),
                  in_specs=[main(0), prev(0), main(nf), prev(nf), wsp(0), wsp(nf), vsp(0), vsp(nf)],
                  out_specs=[main(0)] * 3, out_shape=[jax.ShapeDtypeStruct((cfg.T, cfg.F), BF16), hs, hs],
                  scratch_shapes=[pltpu.VMEM((HALO + CH, tf), F32)] * 2,
                  compiler_params=_params("parallel", "parallel"))(h0, h0, h0, h0, w, w, cb, cb)


def _ffn_conv_bwd(df, h0, hg, hl, w, cfg, *, name):
    KF, HALO = cfg.KF, FFN_HALO
    ts, tf = _ffn_tiles(cfg)
    tpb, nf = cfg.S // ts, cfg.F // tf
    ext = ts + HALO

    CH = FFN_CHUNK

    def body(g_ref, l_ref, hg_ref, hgn_ref, hl_ref, hln_ref, d_ref, dn_ref, wg_ref, wl_ref,
             dg_ref, dl_ref, dwg_ref, dwl_ref, dcg_ref, dcl_ref, dhg_s, dhl_s):
        i = pl.program_id(1)
        last = i % tpb == tpb - 1

        @pl.when(i == 0)
        def _():
            dwg_ref[...] = jnp.zeros_like(dwg_ref)
            dwl_ref[...] = jnp.zeros_like(dwl_ref)
            dcg_ref[...] = jnp.zeros_like(dcg_ref)
            dcl_ref[...] = jnp.zeros_like(dcl_ref)

        wg, wl = [wg_ref[k:k + 1, :] for k in range(KF)], [wl_ref[k:k + 1, :] for k in range(KF)]

        def grads(hg, hl, d):
            return d * hl * _gelu_grad(hg), d * _gelu(hg)

        for r0 in range(0, ts, CH):
            dhg_s[r0:r0 + CH, :], dhl_s[r0:r0 + CH, :] = grads(hg_ref[r0:r0 + CH, :], hl_ref[r0:r0 + CH, :], d_ref[r0:r0 + CH, :])
        dhg_s[ts:ext, :], dhl_s[ts:ext, :] = grads(hgn_ref[...], hln_ref[...], jnp.where(last, 0.0, dn_ref[...]))

        for dh_s, x_ref, wk, dx_ref, dw_ref, dc_ref in ((dhg_s, g_ref, wg, dg_ref, dwg_ref, dcg_ref),
                                                        (dhl_s, l_ref, wl, dl_ref, dwl_ref, dcl_ref)):
            dw_acc = [jnp.zeros((CH, tf), F32) for _ in range(KF)]
            for r0 in range(0, ts, CH):
                x = x_ref[r0:r0 + CH, :]
                dx = jnp.zeros((CH, tf), F32)
                for k in range(KF):
                    dhk = dh_s[r0 + KF - 1 - k:r0 + KF - 1 - k + CH, :]
                    dx = dx + wk[k] * dhk
                    dw_acc[k] = dw_acc[k] + x * dhk
                    if k == KF - 1:
                        dc_acc = dhk if r0 == 0 else dc_acc + dhk
                dx_ref[r0:r0 + CH, :] = dx.astype(BF16)
            for k in range(KF):
                dw_ref[k:k + 1, :] += jnp.sum(dw_acc[k], axis=0, keepdims=True)
            dc_ref[...] += jnp.sum(dc_acc, axis=0, keepdims=True)

    hb = ts // HALO
    nhb = cfg.T // HALO
    main = lambda off: pl.BlockSpec((ts, tf), lambda j, i: (i, off + j))
    nxt = pl.BlockSpec((HALO, tf), lambda j, i: (jnp.minimum((i + 1) * hb, nhb - 1), j))
    wsp = lambda off: pl.BlockSpec((8, tf), lambda j, i: (0, off + j))
    vsp = pl.BlockSpec((1, tf), lambda j, i: (0, j))
    dxs, dws, dcs = (jax.ShapeDtypeStruct((cfg.T, cfg.F), BF16), jax.ShapeDtypeStruct((8, cfg.F), F32),
                     jax.ShapeDtypeStruct((1, cfg.F), F32))
    return _pcall(body, name=name, grid=(nf, cfg.T // ts),
                  in_specs=[main(0), main(nf), main(0), nxt, main(0), nxt, main(0), nxt, wsp(0), wsp(nf)],
                  out_specs=[main(0), main(0), wsp(0), wsp(0), vsp, vsp],
                  out_shape=[dxs, dxs, dws, dws, dcs, dcs],
                  scratch_shapes=[pltpu.VMEM((ext, tf), F32)] * 2,
                  compiler_params=_params("parallel", "arbitrary"))(h0, h0, hg, hg, hl, hl, df, df, w, w)


def _ada_fwd(c_all, w, b, *, name):
    L, D, n = w.shape
    B = c_all.shape[0]

    def body(c_ref, w_ref, b_ref, o_ref):
        c = c_ref[...]
        act = (c * _sigmoid(c)).astype(BF16)
        o_ref[0] = jnp.dot(act, w_ref[0].astype(BF16), preferred_element_type=F32) + b_ref[0]

    return _pcall(body, name=name, grid=(L,),
                  in_specs=[pl.BlockSpec((B, D), lambda l: (0, 0)), pl.BlockSpec((1, D, n), lambda l: (l, 0, 0)),
                            pl.BlockSpec((1, 1, n), lambda l: (l, 0, 0))],
                  out_specs=pl.BlockSpec((1, B, n), lambda l: (l, 0, 0)), out_shape=jax.ShapeDtypeStruct((L, B, n), F32),
                  compiler_params=_params("parallel"))(c_all, w, b)


def _ada_bwd(c_all, dmod, *, name):
    L, B, n = dmod.shape
    D = c_all.shape[1]

    def body(c_ref, d_ref, o_ref):
        c = c_ref[...]
        act = (c * _sigmoid(c)).astype(BF16)
        o_ref[0] = lax.dot_general(act, d_ref[0].astype(BF16), TN, preferred_element_type=F32)

    return _pcall(body, name=name, grid=(L,),
                  in_specs=[pl.BlockSpec((B, D), lambda l: (0, 0)), pl.BlockSpec((1, B, n), lambda l: (l, 0, 0))],
                  out_specs=pl.BlockSpec((1, D, n), lambda l: (l, 0, 0)), out_shape=jax.ShapeDtypeStruct((L, D, n), F32),
                  compiler_params=_params("parallel"))(c_all, dmod)


def _slot_sum(x, *, name):
    n, R, W = x.shape
    tr = _pick(R, (256, 128, 64, 32, 16, 8))

    def body(x_ref, o_ref):
        acc = x_ref[0].astype(F32)
        for k in range(1, n):
            acc = acc + x_ref[k].astype(F32)
        o_ref[...] = acc

    return _pcall(body, name=name, grid=(R // tr,), in_specs=[pl.BlockSpec((n, tr, W), lambda i: (0, i, 0))],
                  out_specs=pl.BlockSpec((tr, W), lambda i: (i, 0)), out_shape=jax.ShapeDtypeStruct((R, W), F32),
                  compiler_params=_params("parallel"))(x)


def _adamw_math(g, w, m, v):
    r1, r2 = 1.0 / (1.0 - ADAM_B1 ** ADAM_STEP), 1.0 / (1.0 - ADAM_B2 ** ADAM_STEP)
    m2 = ADAM_B1 * m + (1.0 - ADAM_B1) * g
    v2 = ADAM_B2 * v + (1.0 - ADAM_B2) * (g * g)
    return -ADAM_LR * ((m2 * r1) / (jnp.sqrt(v2 * r2) + ADAM_EPS) + ADAM_WD * w), m2, v2


def _adamw_many(gs, ws, ms, vs, *, name):
    n = len(gs)

    def body(*refs):
        ins, outs = refs[:4 * n], refs[4 * n:]
        for i in range(n):
            d, m2, v2 = _adamw_math(*(ins[j * n + i][...] for j in range(4)))
            outs[i][...], outs[n + i][...], outs[2 * n + i][...] = d, m2, v2

    vm = pl.BlockSpec(memory_space=pltpu.VMEM)
    outs = _pcall(body, name=name, in_specs=[vm] * (4 * n), out_specs=[vm] * (3 * n),
                  out_shape=[jax.ShapeDtypeStruct(a.shape, F32) for _ in range(3) for a in ws],
                  compiler_params=pltpu.CompilerParams(vmem_limit_bytes=VMEM_LIMIT))(*gs, *ws, *ms, *vs)
    return outs[:n], outs[n:2 * n], outs[2 * n:]


def _adamw(gs, w, m, v, *, name):
    n, R, W = gs.shape
    tr = _pick(R, (256, 128, 64, 32, 16, 8))

    def body(g_ref, w_ref, m_ref, v_ref, go_ref, d_ref, mo_ref, vo_ref):
        g = g_ref[0].astype(F32)
        for k in range(1, n):
            g = g + g_ref[k].astype(F32)
        go_ref[...] = g
        d_ref[...], mo_ref[...], vo_ref[...] = _adamw_math(g, w_ref[...], m_ref[...], v_ref[...])

    blk = pl.BlockSpec((tr, W), lambda i: (i, 0))
    o = jax.ShapeDtypeStruct((R, W), F32)
    return _pcall(body, name=name, grid=(R // tr,), in_specs=[pl.BlockSpec((n, tr, W), lambda i: (0, i, 0)), blk, blk, blk],
                  out_specs=[blk] * 4, out_shape=[o] * 4, compiler_params=_params("parallel"))(gs, w, m, v)


def _peer_copies(x_ref, land_ref, send_sems, recv_sems, all_to_all):
    mx, my, mc = lax.axis_index("x"), lax.axis_index("y"), lax.axis_index("c")
    me = 4 * mx + 2 * my + mc
    copies = []
    for k in range(1, N_DEV):
        px, py, pc = mx ^ ((k >> 2) & 1), my ^ ((k >> 1) & 1), mc ^ (k & 1)
        copies.append(pltpu.make_async_remote_copy(
            src_ref=x_ref.at[4 * px + 2 * py + pc] if all_to_all else x_ref, dst_ref=land_ref.at[me],
            send_sem=send_sems.at[k - 1], recv_sem=recv_sems.at[k - 1], device_id=(px, py, pc),
            device_id_type=pl.DeviceIdType.MESH))
    return copies


def _gather_two_level(x, *, name, after=None):
    def body(x_ref, *rest):
        o_ref, send_sems, recv_sems, local_sem = rest[-4:]
        mx, my, mc = lax.axis_index("x"), lax.axis_index("y"), lax.axis_index("c")
        me, sibling = (mx, my, mc), (mx, my, 1 - mc)
        chips = [(1 - mx, my), (mx, 1 - my), (1 - mx, 1 - my)]

        def slot(px, py, pc):
            return o_ref.at[4 * px + 2 * py + pc]

        def copy(k, block, to, src=None):
            return pltpu.make_async_remote_copy(
                src_ref=slot(*block) if src is None else src, dst_ref=slot(*block), send_sem=send_sems.at[k],
                recv_sem=recv_sems.at[k], device_id=to, device_id_type=pl.DeviceIdType.MESH)

        mine = pltpu.make_async_copy(x_ref, slot(*me), local_sem)
        mine.start()
        first = [copy(0, me, sibling, src=x_ref)] + [copy(1 + j, me, (*chip, mc), src=x_ref) for j, chip in enumerate(chips)]
        for cp in first:
            cp.start()
        passed = [copy(4 + j, (*chip, mc), sibling) for j, chip in enumerate(chips)]
        for j, chip in enumerate(chips):
            copy(1 + j, (*chip, mc), me).wait_recv()
            passed[j].start()
        copy(0, sibling, me).wait_recv()
        for j, chip in enumerate(chips):
            copy(4 + j, (*chip, 1 - mc), me).wait_recv()
        for cp in first + passed:
            cp.wait_send()
        mine.wait()

    anyspec = pl.BlockSpec(memory_space=pl.ANY)
    args = [x] if after is None else [x, after]
    return _pcall(body, name=name, in_specs=[anyspec] * len(args), out_specs=anyspec,
                  out_shape=jax.ShapeDtypeStruct((N_DEV,) + tuple(x.shape), x.dtype),
                  scratch_shapes=[pltpu.SemaphoreType.DMA((N_DEV - 1,)), pltpu.SemaphoreType.DMA((N_DEV - 1,)),
                                  pltpu.SemaphoreType.DMA(())])(*args)


_HBM = pl.BlockSpec(memory_space=pltpu.HBM)
_SEM = pl.BlockSpec(memory_space=pltpu.SEMAPHORE)
_EFFECT = pltpu.SideEffectType.DATAFLOW_SIDE_EFFECTING


def _exchange_start(x, *, all_to_all, name, after=None):
    blk = x.shape[1:] if all_to_all else x.shape
    land = lax.empty((N_DEV,) + tuple(blk), x.dtype)
    has_after = after is not None

    def body(*refs):
        x_ref, land_ref = refs[0], refs[1]
        send_sems, recv_sems, _, _, token, local_sem = refs[2 + has_after:]
        me = 4 * lax.axis_index("x") + 2 * lax.axis_index("y") + lax.axis_index("c")
        mine = pltpu.make_async_copy(x_ref.at[me] if all_to_all else x_ref, land_ref.at[me], local_sem)
        mine.start()
        mine.wait()
        for cp in _peer_copies(x_ref, land_ref, send_sems, recv_sems, all_to_all):
            cp.start()
        token[...] = jnp.zeros_like(token)

    n_sem = pltpu.SemaphoreType.DMA((N_DEV - 1,))
    args = [pltpu.with_memory_space_constraint(x, pltpu.HBM), pltpu.with_memory_space_constraint(land, pltpu.HBM)]
    in_specs = [_HBM, _HBM]
    if has_after:
        args.append(after)
        in_specs.append(pl.BlockSpec(memory_space=pl.ANY))
    send_sems, recv_sems, x_thru, land_thru, token = _pcall(
        body, name=name, in_specs=in_specs,
        out_shape=(n_sem, n_sem, pltpu.HBM(x.shape, x.dtype), pltpu.HBM(land.shape, land.dtype),
                   jax.ShapeDtypeStruct((8, LANES), F32)),
        out_specs=(_SEM, _SEM, _HBM, _HBM, pl.BlockSpec(memory_space=pltpu.VMEM)), input_output_aliases={0: 2, 1: 3},
        scratch_shapes=[pltpu.SemaphoreType.DMA(())],
        compiler_params=pltpu.CompilerParams(has_side_effects=_EFFECT))(*args)
    return (send_sems, recv_sems, x_thru, land_thru, all_to_all), token


def _exchange_wait(state, after, *, name):
    send_sems, recv_sems, x_thru, land_thru, all_to_all = state

    def body(x_ref, land_ref, send_sems, recv_sems, after_ref, x_dead, landed):
        for cp in _peer_copies(x_ref, land_ref, send_sems, recv_sems, all_to_all):
            cp.wait_send()
            cp.wait_recv()

    return _pcall(
        body, name=name, in_specs=(_HBM, _HBM, _SEM, _SEM, pl.BlockSpec(memory_space=pl.ANY)),
        out_shape=(pltpu.HBM(x_thru.shape, x_thru.dtype), pltpu.HBM(land_thru.shape, land_thru.dtype)),
        out_specs=(_HBM, _HBM), input_output_aliases={0: 0, 1: 1},
        compiler_params=pltpu.CompilerParams(has_side_effects=_EFFECT))(x_thru, land_thru, send_sems, recv_sems, after)[1]


def _exchange(x, *, all_to_all, name, after=None):
    blk = x.shape[1:] if all_to_all else x.shape

    def body(x_ref, *rest):
        o_ref, send_sems, recv_sems, local_sem = rest[-4:]
        me = 4 * lax.axis_index("x") + 2 * lax.axis_index("y") + lax.axis_index("c")
        mine = pltpu.make_async_copy(x_ref.at[me] if all_to_all else x_ref, o_ref.at[me], local_sem)
        mine.start()
        copies = _peer_copies(x_ref, o_ref, send_sems, recv_sems, all_to_all)
        for cp in copies:
            cp.start()
        for cp in copies:
            cp.wait()
        mine.wait()

    anyspec = pl.BlockSpec(memory_space=pl.ANY)
    args = [x] if after is None else [x, after]
    return _pcall(body, name=name, in_specs=[anyspec] * len(args), out_specs=anyspec,
                  out_shape=jax.ShapeDtypeStruct((N_DEV,) + tuple(blk), x.dtype),
                  scratch_shapes=[pltpu.SemaphoreType.DMA((N_DEV - 1,)), pltpu.SemaphoreType.DMA((N_DEV - 1,)),
                                  pltpu.SemaphoreType.DMA(())])(*args)


PACK_ROWS = 16


def _pack(arrs, width, dtype, lead=0):
    parts, segs, r = [], [], 0
    for a in arrs:
        lshape, shape = a.shape[:lead], a.shape[lead:]
        n = math.prod(shape)
        rows = -(-n // width)
        rows_p = -(-rows // PACK_ROWS) * PACK_ROWS
        if n == rows * width:
            blk = a.reshape(lshape + (rows, width)).astype(dtype)
            parts.append(jnp.pad(blk, [(0, 0)] * lead + [(0, rows_p - rows), (0, 0)]) if rows_p > rows else blk)
        else:
            flat = jnp.pad(a.reshape(lshape + (n,)).astype(dtype), [(0, 0)] * lead + [(0, rows_p * width - n)])
            parts.append(flat.reshape(lshape + (rows_p, width)))
        segs.append((r, n, shape))
        r += rows_p
    return jnp.concatenate(parts, axis=lead), segs


def _unpack(p, segs):
    lshape, width = p.shape[:-2], p.shape[-1]
    outs = []
    for r, n, shape in segs:
        rows = -(-n // width)
        blk = p[..., r:r + rows, :]
        if n != rows * width:
            blk = blk.reshape(lshape + (rows * width,))[..., :n]
        outs.append(blk.reshape(lshape + shape))
    return outs


def _split_cols(a, f_off, h):
    return jnp.concatenate([a[..., :f_off], a[..., f_off + h:]], axis=-1), a[..., f_off:f_off + h]


def _merge_cols(main, f, f_off):
    return jnp.concatenate([main[..., :f_off], f, main[..., f_off:]], axis=-1)


def _pad_to(a, n, axis):
    pad = [(0, 0)] * a.ndim
    pad[axis] = (0, n - a.shape[axis])
    return jnp.pad(a, pad)


def kernel(x, c, w_ada, b_ada, w_in, b_in, conv_a_w, conv_a_b, ln_conv_g, ln_conv_b, w_conv_proj, w_attn_proj, w_mix_out, b_mix_out, ln1_g, ln1_b, w_ffn_up, ffn_conv_w, ffn_conv_b, w_ffn_down, ln2_g, ln2_b, loss_target, m_w_ada, m_b_ada, m_w_in, m_b_in, m_conv_a_w, m_conv_a_b, m_ln_conv_g, m_ln_conv_b, m_w_conv_proj, m_w_attn_proj, m_w_mix_out, m_b_mix_out, m_ln1_g, m_ln1_b, m_w_ffn_up, m_ffn_conv_w, m_ffn_conv_b, m_w_ffn_down, m_ln2_g, m_ln2_b, v_w_ada, v_b_ada, v_w_in, v_b_in, v_conv_a_w, v_conv_a_b, v_ln_conv_g, v_ln_conv_b, v_w_conv_proj, v_w_attn_proj, v_w_mix_out, v_b_mix_out, v_ln1_g, v_ln1_b, v_w_ffn_up, v_ffn_conv_w, v_ffn_conv_b, v_w_ffn_down, v_ln2_g, v_ln2_b):
    L, D = w_ada.shape[0], w_ada.shape[1]
    Bl, S, _ = x.shape
    C, KW, AW = conv_a_b.shape[1], conv_a_w.shape[1], w_attn_proj.shape[1]
    F, KF, n_in_all = ffn_conv_b.shape[1] // 2, ffn_conv_w.shape[1], b_in.shape[1]
    H = n_in_all - 2 * C - 3 * AW - 2 * D
    cfg = Cfg(L=L, Bl=Bl, S=S, D=D, C=C, KW=KW, H=H, Dh=AW // H, F=F, KF=KF)
    T, NM = cfg.T, cfg.NM
    f_off = 2 * C + 3 * AW
    n_ada = w_ada.shape[2]
    me = 4 * lax.axis_index("x") + 2 * lax.axis_index("y") + lax.axis_index("c")

    def my_cols(a, n):
        return lax.dynamic_slice_in_dim(a, me * n, n, axis=a.ndim - 1)

    spack, ssegs = _pack([c, conv_a_w, ffn_conv_w], D, F32)
    c_g, caw_g, fcw_g = _unpack(_exchange(spack, all_to_all=False, name="gather_small"), ssegs)
    c_all = c_g.reshape(N_DEV * Bl, D)
    caw = _pad_to(jnp.moveaxis(caw_g, 0, 2).reshape(L, KW, C), 32, 1)
    fcw = _pad_to(jnp.moveaxis(fcw_g, 0, 2).reshape(L, KF, 2 * F), 8, 1)

    mod_part = _ada_fwd(c_all, w_ada, my_cols(b_ada, n_ada)[:, None, :], name="ada_fwd")
    mod_send = jnp.moveaxis(mod_part.reshape(L, N_DEV, Bl, n_ada), 1, 0).reshape(N_DEV, L * Bl, n_ada)
    mod_recv = _exchange(mod_send, all_to_all=True, name="exchange_mod")
    mod = jnp.moveaxis(mod_recv.reshape(N_DEV, L, Bl, n_ada), 0, 2).reshape(L, Bl, 6, 1, D)
    shift1, scale1, gate1, shift2, scale2, gate2 = (mod[:, :, i] for i in range(6))

    big_names = ["w_in", "w_conv_proj", "w_attn_proj", "w_mix_out", "w_ffn_up", "w_ffn_down"]
    transposed = (True, True, True, False, True, False)

    def shard_items(arrs, grp):
        return [arrs[i][l].T if transposed[i] else arrs[i][l] for l, i in grp]

    W = [dict() for _ in range(L)]

    def set_weights(landed, segs, grp):
        for (l, i), a in zip(grp, _unpack(landed, segs)):
            a = a.reshape((-1, a.shape[-1]))
            if i == 0:
                wm_t, wf_t = _split_cols(a.T, f_off, H)
                bm, bf = _split_cols(b_in[l], f_off, H)
                W[l].update(wm_t=wm_t.T, wf_t=_pad_to(wf_t.T, LANES, 0), bm=bm[None], bf=_pad_to(bf, LANES, 0)[None])
            else:
                W[l][("w_cp_t", "w_ap_t", "w_mo", "w_up_t", "w_dn")[i - 1]] = a

    big_w = (w_in, w_conv_proj, w_attn_proj, w_mix_out, w_ffn_up, w_ffn_down)
    w_groups = [[(l, i) for i in range(6)] for l in range(L)]
    pack, segs = _pack(shard_items(big_w, w_groups[0]), D, BF16)
    landed0 = _gather_two_level(pack, name="gather_weights_0", after=mod_recv)
    set_weights(landed0, segs, w_groups[0])
    w_state, token = {}, landed0
    for l in range(1, L):
        pack, segs = _pack(shard_items(big_w, w_groups[l]), D, BF16)
        state, token = _exchange_start(pack, all_to_all=False, name=f"gather_weights_start_{l}", after=token)
        w_state[l] = (state, pack, segs)

    def wait_weights(l, after):
        state, pack, segs = w_state[l]
        set_weights(_exchange_wait(state, after, name=f"gather_weights_wait_{l}"), segs, w_groups[l])

    xf = x.reshape(T, D)
    u = _ln_mod_fwd(xf, shift1[0], scale1[0], cfg, name="ln_mod_fwd")
    saved = []
    xin = xf
    for l in range(L):
        w = W[l]
        if l > 0:
            wait_weights(l, u)
        zm = _matmul(u, w["wm_t"], mode="nt", bias=w["bm"], name=f"in_proj_{l}", after=token if l == 0 else None)
        zf = _matmul(u, w["wf_t"], mode="nt", bias=w["bf"], name=f"in_proj_f_{l}")
        a3 = _conv_a_fwd(zm, caw[l], conv_a_b[l][None], ln_conv_g[l][None], ln_conv_b[l][None], cfg, name=f"conv_a_fwd_{l}")
        cum_c = _fgate_fwd(zf, cfg, name=f"fgate_fwd_{l}")
        o, o32, lse = _attn_fwd(zm, cum_c, cfg, name=f"attn_fwd_{l}")
        ya =_matmul(a3, w["w_cp_t"], mode="nt", name=f"conv_proj_{l}")
        yb = _matmul(o, w["w_ap_t"], mode="nt", name=f"attn_proj_{l}")
        mg = _merge_fwd(zm, ya, yb, cfg, name=f"merge_fwd_{l}")
        mix = _matmul(mg, w["w_mo"], mode="nn", bias=b_mix_out[l][None], name=f"mix_out_{l}")
        x1, u2 = _res_ln_fwd(xin, mix, gate1[l], ln1_g[l][None], ln1_b[l][None], cfg, name=f"res_ln1_fwd_{l}",
                             nxt=(shift2[l], scale2[l]))
        h0 = _matmul(u2, w["w_up_t"], mode="nt", name=f"ffn_up_{l}")
        fa, hg, hl = _ffn_conv_fwd(h0, fcw[l], ffn_conv_b[l][None], cfg, name=f"ffn_conv_fwd_{l}")
        ffn = _matmul(fa, w["w_dn"], mode="nn", name=f"ffn_down_{l}")
        saved.append(dict(x=xin, u=u, zm=zm, zf=zf, a3=a3, cum_c=cum_c, o=o, o32=o32, lse=lse, ya=ya, yb=yb, mg=mg, mix=mix,
                          x1=x1, u2=u2, h0=h0, hg=hg, hl=hl, fa=fa, ffn=ffn))
        if l + 1 < L:
            xin, u = _res_ln_fwd(x1, ffn, gate2[l], ln2_g[l][None], ln2_b[l][None], cfg, name=f"res_ln2_fwd_{l}",
                                 nxt=(shift1[l + 1], scale1[l + 1]))
        else:
            xin = _res_ln_fwd(x1, ffn, gate2[l], ln2_g[l][None], ln2_b[l][None], cfg, name=f"res_ln2_fwd_{l}")

    dx, loss_tiles = _loss_grad(xin, loss_target.reshape(T, D), cfg, name="loss_grad")
    loss = lax.psum(0.5 / D * jnp.sum(loss_tiles[:, 0, 0]), ("x", "y", "c"))

    gbig = {}
    g_groups = [[(l, i) for i in range(6)] for l in reversed(range(1, L))] + [[(0, 4), (0, 5)], [(0, 1), (0, 2), (0, 3)], [(0, 0)]]
    g_state = []

    def start_grads(after=None):
        grp = g_groups[len(g_state)]
        send, segs = _pack([gbig[k].reshape((N_DEV, -1, gbig[k].shape[1])) for k in grp], D, BF16, lead=1)
        state, tok = _exchange_start(send, all_to_all=True, name=f"exchange_grads_start_{len(g_state)}", after=after)
        g_state.append((state, send, segs, grp))
        return tok

    gsm = [dict() for _ in range(L)]
    dmods = [None] * L
    token = None
    for l in reversed(range(L)):
        w, s = W[l], saved[l]
        dres2, dffn, dg2, db2, dgate2, _ = _res_ln_bwd(dx, s["x1"], s["ffn"], gate2[l], ln2_g[l][None], cfg, name=f"res_ln2_bwd_{l}")
        dfa = _matmul(dffn, w["w_dn"], mode="nt", name=f"d_ffn_act_{l}", after=token)
        gbig[l, 5] = _matmul(s["fa"], dffn, mode="tn", name=f"dw_ffn_down_{l}")
        dh0g, dh0l, dwg, dwl, dcg, dcl = _ffn_conv_bwd(dfa, s["h0"], s["hg"], s["hl"], fcw[l], cfg, name=f"ffn_conv_bwd_{l}")
        du2 = _matmul((dh0g, dh0l), w["w_up_t"], mode="nn", name=f"d_u2_{l}")
        gbig[l, 4] = _matmul((dh0g, dh0l), s["u2"], mode="tn", name=f"dw_ffn_up_{l}")
        token = start_grads() if l == 0 else None
        dx1, dscale2, dshift2 = _ln_mod_bwd(du2, s["x1"], scale2[l], dres2, cfg, name=f"ln_mod2_bwd_{l}")
        dres1, dmix, dg1, db1, dgate1, dbmo = _res_ln_bwd(dx1, s["x"], s["mix"], gate1[l], ln1_g[l][None], cfg, name=f"res_ln1_bwd_{l}")
        dmg = _matmul(dmix, w["w_mo"], mode="nt", name=f"d_merge_{l}", after=token)
        gbig[l, 3] = _matmul(s["mg"], dmix, mode="tn", name=f"dw_mix_out_{l}")
        dya, dyb, dzga, dzgb = _merge_bwd(dmg, s["zm"], s["ya"], s["yb"], cfg, name=f"merge_bwd_{l}")
        gbig[l, 1] = _matmul(dya, s["a3"], mode="tn", name=f"dw_conv_proj_{l}")
        da3 = _matmul(dya, w["w_cp_t"], mode="nn", name=f"d_a3_{l}")
        gbig[l, 2] = _matmul(dyb, s["o"], mode="tn", name=f"dw_attn_proj_{l}")
        token = start_grads() if l == 0 else None
        do = _matmul(dyb, w["w_ap_t"], mode="nn", out_dtype=BF16, name=f"d_o_{l}", after=token)
        dq, dk, dv, dcum_c = _attn_bwd(s["zm"], s["cum_c"], s["o32"], do, s["lse"], cfg, name=f"attn_bwd_{l}")
        dzf = _fgate_bwd(dcum_c, s["zf"], cfg, name=f"fgate_bwd_{l}")
        dzglu, dcaw, dcab, dlcg, dlcb = _conv_a_bwd(da3, s["zm"], caw[l], conv_a_b[l][None], ln_conv_g[l][None],
                                                    ln_conv_b[l][None], cfg, name=f"conv_a_bwd_{l}")
        dzm = jnp.concatenate([dzglu, dq, dk, dv, dzga, dzgb], axis=1)
        du1 = _matmul(dzf, w["wf_t"], mode="nn", name=f"d_u1_f_{l}")
        du1 = _matmul(dzm, w["wm_t"], mode="nn", add=du1, name=f"d_u1_{l}")
        dwm_t = _matmul(dzm, s["u"], mode="tn", name=f"dw_in_{l}")
        dwf_t = _matmul(dzf, s["u"], mode="tn", name=f"dw_in_f_{l}")
        gbig[l, 0] = _merge_cols(dwm_t.T, dwf_t[:H].T, f_off).T
        token = start_grads() if l > 0 else None
        dbm, dbf = _colsum(dzm, name=f"db_in_{l}"), _colsum(dzf, name=f"db_in_f_{l}")
        dx, dscale1, dshift1 = _ln_mod_bwd(du1, s["x"], scale1[l], dres1, cfg, name=f"ln_mod1_bwd_{l}")
        dmods[l] = jnp.concatenate([dshift1, dscale1, dgate1, dshift2, dscale2, dgate2], axis=1).reshape(Bl, 6 * D)
        gsm[l] = dict(b_in=_merge_cols(dbm[0], dbf[0, :H], f_off), conv_a_b=dcab[0], ln_conv_g=dlcg[0], ln_conv_b=dlcb[0],
                      b_mix_out=dbmo[0], ln1_g=dg1[0], ln1_b=db1[0], ffn_conv_b=jnp.concatenate([dcg[0], dcl[0]]),
                      ln2_g=dg2[0], ln2_b=db2[0], conv_a_w=dcaw[:KW], ffn_conv_w=jnp.concatenate([dwg[:KF], dwl[:KF]], axis=1))
    grad_x = dx.reshape(Bl, S, D)

    small_names = ["b_in", "conv_a_b", "ln_conv_g", "ln_conv_b", "b_mix_out", "ln1_g", "ln1_b", "ffn_conv_b", "ln2_g", "ln2_b",
                   "conv_a_w", "ffn_conv_w"]
    gs_list = [jnp.stack(dmods)] + [jnp.stack([gsm[l][n] for l in range(L)]) for n in small_names]
    gspack, gssegs = _pack(gs_list, D, F32)
    gs_all = _gather_two_level(gspack, name="gather_small_grads")
    start_grads(after=gs_all)
    dmod_all = jnp.moveaxis(_unpack(gs_all, gssegs)[0], 0, 1).reshape(L, N_DEV * Bl, 6 * D)
    g_small = dict(zip(small_names, _unpack(_slot_sum(gs_all, name="sum_small_grads"), gssegs)[1:]))
    g_small["conv_a_w"] = my_cols(g_small["conv_a_w"], C // N_DEV)
    g_small["ffn_conv_w"] = my_cols(g_small["ffn_conv_w"], 2 * F // N_DEV)
    g_small["w_ada"] = _ada_bwd(c_all, my_cols(dmod_all, n_ada), name="ada_bwd")
    g_small["b_ada"] = jnp.stack([_colsum(dmod_all[l], name=f"db_ada_{l}")[0] for l in range(L)])

    given = dict(w_in=(w_in, m_w_in, v_w_in), w_conv_proj=(w_conv_proj, m_w_conv_proj, v_w_conv_proj),
                 w_attn_proj=(w_attn_proj, m_w_attn_proj, v_w_attn_proj), w_mix_out=(w_mix_out, m_w_mix_out, v_w_mix_out),
                 w_ffn_up=(w_ffn_up, m_w_ffn_up, v_w_ffn_up), w_ffn_down=(w_ffn_down, m_w_ffn_down, v_w_ffn_down),
                 w_ada=(w_ada, m_w_ada, v_w_ada), b_ada=(b_ada, m_b_ada, v_b_ada), b_in=(b_in, m_b_in, v_b_in),
                 conv_a_w=(conv_a_w, m_conv_a_w, v_conv_a_w), conv_a_b=(conv_a_b, m_conv_a_b, v_conv_a_b),
                 ln_conv_g=(ln_conv_g, m_ln_conv_g, v_ln_conv_g), ln_conv_b=(ln_conv_b, m_ln_conv_b, v_ln_conv_b),
                 b_mix_out=(b_mix_out, m_b_mix_out, v_b_mix_out), ln1_g=(ln1_g, m_ln1_g, v_ln1_g), ln1_b=(ln1_b, m_ln1_b, v_ln1_b),
                 ffn_conv_w=(ffn_conv_w, m_ffn_conv_w, v_ffn_conv_w), ffn_conv_b=(ffn_conv_b, m_ffn_conv_b, v_ffn_conv_b),
                 ln2_g=(ln2_g, m_ln2_g, v_ln2_g), ln2_b=(ln2_b, m_ln2_b, v_ln2_b))
    res, kinds = {}, ("grad", "delta", "new_m", "new_v")
    loc_names = ["b_ada"] + small_names
    deltas, new_ms, new_vs = _adamw_many([g_small[n] for n in loc_names], *([given[n][j] for n in loc_names] for j in range(3)),
                                         name="adamw_small")
    for n, d, m2, v2 in zip(loc_names, deltas, new_ms, new_vs):
        res["grad", n], res["delta", n], res["new_m", n], res["new_v", n] = g_small[n], d, m2, v2
    rows_ada = (L * D * n_ada // D, D)
    outs = _adamw(g_small["w_ada"].reshape((1,) + rows_ada), *(a.reshape(rows_ada) for a in given["w_ada"]), name="adamw_w_ada")
    for kind, a in zip(kinds, outs):
        res[kind, "w_ada"] = a.reshape(w_ada.shape)

    big_parts = {}
    after = outs[0]
    for gi, (state, send, segs, grp) in enumerate(g_state):
        landed = _exchange_wait(state, after, name=f"exchange_grads_wait_{gi}")
        wmv = [_pack(shard_items([given[n][j] for n in big_names], grp), D, F32)[0] for j in range(3)]
        outs = _adamw(landed, *wmv, name=f"adamw_big_{gi}")
        for kind, packed in zip(kinds, outs):
            for (l, i), a in zip(grp, _unpack(packed, segs)):
                big_parts[kind, l, i] = a.T if transposed[i] else a
        after = outs[0]
    for kind in kinds:
        for i, n in enumerate(big_names):
            res[kind, n] = jnp.stack([big_parts[kind, l, i] for l in range(L)])

    order = ["w_ada", "b_ada", "w_in", "b_in", "conv_a_w", "conv_a_b", "ln_conv_g", "ln_conv_b", "w_conv_proj", "w_attn_proj",
             "w_mix_out", "b_mix_out", "ln1_g", "ln1_b", "w_ffn_up", "ffn_conv_w", "ffn_conv_b", "w_ffn_down", "ln2_g", "ln2_b"]
    return (loss, grad_x, *[res[k, n] for k in ("grad", "delta", "new_m", "new_v") for n in order])
```

```python
import functools
import math
from typing import NamedTuple

import jax
import jax.numpy as jnp
from jax import lax
from jax.experimental import pallas as pl
from jax.experimental.pallas import tpu as pltpu

F32, BF16 = jnp.float32, jnp.bfloat16
LN_EPS = 1e-5
ADAM_LR, ADAM_B1, ADAM_B2, ADAM_EPS, ADAM_WD, ADAM_STEP = 0.001, 0.9, 0.999, 1e-08, 0.01, 10
N_DEV = 8
LANES = 128
VMEM_LIMIT = 56 * 1024 * 1024
NEG = -1e30
NT = (((1,), (1,)), ((), ()))
TN = (((0,), (0,)), ((), ()))


class Cfg(NamedTuple):
    L: int
    Bl: int
    S: int
    D: int
    C: int
    KW: int
    H: int
    Dh: int
    F: int
    KF: int

    @property
    def T(self): return self.Bl * self.S
    @property
    def AW(self): return self.H * self.Dh
    @property
    def NM(self): return 2 * self.C + 3 * self.AW + 2 * self.D
    @property
    def q_off(self): return 2 * self.C
    @property
    def g_off(self): return 2 * self.C + 3 * self.AW
    @property
    def alpha(self): return (2.0 * self.L) ** 0.25


def _pcall(body, **kw):
    return pl.pallas_call(body, **kw)


def _params(*sem):
    return pltpu.CompilerParams(dimension_semantics=sem, vmem_limit_bytes=VMEM_LIMIT)


def _pick(n, prefs):
    for p in prefs:
        if n % p == 0:
            return p
    return n


def _sigmoid(x):
    return 1.0 / (1.0 + jnp.exp(-x))


def _ln_stats(x):
    mu = jnp.mean(x, axis=-1, keepdims=True)
    xc = x - mu
    var = jnp.mean(xc * xc, axis=-1, keepdims=True)
    rstd = lax.rsqrt(var + LN_EPS)
    return xc * rstd, rstd


def _ln_bwd(dxh, xh, rstd):
    return rstd * (dxh - jnp.mean(dxh, axis=-1, keepdims=True) - xh * jnp.mean(dxh * xh, axis=-1, keepdims=True))


def _matmul(a, b, *, mode, name, bias=None, add=None, out_dtype=F32, tm=None, tn=None, tk=None, after=None):
    parts = tuple(a) if isinstance(a, (tuple, list)) else (a,)
    P = len(parts)
    if mode == "tn":
        K, Mp = parts[0].shape
        M, Kp = P * Mp, K
    else:
        M, Kp = parts[0].shape
        K, Mp = P * Kp, M
    N = b.shape[0] if mode == "nt" else b.shape[1]
    lane_tiles = (1536, 1408, 1024, 768, 512, 256, 128)
    tm = tm or _pick(Mp, lane_tiles if mode == "tn" else (1024, 512, 256, 128, 64, 32, 16, 8))
    tn = tn or _pick(N, lane_tiles)
    tk = tk or _pick(Kp, (1024, 512, 256, 128) if mode == "tn" else lane_tiles)
    nk = K // tk
    per = Mp // tm if mode == "tn" else Kp // tk
    dn = {"nn": (((1,), (0,)), ((), ())), "nt": NT, "tn": TN}[mode]
    has_bias, has_add, has_after = bias is not None, add is not None, after is not None

    def body(*refs):
        a_refs, b_ref = refs[:P], refs[P]
        pos = P + 1
        bias_ref = refs[pos] if has_bias else None
        pos += has_bias
        add_ref = refs[pos] if has_add else None
        pos += has_add + has_after
        o_ref = refs[pos]
        acc_ref = refs[pos + 1] if nk > 1 else None
        k = pl.program_id(2)

        def finish(acc):
            if has_bias:
                acc = acc + bias_ref[...]
            if has_add:
                acc = acc + add_ref[...]
            o_ref[...] = acc.astype(out_dtype)

        def accumulate(a_ref):
            part = lax.dot_general(a_ref[...], b_ref[...], dn, preferred_element_type=F32)
            if nk == 1:
                finish(part)
            else:
                @pl.when(k == 0)
                def _():
                    acc_ref[...] = part

                @pl.when(k > 0)
                def _():
                    acc_ref[...] += part

        if P == 1:
            accumulate(a_refs[0])
        else:
            step = pl.program_id(0 if mode == "tn" else 2)
            for p in range(P):
                pl.when(step // per == p)(functools.partial(accumulate, a_refs[p]))
        if nk > 1:
            @pl.when(k == nk - 1)
            def _():
                finish(acc_ref[...])

    def a_spec(p):
        if mode == "tn":
            return pl.BlockSpec((tk, tm), lambda i, j, k: (k, jnp.clip(i - p * per, 0, per - 1)))
        return pl.BlockSpec((tm, tk), lambda i, j, k: (i, jnp.clip(k - p * per, 0, per - 1)))

    b_spec = pl.BlockSpec((tn, tk), lambda i, j, k: (j, k)) if mode == "nt" else pl.BlockSpec((tk, tn), lambda i, j, k: (k, j))
    in_specs, args = [a_spec(p) for p in range(P)] + [b_spec], list(parts) + [b]
    if has_bias:
        in_specs.append(pl.BlockSpec((1, tn), lambda i, j, k: (0, j)))
        args.append(bias)
    if has_add:
        in_specs.append(pl.BlockSpec((tm, tn), lambda i, j, k: (i, j)))
        args.append(add)
    if has_after:
        in_specs.append(pl.BlockSpec(memory_space=pl.ANY))
        args.append(after)
    return _pcall(
        body, name=name, grid=(M // tm, N // tn, nk), in_specs=in_specs,
        out_specs=pl.BlockSpec((tm, tn), lambda i, j, k: (i, j)),
        out_shape=jax.ShapeDtypeStruct((M, N), out_dtype),
        scratch_shapes=[pltpu.VMEM((tm, tn), F32)] if nk > 1 else [],
        compiler_params=_params("parallel", "parallel", "arbitrary"),
    )(*args)


def _colsum(x, *, name):
    T, N = x.shape
    tr = _pick(T, (512, 256, 128, 64, 32, 16))
    tc = _pick(N, (1536, 1024, 512, 256, 128))

    def body(x_ref, o_ref):
        @pl.when(pl.program_id(1) == 0)
        def _():
            o_ref[...] = jnp.zeros_like(o_ref)

        o_ref[...] += jnp.sum(x_ref[...].astype(F32), axis=0, keepdims=True)

    return _pcall(body, name=name, grid=(N // tc, T // tr), in_specs=[pl.BlockSpec((tr, tc), lambda j, i: (i, j))],
                  out_specs=pl.BlockSpec((1, tc), lambda j, i: (0, j)), out_shape=jax.ShapeDtypeStruct((1, N), F32),
                  compiler_params=_params("parallel", "arbitrary"))(x)


def _row_tile(cfg):
    return _pick(cfg.S, (512, 256, 128, 64, 32, 16, 8))


def _ln_mod_fwd(x, shift, scale, cfg, *, name):
    tr = _row_tile(cfg)
    tpb = cfg.S // tr

    def body(x_ref, sh_ref, sc_ref, u_ref):
        xh, _ = _ln_stats(x_ref[...])
        u_ref[...] = (xh * (1.0 + sc_ref[0]) + sh_ref[0]).astype(BF16)

    row = pl.BlockSpec((tr, cfg.D), lambda i: (i, 0))
    per_b = pl.BlockSpec((1, 1, cfg.D), lambda i: (i // tpb, 0, 0))
    return _pcall(body, name=name, grid=(cfg.T // tr,), in_specs=[row, per_b, per_b], out_specs=row,
                  out_shape=jax.ShapeDtypeStruct((cfg.T, cfg.D), BF16), compiler_params=_params("parallel"))(x, shift, scale)


def _res_ln_fwd(xin, br, gate, g, b, cfg, *, name, nxt=None):
    tr = _row_tile(cfg)
    tpb = cfg.S // tr
    alpha = cfg.alpha

    def body(*refs):
        x_ref, br_ref, gt_ref, g_ref, b_ref = refs[:5]
        r = alpha * x_ref[...] + (1.0 + gt_ref[0]) * br_ref[...]
        xh, _ = _ln_stats(r)
        xo = xh * g_ref[...] + b_ref[...]
        if nxt is None:
            refs[5][...] = xo
        else:
            sh_ref, sc_ref, xo_ref, u_ref = refs[5:]
            xo_ref[...] = xo
            uh, _ = _ln_stats(xo)
            u_ref[...] = (uh * (1.0 + sc_ref[0]) + sh_ref[0]).astype(BF16)

    row = pl.BlockSpec((tr, cfg.D), lambda i: (i, 0))
    per_b = pl.BlockSpec((1, 1, cfg.D), lambda i: (i // tpb, 0, 0))
    vec = pl.BlockSpec((1, cfg.D), lambda i: (0, 0))
    in_specs, args = [row, row, per_b, vec, vec], [xin, br, gate, g, b]
    out_specs, out_shape = row, jax.ShapeDtypeStruct((cfg.T, cfg.D), F32)
    if nxt is not None:
        in_specs += [per_b, per_b]
        args += list(nxt)
        out_specs = [row, row]
        out_shape = [out_shape, jax.ShapeDtypeStruct((cfg.T, cfg.D), BF16)]
    return _pcall(body, name=name, grid=(cfg.T // tr,), in_specs=in_specs, out_specs=out_specs, out_shape=out_shape,
                  compiler_params=_params("parallel"))(*args)


def _loss_grad(y, tgt, cfg, *, name):
    tr = _row_tile(cfg)
    nt = cfg.T // tr
    inv_d = 1.0 / cfg.D

    def body(y_ref, t_ref, dy_ref, ls_ref):
        e = y_ref[...] - t_ref[...]
        dy_ref[...] = e * inv_d
        ls_ref[...] = jnp.full((1, 1, LANES), jnp.sum(e * e), F32)

    row = pl.BlockSpec((tr, cfg.D), lambda i: (i, 0))
    return _pcall(body, name=name, grid=(nt,), in_specs=[row, row],
                  out_specs=[row, pl.BlockSpec((1, 1, LANES), lambda i: (i, 0, 0))],
                  out_shape=[jax.ShapeDtypeStruct((cfg.T, cfg.D), F32), jax.ShapeDtypeStruct((nt, 1, LANES), F32)],
                  compiler_params=_params("parallel"))(y, tgt)


def _res_ln_bwd(dy, xin, br, gate, g, cfg, *, name, mod=None):
    tr = _row_tile(cfg)
    tpb = cfg.S // tr
    alpha = cfg.alpha
    fused = mod is not None

    def body(*refs):
        if fused:
            du_ref, xa_ref, sc_ref, dres_ref, x_ref, br_ref, gt_ref, g_ref = refs[:8]
            dx_ref, dbr_ref, dg_ref, db_ref, dgt_ref, dbs_ref, dsc_ref, dsh_ref = refs[8:]
        else:
            dy_ref, x_ref, br_ref, gt_ref, g_ref, dx_ref, dbr_ref, dg_ref, db_ref, dgt_ref, dbs_ref = refs
        i = pl.program_id(0)

        @pl.when(i == 0)
        def _():
            dg_ref[...] = jnp.zeros_like(dg_ref)
            db_ref[...] = jnp.zeros_like(db_ref)
            dbs_ref[...] = jnp.zeros_like(dbs_ref)

        @pl.when(i % tpb == 0)
        def _():
            dgt_ref[...] = jnp.zeros_like(dgt_ref)
            if fused:
                dsc_ref[...] = jnp.zeros_like(dsc_ref)
                dsh_ref[...] = jnp.zeros_like(dsh_ref)

        if fused:
            du = du_ref[...]
            ah, arstd = _ln_stats(xa_ref[...])
            dsc_ref[0] += jnp.sum(du * ah, axis=0, keepdims=True)
            dsh_ref[0] += jnp.sum(du, axis=0, keepdims=True)
            dy = _ln_bwd(du * (1.0 + sc_ref[0]), ah, arstd) + dres_ref[...]
        else:
            dy = dy_ref[...]
        brv, one_gate = br_ref[...], 1.0 + gt_ref[0]
        xh, rstd = _ln_stats(alpha * x_ref[...] + one_gate * brv)
        dg_ref[...] += jnp.sum(dy * xh, axis=0, keepdims=True)
        db_ref[...] += jnp.sum(dy, axis=0, keepdims=True)
        dr = _ln_bwd(dy * g_ref[...], xh, rstd)
        dx_ref[...] = alpha * dr
        dbr = one_gate * dr
        dbr_ref[...] = dbr.astype(BF16)
        dbs_ref[...] += jnp.sum(dbr, axis=0, keepdims=True)
        dgt_ref[0] += jnp.sum(dr * brv, axis=0, keepdims=True)

    row = pl.BlockSpec((tr, cfg.D), lambda i: (i, 0))
    per_b = pl.BlockSpec((1, 1, cfg.D), lambda i: (i // tpb, 0, 0))
    vec = pl.BlockSpec((1, cfg.D), lambda i: (0, 0))
    vs = jax.ShapeDtypeStruct((1, cfg.D), F32)
    bs = jax.ShapeDtypeStruct((cfg.Bl, 1, cfg.D), F32)
    in_specs, args = [row, row, row, per_b, vec], [dy, xin, br, gate, g]
    out_specs = [row, row, vec, vec, per_b, vec]
    out_shape = [jax.ShapeDtypeStruct((cfg.T, cfg.D), F32), jax.ShapeDtypeStruct((cfg.T, cfg.D), BF16), vs, vs, bs, vs]
    if fused:
        in_specs, args = [row, row, per_b, row] + in_specs[1:], list(mod) + args[1:]
        out_specs, out_shape = out_specs + [per_b, per_b], out_shape + [bs, bs]
    return _pcall(body, name=name, grid=(cfg.T // tr,), in_specs=in_specs, out_specs=out_specs, out_shape=out_shape,
                  compiler_params=_params("arbitrary"))(*args)


def _ln_mod_bwd(du, xin, scale, dres, cfg, *, name):
    tr = _row_tile(cfg)
    tpb = cfg.S // tr

    def body(du_ref, x_ref, sc_ref, dres_ref, dx_ref, dsc_ref, dsh_ref):
        @pl.when(pl.program_id(0) % tpb == 0)
        def _():
            dsc_ref[...] = jnp.zeros_like(dsc_ref)
            dsh_ref[...] = jnp.zeros_like(dsh_ref)

        du = du_ref[...]
        xh, rstd = _ln_stats(x_ref[...])
        dsc_ref[0] += jnp.sum(du * xh, axis=0, keepdims=True)
        dsh_ref[0] += jnp.sum(du, axis=0, keepdims=True)
        dx_ref[...] = _ln_bwd(du * (1.0 + sc_ref[0]), xh, rstd) + dres_ref[...]

    row = pl.BlockSpec((tr, cfg.D), lambda i: (i, 0))
    per_b = pl.BlockSpec((1, 1, cfg.D), lambda i: (i // tpb, 0, 0))
    bs = jax.ShapeDtypeStruct((cfg.Bl, 1, cfg.D), F32)
    return _pcall(body, name=name, grid=(cfg.T // tr,), in_specs=[row, row, per_b, row], out_specs=[row, per_b, per_b],
                  out_shape=[jax.ShapeDtypeStruct((cfg.T, cfg.D), F32), bs, bs],
                  compiler_params=_params("arbitrary"))(du, xin, scale, dres)


def _merge_tiles(cfg):
    tr = _pick(cfg.T, (512, 256, 128, 64, 32, 16))
    tc = _pick(math.gcd(cfg.g_off, cfg.D), (512, 256, 128))
    return tr, tc


def _merge_fwd(zm, ya, yb, cfg, *, name):
    tr, tc = _merge_tiles(cfg)
    ga0, gb0 = cfg.g_off // tc, (cfg.g_off + cfg.D) // tc

    def body(ga_ref, gb_ref, ya_ref, yb_ref, m_ref):
        m_ref[...] = (_sigmoid(ga_ref[...]) * ya_ref[...] + _sigmoid(gb_ref[...]) * yb_ref[...]).astype(BF16)

    blk = pl.BlockSpec((tr, tc), lambda i, j: (i, j))
    return _pcall(body, name=name, grid=(cfg.T // tr, cfg.D // tc),
                  in_specs=[pl.BlockSpec((tr, tc), lambda i, j: (i, ga0 + j)), pl.BlockSpec((tr, tc), lambda i, j: (i, gb0 + j)), blk, blk],
                  out_specs=blk, out_shape=jax.ShapeDtypeStruct((cfg.T, cfg.D), BF16),
                  compiler_params=_params("parallel", "parallel"))(zm, zm, ya, yb)


def _merge_bwd(dm, zm, ya, yb, cfg, *, name):
    tr, tc = _merge_tiles(cfg)
    ga0, gb0 = cfg.g_off // tc, (cfg.g_off + cfg.D) // tc

    def body(dm_ref, ga_ref, gb_ref, ya_ref, yb_ref, dya_ref, dyb_ref, dga_ref, dgb_ref):
        dm = dm_ref[...]
        ga, gb = _sigmoid(ga_ref[...]), _sigmoid(gb_ref[...])
        dya_ref[...] = (dm * ga).astype(BF16)
        dyb_ref[...] = (dm * gb).astype(BF16)
        dga_ref[...] = (dm * ya_ref[...] * ga * (1.0 - ga)).astype(BF16)
        dgb_ref[...] = (dm * yb_ref[...] * gb * (1.0 - gb)).astype(BF16)

    blk = pl.BlockSpec((tr, tc), lambda i, j: (i, j))
    o = jax.ShapeDtypeStruct((cfg.T, cfg.D), BF16)
    return _pcall(body, name=name, grid=(cfg.T // tr, cfg.D // tc),
                  in_specs=[blk, pl.BlockSpec((tr, tc), lambda i, j: (i, ga0 + j)), pl.BlockSpec((tr, tc), lambda i, j: (i, gb0 + j)), blk, blk],
                  out_specs=[blk] * 4, out_shape=[o] * 4, compiler_params=_params("parallel", "parallel"))(dm, zm, zm, ya, yb)


CONV_A_HALO = 32
CONV_A_CHUNK = 32
CONV_A_TAPS = 32
FFN_TAPS = 8


SUBLANES = 8


def _conv_a_tile(cfg):
    assert cfg.KW - 1 <= CONV_A_HALO
    return _pick(cfg.S, (256, 128, 64, 32))


def _shift_copies(src_s, sh_s):
    rows = src_s.shape[0] - SUBLANES
    for b in range(1, SUBLANES):
        sh_s[b - 1, :, :] = src_s[b:b + rows, :]


def _rows(src_s, sh_s, start, n):
    a, b = divmod(start, SUBLANES)
    return src_s[start:start + n, :] if b == 0 else sh_s[b - 1, SUBLANES * a:SUBLANES * a + n, :]


def _conv_a_fwd(zm, w, cb, g, b, cfg, *, name):
    C, KW, HALO, CH = cfg.C, cfg.KW, CONV_A_HALO, CONV_A_CHUNK
    ts = _conv_a_tile(cfg)
    tpb = cfg.S // ts
    lead = HALO - (KW - 1)

    def body(z_ref, zp_ref, w_ref, cb_ref, g_ref, b_ref, o_ref, a0_s, a0_sh):
        first = pl.program_id(0) % tpb == 0
        prev = zp_ref[:, :C] * _sigmoid(zp_ref[:, C:])
        a0_s[0:HALO, :] = jnp.where(first, 0.0, prev)
        a0_s[HALO:HALO + ts, :] = z_ref[:, :C] * _sigmoid(z_ref[:, C:])
        _shift_copies(a0_s, a0_sh)
        for r0 in range(0, ts, CH):
            acc = jnp.zeros((CH, C), F32)
            for k in range(KW):
                acc = acc + w_ref[k:k + 1, :] * _rows(a0_s, a0_sh, r0 + lead + k, CH)
            xh, _ = _ln_stats(acc + cb_ref[...])
            a2 = xh * g_ref[...] + b_ref[...]
            o_ref[r0:r0 + CH, :] = (a2 * _sigmoid(a2)).astype(BF16)

    hb = ts // HALO
    vec = pl.BlockSpec((1, C), lambda i: (0, 0))
    return _pcall(body, name=name, grid=(cfg.T // ts,),
                  in_specs=[pl.BlockSpec((ts, 2 * C), lambda i: (i, 0)),
                            pl.BlockSpec((HALO, 2 * C), lambda i: (jnp.maximum(i * hb - 1, 0), 0)),
                            pl.BlockSpec((CONV_A_TAPS, C), lambda i: (0, 0)), vec, vec, vec],
                  out_specs=pl.BlockSpec((ts, C), lambda i: (i, 0)), out_shape=jax.ShapeDtypeStruct((cfg.T, C), BF16),
                  scratch_shapes=[pltpu.VMEM((HALO + ts, C), F32), pltpu.VMEM((SUBLANES - 1, HALO + ts - SUBLANES, C), F32)],
                  compiler_params=_params("parallel"))(zm, zm, w, cb, g, b)


def _conv_a_bwd(da3, zm, w, cb, g, b, cfg, *, name):
    C, KW, HALO, CH = cfg.C, cfg.KW, CONV_A_HALO, CONV_A_CHUNK
    ts = _conv_a_tile(cfg)
    tpb = cfg.S // ts
    nt = cfg.T // ts
    lead = HALO - (KW - 1)
    ext = ts + HALO

    def body(z_ref, zp_ref, zn_ref, d_ref, dn_ref, w_ref, cb_ref, g_ref, b_ref,
             dz_ref, dw_ref, dcb_ref, dg_ref, db_ref, a0_s, d3_s, da1_s, a0_sh, da1_sh):
        i = pl.program_id(0)
        first, last = i % tpb == 0, i % tpb == tpb - 1

        @pl.when(i == 0)
        def _():
            dw_ref[...] = jnp.zeros_like(dw_ref)
            dcb_ref[...] = jnp.zeros_like(dcb_ref)
            dg_ref[...] = jnp.zeros_like(dg_ref)
            db_ref[...] = jnp.zeros_like(db_ref)

        a0_s[0:HALO, :] = jnp.where(first, 0.0, zp_ref[:, :C] * _sigmoid(zp_ref[:, C:]))
        a0_s[HALO:HALO + ts, :] = z_ref[:, :C] * _sigmoid(z_ref[:, C:])
        a0_s[HALO + ts:HALO + ext, :] = zn_ref[:, :C] * _sigmoid(zn_ref[:, C:])
        d3_s[0:ts, :] = d_ref[...]
        d3_s[ts:ext, :] = jnp.where(last, 0.0, dn_ref[...])
        _shift_copies(a0_s, a0_sh)
        dcb, dg, db = jnp.zeros((1, C), F32), jnp.zeros((1, C), F32), jnp.zeros((1, C), F32)
        for r0 in range(0, ext, CH):
            acc = jnp.zeros((CH, C), F32)
            for k in range(KW):
                acc = acc + w_ref[k:k + 1, :] * _rows(a0_s, a0_sh, r0 + lead + k, CH)
            xh, rstd = _ln_stats(acc + cb_ref[...])
            a2 = xh * g_ref[...] + b_ref[...]
            sg = _sigmoid(a2)
            da2 = d3_s[r0:r0 + CH, :] * (sg * (1.0 + a2 * (1.0 - sg)))
            da1 = _ln_bwd(da2 * g_ref[...], xh, rstd)
            da1_s[r0:r0 + CH, :] = da1
            if r0 < ts:
                dg = dg + jnp.sum(da2 * xh, axis=0, keepdims=True)
                db = db + jnp.sum(da2, axis=0, keepdims=True)
                dcb = dcb + jnp.sum(da1, axis=0, keepdims=True)
        dg_ref[...] += dg
        db_ref[...] += db
        dcb_ref[...] += dcb
        _shift_copies(da1_s, da1_sh)
        for k in range(KW):
            dwk = jnp.zeros((CH, C), F32)
            for r0 in range(0, ts, CH):
                dwk = dwk + da1_s[r0:r0 + CH, :] * _rows(a0_s, a0_sh, r0 + lead + k, CH)
            dw_ref[k:k + 1, :] += jnp.sum(dwk, axis=0, keepdims=True)
        for r0 in range(0, ts, CH):
            da0 = jnp.zeros((CH, C), F32)
            for k in range(KW):
                da0 = da0 + w_ref[k:k + 1, :] * _rows(da1_s, da1_sh, r0 + KW - 1 - k, CH)
            val, sg = z_ref[r0:r0 + CH, :C], _sigmoid(z_ref[r0:r0 + CH, C:])
            dz_ref[r0:r0 + CH, :C] = (da0 * sg).astype(BF16)
            dz_ref[r0:r0 + CH, C:] = (da0 * val * sg * (1.0 - sg)).astype(BF16)

    hb = ts // HALO
    nhb = cfg.T // HALO
    vec = pl.BlockSpec((1, C), lambda i: (0, 0))
    vs = jax.ShapeDtypeStruct((1, C), F32)
    return _pcall(body, name=name, grid=(nt,),
                  in_specs=[pl.BlockSpec((ts, 2 * C), lambda i: (i, 0)),
                            pl.BlockSpec((HALO, 2 * C), lambda i: (jnp.maximum(i * hb - 1, 0), 0)),
                            pl.BlockSpec((HALO, 2 * C), lambda i: (jnp.minimum((i + 1) * hb, nhb - 1), 0)),
                            pl.BlockSpec((ts, C), lambda i: (i, 0)),
                            pl.BlockSpec((HALO, C), lambda i: (jnp.minimum((i + 1) * hb, nhb - 1), 0)),
                            pl.BlockSpec((CONV_A_TAPS, C), lambda i: (0, 0)), vec, vec, vec],
                  out_specs=[pl.BlockSpec((ts, 2 * C), lambda i: (i, 0)), pl.BlockSpec((CONV_A_TAPS, C), lambda i: (0, 0)), vec, vec, vec],
                  out_shape=[jax.ShapeDtypeStruct((cfg.T, 2 * C), BF16), jax.ShapeDtypeStruct((CONV_A_TAPS, C), F32), vs, vs, vs],
                  scratch_shapes=[pltpu.VMEM((HALO + ext, C), F32), pltpu.VMEM((ext, C), F32), pltpu.VMEM((ext, C), F32),
                                  pltpu.VMEM((SUBLANES - 1, HALO + ext - SUBLANES, C), F32),
                                  pltpu.VMEM((SUBLANES - 1, ext - SUBLANES, C), F32)],
                  compiler_params=_params("arbitrary"))(zm, zm, zm, da3, da3, w, cb, g, b)


def _cum_tile(cfg):
    return _pick(cfg.S, (256, 128, 64, 32, 16, 8))


def _fgate_fwd(zf, cfg, *, name):
    tc = _cum_tile(cfg)
    tpb = cfg.S // tc
    hp = _attn_tiles(cfg)[2]
    nb = cfg.H // hp

    def body(z_ref, o_ref, carry):
        @pl.when(pl.program_id(0) % tpb == 0)
        def _():
            carry[...] = jnp.zeros_like(carry)

        z = z_ref[...]
        logf = jnp.minimum(z, 0.0) - jnp.log(1.0 + jnp.exp(-jnp.abs(z)))
        tri = (lax.broadcasted_iota(jnp.int32, (tc, tc), 0) >= lax.broadcasted_iota(jnp.int32, (tc, tc), 1)).astype(F32)
        cum = jnp.dot(tri, logf, precision=lax.Precision.HIGHEST, preferred_element_type=F32) + carry[...]
        carry[...] = cum[tc - 1:tc, :]
        o_ref[0] = cum
        for b in range(1, nb):
            o_ref[b] = pltpu.roll(cum, LANES - hp * b, axis=1)

    return _pcall(body, name=name, grid=(cfg.T // tc,), in_specs=[pl.BlockSpec((tc, LANES), lambda i: (i, 0))],
                  out_specs=pl.BlockSpec((nb, tc, LANES), lambda i: (0, i, 0)),
                  out_shape=jax.ShapeDtypeStruct((nb, cfg.T, LANES), F32), scratch_shapes=[pltpu.VMEM((1, LANES), F32)],
                  compiler_params=_params("arbitrary"))(zf)


def _fgate_bwd(dcum_c, zf, cfg, *, name):
    tc = _cum_tile(cfg)
    tpb = cfg.S // tc
    nt = cfg.T // tc
    hp = _attn_tiles(cfg)[2]
    nb = cfg.H // hp

    def body(d_ref, z_ref, o_ref, carry):
        @pl.when(pl.program_id(0) % tpb == 0)
        def _():
            carry[...] = jnp.zeros_like(carry)

        d = d_ref[0]
        for b in range(1, nb):
            d = d + pltpu.roll(d_ref[b], hp * b, axis=1)
        tri = (lax.broadcasted_iota(jnp.int32, (tc, tc), 0) <= lax.broadcasted_iota(jnp.int32, (tc, tc), 1)).astype(F32)
        suf = jnp.dot(tri, d, precision=lax.Precision.HIGHEST, preferred_element_type=F32) + carry[...]
        o_ref[...] = (suf * _sigmoid(-z_ref[...])).astype(BF16)
        carry[...] = suf[0:1, :]

    blk = pl.BlockSpec((tc, LANES), lambda i: (nt - 1 - i, 0))
    return _pcall(body, name=name, grid=(nt,), in_specs=[pl.BlockSpec((nb, tc, LANES), lambda i: (0, nt - 1 - i, 0)), blk],
                  out_specs=blk, out_shape=jax.ShapeDtypeStruct((cfg.T, LANES), BF16),
                  scratch_shapes=[pltpu.VMEM((1, LANES), F32)], compiler_params=_params("arbitrary"))(dcum_c, zf)


def _attn_tiles(cfg):
    assert LANES % cfg.Dh == 0 and cfg.H % (LANES // cfg.Dh) == 0
    tk = _pick(cfg.S, (256, 128))
    tq = _pick(cfg.S, (2 * tk, tk))
    return tq, tk, LANES // cfg.Dh


BIAS_LANES = 3


def _head_lanes(hd, cfg, hp):
    li = lax.broadcasted_iota(jnp.int32, (1, LANES), 1)
    own = (li >= hd * cfg.Dh) & (li < (hd + 1) * cfg.Dh)
    return own, li, ((hd + 1) % hp) * cfg.Dh


def _q_aug(q, hd, cfg, hp):
    own, li, b0 = _head_lanes(hd, cfg, hp)
    ones = ((li >= b0) & (li < b0 + BIAS_LANES)).astype(F32)
    return jnp.where(own, q * cfg.Dh ** -0.5, ones).astype(BF16)


def _k_aug(k, ck, hd, cfg, hp):
    own, li, b0 = _head_lanes(hd, cfg, hp)
    hi = ck.astype(BF16).astype(F32)
    mid = (ck - hi).astype(BF16).astype(F32)
    lo = ck - hi - mid
    bias = jnp.where(li == b0, -hi, jnp.where(li == b0 + 1, -mid, jnp.where(li == b0 + 2, -lo, 0.0)))
    return jnp.where(own, k, bias).astype(BF16)


def _attn_fwd(zm, cum_c, cfg, *, name):
    S, Dh = cfg.S, cfg.Dh
    tq, tk, hp = _attn_tiles(cfg)
    assert hp >= 2
    nq, nb, per = S // tq, cfg.H // hp, tq // tk
    qb, kb, vb = cfg.q_off // LANES, (cfg.q_off + cfg.AW) // LANES, (cfg.q_off + 2 * cfg.AW) // LANES

    def body(q_ref, k_ref, v_ref, cc_ref, o_ref, o32_ref, lse_ref, ka_s, vt_s):
        qi = pl.program_id(2)

        @pl.when(qi == 0)
        def _():
            def prep(c, _):
                r = pl.multiple_of(c * tk, tk)
                kc = k_ref[pl.ds(r, tk), :]
                for hd in range(hp):
                    ka_s[hd, pl.ds(r, tk), :] = _k_aug(kc, cc_ref[0, pl.ds(r, tk), hd:hd + 1], hd, cfg, hp)
                vt_s[:, pl.ds(r, tk)] = v_ref[pl.ds(r, tk), :].T.astype(BF16)
                return 0

            lax.fori_loop(0, S // tk, prep, 0)

        key_i = lax.broadcasted_iota(jnp.int32, (tk, tq), 0)
        qry_i = lax.broadcasted_iota(jnp.int32, (tk, tq), 1)
        qf = q_ref[...]
        qa = [_q_aug(qf, hd, cfg, hp) for hd in range(hp)]

        def scores(j):
            r = pl.multiple_of(j * tk, tk)
            return tuple(lax.dot_general(ka_s[hd, pl.ds(r, tk), :], qa[hd], NT, preferred_element_type=F32) for hd in range(hp))

        def chunk(j, s_all, carry, diag=None):
            r = pl.multiple_of(j * tk, tk)
            new = []
            for hd in range(hp):
                m, l, acc = carry[hd]
                s = s_all[hd]
                if diag is not None:
                    s = jnp.where(key_i + diag * tk <= qry_i, s, NEG)
                m_new = jnp.maximum(m, jnp.max(s, axis=0, keepdims=True))
                a = jnp.exp(m - m_new)
                p = jnp.exp(s - m_new)
                l = a * l + jnp.sum(p, axis=0, keepdims=True)
                p_hi = p.astype(BF16)
                p_lo = (p - p_hi.astype(F32)).astype(BF16)
                vt = vt_s[hd * Dh:(hd + 1) * Dh, pl.ds(r, tk)]
                acc = a * acc + (jnp.dot(vt, p_hi, preferred_element_type=F32) + jnp.dot(vt, p_lo, preferred_element_type=F32))
                new.append((m_new, l, acc))
            return tuple(new)

        init = tuple((jnp.full((1, tq), NEG, F32), jnp.zeros((1, tq), F32), jnp.zeros((Dh, tq), F32)) for _ in range(hp))
        n_full = qi * per

        def step(j, c):
            stats, s_cur = c
            s_next = scores(j + 1)
            return chunk(j, s_cur, stats), s_next

        res, s_cur = lax.fori_loop(0, n_full, step, (init, scores(0)))
        for d in range(per):
            s_next = scores(n_full + d + 1) if d + 1 < per else None
            res = chunk(n_full + d, s_cur, res, diag=d)
            s_cur = s_next
        o = jnp.concatenate([acc / l for _, l, acc in res], axis=0).T
        o_ref[...] = o.astype(BF16)
        o32_ref[...] = o
        lse_ref[...] = jnp.zeros_like(lse_ref)
        for hd in range(hp):
            lse_ref[0, 0, hd:hd + 1, :] = res[hd][0] + jnp.log(res[hd][1])

    return _pcall(body, name=name, grid=(cfg.Bl, nb, nq),
                  in_specs=[pl.BlockSpec((tq, LANES), lambda b, h, i: (b * nq + i, qb + h)),
                            pl.BlockSpec((S, LANES), lambda b, h, i: (b, kb + h)),
                            pl.BlockSpec((S, LANES), lambda b, h, i: (b, vb + h)),
                            pl.BlockSpec((1, S, LANES), lambda b, h, i: (h, b, 0))],
                  out_specs=[pl.BlockSpec((tq, LANES), lambda b, h, i: (b * nq + i, h)),
                             pl.BlockSpec((tq, LANES), lambda b, h, i: (b * nq + i, h)),
                             pl.BlockSpec((1, 1, SUBLANES, tq), lambda b, h, i: (b, h, 0, i))],
                  out_shape=[jax.ShapeDtypeStruct((cfg.T, cfg.AW), BF16), jax.ShapeDtypeStruct((cfg.T, cfg.AW), F32),
                             jax.ShapeDtypeStruct((cfg.Bl, nb, SUBLANES, S), F32)],
                  scratch_shapes=[pltpu.VMEM((hp, S, LANES), BF16), pltpu.VMEM((LANES, S), BF16)],
                  compiler_params=_params("parallel", "parallel", "arbitrary"))(zm, zm, zm, cum_c)


def _attn_bwd(zm, cum_c, o, do, lse, cfg, *, name):
    S, Dh = cfg.S, cfg.Dh
    tq, t, hp = _attn_tiles(cfg)
    nq, nk, nb, per = S // tq, S // t, cfg.H // hp, tq // t
    qb, kb, vb = cfg.q_off // LANES, (cfg.q_off + cfg.AW) // LANES, (cfg.q_off + 2 * cfg.AW) // LANES
    scale = Dh ** -0.5

    def body(q_ref, k_ref, v_ref, cc_ref, o_ref, do_ref, lse_ref, dq_ref, dk_ref, dv_ref, dcc_ref,
             ka_s, qa_s, vz_s, kt_s, dd_s, dqt_s):
        li = lax.broadcasted_iota(jnp.int32, (1, LANES), 1)
        ri = lax.broadcasted_iota(jnp.int32, (LANES, 1), 0)
        key_i = lax.broadcasted_iota(jnp.int32, (t, tq), 0)
        qry_i = lax.broadcasted_iota(jnp.int32, (t, tq), 1)

        def prep(c, _):
            r = pl.multiple_of(c * t, t)
            kc, vc, qc = k_ref[pl.ds(r, t), :], v_ref[pl.ds(r, t), :], q_ref[pl.ds(r, t), :]
            prod_t = (do_ref[pl.ds(r, t), :].astype(F32) * o_ref[pl.ds(r, t), :].astype(F32)).T
            for hd in range(hp):
                own = _head_lanes(hd, cfg, hp)[0]
                ka_s[hd, pl.ds(r, t), :] = _k_aug(kc, cc_ref[0, pl.ds(r, t), hd:hd + 1], hd, cfg, hp)
                qa_s[hd, pl.ds(r, t), :] = _q_aug(qc, hd, cfg, hp)
                vz_s[hd, pl.ds(r, t), :] = jnp.where(own, vc, 0.0).astype(BF16)
                dd_s[hd:hd + 1, pl.ds(r, t)] = jnp.sum(prod_t[hd * Dh:(hd + 1) * Dh, :], axis=0, keepdims=True)
            kt_s[:, pl.ds(r, t)] = kc.T.astype(BF16)
            dqt_s[:, pl.ds(r, t)] = jnp.zeros((LANES, t), F32)
            return 0

        lax.fori_loop(0, nk, prep, 0)

        def kv_step(j, _):
            rk = pl.multiple_of(j * t, t)
            i0 = j // per

            def tile(i, carry, masked):
                rq = pl.multiple_of(i * tq, tq)
                dob = do_ref[pl.ds(rq, tq), :]
                new, dq_t = [], None
                for hd in range(hp):
                    dk_h, dv_h, dsum_h = carry[hd]
                    qa = qa_s[hd, pl.ds(rq, tq), :]
                    s = lax.dot_general(ka_s[hd, pl.ds(rk, t), :], qa, NT, preferred_element_type=F32)
                    p = jnp.exp(s - lse_ref[0, 0, hd:hd + 1, pl.ds(rq, tq)])
                    if masked:
                        p = jnp.where(key_i + (rk - rq) <= qry_i, p, 0.0)
                    dp = lax.dot_general(vz_s[hd, pl.ds(rk, t), :], dob, NT, preferred_element_type=F32)
                    ds = p * (dp - dd_s[hd:hd + 1, pl.ds(rq, tq)])
                    dsb = ds.astype(BF16)
                    dv_h = dv_h + jnp.dot(p.astype(BF16), dob, preferred_element_type=F32)
                    dk_h = dk_h + jnp.dot(dsb, qa, preferred_element_type=F32)
                    dq_h = jnp.dot(kt_s[:, pl.ds(rk, t)], dsb, preferred_element_type=F32)
                    dq_t = dq_h if hd == 0 else jnp.where((ri >= hd * Dh) & (ri < (hd + 1) * Dh), dq_h, dq_t)
                    for c0 in range(0, tq, LANES):
                        dsum_h = dsum_h + ds[:, c0:c0 + LANES]
                    new.append((dk_h, dv_h, dsum_h))
                dqt_s[:, pl.ds(rq, tq)] += dq_t * scale
                return tuple(new)

            zero = tuple((jnp.zeros((t, LANES), F32),) * 3 for _ in range(hp))
            res = lax.fori_loop(i0 + 1, nq, functools.partial(tile, masked=False), tile(i0, zero, True))
            dk, dv, dcc = res[0][0], res[0][1], jnp.zeros((t, LANES), F32)
            for hd in range(hp):
                own = _head_lanes(hd, cfg, hp)[0]
                if hd > 0:
                    dk, dv = jnp.where(own, res[hd][0], dk), jnp.where(own, res[hd][1], dv)
                dcc = dcc + jnp.where(li == hd, -jnp.sum(res[hd][2], axis=1, keepdims=True), 0.0)
            dk_ref[pl.ds(rk, t), :] = dk.astype(BF16)
            dv_ref[pl.ds(rk, t), :] = dv.astype(BF16)
            dcc_ref[0, pl.ds(rk, t), :] = dcc
            return 0

        lax.fori_loop(0, nk, kv_step, 0)

        def finish(c, _):
            r = pl.multiple_of(c * t, t)
            dq_ref[pl.ds(r, t), :] = dqt_s[:, pl.ds(r, t)].T.astype(BF16)
            return 0

        lax.fori_loop(0, nk, finish, 0)

    blk = pl.BlockSpec((S, LANES), lambda b, h: (b, h))
    cc = pl.BlockSpec((1, S, LANES), lambda b, h: (h, b, 0))
    os_ = jax.ShapeDtypeStruct((cfg.T, cfg.AW), BF16)
    return _pcall(body, name=name, grid=(cfg.Bl, nb),
                  in_specs=[pl.BlockSpec((S, LANES), lambda b, h: (b, qb + h)), pl.BlockSpec((S, LANES), lambda b, h: (b, kb + h)),
                            pl.BlockSpec((S, LANES), lambda b, h: (b, vb + h)), cc, blk, blk,
                            pl.BlockSpec((1, 1, SUBLANES, S), lambda b, h: (b, h, 0, 0))],
                  out_specs=[blk, blk, blk, cc],
                  out_shape=[os_, os_, os_, jax.ShapeDtypeStruct((nb, cfg.T, LANES), F32)],
                  scratch_shapes=[pltpu.VMEM((hp, S, LANES), BF16)] * 3 + [pltpu.VMEM((LANES, S), BF16),
                                  pltpu.VMEM((SUBLANES, S), F32), pltpu.VMEM((LANES, S), F32)],
                  compiler_params=_params("parallel", "parallel"))(zm, zm, zm, cum_c, o, do, lse)


FFN_HALO = 8
FFN_CHUNK = 16


def _ffn_tiles(cfg):
    assert cfg.KF - 1 <= FFN_HALO
    return _pick(cfg.S, (512, 256, 128, 64, 32, 16, 8)), _pick(cfg.F, (256, 128))


def _gelu(x):
    return 0.5 * x * (1.0 + lax.erf(x * (2.0 ** -0.5)))


def _gelu_grad(x):
    return 0.5 * (1.0 + lax.erf(x * (2.0 ** -0.5))) + x * jnp.exp(-0.5 * x * x) * ((2.0 * math.pi) ** -0.5)


def _ffn_conv_fwd(h0, w, cb, cfg, *, name):
    KF, HALO = cfg.KF, FFN_HALO
    ts, tf = _ffn_tiles(cfg)
    tpb, nf = cfg.S // ts, cfg.F // tf
    lead = HALO - (KF - 1)

    CH = FFN_CHUNK

    def body(g_ref, gp_ref, l_ref, lp_ref, wg_ref, wl_ref, cg_ref, cl_ref, o_ref, hg_ref, hl_ref, g_s, l_s):
        first = pl.program_id(1) % tpb == 0
        for s, main, prev in ((g_s, g_ref, gp_ref), (l_s, l_ref, lp_ref)):
            s[0:HALO, :] = jnp.where(first, 0.0, prev[...])
            s[HALO:HALO + CH, :] = main[0:CH, :]
        wg, wl = [wg_ref[k:k + 1, :] for k in range(KF)], [wl_ref[k:k + 1, :] for k in range(KF)]
        for r0 in range(0, ts, CH):
            hg, hl = cg_ref[...], cl_ref[...]
            for k in range(KF):
                if r0 == 0:
                    xg, xl = g_s[lead + k:lead + k + CH, :], l_s[lead + k:lead + k + CH, :]
                else:
                    a = r0 - (KF - 1) + k
                    xg, xl = g_ref[a:a + CH, :], l_ref[a:a + CH, :]
                hg, hl = hg + wg[k] * xg, hl + wl[k] * xl
            o_ref[r0:r0 + CH, :] = (_gelu(hg) * hl).astype(BF16)
            hg_ref[r0:r0 + CH, :], hl_ref[r0:r0 + CH, :] = hg, hl

    hb = ts // HALO
    prev = lambda off: pl.BlockSpec((HALO, tf), lambda j, i: (jnp.maximum(i * hb - 1, 0), off + j))
    main = lambda off: pl.BlockSpec((ts, tf), lambda j, i: (i, off + j))
    wsp = lambda off: pl.BlockSpec((FFN_TAPS, tf), lambda j, i: (0, off + j))
    vsp = lambda off: pl.BlockSpec((1, tf), lambda j, i: (0, off + j))
    hs = jax.ShapeDtypeStruct((cfg.T, cfg.F), F32)
    return _pcall(body, name=name, grid=(nf, cfg.T // ts),
                  in_specs=[main(0), prev(0), main(nf), prev(nf), wsp(0), wsp(nf), vsp(0), vsp(nf)],
                  out_specs=[main(0)] * 3, out_shape=[jax.ShapeDtypeStruct((cfg.T, cfg.F), BF16), hs, hs],
                  scratch_shapes=[pltpu.VMEM((HALO + CH, tf), F32)] * 2,
                  compiler_params=_params("parallel", "parallel"))(h0, h0, h0, h0, w, w, cb, cb)


def _ffn_conv_bwd(df, h0, hg, hl, w, cfg, *, name):
    KF, HALO = cfg.KF, FFN_HALO
    ts, tf = _ffn_tiles(cfg)
    tpb, nf = cfg.S // ts, cfg.F // tf
    ext = ts + HALO

    CH = FFN_CHUNK

    def body(g_ref, l_ref, hg_ref, hgn_ref, hl_ref, hln_ref, d_ref, dn_ref, wg_ref, wl_ref,
             dg_ref, dl_ref, dwg_ref, dwl_ref, dcg_ref, dcl_ref, dhg_s, dhl_s):
        i = pl.program_id(1)
        last = i % tpb == tpb - 1

        @pl.when(i == 0)
        def _():
            dwg_ref[...] = jnp.zeros_like(dwg_ref)
            dwl_ref[...] = jnp.zeros_like(dwl_ref)
            dcg_ref[...] = jnp.zeros_like(dcg_ref)
            dcl_ref[...] = jnp.zeros_like(dcl_ref)

        wg, wl = [wg_ref[k:k + 1, :] for k in range(KF)], [wl_ref[k:k + 1, :] for k in range(KF)]

        def grads(hg, hl, d):
            return d * hl * _gelu_grad(hg), d * _gelu(hg)

        for r0 in range(0, ts, CH):
            dhg_s[r0:r0 + CH, :], dhl_s[r0:r0 + CH, :] = grads(hg_ref[r0:r0 + CH, :], hl_ref[r0:r0 + CH, :], d_ref[r0:r0 + CH, :])
        dhg_s[ts:ext, :], dhl_s[ts:ext, :] = grads(hgn_ref[...], hln_ref[...], jnp.where(last, 0.0, dn_ref[...]))

        for dh_s, x_ref, wk, dx_ref, dw_ref, dc_ref in ((dhg_s, g_ref, wg, dg_ref, dwg_ref, dcg_ref),
                                                        (dhl_s, l_ref, wl, dl_ref, dwl_ref, dcl_ref)):
            dw_acc = [jnp.zeros((CH, tf), F32) for _ in range(KF)]
            for r0 in range(0, ts, CH):
                x = x_ref[r0:r0 + CH, :]
                dx = jnp.zeros((CH, tf), F32)
                for k in range(KF):
                    dhk = dh_s[r0 + KF - 1 - k:r0 + KF - 1 - k + CH, :]
                    dx = dx + wk[k] * dhk
                    dw_acc[k] = dw_acc[k] + x * dhk
                    if k == KF - 1:
                        dc_acc = dhk if r0 == 0 else dc_acc + dhk
                dx_ref[r0:r0 + CH, :] = dx.astype(BF16)
            for k in range(KF):
                dw_ref[k:k + 1, :] += jnp.sum(dw_acc[k], axis=0, keepdims=True)
            dc_ref[...] += jnp.sum(dc_acc, axis=0, keepdims=True)

    hb = ts // HALO
    nhb = cfg.T // HALO
    main = lambda off: pl.BlockSpec((ts, tf), lambda j, i: (i, off + j))
    nxt = pl.BlockSpec((HALO, tf), lambda j, i: (jnp.minimum((i + 1) * hb, nhb - 1), j))
    wsp = lambda off: pl.BlockSpec((FFN_TAPS, tf), lambda j, i: (0, off + j))
    vsp = pl.BlockSpec((1, tf), lambda j, i: (0, j))
    dxs, dws, dcs = (jax.ShapeDtypeStruct((cfg.T, cfg.F), BF16), jax.ShapeDtypeStruct((FFN_TAPS, cfg.F), F32),
                     jax.ShapeDtypeStruct((1, cfg.F), F32))
    return _pcall(body, name=name, grid=(nf, cfg.T // ts),
                  in_specs=[main(0), main(nf), main(0), nxt, main(0), nxt, main(0), nxt, wsp(0), wsp(nf)],
                  out_specs=[main(0), main(0), wsp(0), wsp(0), vsp, vsp],
                  out_shape=[dxs, dxs, dws, dws, dcs, dcs],
                  scratch_shapes=[pltpu.VMEM((ext, tf), F32)] * 2,
                  compiler_params=_params("parallel", "arbitrary"))(h0, h0, hg, hg, hl, hl, df, df, w, w)


def _ada_fwd(c_all, w, b, *, name):
    L, D, n = w.shape
    B = c_all.shape[0]

    def body(c_ref, w_ref, b_ref, o_ref):
        c = c_ref[...]
        act = (c * _sigmoid(c)).astype(BF16)
        o_ref[0] = jnp.dot(act, w_ref[0].astype(BF16), preferred_element_type=F32) + b_ref[0]

    return _pcall(body, name=name, grid=(L,),
                  in_specs=[pl.BlockSpec((B, D), lambda l: (0, 0)), pl.BlockSpec((1, D, n), lambda l: (l, 0, 0)),
                            pl.BlockSpec((1, 1, n), lambda l: (l, 0, 0))],
                  out_specs=pl.BlockSpec((1, B, n), lambda l: (l, 0, 0)), out_shape=jax.ShapeDtypeStruct((L, B, n), F32),
                  compiler_params=_params("parallel"))(c_all, w, b)


def _ada_bwd(c_all, dmod, *, name):
    L, B, n = dmod.shape
    D = c_all.shape[1]

    def body(c_ref, d_ref, o_ref):
        c = c_ref[...]
        act = (c * _sigmoid(c)).astype(BF16)
        o_ref[0] = lax.dot_general(act, d_ref[0].astype(BF16), TN, preferred_element_type=F32)

    return _pcall(body, name=name, grid=(L,),
                  in_specs=[pl.BlockSpec((B, D), lambda l: (0, 0)), pl.BlockSpec((1, B, n), lambda l: (l, 0, 0))],
                  out_specs=pl.BlockSpec((1, D, n), lambda l: (l, 0, 0)), out_shape=jax.ShapeDtypeStruct((L, D, n), F32),
                  compiler_params=_params("parallel"))(c_all, dmod)


def _slot_sum(x, *, name):
    n, R, W = x.shape
    tr = _pick(R, (256, 128, 64, 32, 16, 8))

    def body(x_ref, o_ref):
        acc = x_ref[0].astype(F32)
        for k in range(1, n):
            acc = acc + x_ref[k].astype(F32)
        o_ref[...] = acc

    return _pcall(body, name=name, grid=(R // tr,), in_specs=[pl.BlockSpec((n, tr, W), lambda i: (0, i, 0))],
                  out_specs=pl.BlockSpec((tr, W), lambda i: (i, 0)), out_shape=jax.ShapeDtypeStruct((R, W), F32),
                  compiler_params=_params("parallel"))(x)


def _adamw_math(g, w, m, v):
    r1, r2 = 1.0 / (1.0 - ADAM_B1 ** ADAM_STEP), 1.0 / (1.0 - ADAM_B2 ** ADAM_STEP)
    m2 = ADAM_B1 * m + (1.0 - ADAM_B1) * g
    v2 = ADAM_B2 * v + (1.0 - ADAM_B2) * (g * g)
    return -ADAM_LR * ((m2 * r1) / (jnp.sqrt(v2 * r2) + ADAM_EPS) + ADAM_WD * w), m2, v2


def _adamw_many(gs, ws, ms, vs, *, name):
    n = len(gs)

    def body(*refs):
        ins, outs = refs[:4 * n], refs[4 * n:]
        for i in range(n):
            d, m2, v2 = _adamw_math(*(ins[j * n + i][...] for j in range(4)))
            outs[i][...], outs[n + i][...], outs[2 * n + i][...] = d, m2, v2

    vm = pl.BlockSpec(memory_space=pltpu.VMEM)
    outs = _pcall(body, name=name, in_specs=[vm] * (4 * n), out_specs=[vm] * (3 * n),
                  out_shape=[jax.ShapeDtypeStruct(a.shape, F32) for _ in range(3) for a in ws],
                  compiler_params=pltpu.CompilerParams(vmem_limit_bytes=VMEM_LIMIT))(*gs, *ws, *ms, *vs)
    return outs[:n], outs[n:2 * n], outs[2 * n:]


def _adamw(gs, w, m, v, *, name):
    n, R, W = gs.shape
    tr = _pick(R, (256, 128, 64, 32, 16, 8))

    def body(g_ref, w_ref, m_ref, v_ref, go_ref, d_ref, mo_ref, vo_ref):
        g = g_ref[0].astype(F32)
        for k in range(1, n):
            g = g + g_ref[k].astype(F32)
        go_ref[...] = g
        d_ref[...], mo_ref[...], vo_ref[...] = _adamw_math(g, w_ref[...], m_ref[...], v_ref[...])

    blk = pl.BlockSpec((tr, W), lambda i: (i, 0))
    o = jax.ShapeDtypeStruct((R, W), F32)
    return _pcall(body, name=name, grid=(R // tr,), in_specs=[pl.BlockSpec((n, tr, W), lambda i: (0, i, 0)), blk, blk, blk],
                  out_specs=[blk] * 4, out_shape=[o] * 4, compiler_params=_params("parallel"))(gs, w, m, v)


def _peer_copies(x_ref, land_ref, send_sems, recv_sems, all_to_all):
    mx, my, mc = lax.axis_index("x"), lax.axis_index("y"), lax.axis_index("c")
    me = 4 * mx + 2 * my + mc
    copies = []
    for k in range(1, N_DEV):
        px, py, pc = mx ^ ((k >> 2) & 1), my ^ ((k >> 1) & 1), mc ^ (k & 1)
        copies.append(pltpu.make_async_remote_copy(
            src_ref=x_ref.at[4 * px + 2 * py + pc] if all_to_all else x_ref, dst_ref=land_ref.at[me],
            send_sem=send_sems.at[k - 1], recv_sem=recv_sems.at[k - 1], device_id=(px, py, pc),
            device_id_type=pl.DeviceIdType.MESH))
    return copies


def _gather_two_level(x, *, name, after=None):
    def body(x_ref, *rest):
        o_ref, send_sems, recv_sems, local_sem = rest[-4:]
        mx, my, mc = lax.axis_index("x"), lax.axis_index("y"), lax.axis_index("c")
        me, sibling = (mx, my, mc), (mx, my, 1 - mc)
        chips = [(1 - mx, my), (mx, 1 - my), (1 - mx, 1 - my)]

        def slot(px, py, pc):
            return o_ref.at[4 * px + 2 * py + pc]

        def copy(k, block, to, src=None):
            return pltpu.make_async_remote_copy(
                src_ref=slot(*block) if src is None else src, dst_ref=slot(*block), send_sem=send_sems.at[k],
                recv_sem=recv_sems.at[k], device_id=to, device_id_type=pl.DeviceIdType.MESH)

        mine = pltpu.make_async_copy(x_ref, slot(*me), local_sem)
        mine.start()
        first = [copy(0, me, sibling, src=x_ref)] + [copy(1 + j, me, (*chip, mc), src=x_ref) for j, chip in enumerate(chips)]
        for cp in first:
            cp.start()
        passed = [copy(4 + j, (*chip, mc), sibling) for j, chip in enumerate(chips)]
        for j, chip in enumerate(chips):
            copy(1 + j, (*chip, mc), me).wait_recv()
            passed[j].start()
        copy(0, sibling, me).wait_recv()
        for j, chip in enumerate(chips):
            copy(4 + j, (*chip, 1 - mc), me).wait_recv()
        for cp in first + passed:
            cp.wait_send()
        mine.wait()

    anyspec = pl.BlockSpec(memory_space=pl.ANY)
    args = [x] if after is None else [x, after]
    return _pcall(body, name=name, in_specs=[anyspec] * len(args), out_specs=anyspec,
                  out_shape=jax.ShapeDtypeStruct((N_DEV,) + tuple(x.shape), x.dtype),
                  scratch_shapes=[pltpu.SemaphoreType.DMA((N_DEV - 1,)), pltpu.SemaphoreType.DMA((N_DEV - 1,)),
                                  pltpu.SemaphoreType.DMA(())])(*args)


_HBM = pl.BlockSpec(memory_space=pltpu.HBM)
_SEM = pl.BlockSpec(memory_space=pltpu.SEMAPHORE)
_EFFECT = pltpu.SideEffectType.DATAFLOW_SIDE_EFFECTING


def _exchange_start(x, *, all_to_all, name, after=None):
    blk = x.shape[1:] if all_to_all else x.shape
    land = lax.empty((N_DEV,) + tuple(blk), x.dtype)
    has_after = after is not None

    def body(*refs):
        x_ref, land_ref = refs[0], refs[1]
        send_sems, recv_sems, _, _, token, local_sem = refs[2 + has_after:]
        me = 4 * lax.axis_index("x") + 2 * lax.axis_index("y") + lax.axis_index("c")
        mine = pltpu.make_async_copy(x_ref.at[me] if all_to_all else x_ref, land_ref.at[me], local_sem)
        mine.start()
        mine.wait()
        for cp in _peer_copies(x_ref, land_ref, send_sems, recv_sems, all_to_all):
            cp.start()
        token[...] = jnp.zeros_like(token)

    n_sem = pltpu.SemaphoreType.DMA((N_DEV - 1,))
    args = [pltpu.with_memory_space_constraint(x, pltpu.HBM), pltpu.with_memory_space_constraint(land, pltpu.HBM)]
    in_specs = [_HBM, _HBM]
    if has_after:
        args.append(after)
        in_specs.append(pl.BlockSpec(memory_space=pl.ANY))
    send_sems, recv_sems, x_thru, land_thru, token = _pcall(
        body, name=name, in_specs=in_specs,
        out_shape=(n_sem, n_sem, pltpu.HBM(x.shape, x.dtype), pltpu.HBM(land.shape, land.dtype),
                   jax.ShapeDtypeStruct((SUBLANES, LANES), F32)),
        out_specs=(_SEM, _SEM, _HBM, _HBM, pl.BlockSpec(memory_space=pltpu.VMEM)), input_output_aliases={0: 2, 1: 3},
        scratch_shapes=[pltpu.SemaphoreType.DMA(())],
        compiler_params=pltpu.CompilerParams(has_side_effects=_EFFECT))(*args)
    return (send_sems, recv_sems, x_thru, land_thru, all_to_all), token


def _exchange_wait(state, after, *, name):
    send_sems, recv_sems, x_thru, land_thru, all_to_all = state

    def body(x_ref, land_ref, send_sems, recv_sems, after_ref, x_dead, landed):
        for cp in _peer_copies(x_ref, land_ref, send_sems, recv_sems, all_to_all):
            cp.wait_send()
            cp.wait_recv()

    return _pcall(
        body, name=name, in_specs=(_HBM, _HBM, _SEM, _SEM, pl.BlockSpec(memory_space=pl.ANY)),
        out_shape=(pltpu.HBM(x_thru.shape, x_thru.dtype), pltpu.HBM(land_thru.shape, land_thru.dtype)),
        out_specs=(_HBM, _HBM), input_output_aliases={0: 0, 1: 1},
        compiler_params=pltpu.CompilerParams(has_side_effects=_EFFECT))(x_thru, land_thru, send_sems, recv_sems, after)[1]


def _exchange(x, *, all_to_all, name):
    blk = x.shape[1:] if all_to_all else x.shape

    def body(x_ref, o_ref, send_sems, recv_sems, local_sem):
        me = 4 * lax.axis_index("x") + 2 * lax.axis_index("y") + lax.axis_index("c")
        mine = pltpu.make_async_copy(x_ref.at[me] if all_to_all else x_ref, o_ref.at[me], local_sem)
        mine.start()
        copies = _peer_copies(x_ref, o_ref, send_sems, recv_sems, all_to_all)
        for cp in copies:
            cp.start()
        for cp in copies:
            cp.wait()
        mine.wait()

    anyspec = pl.BlockSpec(memory_space=pl.ANY)
    return _pcall(body, name=name, in_specs=[anyspec], out_specs=anyspec,
                  out_shape=jax.ShapeDtypeStruct((N_DEV,) + tuple(blk), x.dtype),
                  scratch_shapes=[pltpu.SemaphoreType.DMA((N_DEV - 1,)), pltpu.SemaphoreType.DMA((N_DEV - 1,)),
                                  pltpu.SemaphoreType.DMA(())])(x)


PACK_ROWS = 16


def _pack(arrs, width, dtype, lead=0):
    parts, segs, r = [], [], 0
    for a in arrs:
        lshape, shape = a.shape[:lead], a.shape[lead:]
        n = math.prod(shape)
        rows = -(-n // width)
        rows_p = -(-rows // PACK_ROWS) * PACK_ROWS
        if n == rows * width:
            blk = a.reshape(lshape + (rows, width)).astype(dtype)
            parts.append(jnp.pad(blk, [(0, 0)] * lead + [(0, rows_p - rows), (0, 0)]) if rows_p > rows else blk)
        else:
            flat = jnp.pad(a.reshape(lshape + (n,)).astype(dtype), [(0, 0)] * lead + [(0, rows_p * width - n)])
            parts.append(flat.reshape(lshape + (rows_p, width)))
        segs.append((r, n, shape))
        r += rows_p
    return jnp.concatenate(parts, axis=lead), segs


def _unpack(p, segs):
    lshape, width = p.shape[:-2], p.shape[-1]
    outs = []
    for r, n, shape in segs:
        rows = -(-n // width)
        blk = p[..., r:r + rows, :]
        if n != rows * width:
            blk = blk.reshape(lshape + (rows * width,))[..., :n]
        outs.append(blk.reshape(lshape + shape))
    return outs


def _split_cols(a, f_off, h):
    return jnp.concatenate([a[..., :f_off], a[..., f_off + h:]], axis=-1), a[..., f_off:f_off + h]


def _merge_cols(main, f, f_off):
    return jnp.concatenate([main[..., :f_off], f, main[..., f_off:]], axis=-1)


def _pad_to(a, n, axis):
    pad = [(0, 0)] * a.ndim
    pad[axis] = (0, n - a.shape[axis])
    return jnp.pad(a, pad)


def kernel(x, c, w_ada, b_ada, w_in, b_in, conv_a_w, conv_a_b, ln_conv_g, ln_conv_b, w_conv_proj, w_attn_proj, w_mix_out, b_mix_out, ln1_g, ln1_b, w_ffn_up, ffn_conv_w, ffn_conv_b, w_ffn_down, ln2_g, ln2_b, loss_target, m_w_ada, m_b_ada, m_w_in, m_b_in, m_conv_a_w, m_conv_a_b, m_ln_conv_g, m_ln_conv_b, m_w_conv_proj, m_w_attn_proj, m_w_mix_out, m_b_mix_out, m_ln1_g, m_ln1_b, m_w_ffn_up, m_ffn_conv_w, m_ffn_conv_b, m_w_ffn_down, m_ln2_g, m_ln2_b, v_w_ada, v_b_ada, v_w_in, v_b_in, v_conv_a_w, v_conv_a_b, v_ln_conv_g, v_ln_conv_b, v_w_conv_proj, v_w_attn_proj, v_w_mix_out, v_b_mix_out, v_ln1_g, v_ln1_b, v_w_ffn_up, v_ffn_conv_w, v_ffn_conv_b, v_w_ffn_down, v_ln2_g, v_ln2_b):
    L, D = w_ada.shape[0], w_ada.shape[1]
    Bl, S, _ = x.shape
    C, KW, AW = conv_a_b.shape[1], conv_a_w.shape[1], w_attn_proj.shape[1]
    F, KF, n_in_all = ffn_conv_b.shape[1] // 2, ffn_conv_w.shape[1], b_in.shape[1]
    H = n_in_all - 2 * C - 3 * AW - 2 * D
    cfg = Cfg(L=L, Bl=Bl, S=S, D=D, C=C, KW=KW, H=H, Dh=AW // H, F=F, KF=KF)
    T, NM = cfg.T, cfg.NM
    f_off = 2 * C + 3 * AW
    n_ada = w_ada.shape[2]
    me = 4 * lax.axis_index("x") + 2 * lax.axis_index("y") + lax.axis_index("c")

    def my_cols(a, n):
        return lax.dynamic_slice_in_dim(a, me * n, n, axis=a.ndim - 1)

    spack, ssegs = _pack([c, conv_a_w, ffn_conv_w], D, F32)
    c_g, caw_g, fcw_g = _unpack(_exchange(spack, all_to_all=False, name="gather_small"), ssegs)
    c_all = c_g.reshape(N_DEV * Bl, D)
    caw = _pad_to(jnp.moveaxis(caw_g, 0, 2).reshape(L, KW, C), CONV_A_TAPS, 1)
    fcw = _pad_to(jnp.moveaxis(fcw_g, 0, 2).reshape(L, KF, 2 * F), FFN_TAPS, 1)

    mod_part = _ada_fwd(c_all, w_ada, my_cols(b_ada, n_ada)[:, None, :], name="ada_fwd")
    mod_send = jnp.moveaxis(mod_part.reshape(L, N_DEV, Bl, n_ada), 1, 0).reshape(N_DEV, L * Bl, n_ada)
    mod_recv = _exchange(mod_send, all_to_all=True, name="exchange_mod")
    mod = jnp.moveaxis(mod_recv.reshape(N_DEV, L, Bl, n_ada), 0, 2).reshape(L, Bl, 6, 1, D)
    shift1, scale1, gate1, shift2, scale2, gate2 = (mod[:, :, i] for i in range(6))

    big_names = ["w_in", "w_conv_proj", "w_attn_proj", "w_mix_out", "w_ffn_up", "w_ffn_down"]
    transposed = (True, True, True, False, True, False)

    def shard_items(arrs, grp):
        return [arrs[i][l].T if transposed[i] else arrs[i][l] for l, i in grp]

    W = [dict() for _ in range(L)]

    def set_weights(landed, segs, grp):
        for (l, i), a in zip(grp, _unpack(landed, segs)):
            a = a.reshape((-1, a.shape[-1]))
            if i == 0:
                wm_t, wf_t = _split_cols(a.T, f_off, H)
                bm, bf = _split_cols(b_in[l], f_off, H)
                W[l].update(wm_t=wm_t.T, wf_t=_pad_to(wf_t.T, LANES, 0), bm=bm[None], bf=_pad_to(bf, LANES, 0)[None])
            else:
                W[l][("w_cp_t", "w_ap_t", "w_mo", "w_up_t", "w_dn")[i - 1]] = a

    big_w = (w_in, w_conv_proj, w_attn_proj, w_mix_out, w_ffn_up, w_ffn_down)
    w_groups = [[(l, i) for i in range(6)] for l in range(L)]
    pack, segs = _pack(shard_items(big_w, w_groups[0]), D, BF16)
    landed0 = _gather_two_level(pack, name="gather_weights_0", after=mod_recv)
    set_weights(landed0, segs, w_groups[0])
    w_state, token = {}, landed0
    for l in range(1, L):
        pack, segs = _pack(shard_items(big_w, w_groups[l]), D, BF16)
        state, token = _exchange_start(pack, all_to_all=False, name=f"gather_weights_start_{l}", after=token)
        w_state[l] = (state, pack, segs)

    def wait_weights(l, after):
        state, pack, segs = w_state[l]
        set_weights(_exchange_wait(state, after, name=f"gather_weights_wait_{l}"), segs, w_groups[l])

    xf = x.reshape(T, D)
    u = _ln_mod_fwd(xf, shift1[0], scale1[0], cfg, name="ln_mod_fwd")
    saved = []
    xin = xf
    for l in range(L):
        w = W[l]
        if l > 0:
            wait_weights(l, u)
        zm = _matmul(u, w["wm_t"], mode="nt", bias=w["bm"], name=f"in_proj_{l}", after=token if l == 0 else None)
        zf = _matmul(u, w["wf_t"], mode="nt", bias=w["bf"], name=f"in_proj_f_{l}")
        a3 = _conv_a_fwd(zm, caw[l], conv_a_b[l][None], ln_conv_g[l][None], ln_conv_b[l][None], cfg, name=f"conv_a_fwd_{l}")
        cum_c = _fgate_fwd(zf, cfg, name=f"fgate_fwd_{l}")
        o, o32, lse = _attn_fwd(zm, cum_c, cfg, name=f"attn_fwd_{l}")
        ya =_matmul(a3, w["w_cp_t"], mode="nt", name=f"conv_proj_{l}")
        yb = _matmul(o, w["w_ap_t"], mode="nt", name=f"attn_proj_{l}")
        mg = _merge_fwd(zm, ya, yb, cfg, name=f"merge_fwd_{l}")
        mix = _matmul(mg, w["w_mo"], mode="nn", bias=b_mix_out[l][None], name=f"mix_out_{l}")
        x1, u2 = _res_ln_fwd(xin, mix, gate1[l], ln1_g[l][None], ln1_b[l][None], cfg, name=f"res_ln1_fwd_{l}",
                             nxt=(shift2[l], scale2[l]))
        h0 = _matmul(u2, w["w_up_t"], mode="nt", name=f"ffn_up_{l}")
        fa, hg, hl = _ffn_conv_fwd(h0, fcw[l], ffn_conv_b[l][None], cfg, name=f"ffn_conv_fwd_{l}")
        ffn = _matmul(fa, w["w_dn"], mode="nn", name=f"ffn_down_{l}")
        saved.append(dict(x=xin, u=u, zm=zm, zf=zf, a3=a3, cum_c=cum_c, o=o, o32=o32, lse=lse, ya=ya, yb=yb, mg=mg, mix=mix,
                          x1=x1, u2=u2, h0=h0, hg=hg, hl=hl, fa=fa, ffn=ffn))
        if l + 1 < L:
            xin, u = _res_ln_fwd(x1, ffn, gate2[l], ln2_g[l][None], ln2_b[l][None], cfg, name=f"res_ln2_fwd_{l}",
                                 nxt=(shift1[l + 1], scale1[l + 1]))
        else:
            xin = _res_ln_fwd(x1, ffn, gate2[l], ln2_g[l][None], ln2_b[l][None], cfg, name=f"res_ln2_fwd_{l}")

    dx, loss_tiles = _loss_grad(xin, loss_target.reshape(T, D), cfg, name="loss_grad")
    loss = lax.psum(0.5 / D * jnp.sum(loss_tiles[:, 0, 0]), ("x", "y", "c"))

    gbig = {}
    g_groups = [[(l, i) for i in range(6)] for l in reversed(range(1, L))] + [[(0, 4), (0, 5)], [(0, 1), (0, 2), (0, 3)], [(0, 0)]]
    g_state = []

    def start_grads(after=None):
        grp = g_groups[len(g_state)]
        send, segs = _pack([gbig[k].reshape((N_DEV, -1, gbig[k].shape[1])) for k in grp], D, BF16, lead=1)
        state, tok = _exchange_start(send, all_to_all=True, name=f"exchange_grads_start_{len(g_state)}", after=after)
        g_state.append((state, send, segs, grp))
        return tok

    gsm = [dict() for _ in range(L)]
    dmods = [None] * L
    token = None
    for l in reversed(range(L)):
        w, s = W[l], saved[l]
        if l == L - 1:
            top = _res_ln_bwd(dx, s["x1"], s["ffn"], gate2[l], ln2_g[l][None], cfg, name=f"res_ln2_bwd_{l}")
        dres2, dffn, dg2, db2, dgate2 = top[:5]
        dfa = _matmul(dffn, w["w_dn"], mode="nt", name=f"d_ffn_act_{l}", after=token)
        gbig[l, 5] = _matmul(s["fa"], dffn, mode="tn", name=f"dw_ffn_down_{l}")
        dh0g, dh0l, dwg, dwl, dcg, dcl = _ffn_conv_bwd(dfa, s["h0"], s["hg"], s["hl"], fcw[l], cfg, name=f"ffn_conv_bwd_{l}")
        du2 = _matmul((dh0g, dh0l), w["w_up_t"], mode="nn", name=f"d_u2_{l}")
        gbig[l, 4] = _matmul((dh0g, dh0l), s["u2"], mode="tn", name=f"dw_ffn_up_{l}")
        token = start_grads() if l == 0 else None
        dres1, dmix, dg1, db1, dgate1, dbmo, dscale2, dshift2 = _res_ln_bwd(
            None, s["x"], s["mix"], gate1[l], ln1_g[l][None], cfg, name=f"res_ln1_bwd_{l}", mod=(du2, s["x1"], scale2[l], dres2))
        dmg = _matmul(dmix, w["w_mo"], mode="nt", name=f"d_merge_{l}", after=token)
        gbig[l, 3] = _matmul(s["mg"], dmix, mode="tn", name=f"dw_mix_out_{l}")
        dya, dyb, dzga, dzgb = _merge_bwd(dmg, s["zm"], s["ya"], s["yb"], cfg, name=f"merge_bwd_{l}")
        gbig[l, 1] = _matmul(dya, s["a3"], mode="tn", name=f"dw_conv_proj_{l}")
        da3 = _matmul(dya, w["w_cp_t"], mode="nn", name=f"d_a3_{l}")
        gbig[l, 2] = _matmul(dyb, s["o"], mode="tn", name=f"dw_attn_proj_{l}")
        token = start_grads() if l == 0 else None
        do = _matmul(dyb, w["w_ap_t"], mode="nn", out_dtype=BF16, name=f"d_o_{l}", after=token)
        dq, dk, dv, dcum_c = _attn_bwd(s["zm"], s["cum_c"], s["o32"], do, s["lse"], cfg, name=f"attn_bwd_{l}")
        dzf = _fgate_bwd(dcum_c, s["zf"], cfg, name=f"fgate_bwd_{l}")
        dzglu, dcaw, dcab, dlcg, dlcb = _conv_a_bwd(da3, s["zm"], caw[l], conv_a_b[l][None], ln_conv_g[l][None],
                                                    ln_conv_b[l][None], cfg, name=f"conv_a_bwd_{l}")
        dzm = jnp.concatenate([dzglu, dq, dk, dv, dzga, dzgb], axis=1)
        du1 = _matmul(dzf, w["wf_t"], mode="nn", name=f"d_u1_f_{l}")
        du1 = _matmul(dzm, w["wm_t"], mode="nn", add=du1, name=f"d_u1_{l}")
        dwm_t = _matmul(dzm, s["u"], mode="tn", name=f"dw_in_{l}")
        dwf_t = _matmul(dzf, s["u"], mode="tn", name=f"dw_in_f_{l}")
        gbig[l, 0] = _merge_cols(dwm_t.T, dwf_t[:H].T, f_off).T
        token = start_grads() if l > 0 else None
        dbm, dbf = _colsum(dzm, name=f"db_in_{l}"), _colsum(dzf, name=f"db_in_f_{l}")
        if l > 0:
            below = saved[l - 1]
            top = _res_ln_bwd(None, below["x1"], below["ffn"], gate2[l - 1], ln2_g[l - 1][None], cfg,
                              name=f"res_ln2_bwd_{l - 1}", mod=(du1, s["x"], scale1[l], dres1))
            dscale1, dshift1 = top[6], top[7]
        else:
            dx, dscale1, dshift1 = _ln_mod_bwd(du1, s["x"], scale1[l], dres1, cfg, name=f"ln_mod1_bwd_{l}")
        dmods[l] = jnp.concatenate([dshift1, dscale1, dgate1, dshift2, dscale2, dgate2], axis=1).reshape(Bl, 6 * D)
        gsm[l] = dict(b_in=_merge_cols(dbm[0], dbf[0, :H], f_off), conv_a_b=dcab[0], ln_conv_g=dlcg[0], ln_conv_b=dlcb[0],
                      b_mix_out=dbmo[0], ln1_g=dg1[0], ln1_b=db1[0], ffn_conv_b=jnp.concatenate([dcg[0], dcl[0]]),
                      ln2_g=dg2[0], ln2_b=db2[0], conv_a_w=dcaw[:KW], ffn_conv_w=jnp.concatenate([dwg[:KF], dwl[:KF]], axis=1))
    grad_x = dx.reshape(Bl, S, D)

    small_names = ["b_in", "conv_a_b", "ln_conv_g", "ln_conv_b", "b_mix_out", "ln1_g", "ln1_b", "ffn_conv_b", "ln2_g", "ln2_b",
                   "conv_a_w", "ffn_conv_w"]
    gs_list = [jnp.stack(dmods)] + [jnp.stack([gsm[l][n] for l in range(L)]) for n in small_names]
    gspack, gssegs = _pack(gs_list, D, F32)
    gs_all = _gather_two_level(gspack, name="gather_small_grads")
    start_grads(after=gs_all)
    dmod_all = jnp.moveaxis(_unpack(gs_all, gssegs)[0], 0, 1).reshape(L, N_DEV * Bl, 6 * D)
    g_small = dict(zip(small_names, _unpack(_slot_sum(gs_all, name="sum_small_grads"), gssegs)[1:]))
    g_small["conv_a_w"] = my_cols(g_small["conv_a_w"], C // N_DEV)
    g_small["ffn_conv_w"] = my_cols(g_small["ffn_conv_w"], 2 * F // N_DEV)
    g_small["w_ada"] = _ada_bwd(c_all, my_cols(dmod_all, n_ada), name="ada_bwd")
    g_small["b_ada"] = jnp.stack([_colsum(dmod_all[l], name=f"db_ada_{l}")[0] for l in range(L)])

    given = dict(w_in=(w_in, m_w_in, v_w_in), w_conv_proj=(w_conv_proj, m_w_conv_proj, v_w_conv_proj),
                 w_attn_proj=(w_attn_proj, m_w_attn_proj, v_w_attn_proj), w_mix_out=(w_mix_out, m_w_mix_out, v_w_mix_out),
                 w_ffn_up=(w_ffn_up, m_w_ffn_up, v_w_ffn_up), w_ffn_down=(w_ffn_down, m_w_ffn_down, v_w_ffn_down),
                 w_ada=(w_ada, m_w_ada, v_w_ada), b_ada=(b_ada, m_b_ada, v_b_ada), b_in=(b_in, m_b_in, v_b_in),
                 conv_a_w=(conv_a_w, m_conv_a_w, v_conv_a_w), conv_a_b=(conv_a_b, m_conv_a_b, v_conv_a_b),
                 ln_conv_g=(ln_conv_g, m_ln_conv_g, v_ln_conv_g), ln_conv_b=(ln_conv_b, m_ln_conv_b, v_ln_conv_b),
                 b_mix_out=(b_mix_out, m_b_mix_out, v_b_mix_out), ln1_g=(ln1_g, m_ln1_g, v_ln1_g), ln1_b=(ln1_b, m_ln1_b, v_ln1_b),
                 ffn_conv_w=(ffn_conv_w, m_ffn_conv_w, v_ffn_conv_w), ffn_conv_b=(ffn_conv_b, m_ffn_conv_b, v_ffn_conv_b),
                 ln2_g=(ln2_g, m_ln2_g, v_ln2_g), ln2_b=(ln2_b, m_ln2_b, v_ln2_b))
    res, kinds = {}, ("grad", "delta", "new_m", "new_v")
    loc_names = ["b_ada"] + small_names
    deltas, new_ms, new_vs = _adamw_many([g_small[n] for n in loc_names], *([given[n][j] for n in loc_names] for j in range(3)),
                                         name="adamw_small")
    for n, d, m2, v2 in zip(loc_names, deltas, new_ms, new_vs):
        res["grad", n], res["delta", n], res["new_m", n], res["new_v", n] = g_small[n], d, m2, v2
    rows_ada = (L * D * n_ada // D, D)
    outs = _adamw(g_small["w_ada"].reshape((1,) + rows_ada), *(a.reshape(rows_ada) for a in given["w_ada"]), name="adamw_w_ada")
    for kind, a in zip(kinds, outs):
        res[kind, "w_ada"] = a.reshape(w_ada.shape)

    big_parts = {}
    after = outs[0]
    for gi, (state, send, segs, grp) in enumerate(g_state):
        landed = _exchange_wait(state, after, name=f"exchange_grads_wait_{gi}")
        wmv = [_pack(shard_items([given[n][j] for n in big_names], grp), D, F32)[0] for j in range(3)]
        outs = _adamw(landed, *wmv, name=f"adamw_big_{gi}")
        for kind, packed in zip(kinds, outs):
            for (l, i), a in zip(grp, _unpack(packed, segs)):
                big_parts[kind, l, i] = a.T if transposed[i] else a
        after = outs[0]
    for kind in kinds:
        for i, n in enumerate(big_names):
            res[kind, n] = jnp.stack([big_parts[kind, l, i] for l in range(L)])

    order = ["w_ada", "b_ada", "w_in", "b_in", "conv_a_w", "conv_a_b", "ln_conv_g", "ln_conv_b", "w_conv_proj", "w_attn_proj",
             "w_mix_out", "b_mix_out", "ln1_g", "ln1_b", "w_ffn_up", "ffn_conv_w", "ffn_conv_b", "w_ffn_down", "ln2_g", "ln2_b"]
    return (loss, grad_x, *[res[k, n] for k in ("grad", "delta", "new_m", "new_v") for n in order])
```

```python
import functools
import math
from typing import NamedTuple

import jax
import jax.numpy as jnp
from jax import lax
from jax.experimental import pallas as pl
from jax.experimental.pallas import tpu as pltpu

F32, BF16 = jnp.float32, jnp.bfloat16
LN_EPS = 1e-5
ADAM_LR, ADAM_B1, ADAM_B2, ADAM_EPS, ADAM_WD, ADAM_STEP = 0.001, 0.9, 0.999, 1e-08, 0.01, 10
N_DEV = 8
LANES = 128
VMEM_LIMIT = 56 * 1024 * 1024
NEG = -1e30
NT = (((1,), (1,)), ((), ()))
TN = (((0,), (0,)), ((), ()))


class Cfg(NamedTuple):
    L: int
    Bl: int
    S: int
    D: int
    C: int
    KW: int
    H: int
    Dh: int
    F: int
    KF: int

    @property
    def T(self): return self.Bl * self.S
    @property
    def AW(self): return self.H * self.Dh
    @property
    def NM(self): return 2 * self.C + 3 * self.AW + 2 * self.D
    @property
    def q_off(self): return 2 * self.C
    @property
    def g_off(self): return 2 * self.C + 3 * self.AW
    @property
    def alpha(self): return (2.0 * self.L) ** 0.25


def _pcall(body, **kw):
    return pl.pallas_call(body, **kw)


def _params(*sem):
    return pltpu.CompilerParams(dimension_semantics=sem, vmem_limit_bytes=VMEM_LIMIT)


def _pick(n, prefs):
    for p in prefs:
        if n % p == 0:
            return p
    return n


def _sigmoid(x):
    return 1.0 / (1.0 + jnp.exp(-x))


def _ln_stats(x):
    mu = jnp.mean(x, axis=-1, keepdims=True)
    xc = x - mu
    var = jnp.mean(xc * xc, axis=-1, keepdims=True)
    rstd = lax.rsqrt(var + LN_EPS)
    return xc * rstd, rstd


def _ln_bwd(dxh, xh, rstd):
    return rstd * (dxh - jnp.mean(dxh, axis=-1, keepdims=True) - xh * jnp.mean(dxh * xh, axis=-1, keepdims=True))


def _matmul(a, b, *, mode, name, bias=None, add=None, out_dtype=F32, tm=None, tn=None, tk=None, after=None):
    parts = tuple(a) if isinstance(a, (tuple, list)) else (a,)
    P = len(parts)
    if mode == "tn":
        K, Mp = parts[0].shape
        M, Kp = P * Mp, K
    else:
        M, Kp = parts[0].shape
        K, Mp = P * Kp, M
    N = b.shape[0] if mode == "nt" else b.shape[1]
    lane_tiles = (1536, 1408, 1024, 768, 512, 256, 128)
    tm = tm or _pick(Mp, lane_tiles if mode == "tn" else (1024, 512, 256, 128, 64, 32, 16, 8))
    tn = tn or _pick(N, lane_tiles)
    tk = tk or _pick(Kp, (1024, 512, 256, 128) if mode == "tn" else lane_tiles)
    nk = K // tk
    per = Mp // tm if mode == "tn" else Kp // tk
    dn = {"nn": (((1,), (0,)), ((), ())), "nt": NT, "tn": TN}[mode]
    has_bias, has_add, has_after = bias is not None, add is not None, after is not None

    def body(*refs):
        a_refs, b_ref = refs[:P], refs[P]
        pos = P + 1
        bias_ref = refs[pos] if has_bias else None
        pos += has_bias
        add_ref = refs[pos] if has_add else None
        pos += has_add + has_after
        o_ref = refs[pos]
        acc_ref = refs[pos + 1] if nk > 1 else None
        k = pl.program_id(2)

        def finish(acc):
            if has_bias:
                acc = acc + bias_ref[...]
            if has_add:
                acc = acc + add_ref[...]
            o_ref[...] = acc.astype(out_dtype)

        def accumulate(a_ref):
            part = lax.dot_general(a_ref[...], b_ref[...], dn, preferred_element_type=F32)
            if nk == 1:
                finish(part)
            else:
                @pl.when(k == 0)
                def _():
                    acc_ref[...] = part

                @pl.when(k > 0)
                def _():
                    acc_ref[...] += part

        if P == 1:
            accumulate(a_refs[0])
        else:
            step = pl.program_id(0 if mode == "tn" else 2)
            for p in range(P):
                pl.when(step // per == p)(functools.partial(accumulate, a_refs[p]))
        if nk > 1:
            @pl.when(k == nk - 1)
            def _():
                finish(acc_ref[...])

    def a_spec(p):
        if mode == "tn":
            return pl.BlockSpec((tk, tm), lambda i, j, k: (k, jnp.clip(i - p * per, 0, per - 1)))
        return pl.BlockSpec((tm, tk), lambda i, j, k: (i, jnp.clip(k - p * per, 0, per - 1)))

    b_spec = pl.BlockSpec((tn, tk), lambda i, j, k: (j, k)) if mode == "nt" else pl.BlockSpec((tk, tn), lambda i, j, k: (k, j))
    in_specs, args = [a_spec(p) for p in range(P)] + [b_spec], list(parts) + [b]
    if has_bias:
        in_specs.append(pl.BlockSpec((1, tn), lambda i, j, k: (0, j)))
        args.append(bias)
    if has_add:
        in_specs.append(pl.BlockSpec((tm, tn), lambda i, j, k: (i, j)))
        args.append(add)
    if has_after:
        in_specs.append(pl.BlockSpec(memory_space=pl.ANY))
        args.append(after)
    return _pcall(
        body, name=name, grid=(M // tm, N // tn, nk), in_specs=in_specs,
        out_specs=pl.BlockSpec((tm, tn), lambda i, j, k: (i, j)),
        out_shape=jax.ShapeDtypeStruct((M, N), out_dtype),
        scratch_shapes=[pltpu.VMEM((tm, tn), F32)] if nk > 1 else [],
        compiler_params=_params("parallel", "parallel", "arbitrary"),
    )(*args)


def _colsum(x, *, name):
    T, N = x.shape
    tr = _pick(T, (512, 256, 128, 64, 32, 16))
    tc = _pick(N, (1536, 1024, 512, 256, 128))

    def body(x_ref, o_ref):
        @pl.when(pl.program_id(1) == 0)
        def _():
            o_ref[...] = jnp.zeros_like(o_ref)

        o_ref[...] += jnp.sum(x_ref[...].astype(F32), axis=0, keepdims=True)

    return _pcall(body, name=name, grid=(N // tc, T // tr), in_specs=[pl.BlockSpec((tr, tc), lambda j, i: (i, j))],
                  out_specs=pl.BlockSpec((1, tc), lambda j, i: (0, j)), out_shape=jax.ShapeDtypeStruct((1, N), F32),
                  compiler_params=_params("parallel", "arbitrary"))(x)


def _row_tile(cfg):
    return _pick(cfg.S, (512, 256, 128, 64, 32, 16, 8))


def _ln_mod_fwd(x, shift, scale, cfg, *, name):
    tr = _row_tile(cfg)
    tpb = cfg.S // tr

    def body(x_ref, sh_ref, sc_ref, u_ref):
        xh, _ = _ln_stats(x_ref[...])
        u_ref[...] = (xh * (1.0 + sc_ref[0]) + sh_ref[0]).astype(BF16)

    row = pl.BlockSpec((tr, cfg.D), lambda i: (i, 0))
    per_b = pl.BlockSpec((1, 1, cfg.D), lambda i: (i // tpb, 0, 0))
    return _pcall(body, name=name, grid=(cfg.T // tr,), in_specs=[row, per_b, per_b], out_specs=row,
                  out_shape=jax.ShapeDtypeStruct((cfg.T, cfg.D), BF16), compiler_params=_params("parallel"))(x, shift, scale)


def _res_ln_fwd(xin, br, gate, g, b, cfg, *, name, nxt=None):
    tr = _row_tile(cfg)
    tpb = cfg.S // tr
    alpha = cfg.alpha

    def body(*refs):
        x_ref, br_ref, gt_ref, g_ref, b_ref = refs[:5]
        r = alpha * x_ref[...] + (1.0 + gt_ref[0]) * br_ref[...]
        xh, _ = _ln_stats(r)
        xo = xh * g_ref[...] + b_ref[...]
        if nxt is None:
            refs[5][...] = xo
        else:
            sh_ref, sc_ref, xo_ref, u_ref = refs[5:]
            xo_ref[...] = xo
            uh, _ = _ln_stats(xo)
            u_ref[...] = (uh * (1.0 + sc_ref[0]) + sh_ref[0]).astype(BF16)

    row = pl.BlockSpec((tr, cfg.D), lambda i: (i, 0))
    per_b = pl.BlockSpec((1, 1, cfg.D), lambda i: (i // tpb, 0, 0))
    vec = pl.BlockSpec((1, cfg.D), lambda i: (0, 0))
    in_specs, args = [row, row, per_b, vec, vec], [xin, br, gate, g, b]
    out_specs, out_shape = row, jax.ShapeDtypeStruct((cfg.T, cfg.D), F32)
    if nxt is not None:
        in_specs += [per_b, per_b]
        args += list(nxt)
        out_specs = [row, row]
        out_shape = [out_shape, jax.ShapeDtypeStruct((cfg.T, cfg.D), BF16)]
    return _pcall(body, name=name, grid=(cfg.T // tr,), in_specs=in_specs, out_specs=out_specs, out_shape=out_shape,
                  compiler_params=_params("parallel"))(*args)


def _loss_grad(y, tgt, cfg, *, name):
    tr = _row_tile(cfg)
    nt = cfg.T // tr
    inv_d = 1.0 / cfg.D

    def body(y_ref, t_ref, dy_ref, ls_ref):
        e = y_ref[...] - t_ref[...]
        dy_ref[...] = e * inv_d
        ls_ref[...] = jnp.full((1, 1, LANES), jnp.sum(e * e), F32)

    row = pl.BlockSpec((tr, cfg.D), lambda i: (i, 0))
    return _pcall(body, name=name, grid=(nt,), in_specs=[row, row],
                  out_specs=[row, pl.BlockSpec((1, 1, LANES), lambda i: (i, 0, 0))],
                  out_shape=[jax.ShapeDtypeStruct((cfg.T, cfg.D), F32), jax.ShapeDtypeStruct((nt, 1, LANES), F32)],
                  compiler_params=_params("parallel"))(y, tgt)


def _res_ln_bwd(dy, xin, br, gate, g, cfg, *, name, mod=None):
    tr = _row_tile(cfg)
    tpb = cfg.S // tr
    alpha = cfg.alpha
    fused = mod is not None

    def body(*refs):
        if fused:
            du_ref, xa_ref, sc_ref, dres_ref, x_ref, br_ref, gt_ref, g_ref = refs[:8]
            dx_ref, dbr_ref, dg_ref, db_ref, dgt_ref, dbs_ref, dsc_ref, dsh_ref = refs[8:]
        else:
            dy_ref, x_ref, br_ref, gt_ref, g_ref, dx_ref, dbr_ref, dg_ref, db_ref, dgt_ref, dbs_ref = refs
        i = pl.program_id(0)

        @pl.when(i == 0)
        def _():
            dg_ref[...] = jnp.zeros_like(dg_ref)
            db_ref[...] = jnp.zeros_like(db_ref)
            dbs_ref[...] = jnp.zeros_like(dbs_ref)

        @pl.when(i % tpb == 0)
        def _():
            dgt_ref[...] = jnp.zeros_like(dgt_ref)
            if fused:
                dsc_ref[...] = jnp.zeros_like(dsc_ref)
                dsh_ref[...] = jnp.zeros_like(dsh_ref)

        if fused:
            du = du_ref[...]
            ah, arstd = _ln_stats(xa_ref[...])
            dsc_ref[0] += jnp.sum(du * ah, axis=0, keepdims=True)
            dsh_ref[0] += jnp.sum(du, axis=0, keepdims=True)
            dy = _ln_bwd(du * (1.0 + sc_ref[0]), ah, arstd) + dres_ref[...]
        else:
            dy = dy_ref[...]
        brv, one_gate = br_ref[...], 1.0 + gt_ref[0]
        xh, rstd = _ln_stats(alpha * x_ref[...] + one_gate * brv)
        dg_ref[...] += jnp.sum(dy * xh, axis=0, keepdims=True)
        db_ref[...] += jnp.sum(dy, axis=0, keepdims=True)
        dr = _ln_bwd(dy * g_ref[...], xh, rstd)
        dx_ref[...] = alpha * dr
        dbr = one_gate * dr
        dbr_ref[...] = dbr.astype(BF16)
        dbs_ref[...] += jnp.sum(dbr, axis=0, keepdims=True)
        dgt_ref[0] += jnp.sum(dr * brv, axis=0, keepdims=True)

    row = pl.BlockSpec((tr, cfg.D), lambda i: (i, 0))
    per_b = pl.BlockSpec((1, 1, cfg.D), lambda i: (i // tpb, 0, 0))
    vec = pl.BlockSpec((1, cfg.D), lambda i: (0, 0))
    vs = jax.ShapeDtypeStruct((1, cfg.D), F32)
    bs = jax.ShapeDtypeStruct((cfg.Bl, 1, cfg.D), F32)
    in_specs, args = [row, row, row, per_b, vec], [dy, xin, br, gate, g]
    out_specs = [row, row, vec, vec, per_b, vec]
    out_shape = [jax.ShapeDtypeStruct((cfg.T, cfg.D), F32), jax.ShapeDtypeStruct((cfg.T, cfg.D), BF16), vs, vs, bs, vs]
    if fused:
        in_specs, args = [row, row, per_b, row] + in_specs[1:], list(mod) + args[1:]
        out_specs, out_shape = out_specs + [per_b, per_b], out_shape + [bs, bs]
    return _pcall(body, name=name, grid=(cfg.T // tr,), in_specs=in_specs, out_specs=out_specs, out_shape=out_shape,
                  compiler_params=_params("arbitrary"))(*args)


def _ln_mod_bwd(du, xin, scale, dres, cfg, *, name):
    tr = _row_tile(cfg)
    tpb = cfg.S // tr

    def body(du_ref, x_ref, sc_ref, dres_ref, dx_ref, dsc_ref, dsh_ref):
        @pl.when(pl.program_id(0) % tpb == 0)
        def _():
            dsc_ref[...] = jnp.zeros_like(dsc_ref)
            dsh_ref[...] = jnp.zeros_like(dsh_ref)

        du = du_ref[...]
        xh, rstd = _ln_stats(x_ref[...])
        dsc_ref[0] += jnp.sum(du * xh, axis=0, keepdims=True)
        dsh_ref[0] += jnp.sum(du, axis=0, keepdims=True)
        dx_ref[...] = _ln_bwd(du * (1.0 + sc_ref[0]), xh, rstd) + dres_ref[...]

    row = pl.BlockSpec((tr, cfg.D), lambda i: (i, 0))
    per_b = pl.BlockSpec((1, 1, cfg.D), lambda i: (i // tpb, 0, 0))
    bs = jax.ShapeDtypeStruct((cfg.Bl, 1, cfg.D), F32)
    return _pcall(body, name=name, grid=(cfg.T // tr,), in_specs=[row, row, per_b, row], out_specs=[row, per_b, per_b],
                  out_shape=[jax.ShapeDtypeStruct((cfg.T, cfg.D), F32), bs, bs],
                  compiler_params=_params("arbitrary"))(du, xin, scale, dres)


def _merge_tiles(cfg):
    tr = _pick(cfg.T, (512, 256, 128, 64, 32, 16))
    tc = _pick(math.gcd(cfg.g_off, cfg.D), (512, 256, 128))
    return tr, tc


def _merge_fwd(zm, ya, yb, cfg, *, name):
    tr, tc = _merge_tiles(cfg)
    ga0, gb0 = cfg.g_off // tc, (cfg.g_off + cfg.D) // tc

    def body(ga_ref, gb_ref, ya_ref, yb_ref, m_ref):
        m_ref[...] = (_sigmoid(ga_ref[...]) * ya_ref[...] + _sigmoid(gb_ref[...]) * yb_ref[...]).astype(BF16)

    blk = pl.BlockSpec((tr, tc), lambda i, j: (i, j))
    return _pcall(body, name=name, grid=(cfg.T // tr, cfg.D // tc),
                  in_specs=[pl.BlockSpec((tr, tc), lambda i, j: (i, ga0 + j)), pl.BlockSpec((tr, tc), lambda i, j: (i, gb0 + j)), blk, blk],
                  out_specs=blk, out_shape=jax.ShapeDtypeStruct((cfg.T, cfg.D), BF16),
                  compiler_params=_params("parallel", "parallel"))(zm, zm, ya, yb)


def _merge_bwd(dm, zm, ya, yb, cfg, *, name):
    tr, tc = _merge_tiles(cfg)
    ga0, gb0 = cfg.g_off // tc, (cfg.g_off + cfg.D) // tc

    def body(dm_ref, ga_ref, gb_ref, ya_ref, yb_ref, dya_ref, dyb_ref, dga_ref, dgb_ref):
        dm = dm_ref[...]
        ga, gb = _sigmoid(ga_ref[...]), _sigmoid(gb_ref[...])
        dya_ref[...] = (dm * ga).astype(BF16)
        dyb_ref[...] = (dm * gb).astype(BF16)
        dga_ref[...] = (dm * ya_ref[...] * ga * (1.0 - ga)).astype(BF16)
        dgb_ref[...] = (dm * yb_ref[...] * gb * (1.0 - gb)).astype(BF16)

    blk = pl.BlockSpec((tr, tc), lambda i, j: (i, j))
    o = jax.ShapeDtypeStruct((cfg.T, cfg.D), BF16)
    return _pcall(body, name=name, grid=(cfg.T // tr, cfg.D // tc),
                  in_specs=[blk, pl.BlockSpec((tr, tc), lambda i, j: (i, ga0 + j)), pl.BlockSpec((tr, tc), lambda i, j: (i, gb0 + j)), blk, blk],
                  out_specs=[blk] * 4, out_shape=[o] * 4, compiler_params=_params("parallel", "parallel"))(dm, zm, zm, ya, yb)


CONV_A_HALO = 32
CONV_A_CHUNK = 32
CONV_A_TAPS = 32
FFN_TAPS = 8


SUBLANES = 8


def _conv_a_tile(cfg):
    assert cfg.KW - 1 <= CONV_A_HALO
    return _pick(cfg.S, (256, 128, 64, 32))


def _shift_copies(src_s, sh_s):
    rows = src_s.shape[0] - SUBLANES
    for b in range(1, SUBLANES):
        sh_s[b - 1, :, :] = src_s[b:b + rows, :]


def _rows(src_s, sh_s, start, n):
    a, b = divmod(start, SUBLANES)
    return src_s[start:start + n, :] if b == 0 else sh_s[b - 1, SUBLANES * a:SUBLANES * a + n, :]


def _conv_a_fwd(zm, w, cb, g, b, cfg, *, name):
    C, KW, HALO, CH = cfg.C, cfg.KW, CONV_A_HALO, CONV_A_CHUNK
    ts = _conv_a_tile(cfg)
    tpb = cfg.S // ts
    lead = HALO - (KW - 1)

    def body(z_ref, zp_ref, w_ref, cb_ref, g_ref, b_ref, o_ref, a0_s, a0_sh):
        first = pl.program_id(0) % tpb == 0
        prev = zp_ref[:, :C] * _sigmoid(zp_ref[:, C:])
        a0_s[0:HALO, :] = jnp.where(first, 0.0, prev)
        a0_s[HALO:HALO + ts, :] = z_ref[:, :C] * _sigmoid(z_ref[:, C:])
        _shift_copies(a0_s, a0_sh)
        for r0 in range(0, ts, CH):
            acc = jnp.zeros((CH, C), F32)
            for k in range(KW):
                acc = acc + w_ref[k:k + 1, :] * _rows(a0_s, a0_sh, r0 + lead + k, CH)
            xh, _ = _ln_stats(acc + cb_ref[...])
            a2 = xh * g_ref[...] + b_ref[...]
            o_ref[r0:r0 + CH, :] = (a2 * _sigmoid(a2)).astype(BF16)

    hb = ts // HALO
    vec = pl.BlockSpec((1, C), lambda i: (0, 0))
    return _pcall(body, name=name, grid=(cfg.T // ts,),
                  in_specs=[pl.BlockSpec((ts, 2 * C), lambda i: (i, 0)),
                            pl.BlockSpec((HALO, 2 * C), lambda i: (jnp.maximum(i * hb - 1, 0), 0)),
                            pl.BlockSpec((CONV_A_TAPS, C), lambda i: (0, 0)), vec, vec, vec],
                  out_specs=pl.BlockSpec((ts, C), lambda i: (i, 0)), out_shape=jax.ShapeDtypeStruct((cfg.T, C), BF16),
                  scratch_shapes=[pltpu.VMEM((HALO + ts, C), F32), pltpu.VMEM((SUBLANES - 1, HALO + ts - SUBLANES, C), F32)],
                  compiler_params=_params("parallel"))(zm, zm, w, cb, g, b)


def _conv_a_bwd(da3, zm, w, cb, g, b, cfg, *, name):
    C, KW, HALO, CH = cfg.C, cfg.KW, CONV_A_HALO, CONV_A_CHUNK
    ts = _conv_a_tile(cfg)
    tpb = cfg.S // ts
    nt = cfg.T // ts
    lead = HALO - (KW - 1)
    ext = ts + HALO

    def body(z_ref, zp_ref, zn_ref, d_ref, dn_ref, w_ref, cb_ref, g_ref, b_ref,
             dz_ref, dw_ref, dcb_ref, dg_ref, db_ref, a0_s, d3_s, da1_s, a0_sh, da1_sh):
        i = pl.program_id(0)
        first, last = i % tpb == 0, i % tpb == tpb - 1

        @pl.when(i == 0)
        def _():
            dw_ref[...] = jnp.zeros_like(dw_ref)
            dcb_ref[...] = jnp.zeros_like(dcb_ref)
            dg_ref[...] = jnp.zeros_like(dg_ref)
            db_ref[...] = jnp.zeros_like(db_ref)

        a0_s[0:HALO, :] = jnp.where(first, 0.0, zp_ref[:, :C] * _sigmoid(zp_ref[:, C:]))
        a0_s[HALO:HALO + ts, :] = z_ref[:, :C] * _sigmoid(z_ref[:, C:])
        a0_s[HALO + ts:HALO + ext, :] = zn_ref[:, :C] * _sigmoid(zn_ref[:, C:])
        d3_s[0:ts, :] = d_ref[...]
        d3_s[ts:ext, :] = jnp.where(last, 0.0, dn_ref[...])
        _shift_copies(a0_s, a0_sh)
        dcb, dg, db = jnp.zeros((1, C), F32), jnp.zeros((1, C), F32), jnp.zeros((1, C), F32)
        for r0 in range(0, ext, CH):
            acc = jnp.zeros((CH, C), F32)
            for k in range(KW):
                acc = acc + w_ref[k:k + 1, :] * _rows(a0_s, a0_sh, r0 + lead + k, CH)
            xh, rstd = _ln_stats(acc + cb_ref[...])
            a2 = xh * g_ref[...] + b_ref[...]
            sg = _sigmoid(a2)
            da2 = d3_s[r0:r0 + CH, :] * (sg * (1.0 + a2 * (1.0 - sg)))
            da1 = _ln_bwd(da2 * g_ref[...], xh, rstd)
            da1_s[r0:r0 + CH, :] = da1
            if r0 < ts:
                dg = dg + jnp.sum(da2 * xh, axis=0, keepdims=True)
                db = db + jnp.sum(da2, axis=0, keepdims=True)
                dcb = dcb + jnp.sum(da1, axis=0, keepdims=True)
        dg_ref[...] += dg
        db_ref[...] += db
        dcb_ref[...] += dcb
        _shift_copies(da1_s, da1_sh)
        for k in range(KW):
            dwk = jnp.zeros((CH, C), F32)
            for r0 in range(0, ts, CH):
                dwk = dwk + da1_s[r0:r0 + CH, :] * _rows(a0_s, a0_sh, r0 + lead + k, CH)
            dw_ref[k:k + 1, :] += jnp.sum(dwk, axis=0, keepdims=True)
        for r0 in range(0, ts, CH):
            da0 = jnp.zeros((CH, C), F32)
            for k in range(KW):
                da0 = da0 + w_ref[k:k + 1, :] * _rows(da1_s, da1_sh, r0 + KW - 1 - k, CH)
            val, sg = z_ref[r0:r0 + CH, :C], _sigmoid(z_ref[r0:r0 + CH, C:])
            dz_ref[r0:r0 + CH, :C] = (da0 * sg).astype(BF16)
            dz_ref[r0:r0 + CH, C:] = (da0 * val * sg * (1.0 - sg)).astype(BF16)

    hb = ts // HALO
    nhb = cfg.T // HALO
    vec = pl.BlockSpec((1, C), lambda i: (0, 0))
    vs = jax.ShapeDtypeStruct((1, C), F32)
    return _pcall(body, name=name, grid=(nt,),
                  in_specs=[pl.BlockSpec((ts, 2 * C), lambda i: (i, 0)),
                            pl.BlockSpec((HALO, 2 * C), lambda i: (jnp.maximum(i * hb - 1, 0), 0)),
                            pl.BlockSpec((HALO, 2 * C), lambda i: (jnp.minimum((i + 1) * hb, nhb - 1), 0)),
                            pl.BlockSpec((ts, C), lambda i: (i, 0)),
                            pl.BlockSpec((HALO, C), lambda i: (jnp.minimum((i + 1) * hb, nhb - 1), 0)),
                            pl.BlockSpec((CONV_A_TAPS, C), lambda i: (0, 0)), vec, vec, vec],
                  out_specs=[pl.BlockSpec((ts, 2 * C), lambda i: (i, 0)), pl.BlockSpec((CONV_A_TAPS, C), lambda i: (0, 0)), vec, vec, vec],
                  out_shape=[jax.ShapeDtypeStruct((cfg.T, 2 * C), BF16), jax.ShapeDtypeStruct((CONV_A_TAPS, C), F32), vs, vs, vs],
                  scratch_shapes=[pltpu.VMEM((HALO + ext, C), F32), pltpu.VMEM((ext, C), F32), pltpu.VMEM((ext, C), F32),
                                  pltpu.VMEM((SUBLANES - 1, HALO + ext - SUBLANES, C), F32),
                                  pltpu.VMEM((SUBLANES - 1, ext - SUBLANES, C), F32)],
                  compiler_params=_params("arbitrary"))(zm, zm, zm, da3, da3, w, cb, g, b)


def _cum_tile(cfg):
    return _pick(cfg.S, (256, 128, 64, 32, 16, 8))


def _fgate_fwd(zf, cfg, *, name):
    tc = _cum_tile(cfg)
    tpb = cfg.S // tc
    hp = _attn_tiles(cfg)[2]
    nb = cfg.H // hp

    def body(z_ref, o_ref, carry):
        @pl.when(pl.program_id(0) % tpb == 0)
        def _():
            carry[...] = jnp.zeros_like(carry)

        z = z_ref[...]
        logf = jnp.minimum(z, 0.0) - jnp.log(1.0 + jnp.exp(-jnp.abs(z)))
        tri = (lax.broadcasted_iota(jnp.int32, (tc, tc), 0) >= lax.broadcasted_iota(jnp.int32, (tc, tc), 1)).astype(F32)
        cum = jnp.dot(tri, logf, precision=lax.Precision.HIGHEST, preferred_element_type=F32) + carry[...]
        carry[...] = cum[tc - 1:tc, :]
        o_ref[0] = cum
        for b in range(1, nb):
            o_ref[b] = pltpu.roll(cum, LANES - hp * b, axis=1)

    return _pcall(body, name=name, grid=(cfg.T // tc,), in_specs=[pl.BlockSpec((tc, LANES), lambda i: (i, 0))],
                  out_specs=pl.BlockSpec((nb, tc, LANES), lambda i: (0, i, 0)),
                  out_shape=jax.ShapeDtypeStruct((nb, cfg.T, LANES), F32), scratch_shapes=[pltpu.VMEM((1, LANES), F32)],
                  compiler_params=_params("arbitrary"))(zf)


def _fgate_bwd(dcum_c, zf, cfg, *, name):
    tc = _cum_tile(cfg)
    tpb = cfg.S // tc
    nt = cfg.T // tc
    hp = _attn_tiles(cfg)[2]
    nb = cfg.H // hp

    def body(d_ref, z_ref, o_ref, carry):
        @pl.when(pl.program_id(0) % tpb == 0)
        def _():
            carry[...] = jnp.zeros_like(carry)

        d = d_ref[0]
        for b in range(1, nb):
            d = d + pltpu.roll(d_ref[b], hp * b, axis=1)
        tri = (lax.broadcasted_iota(jnp.int32, (tc, tc), 0) <= lax.broadcasted_iota(jnp.int32, (tc, tc), 1)).astype(F32)
        suf = jnp.dot(tri, d, precision=lax.Precision.HIGHEST, preferred_element_type=F32) + carry[...]
        o_ref[...] = (suf * _sigmoid(-z_ref[...])).astype(BF16)
        carry[...] = suf[0:1, :]

    blk = pl.BlockSpec((tc, LANES), lambda i: (nt - 1 - i, 0))
    return _pcall(body, name=name, grid=(nt,), in_specs=[pl.BlockSpec((nb, tc, LANES), lambda i: (0, nt - 1 - i, 0)), blk],
                  out_specs=blk, out_shape=jax.ShapeDtypeStruct((cfg.T, LANES), BF16),
                  scratch_shapes=[pltpu.VMEM((1, LANES), F32)], compiler_params=_params("arbitrary"))(dcum_c, zf)


def _attn_tiles(cfg):
    assert LANES % cfg.Dh == 0 and cfg.H % (LANES // cfg.Dh) == 0
    tk = _pick(cfg.S, (256, 128))
    tq = _pick(cfg.S, (2 * tk, tk))
    return tq, tk, LANES // cfg.Dh


BIAS_LANES = 3


def _head_lanes(hd, cfg, hp):
    li = lax.broadcasted_iota(jnp.int32, (1, LANES), 1)
    own = (li >= hd * cfg.Dh) & (li < (hd + 1) * cfg.Dh)
    return own, li, ((hd + 1) % hp) * cfg.Dh


def _q_aug(q, hd, cfg, hp):
    own, li, b0 = _head_lanes(hd, cfg, hp)
    ones = ((li >= b0) & (li < b0 + BIAS_LANES)).astype(F32)
    return jnp.where(own, q * cfg.Dh ** -0.5, ones).astype(BF16)


def _k_aug(k, ck, hd, cfg, hp):
    own, li, b0 = _head_lanes(hd, cfg, hp)
    hi = ck.astype(BF16).astype(F32)
    mid = (ck - hi).astype(BF16).astype(F32)
    lo = ck - hi - mid
    bias = jnp.where(li == b0, -hi, jnp.where(li == b0 + 1, -mid, jnp.where(li == b0 + 2, -lo, 0.0)))
    return jnp.where(own, k, bias).astype(BF16)


def _attn_fwd(zm, cum_c, cfg, *, name):
    S, Dh = cfg.S, cfg.Dh
    tq, tk, hp = _attn_tiles(cfg)
    assert hp >= 2
    nq, nb, per = S // tq, cfg.H // hp, tq // tk
    qb, kb, vb = cfg.q_off // LANES, (cfg.q_off + cfg.AW) // LANES, (cfg.q_off + 2 * cfg.AW) // LANES

    def body(q_ref, k_ref, v_ref, cc_ref, o_ref, o32_ref, lse_ref, ka_s, vt_s):
        qi = pl.program_id(2)

        @pl.when(qi == 0)
        def _():
            def prep(c, _):
                r = pl.multiple_of(c * tk, tk)
                kc = k_ref[pl.ds(r, tk), :]
                for hd in range(hp):
                    ka_s[hd, pl.ds(r, tk), :] = _k_aug(kc, cc_ref[0, pl.ds(r, tk), hd:hd + 1], hd, cfg, hp)
                vt_s[:, pl.ds(r, tk)] = v_ref[pl.ds(r, tk), :].T.astype(BF16)
                return 0

            lax.fori_loop(0, S // tk, prep, 0)

        key_i = lax.broadcasted_iota(jnp.int32, (tk, tq), 0)
        qry_i = lax.broadcasted_iota(jnp.int32, (tk, tq), 1)
        qf = q_ref[...]
        qa = [_q_aug(qf, hd, cfg, hp) for hd in range(hp)]

        def scores(j):
            r = pl.multiple_of(j * tk, tk)
            return tuple(lax.dot_general(ka_s[hd, pl.ds(r, tk), :], qa[hd], NT, preferred_element_type=F32) for hd in range(hp))

        def chunk(j, s_all, carry, diag=None):
            r = pl.multiple_of(j * tk, tk)
            new = []
            for hd in range(hp):
                m, l, acc = carry[hd]
                s = s_all[hd]
                if diag is not None:
                    s = jnp.where(key_i + diag * tk <= qry_i, s, NEG)
                m_new = jnp.maximum(m, jnp.max(s, axis=0, keepdims=True))
                a = jnp.exp(m - m_new)
                p = jnp.exp(s - m_new)
                l = a * l + jnp.sum(p, axis=0, keepdims=True)
                p_hi = p.astype(BF16)
                p_lo = (p - p_hi.astype(F32)).astype(BF16)
                vt = vt_s[hd * Dh:(hd + 1) * Dh, pl.ds(r, tk)]
                acc = a * acc + (jnp.dot(vt, p_hi, preferred_element_type=F32) + jnp.dot(vt, p_lo, preferred_element_type=F32))
                new.append((m_new, l, acc))
            return tuple(new)

        init = tuple((jnp.full((1, tq), NEG, F32), jnp.zeros((1, tq), F32), jnp.zeros((Dh, tq), F32)) for _ in range(hp))
        n_full = qi * per

        def step(j, c):
            stats, s_cur = c
            s_next = scores(j + 1)
            return chunk(j, s_cur, stats), s_next

        res, s_cur = lax.fori_loop(0, n_full, step, (init, scores(0)))
        for d in range(per):
            s_next = scores(n_full + d + 1) if d + 1 < per else None
            res = chunk(n_full + d, s_cur, res, diag=d)
            s_cur = s_next
        o = jnp.concatenate([acc / l for _, l, acc in res], axis=0).T
        o_ref[...] = o.astype(BF16)
        o32_ref[...] = o
        lse_ref[...] = jnp.zeros_like(lse_ref)
        for hd in range(hp):
            lse_ref[0, 0, hd:hd + 1, :] = res[hd][0] + jnp.log(res[hd][1])

    return _pcall(body, name=name, grid=(cfg.Bl, nb, nq),
                  in_specs=[pl.BlockSpec((tq, LANES), lambda b, h, i: (b * nq + i, qb + h)),
                            pl.BlockSpec((S, LANES), lambda b, h, i: (b, kb + h)),
                            pl.BlockSpec((S, LANES), lambda b, h, i: (b, vb + h)),
                            pl.BlockSpec((1, S, LANES), lambda b, h, i: (h, b, 0))],
                  out_specs=[pl.BlockSpec((tq, LANES), lambda b, h, i: (b * nq + i, h)),
                             pl.BlockSpec((tq, LANES), lambda b, h, i: (b * nq + i, h)),
                             pl.BlockSpec((1, 1, SUBLANES, tq), lambda b, h, i: (b, h, 0, i))],
                  out_shape=[jax.ShapeDtypeStruct((cfg.T, cfg.AW), BF16), jax.ShapeDtypeStruct((cfg.T, cfg.AW), F32),
                             jax.ShapeDtypeStruct((cfg.Bl, nb, SUBLANES, S), F32)],
                  scratch_shapes=[pltpu.VMEM((hp, S, LANES), BF16), pltpu.VMEM((LANES, S), BF16)],
                  compiler_params=_params("parallel", "parallel", "arbitrary"))(zm, zm, zm, cum_c)


def _attn_bwd(zm, cum_c, o, do, lse, cfg, *, name):
    S, Dh = cfg.S, cfg.Dh
    tq, t, hp = _attn_tiles(cfg)
    nq, nk, nb, per = S // tq, S // t, cfg.H // hp, tq // t
    qb, kb, vb = cfg.q_off // LANES, (cfg.q_off + cfg.AW) // LANES, (cfg.q_off + 2 * cfg.AW) // LANES
    scale = Dh ** -0.5

    def body(q_ref, k_ref, v_ref, cc_ref, o_ref, do_ref, lse_ref, dq_ref, dk_ref, dv_ref, dcc_ref,
             ka_s, qa_s, vz_s, kt_s, dd_s, dqt_s):
        li = lax.broadcasted_iota(jnp.int32, (1, LANES), 1)
        ri = lax.broadcasted_iota(jnp.int32, (LANES, 1), 0)
        key_i = lax.broadcasted_iota(jnp.int32, (t, tq), 0)
        qry_i = lax.broadcasted_iota(jnp.int32, (t, tq), 1)

        def prep(c, _):
            r = pl.multiple_of(c * t, t)
            kc, vc, qc = k_ref[pl.ds(r, t), :], v_ref[pl.ds(r, t), :], q_ref[pl.ds(r, t), :]
            prod_t = (do_ref[pl.ds(r, t), :].astype(F32) * o_ref[pl.ds(r, t), :].astype(F32)).T
            for hd in range(hp):
                own = _head_lanes(hd, cfg, hp)[0]
                ka_s[hd, pl.ds(r, t), :] = _k_aug(kc, cc_ref[0, pl.ds(r, t), hd:hd + 1], hd, cfg, hp)
                qa_s[hd, pl.ds(r, t), :] = _q_aug(qc, hd, cfg, hp)
                vz_s[hd, pl.ds(r, t), :] = jnp.where(own, vc, 0.0).astype(BF16)
                dd_s[hd:hd + 1, pl.ds(r, t)] = jnp.sum(prod_t[hd * Dh:(hd + 1) * Dh, :], axis=0, keepdims=True)
            kt_s[:, pl.ds(r, t)] = kc.T.astype(BF16)
            dqt_s[:, pl.ds(r, t)] = jnp.zeros((LANES, t), F32)
            return 0

        lax.fori_loop(0, nk, prep, 0)

        def kv_step(j, _):
            rk = pl.multiple_of(j * t, t)
            i0 = j // per

            def tile(i, carry, masked):
                rq = pl.multiple_of(i * tq, tq)
                dob = do_ref[pl.ds(rq, tq), :]
                new, dq_t = [], None
                for hd in range(hp):
                    dk_h, dv_h, dsum_h = carry[hd]
                    qa = qa_s[hd, pl.ds(rq, tq), :]
                    s = lax.dot_general(ka_s[hd, pl.ds(rk, t), :], qa, NT, preferred_element_type=F32)
                    p = jnp.exp(s - lse_ref[0, 0, hd:hd + 1, pl.ds(rq, tq)])
                    if masked:
                        p = jnp.where(key_i + (rk - rq) <= qry_i, p, 0.0)
                    dp = lax.dot_general(vz_s[hd, pl.ds(rk, t), :], dob, NT, preferred_element_type=F32)
                    ds = p * (dp - dd_s[hd:hd + 1, pl.ds(rq, tq)])
                    dsb = ds.astype(BF16)
                    dv_h = dv_h + jnp.dot(p.astype(BF16), dob, preferred_element_type=F32)
                    dk_h = dk_h + jnp.dot(dsb, qa, preferred_element_type=F32)
                    dq_h = jnp.dot(kt_s[:, pl.ds(rk, t)], dsb, preferred_element_type=F32)
                    dq_t = dq_h if hd == 0 else jnp.where((ri >= hd * Dh) & (ri < (hd + 1) * Dh), dq_h, dq_t)
                    for c0 in range(0, tq, LANES):
                        dsum_h = dsum_h + ds[:, c0:c0 + LANES]
                    new.append((dk_h, dv_h, dsum_h))
                dqt_s[:, pl.ds(rq, tq)] += dq_t * scale
                return tuple(new)

            zero = tuple((jnp.zeros((t, LANES), F32),) * 3 for _ in range(hp))
            res = lax.fori_loop(i0 + 1, nq, functools.partial(tile, masked=False), tile(i0, zero, True))
            dk, dv, dcc = res[0][0], res[0][1], jnp.zeros((t, LANES), F32)
            for hd in range(hp):
                own = _head_lanes(hd, cfg, hp)[0]
                if hd > 0:
                    dk, dv = jnp.where(own, res[hd][0], dk), jnp.where(own, res[hd][1], dv)
                dcc = dcc + jnp.where(li == hd, -jnp.sum(res[hd][2], axis=1, keepdims=True), 0.0)
            dk_ref[pl.ds(rk, t), :] = dk.astype(BF16)
            dv_ref[pl.ds(rk, t), :] = dv.astype(BF16)
            dcc_ref[0, pl.ds(rk, t), :] = dcc
            return 0

        lax.fori_loop(0, nk, kv_step, 0)

        def finish(c, _):
            r = pl.multiple_of(c * t, t)
            dq_ref[pl.ds(r, t), :] = dqt_s[:, pl.ds(r, t)].T.astype(BF16)
            return 0

        lax.fori_loop(0, nk, finish, 0)

    blk = pl.BlockSpec((S, LANES), lambda b, h: (b, h))
    cc = pl.BlockSpec((1, S, LANES), lambda b, h: (h, b, 0))
    os_ = jax.ShapeDtypeStruct((cfg.T, cfg.AW), BF16)
    return _pcall(body, name=name, grid=(cfg.Bl, nb),
                  in_specs=[pl.BlockSpec((S, LANES), lambda b, h: (b, qb + h)), pl.BlockSpec((S, LANES), lambda b, h: (b, kb + h)),
                            pl.BlockSpec((S, LANES), lambda b, h: (b, vb + h)), cc, blk, blk,
                            pl.BlockSpec((1, 1, SUBLANES, S), lambda b, h: (b, h, 0, 0))],
                  out_specs=[blk, blk, blk, cc],
                  out_shape=[os_, os_, os_, jax.ShapeDtypeStruct((nb, cfg.T, LANES), F32)],
                  scratch_shapes=[pltpu.VMEM((hp, S, LANES), BF16)] * 3 + [pltpu.VMEM((LANES, S), BF16),
                                  pltpu.VMEM((SUBLANES, S), F32), pltpu.VMEM((LANES, S), F32)],
                  compiler_params=_params("parallel", "parallel"))(zm, zm, zm, cum_c, o, do, lse)


FFN_HALO = 8
FFN_CHUNK = 16


def _ffn_tiles(cfg):
    assert cfg.KF - 1 <= FFN_HALO
    return _pick(cfg.S, (512, 256, 128, 64, 32, 16, 8)), _pick(cfg.F, (256, 128))


def _gelu(x):
    return 0.5 * x * (1.0 + lax.erf(x * (2.0 ** -0.5)))


def _gelu_grad(x):
    return 0.5 * (1.0 + lax.erf(x * (2.0 ** -0.5))) + x * jnp.exp(-0.5 * x * x) * ((2.0 * math.pi) ** -0.5)


def _ffn_conv_fwd(h0, w, cb, cfg, *, name):
    KF, HALO = cfg.KF, FFN_HALO
    ts, tf = _ffn_tiles(cfg)
    tpb, nf = cfg.S // ts, cfg.F // tf
    lead = HALO - (KF - 1)

    CH = FFN_CHUNK

    def body(g_ref, gp_ref, l_ref, lp_ref, wg_ref, wl_ref, cg_ref, cl_ref, o_ref, hg_ref, hl_ref, g_s, l_s):
        first = pl.program_id(1) % tpb == 0
        for s, main, prev in ((g_s, g_ref, gp_ref), (l_s, l_ref, lp_ref)):
            s[0:HALO, :] = jnp.where(first, 0.0, prev[...])
            s[HALO:HALO + CH, :] = main[0:CH, :]
        wg, wl = [wg_ref[k:k + 1, :] for k in range(KF)], [wl_ref[k:k + 1, :] for k in range(KF)]
        for r0 in range(0, ts, CH):
            hg, hl = cg_ref[...], cl_ref[...]
            for k in range(KF):
                if r0 == 0:
                    xg, xl = g_s[lead + k:lead + k + CH, :], l_s[lead + k:lead + k + CH, :]
                else:
                    a = r0 - (KF - 1) + k
                    xg, xl = g_ref[a:a + CH, :], l_ref[a:a + CH, :]
                hg, hl = hg + wg[k] * xg, hl + wl[k] * xl
            o_ref[r0:r0 + CH, :] = (_gelu(hg) * hl).astype(BF16)
            hg_ref[r0:r0 + CH, :], hl_ref[r0:r0 + CH, :] = hg, hl

    hb = ts // HALO
    prev = lambda off: pl.BlockSpec((HALO, tf), lambda j, i: (jnp.maximum(i * hb - 1, 0), off + j))
    main = lambda off: pl.BlockSpec((ts, tf), lambda j, i: (i, off + j))
    wsp = lambda off: pl.BlockSpec((FFN_TAPS, tf), lambda j, i: (0, off + j))
    vsp = lambda off: pl.BlockSpec((1, tf), lambda j, i: (0, off + j))
    hs = jax.ShapeDtypeStruct((cfg.T, cfg.F), F32)
    return _pcall(body, name=name, grid=(nf, cfg.T // ts),
                  in_specs=[main(0), prev(0), main(nf), prev(nf), wsp(0), wsp(nf), vsp(0), vsp(nf)],
                  out_specs=[main(0)] * 3, out_shape=[jax.ShapeDtypeStruct((cfg.T, cfg.F), BF16), hs, hs],
                  scratch_shapes=[pltpu.VMEM((HALO + CH, tf), F32)] * 2,
                  compiler_params=_params("parallel", "parallel"))(h0, h0, h0, h0, w, w, cb, cb)


def _ffn_conv_bwd(df, h0, hg, hl, w, cfg, *, name):
    KF, HALO = cfg.KF, FFN_HALO
    ts, tf = _ffn_tiles(cfg)
    tpb, nf = cfg.S // ts, cfg.F // tf
    ext = ts + HALO

    CH = FFN_CHUNK

    def body(g_ref, l_ref, hg_ref, hgn_ref, hl_ref, hln_ref, d_ref, dn_ref, wg_ref, wl_ref,
             dg_ref, dl_ref, dwg_ref, dwl_ref, dcg_ref, dcl_ref, dhg_s, dhl_s):
        i = pl.program_id(1)
        last = i % tpb == tpb - 1

        @pl.when(i == 0)
        def _():
            dwg_ref[...] = jnp.zeros_like(dwg_ref)
            dwl_ref[...] = jnp.zeros_like(dwl_ref)
            dcg_ref[...] = jnp.zeros_like(dcg_ref)
            dcl_ref[...] = jnp.zeros_like(dcl_ref)

        wg, wl = [wg_ref[k:k + 1, :] for k in range(KF)], [wl_ref[k:k + 1, :] for k in range(KF)]

        def grads(hg, hl, d):
            return d * hl * _gelu_grad(hg), d * _gelu(hg)

        for r0 in range(0, ts, CH):
            dhg_s[r0:r0 + CH, :], dhl_s[r0:r0 + CH, :] = grads(hg_ref[r0:r0 + CH, :], hl_ref[r0:r0 + CH, :], d_ref[r0:r0 + CH, :])
        dhg_s[ts:ext, :], dhl_s[ts:ext, :] = grads(hgn_ref[...], hln_ref[...], jnp.where(last, 0.0, dn_ref[...]))

        for dh_s, x_ref, wk, dx_ref, dw_ref, dc_ref in ((dhg_s, g_ref, wg, dg_ref, dwg_ref, dcg_ref),
                                                        (dhl_s, l_ref, wl, dl_ref, dwl_ref, dcl_ref)):
            dw_acc = [jnp.zeros((CH, tf), F32) for _ in range(KF)]
            for r0 in range(0, ts, CH):
                x = x_ref[r0:r0 + CH, :]
                dx = jnp.zeros((CH, tf), F32)
                for k in range(KF):
                    dhk = dh_s[r0 + KF - 1 - k:r0 + KF - 1 - k + CH, :]
                    dx = dx + wk[k] * dhk
                    dw_acc[k] = dw_acc[k] + x * dhk
                    if k == KF - 1:
                        dc_acc = dhk if r0 == 0 else dc_acc + dhk
                dx_ref[r0:r0 + CH, :] = dx.astype(BF16)
            for k in range(KF):
                dw_ref[k:k + 1, :] += jnp.sum(dw_acc[k], axis=0, keepdims=True)
            dc_ref[...] += jnp.sum(dc_acc, axis=0, keepdims=True)

    hb = ts // HALO
    nhb = cfg.T // HALO
    main = lambda off: pl.BlockSpec((ts, tf), lambda j, i: (i, off + j))
    nxt = pl.BlockSpec((HALO, tf), lambda j, i: (jnp.minimum((i + 1) * hb, nhb - 1), j))
    wsp = lambda off: pl.BlockSpec((FFN_TAPS, tf), lambda j, i: (0, off + j))
    vsp = pl.BlockSpec((1, tf), lambda j, i: (0, j))
    dxs, dws, dcs = (jax.ShapeDtypeStruct((cfg.T, cfg.F), BF16), jax.ShapeDtypeStruct((FFN_TAPS, cfg.F), F32),
                     jax.ShapeDtypeStruct((1, cfg.F), F32))
    return _pcall(body, name=name, grid=(nf, cfg.T // ts),
                  in_specs=[main(0), main(nf), main(0), nxt, main(0), nxt, main(0), nxt, wsp(0), wsp(nf)],
                  out_specs=[main(0), main(0), wsp(0), wsp(0), vsp, vsp],
                  out_shape=[dxs, dxs, dws, dws, dcs, dcs],
                  scratch_shapes=[pltpu.VMEM((ext, tf), F32)] * 2,
                  compiler_params=_params("parallel", "arbitrary"))(h0, h0, hg, hg, hl, hl, df, df, w, w)


def _ada_fwd(c_all, w, b, *, name):
    L, D, n = w.shape
    B = c_all.shape[0]

    def body(c_ref, w_ref, b_ref, o_ref):
        c = c_ref[...]
        act = (c * _sigmoid(c)).astype(BF16)
        o_ref[0] = jnp.dot(act, w_ref[0].astype(BF16), preferred_element_type=F32) + b_ref[0]

    return _pcall(body, name=name, grid=(L,),
                  in_specs=[pl.BlockSpec((B, D), lambda l: (0, 0)), pl.BlockSpec((1, D, n), lambda l: (l, 0, 0)),
                            pl.BlockSpec((1, 1, n), lambda l: (l, 0, 0))],
                  out_specs=pl.BlockSpec((1, B, n), lambda l: (l, 0, 0)), out_shape=jax.ShapeDtypeStruct((L, B, n), F32),
                  compiler_params=_params("parallel"))(c_all, w, b)


def _ada_bwd(c_all, dmod, *, name):
    L, B, n = dmod.shape
    D = c_all.shape[1]

    def body(c_ref, d_ref, o_ref):
        c = c_ref[...]
        act = (c * _sigmoid(c)).astype(BF16)
        o_ref[0] = lax.dot_general(act, d_ref[0].astype(BF16), TN, preferred_element_type=F32)

    return _pcall(body, name=name, grid=(L,),
                  in_specs=[pl.BlockSpec((B, D), lambda l: (0, 0)), pl.BlockSpec((1, B, n), lambda l: (l, 0, 0))],
                  out_specs=pl.BlockSpec((1, D, n), lambda l: (l, 0, 0)), out_shape=jax.ShapeDtypeStruct((L, D, n), F32),
                  compiler_params=_params("parallel"))(c_all, dmod)


def _slot_sum(x, *, name):
    n, R, W = x.shape
    tr = _pick(R, (256, 128, 64, 32, 16, 8))

    def body(x_ref, o_ref):
        acc = x_ref[0].astype(F32)
        for k in range(1, n):
            acc = acc + x_ref[k].astype(F32)
        o_ref[...] = acc

    return _pcall(body, name=name, grid=(R // tr,), in_specs=[pl.BlockSpec((n, tr, W), lambda i: (0, i, 0))],
                  out_specs=pl.BlockSpec((tr, W), lambda i: (i, 0)), out_shape=jax.ShapeDtypeStruct((R, W), F32),
                  compiler_params=_params("parallel"))(x)


def _adamw_math(g, w, m, v):
    r1, r2 = 1.0 / (1.0 - ADAM_B1 ** ADAM_STEP), 1.0 / (1.0 - ADAM_B2 ** ADAM_STEP)
    m2 = ADAM_B1 * m + (1.0 - ADAM_B1) * g
    v2 = ADAM_B2 * v + (1.0 - ADAM_B2) * (g * g)
    return -ADAM_LR * ((m2 * r1) / (jnp.sqrt(v2 * r2) + ADAM_EPS) + ADAM_WD * w), m2, v2


def _adamw_many(gs, ws, ms, vs, *, name):
    n = len(gs)

    def body(*refs):
        ins, outs = refs[:4 * n], refs[4 * n:]
        for i in range(n):
            d, m2, v2 = _adamw_math(*(ins[j * n + i][...] for j in range(4)))
            outs[i][...], outs[n + i][...], outs[2 * n + i][...] = d, m2, v2

    vm = pl.BlockSpec(memory_space=pltpu.VMEM)
    outs = _pcall(body, name=name, in_specs=[vm] * (4 * n), out_specs=[vm] * (3 * n),
                  out_shape=[jax.ShapeDtypeStruct(a.shape, F32) for _ in range(3) for a in ws],
                  compiler_params=pltpu.CompilerParams(vmem_limit_bytes=VMEM_LIMIT))(*gs, *ws, *ms, *vs)
    return outs[:n], outs[n:2 * n], outs[2 * n:]


def _adamw(gs, w, m, v, *, name):
    n, R, W = gs.shape
    tr = _pick(R, (256, 128, 64, 32, 16, 8))

    def body(g_ref, w_ref, m_ref, v_ref, go_ref, d_ref, mo_ref, vo_ref):
        g = g_ref[0].astype(F32)
        for k in range(1, n):
            g = g + g_ref[k].astype(F32)
        go_ref[...] = g
        d_ref[...], mo_ref[...], vo_ref[...] = _adamw_math(g, w_ref[...], m_ref[...], v_ref[...])

    blk = pl.BlockSpec((tr, W), lambda i: (i, 0))
    o = jax.ShapeDtypeStruct((R, W), F32)
    return _pcall(body, name=name, grid=(R // tr,), in_specs=[pl.BlockSpec((n, tr, W), lambda i: (0, i, 0)), blk, blk, blk],
                  out_specs=[blk] * 4, out_shape=[o] * 4, compiler_params=_params("parallel"))(gs, w, m, v)


ALL_PEERS = tuple(range(1, N_DEV))


def _peer_copies(x_ref, land_ref, send_sems, recv_sems, all_to_all, ks=ALL_PEERS):
    mx, my, mc = lax.axis_index("x"), lax.axis_index("y"), lax.axis_index("c")
    me = 4 * mx + 2 * my + mc
    copies = []
    for n, k in enumerate(ks):
        px, py, pc = mx ^ ((k >> 2) & 1), my ^ ((k >> 1) & 1), mc ^ (k & 1)
        copies.append(pltpu.make_async_remote_copy(
            src_ref=x_ref.at[4 * px + 2 * py + pc] if all_to_all else x_ref, dst_ref=land_ref.at[me],
            send_sem=send_sems.at[n], recv_sem=recv_sems.at[n], device_id=(px, py, pc),
            device_id_type=pl.DeviceIdType.MESH))
    return copies


def _gather_two_level(x, *, name, after=None):
    def body(x_ref, *rest):
        o_ref, send_sems, recv_sems, local_sem = rest[-4:]
        mx, my, mc = lax.axis_index("x"), lax.axis_index("y"), lax.axis_index("c")
        me, sibling = (mx, my, mc), (mx, my, 1 - mc)
        chips = [(1 - mx, my), (mx, 1 - my), (1 - mx, 1 - my)]

        def slot(px, py, pc):
            return o_ref.at[4 * px + 2 * py + pc]

        def copy(k, block, to, src=None):
            return pltpu.make_async_remote_copy(
                src_ref=slot(*block) if src is None else src, dst_ref=slot(*block), send_sem=send_sems.at[k],
                recv_sem=recv_sems.at[k], device_id=to, device_id_type=pl.DeviceIdType.MESH)

        mine = pltpu.make_async_copy(x_ref, slot(*me), local_sem)
        mine.start()
        first = [copy(0, me, sibling, src=x_ref)] + [copy(1 + j, me, (*chip, mc), src=x_ref) for j, chip in enumerate(chips)]
        for cp in first:
            cp.start()
        passed = [copy(4 + j, (*chip, mc), sibling) for j, chip in enumerate(chips)]
        for j, chip in enumerate(chips):
            copy(1 + j, (*chip, mc), me).wait_recv()
            passed[j].start()
        copy(0, sibling, me).wait_recv()
        for j, chip in enumerate(chips):
            copy(4 + j, (*chip, 1 - mc), me).wait_recv()
        for cp in first + passed:
            cp.wait_send()
        mine.wait()

    anyspec = pl.BlockSpec(memory_space=pl.ANY)
    args = [x] if after is None else [x, after]
    return _pcall(body, name=name, in_specs=[anyspec] * len(args), out_specs=anyspec,
                  out_shape=jax.ShapeDtypeStruct((N_DEV,) + tuple(x.shape), x.dtype),
                  scratch_shapes=[pltpu.SemaphoreType.DMA((N_DEV - 1,)), pltpu.SemaphoreType.DMA((N_DEV - 1,)),
                                  pltpu.SemaphoreType.DMA(())])(*args)


_HBM = pl.BlockSpec(memory_space=pltpu.HBM)
_SEM = pl.BlockSpec(memory_space=pltpu.SEMAPHORE)
_EFFECT = pltpu.SideEffectType.DATAFLOW_SIDE_EFFECTING


def _exchange_start(x, *, all_to_all, name, after=None, ks=ALL_PEERS, prev=None):
    blk = x.shape[1:] if all_to_all else x.shape
    first = prev is None
    if first:
        land, parts = lax.empty((N_DEV,) + tuple(blk), x.dtype), ()
    else:
        x, land, _, parts = prev
    has_after = after is not None

    def body(*refs):
        x_ref, land_ref = refs[0], refs[1]
        send_sems, recv_sems, _, _, token, local_sem = refs[2 + has_after:]
        if first:
            me = 4 * lax.axis_index("x") + 2 * lax.axis_index("y") + lax.axis_index("c")
            mine = pltpu.make_async_copy(x_ref.at[me] if all_to_all else x_ref, land_ref.at[me], local_sem)
            mine.start()
            mine.wait()
        for cp in _peer_copies(x_ref, land_ref, send_sems, recv_sems, all_to_all, ks):
            cp.start()
        token[...] = jnp.zeros_like(token)

    n_sem = pltpu.SemaphoreType.DMA((len(ks),))
    args = [pltpu.with_memory_space_constraint(x, pltpu.HBM), pltpu.with_memory_space_constraint(land, pltpu.HBM)]
    in_specs = [_HBM, _HBM]
    if has_after:
        args.append(after)
        in_specs.append(pl.BlockSpec(memory_space=pl.ANY))
    send_sems, recv_sems, x_thru, land_thru, token = _pcall(
        body, name=name, in_specs=in_specs,
        out_shape=(n_sem, n_sem, pltpu.HBM(x.shape, x.dtype), pltpu.HBM(land.shape, land.dtype),
                   jax.ShapeDtypeStruct((SUBLANES, LANES), F32)),
        out_specs=(_SEM, _SEM, _HBM, _HBM, pl.BlockSpec(memory_space=pltpu.VMEM)), input_output_aliases={0: 2, 1: 3},
        scratch_shapes=[pltpu.SemaphoreType.DMA(())],
        compiler_params=pltpu.CompilerParams(has_side_effects=_EFFECT))(*args)
    return (x_thru, land_thru, all_to_all, parts + ((send_sems, recv_sems, tuple(ks)),)), token


def _exchange_wait(state, after, *, name):
    x_thru, land_thru, all_to_all, parts = state
    n = len(parts)

    def body(x_ref, land_ref, *rest):
        for p in range(n):
            for cp in _peer_copies(x_ref, land_ref, rest[2 * p], rest[2 * p + 1], all_to_all, parts[p][2]):
                cp.wait_send()
                cp.wait_recv()

    sems = [s for part in parts for s in part[:2]]
    return _pcall(
        body, name=name, in_specs=(_HBM, _HBM) + (_SEM,) * (2 * n) + (pl.BlockSpec(memory_space=pl.ANY),),
        out_shape=(pltpu.HBM(x_thru.shape, x_thru.dtype), pltpu.HBM(land_thru.shape, land_thru.dtype)),
        out_specs=(_HBM, _HBM), input_output_aliases={0: 0, 1: 1},
        compiler_params=pltpu.CompilerParams(has_side_effects=_EFFECT))(x_thru, land_thru, *sems, after)[1]


def _exchange(x, *, all_to_all, name):
    blk = x.shape[1:] if all_to_all else x.shape

    def body(x_ref, o_ref, send_sems, recv_sems, local_sem):
        me = 4 * lax.axis_index("x") + 2 * lax.axis_index("y") + lax.axis_index("c")
        mine = pltpu.make_async_copy(x_ref.at[me] if all_to_all else x_ref, o_ref.at[me], local_sem)
        mine.start()
        copies = _peer_copies(x_ref, o_ref, send_sems, recv_sems, all_to_all)
        for cp in copies:
            cp.start()
        for cp in copies:
            cp.wait()
        mine.wait()

    anyspec = pl.BlockSpec(memory_space=pl.ANY)
    return _pcall(body, name=name, in_specs=[anyspec], out_specs=anyspec,
                  out_shape=jax.ShapeDtypeStruct((N_DEV,) + tuple(blk), x.dtype),
                  scratch_shapes=[pltpu.SemaphoreType.DMA((N_DEV - 1,)), pltpu.SemaphoreType.DMA((N_DEV - 1,)),
                                  pltpu.SemaphoreType.DMA(())])(x)


PACK_ROWS = 16


def _pack(arrs, width, dtype, lead=0):
    parts, segs, r = [], [], 0
    for a in arrs:
        lshape, shape = a.shape[:lead], a.shape[lead:]
        n = math.prod(shape)
        rows = -(-n // width)
        rows_p = -(-rows // PACK_ROWS) * PACK_ROWS
        if n == rows * width:
            blk = a.reshape(lshape + (rows, width)).astype(dtype)
            parts.append(jnp.pad(blk, [(0, 0)] * lead + [(0, rows_p - rows), (0, 0)]) if rows_p > rows else blk)
        else:
            flat = jnp.pad(a.reshape(lshape + (n,)).astype(dtype), [(0, 0)] * lead + [(0, rows_p * width - n)])
            parts.append(flat.reshape(lshape + (rows_p, width)))
        segs.append((r, n, shape))
        r += rows_p
    return jnp.concatenate(parts, axis=lead), segs


def _unpack(p, segs):
    lshape, width = p.shape[:-2], p.shape[-1]
    outs = []
    for r, n, shape in segs:
        rows = -(-n // width)
        blk = p[..., r:r + rows, :]
        if n != rows * width:
            blk = blk.reshape(lshape + (rows * width,))[..., :n]
        outs.append(blk.reshape(lshape + shape))
    return outs


def _split_cols(a, f_off, h):
    return jnp.concatenate([a[..., :f_off], a[..., f_off + h:]], axis=-1), a[..., f_off:f_off + h]


def _merge_cols(main, f, f_off):
    return jnp.concatenate([main[..., :f_off], f, main[..., f_off:]], axis=-1)


def _pad_to(a, n, axis):
    pad = [(0, 0)] * a.ndim
    pad[axis] = (0, n - a.shape[axis])
    return jnp.pad(a, pad)


def kernel(x, c, w_ada, b_ada, w_in, b_in, conv_a_w, conv_a_b, ln_conv_g, ln_conv_b, w_conv_proj, w_attn_proj, w_mix_out, b_mix_out, ln1_g, ln1_b, w_ffn_up, ffn_conv_w, ffn_conv_b, w_ffn_down, ln2_g, ln2_b, loss_target, m_w_ada, m_b_ada, m_w_in, m_b_in, m_conv_a_w, m_conv_a_b, m_ln_conv_g, m_ln_conv_b, m_w_conv_proj, m_w_attn_proj, m_w_mix_out, m_b_mix_out, m_ln1_g, m_ln1_b, m_w_ffn_up, m_ffn_conv_w, m_ffn_conv_b, m_w_ffn_down, m_ln2_g, m_ln2_b, v_w_ada, v_b_ada, v_w_in, v_b_in, v_conv_a_w, v_conv_a_b, v_ln_conv_g, v_ln_conv_b, v_w_conv_proj, v_w_attn_proj, v_w_mix_out, v_b_mix_out, v_ln1_g, v_ln1_b, v_w_ffn_up, v_ffn_conv_w, v_ffn_conv_b, v_w_ffn_down, v_ln2_g, v_ln2_b):
    L, D = w_ada.shape[0], w_ada.shape[1]
    Bl, S, _ = x.shape
    C, KW, AW = conv_a_b.shape[1], conv_a_w.shape[1], w_attn_proj.shape[1]
    F, KF, n_in_all = ffn_conv_b.shape[1] // 2, ffn_conv_w.shape[1], b_in.shape[1]
    H = n_in_all - 2 * C - 3 * AW - 2 * D
    cfg = Cfg(L=L, Bl=Bl, S=S, D=D, C=C, KW=KW, H=H, Dh=AW // H, F=F, KF=KF)
    T, NM = cfg.T, cfg.NM
    f_off = 2 * C + 3 * AW
    n_ada = w_ada.shape[2]
    me = 4 * lax.axis_index("x") + 2 * lax.axis_index("y") + lax.axis_index("c")

    def my_cols(a, n):
        return lax.dynamic_slice_in_dim(a, me * n, n, axis=a.ndim - 1)

    spack, ssegs = _pack([c, conv_a_w, ffn_conv_w], D, F32)
    c_g, caw_g, fcw_g = _unpack(_exchange(spack, all_to_all=False, name="gather_small"), ssegs)
    c_all = c_g.reshape(N_DEV * Bl, D)
    caw = _pad_to(jnp.moveaxis(caw_g, 0, 2).reshape(L, KW, C), CONV_A_TAPS, 1)
    fcw = _pad_to(jnp.moveaxis(fcw_g, 0, 2).reshape(L, KF, 2 * F), FFN_TAPS, 1)

    mod_part = _ada_fwd(c_all, w_ada, my_cols(b_ada, n_ada)[:, None, :], name="ada_fwd")
    mod_send = jnp.moveaxis(mod_part.reshape(L, N_DEV, Bl, n_ada), 1, 0).reshape(N_DEV, L * Bl, n_ada)
    mod_recv = _exchange(mod_send, all_to_all=True, name="exchange_mod")
    mod = jnp.moveaxis(mod_recv.reshape(N_DEV, L, Bl, n_ada), 0, 2).reshape(L, Bl, 6, 1, D)
    shift1, scale1, gate1, shift2, scale2, gate2 = (mod[:, :, i] for i in range(6))

    big_names = ["w_in", "w_conv_proj", "w_attn_proj", "w_mix_out", "w_ffn_up", "w_ffn_down"]
    transposed = (True, True, True, False, True, False)

    def shard_items(arrs, grp):
        return [arrs[i][l].T if transposed[i] else arrs[i][l] for l, i in grp]

    W = [dict() for _ in range(L)]

    def set_weights(landed, segs, grp):
        for (l, i), a in zip(grp, _unpack(landed, segs)):
            a = a.reshape((-1, a.shape[-1]))
            if i == 0:
                wm_t, wf_t = _split_cols(a.T, f_off, H)
                bm, bf = _split_cols(b_in[l], f_off, H)
                W[l].update(wm_t=wm_t.T, wf_t=_pad_to(wf_t.T, LANES, 0), bm=bm[None], bf=_pad_to(bf, LANES, 0)[None])
            else:
                W[l][("w_cp_t", "w_ap_t", "w_mo", "w_up_t", "w_dn")[i - 1]] = a

    big_w = (w_in, w_conv_proj, w_attn_proj, w_mix_out, w_ffn_up, w_ffn_down)
    w_groups = [[(l, i) for i in range(6)] for l in range(L)]
    pack, segs = _pack(shard_items(big_w, w_groups[0]), D, BF16)
    landed0 = _gather_two_level(pack, name="gather_weights_0", after=mod_recv)
    set_weights(landed0, segs, w_groups[0])
    w_state, token = {}, landed0
    for l in range(1, L):
        pack, segs = _pack(shard_items(big_w, w_groups[l]), D, BF16)
        state, token = _exchange_start(pack, all_to_all=False, name=f"gather_weights_start_{l}", after=token, ks=ALL_PEERS[:3])
        w_state[l] = [state, pack, segs]

    def wait_weights(l, after):
        state, _, segs = w_state[l]
        set_weights(_exchange_wait(state, after, name=f"gather_weights_wait_{l}"), segs, w_groups[l])

    xf = x.reshape(T, D)
    u = _ln_mod_fwd(xf, shift1[0], scale1[0], cfg, name="ln_mod_fwd")
    saved = []
    xin = xf
    for l in range(L):
        w = W[l]
        if l > 0:
            wait_weights(l, u)
        zm = _matmul(u, w["wm_t"], mode="nt", bias=w["bm"], name=f"in_proj_{l}", after=token if l == 0 else None)
        token = None
        if l == 0:
            for lw in range(1, L):
                w_state[lw][0], token = _exchange_start(w_state[lw][1], all_to_all=False, name=f"gather_weights_start_b_{lw}",
                                                        after=zm if token is None else token, ks=ALL_PEERS[3:], prev=w_state[lw][0])
        zf = _matmul(u, w["wf_t"], mode="nt", bias=w["bf"], name=f"in_proj_f_{l}", after=token)
        a3 = _conv_a_fwd(zm, caw[l], conv_a_b[l][None], ln_conv_g[l][None], ln_conv_b[l][None], cfg, name=f"conv_a_fwd_{l}")
        cum_c = _fgate_fwd(zf, cfg, name=f"fgate_fwd_{l}")
        o, o32, lse = _attn_fwd(zm, cum_c, cfg, name=f"attn_fwd_{l}")
        ya =_matmul(a3, w["w_cp_t"], mode="nt", name=f"conv_proj_{l}")
        yb = _matmul(o, w["w_ap_t"], mode="nt", name=f"attn_proj_{l}")
        mg = _merge_fwd(zm, ya, yb, cfg, name=f"merge_fwd_{l}")
        mix = _matmul(mg, w["w_mo"], mode="nn", bias=b_mix_out[l][None], name=f"mix_out_{l}")
        x1, u2 = _res_ln_fwd(xin, mix, gate1[l], ln1_g[l][None], ln1_b[l][None], cfg, name=f"res_ln1_fwd_{l}",
                             nxt=(shift2[l], scale2[l]))
        h0 = _matmul(u2, w["w_up_t"], mode="nt", name=f"ffn_up_{l}")
        fa, hg, hl = _ffn_conv_fwd(h0, fcw[l], ffn_conv_b[l][None], cfg, name=f"ffn_conv_fwd_{l}")
        ffn = _matmul(fa, w["w_dn"], mode="nn", name=f"ffn_down_{l}")
        saved.append(dict(x=xin, u=u, zm=zm, zf=zf, a3=a3, cum_c=cum_c, o=o, o32=o32, lse=lse, ya=ya, yb=yb, mg=mg, mix=mix,
                          x1=x1, u2=u2, h0=h0, hg=hg, hl=hl, fa=fa, ffn=ffn))
        if l + 1 < L:
            xin, u = _res_ln_fwd(x1, ffn, gate2[l], ln2_g[l][None], ln2_b[l][None], cfg, name=f"res_ln2_fwd_{l}",
                                 nxt=(shift1[l + 1], scale1[l + 1]))
        else:
            xin = _res_ln_fwd(x1, ffn, gate2[l], ln2_g[l][None], ln2_b[l][None], cfg, name=f"res_ln2_fwd_{l}")

    dx, loss_tiles = _loss_grad(xin, loss_target.reshape(T, D), cfg, name="loss_grad")
    loss = lax.psum(0.5 / D * jnp.sum(loss_tiles[:, 0, 0]), ("x", "y", "c"))

    gbig = {}
    g_groups = [[(l, i) for i in range(6)] for l in reversed(range(1, L))] + [[(0, 4), (0, 5)], [(0, 1), (0, 2), (0, 3)], [(0, 0)]]
    g_state = []

    def start_grads(after=None):
        grp = g_groups[len(g_state)]
        send, segs = _pack([gbig[k].reshape((N_DEV, -1, gbig[k].shape[1])) for k in grp], D, BF16, lead=1)
        state, tok = _exchange_start(send, all_to_all=True, name=f"exchange_grads_start_{len(g_state)}", after=after)
        g_state.append((state, send, segs, grp))
        return tok

    gsm = [dict() for _ in range(L)]
    dmods = [None] * L
    token = None
    for l in reversed(range(L)):
        w, s = W[l], saved[l]
        if l == L - 1:
            top = _res_ln_bwd(dx, s["x1"], s["ffn"], gate2[l], ln2_g[l][None], cfg, name=f"res_ln2_bwd_{l}")
        dres2, dffn, dg2, db2, dgate2 = top[:5]
        dfa = _matmul(dffn, w["w_dn"], mode="nt", name=f"d_ffn_act_{l}", after=token)
        gbig[l, 5] = _matmul(s["fa"], dffn, mode="tn", name=f"dw_ffn_down_{l}")
        dh0g, dh0l, dwg, dwl, dcg, dcl = _ffn_conv_bwd(dfa, s["h0"], s["hg"], s["hl"], fcw[l], cfg, name=f"ffn_conv_bwd_{l}")
        du2 = _matmul((dh0g, dh0l), w["w_up_t"], mode="nn", name=f"d_u2_{l}")
        gbig[l, 4] = _matmul((dh0g, dh0l), s["u2"], mode="tn", name=f"dw_ffn_up_{l}")
        token = start_grads() if l == 0 else None
        dres1, dmix, dg1, db1, dgate1, dbmo, dscale2, dshift2 = _res_ln_bwd(
            None, s["x"], s["mix"], gate1[l], ln1_g[l][None], cfg, name=f"res_ln1_bwd_{l}", mod=(du2, s["x1"], scale2[l], dres2))
        dmg = _matmul(dmix, w["w_mo"], mode="nt", name=f"d_merge_{l}", after=token)
        gbig[l, 3] = _matmul(s["mg"], dmix, mode="tn", name=f"dw_mix_out_{l}")
        dya, dyb, dzga, dzgb = _merge_bwd(dmg, s["zm"], s["ya"], s["yb"], cfg, name=f"merge_bwd_{l}")
        gbig[l, 1] = _matmul(dya, s["a3"], mode="tn", name=f"dw_conv_proj_{l}")
        da3 = _matmul(dya, w["w_cp_t"], mode="nn", name=f"d_a3_{l}")
        gbig[l, 2] = _matmul(dyb, s["o"], mode="tn", name=f"dw_attn_proj_{l}")
        token = start_grads() if l == 0 else None
        do = _matmul(dyb, w["w_ap_t"], mode="nn", out_dtype=BF16, name=f"d_o_{l}", after=token)
        dq, dk, dv, dcum_c = _attn_bwd(s["zm"], s["cum_c"], s["o32"], do, s["lse"], cfg, name=f"attn_bwd_{l}")
        dzf = _fgate_bwd(dcum_c, s["zf"], cfg, name=f"fgate_bwd_{l}")
        dzglu, dcaw, dcab, dlcg, dlcb = _conv_a_bwd(da3, s["zm"], caw[l], conv_a_b[l][None], ln_conv_g[l][None],
                                                    ln_conv_b[l][None], cfg, name=f"conv_a_bwd_{l}")
        dzm = jnp.concatenate([dzglu, dq, dk, dv, dzga, dzgb], axis=1)
        du1 = _matmul(dzf, w["wf_t"], mode="nn", name=f"d_u1_f_{l}")
        du1 = _matmul(dzm, w["wm_t"], mode="nn", add=du1, name=f"d_u1_{l}")
        dwm_t = _matmul(dzm, s["u"], mode="tn", name=f"dw_in_{l}")
        dwf_t = _matmul(dzf, s["u"], mode="tn", name=f"dw_in_f_{l}")
        gbig[l, 0] = _merge_cols(dwm_t.T, dwf_t[:H].T, f_off).T
        token = start_grads() if l > 0 else None
        dbm, dbf = _colsum(dzm, name=f"db_in_{l}"), _colsum(dzf, name=f"db_in_f_{l}")
        if l > 0:
            below = saved[l - 1]
            top = _res_ln_bwd(None, below["x1"], below["ffn"], gate2[l - 1], ln2_g[l - 1][None], cfg,
                              name=f"res_ln2_bwd_{l - 1}", mod=(du1, s["x"], scale1[l], dres1))
            dscale1, dshift1 = top[6], top[7]
        else:
            dx, dscale1, dshift1 = _ln_mod_bwd(du1, s["x"], scale1[l], dres1, cfg, name=f"ln_mod1_bwd_{l}")
        dmods[l] = jnp.concatenate([dshift1, dscale1, dgate1, dshift2, dscale2, dgate2], axis=1).reshape(Bl, 6 * D)
        gsm[l] = dict(b_in=_merge_cols(dbm[0], dbf[0, :H], f_off), conv_a_b=dcab[0], ln_conv_g=dlcg[0], ln_conv_b=dlcb[0],
                      b_mix_out=dbmo[0], ln1_g=dg1[0], ln1_b=db1[0], ffn_conv_b=jnp.concatenate([dcg[0], dcl[0]]),
                      ln2_g=dg2[0], ln2_b=db2[0], conv_a_w=dcaw[:KW], ffn_conv_w=jnp.concatenate([dwg[:KF], dwl[:KF]], axis=1))
    grad_x = dx.reshape(Bl, S, D)

    small_names = ["b_in", "conv_a_b", "ln_conv_g", "ln_conv_b", "b_mix_out", "ln1_g", "ln1_b", "ffn_conv_b", "ln2_g", "ln2_b",
                   "conv_a_w", "ffn_conv_w"]
    gs_list = [jnp.stack(dmods)] + [jnp.stack([gsm[l][n] for l in range(L)]) for n in small_names]
    gspack, gssegs = _pack(gs_list, D, F32)
    gs_all = _gather_two_level(gspack, name="gather_small_grads")
    start_grads(after=gs_all)
    dmod_all = jnp.moveaxis(_unpack(gs_all, gssegs)[0], 0, 1).reshape(L, N_DEV * Bl, 6 * D)
    g_small = dict(zip(small_names, _unpack(_slot_sum(gs_all, name="sum_small_grads"), gssegs)[1:]))
    g_small["conv_a_w"] = my_cols(g_small["conv_a_w"], C // N_DEV)
    g_small["ffn_conv_w"] = my_cols(g_small["ffn_conv_w"], 2 * F // N_DEV)
    g_small["w_ada"] = _ada_bwd(c_all, my_cols(dmod_all, n_ada), name="ada_bwd")
    g_small["b_ada"] = jnp.stack([_colsum(dmod_all[l], name=f"db_ada_{l}")[0] for l in range(L)])

    given = dict(w_in=(w_in, m_w_in, v_w_in), w_conv_proj=(w_conv_proj, m_w_conv_proj, v_w_conv_proj),
                 w_attn_proj=(w_attn_proj, m_w_attn_proj, v_w_attn_proj), w_mix_out=(w_mix_out, m_w_mix_out, v_w_mix_out),
                 w_ffn_up=(w_ffn_up, m_w_ffn_up, v_w_ffn_up), w_ffn_down=(w_ffn_down, m_w_ffn_down, v_w_ffn_down),
                 w_ada=(w_ada, m_w_ada, v_w_ada), b_ada=(b_ada, m_b_ada, v_b_ada), b_in=(b_in, m_b_in, v_b_in),
                 conv_a_w=(conv_a_w, m_conv_a_w, v_conv_a_w), conv_a_b=(conv_a_b, m_conv_a_b, v_conv_a_b),
                 ln_conv_g=(ln_conv_g, m_ln_conv_g, v_ln_conv_g), ln_conv_b=(ln_conv_b, m_ln_conv_b, v_ln_conv_b),
                 b_mix_out=(b_mix_out, m_b_mix_out, v_b_mix_out), ln1_g=(ln1_g, m_ln1_g, v_ln1_g), ln1_b=(ln1_b, m_ln1_b, v_ln1_b),
                 ffn_conv_w=(ffn_conv_w, m_ffn_conv_w, v_ffn_conv_w), ffn_conv_b=(ffn_conv_b, m_ffn_conv_b, v_ffn_conv_b),
                 ln2_g=(ln2_g, m_ln2_g, v_ln2_g), ln2_b=(ln2_b, m_ln2_b, v_ln2_b))
    res, kinds = {}, ("grad", "delta", "new_m", "new_v")
    loc_names = ["b_ada"] + small_names
    deltas, new_ms, new_vs = _adamw_many([g_small[n] for n in loc_names], *([given[n][j] for n in loc_names] for j in range(3)),
                                         name="adamw_small")
    for n, d, m2, v2 in zip(loc_names, deltas, new_ms, new_vs):
        res["grad", n], res["delta", n], res["new_m", n], res["new_v", n] = g_small[n], d, m2, v2
    rows_ada = (L * D * n_ada // D, D)
    outs = _adamw(g_small["w_ada"].reshape((1,) + rows_ada), *(a.reshape(rows_ada) for a in given["w_ada"]), name="adamw_w_ada")
    for kind, a in zip(kinds, outs):
        res[kind, "w_ada"] = a.reshape(w_ada.shape)

    big_parts = {}
    after = outs[0]
    for gi, (state, send, segs, grp) in enumerate(g_state):
        landed = _exchange_wait(state, after, name=f"exchange_grads_wait_{gi}")
        wmv = [_pack(shard_items([given[n][j] for n in big_names], grp), D, F32)[0] for j in range(3)]
        outs = _adamw(landed, *wmv, name=f"adamw_big_{gi}")
        for kind, packed in zip(kinds, outs):
            for (l, i), a in zip(grp, _unpack(packed, segs)):
                big_parts[kind, l, i] = a.T if transposed[i] else a
        after = outs[0]
    for kind in kinds:
        for i, n in enumerate(big_names):
            res[kind, n] = jnp.stack([big_parts[kind, l, i] for l in range(L)])

    order = ["w_ada", "b_ada", "w_in", "b_in", "conv_a_w", "conv_a_b", "ln_conv_g", "ln_conv_b", "w_conv_proj", "w_attn_proj",
             "w_mix_out", "b_mix_out", "ln1_g", "ln1_b", "w_ffn_up", "ffn_conv_w", "ffn_conv_b", "w_ffn_down", "ln2_g", "ln2_b"]
    return (loss, grad_x, *[res[k, n] for k in ("grad", "delta", "new_m", "new_v") for n in order])
```

```python
import functools
import math
from typing import NamedTuple

import jax
import jax.numpy as jnp
from jax import lax
from jax.experimental import pallas as pl
from jax.experimental.pallas import tpu as pltpu

F32, BF16 = jnp.float32, jnp.bfloat16
LN_EPS = 1e-5
ADAM_LR, ADAM_B1, ADAM_B2, ADAM_EPS, ADAM_WD, ADAM_STEP = 0.001, 0.9, 0.999, 1e-08, 0.01, 10
N_DEV = 8
LANES = 128
VMEM_LIMIT = 56 * 1024 * 1024
NEG = -1e30
NT = (((1,), (1,)), ((), ()))
TN = (((0,), (0,)), ((), ()))


class Cfg(NamedTuple):
    L: int
    Bl: int
    S: int
    D: int
    C: int
    KW: int
    H: int
    Dh: int
    F: int
    KF: int

    @property
    def T(self): return self.Bl * self.S
    @property
    def AW(self): return self.H * self.Dh
    @property
    def NM(self): return 2 * self.C + 3 * self.AW + 2 * self.D
    @property
    def q_off(self): return 2 * self.C
    @property
    def g_off(self): return 2 * self.C + 3 * self.AW
    @property
    def alpha(self): return (2.0 * self.L) ** 0.25


def _pcall(body, **kw):
    return pl.pallas_call(body, **kw)


def _params(*sem):
    return pltpu.CompilerParams(dimension_semantics=sem, vmem_limit_bytes=VMEM_LIMIT)


def _pick(n, prefs):
    for p in prefs:
        if n % p == 0:
            return p
    return n


def _sigmoid(x):
    return 1.0 / (1.0 + jnp.exp(-x))


def _ln_stats(x):
    mu = jnp.mean(x, axis=-1, keepdims=True)
    xc = x - mu
    var = jnp.mean(xc * xc, axis=-1, keepdims=True)
    rstd = lax.rsqrt(var + LN_EPS)
    return xc * rstd, rstd


def _ln_bwd(dxh, xh, rstd):
    return rstd * (dxh - jnp.mean(dxh, axis=-1, keepdims=True) - xh * jnp.mean(dxh * xh, axis=-1, keepdims=True))


def _matmul(a, b, *, mode, name, bias=None, add=None, out_dtype=F32, tm=None, tn=None, tk=None, after=None):
    parts = tuple(a) if isinstance(a, (tuple, list)) else (a,)
    P = len(parts)
    if mode == "tn":
        K, Mp = parts[0].shape
        M, Kp = P * Mp, K
    else:
        M, Kp = parts[0].shape
        K, Mp = P * Kp, M
    N = b.shape[0] if mode == "nt" else b.shape[1]
    lane_tiles = (1536, 1408, 1024, 768, 512, 256, 128)
    tm = tm or _pick(Mp, lane_tiles if mode == "tn" else (1024, 512, 256, 128, 64, 32, 16, 8))
    tn = tn or _pick(N, lane_tiles)
    tk = tk or _pick(Kp, (1024, 512, 256, 128) if mode == "tn" else lane_tiles)
    nk = K // tk
    per = Mp // tm if mode == "tn" else Kp // tk
    dn = {"nn": (((1,), (0,)), ((), ())), "nt": NT, "tn": TN}[mode]
    has_bias, has_add, has_after = bias is not None, add is not None, after is not None

    def body(*refs):
        a_refs, b_ref = refs[:P], refs[P]
        pos = P + 1
        bias_ref = refs[pos] if has_bias else None
        pos += has_bias
        add_ref = refs[pos] if has_add else None
        pos += has_add + has_after
        o_ref = refs[pos]
        acc_ref = refs[pos + 1] if nk > 1 else None
        k = pl.program_id(2)

        def finish(acc):
            if has_bias:
                acc = acc + bias_ref[...]
            if has_add:
                acc = acc + add_ref[...]
            o_ref[...] = acc.astype(out_dtype)

        def accumulate(a_ref):
            part = lax.dot_general(a_ref[...], b_ref[...], dn, preferred_element_type=F32)
            if nk == 1:
                finish(part)
            else:
                @pl.when(k == 0)
                def _():
                    acc_ref[...] = part

                @pl.when(k > 0)
                def _():
                    acc_ref[...] += part

        if P == 1:
            accumulate(a_refs[0])
        else:
            step = pl.program_id(0 if mode == "tn" else 2)
            for p in range(P):
                pl.when(step // per == p)(functools.partial(accumulate, a_refs[p]))
        if nk > 1:
            @pl.when(k == nk - 1)
            def _():
                finish(acc_ref[...])

    def a_spec(p):
        if mode == "tn":
            return pl.BlockSpec((tk, tm), lambda i, j, k: (k, jnp.clip(i - p * per, 0, per - 1)))
        return pl.BlockSpec((tm, tk), lambda i, j, k: (i, jnp.clip(k - p * per, 0, per - 1)))

    b_spec = pl.BlockSpec((tn, tk), lambda i, j, k: (j, k)) if mode == "nt" else pl.BlockSpec((tk, tn), lambda i, j, k: (k, j))
    in_specs, args = [a_spec(p) for p in range(P)] + [b_spec], list(parts) + [b]
    if has_bias:
        in_specs.append(pl.BlockSpec((1, tn), lambda i, j, k: (0, j)))
        args.append(bias)
    if has_add:
        in_specs.append(pl.BlockSpec((tm, tn), lambda i, j, k: (i, j)))
        args.append(add)
    if has_after:
        in_specs.append(pl.BlockSpec(memory_space=pl.ANY))
        args.append(after)
    return _pcall(
        body, name=name, grid=(M // tm, N // tn, nk), in_specs=in_specs,
        out_specs=pl.BlockSpec((tm, tn), lambda i, j, k: (i, j)),
        out_shape=jax.ShapeDtypeStruct((M, N), out_dtype),
        scratch_shapes=[pltpu.VMEM((tm, tn), F32)] if nk > 1 else [],
        compiler_params=_params("parallel", "parallel", "arbitrary"),
    )(*args)


def _colsum(x, *, name):
    T, N = x.shape
    tr = _pick(T, (512, 256, 128, 64, 32, 16))
    tc = _pick(N, (1536, 1024, 512, 256, 128))

    def body(x_ref, o_ref):
        @pl.when(pl.program_id(1) == 0)
        def _():
            o_ref[...] = jnp.zeros_like(o_ref)

        o_ref[...] += jnp.sum(x_ref[...].astype(F32), axis=0, keepdims=True)

    return _pcall(body, name=name, grid=(N // tc, T // tr), in_specs=[pl.BlockSpec((tr, tc), lambda j, i: (i, j))],
                  out_specs=pl.BlockSpec((1, tc), lambda j, i: (0, j)), out_shape=jax.ShapeDtypeStruct((1, N), F32),
                  compiler_params=_params("parallel", "arbitrary"))(x)


def _row_tile(cfg):
    return _pick(cfg.S, (512, 256, 128, 64, 32, 16, 8))


def _ln_mod_fwd(x, shift, scale, cfg, *, name):
    tr = _row_tile(cfg)
    tpb = cfg.S // tr

    def body(x_ref, sh_ref, sc_ref, u_ref):
        xh, _ = _ln_stats(x_ref[...])
        u_ref[...] = (xh * (1.0 + sc_ref[0]) + sh_ref[0]).astype(BF16)

    row = pl.BlockSpec((tr, cfg.D), lambda i: (i, 0))
    per_b = pl.BlockSpec((1, 1, cfg.D), lambda i: (i // tpb, 0, 0))
    return _pcall(body, name=name, grid=(cfg.T // tr,), in_specs=[row, per_b, per_b], out_specs=row,
                  out_shape=jax.ShapeDtypeStruct((cfg.T, cfg.D), BF16), compiler_params=_params("parallel"))(x, shift, scale)


def _res_ln_fwd(xin, br, gate, g, b, cfg, *, name, nxt=None):
    tr = _row_tile(cfg)
    tpb = cfg.S // tr
    alpha = cfg.alpha

    def body(*refs):
        x_ref, br_ref, gt_ref, g_ref, b_ref = refs[:5]
        r = alpha * x_ref[...] + (1.0 + gt_ref[0]) * br_ref[...]
        xh, _ = _ln_stats(r)
        xo = xh * g_ref[...] + b_ref[...]
        if nxt is None:
            refs[5][...] = xo
        else:
            sh_ref, sc_ref, xo_ref, u_ref = refs[5:]
            xo_ref[...] = xo
            uh, _ = _ln_stats(xo)
            u_ref[...] = (uh * (1.0 + sc_ref[0]) + sh_ref[0]).astype(BF16)

    row = pl.BlockSpec((tr, cfg.D), lambda i: (i, 0))
    per_b = pl.BlockSpec((1, 1, cfg.D), lambda i: (i // tpb, 0, 0))
    vec = pl.BlockSpec((1, cfg.D), lambda i: (0, 0))
    in_specs, args = [row, row, per_b, vec, vec], [xin, br, gate, g, b]
    out_specs, out_shape = row, jax.ShapeDtypeStruct((cfg.T, cfg.D), F32)
    if nxt is not None:
        in_specs += [per_b, per_b]
        args += list(nxt)
        out_specs = [row, row]
        out_shape = [out_shape, jax.ShapeDtypeStruct((cfg.T, cfg.D), BF16)]
    return _pcall(body, name=name, grid=(cfg.T // tr,), in_specs=in_specs, out_specs=out_specs, out_shape=out_shape,
                  compiler_params=_params("parallel"))(*args)


def _loss_grad(y, tgt, cfg, *, name):
    tr = _row_tile(cfg)
    nt = cfg.T // tr
    inv_d = 1.0 / cfg.D

    def body(y_ref, t_ref, dy_ref, ls_ref):
        e = y_ref[...] - t_ref[...]
        dy_ref[...] = e * inv_d
        ls_ref[...] = jnp.full((1, 1, LANES), jnp.sum(e * e), F32)

    row = pl.BlockSpec((tr, cfg.D), lambda i: (i, 0))
    return _pcall(body, name=name, grid=(nt,), in_specs=[row, row],
                  out_specs=[row, pl.BlockSpec((1, 1, LANES), lambda i: (i, 0, 0))],
                  out_shape=[jax.ShapeDtypeStruct((cfg.T, cfg.D), F32), jax.ShapeDtypeStruct((nt, 1, LANES), F32)],
                  compiler_params=_params("parallel"))(y, tgt)


def _res_ln_bwd(dy, xin, br, gate, g, cfg, *, name, mod=None):
    tr = _row_tile(cfg)
    tpb = cfg.S // tr
    alpha = cfg.alpha
    fused = mod is not None

    def body(*refs):
        if fused:
            du_ref, xa_ref, sc_ref, dres_ref, x_ref, br_ref, gt_ref, g_ref = refs[:8]
            dx_ref, dbr_ref, dg_ref, db_ref, dgt_ref, dbs_ref, dsc_ref, dsh_ref = refs[8:]
        else:
            dy_ref, x_ref, br_ref, gt_ref, g_ref, dx_ref, dbr_ref, dg_ref, db_ref, dgt_ref, dbs_ref = refs
        i = pl.program_id(0)

        @pl.when(i == 0)
        def _():
            dg_ref[...] = jnp.zeros_like(dg_ref)
            db_ref[...] = jnp.zeros_like(db_ref)
            dbs_ref[...] = jnp.zeros_like(dbs_ref)

        @pl.when(i % tpb == 0)
        def _():
            dgt_ref[...] = jnp.zeros_like(dgt_ref)
            if fused:
                dsc_ref[...] = jnp.zeros_like(dsc_ref)
                dsh_ref[...] = jnp.zeros_like(dsh_ref)

        if fused:
            du = du_ref[...]
            ah, arstd = _ln_stats(xa_ref[...])
            dsc_ref[0] += jnp.sum(du * ah, axis=0, keepdims=True)
            dsh_ref[0] += jnp.sum(du, axis=0, keepdims=True)
            dy = _ln_bwd(du * (1.0 + sc_ref[0]), ah, arstd) + dres_ref[...]
        else:
            dy = dy_ref[...]
        brv, one_gate = br_ref[...], 1.0 + gt_ref[0]
        xh, rstd = _ln_stats(alpha * x_ref[...] + one_gate * brv)
        dg_ref[...] += jnp.sum(dy * xh, axis=0, keepdims=True)
        db_ref[...] += jnp.sum(dy, axis=0, keepdims=True)
        dr = _ln_bwd(dy * g_ref[...], xh, rstd)
        dx_ref[...] = alpha * dr
        dbr = one_gate * dr
        dbr_ref[...] = dbr.astype(BF16)
        dbs_ref[...] += jnp.sum(dbr, axis=0, keepdims=True)
        dgt_ref[0] += jnp.sum(dr * brv, axis=0, keepdims=True)

    row = pl.BlockSpec((tr, cfg.D), lambda i: (i, 0))
    per_b = pl.BlockSpec((1, 1, cfg.D), lambda i: (i // tpb, 0, 0))
    vec = pl.BlockSpec((1, cfg.D), lambda i: (0, 0))
    vs = jax.ShapeDtypeStruct((1, cfg.D), F32)
    bs = jax.ShapeDtypeStruct((cfg.Bl, 1, cfg.D), F32)
    in_specs, args = [row, row, row, per_b, vec], [dy, xin, br, gate, g]
    out_specs = [row, row, vec, vec, per_b, vec]
    out_shape = [jax.ShapeDtypeStruct((cfg.T, cfg.D), F32), jax.ShapeDtypeStruct((cfg.T, cfg.D), BF16), vs, vs, bs, vs]
    if fused:
        in_specs, args = [row, row, per_b, row] + in_specs[1:], list(mod) + args[1:]
        out_specs, out_shape = out_specs + [per_b, per_b], out_shape + [bs, bs]
    return _pcall(body, name=name, grid=(cfg.T // tr,), in_specs=in_specs, out_specs=out_specs, out_shape=out_shape,
                  compiler_params=_params("arbitrary"))(*args)


def _ln_mod_bwd(du, xin, scale, dres, cfg, *, name):
    tr = _row_tile(cfg)
    tpb = cfg.S // tr

    def body(du_ref, x_ref, sc_ref, dres_ref, dx_ref, dsc_ref, dsh_ref):
        @pl.when(pl.program_id(0) % tpb == 0)
        def _():
            dsc_ref[...] = jnp.zeros_like(dsc_ref)
            dsh_ref[...] = jnp.zeros_like(dsh_ref)

        du = du_ref[...]
        xh, rstd = _ln_stats(x_ref[...])
        dsc_ref[0] += jnp.sum(du * xh, axis=0, keepdims=True)
        dsh_ref[0] += jnp.sum(du, axis=0, keepdims=True)
        dx_ref[...] = _ln_bwd(du * (1.0 + sc_ref[0]), xh, rstd) + dres_ref[...]

    row = pl.BlockSpec((tr, cfg.D), lambda i: (i, 0))
    per_b = pl.BlockSpec((1, 1, cfg.D), lambda i: (i // tpb, 0, 0))
    bs = jax.ShapeDtypeStruct((cfg.Bl, 1, cfg.D), F32)
    return _pcall(body, name=name, grid=(cfg.T // tr,), in_specs=[row, row, per_b, row], out_specs=[row, per_b, per_b],
                  out_shape=[jax.ShapeDtypeStruct((cfg.T, cfg.D), F32), bs, bs],
                  compiler_params=_params("arbitrary"))(du, xin, scale, dres)


def _merge_tiles(cfg):
    tr = _pick(cfg.T, (512, 256, 128, 64, 32, 16))
    tc = _pick(math.gcd(cfg.g_off, cfg.D), (512, 256, 128))
    return tr, tc


def _merge_fwd(zm, ya, yb, cfg, *, name):
    tr, tc = _merge_tiles(cfg)
    ga0, gb0 = cfg.g_off // tc, (cfg.g_off + cfg.D) // tc

    def body(ga_ref, gb_ref, ya_ref, yb_ref, m_ref):
        m_ref[...] = (_sigmoid(ga_ref[...]) * ya_ref[...] + _sigmoid(gb_ref[...]) * yb_ref[...]).astype(BF16)

    blk = pl.BlockSpec((tr, tc), lambda i, j: (i, j))
    return _pcall(body, name=name, grid=(cfg.T // tr, cfg.D // tc),
                  in_specs=[pl.BlockSpec((tr, tc), lambda i, j: (i, ga0 + j)), pl.BlockSpec((tr, tc), lambda i, j: (i, gb0 + j)), blk, blk],
                  out_specs=blk, out_shape=jax.ShapeDtypeStruct((cfg.T, cfg.D), BF16),
                  compiler_params=_params("parallel", "parallel"))(zm, zm, ya, yb)


def _merge_bwd(dm, zm, ya, yb, cfg, *, name):
    tr, tc = _merge_tiles(cfg)
    ga0, gb0 = cfg.g_off // tc, (cfg.g_off + cfg.D) // tc

    def body(dm_ref, ga_ref, gb_ref, ya_ref, yb_ref, dya_ref, dyb_ref, dga_ref, dgb_ref):
        dm = dm_ref[...]
        ga, gb = _sigmoid(ga_ref[...]), _sigmoid(gb_ref[...])
        dya_ref[...] = (dm * ga).astype(BF16)
        dyb_ref[...] = (dm * gb).astype(BF16)
        dga_ref[...] = (dm * ya_ref[...] * ga * (1.0 - ga)).astype(BF16)
        dgb_ref[...] = (dm * yb_ref[...] * gb * (1.0 - gb)).astype(BF16)

    blk = pl.BlockSpec((tr, tc), lambda i, j: (i, j))
    o = jax.ShapeDtypeStruct((cfg.T, cfg.D), BF16)
    return _pcall(body, name=name, grid=(cfg.T // tr, cfg.D // tc),
                  in_specs=[blk, pl.BlockSpec((tr, tc), lambda i, j: (i, ga0 + j)), pl.BlockSpec((tr, tc), lambda i, j: (i, gb0 + j)), blk, blk],
                  out_specs=[blk] * 4, out_shape=[o] * 4, compiler_params=_params("parallel", "parallel"))(dm, zm, zm, ya, yb)


CONV_A_HALO = 32
CONV_A_CHUNK = 32
CONV_A_TAPS = 32
FFN_TAPS = 8


SUBLANES = 8


def _conv_a_tile(cfg):
    assert cfg.KW - 1 <= CONV_A_HALO
    return _pick(cfg.S, (256, 128, 64, 32))


def _shift_copies(src_s, sh_s):
    rows = src_s.shape[0] - SUBLANES
    for b in range(1, SUBLANES):
        sh_s[b - 1, :, :] = src_s[b:b + rows, :]


def _rows(src_s, sh_s, start, n):
    a, b = divmod(start, SUBLANES)
    return src_s[start:start + n, :] if b == 0 else sh_s[b - 1, SUBLANES * a:SUBLANES * a + n, :]


def _conv_a_fwd(zm, w, cb, g, b, cfg, *, name):
    C, KW, HALO, CH = cfg.C, cfg.KW, CONV_A_HALO, CONV_A_CHUNK
    ts = _conv_a_tile(cfg)
    tpb = cfg.S // ts
    lead = HALO - (KW - 1)

    def body(z_ref, zp_ref, w_ref, cb_ref, g_ref, b_ref, o_ref, a0_s, a0_sh):
        first = pl.program_id(0) % tpb == 0
        prev = zp_ref[:, :C] * _sigmoid(zp_ref[:, C:])
        a0_s[0:HALO, :] = jnp.where(first, 0.0, prev)
        a0_s[HALO:HALO + ts, :] = z_ref[:, :C] * _sigmoid(z_ref[:, C:])
        _shift_copies(a0_s, a0_sh)
        for r0 in range(0, ts, CH):
            acc = jnp.zeros((CH, C), F32)
            for k in range(KW):
                acc = acc + w_ref[k:k + 1, :] * _rows(a0_s, a0_sh, r0 + lead + k, CH)
            xh, _ = _ln_stats(acc + cb_ref[...])
            a2 = xh * g_ref[...] + b_ref[...]
            o_ref[r0:r0 + CH, :] = (a2 * _sigmoid(a2)).astype(BF16)

    hb = ts // HALO
    vec = pl.BlockSpec((1, C), lambda i: (0, 0))
    return _pcall(body, name=name, grid=(cfg.T // ts,),
                  in_specs=[pl.BlockSpec((ts, 2 * C), lambda i: (i, 0)),
                            pl.BlockSpec((HALO, 2 * C), lambda i: (jnp.maximum(i * hb - 1, 0), 0)),
                            pl.BlockSpec((CONV_A_TAPS, C), lambda i: (0, 0)), vec, vec, vec],
                  out_specs=pl.BlockSpec((ts, C), lambda i: (i, 0)), out_shape=jax.ShapeDtypeStruct((cfg.T, C), BF16),
                  scratch_shapes=[pltpu.VMEM((HALO + ts, C), F32), pltpu.VMEM((SUBLANES - 1, HALO + ts - SUBLANES, C), F32)],
                  compiler_params=_params("parallel"))(zm, zm, w, cb, g, b)


def _conv_a_bwd(da3, zm, w, cb, g, b, cfg, *, name):
    C, KW, HALO, CH = cfg.C, cfg.KW, CONV_A_HALO, CONV_A_CHUNK
    ts = _conv_a_tile(cfg)
    tpb = cfg.S // ts
    nt = cfg.T // ts
    lead = HALO - (KW - 1)
    ext = ts + HALO

    def body(z_ref, zp_ref, zn_ref, d_ref, dn_ref, w_ref, cb_ref, g_ref, b_ref,
             dz_ref, dw_ref, dcb_ref, dg_ref, db_ref, a0_s, d3_s, da1_s, a0_sh, da1_sh):
        i = pl.program_id(0)
        first, last = i % tpb == 0, i % tpb == tpb - 1

        @pl.when(i == 0)
        def _():
            dw_ref[...] = jnp.zeros_like(dw_ref)
            dcb_ref[...] = jnp.zeros_like(dcb_ref)
            dg_ref[...] = jnp.zeros_like(dg_ref)
            db_ref[...] = jnp.zeros_like(db_ref)

        a0_s[0:HALO, :] = jnp.where(first, 0.0, zp_ref[:, :C] * _sigmoid(zp_ref[:, C:]))
        a0_s[HALO:HALO + ts, :] = z_ref[:, :C] * _sigmoid(z_ref[:, C:])
        a0_s[HALO + ts:HALO + ext, :] = zn_ref[:, :C] * _sigmoid(zn_ref[:, C:])
        d3_s[0:ts, :] = d_ref[...]
        d3_s[ts:ext, :] = jnp.where(last, 0.0, dn_ref[...])
        _shift_copies(a0_s, a0_sh)
        dcb, dg, db = jnp.zeros((1, C), F32), jnp.zeros((1, C), F32), jnp.zeros((1, C), F32)
        for r0 in range(0, ext, CH):
            acc = jnp.zeros((CH, C), F32)
            for k in range(KW):
                acc = acc + w_ref[k:k + 1, :] * _rows(a0_s, a0_sh, r0 + lead + k, CH)
            xh, rstd = _ln_stats(acc + cb_ref[...])
            a2 = xh * g_ref[...] + b_ref[...]
            sg = _sigmoid(a2)
            da2 = d3_s[r0:r0 + CH, :] * (sg * (1.0 + a2 * (1.0 - sg)))
            da1 = _ln_bwd(da2 * g_ref[...], xh, rstd)
            da1_s[r0:r0 + CH, :] = da1
            if r0 < ts:
                dg = dg + jnp.sum(da2 * xh, axis=0, keepdims=True)
                db = db + jnp.sum(da2, axis=0, keepdims=True)
                dcb = dcb + jnp.sum(da1, axis=0, keepdims=True)
        dg_ref[...] += dg
        db_ref[...] += db
        dcb_ref[...] += dcb
        _shift_copies(da1_s, da1_sh)
        for k in range(KW):
            dwk = jnp.zeros((CH, C), F32)
            for r0 in range(0, ts, CH):
                dwk = dwk + da1_s[r0:r0 + CH, :] * _rows(a0_s, a0_sh, r0 + lead + k, CH)
            dw_ref[k:k + 1, :] += jnp.sum(dwk, axis=0, keepdims=True)
        for r0 in range(0, ts, CH):
            da0 = jnp.zeros((CH, C), F32)
            for k in range(KW):
                da0 = da0 + w_ref[k:k + 1, :] * _rows(da1_s, da1_sh, r0 + KW - 1 - k, CH)
            val, sg = z_ref[r0:r0 + CH, :C], _sigmoid(z_ref[r0:r0 + CH, C:])
            dz_ref[r0:r0 + CH, :C] = (da0 * sg).astype(BF16)
            dz_ref[r0:r0 + CH, C:] = (da0 * val * sg * (1.0 - sg)).astype(BF16)

    hb = ts // HALO
    nhb = cfg.T // HALO
    vec = pl.BlockSpec((1, C), lambda i: (0, 0))
    vs = jax.ShapeDtypeStruct((1, C), F32)
    return _pcall(body, name=name, grid=(nt,),
                  in_specs=[pl.BlockSpec((ts, 2 * C), lambda i: (i, 0)),
                            pl.BlockSpec((HALO, 2 * C), lambda i: (jnp.maximum(i * hb - 1, 0), 0)),
                            pl.BlockSpec((HALO, 2 * C), lambda i: (jnp.minimum((i + 1) * hb, nhb - 1), 0)),
                            pl.BlockSpec((ts, C), lambda i: (i, 0)),
                            pl.BlockSpec((HALO, C), lambda i: (jnp.minimum((i + 1) * hb, nhb - 1), 0)),
                            pl.BlockSpec((CONV_A_TAPS, C), lambda i: (0, 0)), vec, vec, vec],
                  out_specs=[pl.BlockSpec((ts, 2 * C), lambda i: (i, 0)), pl.BlockSpec((CONV_A_TAPS, C), lambda i: (0, 0)), vec, vec, vec],
                  out_shape=[jax.ShapeDtypeStruct((cfg.T, 2 * C), BF16), jax.ShapeDtypeStruct((CONV_A_TAPS, C), F32), vs, vs, vs],
                  scratch_shapes=[pltpu.VMEM((HALO + ext, C), F32), pltpu.VMEM((ext, C), F32), pltpu.VMEM((ext, C), F32),
                                  pltpu.VMEM((SUBLANES - 1, HALO + ext - SUBLANES, C), F32),
                                  pltpu.VMEM((SUBLANES - 1, ext - SUBLANES, C), F32)],
                  compiler_params=_params("arbitrary"))(zm, zm, zm, da3, da3, w, cb, g, b)


def _cum_tile(cfg):
    return _pick(cfg.S, (256, 128, 64, 32, 16, 8))


def _fgate_fwd(zf, cfg, *, name):
    tc = _cum_tile(cfg)
    tpb = cfg.S // tc
    hp = _attn_tiles(cfg)[2]
    nb = cfg.H // hp

    def body(z_ref, o_ref, carry):
        @pl.when(pl.program_id(0) % tpb == 0)
        def _():
            carry[...] = jnp.zeros_like(carry)

        z = z_ref[...]
        logf = jnp.minimum(z, 0.0) - jnp.log(1.0 + jnp.exp(-jnp.abs(z)))
        tri = (lax.broadcasted_iota(jnp.int32, (tc, tc), 0) >= lax.broadcasted_iota(jnp.int32, (tc, tc), 1)).astype(F32)
        cum = jnp.dot(tri, logf, precision=lax.Precision.HIGHEST, preferred_element_type=F32) + carry[...]
        carry[...] = cum[tc - 1:tc, :]
        o_ref[0] = cum
        for b in range(1, nb):
            o_ref[b] = pltpu.roll(cum, LANES - hp * b, axis=1)

    return _pcall(body, name=name, grid=(cfg.T // tc,), in_specs=[pl.BlockSpec((tc, LANES), lambda i: (i, 0))],
                  out_specs=pl.BlockSpec((nb, tc, LANES), lambda i: (0, i, 0)),
                  out_shape=jax.ShapeDtypeStruct((nb, cfg.T, LANES), F32), scratch_shapes=[pltpu.VMEM((1, LANES), F32)],
                  compiler_params=_params("arbitrary"))(zf)


def _fgate_bwd(dcum_c, zf, cfg, *, name):
    tc = _cum_tile(cfg)
    tpb = cfg.S // tc
    nt = cfg.T // tc
    hp = _attn_tiles(cfg)[2]
    nb = cfg.H // hp

    def body(d_ref, z_ref, o_ref, carry):
        @pl.when(pl.program_id(0) % tpb == 0)
        def _():
            carry[...] = jnp.zeros_like(carry)

        d = d_ref[0]
        for b in range(1, nb):
            d = d + pltpu.roll(d_ref[b], hp * b, axis=1)
        tri = (lax.broadcasted_iota(jnp.int32, (tc, tc), 0) <= lax.broadcasted_iota(jnp.int32, (tc, tc), 1)).astype(F32)
        suf = jnp.dot(tri, d, precision=lax.Precision.HIGHEST, preferred_element_type=F32) + carry[...]
        o_ref[...] = (suf * _sigmoid(-z_ref[...])).astype(BF16)
        carry[...] = suf[0:1, :]

    blk = pl.BlockSpec((tc, LANES), lambda i: (nt - 1 - i, 0))
    return _pcall(body, name=name, grid=(nt,), in_specs=[pl.BlockSpec((nb, tc, LANES), lambda i: (0, nt - 1 - i, 0)), blk],
                  out_specs=blk, out_shape=jax.ShapeDtypeStruct((cfg.T, LANES), BF16),
                  scratch_shapes=[pltpu.VMEM((1, LANES), F32)], compiler_params=_params("arbitrary"))(dcum_c, zf)


def _attn_tiles(cfg):
    assert LANES % cfg.Dh == 0 and cfg.H % (LANES // cfg.Dh) == 0
    tk = _pick(cfg.S, (256, 128))
    tq = _pick(cfg.S, (2 * tk, tk))
    return tq, tk, LANES // cfg.Dh


BIAS_LANES = 3


def _head_lanes(hd, cfg, hp):
    li = lax.broadcasted_iota(jnp.int32, (1, LANES), 1)
    own = (li >= hd * cfg.Dh) & (li < (hd + 1) * cfg.Dh)
    return own, li, ((hd + 1) % hp) * cfg.Dh


def _q_aug(q, hd, cfg, hp):
    own, li, b0 = _head_lanes(hd, cfg, hp)
    ones = ((li >= b0) & (li < b0 + BIAS_LANES)).astype(F32)
    return jnp.where(own, q * cfg.Dh ** -0.5, ones).astype(BF16)


def _k_aug(k, ck, hd, cfg, hp):
    own, li, b0 = _head_lanes(hd, cfg, hp)
    hi = ck.astype(BF16).astype(F32)
    mid = (ck - hi).astype(BF16).astype(F32)
    lo = ck - hi - mid
    bias = jnp.where(li == b0, -hi, jnp.where(li == b0 + 1, -mid, jnp.where(li == b0 + 2, -lo, 0.0)))
    return jnp.where(own, k, bias).astype(BF16)


def _attn_fwd(zm, cum_c, cfg, *, name):
    S, Dh = cfg.S, cfg.Dh
    tq, tk, hp = _attn_tiles(cfg)
    assert hp >= 2
    nq, nb, per = S // tq, cfg.H // hp, tq // tk
    qb, kb, vb = cfg.q_off // LANES, (cfg.q_off + cfg.AW) // LANES, (cfg.q_off + 2 * cfg.AW) // LANES

    def body(q_ref, k_ref, v_ref, cc_ref, o_ref, o32_ref, lse_ref, ka_s, vt_s):
        qi = pl.program_id(2)

        @pl.when(qi == 0)
        def _():
            def prep(c, _):
                r = pl.multiple_of(c * tk, tk)
                kc = k_ref[pl.ds(r, tk), :]
                for hd in range(hp):
                    ka_s[hd, pl.ds(r, tk), :] = _k_aug(kc, cc_ref[0, pl.ds(r, tk), hd:hd + 1], hd, cfg, hp)
                vt_s[:, pl.ds(r, tk)] = v_ref[pl.ds(r, tk), :].T.astype(BF16)
                return 0

            lax.fori_loop(0, S // tk, prep, 0)

        key_i = lax.broadcasted_iota(jnp.int32, (tk, tq), 0)
        qry_i = lax.broadcasted_iota(jnp.int32, (tk, tq), 1)
        qf = q_ref[...]
        qa = [_q_aug(qf, hd, cfg, hp) for hd in range(hp)]

        def scores(j):
            r = pl.multiple_of(j * tk, tk)
            return tuple(lax.dot_general(ka_s[hd, pl.ds(r, tk), :], qa[hd], NT, preferred_element_type=F32) for hd in range(hp))

        def chunk(j, s_all, carry, diag=None):
            r = pl.multiple_of(j * tk, tk)
            new = []
            for hd in range(hp):
                m, l, acc = carry[hd]
                s = s_all[hd]
                if diag is not None:
                    s = jnp.where(key_i + diag * tk <= qry_i, s, NEG)
                m_new = jnp.maximum(m, jnp.max(s, axis=0, keepdims=True))
                a = jnp.exp(m - m_new)
                p = jnp.exp(s - m_new)
                l = a * l + jnp.sum(p, axis=0, keepdims=True)
                p_hi = p.astype(BF16)
                p_lo = (p - p_hi.astype(F32)).astype(BF16)
                vt = vt_s[hd * Dh:(hd + 1) * Dh, pl.ds(r, tk)]
                acc = a * acc + (jnp.dot(vt, p_hi, preferred_element_type=F32) + jnp.dot(vt, p_lo, preferred_element_type=F32))
                new.append((m_new, l, acc))
            return tuple(new)

        init = tuple((jnp.full((1, tq), NEG, F32), jnp.zeros((1, tq), F32), jnp.zeros((Dh, tq), F32)) for _ in range(hp))
        n_full = qi * per

        def step(j, c):
            stats, s_cur = c
            s_next = scores(j + 1)
            return chunk(j, s_cur, stats), s_next

        res, s_cur = lax.fori_loop(0, n_full, step, (init, scores(0)))
        for d in range(per):
            s_next = scores(n_full + d + 1) if d + 1 < per else None
            res = chunk(n_full + d, s_cur, res, diag=d)
            s_cur = s_next
        o = jnp.concatenate([acc / l for _, l, acc in res], axis=0).T
        o_ref[...] = o.astype(BF16)
        o32_ref[...] = o
        lse_ref[...] = jnp.zeros_like(lse_ref)
        for hd in range(hp):
            lse_ref[0, 0, hd:hd + 1, :] = res[hd][0] + jnp.log(res[hd][1])

    return _pcall(body, name=name, grid=(cfg.Bl, nb, nq),
                  in_specs=[pl.BlockSpec((tq, LANES), lambda b, h, i: (b * nq + i, qb + h)),
                            pl.BlockSpec((S, LANES), lambda b, h, i: (b, kb + h)),
                            pl.BlockSpec((S, LANES), lambda b, h, i: (b, vb + h)),
                            pl.BlockSpec((1, S, LANES), lambda b, h, i: (h, b, 0))],
                  out_specs=[pl.BlockSpec((tq, LANES), lambda b, h, i: (b * nq + i, h)),
                             pl.BlockSpec((tq, LANES), lambda b, h, i: (b * nq + i, h)),
                             pl.BlockSpec((1, 1, SUBLANES, tq), lambda b, h, i: (b, h, 0, i))],
                  out_shape=[jax.ShapeDtypeStruct((cfg.T, cfg.AW), BF16), jax.ShapeDtypeStruct((cfg.T, cfg.AW), F32),
                             jax.ShapeDtypeStruct((cfg.Bl, nb, SUBLANES, S), F32)],
                  scratch_shapes=[pltpu.VMEM((hp, S, LANES), BF16), pltpu.VMEM((LANES, S), BF16)],
                  compiler_params=_params("parallel", "parallel", "arbitrary"))(zm, zm, zm, cum_c)


def _attn_bwd(zm, cum_c, o, do, lse, cfg, *, name):
    S, Dh = cfg.S, cfg.Dh
    tq, t, hp = _attn_tiles(cfg)
    nq, nk, nb, per = S // tq, S // t, cfg.H // hp, tq // t
    qb, kb, vb = cfg.q_off // LANES, (cfg.q_off + cfg.AW) // LANES, (cfg.q_off + 2 * cfg.AW) // LANES
    scale = Dh ** -0.5

    def body(q_ref, k_ref, v_ref, cc_ref, o_ref, do_ref, lse_ref, dq_ref, dk_ref, dv_ref, dcc_ref,
             ka_s, qa_s, vz_s, kt_s, dd_s, dqt_s):
        li = lax.broadcasted_iota(jnp.int32, (1, LANES), 1)
        ri = lax.broadcasted_iota(jnp.int32, (LANES, 1), 0)
        key_i = lax.broadcasted_iota(jnp.int32, (t, tq), 0)
        qry_i = lax.broadcasted_iota(jnp.int32, (t, tq), 1)

        def prep(c, _):
            r = pl.multiple_of(c * t, t)
            kc, vc, qc = k_ref[pl.ds(r, t), :], v_ref[pl.ds(r, t), :], q_ref[pl.ds(r, t), :]
            prod_t = (do_ref[pl.ds(r, t), :].astype(F32) * o_ref[pl.ds(r, t), :].astype(F32)).T
            for hd in range(hp):
                own = _head_lanes(hd, cfg, hp)[0]
                ka_s[hd, pl.ds(r, t), :] = _k_aug(kc, cc_ref[0, pl.ds(r, t), hd:hd + 1], hd, cfg, hp)
                qa_s[hd, pl.ds(r, t), :] = _q_aug(qc, hd, cfg, hp)
                vz_s[hd, pl.ds(r, t), :] = jnp.where(own, vc, 0.0).astype(BF16)
                dd_s[hd:hd + 1, pl.ds(r, t)] = jnp.sum(prod_t[hd * Dh:(hd + 1) * Dh, :], axis=0, keepdims=True)
            kt_s[:, pl.ds(r, t)] = kc.T.astype(BF16)
            dqt_s[:, pl.ds(r, t)] = jnp.zeros((LANES, t), F32)
            return 0

        lax.fori_loop(0, nk, prep, 0)

        def kv_step(j, _):
            rk = pl.multiple_of(j * t, t)
            i0 = j // per

            def tile(i, carry, masked):
                rq = pl.multiple_of(i * tq, tq)
                dob = do_ref[pl.ds(rq, tq), :]
                new, dq_t = [], None
                for hd in range(hp):
                    dk_h, dv_h, dsum_h = carry[hd]
                    qa = qa_s[hd, pl.ds(rq, tq), :]
                    s = lax.dot_general(ka_s[hd, pl.ds(rk, t), :], qa, NT, preferred_element_type=F32)
                    p = jnp.exp(s - lse_ref[0, 0, hd:hd + 1, pl.ds(rq, tq)])
                    if masked:
                        p = jnp.where(key_i + (rk - rq) <= qry_i, p, 0.0)
                    dp = lax.dot_general(vz_s[hd, pl.ds(rk, t), :], dob, NT, preferred_element_type=F32)
                    ds = p * (dp - dd_s[hd:hd + 1, pl.ds(rq, tq)])
                    dsb = ds.astype(BF16)
                    dv_h = dv_h + jnp.dot(p.astype(BF16), dob, preferred_element_type=F32)
                    dk_h = dk_h + jnp.dot(dsb, qa, preferred_element_type=F32)
                    dq_h = jnp.dot(kt_s[:, pl.ds(rk, t)], dsb, preferred_element_type=F32)
                    dq_t = dq_h if hd == 0 else jnp.where((ri >= hd * Dh) & (ri < (hd + 1) * Dh), dq_h, dq_t)
                    for c0 in range(0, tq, LANES):
                        dsum_h = dsum_h + ds[:, c0:c0 + LANES]
                    new.append((dk_h, dv_h, dsum_h))
                dqt_s[:, pl.ds(rq, tq)] += dq_t * scale
                return tuple(new)

            zero = tuple((jnp.zeros((t, LANES), F32),) * 3 for _ in range(hp))
            res = lax.fori_loop(i0 + 1, nq, functools.partial(tile, masked=False), tile(i0, zero, True))
            dk, dv, dcc = res[0][0], res[0][1], jnp.zeros((t, LANES), F32)
            for hd in range(hp):
                own = _head_lanes(hd, cfg, hp)[0]
                if hd > 0:
                    dk, dv = jnp.where(own, res[hd][0], dk), jnp.where(own, res[hd][1], dv)
                dcc = dcc + jnp.where(li == hd, -jnp.sum(res[hd][2], axis=1, keepdims=True), 0.0)
            dk_ref[pl.ds(rk, t), :] = dk.astype(BF16)
            dv_ref[pl.ds(rk, t), :] = dv.astype(BF16)
            dcc_ref[0, pl.ds(rk, t), :] = dcc
            return 0

        lax.fori_loop(0, nk, kv_step, 0)

        def finish(c, _):
            r = pl.multiple_of(c * t, t)
            dq_ref[pl.ds(r, t), :] = dqt_s[:, pl.ds(r, t)].T.astype(BF16)
            return 0

        lax.fori_loop(0, nk, finish, 0)

    blk = pl.BlockSpec((S, LANES), lambda b, h: (b, h))
    cc = pl.BlockSpec((1, S, LANES), lambda b, h: (h, b, 0))
    os_ = jax.ShapeDtypeStruct((cfg.T, cfg.AW), BF16)
    return _pcall(body, name=name, grid=(cfg.Bl, nb),
                  in_specs=[pl.BlockSpec((S, LANES), lambda b, h: (b, qb + h)), pl.BlockSpec((S, LANES), lambda b, h: (b, kb + h)),
                            pl.BlockSpec((S, LANES), lambda b, h: (b, vb + h)), cc, blk, blk,
                            pl.BlockSpec((1, 1, SUBLANES, S), lambda b, h: (b, h, 0, 0))],
                  out_specs=[blk, blk, blk, cc],
                  out_shape=[os_, os_, os_, jax.ShapeDtypeStruct((nb, cfg.T, LANES), F32)],
                  scratch_shapes=[pltpu.VMEM((hp, S, LANES), BF16)] * 3 + [pltpu.VMEM((LANES, S), BF16),
                                  pltpu.VMEM((SUBLANES, S), F32), pltpu.VMEM((LANES, S), F32)],
                  compiler_params=_params("parallel", "parallel"))(zm, zm, zm, cum_c, o, do, lse)


FFN_HALO = 8
FFN_CHUNK = 16


def _ffn_tiles(cfg):
    assert cfg.KF - 1 <= FFN_HALO
    return _pick(cfg.S, (512, 256, 128, 64, 32, 16, 8)), _pick(cfg.F, (256, 128))


def _gelu(x):
    return 0.5 * x * (1.0 + lax.erf(x * (2.0 ** -0.5)))


def _gelu_grad(x):
    return 0.5 * (1.0 + lax.erf(x * (2.0 ** -0.5))) + x * jnp.exp(-0.5 * x * x) * ((2.0 * math.pi) ** -0.5)


def _ffn_conv_fwd(h0, w, cb, cfg, *, name):
    KF, HALO = cfg.KF, FFN_HALO
    ts, tf = _ffn_tiles(cfg)
    tpb, nf = cfg.S // ts, cfg.F // tf
    lead = HALO - (KF - 1)

    CH = FFN_CHUNK

    def body(g_ref, gp_ref, l_ref, lp_ref, wg_ref, wl_ref, cg_ref, cl_ref, o_ref, hg_ref, hl_ref, g_s, l_s):
        first = pl.program_id(1) % tpb == 0
        for s, main, prev in ((g_s, g_ref, gp_ref), (l_s, l_ref, lp_ref)):
            s[0:HALO, :] = jnp.where(first, 0.0, prev[...])
            s[HALO:HALO + CH, :] = main[0:CH, :]
        wg, wl = [wg_ref[k:k + 1, :] for k in range(KF)], [wl_ref[k:k + 1, :] for k in range(KF)]
        for r0 in range(0, ts, CH):
            hg, hl = cg_ref[...], cl_ref[...]
            for k in range(KF):
                if r0 == 0:
                    xg, xl = g_s[lead + k:lead + k + CH, :], l_s[lead + k:lead + k + CH, :]
                else:
                    a = r0 - (KF - 1) + k
                    xg, xl = g_ref[a:a + CH, :], l_ref[a:a + CH, :]
                hg, hl = hg + wg[k] * xg, hl + wl[k] * xl
            o_ref[r0:r0 + CH, :] = (_gelu(hg) * hl).astype(BF16)
            hg_ref[r0:r0 + CH, :], hl_ref[r0:r0 + CH, :] = hg, hl

    hb = ts // HALO
    prev = lambda off: pl.BlockSpec((HALO, tf), lambda j, i: (jnp.maximum(i * hb - 1, 0), off + j))
    main = lambda off: pl.BlockSpec((ts, tf), lambda j, i: (i, off + j))
    wsp = lambda off: pl.BlockSpec((FFN_TAPS, tf), lambda j, i: (0, off + j))
    vsp = lambda off: pl.BlockSpec((1, tf), lambda j, i: (0, off + j))
    hs = jax.ShapeDtypeStruct((cfg.T, cfg.F), F32)
    return _pcall(body, name=name, grid=(nf, cfg.T // ts),
                  in_specs=[main(0), prev(0), main(nf), prev(nf), wsp(0), wsp(nf), vsp(0), vsp(nf)],
                  out_specs=[main(0)] * 3, out_shape=[jax.ShapeDtypeStruct((cfg.T, cfg.F), BF16), hs, hs],
                  scratch_shapes=[pltpu.VMEM((HALO + CH, tf), F32)] * 2,
                  compiler_params=_params("parallel", "parallel"))(h0, h0, h0, h0, w, w, cb, cb)


def _ffn_conv_bwd(df, h0, hg, hl, w, cfg, *, name):
    KF, HALO = cfg.KF, FFN_HALO
    ts, tf = _ffn_tiles(cfg)
    tpb, nf = cfg.S // ts, cfg.F // tf
    ext = ts + HALO

    CH = FFN_CHUNK

    def body(g_ref, l_ref, hg_ref, hgn_ref, hl_ref, hln_ref, d_ref, dn_ref, wg_ref, wl_ref,
             dg_ref, dl_ref, dwg_ref, dwl_ref, dcg_ref, dcl_ref, dhg_s, dhl_s):
        i = pl.program_id(1)
        last = i % tpb == tpb - 1

        @pl.when(i == 0)
        def _():
            dwg_ref[...] = jnp.zeros_like(dwg_ref)
            dwl_ref[...] = jnp.zeros_like(dwl_ref)
            dcg_ref[...] = jnp.zeros_like(dcg_ref)
            dcl_ref[...] = jnp.zeros_like(dcl_ref)

        wg, wl = [wg_ref[k:k + 1, :] for k in range(KF)], [wl_ref[k:k + 1, :] for k in range(KF)]

        def grads(hg, hl, d):
            return d * hl * _gelu_grad(hg), d * _gelu(hg)

        for r0 in range(0, ts, CH):
            dhg_s[r0:r0 + CH, :], dhl_s[r0:r0 + CH, :] = grads(hg_ref[r0:r0 + CH, :], hl_ref[r0:r0 + CH, :], d_ref[r0:r0 + CH, :])
        dhg_s[ts:ext, :], dhl_s[ts:ext, :] = grads(hgn_ref[...], hln_ref[...], jnp.where(last, 0.0, dn_ref[...]))

        for dh_s, x_ref, wk, dx_ref, dw_ref, dc_ref in ((dhg_s, g_ref, wg, dg_ref, dwg_ref, dcg_ref),
                                                        (dhl_s, l_ref, wl, dl_ref, dwl_ref, dcl_ref)):
            dw_acc = [jnp.zeros((CH, tf), F32) for _ in range(KF)]
            for r0 in range(0, ts, CH):
                x = x_ref[r0:r0 + CH, :]
                dx = jnp.zeros((CH, tf), F32)
                for k in range(KF):
                    dhk = dh_s[r0 + KF - 1 - k:r0 + KF - 1 - k + CH, :]
                    dx = dx + wk[k] * dhk
                    dw_acc[k] = dw_acc[k] + x * dhk
                    if k == KF - 1:
                        dc_acc = dhk if r0 == 0 else dc_acc + dhk
                dx_ref[r0:r0 + CH, :] = dx.astype(BF16)
            for k in range(KF):
                dw_ref[k:k + 1, :] += jnp.sum(dw_acc[k], axis=0, keepdims=True)
            dc_ref[...] += jnp.sum(dc_acc, axis=0, keepdims=True)

    hb = ts // HALO
    nhb = cfg.T // HALO
    main = lambda off: pl.BlockSpec((ts, tf), lambda j, i: (i, off + j))
    nxt = pl.BlockSpec((HALO, tf), lambda j, i: (jnp.minimum((i + 1) * hb, nhb - 1), j))
    wsp = lambda off: pl.BlockSpec((FFN_TAPS, tf), lambda j, i: (0, off + j))
    vsp = pl.BlockSpec((1, tf), lambda j, i: (0, j))
    dxs, dws, dcs = (jax.ShapeDtypeStruct((cfg.T, cfg.F), BF16), jax.ShapeDtypeStruct((FFN_TAPS, cfg.F), F32),
                     jax.ShapeDtypeStruct((1, cfg.F), F32))
    return _pcall(body, name=name, grid=(nf, cfg.T // ts),
                  in_specs=[main(0), main(nf), main(0), nxt, main(0), nxt, main(0), nxt, wsp(0), wsp(nf)],
                  out_specs=[main(0), main(0), wsp(0), wsp(0), vsp, vsp],
                  out_shape=[dxs, dxs, dws, dws, dcs, dcs],
                  scratch_shapes=[pltpu.VMEM((ext, tf), F32)] * 2,
                  compiler_params=_params("parallel", "arbitrary"))(h0, h0, hg, hg, hl, hl, df, df, w, w)


def _ada_fwd(c_all, w, b, *, name):
    L, D, n = w.shape
    B = c_all.shape[0]

    def body(c_ref, w_ref, b_ref, o_ref):
        c = c_ref[...]
        act = (c * _sigmoid(c)).astype(BF16)
        o_ref[0] = jnp.dot(act, w_ref[0].astype(BF16), preferred_element_type=F32) + b_ref[0]

    return _pcall(body, name=name, grid=(L,),
                  in_specs=[pl.BlockSpec((B, D), lambda l: (0, 0)), pl.BlockSpec((1, D, n), lambda l: (l, 0, 0)),
                            pl.BlockSpec((1, 1, n), lambda l: (l, 0, 0))],
                  out_specs=pl.BlockSpec((1, B, n), lambda l: (l, 0, 0)), out_shape=jax.ShapeDtypeStruct((L, B, n), F32),
                  compiler_params=_params("parallel"))(c_all, w, b)


def _ada_bwd(c_all, dmod, *, name):
    L, B, n = dmod.shape
    D = c_all.shape[1]

    def body(c_ref, d_ref, o_ref):
        c = c_ref[...]
        act = (c * _sigmoid(c)).astype(BF16)
        o_ref[0] = lax.dot_general(act, d_ref[0].astype(BF16), TN, preferred_element_type=F32)

    return _pcall(body, name=name, grid=(L,),
                  in_specs=[pl.BlockSpec((B, D), lambda l: (0, 0)), pl.BlockSpec((1, B, n), lambda l: (l, 0, 0))],
                  out_specs=pl.BlockSpec((1, D, n), lambda l: (l, 0, 0)), out_shape=jax.ShapeDtypeStruct((L, D, n), F32),
                  compiler_params=_params("parallel"))(c_all, dmod)


def _slot_sum(x, *, name):
    n, R, W = x.shape
    tr = _pick(R, (256, 128, 64, 32, 16, 8))

    def body(x_ref, o_ref):
        acc = x_ref[0].astype(F32)
        for k in range(1, n):
            acc = acc + x_ref[k].astype(F32)
        o_ref[...] = acc

    return _pcall(body, name=name, grid=(R // tr,), in_specs=[pl.BlockSpec((n, tr, W), lambda i: (0, i, 0))],
                  out_specs=pl.BlockSpec((tr, W), lambda i: (i, 0)), out_shape=jax.ShapeDtypeStruct((R, W), F32),
                  compiler_params=_params("parallel"))(x)


def _adamw_math(g, w, m, v):
    r1, r2 = 1.0 / (1.0 - ADAM_B1 ** ADAM_STEP), 1.0 / (1.0 - ADAM_B2 ** ADAM_STEP)
    m2 = ADAM_B1 * m + (1.0 - ADAM_B1) * g
    v2 = ADAM_B2 * v + (1.0 - ADAM_B2) * (g * g)
    return -ADAM_LR * ((m2 * r1) / (jnp.sqrt(v2 * r2) + ADAM_EPS) + ADAM_WD * w), m2, v2


def _adamw_many(gs, ws, ms, vs, *, name):
    n = len(gs)

    def body(*refs):
        ins, outs = refs[:4 * n], refs[4 * n:]
        for i in range(n):
            d, m2, v2 = _adamw_math(*(ins[j * n + i][...] for j in range(4)))
            outs[i][...], outs[n + i][...], outs[2 * n + i][...] = d, m2, v2

    vm = pl.BlockSpec(memory_space=pltpu.VMEM)
    outs = _pcall(body, name=name, in_specs=[vm] * (4 * n), out_specs=[vm] * (3 * n),
                  out_shape=[jax.ShapeDtypeStruct(a.shape, F32) for _ in range(3) for a in ws],
                  compiler_params=pltpu.CompilerParams(vmem_limit_bytes=VMEM_LIMIT))(*gs, *ws, *ms, *vs)
    return outs[:n], outs[n:2 * n], outs[2 * n:]


def _adamw(gs, w, m, v, *, name):
    n, R, W = gs.shape
    tr = _pick(R, (256, 128, 64, 32, 16, 8))

    def body(g_ref, w_ref, m_ref, v_ref, go_ref, d_ref, mo_ref, vo_ref):
        g = g_ref[0].astype(F32)
        for k in range(1, n):
            g = g + g_ref[k].astype(F32)
        go_ref[...] = g
        d_ref[...], mo_ref[...], vo_ref[...] = _adamw_math(g, w_ref[...], m_ref[...], v_ref[...])

    blk = pl.BlockSpec((tr, W), lambda i: (i, 0))
    o = jax.ShapeDtypeStruct((R, W), F32)
    return _pcall(body, name=name, grid=(R // tr,), in_specs=[pl.BlockSpec((n, tr, W), lambda i: (0, i, 0)), blk, blk, blk],
                  out_specs=[blk] * 4, out_shape=[o] * 4, compiler_params=_params("parallel"))(gs, w, m, v)


ALL_PEERS = tuple(range(1, N_DEV))


def _peer_copies(x_ref, land_ref, send_sems, recv_sems, all_to_all, ks=ALL_PEERS):
    mx, my, mc = lax.axis_index("x"), lax.axis_index("y"), lax.axis_index("c")
    me = 4 * mx + 2 * my + mc
    copies = []
    for n, k in enumerate(ks):
        px, py, pc = mx ^ ((k >> 2) & 1), my ^ ((k >> 1) & 1), mc ^ (k & 1)
        copies.append(pltpu.make_async_remote_copy(
            src_ref=x_ref.at[4 * px + 2 * py + pc] if all_to_all else x_ref, dst_ref=land_ref.at[me],
            send_sem=send_sems.at[n], recv_sem=recv_sems.at[n], device_id=(px, py, pc),
            device_id_type=pl.DeviceIdType.MESH))
    return copies


def _gather_two_level(x, *, name, after=None):
    def body(x_ref, *rest):
        o_ref, send_sems, recv_sems, local_sem = rest[-4:]
        mx, my, mc = lax.axis_index("x"), lax.axis_index("y"), lax.axis_index("c")
        me, sibling = (mx, my, mc), (mx, my, 1 - mc)
        chips = [(1 - mx, my), (mx, 1 - my), (1 - mx, 1 - my)]

        def slot(px, py, pc):
            return o_ref.at[4 * px + 2 * py + pc]

        def copy(k, block, to, src=None):
            return pltpu.make_async_remote_copy(
                src_ref=slot(*block) if src is None else src, dst_ref=slot(*block), send_sem=send_sems.at[k],
                recv_sem=recv_sems.at[k], device_id=to, device_id_type=pl.DeviceIdType.MESH)

        mine = pltpu.make_async_copy(x_ref, slot(*me), local_sem)
        mine.start()
        first = [copy(0, me, sibling, src=x_ref)] + [copy(1 + j, me, (*chip, mc), src=x_ref) for j, chip in enumerate(chips)]
        for cp in first:
            cp.start()
        passed = [copy(4 + j, (*chip, mc), sibling) for j, chip in enumerate(chips)]
        for j, chip in enumerate(chips):
            copy(1 + j, (*chip, mc), me).wait_recv()
            passed[j].start()
        copy(0, sibling, me).wait_recv()
        for j, chip in enumerate(chips):
            copy(4 + j, (*chip, 1 - mc), me).wait_recv()
        for cp in first + passed:
            cp.wait_send()
        mine.wait()

    anyspec = pl.BlockSpec(memory_space=pl.ANY)
    args = [x] if after is None else [x, after]
    return _pcall(body, name=name, in_specs=[anyspec] * len(args), out_specs=anyspec,
                  out_shape=jax.ShapeDtypeStruct((N_DEV,) + tuple(x.shape), x.dtype),
                  scratch_shapes=[pltpu.SemaphoreType.DMA((N_DEV - 1,)), pltpu.SemaphoreType.DMA((N_DEV - 1,)),
                                  pltpu.SemaphoreType.DMA(())])(*args)


_HBM = pl.BlockSpec(memory_space=pltpu.HBM)
_SEM = pl.BlockSpec(memory_space=pltpu.SEMAPHORE)
_EFFECT = pltpu.SideEffectType.DATAFLOW_SIDE_EFFECTING


def _local_copy(x_ref, land_ref, local_sem, all_to_all):
    me = 4 * lax.axis_index("x") + 2 * lax.axis_index("y") + lax.axis_index("c")
    return pltpu.make_async_copy(x_ref.at[me] if all_to_all else x_ref, land_ref.at[me], local_sem)


def _exchange_start(x, *, all_to_all, name, after=None):
    blk = x.shape[1:] if all_to_all else x.shape
    land = lax.empty((N_DEV,) + tuple(blk), x.dtype)
    has_after = after is not None

    def body(*refs):
        x_ref, land_ref = refs[0], refs[1]
        send_sems, recv_sems, local_sem, _, _, token = refs[2 + has_after:]
        _local_copy(x_ref, land_ref, local_sem, all_to_all).start()
        for cp in _peer_copies(x_ref, land_ref, send_sems, recv_sems, all_to_all):
            cp.start()
        token[...] = jnp.zeros_like(token)

    n_sem = pltpu.SemaphoreType.DMA((N_DEV - 1,))
    args = [pltpu.with_memory_space_constraint(x, pltpu.HBM), pltpu.with_memory_space_constraint(land, pltpu.HBM)]
    in_specs = [_HBM, _HBM]
    if has_after:
        args.append(after)
        in_specs.append(pl.BlockSpec(memory_space=pl.ANY))
    send_sems, recv_sems, local_sem, x_thru, land_thru, token = _pcall(
        body, name=name, in_specs=in_specs,
        out_shape=(n_sem, n_sem, pltpu.SemaphoreType.DMA(()), pltpu.HBM(x.shape, x.dtype), pltpu.HBM(land.shape, land.dtype),
                   jax.ShapeDtypeStruct((SUBLANES, LANES), F32)),
        out_specs=(_SEM, _SEM, _SEM, _HBM, _HBM, pl.BlockSpec(memory_space=pltpu.VMEM)), input_output_aliases={0: 3, 1: 4},
        compiler_params=pltpu.CompilerParams(has_side_effects=_EFFECT))(*args)
    return (send_sems, recv_sems, local_sem, x_thru, land_thru, all_to_all), token


def _exchange_wait(state, after, *, name):
    send_sems, recv_sems, local_sem, x_thru, land_thru, all_to_all = state

    def body(x_ref, land_ref, send_sems, recv_sems, local_sem, after_ref, x_dead, landed):
        _local_copy(x_ref, land_ref, local_sem, all_to_all).wait()
        for cp in _peer_copies(x_ref, land_ref, send_sems, recv_sems, all_to_all):
            cp.wait_send()
            cp.wait_recv()

    return _pcall(
        body, name=name, in_specs=(_HBM, _HBM, _SEM, _SEM, _SEM, pl.BlockSpec(memory_space=pl.ANY)),
        out_shape=(pltpu.HBM(x_thru.shape, x_thru.dtype), pltpu.HBM(land_thru.shape, land_thru.dtype)),
        out_specs=(_HBM, _HBM), input_output_aliases={0: 0, 1: 1},
        compiler_params=pltpu.CompilerParams(has_side_effects=_EFFECT))(
            x_thru, land_thru, send_sems, recv_sems, local_sem, after)[1]


def _exchange(x, *, all_to_all, name):
    blk = x.shape[1:] if all_to_all else x.shape

    def body(x_ref, o_ref, send_sems, recv_sems, local_sem):
        me = 4 * lax.axis_index("x") + 2 * lax.axis_index("y") + lax.axis_index("c")
        mine = pltpu.make_async_copy(x_ref.at[me] if all_to_all else x_ref, o_ref.at[me], local_sem)
        mine.start()
        copies = _peer_copies(x_ref, o_ref, send_sems, recv_sems, all_to_all)
        for cp in copies:
            cp.start()
        for cp in copies:
            cp.wait()
        mine.wait()

    anyspec = pl.BlockSpec(memory_space=pl.ANY)
    return _pcall(body, name=name, in_specs=[anyspec], out_specs=anyspec,
                  out_shape=jax.ShapeDtypeStruct((N_DEV,) + tuple(blk), x.dtype),
                  scratch_shapes=[pltpu.SemaphoreType.DMA((N_DEV - 1,)), pltpu.SemaphoreType.DMA((N_DEV - 1,)),
                                  pltpu.SemaphoreType.DMA(())])(x)


PACK_ROWS = 16


def _pack(arrs, width, dtype, lead=0):
    parts, segs, r = [], [], 0
    for a in arrs:
        lshape, shape = a.shape[:lead], a.shape[lead:]
        n = math.prod(shape)
        rows = -(-n // width)
        rows_p = -(-rows // PACK_ROWS) * PACK_ROWS
        if n == rows * width:
            blk = a.reshape(lshape + (rows, width)).astype(dtype)
            parts.append(jnp.pad(blk, [(0, 0)] * lead + [(0, rows_p - rows), (0, 0)]) if rows_p > rows else blk)
        else:
            flat = jnp.pad(a.reshape(lshape + (n,)).astype(dtype), [(0, 0)] * lead + [(0, rows_p * width - n)])
            parts.append(flat.reshape(lshape + (rows_p, width)))
        segs.append((r, n, shape))
        r += rows_p
    return jnp.concatenate(parts, axis=lead), segs


def _unpack(p, segs):
    lshape, width = p.shape[:-2], p.shape[-1]
    outs = []
    for r, n, shape in segs:
        rows = -(-n // width)
        blk = p[..., r:r + rows, :]
        if n != rows * width:
            blk = blk.reshape(lshape + (rows * width,))[..., :n]
        outs.append(blk.reshape(lshape + shape))
    return outs


def _split_cols(a, f_off, h):
    return jnp.concatenate([a[..., :f_off], a[..., f_off + h:]], axis=-1), a[..., f_off:f_off + h]


def _merge_cols(main, f, f_off):
    return jnp.concatenate([main[..., :f_off], f, main[..., f_off:]], axis=-1)


def _pad_to(a, n, axis):
    pad = [(0, 0)] * a.ndim
    pad[axis] = (0, n - a.shape[axis])
    return jnp.pad(a, pad)


def kernel(x, c, w_ada, b_ada, w_in, b_in, conv_a_w, conv_a_b, ln_conv_g, ln_conv_b, w_conv_proj, w_attn_proj, w_mix_out, b_mix_out, ln1_g, ln1_b, w_ffn_up, ffn_conv_w, ffn_conv_b, w_ffn_down, ln2_g, ln2_b, loss_target, m_w_ada, m_b_ada, m_w_in, m_b_in, m_conv_a_w, m_conv_a_b, m_ln_conv_g, m_ln_conv_b, m_w_conv_proj, m_w_attn_proj, m_w_mix_out, m_b_mix_out, m_ln1_g, m_ln1_b, m_w_ffn_up, m_ffn_conv_w, m_ffn_conv_b, m_w_ffn_down, m_ln2_g, m_ln2_b, v_w_ada, v_b_ada, v_w_in, v_b_in, v_conv_a_w, v_conv_a_b, v_ln_conv_g, v_ln_conv_b, v_w_conv_proj, v_w_attn_proj, v_w_mix_out, v_b_mix_out, v_ln1_g, v_ln1_b, v_w_ffn_up, v_ffn_conv_w, v_ffn_conv_b, v_w_ffn_down, v_ln2_g, v_ln2_b):
    L, D = w_ada.shape[0], w_ada.shape[1]
    Bl, S, _ = x.shape
    C, KW, AW = conv_a_b.shape[1], conv_a_w.shape[1], w_attn_proj.shape[1]
    F, KF, n_in_all = ffn_conv_b.shape[1] // 2, ffn_conv_w.shape[1], b_in.shape[1]
    H = n_in_all - 2 * C - 3 * AW - 2 * D
    cfg = Cfg(L=L, Bl=Bl, S=S, D=D, C=C, KW=KW, H=H, Dh=AW // H, F=F, KF=KF)
    T, NM = cfg.T, cfg.NM
    f_off = 2 * C + 3 * AW
    n_ada = w_ada.shape[2]
    me = 4 * lax.axis_index("x") + 2 * lax.axis_index("y") + lax.axis_index("c")

    def my_cols(a, n):
        return lax.dynamic_slice_in_dim(a, me * n, n, axis=a.ndim - 1)

    spack, ssegs = _pack([c, conv_a_w, ffn_conv_w], D, F32)
    c_g, caw_g, fcw_g = _unpack(_exchange(spack, all_to_all=False, name="gather_small"), ssegs)
    c_all = c_g.reshape(N_DEV * Bl, D)
    caw = _pad_to(jnp.moveaxis(caw_g, 0, 2).reshape(L, KW, C), CONV_A_TAPS, 1)
    fcw = _pad_to(jnp.moveaxis(fcw_g, 0, 2).reshape(L, KF, 2 * F), FFN_TAPS, 1)

    mod_part = _ada_fwd(c_all, w_ada, my_cols(b_ada, n_ada)[:, None, :], name="ada_fwd")
    mod_send = jnp.moveaxis(mod_part.reshape(L, N_DEV, Bl, n_ada), 1, 0).reshape(N_DEV, L * Bl, n_ada)
    mod_recv = _exchange(mod_send, all_to_all=True, name="exchange_mod")
    mod = jnp.moveaxis(mod_recv.reshape(N_DEV, L, Bl, n_ada), 0, 2).reshape(L, Bl, 6, 1, D)
    shift1, scale1, gate1, shift2, scale2, gate2 = (mod[:, :, i] for i in range(6))

    big_names = ["w_in", "w_conv_proj", "w_attn_proj", "w_mix_out", "w_ffn_up", "w_ffn_down"]
    transposed = (True, True, True, False, True, False)

    def shard_items(arrs, grp):
        return [arrs[i][l].T if transposed[i] else arrs[i][l] for l, i in grp]

    W = [dict() for _ in range(L)]

    def set_weights(landed, segs, grp):
        for (l, i), a in zip(grp, _unpack(landed, segs)):
            a = a.reshape((-1, a.shape[-1]))
            if i == 0:
                wm_t, wf_t = _split_cols(a.T, f_off, H)
                bm, bf = _split_cols(b_in[l], f_off, H)
                W[l].update(wm_t=wm_t.T, wf_t=_pad_to(wf_t.T, LANES, 0), bm=bm[None], bf=_pad_to(bf, LANES, 0)[None])
            else:
                W[l][("w_cp_t", "w_ap_t", "w_mo", "w_up_t", "w_dn")[i - 1]] = a

    big_w = (w_in, w_conv_proj, w_attn_proj, w_mix_out, w_ffn_up, w_ffn_down)
    w_groups = [[(l, i) for i in range(6)] for l in range(L)]
    pack, segs = _pack(shard_items(big_w, w_groups[0]), D, BF16)
    landed0 = _gather_two_level(pack, name="gather_weights_0", after=mod_recv)
    set_weights(landed0, segs, w_groups[0])
    w_state, token = {}, landed0
    for l in range(1, L):
        pack, segs = _pack(shard_items(big_w, w_groups[l]), D, BF16)
        state, token = _exchange_start(pack, all_to_all=False, name=f"gather_weights_start_{l}", after=token)
        w_state[l] = [state, pack, segs]

    def wait_weights(l, after):
        state, _, segs = w_state[l]
        set_weights(_exchange_wait(state, after, name=f"gather_weights_wait_{l}"), segs, w_groups[l])

    xf = x.reshape(T, D)
    u = _ln_mod_fwd(xf, shift1[0], scale1[0], cfg, name="ln_mod_fwd")
    saved = []
    xin = xf
    for l in range(L):
        w = W[l]
        if l > 0:
            wait_weights(l, u)
        zm = _matmul(u, w["wm_t"], mode="nt", bias=w["bm"], name=f"in_proj_{l}", after=token if l == 0 else None)
        zf = _matmul(u, w["wf_t"], mode="nt", bias=w["bf"], name=f"in_proj_f_{l}")
        a3 = _conv_a_fwd(zm, caw[l], conv_a_b[l][None], ln_conv_g[l][None], ln_conv_b[l][None], cfg, name=f"conv_a_fwd_{l}")
        cum_c = _fgate_fwd(zf, cfg, name=f"fgate_fwd_{l}")
        o, o32, lse = _attn_fwd(zm, cum_c, cfg, name=f"attn_fwd_{l}")
        ya =_matmul(a3, w["w_cp_t"], mode="nt", name=f"conv_proj_{l}")
        yb = _matmul(o, w["w_ap_t"], mode="nt", name=f"attn_proj_{l}")
        mg = _merge_fwd(zm, ya, yb, cfg, name=f"merge_fwd_{l}")
        mix = _matmul(mg, w["w_mo"], mode="nn", bias=b_mix_out[l][None], name=f"mix_out_{l}")
        x1, u2 = _res_ln_fwd(xin, mix, gate1[l], ln1_g[l][None], ln1_b[l][None], cfg, name=f"res_ln1_fwd_{l}",
                             nxt=(shift2[l], scale2[l]))
        h0 = _matmul(u2, w["w_up_t"], mode="nt", name=f"ffn_up_{l}")
        fa, hg, hl = _ffn_conv_fwd(h0, fcw[l], ffn_conv_b[l][None], cfg, name=f"ffn_conv_fwd_{l}")
        ffn = _matmul(fa, w["w_dn"], mode="nn", name=f"ffn_down_{l}")
        saved.append(dict(x=xin, u=u, zm=zm, zf=zf, a3=a3, cum_c=cum_c, o=o, o32=o32, lse=lse, ya=ya, yb=yb, mg=mg, mix=mix,
                          x1=x1, u2=u2, h0=h0, hg=hg, hl=hl, fa=fa, ffn=ffn))
        if l + 1 < L:
            xin, u = _res_ln_fwd(x1, ffn, gate2[l], ln2_g[l][None], ln2_b[l][None], cfg, name=f"res_ln2_fwd_{l}",
                                 nxt=(shift1[l + 1], scale1[l + 1]))
        else:
            xin = _res_ln_fwd(x1, ffn, gate2[l], ln2_g[l][None], ln2_b[l][None], cfg, name=f"res_ln2_fwd_{l}")

    dx, loss_tiles = _loss_grad(xin, loss_target.reshape(T, D), cfg, name="loss_grad")
    loss = lax.psum(0.5 / D * jnp.sum(loss_tiles[:, 0, 0]), ("x", "y", "c"))

    gbig = {}
    g_groups = [[(l, i) for i in range(6)] for l in reversed(range(1, L))] + [[(0, 4), (0, 5)], [(0, 1), (0, 2), (0, 3)], [(0, 0)]]
    g_state = []

    def start_grads(after=None):
        grp = g_groups[len(g_state)]
        send, segs = _pack([gbig[k].reshape((N_DEV, -1, gbig[k].shape[1])) for k in grp], D, BF16, lead=1)
        state, tok = _exchange_start(send, all_to_all=True, name=f"exchange_grads_start_{len(g_state)}", after=after)
        g_state.append((state, send, segs, grp))
        return tok

    gsm = [dict() for _ in range(L)]
    dmods = [None] * L
    token = None
    for l in reversed(range(L)):
        w, s = W[l], saved[l]
        if l == L - 1:
            top = _res_ln_bwd(dx, s["x1"], s["ffn"], gate2[l], ln2_g[l][None], cfg, name=f"res_ln2_bwd_{l}")
        dres2, dffn, dg2, db2, dgate2 = top[:5]
        dfa = _matmul(dffn, w["w_dn"], mode="nt", name=f"d_ffn_act_{l}", after=token)
        gbig[l, 5] = _matmul(s["fa"], dffn, mode="tn", name=f"dw_ffn_down_{l}")
        dh0g, dh0l, dwg, dwl, dcg, dcl = _ffn_conv_bwd(dfa, s["h0"], s["hg"], s["hl"], fcw[l], cfg, name=f"ffn_conv_bwd_{l}")
        du2 = _matmul((dh0g, dh0l), w["w_up_t"], mode="nn", name=f"d_u2_{l}")
        gbig[l, 4] = _matmul((dh0g, dh0l), s["u2"], mode="tn", name=f"dw_ffn_up_{l}")
        token = start_grads() if l == 0 else None
        dres1, dmix, dg1, db1, dgate1, dbmo, dscale2, dshift2 = _res_ln_bwd(
            None, s["x"], s["mix"], gate1[l], ln1_g[l][None], cfg, name=f"res_ln1_bwd_{l}", mod=(du2, s["x1"], scale2[l], dres2))
        dmg = _matmul(dmix, w["w_mo"], mode="nt", name=f"d_merge_{l}", after=token)
        gbig[l, 3] = _matmul(s["mg"], dmix, mode="tn", name=f"dw_mix_out_{l}")
        dya, dyb, dzga, dzgb = _merge_bwd(dmg, s["zm"], s["ya"], s["yb"], cfg, name=f"merge_bwd_{l}")
        gbig[l, 1] = _matmul(dya, s["a3"], mode="tn", name=f"dw_conv_proj_{l}")
        da3 = _matmul(dya, w["w_cp_t"], mode="nn", name=f"d_a3_{l}")
        gbig[l, 2] = _matmul(dyb, s["o"], mode="tn", name=f"dw_attn_proj_{l}")
        token = start_grads() if l == 0 else None
        do = _matmul(dyb, w["w_ap_t"], mode="nn", out_dtype=BF16, name=f"d_o_{l}", after=token)
        dq, dk, dv, dcum_c = _attn_bwd(s["zm"], s["cum_c"], s["o32"], do, s["lse"], cfg, name=f"attn_bwd_{l}")
        dzf = _fgate_bwd(dcum_c, s["zf"], cfg, name=f"fgate_bwd_{l}")
        dzglu, dcaw, dcab, dlcg, dlcb = _conv_a_bwd(da3, s["zm"], caw[l], conv_a_b[l][None], ln_conv_g[l][None],
                                                    ln_conv_b[l][None], cfg, name=f"conv_a_bwd_{l}")
        dzm = jnp.concatenate([dzglu, dq, dk, dv, dzga, dzgb], axis=1)
        du1 = _matmul(dzf, w["wf_t"], mode="nn", name=f"d_u1_f_{l}")
        du1 = _matmul(dzm, w["wm_t"], mode="nn", add=du1, name=f"d_u1_{l}")
        dwm_t = _matmul(dzm, s["u"], mode="tn", name=f"dw_in_{l}")
        dwf_t = _matmul(dzf, s["u"], mode="tn", name=f"dw_in_f_{l}")
        gbig[l, 0] = _merge_cols(dwm_t.T, dwf_t[:H].T, f_off).T
        token = start_grads() if l > 0 else None
        dbm, dbf = _colsum(dzm, name=f"db_in_{l}"), _colsum(dzf, name=f"db_in_f_{l}")
        if l > 0:
            below = saved[l - 1]
            top = _res_ln_bwd(None, below["x1"], below["ffn"], gate2[l - 1], ln2_g[l - 1][None], cfg,
                              name=f"res_ln2_bwd_{l - 1}", mod=(du1, s["x"], scale1[l], dres1))
            dscale1, dshift1 = top[6], top[7]
        else:
            dx, dscale1, dshift1 = _ln_mod_bwd(du1, s["x"], scale1[l], dres1, cfg, name=f"ln_mod1_bwd_{l}")
        dmods[l] = jnp.concatenate([dshift1, dscale1, dgate1, dshift2, dscale2, dgate2], axis=1).reshape(Bl, 6 * D)
        gsm[l] = dict(b_in=_merge_cols(dbm[0], dbf[0, :H], f_off), conv_a_b=dcab[0], ln_conv_g=dlcg[0], ln_conv_b=dlcb[0],
                      b_mix_out=dbmo[0], ln1_g=dg1[0], ln1_b=db1[0], ffn_conv_b=jnp.concatenate([dcg[0], dcl[0]]),
                      ln2_g=dg2[0], ln2_b=db2[0], conv_a_w=dcaw[:KW], ffn_conv_w=jnp.concatenate([dwg[:KF], dwl[:KF]], axis=1))
    grad_x = dx.reshape(Bl, S, D)

    small_names = ["b_in", "conv_a_b", "ln_conv_g", "ln_conv_b", "b_mix_out", "ln1_g", "ln1_b", "ffn_conv_b", "ln2_g", "ln2_b",
                   "conv_a_w", "ffn_conv_w"]
    gs_list = [jnp.stack(dmods)] + [jnp.stack([gsm[l][n] for l in range(L)]) for n in small_names]
    gspack, gssegs = _pack(gs_list, D, F32)
    gs_all = _gather_two_level(gspack, name="gather_small_grads")
    start_grads(after=gs_all)
    dmod_all = jnp.moveaxis(_unpack(gs_all, gssegs)[0], 0, 1).reshape(L, N_DEV * Bl, 6 * D)
    g_small = dict(zip(small_names, _unpack(_slot_sum(gs_all, name="sum_small_grads"), gssegs)[1:]))
    g_small["conv_a_w"] = my_cols(g_small["conv_a_w"], C // N_DEV)
    g_small["ffn_conv_w"] = my_cols(g_small["ffn_conv_w"], 2 * F // N_DEV)
    g_small["w_ada"] = _ada_bwd(c_all, my_cols(dmod_all, n_ada), name="ada_bwd")
    g_small["b_ada"] = jnp.stack([_colsum(dmod_all[l], name=f"db_ada_{l}")[0] for l in range(L)])

    given = dict(w_in=(w_in, m_w_in, v_w_in), w_conv_proj=(w_conv_proj, m_w_conv_proj, v_w_conv_proj),
                 w_attn_proj=(w_attn_proj, m_w_attn_proj, v_w_attn_proj), w_mix_out=(w_mix_out, m_w_mix_out, v_w_mix_out),
                 w_ffn_up=(w_ffn_up, m_w_ffn_up, v_w_ffn_up), w_ffn_down=(w_ffn_down, m_w_ffn_down, v_w_ffn_down),
                 w_ada=(w_ada, m_w_ada, v_w_ada), b_ada=(b_ada, m_b_ada, v_b_ada), b_in=(b_in, m_b_in, v_b_in),
                 conv_a_w=(conv_a_w, m_conv_a_w, v_conv_a_w), conv_a_b=(conv_a_b, m_conv_a_b, v_conv_a_b),
                 ln_conv_g=(ln_conv_g, m_ln_conv_g, v_ln_conv_g), ln_conv_b=(ln_conv_b, m_ln_conv_b, v_ln_conv_b),
                 b_mix_out=(b_mix_out, m_b_mix_out, v_b_mix_out), ln1_g=(ln1_g, m_ln1_g, v_ln1_g), ln1_b=(ln1_b, m_ln1_b, v_ln1_b),
                 ffn_conv_w=(ffn_conv_w, m_ffn_conv_w, v_ffn_conv_w), ffn_conv_b=(ffn_conv_b, m_ffn_conv_b, v_ffn_conv_b),
                 ln2_g=(ln2_g, m_ln2_g, v_ln2_g), ln2_b=(ln2_b, m_ln2_b, v_ln2_b))
    res, kinds = {}, ("grad", "delta", "new_m", "new_v")
    loc_names = ["b_ada"] + small_names
    deltas, new_ms, new_vs = _adamw_many([g_small[n] for n in loc_names], *([given[n][j] for n in loc_names] for j in range(3)),
                                         name="adamw_small")
    for n, d, m2, v2 in zip(loc_names, deltas, new_ms, new_vs):
        res["grad", n], res["delta", n], res["new_m", n], res["new_v", n] = g_small[n], d, m2, v2
    rows_ada = (L * D * n_ada // D, D)
    outs = _adamw(g_small["w_ada"].reshape((1,) + rows_ada), *(a.reshape(rows_ada) for a in given["w_ada"]), name="adamw_w_ada")
    for kind, a in zip(kinds, outs):
        res[kind, "w_ada"] = a.reshape(w_ada.shape)

    big_parts = {}
    after = outs[0]
    for gi, (state, send, segs, grp) in enumerate(g_state):
        landed = _exchange_wait(state, after, name=f"exchange_grads_wait_{gi}")
        wmv = [_pack(shard_items([given[n][j] for n in big_names], grp), D, F32)[0] for j in range(3)]
        outs = _adamw(landed, *wmv, name=f"adamw_big_{gi}")
        for kind, packed in zip(kinds, outs):
            for (l, i), a in zip(grp, _unpack(packed, segs)):
                big_parts[kind, l, i] = a.T if transposed[i] else a
        after = outs[0]
    for kind in kinds:
        for i, n in enumerate(big_names):
            res[kind, n] = jnp.stack([big_parts[kind, l, i] for l in range(L)])

    order = ["w_ada", "b_ada", "w_in", "b_in", "conv_a_w", "conv_a_b", "ln_conv_g", "ln_conv_b", "w_conv_proj", "w_attn_proj",
             "w_mix_out", "b_mix_out", "ln1_g", "ln1_b", "w_ffn_up", "ffn_conv_w", "ffn_conv_b", "w_ffn_down", "ln2_g", "ln2_b"]
    return (loss, grad_x, *[res[k, n] for k in ("grad", "delta", "new_m", "new_v") for n in order])
```

```python
import functools
import math
from typing import NamedTuple

import jax
import jax.numpy as jnp
from jax import lax
from jax.experimental import pallas as pl
from jax.experimental.pallas import tpu as pltpu

F32, BF16 = jnp.float32, jnp.bfloat16
LN_EPS = 1e-5
ADAM_LR, ADAM_B1, ADAM_B2, ADAM_EPS, ADAM_WD, ADAM_STEP = 0.001, 0.9, 0.999, 1e-08, 0.01, 10
N_DEV = 8
LANES = 128
VMEM_LIMIT = 56 * 1024 * 1024
NEG = -1e30
NT = (((1,), (1,)), ((), ()))
TN = (((0,), (0,)), ((), ()))


class Cfg(NamedTuple):
    L: int
    Bl: int
    S: int
    D: int
    C: int
    KW: int
    H: int
    Dh: int
    F: int
    KF: int

    @property
    def T(self): return self.Bl * self.S
    @property
    def AW(self): return self.H * self.Dh
    @property
    def NM(self): return 2 * self.C + 3 * self.AW + 2 * self.D
    @property
    def q_off(self): return 2 * self.C
    @property
    def g_off(self): return 2 * self.C + 3 * self.AW
    @property
    def alpha(self): return (2.0 * self.L) ** 0.25


def _pcall(body, **kw):
    return pl.pallas_call(body, **kw)


def _params(*sem):
    return pltpu.CompilerParams(dimension_semantics=sem, vmem_limit_bytes=VMEM_LIMIT)


def _pick(n, prefs):
    for p in prefs:
        if n % p == 0:
            return p
    return n


def _sigmoid(x):
    return 1.0 / (1.0 + jnp.exp(-x))


def _ln_stats(x):
    mu = jnp.mean(x, axis=-1, keepdims=True)
    xc = x - mu
    var = jnp.mean(xc * xc, axis=-1, keepdims=True)
    rstd = lax.rsqrt(var + LN_EPS)
    return xc * rstd, rstd


def _ln_bwd(dxh, xh, rstd):
    return rstd * (dxh - jnp.mean(dxh, axis=-1, keepdims=True) - xh * jnp.mean(dxh * xh, axis=-1, keepdims=True))


def _matmul(a, b, *, mode, name, bias=None, add=None, out_dtype=F32, tm=None, tn=None, tk=None, after=None):
    parts = tuple(a) if isinstance(a, (tuple, list)) else (a,)
    P = len(parts)
    if mode == "tn":
        K, Mp = parts[0].shape
        M, Kp = P * Mp, K
    else:
        M, Kp = parts[0].shape
        K, Mp = P * Kp, M
    N = b.shape[0] if mode == "nt" else b.shape[1]
    lane_tiles = (1536, 1408, 1024, 768, 512, 256, 128)
    tm = tm or _pick(Mp, lane_tiles if mode == "tn" else (1024, 512, 256, 128, 64, 32, 16, 8))
    tn = tn or _pick(N, lane_tiles)
    tk = tk or _pick(Kp, (1024, 512, 256, 128) if mode == "tn" else lane_tiles)
    nk = K // tk
    per = Mp // tm if mode == "tn" else Kp // tk
    dn = {"nn": (((1,), (0,)), ((), ())), "nt": NT, "tn": TN}[mode]
    has_bias, has_add, has_after = bias is not None, add is not None, after is not None

    def body(*refs):
        a_refs, b_ref = refs[:P], refs[P]
        pos = P + 1
        bias_ref = refs[pos] if has_bias else None
        pos += has_bias
        add_ref = refs[pos] if has_add else None
        pos += has_add + has_after
        o_ref = refs[pos]
        acc_ref = refs[pos + 1] if nk > 1 else None
        k = pl.program_id(2)

        def finish(acc):
            if has_bias:
                acc = acc + bias_ref[...]
            if has_add:
                acc = acc + add_ref[...]
            o_ref[...] = acc.astype(out_dtype)

        def accumulate(a_ref):
            part = lax.dot_general(a_ref[...], b_ref[...], dn, preferred_element_type=F32)
            if nk == 1:
                finish(part)
            else:
                @pl.when(k == 0)
                def _():
                    acc_ref[...] = part

                @pl.when(k > 0)
                def _():
                    acc_ref[...] += part

        if P == 1:
            accumulate(a_refs[0])
        else:
            step = pl.program_id(0 if mode == "tn" else 2)
            for p in range(P):
                pl.when(step // per == p)(functools.partial(accumulate, a_refs[p]))
        if nk > 1:
            @pl.when(k == nk - 1)
            def _():
                finish(acc_ref[...])

    def a_spec(p):
        if mode == "tn":
            return pl.BlockSpec((tk, tm), lambda i, j, k: (k, jnp.clip(i - p * per, 0, per - 1)))
        return pl.BlockSpec((tm, tk), lambda i, j, k: (i, jnp.clip(k - p * per, 0, per - 1)))

    b_spec = pl.BlockSpec((tn, tk), lambda i, j, k: (j, k)) if mode == "nt" else pl.BlockSpec((tk, tn), lambda i, j, k: (k, j))
    in_specs, args = [a_spec(p) for p in range(P)] + [b_spec], list(parts) + [b]
    if has_bias:
        in_specs.append(pl.BlockSpec((1, tn), lambda i, j, k: (0, j)))
        args.append(bias)
    if has_add:
        in_specs.append(pl.BlockSpec((tm, tn), lambda i, j, k: (i, j)))
        args.append(add)
    if has_after:
        in_specs.append(pl.BlockSpec(memory_space=pl.ANY))
        args.append(after)
    return _pcall(
        body, name=name, grid=(M // tm, N // tn, nk), in_specs=in_specs,
        out_specs=pl.BlockSpec((tm, tn), lambda i, j, k: (i, j)),
        out_shape=jax.ShapeDtypeStruct((M, N), out_dtype),
        scratch_shapes=[pltpu.VMEM((tm, tn), F32)] if nk > 1 else [],
        compiler_params=_params("parallel", "parallel", "arbitrary"),
    )(*args)


def _colsum(x, *, name):
    T, N = x.shape
    tr = _pick(T, (512, 256, 128, 64, 32, 16))
    tc = _pick(N, (1536, 1024, 512, 256, 128))

    def body(x_ref, o_ref):
        @pl.when(pl.program_id(1) == 0)
        def _():
            o_ref[...] = jnp.zeros_like(o_ref)

        o_ref[...] += jnp.sum(x_ref[...].astype(F32), axis=0, keepdims=True)

    return _pcall(body, name=name, grid=(N // tc, T // tr), in_specs=[pl.BlockSpec((tr, tc), lambda j, i: (i, j))],
                  out_specs=pl.BlockSpec((1, tc), lambda j, i: (0, j)), out_shape=jax.ShapeDtypeStruct((1, N), F32),
                  compiler_params=_params("parallel", "arbitrary"))(x)


def _row_tile(cfg):
    return _pick(cfg.S, (512, 256, 128, 64, 32, 16, 8))


def _ln_mod_fwd(x, shift, scale, cfg, *, name):
    tr = _row_tile(cfg)
    tpb = cfg.S // tr

    def body(x_ref, sh_ref, sc_ref, u_ref):
        xh, _ = _ln_stats(x_ref[...])
        u_ref[...] = (xh * (1.0 + sc_ref[0]) + sh_ref[0]).astype(BF16)

    row = pl.BlockSpec((tr, cfg.D), lambda i: (i, 0))
    per_b = pl.BlockSpec((1, 1, cfg.D), lambda i: (i // tpb, 0, 0))
    return _pcall(body, name=name, grid=(cfg.T // tr,), in_specs=[row, per_b, per_b], out_specs=row,
                  out_shape=jax.ShapeDtypeStruct((cfg.T, cfg.D), BF16), compiler_params=_params("parallel"))(x, shift, scale)


def _res_ln_fwd(xin, br, gate, g, b, cfg, *, name, nxt=None):
    tr = _row_tile(cfg)
    tpb = cfg.S // tr
    alpha = cfg.alpha

    def body(*refs):
        x_ref, br_ref, gt_ref, g_ref, b_ref = refs[:5]
        r = alpha * x_ref[...] + (1.0 + gt_ref[0]) * br_ref[...]
        xh, _ = _ln_stats(r)
        xo = xh * g_ref[...] + b_ref[...]
        if nxt is None:
            refs[5][...] = xo
        else:
            sh_ref, sc_ref, xo_ref, u_ref = refs[5:]
            xo_ref[...] = xo
            uh, _ = _ln_stats(xo)
            u_ref[...] = (uh * (1.0 + sc_ref[0]) + sh_ref[0]).astype(BF16)

    row = pl.BlockSpec((tr, cfg.D), lambda i: (i, 0))
    per_b = pl.BlockSpec((1, 1, cfg.D), lambda i: (i // tpb, 0, 0))
    vec = pl.BlockSpec((1, cfg.D), lambda i: (0, 0))
    in_specs, args = [row, row, per_b, vec, vec], [xin, br, gate, g, b]
    out_specs, out_shape = row, jax.ShapeDtypeStruct((cfg.T, cfg.D), F32)
    if nxt is not None:
        in_specs += [per_b, per_b]
        args += list(nxt)
        out_specs = [row, row]
        out_shape = [out_shape, jax.ShapeDtypeStruct((cfg.T, cfg.D), BF16)]
    return _pcall(body, name=name, grid=(cfg.T // tr,), in_specs=in_specs, out_specs=out_specs, out_shape=out_shape,
                  compiler_params=_params("parallel"))(*args)


def _loss_grad(y, tgt, cfg, *, name):
    tr = _row_tile(cfg)
    nt = cfg.T // tr
    inv_d = 1.0 / cfg.D

    def body(y_ref, t_ref, dy_ref, ls_ref):
        e = y_ref[...] - t_ref[...]
        dy_ref[...] = e * inv_d
        ls_ref[...] = jnp.full((1, 1, LANES), jnp.sum(e * e), F32)

    row = pl.BlockSpec((tr, cfg.D), lambda i: (i, 0))
    return _pcall(body, name=name, grid=(nt,), in_specs=[row, row],
                  out_specs=[row, pl.BlockSpec((1, 1, LANES), lambda i: (i, 0, 0))],
                  out_shape=[jax.ShapeDtypeStruct((cfg.T, cfg.D), F32), jax.ShapeDtypeStruct((nt, 1, LANES), F32)],
                  compiler_params=_params("parallel"))(y, tgt)


def _res_ln_bwd(dy, xin, br, gate, g, cfg, *, name, mod=None):
    tr = _row_tile(cfg)
    tpb = cfg.S // tr
    alpha = cfg.alpha
    fused = mod is not None

    def body(*refs):
        if fused:
            du_ref, xa_ref, sc_ref, dres_ref, x_ref, br_ref, gt_ref, g_ref = refs[:8]
            dx_ref, dbr_ref, dg_ref, db_ref, dgt_ref, dbs_ref, dsc_ref, dsh_ref = refs[8:]
        else:
            dy_ref, x_ref, br_ref, gt_ref, g_ref, dx_ref, dbr_ref, dg_ref, db_ref, dgt_ref, dbs_ref = refs
        i = pl.program_id(0)

        @pl.when(i == 0)
        def _():
            dg_ref[...] = jnp.zeros_like(dg_ref)
            db_ref[...] = jnp.zeros_like(db_ref)
            dbs_ref[...] = jnp.zeros_like(dbs_ref)

        @pl.when(i % tpb == 0)
        def _():
            dgt_ref[...] = jnp.zeros_like(dgt_ref)
            if fused:
                dsc_ref[...] = jnp.zeros_like(dsc_ref)
                dsh_ref[...] = jnp.zeros_like(dsh_ref)

        if fused:
            du = du_ref[...]
            ah, arstd = _ln_stats(xa_ref[...])
            dsc_ref[0] += jnp.sum(du * ah, axis=0, keepdims=True)
            dsh_ref[0] += jnp.sum(du, axis=0, keepdims=True)
            dy = _ln_bwd(du * (1.0 + sc_ref[0]), ah, arstd) + dres_ref[...]
        else:
            dy = dy_ref[...]
        brv, one_gate = br_ref[...], 1.0 + gt_ref[0]
        xh, rstd = _ln_stats(alpha * x_ref[...] + one_gate * brv)
        dg_ref[...] += jnp.sum(dy * xh, axis=0, keepdims=True)
        db_ref[...] += jnp.sum(dy, axis=0, keepdims=True)
        dr = _ln_bwd(dy * g_ref[...], xh, rstd)
        dx_ref[...] = alpha * dr
        dbr = one_gate * dr
        dbr_ref[...] = dbr.astype(BF16)
        dbs_ref[...] += jnp.sum(dbr, axis=0, keepdims=True)
        dgt_ref[0] += jnp.sum(dr * brv, axis=0, keepdims=True)

    row = pl.BlockSpec((tr, cfg.D), lambda i: (i, 0))
    per_b = pl.BlockSpec((1, 1, cfg.D), lambda i: (i // tpb, 0, 0))
    vec = pl.BlockSpec((1, cfg.D), lambda i: (0, 0))
    vs = jax.ShapeDtypeStruct((1, cfg.D), F32)
    bs = jax.ShapeDtypeStruct((cfg.Bl, 1, cfg.D), F32)
    in_specs, args = [row, row, row, per_b, vec], [dy, xin, br, gate, g]
    out_specs = [row, row, vec, vec, per_b, vec]
    out_shape = [jax.ShapeDtypeStruct((cfg.T, cfg.D), F32), jax.ShapeDtypeStruct((cfg.T, cfg.D), BF16), vs, vs, bs, vs]
    if fused:
        in_specs, args = [row, row, per_b, row] + in_specs[1:], list(mod) + args[1:]
        out_specs, out_shape = out_specs + [per_b, per_b], out_shape + [bs, bs]
    return _pcall(body, name=name, grid=(cfg.T // tr,), in_specs=in_specs, out_specs=out_specs, out_shape=out_shape,
                  compiler_params=_params("arbitrary"))(*args)


def _ln_mod_bwd(du, xin, scale, dres, cfg, *, name):
    tr = _row_tile(cfg)
    tpb = cfg.S // tr

    def body(du_ref, x_ref, sc_ref, dres_ref, dx_ref, dsc_ref, dsh_ref):
        @pl.when(pl.program_id(0) % tpb == 0)
        def _():
            dsc_ref[...] = jnp.zeros_like(dsc_ref)
            dsh_ref[...] = jnp.zeros_like(dsh_ref)

        du = du_ref[...]
        xh, rstd = _ln_stats(x_ref[...])
        dsc_ref[0] += jnp.sum(du * xh, axis=0, keepdims=True)
        dsh_ref[0] += jnp.sum(du, axis=0, keepdims=True)
        dx_ref[...] = _ln_bwd(du * (1.0 + sc_ref[0]), xh, rstd) + dres_ref[...]

    row = pl.BlockSpec((tr, cfg.D), lambda i: (i, 0))
    per_b = pl.BlockSpec((1, 1, cfg.D), lambda i: (i // tpb, 0, 0))
    bs = jax.ShapeDtypeStruct((cfg.Bl, 1, cfg.D), F32)
    return _pcall(body, name=name, grid=(cfg.T // tr,), in_specs=[row, row, per_b, row], out_specs=[row, per_b, per_b],
                  out_shape=[jax.ShapeDtypeStruct((cfg.T, cfg.D), F32), bs, bs],
                  compiler_params=_params("arbitrary"))(du, xin, scale, dres)


def _merge_tiles(cfg):
    tr = _pick(cfg.T, (512, 256, 128, 64, 32, 16))
    tc = _pick(math.gcd(cfg.g_off, cfg.D), (512, 256, 128))
    return tr, tc


def _merge_fwd(zm, ya, yb, cfg, *, name):
    tr, tc = _merge_tiles(cfg)
    ga0, gb0 = cfg.g_off // tc, (cfg.g_off + cfg.D) // tc

    def body(ga_ref, gb_ref, ya_ref, yb_ref, m_ref):
        m_ref[...] = (_sigmoid(ga_ref[...]) * ya_ref[...] + _sigmoid(gb_ref[...]) * yb_ref[...]).astype(BF16)

    blk = pl.BlockSpec((tr, tc), lambda i, j: (i, j))
    return _pcall(body, name=name, grid=(cfg.T // tr, cfg.D // tc),
                  in_specs=[pl.BlockSpec((tr, tc), lambda i, j: (i, ga0 + j)), pl.BlockSpec((tr, tc), lambda i, j: (i, gb0 + j)), blk, blk],
                  out_specs=blk, out_shape=jax.ShapeDtypeStruct((cfg.T, cfg.D), BF16),
                  compiler_params=_params("parallel", "parallel"))(zm, zm, ya, yb)


def _merge_bwd(dm, zm, ya, yb, cfg, *, name):
    tr, tc = _merge_tiles(cfg)
    ga0, gb0 = cfg.g_off // tc, (cfg.g_off + cfg.D) // tc

    def body(dm_ref, ga_ref, gb_ref, ya_ref, yb_ref, dya_ref, dyb_ref, dga_ref, dgb_ref):
        dm = dm_ref[...]
        ga, gb = _sigmoid(ga_ref[...]), _sigmoid(gb_ref[...])
        dya_ref[...] = (dm * ga).astype(BF16)
        dyb_ref[...] = (dm * gb).astype(BF16)
        dga_ref[...] = (dm * ya_ref[...] * ga * (1.0 - ga)).astype(BF16)
        dgb_ref[...] = (dm * yb_ref[...] * gb * (1.0 - gb)).astype(BF16)

    blk = pl.BlockSpec((tr, tc), lambda i, j: (i, j))
    o = jax.ShapeDtypeStruct((cfg.T, cfg.D), BF16)
    return _pcall(body, name=name, grid=(cfg.T // tr, cfg.D // tc),
                  in_specs=[blk, pl.BlockSpec((tr, tc), lambda i, j: (i, ga0 + j)), pl.BlockSpec((tr, tc), lambda i, j: (i, gb0 + j)), blk, blk],
                  out_specs=[blk] * 4, out_shape=[o] * 4, compiler_params=_params("parallel", "parallel"))(dm, zm, zm, ya, yb)


CONV_A_HALO = 32
CONV_A_CHUNK = 32
CONV_A_TAPS = 32
FFN_TAPS = 8


SUBLANES = 8


def _conv_a_tile(cfg):
    assert cfg.KW - 1 <= CONV_A_HALO
    return _pick(cfg.S, (256, 128, 64, 32))


def _shift_copies(src_s, sh_s):
    rows = src_s.shape[0] - SUBLANES
    for b in range(1, SUBLANES):
        sh_s[b - 1, :, :] = src_s[b:b + rows, :]


def _rows(src_s, sh_s, start, n):
    a, b = divmod(start, SUBLANES)
    return src_s[start:start + n, :] if b == 0 else sh_s[b - 1, SUBLANES * a:SUBLANES * a + n, :]


def _conv_a_fwd(zm, w, cb, g, b, cfg, *, name):
    C, KW, HALO, CH = cfg.C, cfg.KW, CONV_A_HALO, CONV_A_CHUNK
    ts = _conv_a_tile(cfg)
    tpb = cfg.S // ts
    lead = HALO - (KW - 1)

    def body(z_ref, zp_ref, w_ref, cb_ref, g_ref, b_ref, o_ref, a0_s, a0_sh):
        first = pl.program_id(0) % tpb == 0
        prev = zp_ref[:, :C] * _sigmoid(zp_ref[:, C:])
        a0_s[0:HALO, :] = jnp.where(first, 0.0, prev)
        a0_s[HALO:HALO + ts, :] = z_ref[:, :C] * _sigmoid(z_ref[:, C:])
        _shift_copies(a0_s, a0_sh)
        for r0 in range(0, ts, CH):
            acc = jnp.zeros((CH, C), F32)
            for k in range(KW):
                acc = acc + w_ref[k:k + 1, :] * _rows(a0_s, a0_sh, r0 + lead + k, CH)
            xh, _ = _ln_stats(acc + cb_ref[...])
            a2 = xh * g_ref[...] + b_ref[...]
            o_ref[r0:r0 + CH, :] = (a2 * _sigmoid(a2)).astype(BF16)

    hb = ts // HALO
    vec = pl.BlockSpec((1, C), lambda i: (0, 0))
    return _pcall(body, name=name, grid=(cfg.T // ts,),
                  in_specs=[pl.BlockSpec((ts, 2 * C), lambda i: (i, 0)),
                            pl.BlockSpec((HALO, 2 * C), lambda i: (jnp.maximum(i * hb - 1, 0), 0)),
                            pl.BlockSpec((CONV_A_TAPS, C), lambda i: (0, 0)), vec, vec, vec],
                  out_specs=pl.BlockSpec((ts, C), lambda i: (i, 0)), out_shape=jax.ShapeDtypeStruct((cfg.T, C), BF16),
                  scratch_shapes=[pltpu.VMEM((HALO + ts, C), F32), pltpu.VMEM((SUBLANES - 1, HALO + ts - SUBLANES, C), F32)],
                  compiler_params=_params("parallel"))(zm, zm, w, cb, g, b)


def _conv_a_bwd(da3, zm, w, cb, g, b, cfg, *, name):
    C, KW, HALO, CH = cfg.C, cfg.KW, CONV_A_HALO, CONV_A_CHUNK
    ts = _conv_a_tile(cfg)
    tpb = cfg.S // ts
    nt = cfg.T // ts
    lead = HALO - (KW - 1)
    ext = ts + HALO

    def body(z_ref, zp_ref, zn_ref, d_ref, dn_ref, w_ref, cb_ref, g_ref, b_ref,
             dz_ref, dw_ref, dcb_ref, dg_ref, db_ref, a0_s, d3_s, da1_s, a0_sh, da1_sh):
        i = pl.program_id(0)
        first, last = i % tpb == 0, i % tpb == tpb - 1

        @pl.when(i == 0)
        def _():
            dw_ref[...] = jnp.zeros_like(dw_ref)
            dcb_ref[...] = jnp.zeros_like(dcb_ref)
            dg_ref[...] = jnp.zeros_like(dg_ref)
            db_ref[...] = jnp.zeros_like(db_ref)

        a0_s[0:HALO, :] = jnp.where(first, 0.0, zp_ref[:, :C] * _sigmoid(zp_ref[:, C:]))
        a0_s[HALO:HALO + ts, :] = z_ref[:, :C] * _sigmoid(z_ref[:, C:])
        a0_s[HALO + ts:HALO + ext, :] = zn_ref[:, :C] * _sigmoid(zn_ref[:, C:])
        d3_s[0:ts, :] = d_ref[...]
        d3_s[ts:ext, :] = jnp.where(last, 0.0, dn_ref[...])
        _shift_copies(a0_s, a0_sh)
        dcb, dg, db = jnp.zeros((1, C), F32), jnp.zeros((1, C), F32), jnp.zeros((1, C), F32)
        for r0 in range(0, ext, CH):
            acc = jnp.zeros((CH, C), F32)
            for k in range(KW):
                acc = acc + w_ref[k:k + 1, :] * _rows(a0_s, a0_sh, r0 + lead + k, CH)
            xh, rstd = _ln_stats(acc + cb_ref[...])
            a2 = xh * g_ref[...] + b_ref[...]
            sg = _sigmoid(a2)
            da2 = d3_s[r0:r0 + CH, :] * (sg * (1.0 + a2 * (1.0 - sg)))
            da1 = _ln_bwd(da2 * g_ref[...], xh, rstd)
            da1_s[r0:r0 + CH, :] = da1
            if r0 < ts:
                dg = dg + jnp.sum(da2 * xh, axis=0, keepdims=True)
                db = db + jnp.sum(da2, axis=0, keepdims=True)
                dcb = dcb + jnp.sum(da1, axis=0, keepdims=True)
        dg_ref[...] += dg
        db_ref[...] += db
        dcb_ref[...] += dcb
        _shift_copies(da1_s, da1_sh)
        for k in range(KW):
            dwk = jnp.zeros((CH, C), F32)
            for r0 in range(0, ts, CH):
                dwk = dwk + da1_s[r0:r0 + CH, :] * _rows(a0_s, a0_sh, r0 + lead + k, CH)
            dw_ref[k:k + 1, :] += jnp.sum(dwk, axis=0, keepdims=True)
        for r0 in range(0, ts, CH):
            da0 = jnp.zeros((CH, C), F32)
            for k in range(KW):
                da0 = da0 + w_ref[k:k + 1, :] * _rows(da1_s, da1_sh, r0 + KW - 1 - k, CH)
            val, sg = z_ref[r0:r0 + CH, :C], _sigmoid(z_ref[r0:r0 + CH, C:])
            dz_ref[r0:r0 + CH, :C] = (da0 * sg).astype(BF16)
            dz_ref[r0:r0 + CH, C:] = (da0 * val * sg * (1.0 - sg)).astype(BF16)

    hb = ts // HALO
    nhb = cfg.T // HALO
    vec = pl.BlockSpec((1, C), lambda i: (0, 0))
    vs = jax.ShapeDtypeStruct((1, C), F32)
    return _pcall(body, name=name, grid=(nt,),
                  in_specs=[pl.BlockSpec((ts, 2 * C), lambda i: (i, 0)),
                            pl.BlockSpec((HALO, 2 * C), lambda i: (jnp.maximum(i * hb - 1, 0), 0)),
                            pl.BlockSpec((HALO, 2 * C), lambda i: (jnp.minimum((i + 1) * hb, nhb - 1), 0)),
                            pl.BlockSpec((ts, C), lambda i: (i, 0)),
                            pl.BlockSpec((HALO, C), lambda i: (jnp.minimum((i + 1) * hb, nhb - 1), 0)),
                            pl.BlockSpec((CONV_A_TAPS, C), lambda i: (0, 0)), vec, vec, vec],
                  out_specs=[pl.BlockSpec((ts, 2 * C), lambda i: (i, 0)), pl.BlockSpec((CONV_A_TAPS, C), lambda i: (0, 0)), vec, vec, vec],
                  out_shape=[jax.ShapeDtypeStruct((cfg.T, 2 * C), BF16), jax.ShapeDtypeStruct((CONV_A_TAPS, C), F32), vs, vs, vs],
                  scratch_shapes=[pltpu.VMEM((HALO + ext, C), F32), pltpu.VMEM((ext, C), F32), pltpu.VMEM((ext, C), F32),
                                  pltpu.VMEM((SUBLANES - 1, HALO + ext - SUBLANES, C), F32),
                                  pltpu.VMEM((SUBLANES - 1, ext - SUBLANES, C), F32)],
                  compiler_params=_params("arbitrary"))(zm, zm, zm, da3, da3, w, cb, g, b)


def _cum_tile(cfg):
    return _pick(cfg.S, (256, 128, 64, 32, 16, 8))


def _fgate_fwd(zf, cfg, *, name):
    tc = _cum_tile(cfg)
    tpb = cfg.S // tc
    hp = _attn_tiles(cfg)[2]
    nb = cfg.H // hp

    def body(z_ref, o_ref, carry):
        @pl.when(pl.program_id(0) % tpb == 0)
        def _():
            carry[...] = jnp.zeros_like(carry)

        z = z_ref[...]
        logf = jnp.minimum(z, 0.0) - jnp.log(1.0 + jnp.exp(-jnp.abs(z)))
        tri = (lax.broadcasted_iota(jnp.int32, (tc, tc), 0) >= lax.broadcasted_iota(jnp.int32, (tc, tc), 1)).astype(F32)
        cum = jnp.dot(tri, logf, precision=lax.Precision.HIGHEST, preferred_element_type=F32) + carry[...]
        carry[...] = cum[tc - 1:tc, :]
        o_ref[0] = cum
        for b in range(1, nb):
            o_ref[b] = pltpu.roll(cum, LANES - hp * b, axis=1)

    return _pcall(body, name=name, grid=(cfg.T // tc,), in_specs=[pl.BlockSpec((tc, LANES), lambda i: (i, 0))],
                  out_specs=pl.BlockSpec((nb, tc, LANES), lambda i: (0, i, 0)),
                  out_shape=jax.ShapeDtypeStruct((nb, cfg.T, LANES), F32), scratch_shapes=[pltpu.VMEM((1, LANES), F32)],
                  compiler_params=_params("arbitrary"))(zf)


def _fgate_bwd(dcum_c, zf, cfg, *, name):
    tc = _cum_tile(cfg)
    tpb = cfg.S // tc
    nt = cfg.T // tc
    hp = _attn_tiles(cfg)[2]
    nb = cfg.H // hp

    def body(d_ref, z_ref, o_ref, carry):
        @pl.when(pl.program_id(0) % tpb == 0)
        def _():
            carry[...] = jnp.zeros_like(carry)

        d = d_ref[0]
        for b in range(1, nb):
            d = d + pltpu.roll(d_ref[b], hp * b, axis=1)
        tri = (lax.broadcasted_iota(jnp.int32, (tc, tc), 0) <= lax.broadcasted_iota(jnp.int32, (tc, tc), 1)).astype(F32)
        suf = jnp.dot(tri, d, precision=lax.Precision.HIGHEST, preferred_element_type=F32) + carry[...]
        o_ref[...] = (suf * _sigmoid(-z_ref[...])).astype(BF16)
        carry[...] = suf[0:1, :]

    blk = pl.BlockSpec((tc, LANES), lambda i: (nt - 1 - i, 0))
    return _pcall(body, name=name, grid=(nt,), in_specs=[pl.BlockSpec((nb, tc, LANES), lambda i: (0, nt - 1 - i, 0)), blk],
                  out_specs=blk, out_shape=jax.ShapeDtypeStruct((cfg.T, LANES), BF16),
                  scratch_shapes=[pltpu.VMEM((1, LANES), F32)], compiler_params=_params("arbitrary"))(dcum_c, zf)


def _attn_tiles(cfg):
    assert LANES % cfg.Dh == 0 and cfg.H % (LANES // cfg.Dh) == 0
    tk = _pick(cfg.S, (256, 128))
    tq = _pick(cfg.S, (2 * tk, tk))
    return tq, tk, LANES // cfg.Dh


BIAS_LANES = 3


def _head_lanes(hd, cfg, hp):
    li = lax.broadcasted_iota(jnp.int32, (1, LANES), 1)
    own = (li >= hd * cfg.Dh) & (li < (hd + 1) * cfg.Dh)
    return own, li, ((hd + 1) % hp) * cfg.Dh


def _q_aug(q, hd, cfg, hp):
    own, li, b0 = _head_lanes(hd, cfg, hp)
    ones = ((li >= b0) & (li < b0 + BIAS_LANES)).astype(F32)
    return jnp.where(own, q * cfg.Dh ** -0.5, ones).astype(BF16)


def _k_aug(k, ck, hd, cfg, hp):
    own, li, b0 = _head_lanes(hd, cfg, hp)
    hi = ck.astype(BF16).astype(F32)
    mid = (ck - hi).astype(BF16).astype(F32)
    lo = ck - hi - mid
    bias = jnp.where(li == b0, -hi, jnp.where(li == b0 + 1, -mid, jnp.where(li == b0 + 2, -lo, 0.0)))
    return jnp.where(own, k, bias).astype(BF16)


def _attn_fwd(zm, cum_c, cfg, *, name):
    S, Dh = cfg.S, cfg.Dh
    tq, tk, hp = _attn_tiles(cfg)
    assert hp >= 2
    nq, nb, per = S // tq, cfg.H // hp, tq // tk
    qb, kb, vb = cfg.q_off // LANES, (cfg.q_off + cfg.AW) // LANES, (cfg.q_off + 2 * cfg.AW) // LANES

    def body(q_ref, k_ref, v_ref, cc_ref, o_ref, o32_ref, lse_ref, ka_s, vt_s):
        qi = pl.program_id(2)

        @pl.when(qi == 0)
        def _():
            def prep(c, _):
                r = pl.multiple_of(c * tk, tk)
                kc = k_ref[pl.ds(r, tk), :]
                for hd in range(hp):
                    ka_s[hd, pl.ds(r, tk), :] = _k_aug(kc, cc_ref[0, pl.ds(r, tk), hd:hd + 1], hd, cfg, hp)
                vt_s[:, pl.ds(r, tk)] = v_ref[pl.ds(r, tk), :].T.astype(BF16)
                return 0

            lax.fori_loop(0, S // tk, prep, 0)

        key_i = lax.broadcasted_iota(jnp.int32, (tk, tq), 0)
        qry_i = lax.broadcasted_iota(jnp.int32, (tk, tq), 1)
        qf = q_ref[...]
        qa = [_q_aug(qf, hd, cfg, hp) for hd in range(hp)]

        def scores(j):
            r = pl.multiple_of(j * tk, tk)
            return tuple(lax.dot_general(ka_s[hd, pl.ds(r, tk), :], qa[hd], NT, preferred_element_type=F32) for hd in range(hp))

        def chunk(j, s_all, carry, diag=None):
            r = pl.multiple_of(j * tk, tk)
            new = []
            for hd in range(hp):
                m, l, acc = carry[hd]
                s = s_all[hd]
                if diag is not None:
                    s = jnp.where(key_i + diag * tk <= qry_i, s, NEG)
                m_new = jnp.maximum(m, jnp.max(s, axis=0, keepdims=True))
                a = jnp.exp(m - m_new)
                p = jnp.exp(s - m_new)
                l = a * l + jnp.sum(p, axis=0, keepdims=True)
                p_hi = p.astype(BF16)
                p_lo = (p - p_hi.astype(F32)).astype(BF16)
                vt = vt_s[hd * Dh:(hd + 1) * Dh, pl.ds(r, tk)]
                acc = a * acc + (jnp.dot(vt, p_hi, preferred_element_type=F32) + jnp.dot(vt, p_lo, preferred_element_type=F32))
                new.append((m_new, l, acc))
            return tuple(new)

        init = tuple((jnp.full((1, tq), NEG, F32), jnp.zeros((1, tq), F32), jnp.zeros((Dh, tq), F32)) for _ in range(hp))
        n_full = qi * per

        def step(j, c):
            stats, s_cur = c
            s_next = scores(j + 1)
            return chunk(j, s_cur, stats), s_next

        res, s_cur = lax.fori_loop(0, n_full, step, (init, scores(0)))
        for d in range(per):
            s_next = scores(n_full + d + 1) if d + 1 < per else None
            res = chunk(n_full + d, s_cur, res, diag=d)
            s_cur = s_next
        o = jnp.concatenate([acc / l for _, l, acc in res], axis=0).T
        o_ref[...] = o.astype(BF16)
        o32_ref[...] = o
        lse_ref[...] = jnp.zeros_like(lse_ref)
        for hd in range(hp):
            lse_ref[0, 0, hd:hd + 1, :] = res[hd][0] + jnp.log(res[hd][1])

    return _pcall(body, name=name, grid=(cfg.Bl, nb, nq),
                  in_specs=[pl.BlockSpec((tq, LANES), lambda b, h, i: (b * nq + i, qb + h)),
                            pl.BlockSpec((S, LANES), lambda b, h, i: (b, kb + h)),
                            pl.BlockSpec((S, LANES), lambda b, h, i: (b, vb + h)),
                            pl.BlockSpec((1, S, LANES), lambda b, h, i: (h, b, 0))],
                  out_specs=[pl.BlockSpec((tq, LANES), lambda b, h, i: (b * nq + i, h)),
                             pl.BlockSpec((tq, LANES), lambda b, h, i: (b * nq + i, h)),
                             pl.BlockSpec((1, 1, SUBLANES, tq), lambda b, h, i: (b, h, 0, i))],
                  out_shape=[jax.ShapeDtypeStruct((cfg.T, cfg.AW), BF16), jax.ShapeDtypeStruct((cfg.T, cfg.AW), F32),
                             jax.ShapeDtypeStruct((cfg.Bl, nb, SUBLANES, S), F32)],
                  scratch_shapes=[pltpu.VMEM((hp, S, LANES), BF16), pltpu.VMEM((LANES, S), BF16)],
                  compiler_params=_params("parallel", "parallel", "arbitrary"))(zm, zm, zm, cum_c)


def _attn_bwd(zm, cum_c, o, do, lse, cfg, *, name):
    S, Dh = cfg.S, cfg.Dh
    tq, t, hp = _attn_tiles(cfg)
    nq, nk, nb, per = S // tq, S // t, cfg.H // hp, tq // t
    qb, kb, vb = cfg.q_off // LANES, (cfg.q_off + cfg.AW) // LANES, (cfg.q_off + 2 * cfg.AW) // LANES
    scale = Dh ** -0.5

    def body(q_ref, k_ref, v_ref, cc_ref, o_ref, do_ref, lse_ref, dq_ref, dk_ref, dv_ref, dcc_ref,
             ka_s, qa_s, vz_s, kt_s, dd_s, dqt_s):
        li = lax.broadcasted_iota(jnp.int32, (1, LANES), 1)
        ri = lax.broadcasted_iota(jnp.int32, (LANES, 1), 0)
        key_i = lax.broadcasted_iota(jnp.int32, (t, tq), 0)
        qry_i = lax.broadcasted_iota(jnp.int32, (t, tq), 1)

        def prep(c, _):
            r = pl.multiple_of(c * t, t)
            kc, vc, qc = k_ref[pl.ds(r, t), :], v_ref[pl.ds(r, t), :], q_ref[pl.ds(r, t), :]
            prod_t = (do_ref[pl.ds(r, t), :].astype(F32) * o_ref[pl.ds(r, t), :].astype(F32)).T
            for hd in range(hp):
                own = _head_lanes(hd, cfg, hp)[0]
                ka_s[hd, pl.ds(r, t), :] = _k_aug(kc, cc_ref[0, pl.ds(r, t), hd:hd + 1], hd, cfg, hp)
                qa_s[hd, pl.ds(r, t), :] = _q_aug(qc, hd, cfg, hp)
                vz_s[hd, pl.ds(r, t), :] = jnp.where(own, vc, 0.0).astype(BF16)
                dd_s[hd:hd + 1, pl.ds(r, t)] = jnp.sum(prod_t[hd * Dh:(hd + 1) * Dh, :], axis=0, keepdims=True)
            kt_s[:, pl.ds(r, t)] = kc.T.astype(BF16)
            dqt_s[:, pl.ds(r, t)] = jnp.zeros((LANES, t), F32)
            return 0

        lax.fori_loop(0, nk, prep, 0)

        def kv_step(j, _):
            rk = pl.multiple_of(j * t, t)
            i0 = j // per

            def tile(i, carry, masked):
                rq = pl.multiple_of(i * tq, tq)
                dob = do_ref[pl.ds(rq, tq), :]
                new, dq_t = [], None
                for hd in range(hp):
                    dk_h, dv_h, dsum_h = carry[hd]
                    qa = qa_s[hd, pl.ds(rq, tq), :]
                    s = lax.dot_general(ka_s[hd, pl.ds(rk, t), :], qa, NT, preferred_element_type=F32)
                    p = jnp.exp(s - lse_ref[0, 0, hd:hd + 1, pl.ds(rq, tq)])
                    if masked:
                        p = jnp.where(key_i + (rk - rq) <= qry_i, p, 0.0)
                    dp = lax.dot_general(vz_s[hd, pl.ds(rk, t), :], dob, NT, preferred_element_type=F32)
                    ds = p * (dp - dd_s[hd:hd + 1, pl.ds(rq, tq)])
                    dsb = ds.astype(BF16)
                    dv_h = dv_h + jnp.dot(p.astype(BF16), dob, preferred_element_type=F32)
                    dk_h = dk_h + jnp.dot(dsb, qa, preferred_element_type=F32)
                    dq_h = jnp.dot(kt_s[:, pl.ds(rk, t)], dsb, preferred_element_type=F32)
                    dq_t = dq_h if hd == 0 else jnp.where((ri >= hd * Dh) & (ri < (hd + 1) * Dh), dq_h, dq_t)
                    for c0 in range(0, tq, LANES):
                        dsum_h = dsum_h + ds[:, c0:c0 + LANES]
                    new.append((dk_h, dv_h, dsum_h))
                dqt_s[:, pl.ds(rq, tq)] += dq_t * scale
                return tuple(new)

            zero = tuple((jnp.zeros((t, LANES), F32),) * 3 for _ in range(hp))
            res = lax.fori_loop(i0 + 1, nq, functools.partial(tile, masked=False), tile(i0, zero, True))
            dk, dv, dcc = res[0][0], res[0][1], jnp.zeros((t, LANES), F32)
            for hd in range(hp):
                own = _head_lanes(hd, cfg, hp)[0]
                if hd > 0:
                    dk, dv = jnp.where(own, res[hd][0], dk), jnp.where(own, res[hd][1], dv)
                dcc = dcc + jnp.where(li == hd, -jnp.sum(res[hd][2], axis=1, keepdims=True), 0.0)
            dk_ref[pl.ds(rk, t), :] = dk.astype(BF16)
            dv_ref[pl.ds(rk, t), :] = dv.astype(BF16)
            dcc_ref[0, pl.ds(rk, t), :] = dcc
            return 0

        lax.fori_loop(0, nk, kv_step, 0)

        def finish(c, _):
            r = pl.multiple_of(c * t, t)
            dq_ref[pl.ds(r, t), :] = dqt_s[:, pl.ds(r, t)].T.astype(BF16)
            return 0

        lax.fori_loop(0, nk, finish, 0)

    blk = pl.BlockSpec((S, LANES), lambda b, h: (b, h))
    cc = pl.BlockSpec((1, S, LANES), lambda b, h: (h, b, 0))
    os_ = jax.ShapeDtypeStruct((cfg.T, cfg.AW), BF16)
    return _pcall(body, name=name, grid=(cfg.Bl, nb),
                  in_specs=[pl.BlockSpec((S, LANES), lambda b, h: (b, qb + h)), pl.BlockSpec((S, LANES), lambda b, h: (b, kb + h)),
                            pl.BlockSpec((S, LANES), lambda b, h: (b, vb + h)), cc, blk, blk,
                            pl.BlockSpec((1, 1, SUBLANES, S), lambda b, h: (b, h, 0, 0))],
                  out_specs=[blk, blk, blk, cc],
                  out_shape=[os_, os_, os_, jax.ShapeDtypeStruct((nb, cfg.T, LANES), F32)],
                  scratch_shapes=[pltpu.VMEM((hp, S, LANES), BF16)] * 3 + [pltpu.VMEM((LANES, S), BF16),
                                  pltpu.VMEM((SUBLANES, S), F32), pltpu.VMEM((LANES, S), F32)],
                  compiler_params=_params("parallel", "parallel"))(zm, zm, zm, cum_c, o, do, lse)


FFN_HALO = 8
FFN_CHUNK = 16


def _ffn_tiles(cfg):
    assert cfg.KF - 1 <= FFN_HALO
    return _pick(cfg.S, (512, 256, 128, 64, 32, 16, 8)), _pick(cfg.F, (256, 128))


def _gelu(x):
    return 0.5 * x * (1.0 + lax.erf(x * (2.0 ** -0.5)))


def _gelu_grad(x):
    return 0.5 * (1.0 + lax.erf(x * (2.0 ** -0.5))) + x * jnp.exp(-0.5 * x * x) * ((2.0 * math.pi) ** -0.5)


def _ffn_conv_fwd(h0, w, cb, cfg, *, name):
    KF, HALO = cfg.KF, FFN_HALO
    ts, tf = _ffn_tiles(cfg)
    tpb, nf = cfg.S // ts, cfg.F // tf
    lead = HALO - (KF - 1)

    CH = FFN_CHUNK

    def body(g_ref, gp_ref, l_ref, lp_ref, wg_ref, wl_ref, cg_ref, cl_ref, o_ref, hg_ref, hl_ref, g_s, l_s):
        first = pl.program_id(1) % tpb == 0
        for s, main, prev in ((g_s, g_ref, gp_ref), (l_s, l_ref, lp_ref)):
            s[0:HALO, :] = jnp.where(first, 0.0, prev[...])
            s[HALO:HALO + CH, :] = main[0:CH, :]
        wg, wl = [wg_ref[k:k + 1, :] for k in range(KF)], [wl_ref[k:k + 1, :] for k in range(KF)]
        for r0 in range(0, ts, CH):
            hg, hl = cg_ref[...], cl_ref[...]
            for k in range(KF):
                if r0 == 0:
                    xg, xl = g_s[lead + k:lead + k + CH, :], l_s[lead + k:lead + k + CH, :]
                else:
                    a = r0 - (KF - 1) + k
                    xg, xl = g_ref[a:a + CH, :], l_ref[a:a + CH, :]
                hg, hl = hg + wg[k] * xg, hl + wl[k] * xl
            o_ref[r0:r0 + CH, :] = (_gelu(hg) * hl).astype(BF16)
            hg_ref[r0:r0 + CH, :], hl_ref[r0:r0 + CH, :] = hg, hl

    hb = ts // HALO
    prev = lambda off: pl.BlockSpec((HALO, tf), lambda j, i: (jnp.maximum(i * hb - 1, 0), off + j))
    main = lambda off: pl.BlockSpec((ts, tf), lambda j, i: (i, off + j))
    wsp = lambda off: pl.BlockSpec((FFN_TAPS, tf), lambda j, i: (0, off + j))
    vsp = lambda off: pl.BlockSpec((1, tf), lambda j, i: (0, off + j))
    hs = jax.ShapeDtypeStruct((cfg.T, cfg.F), F32)
    return _pcall(body, name=name, grid=(nf, cfg.T // ts),
                  in_specs=[main(0), prev(0), main(nf), prev(nf), wsp(0), wsp(nf), vsp(0), vsp(nf)],
                  out_specs=[main(0)] * 3, out_shape=[jax.ShapeDtypeStruct((cfg.T, cfg.F), BF16), hs, hs],
                  scratch_shapes=[pltpu.VMEM((HALO + CH, tf), F32)] * 2,
                  compiler_params=_params("parallel", "parallel"))(h0, h0, h0, h0, w, w, cb, cb)


def _ffn_conv_bwd(df, h0, hg, hl, w, cfg, *, name):
    KF, HALO = cfg.KF, FFN_HALO
    ts, tf = _ffn_tiles(cfg)
    tpb, nf = cfg.S // ts, cfg.F // tf
    ext = ts + HALO

    CH = FFN_CHUNK

    def body(g_ref, l_ref, hg_ref, hgn_ref, hl_ref, hln_ref, d_ref, dn_ref, wg_ref, wl_ref,
             dg_ref, dl_ref, dwg_ref, dwl_ref, dcg_ref, dcl_ref, dhg_s, dhl_s):
        i = pl.program_id(1)
        last = i % tpb == tpb - 1

        @pl.when(i == 0)
        def _():
            dwg_ref[...] = jnp.zeros_like(dwg_ref)
            dwl_ref[...] = jnp.zeros_like(dwl_ref)
            dcg_ref[...] = jnp.zeros_like(dcg_ref)
            dcl_ref[...] = jnp.zeros_like(dcl_ref)

        wg, wl = [wg_ref[k:k + 1, :] for k in range(KF)], [wl_ref[k:k + 1, :] for k in range(KF)]

        def grads(hg, hl, d):
            return d * hl * _gelu_grad(hg), d * _gelu(hg)

        for r0 in range(0, ts, CH):
            dhg_s[r0:r0 + CH, :], dhl_s[r0:r0 + CH, :] = grads(hg_ref[r0:r0 + CH, :], hl_ref[r0:r0 + CH, :], d_ref[r0:r0 + CH, :])
        dhg_s[ts:ext, :], dhl_s[ts:ext, :] = grads(hgn_ref[...], hln_ref[...], jnp.where(last, 0.0, dn_ref[...]))

        for dh_s, x_ref, wk, dx_ref, dw_ref, dc_ref in ((dhg_s, g_ref, wg, dg_ref, dwg_ref, dcg_ref),
                                                        (dhl_s, l_ref, wl, dl_ref, dwl_ref, dcl_ref)):
            dw_acc = [jnp.zeros((CH, tf), F32) for _ in range(KF)]
            for r0 in range(0, ts, CH):
                x = x_ref[r0:r0 + CH, :]
                dx = jnp.zeros((CH, tf), F32)
                for k in range(KF):
                    dhk = dh_s[r0 + KF - 1 - k:r0 + KF - 1 - k + CH, :]
                    dx = dx + wk[k] * dhk
                    dw_acc[k] = dw_acc[k] + x * dhk
                    if k == KF - 1:
                        dc_acc = dhk if r0 == 0 else dc_acc + dhk
                dx_ref[r0:r0 + CH, :] = dx.astype(BF16)
            for k in range(KF):
                dw_ref[k:k + 1, :] += jnp.sum(dw_acc[k], axis=0, keepdims=True)
            dc_ref[...] += jnp.sum(dc_acc, axis=0, keepdims=True)

    hb = ts // HALO
    nhb = cfg.T // HALO
    main = lambda off: pl.BlockSpec((ts, tf), lambda j, i: (i, off + j))
    nxt = pl.BlockSpec((HALO, tf), lambda j, i: (jnp.minimum((i + 1) * hb, nhb - 1), j))
    wsp = lambda off: pl.BlockSpec((FFN_TAPS, tf), lambda j, i: (0, off + j))
    vsp = pl.BlockSpec((1, tf), lambda j, i: (0, j))
    dxs, dws, dcs = (jax.ShapeDtypeStruct((cfg.T, cfg.F), BF16), jax.ShapeDtypeStruct((FFN_TAPS, cfg.F), F32),
                     jax.ShapeDtypeStruct((1, cfg.F), F32))
    return _pcall(body, name=name, grid=(nf, cfg.T // ts),
                  in_specs=[main(0), main(nf), main(0), nxt, main(0), nxt, main(0), nxt, wsp(0), wsp(nf)],
                  out_specs=[main(0), main(0), wsp(0), wsp(0), vsp, vsp],
                  out_shape=[dxs, dxs, dws, dws, dcs, dcs],
                  scratch_shapes=[pltpu.VMEM((ext, tf), F32)] * 2,
                  compiler_params=_params("parallel", "arbitrary"))(h0, h0, hg, hg, hl, hl, df, df, w, w)


def _ada_fwd(c_all, w, b, *, name):
    L, D, n = w.shape
    B = c_all.shape[0]

    def body(c_ref, w_ref, b_ref, o_ref):
        c = c_ref[...]
        act = (c * _sigmoid(c)).astype(BF16)
        o_ref[0] = jnp.dot(act, w_ref[0].astype(BF16), preferred_element_type=F32) + b_ref[0]

    return _pcall(body, name=name, grid=(L,),
                  in_specs=[pl.BlockSpec((B, D), lambda l: (0, 0)), pl.BlockSpec((1, D, n), lambda l: (l, 0, 0)),
                            pl.BlockSpec((1, 1, n), lambda l: (l, 0, 0))],
                  out_specs=pl.BlockSpec((1, B, n), lambda l: (l, 0, 0)), out_shape=jax.ShapeDtypeStruct((L, B, n), F32),
                  compiler_params=_params("parallel"))(c_all, w, b)


def _ada_bwd(c_all, dmod, *, name):
    L, B, n = dmod.shape
    D = c_all.shape[1]

    def body(c_ref, d_ref, o_ref):
        c = c_ref[...]
        act = (c * _sigmoid(c)).astype(BF16)
        o_ref[0] = lax.dot_general(act, d_ref[0].astype(BF16), TN, preferred_element_type=F32)

    return _pcall(body, name=name, grid=(L,),
                  in_specs=[pl.BlockSpec((B, D), lambda l: (0, 0)), pl.BlockSpec((1, B, n), lambda l: (l, 0, 0))],
                  out_specs=pl.BlockSpec((1, D, n), lambda l: (l, 0, 0)), out_shape=jax.ShapeDtypeStruct((L, D, n), F32),
                  compiler_params=_params("parallel"))(c_all, dmod)


def _slot_sum(x, *, name):
    n, R, W = x.shape
    tr = _pick(R, (256, 128, 64, 32, 16, 8))

    def body(x_ref, o_ref):
        acc = x_ref[0].astype(F32)
        for k in range(1, n):
            acc = acc + x_ref[k].astype(F32)
        o_ref[...] = acc

    return _pcall(body, name=name, grid=(R // tr,), in_specs=[pl.BlockSpec((n, tr, W), lambda i: (0, i, 0))],
                  out_specs=pl.BlockSpec((tr, W), lambda i: (i, 0)), out_shape=jax.ShapeDtypeStruct((R, W), F32),
                  compiler_params=_params("parallel"))(x)


def _adamw_math(g, w, m, v):
    r1, r2 = 1.0 / (1.0 - ADAM_B1 ** ADAM_STEP), 1.0 / (1.0 - ADAM_B2 ** ADAM_STEP)
    m2 = ADAM_B1 * m + (1.0 - ADAM_B1) * g
    v2 = ADAM_B2 * v + (1.0 - ADAM_B2) * (g * g)
    return -ADAM_LR * ((m2 * r1) / (jnp.sqrt(v2 * r2) + ADAM_EPS) + ADAM_WD * w), m2, v2


def _adamw_many(gs, ws, ms, vs, *, name):
    n = len(gs)

    def body(*refs):
        ins, outs = refs[:4 * n], refs[4 * n:]
        for i in range(n):
            d, m2, v2 = _adamw_math(*(ins[j * n + i][...] for j in range(4)))
            outs[i][...], outs[n + i][...], outs[2 * n + i][...] = d, m2, v2

    vm = pl.BlockSpec(memory_space=pltpu.VMEM)
    outs = _pcall(body, name=name, in_specs=[vm] * (4 * n), out_specs=[vm] * (3 * n),
                  out_shape=[jax.ShapeDtypeStruct(a.shape, F32) for _ in range(3) for a in ws],
                  compiler_params=pltpu.CompilerParams(vmem_limit_bytes=VMEM_LIMIT))(*gs, *ws, *ms, *vs)
    return outs[:n], outs[n:2 * n], outs[2 * n:]


def _adamw(gs, w, m, v, *, name):
    n, R, W = gs.shape
    tr = _pick(R, (256, 128, 64, 32, 16, 8))

    def body(g_ref, w_ref, m_ref, v_ref, go_ref, d_ref, mo_ref, vo_ref):
        g = g_ref[0].astype(F32)
        for k in range(1, n):
            g = g + g_ref[k].astype(F32)
        go_ref[...] = g
        d_ref[...], mo_ref[...], vo_ref[...] = _adamw_math(g, w_ref[...], m_ref[...], v_ref[...])

    blk = pl.BlockSpec((tr, W), lambda i: (i, 0))
    o = jax.ShapeDtypeStruct((R, W), F32)
    return _pcall(body, name=name, grid=(R // tr,), in_specs=[pl.BlockSpec((n, tr, W), lambda i: (0, i, 0)), blk, blk, blk],
                  out_specs=[blk] * 4, out_shape=[o] * 4, compiler_params=_params("parallel"))(gs, w, m, v)


ALL_PEERS = tuple(range(1, N_DEV))


def _peer_copies(x_ref, land_ref, send_sems, recv_sems, all_to_all, ks=ALL_PEERS):
    mx, my, mc = lax.axis_index("x"), lax.axis_index("y"), lax.axis_index("c")
    me = 4 * mx + 2 * my + mc
    copies = []
    for n, k in enumerate(ks):
        px, py, pc = mx ^ ((k >> 2) & 1), my ^ ((k >> 1) & 1), mc ^ (k & 1)
        copies.append(pltpu.make_async_remote_copy(
            src_ref=x_ref.at[4 * px + 2 * py + pc] if all_to_all else x_ref, dst_ref=land_ref.at[me],
            send_sem=send_sems.at[n], recv_sem=recv_sems.at[n], device_id=(px, py, pc),
            device_id_type=pl.DeviceIdType.MESH))
    return copies


def _gather_two_level(x, *, name, after=None):
    def body(x_ref, *rest):
        o_ref, send_sems, recv_sems, local_sem = rest[-4:]
        mx, my, mc = lax.axis_index("x"), lax.axis_index("y"), lax.axis_index("c")
        me, sibling = (mx, my, mc), (mx, my, 1 - mc)
        chips = [(1 - mx, my), (mx, 1 - my), (1 - mx, 1 - my)]

        def slot(px, py, pc):
            return o_ref.at[4 * px + 2 * py + pc]

        def copy(k, block, to, src=None):
            return pltpu.make_async_remote_copy(
                src_ref=slot(*block) if src is None else src, dst_ref=slot(*block), send_sem=send_sems.at[k],
                recv_sem=recv_sems.at[k], device_id=to, device_id_type=pl.DeviceIdType.MESH)

        mine = pltpu.make_async_copy(x_ref, slot(*me), local_sem)
        mine.start()
        first = [copy(0, me, sibling, src=x_ref)] + [copy(1 + j, me, (*chip, mc), src=x_ref) for j, chip in enumerate(chips)]
        for cp in first:
            cp.start()
        passed = [copy(4 + j, (*chip, mc), sibling) for j, chip in enumerate(chips)]
        for j, chip in enumerate(chips):
            copy(1 + j, (*chip, mc), me).wait_recv()
            passed[j].start()
        copy(0, sibling, me).wait_recv()
        for j, chip in enumerate(chips):
            copy(4 + j, (*chip, 1 - mc), me).wait_recv()
        for cp in first + passed:
            cp.wait_send()
        mine.wait()

    anyspec = pl.BlockSpec(memory_space=pl.ANY)
    args = [x] if after is None else [x, after]
    return _pcall(body, name=name, in_specs=[anyspec] * len(args), out_specs=anyspec,
                  out_shape=jax.ShapeDtypeStruct((N_DEV,) + tuple(x.shape), x.dtype),
                  scratch_shapes=[pltpu.SemaphoreType.DMA((N_DEV - 1,)), pltpu.SemaphoreType.DMA((N_DEV - 1,)),
                                  pltpu.SemaphoreType.DMA(())])(*args)


_HBM = pl.BlockSpec(memory_space=pltpu.HBM)
_SEM = pl.BlockSpec(memory_space=pltpu.SEMAPHORE)
_EFFECT = pltpu.SideEffectType.DATAFLOW_SIDE_EFFECTING


def _local_copy(x_ref, land_ref, local_sem, all_to_all):
    me = 4 * lax.axis_index("x") + 2 * lax.axis_index("y") + lax.axis_index("c")
    return pltpu.make_async_copy(x_ref.at[me] if all_to_all else x_ref, land_ref.at[me], local_sem)


def _exchange_start(x, *, all_to_all, name, after=None):
    blk = x.shape[1:] if all_to_all else x.shape
    land = lax.empty((N_DEV,) + tuple(blk), x.dtype)
    has_after = after is not None

    def body(*refs):
        x_ref, land_ref = refs[0], refs[1]
        send_sems, recv_sems, local_sem, _, _, token = refs[2 + has_after:]
        _local_copy(x_ref, land_ref, local_sem, all_to_all).start()
        for cp in _peer_copies(x_ref, land_ref, send_sems, recv_sems, all_to_all):
            cp.start()
        token[...] = jnp.zeros_like(token)

    n_sem = pltpu.SemaphoreType.DMA((N_DEV - 1,))
    args = [pltpu.with_memory_space_constraint(x, pltpu.HBM), pltpu.with_memory_space_constraint(land, pltpu.HBM)]
    in_specs = [_HBM, _HBM]
    if has_after:
        args.append(after)
        in_specs.append(pl.BlockSpec(memory_space=pl.ANY))
    send_sems, recv_sems, local_sem, x_thru, land_thru, token = _pcall(
        body, name=name, in_specs=in_specs,
        out_shape=(n_sem, n_sem, pltpu.SemaphoreType.DMA(()), pltpu.HBM(x.shape, x.dtype), pltpu.HBM(land.shape, land.dtype),
                   jax.ShapeDtypeStruct((SUBLANES, LANES), F32)),
        out_specs=(_SEM, _SEM, _SEM, _HBM, _HBM, pl.BlockSpec(memory_space=pltpu.VMEM)), input_output_aliases={0: 3, 1: 4},
        compiler_params=pltpu.CompilerParams(has_side_effects=_EFFECT))(*args)
    return (send_sems, recv_sems, local_sem, x_thru, land_thru, all_to_all), token


def _exchange_wait(state, after, *, name):
    send_sems, recv_sems, local_sem, x_thru, land_thru, all_to_all = state

    def body(x_ref, land_ref, send_sems, recv_sems, local_sem, after_ref, x_dead, landed):
        _local_copy(x_ref, land_ref, local_sem, all_to_all).wait()
        for cp in _peer_copies(x_ref, land_ref, send_sems, recv_sems, all_to_all):
            cp.wait_send()
            cp.wait_recv()

    return _pcall(
        body, name=name, in_specs=(_HBM, _HBM, _SEM, _SEM, _SEM, pl.BlockSpec(memory_space=pl.ANY)),
        out_shape=(pltpu.HBM(x_thru.shape, x_thru.dtype), pltpu.HBM(land_thru.shape, land_thru.dtype)),
        out_specs=(_HBM, _HBM), input_output_aliases={0: 0, 1: 1},
        compiler_params=pltpu.CompilerParams(has_side_effects=_EFFECT))(
            x_thru, land_thru, send_sems, recv_sems, local_sem, after)[1]


def _exchange(x, *, all_to_all, name):
    blk = x.shape[1:] if all_to_all else x.shape

    def body(x_ref, o_ref, send_sems, recv_sems, local_sem):
        me = 4 * lax.axis_index("x") + 2 * lax.axis_index("y") + lax.axis_index("c")
        mine = pltpu.make_async_copy(x_ref.at[me] if all_to_all else x_ref, o_ref.at[me], local_sem)
        mine.start()
        copies = _peer_copies(x_ref, o_ref, send_sems, recv_sems, all_to_all)
        for cp in copies:
            cp.start()
        for cp in copies:
            cp.wait()
        mine.wait()

    anyspec = pl.BlockSpec(memory_space=pl.ANY)
    return _pcall(body, name=name, in_specs=[anyspec], out_specs=anyspec,
                  out_shape=jax.ShapeDtypeStruct((N_DEV,) + tuple(blk), x.dtype),
                  scratch_shapes=[pltpu.SemaphoreType.DMA((N_DEV - 1,)), pltpu.SemaphoreType.DMA((N_DEV - 1,)),
                                  pltpu.SemaphoreType.DMA(())])(x)


PACK_ROWS = 16


def _pack(arrs, width, dtype, lead=0):
    parts, segs, r = [], [], 0
    for a in arrs:
        lshape, shape = a.shape[:lead], a.shape[lead:]
        n = math.prod(shape)
        rows = -(-n // width)
        rows_p = -(-rows // PACK_ROWS) * PACK_ROWS
        if n == rows * width:
            blk = a.reshape(lshape + (rows, width)).astype(dtype)
            parts.append(jnp.pad(blk, [(0, 0)] * lead + [(0, rows_p - rows), (0, 0)]) if rows_p > rows else blk)
        else:
            flat = jnp.pad(a.reshape(lshape + (n,)).astype(dtype), [(0, 0)] * lead + [(0, rows_p * width - n)])
            parts.append(flat.reshape(lshape + (rows_p, width)))
        segs.append((r, n, shape))
        r += rows_p
    return jnp.concatenate(parts, axis=lead), segs


def _unpack(p, segs):
    lshape, width = p.shape[:-2], p.shape[-1]
    outs = []
    for r, n, shape in segs:
        rows = -(-n // width)
        blk = p[..., r:r + rows, :]
        if n != rows * width:
            blk = blk.reshape(lshape + (rows * width,))[..., :n]
        outs.append(blk.reshape(lshape + shape))
    return outs


def _split_cols(a, f_off, h):
    return jnp.concatenate([a[..., :f_off], a[..., f_off + h:]], axis=-1), a[..., f_off:f_off + h]


def _merge_cols(main, f, f_off):
    return jnp.concatenate([main[..., :f_off], f, main[..., f_off:]], axis=-1)


def _pad_to(a, n, axis):
    pad = [(0, 0)] * a.ndim
    pad[axis] = (0, n - a.shape[axis])
    return jnp.pad(a, pad)


def kernel(x, c, w_ada, b_ada, w_in, b_in, conv_a_w, conv_a_b, ln_conv_g, ln_conv_b, w_conv_proj, w_attn_proj, w_mix_out, b_mix_out, ln1_g, ln1_b, w_ffn_up, ffn_conv_w, ffn_conv_b, w_ffn_down, ln2_g, ln2_b, loss_target, m_w_ada, m_b_ada, m_w_in, m_b_in, m_conv_a_w, m_conv_a_b, m_ln_conv_g, m_ln_conv_b, m_w_conv_proj, m_w_attn_proj, m_w_mix_out, m_b_mix_out, m_ln1_g, m_ln1_b, m_w_ffn_up, m_ffn_conv_w, m_ffn_conv_b, m_w_ffn_down, m_ln2_g, m_ln2_b, v_w_ada, v_b_ada, v_w_in, v_b_in, v_conv_a_w, v_conv_a_b, v_ln_conv_g, v_ln_conv_b, v_w_conv_proj, v_w_attn_proj, v_w_mix_out, v_b_mix_out, v_ln1_g, v_ln1_b, v_w_ffn_up, v_ffn_conv_w, v_ffn_conv_b, v_w_ffn_down, v_ln2_g, v_ln2_b):
    L, D = w_ada.shape[0], w_ada.shape[1]
    Bl, S, _ = x.shape
    C, KW, AW = conv_a_b.shape[1], conv_a_w.shape[1], w_attn_proj.shape[1]
    F, KF, n_in_all = ffn_conv_b.shape[1] // 2, ffn_conv_w.shape[1], b_in.shape[1]
    H = n_in_all - 2 * C - 3 * AW - 2 * D
    cfg = Cfg(L=L, Bl=Bl, S=S, D=D, C=C, KW=KW, H=H, Dh=AW // H, F=F, KF=KF)
    T, NM = cfg.T, cfg.NM
    f_off = 2 * C + 3 * AW
    n_ada = w_ada.shape[2]
    me = 4 * lax.axis_index("x") + 2 * lax.axis_index("y") + lax.axis_index("c")

    def my_cols(a, n):
        return lax.dynamic_slice_in_dim(a, me * n, n, axis=a.ndim - 1)

    spack, ssegs = _pack([c, conv_a_w, ffn_conv_w], D, F32)
    c_g, caw_g, fcw_g = _unpack(_exchange(spack, all_to_all=False, name="gather_small"), ssegs)
    c_all = c_g.reshape(N_DEV * Bl, D)
    caw = _pad_to(jnp.moveaxis(caw_g, 0, 2).reshape(L, KW, C), CONV_A_TAPS, 1)
    fcw = _pad_to(jnp.moveaxis(fcw_g, 0, 2).reshape(L, KF, 2 * F), FFN_TAPS, 1)

    mod_part = _ada_fwd(c_all, w_ada, my_cols(b_ada, n_ada)[:, None, :], name="ada_fwd")
    mod_send = jnp.moveaxis(mod_part.reshape(L, N_DEV, Bl, n_ada), 1, 0).reshape(N_DEV, L * Bl, n_ada)
    mod_recv = _exchange(mod_send, all_to_all=True, name="exchange_mod")
    mod = jnp.moveaxis(mod_recv.reshape(N_DEV, L, Bl, n_ada), 0, 2).reshape(L, Bl, 6, 1, D)
    shift1, scale1, gate1, shift2, scale2, gate2 = (mod[:, :, i] for i in range(6))

    big_names = ["w_in", "w_conv_proj", "w_attn_proj", "w_mix_out", "w_ffn_up", "w_ffn_down"]
    transposed = (True, True, True, False, True, False)

    def shard_items(arrs, grp):
        return [arrs[i][l].T if transposed[i] else arrs[i][l] for l, i in grp]

    W = [dict() for _ in range(L)]

    def set_weights(landed, segs, grp):
        for (l, i), a in zip(grp, _unpack(landed, segs)):
            a = a.reshape((-1, a.shape[-1]))
            if i == 0:
                wm_t, wf_t = _split_cols(a.T, f_off, H)
                bm, bf = _split_cols(b_in[l], f_off, H)
                W[l].update(wm_t=wm_t.T, wf_t=_pad_to(wf_t.T, LANES, 0), bm=bm[None], bf=_pad_to(bf, LANES, 0)[None])
            else:
                W[l][("w_cp_t", "w_ap_t", "w_mo", "w_up_t", "w_dn")[i - 1]] = a

    big_w = (w_in, w_conv_proj, w_attn_proj, w_mix_out, w_ffn_up, w_ffn_down)
    w_groups = [[(l, i) for i in range(6)] for l in range(L)]
    pack, segs = _pack(shard_items(big_w, w_groups[0]), D, BF16)
    landed0 = _gather_two_level(pack, name="gather_weights_0", after=mod_recv)
    set_weights(landed0, segs, w_groups[0])
    w_state, token = {}, landed0
    for l in range(1, L):
        pack, segs = _pack(shard_items(big_w, w_groups[l]), D, BF16)
        state, token = _exchange_start(pack, all_to_all=False, name=f"gather_weights_start_{l}", after=token)
        w_state[l] = [state, pack, segs]

    def wait_weights(l, after):
        state, _, segs = w_state[l]
        set_weights(_exchange_wait(state, after, name=f"gather_weights_wait_{l}"), segs, w_groups[l])

    xf = x.reshape(T, D)
    u = _ln_mod_fwd(xf, shift1[0], scale1[0], cfg, name="ln_mod_fwd")
    saved = []
    xin = xf
    for l in range(L):
        w = W[l]
        if l > 0:
            wait_weights(l, u)
        zm = _matmul(u, w["wm_t"], mode="nt", bias=w["bm"], name=f"in_proj_{l}", after=token if l == 0 else None)
        zf = _matmul(u, w["wf_t"], mode="nt", bias=w["bf"], name=f"in_proj_f_{l}")
        a3 = _conv_a_fwd(zm, caw[l], conv_a_b[l][None], ln_conv_g[l][None], ln_conv_b[l][None], cfg, name=f"conv_a_fwd_{l}")
        cum_c = _fgate_fwd(zf, cfg, name=f"fgate_fwd_{l}")
        o, o32, lse = _attn_fwd(zm, cum_c, cfg, name=f"attn_fwd_{l}")
        ya =_matmul(a3, w["w_cp_t"], mode="nt", name=f"conv_proj_{l}")
        yb = _matmul(o, w["w_ap_t"], mode="nt", name=f"attn_proj_{l}")
        mg = _merge_fwd(zm, ya, yb, cfg, name=f"merge_fwd_{l}")
        mix = _matmul(mg, w["w_mo"], mode="nn", bias=b_mix_out[l][None], name=f"mix_out_{l}")
        x1, u2 = _res_ln_fwd(xin, mix, gate1[l], ln1_g[l][None], ln1_b[l][None], cfg, name=f"res_ln1_fwd_{l}",
                             nxt=(shift2[l], scale2[l]))
        h0 = _matmul(u2, w["w_up_t"], mode="nt", name=f"ffn_up_{l}")
        fa, hg, hl = _ffn_conv_fwd(h0, fcw[l], ffn_conv_b[l][None], cfg, name=f"ffn_conv_fwd_{l}")
        ffn = _matmul(fa, w["w_dn"], mode="nn", name=f"ffn_down_{l}")
        saved.append(dict(x=xin, u=u, zm=zm, zf=zf, a3=a3, cum_c=cum_c, o=o, o32=o32, lse=lse, ya=ya, yb=yb, mg=mg, mix=mix,
                          x1=x1, u2=u2, h0=h0, hg=hg, hl=hl, fa=fa, ffn=ffn))
        if l + 1 < L:
            xin, u = _res_ln_fwd(x1, ffn, gate2[l], ln2_g[l][None], ln2_b[l][None], cfg, name=f"res_ln2_fwd_{l}",
                                 nxt=(shift1[l + 1], scale1[l + 1]))
        else:
            xin = _res_ln_fwd(x1, ffn, gate2[l], ln2_g[l][None], ln2_b[l][None], cfg, name=f"res_ln2_fwd_{l}")

    dx, loss_tiles = _loss_grad(xin, loss_target.reshape(T, D), cfg, name="loss_grad")
    loss = lax.psum(0.5 / D * jnp.sum(loss_tiles[:, 0, 0]), ("x", "y", "c"))

    gbig = {}
    g_groups = [[(l, i) for i in range(6)] for l in reversed(range(1, L))] + [[(0, 4), (0, 5)], [(0, 1), (0, 2), (0, 3)], [(0, 0)]]
    g_state = []

    def start_grads(after=None):
        grp = g_groups[len(g_state)]
        send, segs = _pack([gbig[k].reshape((N_DEV, -1, gbig[k].shape[1])) for k in grp], D, BF16, lead=1)
        state, tok = _exchange_start(send, all_to_all=True, name=f"exchange_grads_start_{len(g_state)}", after=after)
        g_state.append((state, send, segs, grp))
        return tok

    gsm = [dict() for _ in range(L)]
    dmods = [None] * L
    token = None
    for l in reversed(range(L)):
        w, s = W[l], saved[l]
        if l == L - 1:
            top = _res_ln_bwd(dx, s["x1"], s["ffn"], gate2[l], ln2_g[l][None], cfg, name=f"res_ln2_bwd_{l}")
        dres2, dffn, dg2, db2, dgate2 = top[:5]
        dfa = _matmul(dffn, w["w_dn"], mode="nt", name=f"d_ffn_act_{l}", after=token)
        gbig[l, 5] = _matmul(s["fa"], dffn, mode="tn", name=f"dw_ffn_down_{l}")
        dh0g, dh0l, dwg, dwl, dcg, dcl = _ffn_conv_bwd(dfa, s["h0"], s["hg"], s["hl"], fcw[l], cfg, name=f"ffn_conv_bwd_{l}")
        du2 = _matmul((dh0g, dh0l), w["w_up_t"], mode="nn", name=f"d_u2_{l}")
        gbig[l, 4] = _matmul((dh0g, dh0l), s["u2"], mode="tn", name=f"dw_ffn_up_{l}")
        token = start_grads() if l == 0 else None
        dres1, dmix, dg1, db1, dgate1, dbmo, dscale2, dshift2 = _res_ln_bwd(
            None, s["x"], s["mix"], gate1[l], ln1_g[l][None], cfg, name=f"res_ln1_bwd_{l}", mod=(du2, s["x1"], scale2[l], dres2))
        dmg = _matmul(dmix, w["w_mo"], mode="nt", name=f"d_merge_{l}", after=token)
        gbig[l, 3] = _matmul(s["mg"], dmix, mode="tn", name=f"dw_mix_out_{l}")
        dya, dyb, dzga, dzgb = _merge_bwd(dmg, s["zm"], s["ya"], s["yb"], cfg, name=f"merge_bwd_{l}")
        gbig[l, 1] = _matmul(dya, s["a3"], mode="tn", name=f"dw_conv_proj_{l}")
        da3 = _matmul(dya, w["w_cp_t"], mode="nn", name=f"d_a3_{l}")
        gbig[l, 2] = _matmul(dyb, s["o"], mode="tn", name=f"dw_attn_proj_{l}")
        token = start_grads() if l == 0 else None
        do = _matmul(dyb, w["w_ap_t"], mode="nn", out_dtype=BF16, name=f"d_o_{l}", after=token)
        dq, dk, dv, dcum_c = _attn_bwd(s["zm"], s["cum_c"], s["o32"], do, s["lse"], cfg, name=f"attn_bwd_{l}")
        dzf = _fgate_bwd(dcum_c, s["zf"], cfg, name=f"fgate_bwd_{l}")
        dzglu, dcaw, dcab, dlcg, dlcb = _conv_a_bwd(da3, s["zm"], caw[l], conv_a_b[l][None], ln_conv_g[l][None],
                                                    ln_conv_b[l][None], cfg, name=f"conv_a_bwd_{l}")
        dzm = jnp.concatenate([dzglu, dq, dk, dv, dzga, dzgb], axis=1)
        du1 = _matmul(dzf, w["wf_t"], mode="nn", name=f"d_u1_f_{l}")
        du1 = _matmul(dzm, w["wm_t"], mode="nn", add=du1, name=f"d_u1_{l}")
        dwm_t = _matmul(dzm, s["u"], mode="tn", name=f"dw_in_{l}")
        dwf_t = _matmul(dzf, s["u"], mode="tn", name=f"dw_in_f_{l}")
        gbig[l, 0] = _merge_cols(dwm_t.T, dwf_t[:H].T, f_off).T
        token = start_grads()
        dbm, dbf = _colsum(dzm, name=f"db_in_{l}"), _colsum(dzf, name=f"db_in_f_{l}")
        if l > 0:
            below = saved[l - 1]
            top = _res_ln_bwd(None, below["x1"], below["ffn"], gate2[l - 1], ln2_g[l - 1][None], cfg,
                              name=f"res_ln2_bwd_{l - 1}", mod=(du1, s["x"], scale1[l], dres1))
            dscale1, dshift1 = top[6], top[7]
        else:
            dx, dscale1, dshift1 = _ln_mod_bwd(du1, s["x"], scale1[l], dres1, cfg, name=f"ln_mod1_bwd_{l}")
        dmods[l] = jnp.concatenate([dshift1, dscale1, dgate1, dshift2, dscale2, dgate2], axis=1).reshape(Bl, 6 * D)
        gsm[l] = dict(b_in=_merge_cols(dbm[0], dbf[0, :H], f_off), conv_a_b=dcab[0], ln_conv_g=dlcg[0], ln_conv_b=dlcb[0],
                      b_mix_out=dbmo[0], ln1_g=dg1[0], ln1_b=db1[0], ffn_conv_b=jnp.concatenate([dcg[0], dcl[0]]),
                      ln2_g=dg2[0], ln2_b=db2[0], conv_a_w=dcaw[:KW], ffn_conv_w=jnp.concatenate([dwg[:KF], dwl[:KF]], axis=1))
    grad_x = dx.reshape(Bl, S, D)

    small_names = ["b_in", "conv_a_b", "ln_conv_g", "ln_conv_b", "b_mix_out", "ln1_g", "ln1_b", "ffn_conv_b", "ln2_g", "ln2_b",
                   "conv_a_w", "ffn_conv_w"]
    gs_list = [jnp.stack(dmods)] + [jnp.stack([gsm[l][n] for l in range(L)]) for n in small_names]
    gspack, gssegs = _pack(gs_list, D, F32)
    gs_all = _gather_two_level(gspack, name="gather_small_grads", after=token)
    dmod_all = jnp.moveaxis(_unpack(gs_all, gssegs)[0], 0, 1).reshape(L, N_DEV * Bl, 6 * D)
    g_small = dict(zip(small_names, _unpack(_slot_sum(gs_all, name="sum_small_grads"), gssegs)[1:]))
    g_small["conv_a_w"] = my_cols(g_small["conv_a_w"], C // N_DEV)
    g_small["ffn_conv_w"] = my_cols(g_small["ffn_conv_w"], 2 * F // N_DEV)
    g_small["w_ada"] = _ada_bwd(c_all, my_cols(dmod_all, n_ada), name="ada_bwd")
    g_small["b_ada"] = jnp.stack([_colsum(dmod_all[l], name=f"db_ada_{l}")[0] for l in range(L)])

    given = dict(w_in=(w_in, m_w_in, v_w_in), w_conv_proj=(w_conv_proj, m_w_conv_proj, v_w_conv_proj),
                 w_attn_proj=(w_attn_proj, m_w_attn_proj, v_w_attn_proj), w_mix_out=(w_mix_out, m_w_mix_out, v_w_mix_out),
                 w_ffn_up=(w_ffn_up, m_w_ffn_up, v_w_ffn_up), w_ffn_down=(w_ffn_down, m_w_ffn_down, v_w_ffn_down),
                 w_ada=(w_ada, m_w_ada, v_w_ada), b_ada=(b_ada, m_b_ada, v_b_ada), b_in=(b_in, m_b_in, v_b_in),
                 conv_a_w=(conv_a_w, m_conv_a_w, v_conv_a_w), conv_a_b=(conv_a_b, m_conv_a_b, v_conv_a_b),
                 ln_conv_g=(ln_conv_g, m_ln_conv_g, v_ln_conv_g), ln_conv_b=(ln_conv_b, m_ln_conv_b, v_ln_conv_b),
                 b_mix_out=(b_mix_out, m_b_mix_out, v_b_mix_out), ln1_g=(ln1_g, m_ln1_g, v_ln1_g), ln1_b=(ln1_b, m_ln1_b, v_ln1_b),
                 ffn_conv_w=(ffn_conv_w, m_ffn_conv_w, v_ffn_conv_w), ffn_conv_b=(ffn_conv_b, m_ffn_conv_b, v_ffn_conv_b),
                 ln2_g=(ln2_g, m_ln2_g, v_ln2_g), ln2_b=(ln2_b, m_ln2_b, v_ln2_b))
    res, kinds = {}, ("grad", "delta", "new_m", "new_v")
    loc_names = ["b_ada"] + small_names
    deltas, new_ms, new_vs = _adamw_many([g_small[n] for n in loc_names], *([given[n][j] for n in loc_names] for j in range(3)),
                                         name="adamw_small")
    for n, d, m2, v2 in zip(loc_names, deltas, new_ms, new_vs):
        res["grad", n], res["delta", n], res["new_m", n], res["new_v", n] = g_small[n], d, m2, v2
    rows_ada = (L * D * n_ada // D, D)
    outs = _adamw(g_small["w_ada"].reshape((1,) + rows_ada), *(a.reshape(rows_ada) for a in given["w_ada"]), name="adamw_w_ada")
    for kind, a in zip(kinds, outs):
        res[kind, "w_ada"] = a.reshape(w_ada.shape)

    big_parts = {}
    after = outs[0]
    for gi, (state, send, segs, grp) in enumerate(g_state):
        landed = _exchange_wait(state, after, name=f"exchange_grads_wait_{gi}")
        wmv = [_pack(shard_items([given[n][j] for n in big_names], grp), D, F32)[0] for j in range(3)]
        outs = _adamw(landed, *wmv, name=f"adamw_big_{gi}")
        for kind, packed in zip(kinds, outs):
            for (l, i), a in zip(grp, _unpack(packed, segs)):
                big_parts[kind, l, i] = a.T if transposed[i] else a
        after = outs[0]
    for kind in kinds:
        for i, n in enumerate(big_names):
            res[kind, n] = jnp.stack([big_parts[kind, l, i] for l in range(L)])

    order = ["w_ada", "b_ada", "w_in", "b_in", "conv_a_w", "conv_a_b", "ln_conv_g", "ln_conv_b", "w_conv_proj", "w_attn_proj",
             "w_mix_out", "b_mix_out", "ln1_g", "ln1_b", "w_ffn_up", "ffn_conv_w", "ffn_conv_b", "w_ffn_down", "ln2_g", "ln2_b"]
    return (loss, grad_x, *[res[k, n] for k in ("grad", "delta", "new_m", "new_v") for n in order])
```

```python
import functools
import math
from typing import NamedTuple

import jax
import jax.numpy as jnp
from jax import lax
from jax.experimental import pallas as pl
from jax.experimental.pallas import tpu as pltpu

F32, BF16 = jnp.float32, jnp.bfloat16
LN_EPS = 1e-5
ADAM_LR, ADAM_B1, ADAM_B2, ADAM_EPS, ADAM_WD, ADAM_STEP = 0.001, 0.9, 0.999, 1e-08, 0.01, 10
N_DEV = 8
LANES = 128
VMEM_LIMIT = 56 * 1024 * 1024
NEG = -1e30
NT = (((1,), (1,)), ((), ()))
TN = (((0,), (0,)), ((), ()))


class Cfg(NamedTuple):
    L: int
    Bl: int
    S: int
    D: int
    C: int
    KW: int
    H: int
    Dh: int
    F: int
    KF: int

    @property
    def T(self): return self.Bl * self.S
    @property
    def AW(self): return self.H * self.Dh
    @property
    def NM(self): return 2 * self.C + 3 * self.AW + 2 * self.D
    @property
    def q_off(self): return 2 * self.C
    @property
    def g_off(self): return 2 * self.C + 3 * self.AW
    @property
    def alpha(self): return (2.0 * self.L) ** 0.25


def _pcall(body, **kw):
    return pl.pallas_call(body, **kw)


def _params(*sem):
    return pltpu.CompilerParams(dimension_semantics=sem, vmem_limit_bytes=VMEM_LIMIT)


def _pick(n, prefs):
    for p in prefs:
        if n % p == 0:
            return p
    return n


def _sigmoid(x):
    return 1.0 / (1.0 + jnp.exp(-x))


def _ln_stats(x):
    mu = jnp.mean(x, axis=-1, keepdims=True)
    xc = x - mu
    var = jnp.mean(xc * xc, axis=-1, keepdims=True)
    rstd = lax.rsqrt(var + LN_EPS)
    return xc * rstd, rstd


def _ln_bwd(dxh, xh, rstd):
    return rstd * (dxh - jnp.mean(dxh, axis=-1, keepdims=True) - xh * jnp.mean(dxh * xh, axis=-1, keepdims=True))


def _matmul(a, b, *, mode, name, bias=None, add=None, out_dtype=F32, tm=None, tn=None, tk=None, after=None):
    parts = tuple(a) if isinstance(a, (tuple, list)) else (a,)
    P = len(parts)
    if mode == "tn":
        K, Mp = parts[0].shape
        M, Kp = P * Mp, K
    else:
        M, Kp = parts[0].shape
        K, Mp = P * Kp, M
    N = b.shape[0] if mode == "nt" else b.shape[1]
    lane_tiles = (1536, 1408, 1024, 768, 512, 256, 128)
    tm = tm or _pick(Mp, lane_tiles if mode == "tn" else (1024, 512, 256, 128, 64, 32, 16, 8))
    tn = tn or _pick(N, lane_tiles)
    tk = tk or _pick(Kp, (1024, 512, 256, 128) if mode == "tn" else lane_tiles)
    nk = K // tk
    per = Mp // tm if mode == "tn" else Kp // tk
    dn = {"nn": (((1,), (0,)), ((), ())), "nt": NT, "tn": TN}[mode]
    has_bias, has_add, has_after = bias is not None, add is not None, after is not None

    def body(*refs):
        a_refs, b_ref = refs[:P], refs[P]
        pos = P + 1
        bias_ref = refs[pos] if has_bias else None
        pos += has_bias
        add_ref = refs[pos] if has_add else None
        pos += has_add + has_after
        o_ref = refs[pos]
        acc_ref = refs[pos + 1] if nk > 1 else None
        k = pl.program_id(2)

        def finish(acc):
            if has_bias:
                acc = acc + bias_ref[...]
            if has_add:
                acc = acc + add_ref[...]
            o_ref[...] = acc.astype(out_dtype)

        def accumulate(a_ref):
            part = lax.dot_general(a_ref[...], b_ref[...], dn, preferred_element_type=F32)
            if nk == 1:
                finish(part)
            else:
                @pl.when(k == 0)
                def _():
                    acc_ref[...] = part

                @pl.when(k > 0)
                def _():
                    acc_ref[...] += part

        if P == 1:
            accumulate(a_refs[0])
        else:
            step = pl.program_id(0 if mode == "tn" else 2)
            for p in range(P):
                pl.when(step // per == p)(functools.partial(accumulate, a_refs[p]))
        if nk > 1:
            @pl.when(k == nk - 1)
            def _():
                finish(acc_ref[...])

    def a_spec(p):
        if mode == "tn":
            return pl.BlockSpec((tk, tm), lambda i, j, k: (k, jnp.clip(i - p * per, 0, per - 1)))
        return pl.BlockSpec((tm, tk), lambda i, j, k: (i, jnp.clip(k - p * per, 0, per - 1)))

    b_spec = pl.BlockSpec((tn, tk), lambda i, j, k: (j, k)) if mode == "nt" else pl.BlockSpec((tk, tn), lambda i, j, k: (k, j))
    in_specs, args = [a_spec(p) for p in range(P)] + [b_spec], list(parts) + [b]
    if has_bias:
        in_specs.append(pl.BlockSpec((1, tn), lambda i, j, k: (0, j)))
        args.append(bias)
    if has_add:
        in_specs.append(pl.BlockSpec((tm, tn), lambda i, j, k: (i, j)))
        args.append(add)
    if has_after:
        in_specs.append(pl.BlockSpec(memory_space=pl.ANY))
        args.append(after)
    return _pcall(
        body, name=name, grid=(M // tm, N // tn, nk), in_specs=in_specs,
        out_specs=pl.BlockSpec((tm, tn), lambda i, j, k: (i, j)),
        out_shape=jax.ShapeDtypeStruct((M, N), out_dtype),
        scratch_shapes=[pltpu.VMEM((tm, tn), F32)] if nk > 1 else [],
        compiler_params=_params("parallel", "parallel", "arbitrary"),
    )(*args)


def _colsum(x, *, name):
    T, N = x.shape
    tr = _pick(T, (512, 256, 128, 64, 32, 16))
    tc = _pick(N, (1536, 1024, 512, 256, 128))

    def body(x_ref, o_ref):
        @pl.when(pl.program_id(1) == 0)
        def _():
            o_ref[...] = jnp.zeros_like(o_ref)

        o_ref[...] += jnp.sum(x_ref[...].astype(F32), axis=0, keepdims=True)

    return _pcall(body, name=name, grid=(N // tc, T // tr), in_specs=[pl.BlockSpec((tr, tc), lambda j, i: (i, j))],
                  out_specs=pl.BlockSpec((1, tc), lambda j, i: (0, j)), out_shape=jax.ShapeDtypeStruct((1, N), F32),
                  compiler_params=_params("parallel", "arbitrary"))(x)


def _row_tile(cfg):
    return _pick(cfg.S, (512, 256, 128, 64, 32, 16, 8))


def _ln_mod_fwd(x, shift, scale, cfg, *, name):
    tr = _row_tile(cfg)
    tpb = cfg.S // tr

    def body(x_ref, sh_ref, sc_ref, u_ref):
        xh, _ = _ln_stats(x_ref[...])
        u_ref[...] = (xh * (1.0 + sc_ref[0]) + sh_ref[0]).astype(BF16)

    row = pl.BlockSpec((tr, cfg.D), lambda i: (i, 0))
    per_b = pl.BlockSpec((1, 1, cfg.D), lambda i: (i // tpb, 0, 0))
    return _pcall(body, name=name, grid=(cfg.T // tr,), in_specs=[row, per_b, per_b], out_specs=row,
                  out_shape=jax.ShapeDtypeStruct((cfg.T, cfg.D), BF16), compiler_params=_params("parallel"))(x, shift, scale)


def _res_ln_fwd(xin, br, gate, g, b, cfg, *, name, nxt=None):
    tr = _row_tile(cfg)
    tpb = cfg.S // tr
    alpha = cfg.alpha

    def body(*refs):
        x_ref, br_ref, gt_ref, g_ref, b_ref = refs[:5]
        r = alpha * x_ref[...] + (1.0 + gt_ref[0]) * br_ref[...]
        xh, _ = _ln_stats(r)
        xo = xh * g_ref[...] + b_ref[...]
        if nxt is None:
            refs[5][...] = xo
        else:
            sh_ref, sc_ref, xo_ref, u_ref = refs[5:]
            xo_ref[...] = xo
            uh, _ = _ln_stats(xo)
            u_ref[...] = (uh * (1.0 + sc_ref[0]) + sh_ref[0]).astype(BF16)

    row = pl.BlockSpec((tr, cfg.D), lambda i: (i, 0))
    per_b = pl.BlockSpec((1, 1, cfg.D), lambda i: (i // tpb, 0, 0))
    vec = pl.BlockSpec((1, cfg.D), lambda i: (0, 0))
    in_specs, args = [row, row, per_b, vec, vec], [xin, br, gate, g, b]
    out_specs, out_shape = row, jax.ShapeDtypeStruct((cfg.T, cfg.D), F32)
    if nxt is not None:
        in_specs += [per_b, per_b]
        args += list(nxt)
        out_specs = [row, row]
        out_shape = [out_shape, jax.ShapeDtypeStruct((cfg.T, cfg.D), BF16)]
    return _pcall(body, name=name, grid=(cfg.T // tr,), in_specs=in_specs, out_specs=out_specs, out_shape=out_shape,
                  compiler_params=_params("parallel"))(*args)


def _loss_grad(y, tgt, cfg, *, name):
    tr = _row_tile(cfg)
    nt = cfg.T // tr
    inv_d = 1.0 / cfg.D

    def body(y_ref, t_ref, dy_ref, ls_ref):
        e = y_ref[...] - t_ref[...]
        dy_ref[...] = e * inv_d
        ls_ref[...] = jnp.full((1, 1, LANES), jnp.sum(e * e), F32)

    row = pl.BlockSpec((tr, cfg.D), lambda i: (i, 0))
    return _pcall(body, name=name, grid=(nt,), in_specs=[row, row],
                  out_specs=[row, pl.BlockSpec((1, 1, LANES), lambda i: (i, 0, 0))],
                  out_shape=[jax.ShapeDtypeStruct((cfg.T, cfg.D), F32), jax.ShapeDtypeStruct((nt, 1, LANES), F32)],
                  compiler_params=_params("parallel"))(y, tgt)


def _res_ln_bwd(dy, xin, br, gate, g, cfg, *, name, mod=None):
    tr = _row_tile(cfg)
    tpb = cfg.S // tr
    alpha = cfg.alpha
    fused = mod is not None

    def body(*refs):
        if fused:
            du_ref, xa_ref, sc_ref, dres_ref, x_ref, br_ref, gt_ref, g_ref = refs[:8]
            dx_ref, dbr_ref, dg_ref, db_ref, dgt_ref, dbs_ref, dsc_ref, dsh_ref = refs[8:]
        else:
            dy_ref, x_ref, br_ref, gt_ref, g_ref, dx_ref, dbr_ref, dg_ref, db_ref, dgt_ref, dbs_ref = refs
        i = pl.program_id(0)

        @pl.when(i == 0)
        def _():
            dg_ref[...] = jnp.zeros_like(dg_ref)
            db_ref[...] = jnp.zeros_like(db_ref)
            dbs_ref[...] = jnp.zeros_like(dbs_ref)

        @pl.when(i % tpb == 0)
        def _():
            dgt_ref[...] = jnp.zeros_like(dgt_ref)
            if fused:
                dsc_ref[...] = jnp.zeros_like(dsc_ref)
                dsh_ref[...] = jnp.zeros_like(dsh_ref)

        if fused:
            du = du_ref[...]
            ah, arstd = _ln_stats(xa_ref[...])
            dsc_ref[0] += jnp.sum(du * ah, axis=0, keepdims=True)
            dsh_ref[0] += jnp.sum(du, axis=0, keepdims=True)
            dy = _ln_bwd(du * (1.0 + sc_ref[0]), ah, arstd) + dres_ref[...]
        else:
            dy = dy_ref[...]
        brv, one_gate = br_ref[...], 1.0 + gt_ref[0]
        xh, rstd = _ln_stats(alpha * x_ref[...] + one_gate * brv)
        dg_ref[...] += jnp.sum(dy * xh, axis=0, keepdims=True)
        db_ref[...] += jnp.sum(dy, axis=0, keepdims=True)
        dr = _ln_bwd(dy * g_ref[...], xh, rstd)
        dx_ref[...] = alpha * dr
        dbr = one_gate * dr
        dbr_ref[...] = dbr.astype(BF16)
        dbs_ref[...] += jnp.sum(dbr, axis=0, keepdims=True)
        dgt_ref[0] += jnp.sum(dr * brv, axis=0, keepdims=True)

    row = pl.BlockSpec((tr, cfg.D), lambda i: (i, 0))
    per_b = pl.BlockSpec((1, 1, cfg.D), lambda i: (i // tpb, 0, 0))
    vec = pl.BlockSpec((1, cfg.D), lambda i: (0, 0))
    vs = jax.ShapeDtypeStruct((1, cfg.D), F32)
    bs = jax.ShapeDtypeStruct((cfg.Bl, 1, cfg.D), F32)
    in_specs, args = [row, row, row, per_b, vec], [dy, xin, br, gate, g]
    out_specs = [row, row, vec, vec, per_b, vec]
    out_shape = [jax.ShapeDtypeStruct((cfg.T, cfg.D), F32), jax.ShapeDtypeStruct((cfg.T, cfg.D), BF16), vs, vs, bs, vs]
    if fused:
        in_specs, args = [row, row, per_b, row] + in_specs[1:], list(mod) + args[1:]
        out_specs, out_shape = out_specs + [per_b, per_b], out_shape + [bs, bs]
    return _pcall(body, name=name, grid=(cfg.T // tr,), in_specs=in_specs, out_specs=out_specs, out_shape=out_shape,
                  compiler_params=_params("arbitrary"))(*args)


def _ln_mod_bwd(du, xin, scale, dres, cfg, *, name):
    tr = _row_tile(cfg)
    tpb = cfg.S // tr

    def body(du_ref, x_ref, sc_ref, dres_ref, dx_ref, dsc_ref, dsh_ref):
        @pl.when(pl.program_id(0) % tpb == 0)
        def _():
            dsc_ref[...] = jnp.zeros_like(dsc_ref)
            dsh_ref[...] = jnp.zeros_like(dsh_ref)

        du = du_ref[...]
        xh, rstd = _ln_stats(x_ref[...])
        dsc_ref[0] += jnp.sum(du * xh, axis=0, keepdims=True)
        dsh_ref[0] += jnp.sum(du, axis=0, keepdims=True)
        dx_ref[...] = _ln_bwd(du * (1.0 + sc_ref[0]), xh, rstd) + dres_ref[...]

    row = pl.BlockSpec((tr, cfg.D), lambda i: (i, 0))
    per_b = pl.BlockSpec((1, 1, cfg.D), lambda i: (i // tpb, 0, 0))
    bs = jax.ShapeDtypeStruct((cfg.Bl, 1, cfg.D), F32)
    return _pcall(body, name=name, grid=(cfg.T // tr,), in_specs=[row, row, per_b, row], out_specs=[row, per_b, per_b],
                  out_shape=[jax.ShapeDtypeStruct((cfg.T, cfg.D), F32), bs, bs],
                  compiler_params=_params("arbitrary"))(du, xin, scale, dres)


def _merge_tiles(cfg):
    tr = _pick(cfg.T, (512, 256, 128, 64, 32, 16))
    tc = _pick(math.gcd(cfg.g_off, cfg.D), (512, 256, 128))
    return tr, tc


def _merge_fwd(zm, ya, yb, cfg, *, name):
    tr, tc = _merge_tiles(cfg)
    ga0, gb0 = cfg.g_off // tc, (cfg.g_off + cfg.D) // tc

    def body(ga_ref, gb_ref, ya_ref, yb_ref, m_ref):
        m_ref[...] = (_sigmoid(ga_ref[...]) * ya_ref[...] + _sigmoid(gb_ref[...]) * yb_ref[...]).astype(BF16)

    blk = pl.BlockSpec((tr, tc), lambda i, j: (i, j))
    return _pcall(body, name=name, grid=(cfg.T // tr, cfg.D // tc),
                  in_specs=[pl.BlockSpec((tr, tc), lambda i, j: (i, ga0 + j)), pl.BlockSpec((tr, tc), lambda i, j: (i, gb0 + j)), blk, blk],
                  out_specs=blk, out_shape=jax.ShapeDtypeStruct((cfg.T, cfg.D), BF16),
                  compiler_params=_params("parallel", "parallel"))(zm, zm, ya, yb)


def _merge_bwd(dm, zm, ya, yb, cfg, *, name):
    tr, tc = _merge_tiles(cfg)
    ga0, gb0 = cfg.g_off // tc, (cfg.g_off + cfg.D) // tc

    def body(dm_ref, ga_ref, gb_ref, ya_ref, yb_ref, dya_ref, dyb_ref, dga_ref, dgb_ref):
        dm = dm_ref[...]
        ga, gb = _sigmoid(ga_ref[...]), _sigmoid(gb_ref[...])
        dya_ref[...] = (dm * ga).astype(BF16)
        dyb_ref[...] = (dm * gb).astype(BF16)
        dga_ref[...] = (dm * ya_ref[...] * ga * (1.0 - ga)).astype(BF16)
        dgb_ref[...] = (dm * yb_ref[...] * gb * (1.0 - gb)).astype(BF16)

    blk = pl.BlockSpec((tr, tc), lambda i, j: (i, j))
    o = jax.ShapeDtypeStruct((cfg.T, cfg.D), BF16)
    return _pcall(body, name=name, grid=(cfg.T // tr, cfg.D // tc),
                  in_specs=[blk, pl.BlockSpec((tr, tc), lambda i, j: (i, ga0 + j)), pl.BlockSpec((tr, tc), lambda i, j: (i, gb0 + j)), blk, blk],
                  out_specs=[blk] * 4, out_shape=[o] * 4, compiler_params=_params("parallel", "parallel"))(dm, zm, zm, ya, yb)


CONV_A_HALO = 32
CONV_A_CHUNK = 32
CONV_A_TAPS = 32
FFN_TAPS = 8


SUBLANES = 8


def _conv_a_tile(cfg):
    assert cfg.KW - 1 <= CONV_A_HALO
    return _pick(cfg.S, (256, 128, 64, 32))


def _shift_copies(src_s, sh_s):
    rows = src_s.shape[0] - SUBLANES
    for b in range(1, SUBLANES):
        sh_s[b - 1, :, :] = src_s[b:b + rows, :]


def _rows(src_s, sh_s, start, n):
    a, b = divmod(start, SUBLANES)
    return src_s[start:start + n, :] if b == 0 else sh_s[b - 1, SUBLANES * a:SUBLANES * a + n, :]


def _conv_a_fwd(zm, w, cb, g, b, cfg, *, name):
    C, KW, HALO, CH = cfg.C, cfg.KW, CONV_A_HALO, CONV_A_CHUNK
    ts = _conv_a_tile(cfg)
    tpb = cfg.S // ts
    lead = HALO - (KW - 1)

    def body(z_ref, zp_ref, w_ref, cb_ref, g_ref, b_ref, o_ref, a0_s, a0_sh):
        first = pl.program_id(0) % tpb == 0
        prev = zp_ref[:, :C] * _sigmoid(zp_ref[:, C:])
        a0_s[0:HALO, :] = jnp.where(first, 0.0, prev)
        a0_s[HALO:HALO + ts, :] = z_ref[:, :C] * _sigmoid(z_ref[:, C:])
        _shift_copies(a0_s, a0_sh)
        for r0 in range(0, ts, CH):
            acc = jnp.zeros((CH, C), F32)
            for k in range(KW):
                acc = acc + w_ref[k:k + 1, :] * _rows(a0_s, a0_sh, r0 + lead + k, CH)
            xh, _ = _ln_stats(acc + cb_ref[...])
            a2 = xh * g_ref[...] + b_ref[...]
            o_ref[r0:r0 + CH, :] = (a2 * _sigmoid(a2)).astype(BF16)

    hb = ts // HALO
    vec = pl.BlockSpec((1, C), lambda i: (0, 0))
    return _pcall(body, name=name, grid=(cfg.T // ts,),
                  in_specs=[pl.BlockSpec((ts, 2 * C), lambda i: (i, 0)),
                            pl.BlockSpec((HALO, 2 * C), lambda i: (jnp.maximum(i * hb - 1, 0), 0)),
                            pl.BlockSpec((CONV_A_TAPS, C), lambda i: (0, 0)), vec, vec, vec],
                  out_specs=pl.BlockSpec((ts, C), lambda i: (i, 0)), out_shape=jax.ShapeDtypeStruct((cfg.T, C), BF16),
                  scratch_shapes=[pltpu.VMEM((HALO + ts, C), F32), pltpu.VMEM((SUBLANES - 1, HALO + ts - SUBLANES, C), F32)],
                  compiler_params=_params("parallel"))(zm, zm, w, cb, g, b)


def _conv_a_bwd(da3, zm, w, cb, g, b, cfg, *, name):
    C, KW, HALO, CH = cfg.C, cfg.KW, CONV_A_HALO, CONV_A_CHUNK
    ts = _conv_a_tile(cfg)
    tpb = cfg.S // ts
    nt = cfg.T // ts
    lead = HALO - (KW - 1)
    ext = ts + HALO

    def body(z_ref, zp_ref, zn_ref, d_ref, dn_ref, w_ref, cb_ref, g_ref, b_ref,
             dz_ref, dw_ref, dcb_ref, dg_ref, db_ref, a0_s, d3_s, da1_s, a0_sh, da1_sh):
        i = pl.program_id(0)
        first, last = i % tpb == 0, i % tpb == tpb - 1

        @pl.when(i == 0)
        def _():
            dw_ref[...] = jnp.zeros_like(dw_ref)
            dcb_ref[...] = jnp.zeros_like(dcb_ref)
            dg_ref[...] = jnp.zeros_like(dg_ref)
            db_ref[...] = jnp.zeros_like(db_ref)

        a0_s[0:HALO, :] = jnp.where(first, 0.0, zp_ref[:, :C] * _sigmoid(zp_ref[:, C:]))
        a0_s[HALO:HALO + ts, :] = z_ref[:, :C] * _sigmoid(z_ref[:, C:])
        a0_s[HALO + ts:HALO + ext, :] = zn_ref[:, :C] * _sigmoid(zn_ref[:, C:])
        d3_s[0:ts, :] = d_ref[...]
        d3_s[ts:ext, :] = jnp.where(last, 0.0, dn_ref[...])
        _shift_copies(a0_s, a0_sh)
        dcb, dg, db = jnp.zeros((1, C), F32), jnp.zeros((1, C), F32), jnp.zeros((1, C), F32)
        for r0 in range(0, ext, CH):
            acc = jnp.zeros((CH, C), F32)
            for k in range(KW):
                acc = acc + w_ref[k:k + 1, :] * _rows(a0_s, a0_sh, r0 + lead + k, CH)
            xh, rstd = _ln_stats(acc + cb_ref[...])
            a2 = xh * g_ref[...] + b_ref[...]
            sg = _sigmoid(a2)
            da2 = d3_s[r0:r0 + CH, :] * (sg * (1.0 + a2 * (1.0 - sg)))
            da1 = _ln_bwd(da2 * g_ref[...], xh, rstd)
            da1_s[r0:r0 + CH, :] = da1
            if r0 < ts:
                dg = dg + jnp.sum(da2 * xh, axis=0, keepdims=True)
                db = db + jnp.sum(da2, axis=0, keepdims=True)
                dcb = dcb + jnp.sum(da1, axis=0, keepdims=True)
        dg_ref[...] += dg
        db_ref[...] += db
        dcb_ref[...] += dcb
        _shift_copies(da1_s, da1_sh)
        for k in range(KW):
            dwk = jnp.zeros((CH, C), F32)
            for r0 in range(0, ts, CH):
                dwk = dwk + da1_s[r0:r0 + CH, :] * _rows(a0_s, a0_sh, r0 + lead + k, CH)
            dw_ref[k:k + 1, :] += jnp.sum(dwk, axis=0, keepdims=True)
        for r0 in range(0, ts, CH):
            da0 = jnp.zeros((CH, C), F32)
            for k in range(KW):
                da0 = da0 + w_ref[k:k + 1, :] * _rows(da1_s, da1_sh, r0 + KW - 1 - k, CH)
            val, sg = z_ref[r0:r0 + CH, :C], _sigmoid(z_ref[r0:r0 + CH, C:])
            dz_ref[r0:r0 + CH, :C] = (da0 * sg).astype(BF16)
            dz_ref[r0:r0 + CH, C:] = (da0 * val * sg * (1.0 - sg)).astype(BF16)

    hb = ts // HALO
    nhb = cfg.T // HALO
    vec = pl.BlockSpec((1, C), lambda i: (0, 0))
    vs = jax.ShapeDtypeStruct((1, C), F32)
    return _pcall(body, name=name, grid=(nt,),
                  in_specs=[pl.BlockSpec((ts, 2 * C), lambda i: (i, 0)),
                            pl.BlockSpec((HALO, 2 * C), lambda i: (jnp.maximum(i * hb - 1, 0), 0)),
                            pl.BlockSpec((HALO, 2 * C), lambda i: (jnp.minimum((i + 1) * hb, nhb - 1), 0)),
                            pl.BlockSpec((ts, C), lambda i: (i, 0)),
                            pl.BlockSpec((HALO, C), lambda i: (jnp.minimum((i + 1) * hb, nhb - 1), 0)),
                            pl.BlockSpec((CONV_A_TAPS, C), lambda i: (0, 0)), vec, vec, vec],
                  out_specs=[pl.BlockSpec((ts, 2 * C), lambda i: (i, 0)), pl.BlockSpec((CONV_A_TAPS, C), lambda i: (0, 0)), vec, vec, vec],
                  out_shape=[jax.ShapeDtypeStruct((cfg.T, 2 * C), BF16), jax.ShapeDtypeStruct((CONV_A_TAPS, C), F32), vs, vs, vs],
                  scratch_shapes=[pltpu.VMEM((HALO + ext, C), F32), pltpu.VMEM((ext, C), F32), pltpu.VMEM((ext, C), F32),
                                  pltpu.VMEM((SUBLANES - 1, HALO + ext - SUBLANES, C), F32),
                                  pltpu.VMEM((SUBLANES - 1, ext - SUBLANES, C), F32)],
                  compiler_params=_params("arbitrary"))(zm, zm, zm, da3, da3, w, cb, g, b)


def _cum_tile(cfg):
    return _pick(cfg.S, (256, 128, 64, 32, 16, 8))


def _fgate_fwd(zf, cfg, *, name):
    tc = _cum_tile(cfg)
    tpb = cfg.S // tc
    hp = _attn_tiles(cfg)[2]
    nb = cfg.H // hp

    def body(z_ref, o_ref, carry):
        @pl.when(pl.program_id(0) % tpb == 0)
        def _():
            carry[...] = jnp.zeros_like(carry)

        z = z_ref[...]
        logf = jnp.minimum(z, 0.0) - jnp.log(1.0 + jnp.exp(-jnp.abs(z)))
        tri = (lax.broadcasted_iota(jnp.int32, (tc, tc), 0) >= lax.broadcasted_iota(jnp.int32, (tc, tc), 1)).astype(F32)
        cum = jnp.dot(tri, logf, precision=lax.Precision.HIGHEST, preferred_element_type=F32) + carry[...]
        carry[...] = cum[tc - 1:tc, :]
        o_ref[0] = cum
        for b in range(1, nb):
            o_ref[b] = pltpu.roll(cum, LANES - hp * b, axis=1)

    return _pcall(body, name=name, grid=(cfg.T // tc,), in_specs=[pl.BlockSpec((tc, LANES), lambda i: (i, 0))],
                  out_specs=pl.BlockSpec((nb, tc, LANES), lambda i: (0, i, 0)),
                  out_shape=jax.ShapeDtypeStruct((nb, cfg.T, LANES), F32), scratch_shapes=[pltpu.VMEM((1, LANES), F32)],
                  compiler_params=_params("arbitrary"))(zf)


def _fgate_bwd(dcum_c, zf, cfg, *, name):
    tc = _cum_tile(cfg)
    tpb = cfg.S // tc
    nt = cfg.T // tc
    hp = _attn_tiles(cfg)[2]
    nb = cfg.H // hp

    def body(d_ref, z_ref, o_ref, carry):
        @pl.when(pl.program_id(0) % tpb == 0)
        def _():
            carry[...] = jnp.zeros_like(carry)

        d = d_ref[0]
        for b in range(1, nb):
            d = d + pltpu.roll(d_ref[b], hp * b, axis=1)
        tri = (lax.broadcasted_iota(jnp.int32, (tc, tc), 0) <= lax.broadcasted_iota(jnp.int32, (tc, tc), 1)).astype(F32)
        suf = jnp.dot(tri, d, precision=lax.Precision.HIGHEST, preferred_element_type=F32) + carry[...]
        o_ref[...] = (suf * _sigmoid(-z_ref[...])).astype(BF16)
        carry[...] = suf[0:1, :]

    blk = pl.BlockSpec((tc, LANES), lambda i: (nt - 1 - i, 0))
    return _pcall(body, name=name, grid=(nt,), in_specs=[pl.BlockSpec((nb, tc, LANES), lambda i: (0, nt - 1 - i, 0)), blk],
                  out_specs=blk, out_shape=jax.ShapeDtypeStruct((cfg.T, LANES), BF16),
                  scratch_shapes=[pltpu.VMEM((1, LANES), F32)], compiler_params=_params("arbitrary"))(dcum_c, zf)


def _attn_tiles(cfg):
    assert LANES % cfg.Dh == 0 and cfg.H % (LANES // cfg.Dh) == 0
    tk = _pick(cfg.S, (256, 128))
    tq = _pick(cfg.S, (2 * tk, tk))
    return tq, tk, LANES // cfg.Dh


BIAS_LANES = 3


def _head_lanes(hd, cfg, hp):
    li = lax.broadcasted_iota(jnp.int32, (1, LANES), 1)
    own = (li >= hd * cfg.Dh) & (li < (hd + 1) * cfg.Dh)
    return own, li, ((hd + 1) % hp) * cfg.Dh


def _q_aug(q, hd, cfg, hp):
    own, li, b0 = _head_lanes(hd, cfg, hp)
    ones = ((li >= b0) & (li < b0 + BIAS_LANES)).astype(F32)
    return jnp.where(own, q * cfg.Dh ** -0.5, ones).astype(BF16)


def _k_aug(k, ck, hd, cfg, hp):
    own, li, b0 = _head_lanes(hd, cfg, hp)
    hi = ck.astype(BF16).astype(F32)
    mid = (ck - hi).astype(BF16).astype(F32)
    lo = ck - hi - mid
    bias = jnp.where(li == b0, -hi, jnp.where(li == b0 + 1, -mid, jnp.where(li == b0 + 2, -lo, 0.0)))
    return jnp.where(own, k, bias).astype(BF16)


def _attn_fwd(zm, cum_c, cfg, *, name):
    S, Dh = cfg.S, cfg.Dh
    tq, tk, hp = _attn_tiles(cfg)
    assert hp >= 2
    nq, nb, per = S // tq, cfg.H // hp, tq // tk
    qb, kb, vb = cfg.q_off // LANES, (cfg.q_off + cfg.AW) // LANES, (cfg.q_off + 2 * cfg.AW) // LANES

    def body(q_ref, k_ref, v_ref, cc_ref, o_ref, o32_ref, lse_ref, ka_s, vt_s):
        qi = pl.program_id(2)

        @pl.when(qi == 0)
        def _():
            def prep(c, _):
                r = pl.multiple_of(c * tk, tk)
                kc = k_ref[pl.ds(r, tk), :]
                for hd in range(hp):
                    ka_s[hd, pl.ds(r, tk), :] = _k_aug(kc, cc_ref[0, pl.ds(r, tk), hd:hd + 1], hd, cfg, hp)
                vt_s[:, pl.ds(r, tk)] = v_ref[pl.ds(r, tk), :].T.astype(BF16)
                return 0

            lax.fori_loop(0, S // tk, prep, 0)

        key_i = lax.broadcasted_iota(jnp.int32, (tk, tq), 0)
        qry_i = lax.broadcasted_iota(jnp.int32, (tk, tq), 1)
        qf = q_ref[...]
        qa = [_q_aug(qf, hd, cfg, hp) for hd in range(hp)]

        def scores(j):
            r = pl.multiple_of(j * tk, tk)
            return tuple(lax.dot_general(ka_s[hd, pl.ds(r, tk), :], qa[hd], NT, preferred_element_type=F32) for hd in range(hp))

        def chunk(j, s_all, carry, diag=None):
            r = pl.multiple_of(j * tk, tk)
            new = []
            for hd in range(hp):
                m, l, acc = carry[hd]
                s = s_all[hd]
                if diag is not None:
                    s = jnp.where(key_i + diag * tk <= qry_i, s, NEG)
                m_new = jnp.maximum(m, jnp.max(s, axis=0, keepdims=True))
                a = jnp.exp(m - m_new)
                p = jnp.exp(s - m_new)
                l = a * l + jnp.sum(p, axis=0, keepdims=True)
                p_hi = p.astype(BF16)
                p_lo = (p - p_hi.astype(F32)).astype(BF16)
                vt = vt_s[hd * Dh:(hd + 1) * Dh, pl.ds(r, tk)]
                acc = a * acc + (jnp.dot(vt, p_hi, preferred_element_type=F32) + jnp.dot(vt, p_lo, preferred_element_type=F32))
                new.append((m_new, l, acc))
            return tuple(new)

        init = tuple((jnp.full((1, tq), NEG, F32), jnp.zeros((1, tq), F32), jnp.zeros((Dh, tq), F32)) for _ in range(hp))
        n_full = qi * per

        def step(j, c):
            stats, s_cur = c
            s_next = scores(j + 1)
            return chunk(j, s_cur, stats), s_next

        res, s_cur = lax.fori_loop(0, n_full, step, (init, scores(0)))
        for d in range(per):
            s_next = scores(n_full + d + 1) if d + 1 < per else None
            res = chunk(n_full + d, s_cur, res, diag=d)
            s_cur = s_next
        o = jnp.concatenate([acc / l for _, l, acc in res], axis=0).T
        o_ref[...] = o.astype(BF16)
        o32_ref[...] = o
        lse_ref[...] = jnp.zeros_like(lse_ref)
        for hd in range(hp):
            lse_ref[0, 0, hd:hd + 1, :] = res[hd][0] + jnp.log(res[hd][1])

    return _pcall(body, name=name, grid=(cfg.Bl, nb, nq),
                  in_specs=[pl.BlockSpec((tq, LANES), lambda b, h, i: (b * nq + i, qb + h)),
                            pl.BlockSpec((S, LANES), lambda b, h, i: (b, kb + h)),
                            pl.BlockSpec((S, LANES), lambda b, h, i: (b, vb + h)),
                            pl.BlockSpec((1, S, LANES), lambda b, h, i: (h, b, 0))],
                  out_specs=[pl.BlockSpec((tq, LANES), lambda b, h, i: (b * nq + i, h)),
                             pl.BlockSpec((tq, LANES), lambda b, h, i: (b * nq + i, h)),
                             pl.BlockSpec((1, 1, SUBLANES, tq), lambda b, h, i: (b, h, 0, i))],
                  out_shape=[jax.ShapeDtypeStruct((cfg.T, cfg.AW), BF16), jax.ShapeDtypeStruct((cfg.T, cfg.AW), F32),
                             jax.ShapeDtypeStruct((cfg.Bl, nb, SUBLANES, S), F32)],
                  scratch_shapes=[pltpu.VMEM((hp, S, LANES), BF16), pltpu.VMEM((LANES, S), BF16)],
                  compiler_params=_params("parallel", "parallel", "arbitrary"))(zm, zm, zm, cum_c)


def _attn_bwd(zm, cum_c, o, do, lse, cfg, *, name):
    S, Dh = cfg.S, cfg.Dh
    tq, t, hp = _attn_tiles(cfg)
    nq, nk, nb, per = S // tq, S // t, cfg.H // hp, tq // t
    qb, kb, vb = cfg.q_off // LANES, (cfg.q_off + cfg.AW) // LANES, (cfg.q_off + 2 * cfg.AW) // LANES
    scale = Dh ** -0.5

    def body(q_ref, k_ref, v_ref, cc_ref, o_ref, do_ref, lse_ref, dq_ref, dk_ref, dv_ref, dcc_ref,
             ka_s, qa_s, vz_s, kt_s, dd_s, dqt_s):
        li = lax.broadcasted_iota(jnp.int32, (1, LANES), 1)
        ri = lax.broadcasted_iota(jnp.int32, (LANES, 1), 0)
        key_i = lax.broadcasted_iota(jnp.int32, (t, tq), 0)
        qry_i = lax.broadcasted_iota(jnp.int32, (t, tq), 1)

        def prep(c, _):
            r = pl.multiple_of(c * t, t)
            kc, vc, qc = k_ref[pl.ds(r, t), :], v_ref[pl.ds(r, t), :], q_ref[pl.ds(r, t), :]
            prod_t = (do_ref[pl.ds(r, t), :].astype(F32) * o_ref[pl.ds(r, t), :].astype(F32)).T
            for hd in range(hp):
                own = _head_lanes(hd, cfg, hp)[0]
                ka_s[hd, pl.ds(r, t), :] = _k_aug(kc, cc_ref[0, pl.ds(r, t), hd:hd + 1], hd, cfg, hp)
                qa_s[hd, pl.ds(r, t), :] = _q_aug(qc, hd, cfg, hp)
                vz_s[hd, pl.ds(r, t), :] = jnp.where(own, vc, 0.0).astype(BF16)
                dd_s[hd:hd + 1, pl.ds(r, t)] = jnp.sum(prod_t[hd * Dh:(hd + 1) * Dh, :], axis=0, keepdims=True)
            kt_s[:, pl.ds(r, t)] = kc.T.astype(BF16)
            dqt_s[:, pl.ds(r, t)] = jnp.zeros((LANES, t), F32)
            return 0

        lax.fori_loop(0, nk, prep, 0)

        def kv_step(j, _):
            rk = pl.multiple_of(j * t, t)
            i0 = j // per

            def tile(i, carry, masked):
                rq = pl.multiple_of(i * tq, tq)
                dob = do_ref[pl.ds(rq, tq), :]
                new, dq_t = [], None
                for hd in range(hp):
                    dk_h, dv_h, dsum_h = carry[hd]
                    qa = qa_s[hd, pl.ds(rq, tq), :]
                    s = lax.dot_general(ka_s[hd, pl.ds(rk, t), :], qa, NT, preferred_element_type=F32)
                    p = jnp.exp(s - lse_ref[0, 0, hd:hd + 1, pl.ds(rq, tq)])
                    if masked:
                        p = jnp.where(key_i + (rk - rq) <= qry_i, p, 0.0)
                    dp = lax.dot_general(vz_s[hd, pl.ds(rk, t), :], dob, NT, preferred_element_type=F32)
                    ds = p * (dp - dd_s[hd:hd + 1, pl.ds(rq, tq)])
                    dsb = ds.astype(BF16)
                    dv_h = dv_h + jnp.dot(p.astype(BF16), dob, preferred_element_type=F32)
                    dk_h = dk_h + jnp.dot(dsb, qa, preferred_element_type=F32)
                    dq_h = jnp.dot(kt_s[:, pl.ds(rk, t)], dsb, preferred_element_type=F32)
                    dq_t = dq_h if hd == 0 else jnp.where((ri >= hd * Dh) & (ri < (hd + 1) * Dh), dq_h, dq_t)
                    for c0 in range(0, tq, LANES):
                        dsum_h = dsum_h + ds[:, c0:c0 + LANES]
                    new.append((dk_h, dv_h, dsum_h))
                dqt_s[:, pl.ds(rq, tq)] += dq_t * scale
                return tuple(new)

            zero = tuple((jnp.zeros((t, LANES), F32),) * 3 for _ in range(hp))
            res = lax.fori_loop(i0 + 1, nq, functools.partial(tile, masked=False), tile(i0, zero, True))
            dk, dv, dcc = res[0][0], res[0][1], jnp.zeros((t, LANES), F32)
            for hd in range(hp):
                own = _head_lanes(hd, cfg, hp)[0]
                if hd > 0:
                    dk, dv = jnp.where(own, res[hd][0], dk), jnp.where(own, res[hd][1], dv)
                dcc = dcc + jnp.where(li == hd, -jnp.sum(res[hd][2], axis=1, keepdims=True), 0.0)
            dk_ref[pl.ds(rk, t), :] = dk.astype(BF16)
            dv_ref[pl.ds(rk, t), :] = dv.astype(BF16)
            dcc_ref[0, pl.ds(rk, t), :] = dcc
            return 0

        lax.fori_loop(0, nk, kv_step, 0)

        def finish(c, _):
            r = pl.multiple_of(c * t, t)
            dq_ref[pl.ds(r, t), :] = dqt_s[:, pl.ds(r, t)].T.astype(BF16)
            return 0

        lax.fori_loop(0, nk, finish, 0)

    blk = pl.BlockSpec((S, LANES), lambda b, h: (b, h))
    cc = pl.BlockSpec((1, S, LANES), lambda b, h: (h, b, 0))
    os_ = jax.ShapeDtypeStruct((cfg.T, cfg.AW), BF16)
    return _pcall(body, name=name, grid=(cfg.Bl, nb),
                  in_specs=[pl.BlockSpec((S, LANES), lambda b, h: (b, qb + h)), pl.BlockSpec((S, LANES), lambda b, h: (b, kb + h)),
                            pl.BlockSpec((S, LANES), lambda b, h: (b, vb + h)), cc, blk, blk,
                            pl.BlockSpec((1, 1, SUBLANES, S), lambda b, h: (b, h, 0, 0))],
                  out_specs=[blk, blk, blk, cc],
                  out_shape=[os_, os_, os_, jax.ShapeDtypeStruct((nb, cfg.T, LANES), F32)],
                  scratch_shapes=[pltpu.VMEM((hp, S, LANES), BF16)] * 3 + [pltpu.VMEM((LANES, S), BF16),
                                  pltpu.VMEM((SUBLANES, S), F32), pltpu.VMEM((LANES, S), F32)],
                  compiler_params=_params("parallel", "parallel"))(zm, zm, zm, cum_c, o, do, lse)


FFN_HALO = 8
FFN_CHUNK = 16


def _ffn_tiles(cfg):
    assert cfg.KF - 1 <= FFN_HALO
    return _pick(cfg.S, (512, 256, 128, 64, 32, 16, 8)), _pick(cfg.F, (256, 128))


def _gelu(x):
    return 0.5 * x * (1.0 + lax.erf(x * (2.0 ** -0.5)))


def _gelu_grad(x):
    return 0.5 * (1.0 + lax.erf(x * (2.0 ** -0.5))) + x * jnp.exp(-0.5 * x * x) * ((2.0 * math.pi) ** -0.5)


def _ffn_conv_fwd(h0, w, cb, cfg, *, name):
    KF, HALO = cfg.KF, FFN_HALO
    ts, tf = _ffn_tiles(cfg)
    tpb, nf = cfg.S // ts, cfg.F // tf
    lead = HALO - (KF - 1)

    CH = FFN_CHUNK

    def body(g_ref, gp_ref, l_ref, lp_ref, wg_ref, wl_ref, cg_ref, cl_ref, o_ref, hg_ref, hl_ref, g_s, l_s):
        first = pl.program_id(1) % tpb == 0
        for s, main, prev in ((g_s, g_ref, gp_ref), (l_s, l_ref, lp_ref)):
            s[0:HALO, :] = jnp.where(first, 0.0, prev[...])
            s[HALO:HALO + CH, :] = main[0:CH, :]
        wg, wl = [wg_ref[k:k + 1, :] for k in range(KF)], [wl_ref[k:k + 1, :] for k in range(KF)]
        for r0 in range(0, ts, CH):
            hg, hl = cg_ref[...], cl_ref[...]
            for k in range(KF):
                if r0 == 0:
                    xg, xl = g_s[lead + k:lead + k + CH, :], l_s[lead + k:lead + k + CH, :]
                else:
                    a = r0 - (KF - 1) + k
                    xg, xl = g_ref[a:a + CH, :], l_ref[a:a + CH, :]
                hg, hl = hg + wg[k] * xg, hl + wl[k] * xl
            o_ref[r0:r0 + CH, :] = (_gelu(hg) * hl).astype(BF16)
            hg_ref[r0:r0 + CH, :], hl_ref[r0:r0 + CH, :] = hg, hl

    hb = ts // HALO
    prev = lambda off: pl.BlockSpec((HALO, tf), lambda j, i: (jnp.maximum(i * hb - 1, 0), off + j))
    main = lambda off: pl.BlockSpec((ts, tf), lambda j, i: (i, off + j))
    wsp = lambda off: pl.BlockSpec((FFN_TAPS, tf), lambda j, i: (0, off + j))
    vsp = lambda off: pl.BlockSpec((1, tf), lambda j, i: (0, off + j))
    hs = jax.ShapeDtypeStruct((cfg.T, cfg.F), F32)
    return _pcall(body, name=name, grid=(nf, cfg.T // ts),
                  in_specs=[main(0), prev(0), main(nf), prev(nf), wsp(0), wsp(nf), vsp(0), vsp(nf)],
                  out_specs=[main(0)] * 3, out_shape=[jax.ShapeDtypeStruct((cfg.T, cfg.F), BF16), hs, hs],
                  scratch_shapes=[pltpu.VMEM((HALO + CH, tf), F32)] * 2,
                  compiler_params=_params("parallel", "parallel"))(h0, h0, h0, h0, w, w, cb, cb)


def _ffn_conv_bwd(df, h0, hg, hl, w, cfg, *, name):
    KF, HALO = cfg.KF, FFN_HALO
    ts, tf = _ffn_tiles(cfg)
    tpb, nf = cfg.S // ts, cfg.F // tf
    ext = ts + HALO

    CH = FFN_CHUNK

    def body(g_ref, l_ref, hg_ref, hgn_ref, hl_ref, hln_ref, d_ref, dn_ref, wg_ref, wl_ref,
             dg_ref, dl_ref, dwg_ref, dwl_ref, dcg_ref, dcl_ref, dhg_s, dhl_s):
        i = pl.program_id(1)
        last = i % tpb == tpb - 1

        @pl.when(i == 0)
        def _():
            dwg_ref[...] = jnp.zeros_like(dwg_ref)
            dwl_ref[...] = jnp.zeros_like(dwl_ref)
            dcg_ref[...] = jnp.zeros_like(dcg_ref)
            dcl_ref[...] = jnp.zeros_like(dcl_ref)

        wg, wl = [wg_ref[k:k + 1, :] for k in range(KF)], [wl_ref[k:k + 1, :] for k in range(KF)]

        def grads(hg, hl, d):
            return d * hl * _gelu_grad(hg), d * _gelu(hg)

        for r0 in range(0, ts, CH):
            dhg_s[r0:r0 + CH, :], dhl_s[r0:r0 + CH, :] = grads(hg_ref[r0:r0 + CH, :], hl_ref[r0:r0 + CH, :], d_ref[r0:r0 + CH, :])
        dhg_s[ts:ext, :], dhl_s[ts:ext, :] = grads(hgn_ref[...], hln_ref[...], jnp.where(last, 0.0, dn_ref[...]))

        for dh_s, x_ref, wk, dx_ref, dw_ref, dc_ref in ((dhg_s, g_ref, wg, dg_ref, dwg_ref, dcg_ref),
                                                        (dhl_s, l_ref, wl, dl_ref, dwl_ref, dcl_ref)):
            dw_acc = [jnp.zeros((CH, tf), F32) for _ in range(KF)]
            for r0 in range(0, ts, CH):
                x = x_ref[r0:r0 + CH, :]
                dx = jnp.zeros((CH, tf), F32)
                for k in range(KF):
                    dhk = dh_s[r0 + KF - 1 - k:r0 + KF - 1 - k + CH, :]
                    dx = dx + wk[k] * dhk
                    dw_acc[k] = dw_acc[k] + x * dhk
                    if k == KF - 1:
                        dc_acc = dhk if r0 == 0 else dc_acc + dhk
                dx_ref[r0:r0 + CH, :] = dx.astype(BF16)
            for k in range(KF):
                dw_ref[k:k + 1, :] += jnp.sum(dw_acc[k], axis=0, keepdims=True)
            dc_ref[...] += jnp.sum(dc_acc, axis=0, keepdims=True)

    hb = ts // HALO
    nhb = cfg.T // HALO
    main = lambda off: pl.BlockSpec((ts, tf), lambda j, i: (i, off + j))
    nxt = pl.BlockSpec((HALO, tf), lambda j, i: (jnp.minimum((i + 1) * hb, nhb - 1), j))
    wsp = lambda off: pl.BlockSpec((FFN_TAPS, tf), lambda j, i: (0, off + j))
    vsp = pl.BlockSpec((1, tf), lambda j, i: (0, j))
    dxs, dws, dcs = (jax.ShapeDtypeStruct((cfg.T, cfg.F), BF16), jax.ShapeDtypeStruct((FFN_TAPS, cfg.F), F32),
                     jax.ShapeDtypeStruct((1, cfg.F), F32))
    return _pcall(body, name=name, grid=(nf, cfg.T // ts),
                  in_specs=[main(0), main(nf), main(0), nxt, main(0), nxt, main(0), nxt, wsp(0), wsp(nf)],
                  out_specs=[main(0), main(0), wsp(0), wsp(0), vsp, vsp],
                  out_shape=[dxs, dxs, dws, dws, dcs, dcs],
                  scratch_shapes=[pltpu.VMEM((ext, tf), F32)] * 2,
                  compiler_params=_params("parallel", "arbitrary"))(h0, h0, hg, hg, hl, hl, df, df, w, w)


def _ada_fwd(c_all, w, b, *, name):
    L, D, n = w.shape
    B = c_all.shape[0]

    def body(c_ref, w_ref, b_ref, o_ref):
        c = c_ref[...]
        act = (c * _sigmoid(c)).astype(BF16)
        o_ref[0] = jnp.dot(act, w_ref[0].astype(BF16), preferred_element_type=F32) + b_ref[0]

    return _pcall(body, name=name, grid=(L,),
                  in_specs=[pl.BlockSpec((B, D), lambda l: (0, 0)), pl.BlockSpec((1, D, n), lambda l: (l, 0, 0)),
                            pl.BlockSpec((1, 1, n), lambda l: (l, 0, 0))],
                  out_specs=pl.BlockSpec((1, B, n), lambda l: (l, 0, 0)), out_shape=jax.ShapeDtypeStruct((L, B, n), F32),
                  compiler_params=_params("parallel"))(c_all, w, b)


def _ada_bwd(c_all, dmod, *, name):
    L, B, n = dmod.shape
    D = c_all.shape[1]

    def body(c_ref, d_ref, o_ref):
        c = c_ref[...]
        act = (c * _sigmoid(c)).astype(BF16)
        o_ref[0] = lax.dot_general(act, d_ref[0].astype(BF16), TN, preferred_element_type=F32)

    return _pcall(body, name=name, grid=(L,),
                  in_specs=[pl.BlockSpec((B, D), lambda l: (0, 0)), pl.BlockSpec((1, B, n), lambda l: (l, 0, 0))],
                  out_specs=pl.BlockSpec((1, D, n), lambda l: (l, 0, 0)), out_shape=jax.ShapeDtypeStruct((L, D, n), F32),
                  compiler_params=_params("parallel"))(c_all, dmod)


def _slot_sum(x, *, name):
    n, R, W = x.shape
    tr = _pick(R, (256, 128, 64, 32, 16, 8))

    def body(x_ref, o_ref):
        acc = x_ref[0].astype(F32)
        for k in range(1, n):
            acc = acc + x_ref[k].astype(F32)
        o_ref[...] = acc

    return _pcall(body, name=name, grid=(R // tr,), in_specs=[pl.BlockSpec((n, tr, W), lambda i: (0, i, 0))],
                  out_specs=pl.BlockSpec((tr, W), lambda i: (i, 0)), out_shape=jax.ShapeDtypeStruct((R, W), F32),
                  compiler_params=_params("parallel"))(x)


def _adamw_math(g, w, m, v):
    r1, r2 = 1.0 / (1.0 - ADAM_B1 ** ADAM_STEP), 1.0 / (1.0 - ADAM_B2 ** ADAM_STEP)
    m2 = ADAM_B1 * m + (1.0 - ADAM_B1) * g
    v2 = ADAM_B2 * v + (1.0 - ADAM_B2) * (g * g)
    return -ADAM_LR * ((m2 * r1) / (jnp.sqrt(v2 * r2) + ADAM_EPS) + ADAM_WD * w), m2, v2


def _adamw_many(gs, ws, ms, vs, *, name):
    n = len(gs)

    def body(*refs):
        ins, outs = refs[:4 * n], refs[4 * n:]
        for i in range(n):
            d, m2, v2 = _adamw_math(*(ins[j * n + i][...] for j in range(4)))
            outs[i][...], outs[n + i][...], outs[2 * n + i][...] = d, m2, v2

    vm = pl.BlockSpec(memory_space=pltpu.VMEM)
    outs = _pcall(body, name=name, in_specs=[vm] * (4 * n), out_specs=[vm] * (3 * n),
                  out_shape=[jax.ShapeDtypeStruct(a.shape, F32) for _ in range(3) for a in ws],
                  compiler_params=pltpu.CompilerParams(vmem_limit_bytes=VMEM_LIMIT))(*gs, *ws, *ms, *vs)
    return outs[:n], outs[n:2 * n], outs[2 * n:]


def _adamw(gs, w, m, v, *, name):
    n, R, W = gs.shape
    tr = _pick(R, (256, 128, 64, 32, 16, 8))

    def body(g_ref, w_ref, m_ref, v_ref, go_ref, d_ref, mo_ref, vo_ref):
        g = g_ref[0].astype(F32)
        for k in range(1, n):
            g = g + g_ref[k].astype(F32)
        go_ref[...] = g
        d_ref[...], mo_ref[...], vo_ref[...] = _adamw_math(g, w_ref[...], m_ref[...], v_ref[...])

    blk = pl.BlockSpec((tr, W), lambda i: (i, 0))
    o = jax.ShapeDtypeStruct((R, W), F32)
    return _pcall(body, name=name, grid=(R // tr,), in_specs=[pl.BlockSpec((n, tr, W), lambda i: (0, i, 0)), blk, blk, blk],
                  out_specs=[blk] * 4, out_shape=[o] * 4, compiler_params=_params("parallel"))(gs, w, m, v)


ALL_PEERS = tuple(range(1, N_DEV))


def _peer_copies(x_ref, land_ref, send_sems, recv_sems, all_to_all, ks=ALL_PEERS):
    mx, my, mc = lax.axis_index("x"), lax.axis_index("y"), lax.axis_index("c")
    me = 4 * mx + 2 * my + mc
    copies = []
    for n, k in enumerate(ks):
        px, py, pc = mx ^ ((k >> 2) & 1), my ^ ((k >> 1) & 1), mc ^ (k & 1)
        copies.append(pltpu.make_async_remote_copy(
            src_ref=x_ref.at[4 * px + 2 * py + pc] if all_to_all else x_ref, dst_ref=land_ref.at[me],
            send_sem=send_sems.at[n], recv_sem=recv_sems.at[n], device_id=(px, py, pc),
            device_id_type=pl.DeviceIdType.MESH))
    return copies


def _gather_two_level(x, *, name, after=None):
    def body(x_ref, *rest):
        o_ref, send_sems, recv_sems, local_sem = rest[-4:]
        mx, my, mc = lax.axis_index("x"), lax.axis_index("y"), lax.axis_index("c")
        me, sibling = (mx, my, mc), (mx, my, 1 - mc)
        chips = [(1 - mx, my), (mx, 1 - my), (1 - mx, 1 - my)]

        def slot(px, py, pc):
            return o_ref.at[4 * px + 2 * py + pc]

        def copy(k, block, to, src=None):
            return pltpu.make_async_remote_copy(
                src_ref=slot(*block) if src is None else src, dst_ref=slot(*block), send_sem=send_sems.at[k],
                recv_sem=recv_sems.at[k], device_id=to, device_id_type=pl.DeviceIdType.MESH)

        mine = pltpu.make_async_copy(x_ref, slot(*me), local_sem)
        mine.start()
        first = [copy(0, me, sibling, src=x_ref)] + [copy(1 + j, me, (*chip, mc), src=x_ref) for j, chip in enumerate(chips)]
        for cp in first:
            cp.start()
        passed = [copy(4 + j, (*chip, mc), sibling) for j, chip in enumerate(chips)]
        for j, chip in enumerate(chips):
            copy(1 + j, (*chip, mc), me).wait_recv()
            passed[j].start()
        copy(0, sibling, me).wait_recv()
        for j, chip in enumerate(chips):
            copy(4 + j, (*chip, 1 - mc), me).wait_recv()
        for cp in first + passed:
            cp.wait_send()
        mine.wait()

    anyspec = pl.BlockSpec(memory_space=pl.ANY)
    args = [x] if after is None else [x, after]
    return _pcall(body, name=name, in_specs=[anyspec] * len(args), out_specs=anyspec,
                  out_shape=jax.ShapeDtypeStruct((N_DEV,) + tuple(x.shape), x.dtype),
                  scratch_shapes=[pltpu.SemaphoreType.DMA((N_DEV - 1,)), pltpu.SemaphoreType.DMA((N_DEV - 1,)),
                                  pltpu.SemaphoreType.DMA(())])(*args)


_HBM = pl.BlockSpec(memory_space=pltpu.HBM)
_SEM = pl.BlockSpec(memory_space=pltpu.SEMAPHORE)
_EFFECT = pltpu.SideEffectType.DATAFLOW_SIDE_EFFECTING


def _local_copy(x_ref, land_ref, local_sem, all_to_all):
    me = 4 * lax.axis_index("x") + 2 * lax.axis_index("y") + lax.axis_index("c")
    return pltpu.make_async_copy(x_ref.at[me] if all_to_all else x_ref, land_ref.at[me], local_sem)


def _exchange_start(x, *, all_to_all, name, after=None):
    blk = x.shape[1:] if all_to_all else x.shape
    land = lax.empty((N_DEV,) + tuple(blk), x.dtype)
    has_after = after is not None

    def body(*refs):
        x_ref, land_ref = refs[0], refs[1]
        send_sems, recv_sems, local_sem, _, _, token = refs[2 + has_after:]
        _local_copy(x_ref, land_ref, local_sem, all_to_all).start()
        for cp in _peer_copies(x_ref, land_ref, send_sems, recv_sems, all_to_all):
            cp.start()
        token[...] = jnp.zeros_like(token)

    n_sem = pltpu.SemaphoreType.DMA((N_DEV - 1,))
    args = [pltpu.with_memory_space_constraint(x, pltpu.HBM), pltpu.with_memory_space_constraint(land, pltpu.HBM)]
    in_specs = [_HBM, _HBM]
    if has_after:
        args.append(after)
        in_specs.append(pl.BlockSpec(memory_space=pl.ANY))
    send_sems, recv_sems, local_sem, x_thru, land_thru, token = _pcall(
        body, name=name, in_specs=in_specs,
        out_shape=(n_sem, n_sem, pltpu.SemaphoreType.DMA(()), pltpu.HBM(x.shape, x.dtype), pltpu.HBM(land.shape, land.dtype),
                   jax.ShapeDtypeStruct((SUBLANES, LANES), F32)),
        out_specs=(_SEM, _SEM, _SEM, _HBM, _HBM, pl.BlockSpec(memory_space=pltpu.VMEM)), input_output_aliases={0: 3, 1: 4},
        compiler_params=pltpu.CompilerParams(has_side_effects=_EFFECT))(*args)
    return (send_sems, recv_sems, local_sem, x_thru, land_thru, all_to_all), token


def _exchange_wait(state, after, *, name):
    send_sems, recv_sems, local_sem, x_thru, land_thru, all_to_all = state

    def body(x_ref, land_ref, send_sems, recv_sems, local_sem, after_ref, x_dead, landed):
        _local_copy(x_ref, land_ref, local_sem, all_to_all).wait()
        for cp in _peer_copies(x_ref, land_ref, send_sems, recv_sems, all_to_all):
            cp.wait_send()
            cp.wait_recv()

    return _pcall(
        body, name=name, in_specs=(_HBM, _HBM, _SEM, _SEM, _SEM, pl.BlockSpec(memory_space=pl.ANY)),
        out_shape=(pltpu.HBM(x_thru.shape, x_thru.dtype), pltpu.HBM(land_thru.shape, land_thru.dtype)),
        out_specs=(_HBM, _HBM), input_output_aliases={0: 0, 1: 1},
        compiler_params=pltpu.CompilerParams(has_side_effects=_EFFECT))(
            x_thru, land_thru, send_sems, recv_sems, local_sem, after)[1]


def _exchange(x, *, all_to_all, name):
    blk = x.shape[1:] if all_to_all else x.shape

    def body(x_ref, o_ref, send_sems, recv_sems, local_sem):
        me = 4 * lax.axis_index("x") + 2 * lax.axis_index("y") + lax.axis_index("c")
        mine = pltpu.make_async_copy(x_ref.at[me] if all_to_all else x_ref, o_ref.at[me], local_sem)
        mine.start()
        copies = _peer_copies(x_ref, o_ref, send_sems, recv_sems, all_to_all)
        for cp in copies:
            cp.start()
        for cp in copies:
            cp.wait()
        mine.wait()

    anyspec = pl.BlockSpec(memory_space=pl.ANY)
    return _pcall(body, name=name, in_specs=[anyspec], out_specs=anyspec,
                  out_shape=jax.ShapeDtypeStruct((N_DEV,) + tuple(blk), x.dtype),
                  scratch_shapes=[pltpu.SemaphoreType.DMA((N_DEV - 1,)), pltpu.SemaphoreType.DMA((N_DEV - 1,)),
                                  pltpu.SemaphoreType.DMA(())])(x)


PACK_ROWS = 16


def _pack(arrs, width, dtype, lead=0):
    parts, segs, r = [], [], 0
    for a in arrs:
        lshape, shape = a.shape[:lead], a.shape[lead:]
        n = math.prod(shape)
        rows = -(-n // width)
        rows_p = -(-rows // PACK_ROWS) * PACK_ROWS
        if n == rows * width:
            blk = a.reshape(lshape + (rows, width)).astype(dtype)
            parts.append(jnp.pad(blk, [(0, 0)] * lead + [(0, rows_p - rows), (0, 0)]) if rows_p > rows else blk)
        else:
            flat = jnp.pad(a.reshape(lshape + (n,)).astype(dtype), [(0, 0)] * lead + [(0, rows_p * width - n)])
            parts.append(flat.reshape(lshape + (rows_p, width)))
        segs.append((r, n, shape))
        r += rows_p
    return jnp.concatenate(parts, axis=lead), segs


def _unpack(p, segs):
    lshape, width = p.shape[:-2], p.shape[-1]
    outs = []
    for r, n, shape in segs:
        rows = -(-n // width)
        blk = p[..., r:r + rows, :]
        if n != rows * width:
            blk = blk.reshape(lshape + (rows * width,))[..., :n]
        outs.append(blk.reshape(lshape + shape))
    return outs


def _split_cols(a, f_off, h):
    return jnp.concatenate([a[..., :f_off], a[..., f_off + h:]], axis=-1), a[..., f_off:f_off + h]


def _merge_cols(main, f, f_off):
    return jnp.concatenate([main[..., :f_off], f, main[..., f_off:]], axis=-1)


def _pad_to(a, n, axis):
    pad = [(0, 0)] * a.ndim
    pad[axis] = (0, n - a.shape[axis])
    return jnp.pad(a, pad)


def kernel(x, c, w_ada, b_ada, w_in, b_in, conv_a_w, conv_a_b, ln_conv_g, ln_conv_b, w_conv_proj, w_attn_proj, w_mix_out, b_mix_out, ln1_g, ln1_b, w_ffn_up, ffn_conv_w, ffn_conv_b, w_ffn_down, ln2_g, ln2_b, loss_target, m_w_ada, m_b_ada, m_w_in, m_b_in, m_conv_a_w, m_conv_a_b, m_ln_conv_g, m_ln_conv_b, m_w_conv_proj, m_w_attn_proj, m_w_mix_out, m_b_mix_out, m_ln1_g, m_ln1_b, m_w_ffn_up, m_ffn_conv_w, m_ffn_conv_b, m_w_ffn_down, m_ln2_g, m_ln2_b, v_w_ada, v_b_ada, v_w_in, v_b_in, v_conv_a_w, v_conv_a_b, v_ln_conv_g, v_ln_conv_b, v_w_conv_proj, v_w_attn_proj, v_w_mix_out, v_b_mix_out, v_ln1_g, v_ln1_b, v_w_ffn_up, v_ffn_conv_w, v_ffn_conv_b, v_w_ffn_down, v_ln2_g, v_ln2_b):
    L, D = w_ada.shape[0], w_ada.shape[1]
    Bl, S, _ = x.shape
    C, KW, AW = conv_a_b.shape[1], conv_a_w.shape[1], w_attn_proj.shape[1]
    F, KF, n_in_all = ffn_conv_b.shape[1] // 2, ffn_conv_w.shape[1], b_in.shape[1]
    H = n_in_all - 2 * C - 3 * AW - 2 * D
    cfg = Cfg(L=L, Bl=Bl, S=S, D=D, C=C, KW=KW, H=H, Dh=AW // H, F=F, KF=KF)
    T, NM = cfg.T, cfg.NM
    f_off = 2 * C + 3 * AW
    n_ada = w_ada.shape[2]
    me = 4 * lax.axis_index("x") + 2 * lax.axis_index("y") + lax.axis_index("c")

    def my_cols(a, n):
        return lax.dynamic_slice_in_dim(a, me * n, n, axis=a.ndim - 1)

    spack, ssegs = _pack([c, conv_a_w, ffn_conv_w], D, F32)
    c_g, caw_g, fcw_g = _unpack(_exchange(spack, all_to_all=False, name="gather_small"), ssegs)
    c_all = c_g.reshape(N_DEV * Bl, D)
    caw = _pad_to(jnp.moveaxis(caw_g, 0, 2).reshape(L, KW, C), CONV_A_TAPS, 1)
    fcw = _pad_to(jnp.moveaxis(fcw_g, 0, 2).reshape(L, KF, 2 * F), FFN_TAPS, 1)

    mod_part = _ada_fwd(c_all, w_ada, my_cols(b_ada, n_ada)[:, None, :], name="ada_fwd")
    mod_send = jnp.moveaxis(mod_part.reshape(L, N_DEV, Bl, n_ada), 1, 0).reshape(N_DEV, L * Bl, n_ada)
    mod_recv = _exchange(mod_send, all_to_all=True, name="exchange_mod")
    mod = jnp.moveaxis(mod_recv.reshape(N_DEV, L, Bl, n_ada), 0, 2).reshape(L, Bl, 6, 1, D)
    shift1, scale1, gate1, shift2, scale2, gate2 = (mod[:, :, i] for i in range(6))

    big_names = ["w_in", "w_conv_proj", "w_attn_proj", "w_mix_out", "w_ffn_up", "w_ffn_down"]
    transposed = (True, True, True, False, True, False)

    def shard_items(arrs, grp):
        return [arrs[i][l].T if transposed[i] else arrs[i][l] for l, i in grp]

    W = [dict() for _ in range(L)]

    def set_weights(landed, segs, grp):
        for (l, i), a in zip(grp, _unpack(landed, segs)):
            a = a.reshape((-1, a.shape[-1]))
            if i == 0:
                wm_t, wf_t = _split_cols(a.T, f_off, H)
                bm, bf = _split_cols(b_in[l], f_off, H)
                W[l].update(wm_t=wm_t.T, wf_t=_pad_to(wf_t.T, LANES, 0), bm=bm[None], bf=_pad_to(bf, LANES, 0)[None])
            else:
                W[l][("w_cp_t", "w_ap_t", "w_mo", "w_up_t", "w_dn")[i - 1]] = a

    big_w = (w_in, w_conv_proj, w_attn_proj, w_mix_out, w_ffn_up, w_ffn_down)
    w_groups = [[(0, 0)], [(0, i) for i in range(1, 6)]] + [[(l, i) for i in range(6)] for l in range(1, L)]
    pack, segs = _pack(shard_items(big_w, w_groups[0]), D, BF16)
    landed0 = _gather_two_level(pack, name="gather_weights_0", after=mod_recv)
    set_weights(landed0, segs, w_groups[0])
    w_state, token = {}, landed0
    for gi in range(1, len(w_groups)):
        pack, segs = _pack(shard_items(big_w, w_groups[gi]), D, BF16)
        state, token = _exchange_start(pack, all_to_all=False, name=f"gather_weights_start_{gi}", after=token)
        w_state[gi] = [state, pack, segs]

    def wait_weights(gi, after):
        state, _, segs = w_state[gi]
        set_weights(_exchange_wait(state, after, name=f"gather_weights_wait_{gi}"), segs, w_groups[gi])

    xf = x.reshape(T, D)
    u = _ln_mod_fwd(xf, shift1[0], scale1[0], cfg, name="ln_mod_fwd")
    saved = []
    xin = xf
    for l in range(L):
        w = W[l]
        if l > 0:
            wait_weights(l + 1, u)
        zm = _matmul(u, w["wm_t"], mode="nt", bias=w["bm"], name=f"in_proj_{l}", after=token if l == 0 else None)
        zf = _matmul(u, w["wf_t"], mode="nt", bias=w["bf"], name=f"in_proj_f_{l}")
        a3 = _conv_a_fwd(zm, caw[l], conv_a_b[l][None], ln_conv_g[l][None], ln_conv_b[l][None], cfg, name=f"conv_a_fwd_{l}")
        cum_c = _fgate_fwd(zf, cfg, name=f"fgate_fwd_{l}")
        o, o32, lse = _attn_fwd(zm, cum_c, cfg, name=f"attn_fwd_{l}")
        if l == 0:
            wait_weights(1, o)
        ya =_matmul(a3, w["w_cp_t"], mode="nt", name=f"conv_proj_{l}")
        yb = _matmul(o, w["w_ap_t"], mode="nt", name=f"attn_proj_{l}")
        mg = _merge_fwd(zm, ya, yb, cfg, name=f"merge_fwd_{l}")
        mix = _matmul(mg, w["w_mo"], mode="nn", bias=b_mix_out[l][None], name=f"mix_out_{l}")
        x1, u2 = _res_ln_fwd(xin, mix, gate1[l], ln1_g[l][None], ln1_b[l][None], cfg, name=f"res_ln1_fwd_{l}",
                             nxt=(shift2[l], scale2[l]))
        h0 = _matmul(u2, w["w_up_t"], mode="nt", name=f"ffn_up_{l}")
        fa, hg, hl = _ffn_conv_fwd(h0, fcw[l], ffn_conv_b[l][None], cfg, name=f"ffn_conv_fwd_{l}")
        ffn = _matmul(fa, w["w_dn"], mode="nn", name=f"ffn_down_{l}")
        saved.append(dict(x=xin, u=u, zm=zm, zf=zf, a3=a3, cum_c=cum_c, o=o, o32=o32, lse=lse, ya=ya, yb=yb, mg=mg, mix=mix,
                          x1=x1, u2=u2, h0=h0, hg=hg, hl=hl, fa=fa, ffn=ffn))
        if l + 1 < L:
            xin, u = _res_ln_fwd(x1, ffn, gate2[l], ln2_g[l][None], ln2_b[l][None], cfg, name=f"res_ln2_fwd_{l}",
                                 nxt=(shift1[l + 1], scale1[l + 1]))
        else:
            xin = _res_ln_fwd(x1, ffn, gate2[l], ln2_g[l][None], ln2_b[l][None], cfg, name=f"res_ln2_fwd_{l}")

    dx, loss_tiles = _loss_grad(xin, loss_target.reshape(T, D), cfg, name="loss_grad")
    loss = lax.psum(0.5 / D * jnp.sum(loss_tiles[:, 0, 0]), ("x", "y", "c"))

    gbig = {}
    g_groups = [[(l, i) for i in range(6)] for l in reversed(range(1, L))] + [[(0, 4), (0, 5)], [(0, 1), (0, 2), (0, 3)], [(0, 0)]]
    g_state = []

    def start_grads(after=None):
        grp = g_groups[len(g_state)]
        send, segs = _pack([gbig[k].reshape((N_DEV, -1, gbig[k].shape[1])) for k in grp], D, BF16, lead=1)
        state, tok = _exchange_start(send, all_to_all=True, name=f"exchange_grads_start_{len(g_state)}", after=after)
        g_state.append((state, send, segs, grp))
        return tok

    gsm = [dict() for _ in range(L)]
    dmods = [None] * L
    token = None
    for l in reversed(range(L)):
        w, s = W[l], saved[l]
        if l == L - 1:
            top = _res_ln_bwd(dx, s["x1"], s["ffn"], gate2[l], ln2_g[l][None], cfg, name=f"res_ln2_bwd_{l}")
        dres2, dffn, dg2, db2, dgate2 = top[:5]
        dfa = _matmul(dffn, w["w_dn"], mode="nt", name=f"d_ffn_act_{l}", after=token)
        gbig[l, 5] = _matmul(s["fa"], dffn, mode="tn", name=f"dw_ffn_down_{l}")
        dh0g, dh0l, dwg, dwl, dcg, dcl = _ffn_conv_bwd(dfa, s["h0"], s["hg"], s["hl"], fcw[l], cfg, name=f"ffn_conv_bwd_{l}")
        du2 = _matmul((dh0g, dh0l), w["w_up_t"], mode="nn", name=f"d_u2_{l}")
        gbig[l, 4] = _matmul((dh0g, dh0l), s["u2"], mode="tn", name=f"dw_ffn_up_{l}")
        token = start_grads() if l == 0 else None
        dres1, dmix, dg1, db1, dgate1, dbmo, dscale2, dshift2 = _res_ln_bwd(
            None, s["x"], s["mix"], gate1[l], ln1_g[l][None], cfg, name=f"res_ln1_bwd_{l}", mod=(du2, s["x1"], scale2[l], dres2))
        dmg = _matmul(dmix, w["w_mo"], mode="nt", name=f"d_merge_{l}", after=token)
        gbig[l, 3] = _matmul(s["mg"], dmix, mode="tn", name=f"dw_mix_out_{l}")
        dya, dyb, dzga, dzgb = _merge_bwd(dmg, s["zm"], s["ya"], s["yb"], cfg, name=f"merge_bwd_{l}")
        gbig[l, 1] = _matmul(dya, s["a3"], mode="tn", name=f"dw_conv_proj_{l}")
        da3 = _matmul(dya, w["w_cp_t"], mode="nn", name=f"d_a3_{l}")
        gbig[l, 2] = _matmul(dyb, s["o"], mode="tn", name=f"dw_attn_proj_{l}")
        token = start_grads() if l == 0 else None
        do = _matmul(dyb, w["w_ap_t"], mode="nn", out_dtype=BF16, name=f"d_o_{l}", after=token)
        dq, dk, dv, dcum_c = _attn_bwd(s["zm"], s["cum_c"], s["o32"], do, s["lse"], cfg, name=f"attn_bwd_{l}")
        dzf = _fgate_bwd(dcum_c, s["zf"], cfg, name=f"fgate_bwd_{l}")
        dzglu, dcaw, dcab, dlcg, dlcb = _conv_a_bwd(da3, s["zm"], caw[l], conv_a_b[l][None], ln_conv_g[l][None],
                                                    ln_conv_b[l][None], cfg, name=f"conv_a_bwd_{l}")
        dzm = jnp.concatenate([dzglu, dq, dk, dv, dzga, dzgb], axis=1)
        du1 = _matmul(dzf, w["wf_t"], mode="nn", name=f"d_u1_f_{l}")
        du1 = _matmul(dzm, w["wm_t"], mode="nn", add=du1, name=f"d_u1_{l}")
        dwm_t = _matmul(dzm, s["u"], mode="tn", name=f"dw_in_{l}")
        dwf_t = _matmul(dzf, s["u"], mode="tn", name=f"dw_in_f_{l}")
        gbig[l, 0] = _merge_cols(dwm_t.T, dwf_t[:H].T, f_off).T
        token = start_grads()
        dbm, dbf = _colsum(dzm, name=f"db_in_{l}"), _colsum(dzf, name=f"db_in_f_{l}")
        if l > 0:
            below = saved[l - 1]
            top = _res_ln_bwd(None, below["x1"], below["ffn"], gate2[l - 1], ln2_g[l - 1][None], cfg,
                              name=f"res_ln2_bwd_{l - 1}", mod=(du1, s["x"], scale1[l], dres1))
            dscale1, dshift1 = top[6], top[7]
        else:
            dx, dscale1, dshift1 = _ln_mod_bwd(du1, s["x"], scale1[l], dres1, cfg, name=f"ln_mod1_bwd_{l}")
        dmods[l] = jnp.concatenate([dshift1, dscale1, dgate1, dshift2, dscale2, dgate2], axis=1).reshape(Bl, 6 * D)
        gsm[l] = dict(b_in=_merge_cols(dbm[0], dbf[0, :H], f_off), conv_a_b=dcab[0], ln_conv_g=dlcg[0], ln_conv_b=dlcb[0],
                      b_mix_out=dbmo[0], ln1_g=dg1[0], ln1_b=db1[0], ffn_conv_b=jnp.concatenate([dcg[0], dcl[0]]),
                      ln2_g=dg2[0], ln2_b=db2[0], conv_a_w=dcaw[:KW], ffn_conv_w=jnp.concatenate([dwg[:KF], dwl[:KF]], axis=1))
    grad_x = dx.reshape(Bl, S, D)

    small_names = ["b_in", "conv_a_b", "ln_conv_g", "ln_conv_b", "b_mix_out", "ln1_g", "ln1_b", "ffn_conv_b", "ln2_g", "ln2_b",
                   "conv_a_w", "ffn_conv_w"]
    gs_list = [jnp.stack(dmods)] + [jnp.stack([gsm[l][n] for l in range(L)]) for n in small_names]
    gspack, gssegs = _pack(gs_list, D, F32)
    gs_all = _gather_two_level(gspack, name="gather_small_grads", after=token)
    dmod_all = jnp.moveaxis(_unpack(gs_all, gssegs)[0], 0, 1).reshape(L, N_DEV * Bl, 6 * D)
    g_small = dict(zip(small_names, _unpack(_slot_sum(gs_all, name="sum_small_grads"), gssegs)[1:]))
    g_small["conv_a_w"] = my_cols(g_small["conv_a_w"], C // N_DEV)
    g_small["ffn_conv_w"] = my_cols(g_small["ffn_conv_w"], 2 * F // N_DEV)
    g_small["w_ada"] = _ada_bwd(c_all, my_cols(dmod_all, n_ada), name="ada_bwd")
    g_small["b_ada"] = jnp.stack([_colsum(dmod_all[l], name=f"db_ada_{l}")[0] for l in range(L)])

    given = dict(w_in=(w_in, m_w_in, v_w_in), w_conv_proj=(w_conv_proj, m_w_conv_proj, v_w_conv_proj),
                 w_attn_proj=(w_attn_proj, m_w_attn_proj, v_w_attn_proj), w_mix_out=(w_mix_out, m_w_mix_out, v_w_mix_out),
                 w_ffn_up=(w_ffn_up, m_w_ffn_up, v_w_ffn_up), w_ffn_down=(w_ffn_down, m_w_ffn_down, v_w_ffn_down),
                 w_ada=(w_ada, m_w_ada, v_w_ada), b_ada=(b_ada, m_b_ada, v_b_ada), b_in=(b_in, m_b_in, v_b_in),
                 conv_a_w=(conv_a_w, m_conv_a_w, v_conv_a_w), conv_a_b=(conv_a_b, m_conv_a_b, v_conv_a_b),
                 ln_conv_g=(ln_conv_g, m_ln_conv_g, v_ln_conv_g), ln_conv_b=(ln_conv_b, m_ln_conv_b, v_ln_conv_b),
                 b_mix_out=(b_mix_out, m_b_mix_out, v_b_mix_out), ln1_g=(ln1_g, m_ln1_g, v_ln1_g), ln1_b=(ln1_b, m_ln1_b, v_ln1_b),
                 ffn_conv_w=(ffn_conv_w, m_ffn_conv_w, v_ffn_conv_w), ffn_conv_b=(ffn_conv_b, m_ffn_conv_b, v_ffn_conv_b),
                 ln2_g=(ln2_g, m_ln2_g, v_ln2_g), ln2_b=(ln2_b, m_ln2_b, v_ln2_b))
    res, kinds = {}, ("grad", "delta", "new_m", "new_v")
    loc_names = ["b_ada"] + small_names
    deltas, new_ms, new_vs = _adamw_many([g_small[n] for n in loc_names], *([given[n][j] for n in loc_names] for j in range(3)),
                                         name="adamw_small")
    for n, d, m2, v2 in zip(loc_names, deltas, new_ms, new_vs):
        res["grad", n], res["delta", n], res["new_m", n], res["new_v", n] = g_small[n], d, m2, v2
    rows_ada = (L * D * n_ada // D, D)
    outs = _adamw(g_small["w_ada"].reshape((1,) + rows_ada), *(a.reshape(rows_ada) for a in given["w_ada"]), name="adamw_w_ada")
    for kind, a in zip(kinds, outs):
        res[kind, "w_ada"] = a.reshape(w_ada.shape)

    big_parts = {}
    after = outs[0]
    for gi, (state, send, segs, grp) in enumerate(g_state):
        landed = _exchange_wait(state, after, name=f"exchange_grads_wait_{gi}")
        wmv = [_pack(shard_items([given[n][j] for n in big_names], grp), D, F32)[0] for j in range(3)]
        outs = _adamw(landed, *wmv, name=f"adamw_big_{gi}")
        for kind, packed in zip(kinds, outs):
            for (l, i), a in zip(grp, _unpack(packed, segs)):
                big_parts[kind, l, i] = a.T if transposed[i] else a
        after = outs[0]
    for kind in kinds:
        for i, n in enumerate(big_names):
            res[kind, n] = jnp.stack([big_parts[kind, l, i] for l in range(L)])

    order = ["w_ada", "b_ada", "w_in", "b_in", "conv_a_w", "conv_a_b", "ln_conv_g", "ln_conv_b", "w_conv_proj", "w_attn_proj",
             "w_mix_out", "b_mix_out", "ln1_g", "ln1_b", "w_ffn_up", "ffn_conv_w", "ffn_conv_b", "w_ffn_down", "ln2_g", "ln2_b"]
    return (loss, grad_x, *[res[k, n] for k in ("grad", "delta", "new_m", "new_v") for n in order])
```
